```python
import jax
import jax.numpy as jnp
from jax import lax
import numpy as np

D_MODEL = 1024
BATCH = 4
SEQ = 4096
DEPTH = 1

ATTN_GROUPS = 3
ATTN_HEADS_PER_GROUP = 4
ATTN_HEADS = ATTN_GROUPS * ATTN_HEADS_PER_GROUP
ATTN_HEAD_DIM = 128
ATTN_WIDTH = ATTN_HEADS * ATTN_HEAD_DIM
ATTN_OUT_WIDTH = ATTN_HEADS_PER_GROUP * ATTN_HEAD_DIM
ATTN_PATTERNS = ((128, 1), (512, 4), (2048, 16))
ALIBI_MAX_EXP = 8.0
NEG_INF = -1e30
RET_QK_DIM = 256
RET_HEADS = D_MODEL // RET_QK_DIM
RET_V_DIM = 2 * RET_QK_DIM
RET_QK_WIDTH = RET_HEADS * RET_QK_DIM
RET_V_WIDTH = RET_HEADS * RET_V_DIM
RET_CHUNK = 128
IN_SIZES = (ATTN_WIDTH, ATTN_WIDTH, ATTN_WIDTH, RET_QK_WIDTH, RET_QK_WIDTH,
            RET_V_WIDTH, RET_V_WIDTH, D_MODEL, D_MODEL)
N_GROUPS = 4
EXPERTS_PER_GROUP = 8
N_EXPERTS = N_GROUPS * EXPERTS_PER_GROUP
TOP_K_FINE = 2
EXPERT_FF = D_MODEL // 2
MOE_BLOCK = 128
DEEPNORM_ALPHA = (2.0 * DEPTH) ** 0.25
DEEPNORM_BETA = (8.0 * DEPTH) ** -0.25
LN_EPS = 1e-5
ADA_STD = 0.5

kernel_name = 'hybrid_dilated_retention_hmoe_block'


def _layer_norm(x):
    xf = x.astype(jnp.float32)
    mu = jnp.mean(xf, axis=-1, keepdims=True)
    var = jnp.mean(jnp.square(xf - mu), axis=-1, keepdims=True)
    return ((xf - mu) * lax.rsqrt(var + LN_EPS)).astype(x.dtype)


def _layer_norm_affine(x, gain, bias):
    return _layer_norm(x) * gain + bias


def _split_columns(proj, sizes):
    out, start = [], 0
    for n in sizes:
        out.append(proj[..., start:start + n])
        start += n
    return out


def _alibi_slopes(n_heads):
    return jnp.exp2(-ALIBI_MAX_EXP * jnp.arange(1, n_heads + 1, dtype=jnp.float32) / n_heads)


def _dilated_window_attention(q, k, v, slopes, window, dilation):
    b, s, h, dh = q.shape
    half = window // (2 * dilation)
    blk = half
    n_sub = s // dilation
    nb = -(-n_sub // blk)
    n_pad = nb * blk

    def to_sub(t):
        t = t.reshape(b, n_sub, dilation, h, dh).transpose(0, 2, 3, 1, 4)
        return jnp.pad(t, ((0, 0), (0, 0), (0, 0), (0, n_pad - n_sub), (0, 0)))

    def to_band(t):
        tp = jnp.pad(to_sub(t), ((0, 0), (0, 0), (0, 0), (blk, blk), (0, 0)))
        tp = tp.reshape(b, dilation, h, nb + 2, blk, dh)
        return jnp.concatenate([tp[:, :, :, :nb], tp[:, :, :, 1:nb + 1], tp[:, :, :, 2:]], axis=4)

    qb = to_sub(q).reshape(b, dilation, h, nb, blk, dh)
    kb, vb = to_band(k), to_band(v)
    q_idx = jnp.arange(nb)[:, None] * blk + jnp.arange(blk)[None, :]
    k_idx = (jnp.arange(nb)[:, None] - 1) * blk + jnp.arange(3 * blk)[None, :]
    delta = k_idx[:, None, :] - q_idx[:, :, None]
    valid = (jnp.abs(delta) <= half) & (k_idx[:, None, :] >= 0) & (k_idx[:, None, :] < n_sub)
    dist = (dilation * jnp.abs(delta)).astype(jnp.float32)
    bias = -slopes[:, None, None, None] * dist[None]
    scores = jnp.einsum('brhnqd,brhnkd->brhnqk', qb, kb).astype(jnp.float32) * (dh ** -0.5)
    scores = jnp.where(valid, scores + bias, NEG_INF)
    m = jnp.max(scores, axis=-1, keepdims=True)
    lse = m + jnp.log(jnp.sum(jnp.exp(scores - m), axis=-1, keepdims=True))
    probs = jnp.exp(scores - lse)
    out = jnp.einsum('brhnqk,brhnkd->brhnqd', probs.astype(v.dtype), vb)
    out = out.reshape(b, dilation, h, n_pad, dh)[:, :, :, :n_sub]
    out = out.transpose(0, 3, 1, 2, 4).reshape(b, s, h, dh)
    lse = lse[..., 0].reshape(b, dilation, h, n_pad)[..., :n_sub]
    lse = lse.transpose(0, 3, 1, 2).reshape(b, s, h)
    return out, lse


def _retention_one_direction(q, k, v, log_gamma, include_diag):
    b, h, s, dk = q.shape
    dv = v.shape[-1]
    c = RET_CHUNK
    nc = s // c

    def chunks(t):
        return t.reshape(b, h, nc, c, t.shape[-1]).transpose(2, 0, 1, 3, 4)

    pos = jnp.arange(c, dtype=jnp.float32)
    delta = pos[:, None] - pos[None, :]
    lower = delta >= 0 if include_diag else delta > 0
    lg = log_gamma[:, None, None]
    decay_mat = jnp.where(lower[None], jnp.exp(lg * jnp.maximum(delta, 0.0)[None]), 0.0)
    key_decay = jnp.exp(log_gamma[:, None] * (c - 1 - pos)[None])[None, :, :, None]
    query_decay = jnp.exp(log_gamma[:, None] * (pos + 1)[None])[None, :, :, None]
    chunk_decay = jnp.exp(log_gamma * c)[None, :, None, None]

    def step(state, inp):
        qi, ki, vi = inp
        inner = jnp.einsum('bhnd,bhmd->bhnm', qi, ki) * decay_mat
        y = jnp.einsum('bhnm,bhme->bhne', inner, vi)
        y = y + jnp.einsum('bhnd,bhde->bhne', qi, state) * query_decay
        state = state * chunk_decay + jnp.einsum('bhmd,bhme->bhde', ki * key_decay, vi)
        return state, y

    state0 = jnp.zeros((b, h, dk, dv), jnp.float32)
    _, ys = lax.scan(step, state0, (chunks(q), chunks(k), chunks(v)))
    return ys.transpose(1, 2, 0, 3, 4).reshape(b, h, s, dv)


def _bidirectional_retention(q, k, v, logit_fwd, logit_bwd):
    q, k, v = q.astype(jnp.float32), k.astype(jnp.float32), v.astype(jnp.float32)
    lg_f = jax.nn.log_sigmoid(logit_fwd.astype(jnp.float32))
    lg_b = jax.nn.log_sigmoid(logit_bwd.astype(jnp.float32))
    fwd = _retention_one_direction(q, k, v, lg_f, True)
    flip = lambda t: jnp.flip(t, axis=2)
    bwd = flip(_retention_one_direction(flip(q), flip(k), flip(v), lg_b, False))
    return fwd + bwd


def _token_mixing(u, w_in, w_attn_out, ret_decay_fwd, ret_decay_bwd, ret_gn_gain, w_ret_out, w_out):
    b, s, _ = u.shape
    proj = u @ w_in
    aq, ak, av, rq, rk, rv, rg, gate_a, gate_r = _split_columns(proj, IN_SIZES)

    heads = lambda t: t.reshape(b, s, ATTN_HEADS, ATTN_HEAD_DIM)
    aq, ak, av = heads(aq), heads(ak), heads(av)
    slopes = _alibi_slopes(ATTN_HEADS)
    outs, lses = [], []
    for g, (window, dilation) in enumerate(ATTN_PATTERNS):
        sl = slice(g * ATTN_HEADS_PER_GROUP, (g + 1) * ATTN_HEADS_PER_GROUP)
        o, l = _dilated_window_attention(aq[:, :, sl], ak[:, :, sl], av[:, :, sl], slopes[sl], window, dilation)
        outs.append(o)
        lses.append(l)
    dil_w = jax.nn.softmax(jnp.stack(lses), axis=0)
    attn = jnp.einsum('gbsh,gbshd->bshd', dil_w.astype(u.dtype), jnp.stack(outs))
    branch_a = attn.reshape(b, s, ATTN_OUT_WIDTH) @ w_attn_out

    rheads = lambda t, d: t.reshape(b, s, RET_HEADS, d).transpose(0, 2, 1, 3)
    ret = _bidirectional_retention(rheads(rq, RET_QK_DIM), rheads(rk, RET_QK_DIM) * (RET_QK_DIM ** -0.5),
                                   rheads(rv, RET_V_DIM), ret_decay_fwd, ret_decay_bwd)
    ret = _layer_norm(ret).transpose(0, 2, 1, 3).reshape(b, s, RET_V_WIDTH).astype(u.dtype) * ret_gn_gain
    branch_r = (jax.nn.silu(rg) * ret) @ w_ret_out

    merged = jax.nn.sigmoid(gate_a) * branch_a + jax.nn.sigmoid(gate_r) * branch_r
    return merged @ w_out


def _hierarchical_moe(u, w_coarse, b_coarse, w_fine, b_fine, w1, w3, w2):
    b, s, d = u.shape
    t = b * s
    xt = u.reshape(t, d)
    tok = jnp.arange(t)
    coarse = (xt @ w_coarse).astype(jnp.float32) + b_coarse
    p_coarse = jax.nn.softmax(coarse, axis=-1)
    _, g_top = lax.top_k(coarse, 1)
    g_sel = g_top[:, 0]
    p_group = p_coarse[tok, g_sel]
    fine_all = jnp.einsum('td,gde->tge', xt, w_fine).astype(jnp.float32) + b_fine
    fine = fine_all[tok, g_sel]
    f_val, f_idx = lax.top_k(fine, TOP_K_FINE)
    gate = p_group[:, None] * jax.nn.softmax(f_val, axis=-1)
    expert = g_sel[:, None] * EXPERTS_PER_GROUP + f_idx
    n_assign = t * TOP_K_FINE
    e_flat = expert.reshape(n_assign)
    order = jnp.argsort(e_flat)
    e_sorted = e_flat[order]
    tok_sorted = jnp.repeat(tok, TOP_K_FINE)[order]
    gate_sorted = gate.reshape(n_assign)[order]
    counts = jnp.bincount(e_flat, length=N_EXPERTS)
    padded = (counts + MOE_BLOCK - 1) // MOE_BLOCK * MOE_BLOCK
    start = jnp.cumsum(counts) - counts
    end_p = jnp.cumsum(padded)
    start_p = end_p - padded
    dest = start_p[e_sorted] + jnp.arange(n_assign) - start[e_sorted]
    n_rows = (-(-n_assign // MOE_BLOCK) + N_EXPERTS) * MOE_BLOCK
    n_blocks = n_rows // MOE_BLOCK
    rows = jnp.zeros((n_rows, d), u.dtype).at[dest].set(xt[tok_sorted])
    block_expert = jnp.minimum(
        jnp.searchsorted(end_p, jnp.arange(n_blocks) * MOE_BLOCK, side='right'), N_EXPERTS - 1)

    def expert_block(args):
        xb, e = args
        hdn = jax.nn.silu(xb @ w1[e]) * (xb @ w3[e])
        return hdn @ w2[e]

    y_rows = lax.map(expert_block, (rows.reshape(n_blocks, MOE_BLOCK, d), block_expert)).reshape(n_rows, d)
    y_assign = y_rows[dest] * gate_sorted[:, None].astype(u.dtype)
    out = jnp.zeros((t, d), u.dtype).at[tok_sorted].add(y_assign)
    return out.reshape(b, s, d)


def setup_inputs(seed: int = 0) -> dict:
    key = jax.random.key(seed)
    ks = jax.random.split(key, 32)
    f32 = jnp.float32
    beta = DEEPNORM_BETA

    def nrm(k, shape, scale):
        return jax.random.normal(k, shape, f32) * scale

    x = nrm(ks[0], (BATCH, SEQ, D_MODEL), 1.0)
    c = nrm(ks[1], (BATCH, D_MODEL), 1.0)
    w_ada = nrm(ks[2], (DEPTH, D_MODEL, 6 * D_MODEL), ADA_STD * D_MODEL ** -0.5)
    b_ada = nrm(ks[3], (DEPTH, 6 * D_MODEL), 0.01)
    in_scales = (1.0, 1.0, beta, 1.0, 1.0, beta, 1.0, 1.0, 1.0)
    w_in = jnp.concatenate(
        [nrm(ks[4 + i], (DEPTH, D_MODEL, n), sc * D_MODEL ** -0.5)
         for i, (n, sc) in enumerate(zip(IN_SIZES, in_scales))], axis=-1)
    w_attn_out = nrm(ks[13], (DEPTH, ATTN_OUT_WIDTH, D_MODEL), beta * ATTN_OUT_WIDTH ** -0.5)
    gamma0 = 1.0 - jnp.exp2(-5.0 - jnp.arange(RET_HEADS, dtype=f32))
    base_logit = jnp.log(gamma0) - jnp.log1p(-gamma0)
    ret_decay_fwd = base_logit + nrm(ks[14], (DEPTH, RET_HEADS), 0.05)
    ret_decay_bwd = base_logit + nrm(ks[15], (DEPTH, RET_HEADS), 0.05)
    ret_gn_gain = 1.0 + nrm(ks[16], (DEPTH, RET_V_WIDTH), 0.05)
    w_ret_out = nrm(ks[17], (DEPTH, RET_V_WIDTH, D_MODEL), beta * RET_V_WIDTH ** -0.5)
    w_out = nrm(ks[18], (DEPTH, D_MODEL, D_MODEL), beta * D_MODEL ** -0.5)
    ln1_gain = 1.0 + nrm(ks[19], (DEPTH, D_MODEL), 0.05)
    ln1_bias = nrm(ks[20], (DEPTH, D_MODEL), 0.01)
    w_coarse = nrm(ks[21], (DEPTH, D_MODEL, N_GROUPS), D_MODEL ** -0.5)
    b_coarse = nrm(ks[22], (DEPTH, N_GROUPS), 0.01)
    w_fine = nrm(ks[23], (DEPTH, N_GROUPS, D_MODEL, EXPERTS_PER_GROUP), D_MODEL ** -0.5)
    b_fine = nrm(ks[24], (DEPTH, N_GROUPS, EXPERTS_PER_GROUP), 0.01)
    w1 = nrm(ks[25], (DEPTH, N_EXPERTS, D_MODEL, EXPERT_FF), beta * D_MODEL ** -0.5)
    w3 = nrm(ks[26], (DEPTH, N_EXPERTS, D_MODEL, EXPERT_FF), beta * D_MODEL ** -0.5)
    w2 = nrm(ks[27], (DEPTH, N_EXPERTS, EXPERT_FF, D_MODEL), beta * EXPERT_FF ** -0.5)
    ln2_gain = 1.0 + nrm(ks[28], (DEPTH, D_MODEL), 0.05)
    ln2_bias = nrm(ks[29], (DEPTH, D_MODEL), 0.01)
    return {'x': x, 'c': c, 'w_ada': w_ada, 'b_ada': b_ada, 'w_in': w_in, 'w_attn_out': w_attn_out,
            'ret_decay_fwd': ret_decay_fwd, 'ret_decay_bwd': ret_decay_bwd, 'ret_gn_gain': ret_gn_gain,
            'w_ret_out': w_ret_out, 'w_out': w_out, 'ln1_gain': ln1_gain, 'ln1_bias': ln1_bias,
            'w_coarse': w_coarse, 'b_coarse': b_coarse, 'w_fine': w_fine, 'b_fine': b_fine,
            'w1': w1, 'w3': w3, 'w2': w2, 'ln2_gain': ln2_gain, 'ln2_bias': ln2_bias}


def reference(x, c, w_ada, b_ada, w_in, w_attn_out, ret_decay_fwd, ret_decay_bwd, ret_gn_gain,
              w_ret_out, w_out, ln1_gain, ln1_bias, w_coarse, b_coarse, w_fine, b_fine,
              w1, w3, w2, ln2_gain, ln2_bias):
    h = x
    for layer in range(DEPTH):
        mod = c @ w_ada[layer] + b_ada[layer]
        shift1, scale1, gate1, shift2, scale2, gate2 = [m[:, None, :] for m in jnp.split(mod, 6, axis=-1)]
        u = _layer_norm(h) * (1.0 + scale1) + shift1
        y = _token_mixing(u, w_in[layer], w_attn_out[layer], ret_decay_fwd[layer], ret_decay_bwd[layer],
                          ret_gn_gain[layer], w_ret_out[layer], w_out[layer])
        h = _layer_norm_affine(DEEPNORM_ALPHA * h + gate1 * y, ln1_gain[layer], ln1_bias[layer])
        u = _layer_norm(h) * (1.0 + scale2) + shift2
        y = _hierarchical_moe(u, w_coarse[layer], b_coarse[layer], w_fine[layer], b_fine[layer],
                              w1[layer], w3[layer], w2[layer])
        h = _layer_norm_affine(DEEPNORM_ALPHA * h + gate2 * y, ln2_gain[layer], ln2_bias[layer])
    return h
```

```python
import functools
import math

import jax
import jax.numpy as jnp
from jax import lax
from jax.experimental import pallas as pl
from jax.experimental.pallas import tpu as pltpu

F32 = jnp.float32
BF16 = jnp.bfloat16
I32 = jnp.int32

D_MODEL = 1024
ATTN_GROUPS = 3
HEADS_PER_GROUP = 4
HEAD_DIM = 128
ATTN_HEADS = ATTN_GROUPS * HEADS_PER_GROUP
GROUP_WIDTH = HEADS_PER_GROUP * HEAD_DIM
ATTN_PATTERNS = ((128, 1), (512, 4), (2048, 16))
ALIBI_MAX_EXP = 8.0
NEG = -1e30
RET_HEADS = 4
RET_QK = 256
RET_V = 512
N_GROUPS = 4
EXPERTS_PER_GROUP = 8
N_EXPERTS = N_GROUPS * EXPERTS_PER_GROUP
EXPERT_FF = 512
DEPTH = 1
DEEPNORM_ALPHA = (2.0 * DEPTH) ** 0.25
LN_EPS = 1e-5

LANES = 128
PERM_TILE = 512
ATTN_QB = 128
ATTN_HALF = 64
ATTN_KB = ATTN_QB + 2 * ATTN_HALF
RET_CHUNK = 256
MOE_BLK = 256
VMEM_LIMIT = 56 * 1024 * 1024


def _cparams(sem):
    return pltpu.CompilerParams(dimension_semantics=sem, vmem_limit_bytes=VMEM_LIMIT)


def _split_bf16(a):
    hi = a.astype(BF16)
    lo = (a - hi.astype(F32)).astype(BF16)
    return hi, lo


def _dot(a, b):
    return jnp.dot(a, b, preferred_element_type=F32)


def _dot3(a, b):
    ah, al = _split_bf16(a)
    bh, bl = _split_bf16(b)
    return _dot(ah, bh) + _dot(ah, bl) + _dot(al, bh)


def _ln(x):
    mu = jnp.mean(x, axis=-1, keepdims=True)
    xc = x - mu
    var = jnp.mean(xc * xc, axis=-1, keepdims=True)
    return xc * lax.rsqrt(var + LN_EPS)


def _sigmoid(x):
    return 1.0 / (1.0 + jnp.exp(-x))


def _ada_kernel(c_ref, w_ref, b_ref, o_ref):
    o_ref[...] = _dot3(c_ref[...], w_ref[...]) + b_ref[...]


def _ada(c8, w_ada, b_ada):
    n = w_ada.shape[1]
    return pl.pallas_call(
        _ada_kernel,
        grid=(n // D_MODEL,),
        in_specs=[pl.BlockSpec((8, D_MODEL), lambda j: (0, 0)),
                  pl.BlockSpec((D_MODEL, D_MODEL), lambda j: (0, j)),
                  pl.BlockSpec((1, D_MODEL), lambda j: (0, j))],
        out_specs=pl.BlockSpec((8, D_MODEL), lambda j: (0, j)),
        out_shape=jax.ShapeDtypeStruct((8, n), F32),
        compiler_params=_cparams(("arbitrary",)),
        name="ada",
    )(c8, w_ada, b_ada)


INPROJ_TM = 1024
LN_ROWS = 256


def _modulated_ln(x_ref, mod_ref, r0, rows):
    x = x_ref[r0:r0 + rows, :]
    return _ln(x) * (1.0 + mod_ref[1:2, :]) + mod_ref[0:1, :]


def _inproj_attn_kernel(x_ref, mod_ref, w_ref, o_ref, uf_scr, u_scr):
    j = pl.program_id(1)

    @pl.when(j == 0)
    def _():
        for r0 in range(0, INPROJ_TM, LN_ROWS):
            u = _modulated_ln(x_ref, mod_ref, r0, LN_ROWS)
            for cc in range(D_MODEL // LANES):
                uf_scr[cc, r0:r0 + LN_ROWS, :] = u[:, cc * LANES:(cc + 1) * LANES]
            u_scr[0, r0:r0 + LN_ROWS, :] = u.astype(BF16)
        for g in (1, 2):
            dil = ATTN_PATTERNS[g][1]
            n = PERM_TILE // dil
            for t0 in range(0, INPROJ_TM, PERM_TILE):
                for res in range(dil):
                    for cc in range(D_MODEL // LANES):
                        rows = uf_scr[cc, pl.ds(t0 + res, n, stride=dil), :]
                        u_scr[g, t0 + res * n:t0 + (res + 1) * n, cc * LANES:(cc + 1) * LANES] = rows.astype(BF16)

    o_ref[...] = _dot(u_scr[j], w_ref[...]).astype(BF16)


def _inproj_attn(x2, mod3, w_qkv, seq):
    t = x2.shape[0]
    tn = 3 * GROUP_WIDTH
    per_seq = seq // INPROJ_TM
    return pl.pallas_call(
        _inproj_attn_kernel,
        grid=(t // INPROJ_TM, ATTN_GROUPS),
        in_specs=[pl.BlockSpec((INPROJ_TM, D_MODEL), lambda i, j: (i, 0)),
                  pl.BlockSpec((None, 6, D_MODEL), lambda i, j: (i // per_seq, 0, 0)),
                  pl.BlockSpec((D_MODEL, tn), lambda i, j: (0, j))],
        out_specs=pl.BlockSpec((INPROJ_TM, tn), lambda i, j: (i, j)),
        out_shape=jax.ShapeDtypeStruct((t, ATTN_GROUPS * tn), BF16),
        scratch_shapes=[pltpu.VMEM((D_MODEL // LANES, INPROJ_TM, LANES), F32),
                        pltpu.VMEM((ATTN_GROUPS, INPROJ_TM, D_MODEL), BF16)],
        compiler_params=_cparams(("arbitrary", "arbitrary")),
        name="inproj_attn",
    )(x2, mod3, w_qkv)


def _inproj_rest_kernel(x_ref, mod_ref, w_ref, o_ref, u_scr):
    @pl.when(pl.program_id(1) == 0)
    def _():
        for r0 in range(0, INPROJ_TM, LN_ROWS):
            u_scr[r0:r0 + LN_ROWS, :] = _modulated_ln(x_ref, mod_ref, r0, LN_ROWS).astype(BF16)

    o_ref[...] = _dot(u_scr[...], w_ref[...]).astype(BF16)


def _inproj_rest(x2, mod3, w_rest, seq):
    t = x2.shape[0]
    n = w_rest.shape[1]
    tn = 1024
    per_seq = seq // INPROJ_TM
    return pl.pallas_call(
        _inproj_rest_kernel,
        grid=(t // INPROJ_TM, n // tn),
        in_specs=[pl.BlockSpec((INPROJ_TM, D_MODEL), lambda i, j: (i, 0)),
                  pl.BlockSpec((None, 6, D_MODEL), lambda i, j: (i // per_seq, 0, 0)),
                  pl.BlockSpec((D_MODEL, tn), lambda i, j: (0, j))],
        out_specs=pl.BlockSpec((INPROJ_TM, tn), lambda i, j: (i, j)),
        out_shape=jax.ShapeDtypeStruct((t, n), BF16),
        scratch_shapes=[pltpu.VMEM((INPROJ_TM, D_MODEL), BF16)],
        compiler_params=_cparams(("arbitrary", "arbitrary")),
        name="inproj_rest",
    )(x2, mod3, w_rest)


def _attn_kernel(q_ref, k_ref, v_ref, o_ref, lse_ref, q2, k2, v2, o2, l2, *, group, dil, nt, n):
    h = pl.program_id(2)
    n_sub = nt * n
    for t in range(nt):
        q2[t * n:(t + 1) * n, :] = q_ref[t]
        k2[t * n:(t + 1) * n, :] = k_ref[t]
        v2[t * n:(t + 1) * n, :] = v_ref[t]

    head = (group * HEADS_PER_GROUP + h + 1).astype(F32)
    slope = jnp.exp(jnp.zeros((1, ATTN_KB), F32) + head * (-ALIBI_MAX_EXP / ATTN_HEADS * math.log(2.0)))
    slope = slope * float(dil)
    base = (lax.broadcasted_iota(I32, (ATTN_QB, ATTN_KB), 1)
            - lax.broadcasted_iota(I32, (ATTN_QB, ATTN_KB), 0))
    scale = HEAD_DIM ** -0.5

    def block(blk, carry):
        q0 = pl.multiple_of(blk * ATTN_QB, ATTN_QB)
        start = pl.multiple_of(jnp.clip(q0 - ATTN_HALF, 0, n_sub - ATTN_KB), ATTN_HALF)
        qb = q2[pl.ds(q0, ATTN_QB), :]
        kb = k2[pl.ds(start, ATTN_KB), :]
        vb = v2[pl.ds(start, ATTN_KB), :]
        s = lax.dot_general(qb, kb, (((1,), (1,)), ((), ())), preferred_element_type=F32) * scale
        dist = jnp.abs(base + (start - q0))
        s = jnp.where(dist <= ATTN_HALF, s - slope * dist.astype(F32), NEG)
        m = jnp.max(s, axis=-1, keepdims=True)
        p = jnp.exp(s - m)
        l = jnp.sum(p, axis=-1, keepdims=True)
        o = _dot(p.astype(BF16), vb) * (1.0 / l)
        o2[pl.ds(q0, ATTN_QB), :] = o.astype(BF16)
        l2[pl.ds(q0, ATTN_QB), :] = jnp.broadcast_to(m + jnp.log(l), (ATTN_QB, LANES))
        return carry

    lax.fori_loop(0, n_sub // ATTN_QB, block, 0)

    @pl.when(h == 0)
    def _():
        lse_ref[...] = jnp.zeros_like(lse_ref)

    lane = lax.broadcasted_iota(I32, (n, LANES), 1)
    for t in range(nt):
        o_ref[t] = o2[t * n:(t + 1) * n, :]
        lse_ref[t] = jnp.where(lane == h, l2[t * n:(t + 1) * n, :], lse_ref[t])


def _attention(qkv, group, batch, seq):
    dil = ATTN_PATTERNS[group][1]
    if dil == 1:
        nt, n = 1, seq
    else:
        nt, n = seq // PERM_TILE, PERM_TILE // dil
    n_sub = nt * n
    t = batch * seq
    qkv5 = qkv.reshape(batch, nt, dil, n, qkv.shape[1])
    cb = group * 3 * HEADS_PER_GROUP
    blk = (None, nt, None, n, HEAD_DIM)
    out, lse = pl.pallas_call(
        functools.partial(_attn_kernel, group=group, dil=dil, nt=nt, n=n),
        grid=(batch, dil, HEADS_PER_GROUP),
        in_specs=[pl.BlockSpec(blk, lambda b, r, h: (b, 0, r, 0, cb + h)),
                  pl.BlockSpec(blk, lambda b, r, h: (b, 0, r, 0, cb + HEADS_PER_GROUP + h)),
                  pl.BlockSpec(blk, lambda b, r, h: (b, 0, r, 0, cb + 2 * HEADS_PER_GROUP + h))],
        out_specs=[pl.BlockSpec(blk, lambda b, r, h: (b, 0, r, 0, h)),
                   pl.BlockSpec((None, nt, None, n, LANES), lambda b, r, h: (b, 0, r, 0, 0))],
        out_shape=[jax.ShapeDtypeStruct((batch, nt, dil, n, GROUP_WIDTH), BF16),
                   jax.ShapeDtypeStruct((batch, nt, dil, n, LANES), F32)],
        scratch_shapes=[pltpu.VMEM((n_sub, HEAD_DIM), BF16)] * 4 + [pltpu.VMEM((n_sub, LANES), F32)],
        compiler_params=_cparams(("arbitrary", "arbitrary", "arbitrary")),
        name=f"attn_g{group}",
    )(qkv5, qkv5, qkv5)
    return out.reshape(t, GROUP_WIDTH), lse.reshape(t, LANES)


def _log_sigmoid(x):
    return jnp.minimum(x, 0.0) - jnp.log(1.0 + jnp.exp(-jnp.abs(x)))


def _retention_kernel(decay_ref, q_ref, k_ref, v_ref, g_ref, gain_ref, o_ref, state, ybwd, *, nc):
    c = RET_CHUNK
    h = pl.program_id(1)
    i = pl.program_id(2)
    kscale = RET_QK ** -0.5

    def chunk_step(lg, forward):
        q = q_ref[...]
        k = k_ref[...]
        v = v_ref[...]
        row = lax.broadcasted_iota(I32, (c, c), 0)
        col = lax.broadcasted_iota(I32, (c, c), 1)
        pos = lax.broadcasted_iota(I32, (c, 1), 0).astype(F32)
        if forward:
            gap = row - col
            keep = gap >= 0
            key_pow = (c - 1.0) - pos
            query_pow = pos + 1.0
        else:
            gap = col - row
            keep = gap > 0
            key_pow = pos
            query_pow = c - pos
        decay = jnp.where(keep, jnp.exp(lg * jnp.maximum(gap, 0).astype(F32)) * kscale, 0.0)
        inner = lax.dot_general(q, k, (((1,), (1,)), ((), ())), preferred_element_type=F32) * decay
        y = _dot(inner.astype(BF16), v)
        y = y + _dot(q, state[...].astype(BF16)) * jnp.exp(lg * query_pow)
        kd = (k.astype(F32) * (jnp.exp(lg * key_pow) * kscale)).astype(BF16)
        kv = lax.dot_general(kd, v, (((0,), (0,)), ((), ())), preferred_element_type=F32)
        state[...] = state[...] * jnp.exp(lg * float(c)) + kv
        return y

    @pl.when((i == 0) | (i == nc))
    def _():
        state[...] = jnp.zeros_like(state)

    @pl.when(i < nc)
    def _():
        lg = _log_sigmoid(jnp.zeros((1, 1), F32) + decay_ref[RET_HEADS + h])
        ci = nc - 1 - i
        ybwd[pl.ds(pl.multiple_of(ci * c, c), c), :] = chunk_step(lg, False)

    @pl.when(i >= nc)
    def _():
        lg = _log_sigmoid(jnp.zeros((1, 1), F32) + decay_ref[h])
        ci = i - nc
        y = chunk_step(lg, True) + ybwd[pl.ds(pl.multiple_of(ci * c, c), c), :]
        g = g_ref[...].astype(F32)
        o_ref[...] = (g * _sigmoid(g) * (_ln(y) * gain_ref[...])).astype(BF16)


def _retention(rest, decays, gn_gain, batch, seq):
    c = RET_CHUNK
    nc = seq // c
    rest3 = rest.reshape(batch, seq, rest.shape[1])

    def chunk(i):
        return jnp.where(i < nc, nc - 1 - i, i - nc)

    grid_spec = pltpu.PrefetchScalarGridSpec(
        num_scalar_prefetch=1,
        grid=(batch, RET_HEADS, 2 * nc),
        in_specs=[pl.BlockSpec((None, c, RET_QK), lambda b, h, i, d: (b, chunk(i), h)),
                  pl.BlockSpec((None, c, RET_QK), lambda b, h, i, d: (b, chunk(i), RET_HEADS + h)),
                  pl.BlockSpec((None, c, RET_V), lambda b, h, i, d: (b, chunk(i), RET_HEADS + h)),
                  pl.BlockSpec((None, c, RET_V), lambda b, h, i, d: (b, chunk(i), 2 * RET_HEADS + h)),
                  pl.BlockSpec((1, RET_V), lambda b, h, i, d: (0, h))],
        out_specs=pl.BlockSpec((None, c, RET_V), lambda b, h, i, d: (b, jnp.maximum(i - nc, 0), h)),
        scratch_shapes=[pltpu.VMEM((RET_QK, RET_V), F32), pltpu.VMEM((seq, RET_V), F32)],
    )
    out = pl.pallas_call(
        functools.partial(_retention_kernel, nc=nc),
        grid_spec=grid_spec,
        out_shape=jax.ShapeDtypeStruct((batch, seq, RET_HEADS * RET_V), BF16),
        compiler_params=_cparams(("arbitrary", "arbitrary", "arbitrary")),
        name="retention",
    )(decays, rest3, rest3, rest3, rest3, gn_gain)
    return out.reshape(batch * seq, RET_HEADS * RET_V)


MERGE_TM = PERM_TILE


def _merge_kernel(o0_ref, o1_ref, o2_ref, l0_ref, l1_ref, l2_ref, retg_ref, ga_ref, gr_ref, x_ref, mod_ref,
                  wa_ref, wr_ref, wo_ref, ln_ref, wrh_ref, wrl_ref, rb_ref,
                  h1_ref, u2_ref, lg_ref, on_scr, ln_scr):
    for g in (1, 2):
        dil = ATTN_PATTERNS[g][1]
        n = MERGE_TM // dil
        o_ref, l_ref = ((o1_ref, l1_ref), (o2_ref, l2_ref))[g - 1]
        for res in range(dil):
            rows = o_ref[res * n:(res + 1) * n, :].astype(F32)
            for hh in range(HEADS_PER_GROUP):
                on_scr[g - 1, hh, pl.ds(res, n, stride=dil), :] = rows[:, hh * HEAD_DIM:(hh + 1) * HEAD_DIM]
            ln_scr[g - 1, pl.ds(res, n, stride=dil), :] = l_ref[res * n:(res + 1) * n, :]

    l0, l1, l2 = l0_ref[...], ln_scr[0], ln_scr[1]
    lm = jnp.maximum(jnp.maximum(l0, l1), l2)
    e0, e1, e2 = jnp.exp(l0 - lm), jnp.exp(l1 - lm), jnp.exp(l2 - lm)
    inv = 1.0 / (e0 + e1 + e2)
    parts = []
    for hh in range(HEADS_PER_GROUP):
        sl = slice(hh * HEAD_DIM, (hh + 1) * HEAD_DIM)
        acc = (e0[:, hh:hh + 1] * o0_ref[:, sl].astype(F32)
               + e1[:, hh:hh + 1] * on_scr[0, hh]
               + e2[:, hh:hh + 1] * on_scr[1, hh])
        parts.append((acc * inv[:, hh:hh + 1]).astype(BF16))
    attn = jnp.concatenate(parts, axis=1)

    branch_a = _dot(attn, wa_ref[...])
    branch_r = _dot(retg_ref[...], wr_ref[...])
    merged = (_sigmoid(ga_ref[...].astype(F32)) * branch_a + _sigmoid(gr_ref[...].astype(F32)) * branch_r)
    y = _dot(merged.astype(BF16), wo_ref[...])

    h1 = _ln(DEEPNORM_ALPHA * x_ref[...] + mod_ref[2:3, :] * y) * ln_ref[0:1, :] + ln_ref[1:2, :]
    h1_ref[...] = h1
    u2 = _ln(h1) * (1.0 + mod_ref[4:5, :]) + mod_ref[3:4, :]
    u2_ref[...] = u2
    uh, ul = _split_bf16(u2)
    lg_ref[...] = _dot(uh, wrh_ref[...]) + _dot(uh, wrl_ref[...]) + _dot(ul, wrh_ref[...]) + rb_ref[...]


def _merge(outs, lses, retg, rest, x2, mod3, wa, wr, wo, ln1, wr_hi, wr_lo, rbias, seq):
    t = x2.shape[0]
    tm = MERGE_TM
    per_seq = seq // tm
    row = lambda w: pl.BlockSpec((tm, w), lambda i: (i, 0))
    full = lambda a: pl.BlockSpec(a.shape, lambda i: (0,) * a.ndim)
    return pl.pallas_call(
        _merge_kernel,
        grid=(t // tm,),
        in_specs=[row(GROUP_WIDTH)] * 3 + [row(LANES)] * 3 + [
            row(RET_HEADS * RET_V),
            pl.BlockSpec((tm, D_MODEL), lambda i: (i, 6)),
            pl.BlockSpec((tm, D_MODEL), lambda i: (i, 7)),
            row(D_MODEL),
            pl.BlockSpec((None, 6, D_MODEL), lambda i: (i // per_seq, 0, 0)),
            full(wa), full(wr), full(wo), full(ln1), full(wr_hi), full(wr_lo), full(rbias)],
        out_specs=[row(D_MODEL), row(D_MODEL), row(LANES)],
        out_shape=[jax.ShapeDtypeStruct((t, D_MODEL), F32),
                   jax.ShapeDtypeStruct((t, D_MODEL), F32),
                   jax.ShapeDtypeStruct((t, LANES), F32)],
        scratch_shapes=[pltpu.VMEM((2, HEADS_PER_GROUP, tm, HEAD_DIM), F32), pltpu.VMEM((2, tm, LANES), F32)],
        compiler_params=_cparams(("arbitrary",)),
        name="merge",
    )(*outs, *lses, retg, rest, rest, x2, mod3, wa, wr, wo, ln1, wr_hi, wr_lo, rbias)


ROUTE_TM = 512
BIG = 1 << 20


def _route_kernel(lg_ref, cols_ref, ints_ref, cnt_ref, carry):
    i = pl.program_id(0)

    @pl.when(i == 0)
    def _():
        carry[...] = jnp.zeros_like(carry)

    tm = ROUTE_TM
    lg = lg_ref[...]
    lane = lax.broadcasted_iota(I32, (tm, LANES), 1)
    lane_f = lane.astype(F32)
    first = lambda mask: jnp.min(jnp.where(mask, lane_f, float(BIG)), axis=-1, keepdims=True).astype(I32)

    coarse = jnp.where(lane < N_GROUPS, lg, NEG)
    cmax = jnp.max(coarse, axis=-1, keepdims=True)
    gsel = first(coarse == cmax)
    p_group = 1.0 / jnp.sum(jnp.exp(coarse - cmax), axis=-1, keepdims=True)

    lo = N_GROUPS + EXPERTS_PER_GROUP * gsel
    fine = jnp.where((lane >= lo) & (lane < lo + EXPERTS_PER_GROUP), lg, NEG)
    v1 = jnp.max(fine, axis=-1, keepdims=True)
    i1 = first(fine == v1)
    fine2 = jnp.where(lane == i1, NEG, fine)
    v2 = jnp.max(fine2, axis=-1, keepdims=True)
    i2 = first(fine2 == v2)
    ex = jnp.exp(v2 - v1)
    den = 1.0 / (1.0 + ex)
    gate1 = p_group * den
    gate2 = p_group * (ex * den)
    e1 = i1 - N_GROUPS
    e2 = i2 - N_GROUPS

    oh1 = lane == e1
    oh2 = lane == e2
    cnt = jnp.where(oh1 | oh2, 1.0, 0.0)
    r_i = lax.broadcasted_iota(I32, (tm, tm), 0)
    c_i = lax.broadcasted_iota(I32, (tm, tm), 1)
    tri = jnp.where(r_i > c_i, 1.0, 0.0).astype(BF16)
    rank = _dot(tri, cnt.astype(BF16)) + carry[...]
    r1 = jnp.sum(jnp.where(oh1, rank, 0.0), axis=-1, keepdims=True)
    r2 = jnp.sum(jnp.where(oh2, rank, 0.0), axis=-1, keepdims=True)
    carry[...] = carry[...] + jnp.sum(cnt, axis=0, keepdims=True)
    cnt_ref[...] = jnp.broadcast_to(carry[...], cnt_ref.shape)

    cols_ref[...] = jnp.where(lane == 0, gate1, jnp.where(lane == 1, gate2, 0.0))
    packed = jnp.where(lane == 0, e1.astype(F32),
                       jnp.where(lane == 1, e2.astype(F32),
                                 jnp.where(lane == 2, r1, jnp.where(lane == 3, r2, 0.0))))
    ints_ref[...] = packed.T[0:8, :].astype(I32)


def _route(logits):
    t = logits.shape[0]
    tm = ROUTE_TM
    return pl.pallas_call(
        _route_kernel,
        grid=(t // tm,),
        in_specs=[pl.BlockSpec((tm, LANES), lambda i: (i, 0))],
        out_specs=[pl.BlockSpec((tm, LANES), lambda i: (i, 0)),
                   pl.BlockSpec((8, tm), lambda i: (0, i)),
                   pl.BlockSpec((8, LANES), lambda i: (0, 0))],
        out_shape=[jax.ShapeDtypeStruct((t, LANES), F32),
                   jax.ShapeDtypeStruct((8, t), I32),
                   jax.ShapeDtypeStruct((8, LANES), F32)],
        scratch_shapes=[pltpu.VMEM((1, LANES), F32)],
        compiler_params=_cparams(("arbitrary",)),
        name="route",
    )(logits)


def _plan_kernel(ints_ref, cnt_ref, dest_ref, meta_ref, *, n_blocks_pad):
    sub = lax.broadcasted_iota(I32, (LANES, LANES), 0)
    lane = lax.broadcasted_iota(I32, (LANES, LANES), 1)
    cnt = cnt_ref[0:1, :]
    nblk_row = jnp.floor((cnt + (MOE_BLK - 1.0)) * (1.0 / MOE_BLK))
    nblk_mat = jnp.broadcast_to(nblk_row, (LANES, LANES))
    start_col = jnp.sum(jnp.where(lane < sub, nblk_mat, 0.0), axis=-1, keepdims=True)
    nblk_col = jnp.sum(jnp.where(lane == sub, nblk_mat, 0.0), axis=-1, keepdims=True)
    end_col = start_col + nblk_col

    ints = ints_ref[...]
    base = jnp.zeros(ints.shape, F32)
    for e in range(N_EXPERTS):
        base = jnp.where(ints == e, start_col[e:e + 1, :] * float(MOE_BLK), base)
    dest = base[0:2, :].astype(I32) + ints[2:4, :]
    dest_ref[...] = jnp.concatenate([dest, jnp.zeros((6, ints.shape[1]), I32)], axis=0)

    blk = lax.broadcasted_iota(I32, (LANES, n_blocks_pad), 1).astype(F32)
    e_sub = lax.broadcasted_iota(I32, (LANES, n_blocks_pad), 0)
    done = jnp.where((e_sub < N_EXPERTS) & (end_col <= blk), 1.0, 0.0)
    bexp = jnp.minimum(jnp.sum(done, axis=0, keepdims=True), N_EXPERTS - 1.0)
    used = jnp.sum(nblk_row, axis=-1, keepdims=True)
    row = lax.broadcasted_iota(I32, (8, n_blocks_pad), 0)
    meta = jnp.where(row == 0, bexp, jnp.where(row == 1, used, 0.0))
    meta_ref[...] = meta.astype(I32)


def _plan(ints, counts, n_blocks_pad):
    t = ints.shape[1]
    return pl.pallas_call(
        functools.partial(_plan_kernel, n_blocks_pad=n_blocks_pad),
        out_shape=[jax.ShapeDtypeStruct((8, t), I32), jax.ShapeDtypeStruct((8, n_blocks_pad), I32)],
        compiler_params=pltpu.CompilerParams(vmem_limit_bytes=VMEM_LIMIT),
        name="plan",
    )(ints, counts)


DISPATCH_TM = 512


def _dispatch_kernel(dest_ref, u2_ref, rows_in_ref, rows_ref, sem):
    del rows_in_ref
    base = pl.program_id(0) * DISPATCH_TM

    def copy(tok, d):
        return pltpu.make_async_copy(u2_ref.at[pl.ds(tok, 1)], rows_ref.at[pl.ds(d, 1)], sem)

    def issue(t, carry):
        copy(base + t, dest_ref[0, t]).start()
        copy(base + t, dest_ref[1, t]).start()
        return carry

    def drain(t, carry):
        copy(base + t, dest_ref[0, t]).wait()
        copy(base + t, dest_ref[1, t]).wait()
        return carry

    lax.fori_loop(0, DISPATCH_TM, issue, 0)
    lax.fori_loop(0, DISPATCH_TM, drain, 0)


def _dispatch(dest, u2, n_rows):
    t = u2.shape[0]
    rows0 = jnp.zeros((n_rows, D_MODEL), F32)
    return pl.pallas_call(
        _dispatch_kernel,
        grid=(t // DISPATCH_TM,),
        in_specs=[pl.BlockSpec((8, DISPATCH_TM), lambda i: (0, i), memory_space=pltpu.SMEM),
                  pl.BlockSpec(memory_space=pl.ANY),
                  pl.BlockSpec(memory_space=pl.ANY)],
        out_specs=pl.BlockSpec(memory_space=pl.ANY),
        out_shape=jax.ShapeDtypeStruct((n_rows, D_MODEL), F32),
        scratch_shapes=[pltpu.SemaphoreType.DMA(())],
        input_output_aliases={2: 0},
        compiler_params=_cparams(("arbitrary",)),
        name="dispatch",
    )(dest, u2, rows0)


def _experts_kernel(bexp_ref, used_ref, x_ref, w1_ref, w3_ref, w2_ref, y_ref, w1s, w3s, w2s):
    i = pl.program_id(0)
    prev = bexp_ref[jnp.maximum(i - 1, 0)]
    active = i < used_ref[0]

    @pl.when(active & ((i == 0) | (bexp_ref[i] != prev)))
    def _():
        w1s[...] = w1_ref[...].astype(BF16)
        w3s[...] = w3_ref[...].astype(BF16)
        w2s[...] = w2_ref[...].astype(BF16)

    @pl.when(active)
    def _():
        xb = x_ref[...].astype(BF16)
        a = _dot(xb, w1s[...])
        b = _dot(xb, w3s[...])
        hdn = (a * _sigmoid(a) * b).astype(BF16)
        y_ref[...] = _dot(hdn, w2s[...])

    @pl.when(jnp.logical_not(active))
    def _():
        y_ref[...] = jnp.zeros_like(y_ref)


def _experts(bexp, used, rows, w1, w3, w2):
    n_rows = rows.shape[0]
    n_blocks = n_rows // MOE_BLK
    grid_spec = pltpu.PrefetchScalarGridSpec(
        num_scalar_prefetch=2,
        grid=(n_blocks,),
        in_specs=[pl.BlockSpec((MOE_BLK, D_MODEL), lambda i, be, nu: (i, 0)),
                  pl.BlockSpec((None, D_MODEL, EXPERT_FF), lambda i, be, nu: (be[i], 0, 0)),
                  pl.BlockSpec((None, D_MODEL, EXPERT_FF), lambda i, be, nu: (be[i], 0, 0)),
                  pl.BlockSpec((None, EXPERT_FF, D_MODEL), lambda i, be, nu: (be[i], 0, 0))],
        out_specs=pl.BlockSpec((MOE_BLK, D_MODEL), lambda i, be, nu: (i, 0)),
        scratch_shapes=[pltpu.VMEM((D_MODEL, EXPERT_FF), BF16), pltpu.VMEM((D_MODEL, EXPERT_FF), BF16),
                        pltpu.VMEM((EXPERT_FF, D_MODEL), BF16)],
    )
    return pl.pallas_call(
        _experts_kernel,
        grid_spec=grid_spec,
        out_shape=jax.ShapeDtypeStruct((n_rows, D_MODEL), F32),
        compiler_params=_cparams(("arbitrary",)),
        name="experts",
    )(bexp, used, rows, w1, w3, w2)


COMBINE_TM = 256


def _combine_kernel(dest_ref, y_ref, cols_ref, h1_ref, mod_ref, ln_ref, o_ref, ya, yb, sem):
    def copy(d, buf, t):
        return pltpu.make_async_copy(y_ref.at[pl.ds(d, 1)], buf.at[pl.ds(t, 1)], sem)

    def issue(t, carry):
        copy(dest_ref[0, t], ya, t).start()
        copy(dest_ref[1, t], yb, t).start()
        return carry

    def drain(t, carry):
        copy(dest_ref[0, t], ya, t).wait()
        copy(dest_ref[1, t], yb, t).wait()
        return carry

    lax.fori_loop(0, COMBINE_TM, issue, 0)
    lax.fori_loop(0, COMBINE_TM, drain, 0)

    cols = cols_ref[...]
    moe = cols[:, 0:1] * ya[...] + cols[:, 1:2] * yb[...]
    pre = DEEPNORM_ALPHA * h1_ref[...] + mod_ref[5:6, :] * moe
    o_ref[...] = _ln(pre) * ln_ref[0:1, :] + ln_ref[1:2, :]


def _combine(dest, y_rows, cols, h1, mod3, ln2, seq):
    t = h1.shape[0]
    tm = COMBINE_TM
    per_seq = seq // tm
    return pl.pallas_call(
        _combine_kernel,
        grid=(t // tm,),
        in_specs=[pl.BlockSpec((8, tm), lambda i: (0, i), memory_space=pltpu.SMEM),
                  pl.BlockSpec(memory_space=pl.ANY),
                  pl.BlockSpec((tm, LANES), lambda i: (i, 0)),
                  pl.BlockSpec((tm, D_MODEL), lambda i: (i, 0)),
                  pl.BlockSpec((None, 6, D_MODEL), lambda i: (i // per_seq, 0, 0)),
                  pl.BlockSpec((2, D_MODEL), lambda i: (0, 0))],
        out_specs=pl.BlockSpec((tm, D_MODEL), lambda i: (i, 0)),
        out_shape=jax.ShapeDtypeStruct((t, D_MODEL), F32),
        scratch_shapes=[pltpu.VMEM((tm, D_MODEL), F32), pltpu.VMEM((tm, D_MODEL), F32),
                        pltpu.SemaphoreType.DMA(())],
        compiler_params=_cparams(("arbitrary",)),
        name="combine",
    )(dest, y_rows, cols, h1, mod3, ln2)


def _layer(h, c8, w_ada, b_ada, w_in, w_attn_out, decay_f, decay_b, gn_gain, w_ret_out, w_out,
           ln1_gain, ln1_bias, w_coarse, b_coarse, w_fine, b_fine, w1, w3, w2, ln2_gain, ln2_bias):
    batch, seq, d = h.shape
    t = batch * seq
    x2 = h.reshape(t, d)

    mod = _ada(c8, w_ada, b_ada.reshape(1, -1))
    mod3 = mod[:batch].reshape(batch, 6, d)

    aw = ATTN_HEADS * HEAD_DIM
    wq, wk, wv = (w_in[:, s * aw:(s + 1) * aw] for s in range(3))
    gcols = lambda w, g: w[:, g * GROUP_WIDTH:(g + 1) * GROUP_WIDTH]
    w_qkv = jnp.concatenate([gcols(w, g) for g in range(ATTN_GROUPS) for w in (wq, wk, wv)], axis=1).astype(BF16)
    w_rest = w_in[:, 3 * aw:].astype(BF16)

    qkv = _inproj_attn(x2, mod3, w_qkv, seq)
    rest = _inproj_rest(x2, mod3, w_rest, seq)

    outs, lses = zip(*[_attention(qkv, g, batch, seq) for g in range(ATTN_GROUPS)])
    decays = jnp.concatenate([decay_f, decay_b]).astype(F32)
    retg = _retention(rest, decays, gn_gain.reshape(1, -1), batch, seq)

    w_route = jnp.concatenate([w_coarse] + [w_fine[g] for g in range(N_GROUPS)], axis=1)
    n_route = w_route.shape[1]
    w_route = jnp.pad(w_route, ((0, 0), (0, LANES - n_route)))
    wr_hi, wr_lo = _split_bf16(w_route)
    rbias = jnp.pad(jnp.concatenate([b_coarse, b_fine.reshape(-1)]), (0, LANES - n_route)).reshape(1, LANES)
    ln1 = jnp.stack([ln1_gain, ln1_bias])
    h1, u2, logits = _merge(outs, lses, retg, rest, x2, mod3, w_attn_out.astype(BF16), w_ret_out.astype(BF16),
                            w_out.astype(BF16), ln1, wr_hi, wr_lo, rbias, seq)

    cols, ints, counts = _route(logits)
    n_blocks = 2 * t // MOE_BLK + N_EXPERTS
    n_blocks_pad = -(-n_blocks // LANES) * LANES
    dest, meta = _plan(ints, counts, n_blocks_pad)
    rows = _dispatch(dest, u2, n_blocks * MOE_BLK)
    y_rows = _experts(meta[0, :n_blocks], meta[1, :1], rows, w1, w3, w2)
    out = _combine(dest, y_rows, cols, h1, mod3, jnp.stack([ln2_gain, ln2_bias]), seq)
    return out.reshape(batch, seq, d)


def kernel(x, c, w_ada, b_ada, w_in, w_attn_out, ret_decay_fwd, ret_decay_bwd, ret_gn_gain, w_ret_out, w_out,
           ln1_gain, ln1_bias, w_coarse, b_coarse, w_fine, b_fine, w1, w3, w2, ln2_gain, ln2_bias):
    batch = x.shape[0]
    assert batch <= 8 and x.shape[1] % (2 * INPROJ_TM) == 0 and x.shape[2] == D_MODEL
    c8 = jnp.pad(c, ((0, 8 - batch), (0, 0)))
    h = x
    for l in range(w_ada.shape[0]):
        h = _layer(h, c8, w_ada[l], b_ada[l], w_in[l], w_attn_out[l], ret_decay_fwd[l], ret_decay_bwd[l],
                   ret_gn_gain[l], w_ret_out[l], w_out[l], ln1_gain[l], ln1_bias[l], w_coarse[l], b_coarse[l],
                   w_fine[l], b_fine[l], w1[l], w3[l], w2[l], ln2_gain[l], ln2_bias[l])
    return h
```

```python
import functools
import math

import jax
import jax.numpy as jnp
from jax import lax
from jax.experimental import pallas as pl
from jax.experimental.pallas import tpu as pltpu

F32 = jnp.float32
BF16 = jnp.bfloat16
I32 = jnp.int32

D_MODEL = 1024
ATTN_GROUPS = 3
HEADS_PER_GROUP = 4
HEAD_DIM = 128
ATTN_HEADS = ATTN_GROUPS * HEADS_PER_GROUP
GROUP_WIDTH = HEADS_PER_GROUP * HEAD_DIM
ATTN_PATTERNS = ((128, 1), (512, 4), (2048, 16))
ALIBI_MAX_EXP = 8.0
NEG = -1e30
RET_HEADS = 4
RET_QK = 256
RET_V = 512
N_GROUPS = 4
EXPERTS_PER_GROUP = 8
N_EXPERTS = N_GROUPS * EXPERTS_PER_GROUP
EXPERT_FF = 512
DEPTH = 1
DEEPNORM_ALPHA = (2.0 * DEPTH) ** 0.25
LN_EPS = 1e-5

LANES = 128
PERM_TILE = 512
ATTN_QB = 128
ATTN_HALF = 64
ATTN_KB = ATTN_QB + 2 * ATTN_HALF
RET_CHUNK = 256
MOE_BLK = 256
VMEM_LIMIT = 56 * 1024 * 1024


def _cparams(sem):
    return pltpu.CompilerParams(dimension_semantics=sem, vmem_limit_bytes=VMEM_LIMIT)


def _split_bf16(a):
    hi = a.astype(BF16)
    lo = (a - hi.astype(F32)).astype(BF16)
    return hi, lo


def _dot(a, b):
    return jnp.dot(a, b, preferred_element_type=F32)


def _dot3(a, b):
    ah, al = _split_bf16(a)
    bh, bl = _split_bf16(b)
    return _dot(ah, bh) + _dot(ah, bl) + _dot(al, bh)


def _ln(x):
    mu = jnp.mean(x, axis=-1, keepdims=True)
    xc = x - mu
    var = jnp.mean(xc * xc, axis=-1, keepdims=True)
    return xc * lax.rsqrt(var + LN_EPS)


def _sigmoid(x):
    return 1.0 / (1.0 + jnp.exp(-x))


def _ada_kernel(c_ref, w_ref, b_ref, o_ref):
    o_ref[...] = _dot3(c_ref[...], w_ref[...]) + b_ref[...]


def _ada(c8, w_ada, b_ada):
    n = w_ada.shape[1]
    return pl.pallas_call(
        _ada_kernel,
        grid=(n // D_MODEL,),
        in_specs=[pl.BlockSpec((8, D_MODEL), lambda j: (0, 0)),
                  pl.BlockSpec((D_MODEL, D_MODEL), lambda j: (0, j)),
                  pl.BlockSpec((1, D_MODEL), lambda j: (0, j))],
        out_specs=pl.BlockSpec((8, D_MODEL), lambda j: (0, j)),
        out_shape=jax.ShapeDtypeStruct((8, n), F32),
        compiler_params=_cparams(("arbitrary",)),
        name="ada",
    )(c8, w_ada, b_ada)


INPROJ_TM = 1024
LN_ROWS = 256


def _modulated_ln(x_ref, mod_ref, r0, rows):
    x = x_ref[r0:r0 + rows, :]
    return _ln(x) * (1.0 + mod_ref[1:2, :]) + mod_ref[0:1, :]


def _inproj_attn_kernel(x_ref, mod_ref, w_ref, o_ref, uf_scr, u_scr):
    j = pl.program_id(1)

    @pl.when(j == 0)
    def _():
        for r0 in range(0, INPROJ_TM, LN_ROWS):
            u = _modulated_ln(x_ref, mod_ref, r0, LN_ROWS)
            for cc in range(D_MODEL // LANES):
                uf_scr[cc, r0:r0 + LN_ROWS, :] = u[:, cc * LANES:(cc + 1) * LANES]
            u_scr[0, r0:r0 + LN_ROWS, :] = u.astype(BF16)
        for g in (1, 2):
            dil = ATTN_PATTERNS[g][1]
            n = PERM_TILE // dil
            for t0 in range(0, INPROJ_TM, PERM_TILE):
                for res in range(dil):
                    for cc in range(D_MODEL // LANES):
                        rows = uf_scr[cc, pl.ds(t0 + res, n, stride=dil), :]
                        u_scr[g, t0 + res * n:t0 + (res + 1) * n, cc * LANES:(cc + 1) * LANES] = rows.astype(BF16)

    o_ref[...] = _dot(u_scr[j], w_ref[...]).astype(BF16)


def _inproj_attn(x2, mod3, w_qkv, seq):
    t = x2.shape[0]
    tn = 3 * GROUP_WIDTH
    per_seq = seq // INPROJ_TM
    return pl.pallas_call(
        _inproj_attn_kernel,
        grid=(t // INPROJ_TM, ATTN_GROUPS),
        in_specs=[pl.BlockSpec((INPROJ_TM, D_MODEL), lambda i, j: (i, 0)),
                  pl.BlockSpec((None, 6, D_MODEL), lambda i, j: (i // per_seq, 0, 0)),
                  pl.BlockSpec((D_MODEL, tn), lambda i, j: (0, j))],
        out_specs=pl.BlockSpec((INPROJ_TM, tn), lambda i, j: (i, j)),
        out_shape=jax.ShapeDtypeStruct((t, ATTN_GROUPS * tn), BF16),
        scratch_shapes=[pltpu.VMEM((D_MODEL // LANES, INPROJ_TM, LANES), F32),
                        pltpu.VMEM((ATTN_GROUPS, INPROJ_TM, D_MODEL), BF16)],
        compiler_params=_cparams(("arbitrary", "arbitrary")),
        name="inproj_attn",
    )(x2, mod3, w_qkv)


def _inproj_rest_kernel(x_ref, mod_ref, w_ref, o_ref, u_scr):
    @pl.when(pl.program_id(1) == 0)
    def _():
        for r0 in range(0, INPROJ_TM, LN_ROWS):
            u_scr[r0:r0 + LN_ROWS, :] = _modulated_ln(x_ref, mod_ref, r0, LN_ROWS).astype(BF16)

    o_ref[...] = _dot(u_scr[...], w_ref[...]).astype(BF16)


def _inproj_rest(x2, mod3, w_rest, seq):
    t = x2.shape[0]
    n = w_rest.shape[1]
    tn = 1024
    per_seq = seq // INPROJ_TM
    return pl.pallas_call(
        _inproj_rest_kernel,
        grid=(t // INPROJ_TM, n // tn),
        in_specs=[pl.BlockSpec((INPROJ_TM, D_MODEL), lambda i, j: (i, 0)),
                  pl.BlockSpec((None, 6, D_MODEL), lambda i, j: (i // per_seq, 0, 0)),
                  pl.BlockSpec((D_MODEL, tn), lambda i, j: (0, j))],
        out_specs=pl.BlockSpec((INPROJ_TM, tn), lambda i, j: (i, j)),
        out_shape=jax.ShapeDtypeStruct((t, n), BF16),
        scratch_shapes=[pltpu.VMEM((INPROJ_TM, D_MODEL), BF16)],
        compiler_params=_cparams(("arbitrary", "arbitrary")),
        name="inproj_rest",
    )(x2, mod3, w_rest)


def _attn_kernel(q_ref, k_ref, v_ref, o_ref, lse_ref, q2, k2, v2, o2, l2, *, group, dil, nt, n):
    h = pl.program_id(2)
    n_sub = nt * n
    for t in range(nt):
        q2[t * n:(t + 1) * n, :] = q_ref[t]
        k2[t * n:(t + 1) * n, :] = k_ref[t]
        v2[t * n:(t + 1) * n, :] = v_ref[t]

    head = (group * HEADS_PER_GROUP + h + 1).astype(F32)
    slope = jnp.exp(jnp.zeros((1, ATTN_KB), F32) + head * (-ALIBI_MAX_EXP / ATTN_HEADS * math.log(2.0)))
    slope = slope * float(dil)
    base = (lax.broadcasted_iota(I32, (ATTN_QB, ATTN_KB), 1)
            - lax.broadcasted_iota(I32, (ATTN_QB, ATTN_KB), 0))
    scale = HEAD_DIM ** -0.5

    def block(blk, carry):
        q0 = pl.multiple_of(blk * ATTN_QB, ATTN_QB)
        start = pl.multiple_of(jnp.clip(q0 - ATTN_HALF, 0, n_sub - ATTN_KB), ATTN_HALF)
        qb = q2[pl.ds(q0, ATTN_QB), :]
        kb = k2[pl.ds(start, ATTN_KB), :]
        vb = v2[pl.ds(start, ATTN_KB), :]
        s = lax.dot_general(qb, kb, (((1,), (1,)), ((), ())), preferred_element_type=F32) * scale
        dist = jnp.abs(base + (start - q0))
        s = jnp.where(dist <= ATTN_HALF, s - slope * dist.astype(F32), NEG)
        m = jnp.max(s, axis=-1, keepdims=True)
        p = jnp.exp(s - m)
        l = jnp.sum(p, axis=-1, keepdims=True)
        o = _dot(p.astype(BF16), vb) * (1.0 / l)
        o2[pl.ds(q0, ATTN_QB), :] = o.astype(BF16)
        l2[pl.ds(q0, ATTN_QB), :] = jnp.broadcast_to(m + jnp.log(l), (ATTN_QB, LANES))
        return carry

    lax.fori_loop(0, n_sub // ATTN_QB, block, 0)

    @pl.when(h == 0)
    def _():
        lse_ref[...] = jnp.zeros_like(lse_ref)

    lane = lax.broadcasted_iota(I32, (n, LANES), 1)
    for t in range(nt):
        o_ref[t] = o2[t * n:(t + 1) * n, :]
        lse_ref[t] = jnp.where(lane == h, l2[t * n:(t + 1) * n, :], lse_ref[t])


def _attention(qkv, group, batch, seq):
    dil = ATTN_PATTERNS[group][1]
    if dil == 1:
        nt, n = 1, seq
    else:
        nt, n = seq // PERM_TILE, PERM_TILE // dil
    n_sub = nt * n
    t = batch * seq
    qkv5 = qkv.reshape(batch, nt, dil, n, qkv.shape[1])
    cb = group * 3 * HEADS_PER_GROUP
    blk = (None, nt, None, n, HEAD_DIM)
    out, lse = pl.pallas_call(
        functools.partial(_attn_kernel, group=group, dil=dil, nt=nt, n=n),
        grid=(batch, dil, HEADS_PER_GROUP),
        in_specs=[pl.BlockSpec(blk, lambda b, r, h: (b, 0, r, 0, cb + h)),
                  pl.BlockSpec(blk, lambda b, r, h: (b, 0, r, 0, cb + HEADS_PER_GROUP + h)),
                  pl.BlockSpec(blk, lambda b, r, h: (b, 0, r, 0, cb + 2 * HEADS_PER_GROUP + h))],
        out_specs=[pl.BlockSpec(blk, lambda b, r, h: (b, 0, r, 0, h)),
                   pl.BlockSpec((None, nt, None, n, LANES), lambda b, r, h: (b, 0, r, 0, 0))],
        out_shape=[jax.ShapeDtypeStruct((batch, nt, dil, n, GROUP_WIDTH), BF16),
                   jax.ShapeDtypeStruct((batch, nt, dil, n, LANES), F32)],
        scratch_shapes=[pltpu.VMEM((n_sub, HEAD_DIM), BF16)] * 4 + [pltpu.VMEM((n_sub, LANES), F32)],
        compiler_params=_cparams(("arbitrary", "arbitrary", "arbitrary")),
        name=f"attn_g{group}",
    )(qkv5, qkv5, qkv5)
    return out.reshape(t, GROUP_WIDTH), lse.reshape(t, LANES)


def _log_sigmoid(x):
    return jnp.minimum(x, 0.0) - jnp.log(1.0 + jnp.exp(-jnp.abs(x)))


def _retention_kernel(decay_ref, q_ref, k_ref, v_ref, g_ref, gain_ref, o_ref, state, ybwd, *, nc):
    c = RET_CHUNK
    h = pl.program_id(1)
    i = pl.program_id(2)
    kscale = RET_QK ** -0.5

    def chunk_step(lg, forward):
        q = q_ref[...]
        k = k_ref[...]
        v = v_ref[...]
        row = lax.broadcasted_iota(I32, (c, c), 0)
        col = lax.broadcasted_iota(I32, (c, c), 1)
        pos = lax.broadcasted_iota(I32, (c, 1), 0).astype(F32)
        if forward:
            gap = row - col
            keep = gap >= 0
            key_pow = (c - 1.0) - pos
            query_pow = pos + 1.0
        else:
            gap = col - row
            keep = gap > 0
            key_pow = pos
            query_pow = c - pos
        decay = jnp.where(keep, jnp.exp(lg * jnp.maximum(gap, 0).astype(F32)) * kscale, 0.0)
        inner = lax.dot_general(q, k, (((1,), (1,)), ((), ())), preferred_element_type=F32) * decay
        y = _dot(inner.astype(BF16), v)
        y = y + _dot(q, state[...].astype(BF16)) * jnp.exp(lg * query_pow)
        kd = (k.astype(F32) * (jnp.exp(lg * key_pow) * kscale)).astype(BF16)
        kv = lax.dot_general(kd, v, (((0,), (0,)), ((), ())), preferred_element_type=F32)
        state[...] = state[...] * jnp.exp(lg * float(c)) + kv
        return y

    @pl.when((i == 0) | (i == nc))
    def _():
        state[...] = jnp.zeros_like(state)

    @pl.when(i < nc)
    def _():
        lg = _log_sigmoid(jnp.zeros((1, 1), F32) + decay_ref[RET_HEADS + h])
        ci = nc - 1 - i
        ybwd[pl.ds(pl.multiple_of(ci * c, c), c), :] = chunk_step(lg, False)

    @pl.when(i >= nc)
    def _():
        lg = _log_sigmoid(jnp.zeros((1, 1), F32) + decay_ref[h])
        ci = i - nc
        y = chunk_step(lg, True) + ybwd[pl.ds(pl.multiple_of(ci * c, c), c), :]
        g = g_ref[...].astype(F32)
        o_ref[...] = (g * _sigmoid(g) * (_ln(y) * gain_ref[...])).astype(BF16)


def _retention(rest, decays, gn_gain, batch, seq):
    c = RET_CHUNK
    nc = seq // c
    rest3 = rest.reshape(batch, seq, rest.shape[1])

    def chunk(i):
        return jnp.where(i < nc, nc - 1 - i, i - nc)

    grid_spec = pltpu.PrefetchScalarGridSpec(
        num_scalar_prefetch=1,
        grid=(batch, RET_HEADS, 2 * nc),
        in_specs=[pl.BlockSpec((None, c, RET_QK), lambda b, h, i, d: (b, chunk(i), h)),
                  pl.BlockSpec((None, c, RET_QK), lambda b, h, i, d: (b, chunk(i), RET_HEADS + h)),
                  pl.BlockSpec((None, c, RET_V), lambda b, h, i, d: (b, chunk(i), RET_HEADS + h)),
                  pl.BlockSpec((None, c, RET_V), lambda b, h, i, d: (b, chunk(i), 2 * RET_HEADS + h)),
                  pl.BlockSpec((1, RET_V), lambda b, h, i, d: (0, h))],
        out_specs=pl.BlockSpec((None, c, RET_V), lambda b, h, i, d: (b, jnp.maximum(i - nc, 0), h)),
        scratch_shapes=[pltpu.VMEM((RET_QK, RET_V), F32), pltpu.VMEM((seq, RET_V), F32)],
    )
    out = pl.pallas_call(
        functools.partial(_retention_kernel, nc=nc),
        grid_spec=grid_spec,
        out_shape=jax.ShapeDtypeStruct((batch, seq, RET_HEADS * RET_V), BF16),
        compiler_params=_cparams(("arbitrary", "arbitrary", "arbitrary")),
        name="retention",
    )(decays, rest3, rest3, rest3, rest3, gn_gain)
    return out.reshape(batch * seq, RET_HEADS * RET_V)


MERGE_TM = PERM_TILE


def _merge_kernel(o0_ref, o1_ref, o2_ref, l0_ref, l1_ref, l2_ref, retg_ref, ga_ref, gr_ref, x_ref, mod_ref,
                  wa_ref, wr_ref, wo_ref, ln_ref, wrh_ref, wrl_ref, rb_ref,
                  h1_ref, u2_ref, lg_ref, on_scr, ln_scr):
    for g in (1, 2):
        dil = ATTN_PATTERNS[g][1]
        n = MERGE_TM // dil
        o_ref, l_ref = ((o1_ref, l1_ref), (o2_ref, l2_ref))[g - 1]
        for res in range(dil):
            rows = o_ref[res * n:(res + 1) * n, :].astype(F32)
            for hh in range(HEADS_PER_GROUP):
                on_scr[g - 1, hh, pl.ds(res, n, stride=dil), :] = rows[:, hh * HEAD_DIM:(hh + 1) * HEAD_DIM]
            ln_scr[g - 1, pl.ds(res, n, stride=dil), :] = l_ref[res * n:(res + 1) * n, :]

    l0, l1, l2 = l0_ref[...], ln_scr[0], ln_scr[1]
    lm = jnp.maximum(jnp.maximum(l0, l1), l2)
    e0, e1, e2 = jnp.exp(l0 - lm), jnp.exp(l1 - lm), jnp.exp(l2 - lm)
    inv = 1.0 / (e0 + e1 + e2)
    parts = []
    for hh in range(HEADS_PER_GROUP):
        sl = slice(hh * HEAD_DIM, (hh + 1) * HEAD_DIM)
        acc = (e0[:, hh:hh + 1] * o0_ref[:, sl].astype(F32)
               + e1[:, hh:hh + 1] * on_scr[0, hh]
               + e2[:, hh:hh + 1] * on_scr[1, hh])
        parts.append((acc * inv[:, hh:hh + 1]).astype(BF16))
    attn = jnp.concatenate(parts, axis=1)

    branch_a = _dot(attn, wa_ref[...])
    branch_r = _dot(retg_ref[...], wr_ref[...])
    merged = (_sigmoid(ga_ref[...].astype(F32)) * branch_a + _sigmoid(gr_ref[...].astype(F32)) * branch_r)
    y = _dot(merged.astype(BF16), wo_ref[...])

    h1 = _ln(DEEPNORM_ALPHA * x_ref[...] + mod_ref[2:3, :] * y) * ln_ref[0:1, :] + ln_ref[1:2, :]
    h1_ref[...] = h1
    u2 = _ln(h1) * (1.0 + mod_ref[4:5, :]) + mod_ref[3:4, :]
    u2_ref[...] = u2
    uh, ul = _split_bf16(u2)
    lg_ref[...] = _dot(uh, wrh_ref[...]) + _dot(uh, wrl_ref[...]) + _dot(ul, wrh_ref[...]) + rb_ref[...]


def _merge(outs, lses, retg, rest, x2, mod3, wa, wr, wo, ln1, wr_hi, wr_lo, rbias, seq):
    t = x2.shape[0]
    tm = MERGE_TM
    per_seq = seq // tm
    row = lambda w: pl.BlockSpec((tm, w), lambda i: (i, 0))
    full = lambda a: pl.BlockSpec(a.shape, lambda i: (0,) * a.ndim)
    return pl.pallas_call(
        _merge_kernel,
        grid=(t // tm,),
        in_specs=[row(GROUP_WIDTH)] * 3 + [row(LANES)] * 3 + [
            row(RET_HEADS * RET_V),
            pl.BlockSpec((tm, D_MODEL), lambda i: (i, 6)),
            pl.BlockSpec((tm, D_MODEL), lambda i: (i, 7)),
            row(D_MODEL),
            pl.BlockSpec((None, 6, D_MODEL), lambda i: (i // per_seq, 0, 0)),
            full(wa), full(wr), full(wo), full(ln1), full(wr_hi), full(wr_lo), full(rbias)],
        out_specs=[row(D_MODEL), row(D_MODEL), row(LANES)],
        out_shape=[jax.ShapeDtypeStruct((t, D_MODEL), F32),
                   jax.ShapeDtypeStruct((t, D_MODEL), F32),
                   jax.ShapeDtypeStruct((t, LANES), F32)],
        scratch_shapes=[pltpu.VMEM((2, HEADS_PER_GROUP, tm, HEAD_DIM), F32), pltpu.VMEM((2, tm, LANES), F32)],
        compiler_params=_cparams(("arbitrary",)),
        name="merge",
    )(*outs, *lses, retg, rest, rest, x2, mod3, wa, wr, wo, ln1, wr_hi, wr_lo, rbias)


ROUTE_TM = 512
BIG = 1 << 20


def _route_kernel(lg_ref, cols_ref, ints_ref, cnt_ref, carry):
    i = pl.program_id(0)

    @pl.when(i == 0)
    def _():
        carry[...] = jnp.zeros_like(carry)

    tm = ROUTE_TM
    lg = lg_ref[...]
    lane = lax.broadcasted_iota(I32, (tm, LANES), 1)
    lane_f = lane.astype(F32)
    first = lambda mask: jnp.min(jnp.where(mask, lane_f, float(BIG)), axis=-1, keepdims=True).astype(I32)

    coarse = jnp.where(lane < N_GROUPS, lg, NEG)
    cmax = jnp.max(coarse, axis=-1, keepdims=True)
    gsel = first(coarse == cmax)
    p_group = 1.0 / jnp.sum(jnp.exp(coarse - cmax), axis=-1, keepdims=True)

    lo = N_GROUPS + EXPERTS_PER_GROUP * gsel
    fine = jnp.where((lane >= lo) & (lane < lo + EXPERTS_PER_GROUP), lg, NEG)
    v1 = jnp.max(fine, axis=-1, keepdims=True)
    i1 = first(fine == v1)
    fine2 = jnp.where(lane == i1, NEG, fine)
    v2 = jnp.max(fine2, axis=-1, keepdims=True)
    i2 = first(fine2 == v2)
    ex = jnp.exp(v2 - v1)
    den = 1.0 / (1.0 + ex)
    gate1 = p_group * den
    gate2 = p_group * (ex * den)
    e1 = i1 - N_GROUPS
    e2 = i2 - N_GROUPS

    oh1 = lane == e1
    oh2 = lane == e2
    cnt = jnp.where(oh1 | oh2, 1.0, 0.0)
    r_i = lax.broadcasted_iota(I32, (tm, tm), 0)
    c_i = lax.broadcasted_iota(I32, (tm, tm), 1)
    tri = jnp.where(r_i > c_i, 1.0, 0.0).astype(BF16)
    rank = _dot(tri, cnt.astype(BF16)) + carry[...]
    r1 = jnp.sum(jnp.where(oh1, rank, 0.0), axis=-1, keepdims=True)
    r2 = jnp.sum(jnp.where(oh2, rank, 0.0), axis=-1, keepdims=True)
    carry[...] = carry[...] + jnp.sum(cnt, axis=0, keepdims=True)
    cnt_ref[...] = jnp.broadcast_to(carry[...], cnt_ref.shape)

    cols_ref[...] = jnp.where(lane == 0, gate1, jnp.where(lane == 1, gate2, 0.0))
    packed = jnp.where(lane == 0, e1.astype(F32),
                       jnp.where(lane == 1, e2.astype(F32),
                                 jnp.where(lane == 2, r1, jnp.where(lane == 3, r2, 0.0))))
    ints_ref[...] = packed.T[0:8, :].astype(I32)


def _route(logits):
    t = logits.shape[0]
    tm = ROUTE_TM
    return pl.pallas_call(
        _route_kernel,
        grid=(t // tm,),
        in_specs=[pl.BlockSpec((tm, LANES), lambda i: (i, 0))],
        out_specs=[pl.BlockSpec((tm, LANES), lambda i: (i, 0)),
                   pl.BlockSpec((8, tm), lambda i: (0, i)),
                   pl.BlockSpec((8, LANES), lambda i: (0, 0))],
        out_shape=[jax.ShapeDtypeStruct((t, LANES), F32),
                   jax.ShapeDtypeStruct((8, t), I32),
                   jax.ShapeDtypeStruct((8, LANES), F32)],
        scratch_shapes=[pltpu.VMEM((1, LANES), F32)],
        compiler_params=_cparams(("arbitrary",)),
        name="route",
    )(logits)


def _plan_kernel(ints_ref, cnt_ref, dest_ref, meta_ref, *, n_blocks_pad):
    sub = lax.broadcasted_iota(I32, (LANES, LANES), 0)
    lane = lax.broadcasted_iota(I32, (LANES, LANES), 1)
    cnt = cnt_ref[0:1, :]
    nblk_row = jnp.floor((cnt + (MOE_BLK - 1.0)) * (1.0 / MOE_BLK))
    nblk_mat = jnp.broadcast_to(nblk_row, (LANES, LANES))
    start_col = jnp.sum(jnp.where(lane < sub, nblk_mat, 0.0), axis=-1, keepdims=True)
    nblk_col = jnp.sum(jnp.where(lane == sub, nblk_mat, 0.0), axis=-1, keepdims=True)
    end_col = start_col + nblk_col

    ints = ints_ref[...]
    base = jnp.zeros(ints.shape, F32)
    for e in range(N_EXPERTS):
        base = jnp.where(ints == e, start_col[e:e + 1, :] * float(MOE_BLK), base)
    dest = base[0:2, :].astype(I32) + ints[2:4, :]
    dest_ref[...] = jnp.concatenate([dest, jnp.zeros((6, ints.shape[1]), I32)], axis=0)

    blk = lax.broadcasted_iota(I32, (LANES, n_blocks_pad), 1).astype(F32)
    e_sub = lax.broadcasted_iota(I32, (LANES, n_blocks_pad), 0)
    done = jnp.where((e_sub < N_EXPERTS) & (end_col <= blk), 1.0, 0.0)
    bexp = jnp.minimum(jnp.sum(done, axis=0, keepdims=True), N_EXPERTS - 1.0)
    used = jnp.sum(nblk_row, axis=-1, keepdims=True)
    row = lax.broadcasted_iota(I32, (8, n_blocks_pad), 0)
    meta = jnp.where(row == 0, bexp, jnp.where(row == 1, used, 0.0))
    meta_ref[...] = meta.astype(I32)


def _plan(ints, counts, n_blocks_pad):
    t = ints.shape[1]
    return pl.pallas_call(
        functools.partial(_plan_kernel, n_blocks_pad=n_blocks_pad),
        out_shape=[jax.ShapeDtypeStruct((8, t), I32), jax.ShapeDtypeStruct((8, n_blocks_pad), I32)],
        compiler_params=pltpu.CompilerParams(vmem_limit_bytes=VMEM_LIMIT),
        name="plan",
    )(ints, counts)


DISPATCH_TM = 512


def _dispatch_kernel(dest_ref, u2_ref, rows_in_ref, rows_ref, sem):
    del rows_in_ref
    def copy(t, d):
        return pltpu.make_async_copy(u2_ref.at[pl.ds(t, 1)], rows_ref.at[pl.ds(d, 1)], sem)

    def issue(t, carry):
        copy(t, dest_ref[0, t]).start()
        copy(t, dest_ref[1, t]).start()
        return carry

    def drain(t, carry):
        copy(t, dest_ref[0, t]).wait()
        copy(t, dest_ref[1, t]).wait()
        return carry

    lax.fori_loop(0, DISPATCH_TM, issue, 0)
    lax.fori_loop(0, DISPATCH_TM, drain, 0)


def _dispatch(dest, u2, n_rows):
    t = u2.shape[0]
    rows0 = jnp.zeros((n_rows, D_MODEL), F32)
    return pl.pallas_call(
        _dispatch_kernel,
        grid=(t // DISPATCH_TM,),
        in_specs=[pl.BlockSpec((8, DISPATCH_TM), lambda i: (0, i), memory_space=pltpu.SMEM),
                  pl.BlockSpec((DISPATCH_TM, D_MODEL), lambda i: (i, 0)),
                  pl.BlockSpec(memory_space=pl.ANY)],
        out_specs=pl.BlockSpec(memory_space=pl.ANY),
        out_shape=jax.ShapeDtypeStruct((n_rows, D_MODEL), F32),
        scratch_shapes=[pltpu.SemaphoreType.DMA(())],
        input_output_aliases={2: 0},
        compiler_params=_cparams(("arbitrary",)),
        name="dispatch",
    )(dest, u2, rows0)


def _experts_kernel(bexp_ref, used_ref, x_ref, w1_ref, w3_ref, w2_ref, y_ref, w1s, w3s, w2s):
    i = pl.program_id(0)
    prev = bexp_ref[jnp.maximum(i - 1, 0)]
    active = i < used_ref[0]

    @pl.when(active & ((i == 0) | (bexp_ref[i] != prev)))
    def _():
        w1s[...] = w1_ref[...].astype(BF16)
        w3s[...] = w3_ref[...].astype(BF16)
        w2s[...] = w2_ref[...].astype(BF16)

    @pl.when(active)
    def _():
        xb = x_ref[...].astype(BF16)
        a = _dot(xb, w1s[...])
        b = _dot(xb, w3s[...])
        hdn = (a * _sigmoid(a) * b).astype(BF16)
        y_ref[...] = _dot(hdn, w2s[...])

    @pl.when(jnp.logical_not(active))
    def _():
        y_ref[...] = jnp.zeros_like(y_ref)


def _experts(bexp, used, rows, w1, w3, w2):
    n_rows = rows.shape[0]
    n_blocks = n_rows // MOE_BLK
    grid_spec = pltpu.PrefetchScalarGridSpec(
        num_scalar_prefetch=2,
        grid=(n_blocks,),
        in_specs=[pl.BlockSpec((MOE_BLK, D_MODEL), lambda i, be, nu: (i, 0)),
                  pl.BlockSpec((None, D_MODEL, EXPERT_FF), lambda i, be, nu: (be[i], 0, 0)),
                  pl.BlockSpec((None, D_MODEL, EXPERT_FF), lambda i, be, nu: (be[i], 0, 0)),
                  pl.BlockSpec((None, EXPERT_FF, D_MODEL), lambda i, be, nu: (be[i], 0, 0))],
        out_specs=pl.BlockSpec((MOE_BLK, D_MODEL), lambda i, be, nu: (i, 0)),
        scratch_shapes=[pltpu.VMEM((D_MODEL, EXPERT_FF), BF16), pltpu.VMEM((D_MODEL, EXPERT_FF), BF16),
                        pltpu.VMEM((EXPERT_FF, D_MODEL), BF16)],
    )
    return pl.pallas_call(
        _experts_kernel,
        grid_spec=grid_spec,
        out_shape=jax.ShapeDtypeStruct((n_rows, D_MODEL), F32),
        compiler_params=_cparams(("arbitrary",)),
        name="experts",
    )(bexp, used, rows, w1, w3, w2)


COMBINE_TM = 256


def _combine_kernel(dest_ref, y_ref, cols_ref, h1_ref, mod_ref, ln_ref, o_ref, ya, yb, sem):
    def copy(d, buf, t):
        return pltpu.make_async_copy(y_ref.at[pl.ds(d, 1)], buf.at[pl.ds(t, 1)], sem)

    def issue(t, carry):
        copy(dest_ref[0, t], ya, t).start()
        copy(dest_ref[1, t], yb, t).start()
        return carry

    def drain(t, carry):
        copy(dest_ref[0, t], ya, t).wait()
        copy(dest_ref[1, t], yb, t).wait()
        return carry

    lax.fori_loop(0, COMBINE_TM, issue, 0)
    lax.fori_loop(0, COMBINE_TM, drain, 0)

    cols = cols_ref[...]
    moe = cols[:, 0:1] * ya[...] + cols[:, 1:2] * yb[...]
    pre = DEEPNORM_ALPHA * h1_ref[...] + mod_ref[5:6, :] * moe
    o_ref[...] = _ln(pre) * ln_ref[0:1, :] + ln_ref[1:2, :]


def _combine(dest, y_rows, cols, h1, mod3, ln2, seq):
    t = h1.shape[0]
    tm = COMBINE_TM
    per_seq = seq // tm
    return pl.pallas_call(
        _combine_kernel,
        grid=(t // tm,),
        in_specs=[pl.BlockSpec((8, tm), lambda i: (0, i), memory_space=pltpu.SMEM),
                  pl.BlockSpec(memory_space=pl.ANY),
                  pl.BlockSpec((tm, LANES), lambda i: (i, 0)),
                  pl.BlockSpec((tm, D_MODEL), lambda i: (i, 0)),
                  pl.BlockSpec((None, 6, D_MODEL), lambda i: (i // per_seq, 0, 0)),
                  pl.BlockSpec((2, D_MODEL), lambda i: (0, 0))],
        out_specs=pl.BlockSpec((tm, D_MODEL), lambda i: (i, 0)),
        out_shape=jax.ShapeDtypeStruct((t, D_MODEL), F32),
        scratch_shapes=[pltpu.VMEM((tm, D_MODEL), F32), pltpu.VMEM((tm, D_MODEL), F32),
                        pltpu.SemaphoreType.DMA(())],
        compiler_params=_cparams(("arbitrary",)),
        name="combine",
    )(dest, y_rows, cols, h1, mod3, ln2)


def _layer(h, c8, w_ada, b_ada, w_in, w_attn_out, decay_f, decay_b, gn_gain, w_ret_out, w_out,
           ln1_gain, ln1_bias, w_coarse, b_coarse, w_fine, b_fine, w1, w3, w2, ln2_gain, ln2_bias):
    batch, seq, d = h.shape
    t = batch * seq
    x2 = h.reshape(t, d)

    mod = _ada(c8, w_ada, b_ada.reshape(1, -1))
    mod3 = mod[:batch].reshape(batch, 6, d)

    aw = ATTN_HEADS * HEAD_DIM
    wq, wk, wv = (w_in[:, s * aw:(s + 1) * aw] for s in range(3))
    gcols = lambda w, g: w[:, g * GROUP_WIDTH:(g + 1) * GROUP_WIDTH]
    w_qkv = jnp.concatenate([gcols(w, g) for g in range(ATTN_GROUPS) for w in (wq, wk, wv)], axis=1).astype(BF16)
    w_rest = w_in[:, 3 * aw:].astype(BF16)

    qkv = _inproj_attn(x2, mod3, w_qkv, seq)
    rest = _inproj_rest(x2, mod3, w_rest, seq)

    outs, lses = zip(*[_attention(qkv, g, batch, seq) for g in range(ATTN_GROUPS)])
    decays = jnp.concatenate([decay_f, decay_b]).astype(F32)
    retg = _retention(rest, decays, gn_gain.reshape(1, -1), batch, seq)

    w_route = jnp.concatenate([w_coarse] + [w_fine[g] for g in range(N_GROUPS)], axis=1)
    n_route = w_route.shape[1]
    w_route = jnp.pad(w_route, ((0, 0), (0, LANES - n_route)))
    wr_hi, wr_lo = _split_bf16(w_route)
    rbias = jnp.pad(jnp.concatenate([b_coarse, b_fine.reshape(-1)]), (0, LANES - n_route)).reshape(1, LANES)
    ln1 = jnp.stack([ln1_gain, ln1_bias])
    h1, u2, logits = _merge(outs, lses, retg, rest, x2, mod3, w_attn_out.astype(BF16), w_ret_out.astype(BF16),
                            w_out.astype(BF16), ln1, wr_hi, wr_lo, rbias, seq)

    cols, ints, counts = _route(logits)
    n_blocks = 2 * t // MOE_BLK + N_EXPERTS
    n_blocks_pad = -(-n_blocks // LANES) * LANES
    dest, meta = _plan(ints, counts, n_blocks_pad)
    rows = _dispatch(dest, u2, n_blocks * MOE_BLK)
    y_rows = _experts(meta[0, :n_blocks], meta[1, :1], rows, w1, w3, w2)
    out = _combine(dest, y_rows, cols, h1, mod3, jnp.stack([ln2_gain, ln2_bias]), seq)
    return out.reshape(batch, seq, d)


def kernel(x, c, w_ada, b_ada, w_in, w_attn_out, ret_decay_fwd, ret_decay_bwd, ret_gn_gain, w_ret_out, w_out,
           ln1_gain, ln1_bias, w_coarse, b_coarse, w_fine, b_fine, w1, w3, w2, ln2_gain, ln2_bias):
    batch = x.shape[0]
    assert batch <= 8 and x.shape[1] % (2 * INPROJ_TM) == 0 and x.shape[2] == D_MODEL
    c8 = jnp.pad(c, ((0, 8 - batch), (0, 0)))
    h = x
    for l in range(w_ada.shape[0]):
        h = _layer(h, c8, w_ada[l], b_ada[l], w_in[l], w_attn_out[l], ret_decay_fwd[l], ret_decay_bwd[l],
                   ret_gn_gain[l], w_ret_out[l], w_out[l], ln1_gain[l], ln1_bias[l], w_coarse[l], b_coarse[l],
                   w_fine[l], b_fine[l], w1[l], w3[l], w2[l], ln2_gain[l], ln2_bias[l])
    return h
```

```python
import functools
import math

import jax
import jax.numpy as jnp
from jax import lax
from jax.experimental import pallas as pl
from jax.experimental.pallas import tpu as pltpu

F32 = jnp.float32
BF16 = jnp.bfloat16
I32 = jnp.int32

D_MODEL = 1024
ATTN_GROUPS = 3
HEADS_PER_GROUP = 4
HEAD_DIM = 128
ATTN_HEADS = ATTN_GROUPS * HEADS_PER_GROUP
GROUP_WIDTH = HEADS_PER_GROUP * HEAD_DIM
ATTN_PATTERNS = ((128, 1), (512, 4), (2048, 16))
ALIBI_MAX_EXP = 8.0
NEG = -1e30
RET_HEADS = 4
RET_QK = 256
RET_V = 512
N_GROUPS = 4
EXPERTS_PER_GROUP = 8
N_EXPERTS = N_GROUPS * EXPERTS_PER_GROUP
EXPERT_FF = 512
DEPTH = 1
DEEPNORM_ALPHA = (2.0 * DEPTH) ** 0.25
LN_EPS = 1e-5

LANES = 128
PERM_TILE = 512
ATTN_QB = 128
ATTN_HALF = 64
ATTN_KB = ATTN_QB + 2 * ATTN_HALF
RET_CHUNK = 256
MOE_BLK = 256
VMEM_LIMIT = 56 * 1024 * 1024


def _cparams(sem):
    return pltpu.CompilerParams(dimension_semantics=sem, vmem_limit_bytes=VMEM_LIMIT)


def _split_bf16(a):
    hi = a.astype(BF16)
    lo = (a - hi.astype(F32)).astype(BF16)
    return hi, lo


def _dot(a, b):
    return jnp.dot(a, b, preferred_element_type=F32)


def _dot3(a, b):
    ah, al = _split_bf16(a)
    bh, bl = _split_bf16(b)
    return _dot(ah, bh) + _dot(ah, bl) + _dot(al, bh)


def _ln(x):
    mu = jnp.mean(x, axis=-1, keepdims=True)
    xc = x - mu
    var = jnp.mean(xc * xc, axis=-1, keepdims=True)
    return xc * lax.rsqrt(var + LN_EPS)


def _sigmoid(x):
    return 1.0 / (1.0 + jnp.exp(-x))


def _ada_kernel(c_ref, w_ref, b_ref, o_ref):
    o_ref[...] = _dot3(c_ref[...], w_ref[...]) + b_ref[...]


def _ada(c8, w_ada, b_ada):
    n = w_ada.shape[1]
    return pl.pallas_call(
        _ada_kernel,
        grid=(n // D_MODEL,),
        in_specs=[pl.BlockSpec((8, D_MODEL), lambda j: (0, 0)),
                  pl.BlockSpec((D_MODEL, D_MODEL), lambda j: (0, j)),
                  pl.BlockSpec((1, D_MODEL), lambda j: (0, j))],
        out_specs=pl.BlockSpec((8, D_MODEL), lambda j: (0, j)),
        out_shape=jax.ShapeDtypeStruct((8, n), F32),
        compiler_params=_cparams(("arbitrary",)),
        name="ada",
    )(c8, w_ada, b_ada)


INPROJ_TM = 1024
LN_ROWS = 256


def _modulated_ln(x_ref, mod_ref, r0, rows):
    x = x_ref[r0:r0 + rows, :]
    return _ln(x) * (1.0 + mod_ref[1:2, :]) + mod_ref[0:1, :]


def _inproj_attn_kernel(x_ref, mod_ref, w_ref, o_ref, uf_scr, u_scr):
    j = pl.program_id(1)

    @pl.when(j == 0)
    def _():
        for r0 in range(0, INPROJ_TM, LN_ROWS):
            u = _modulated_ln(x_ref, mod_ref, r0, LN_ROWS)
            for cc in range(D_MODEL // LANES):
                uf_scr[cc, r0:r0 + LN_ROWS, :] = u[:, cc * LANES:(cc + 1) * LANES]
            u_scr[0, r0:r0 + LN_ROWS, :] = u.astype(BF16)
        for g in (1, 2):
            dil = ATTN_PATTERNS[g][1]
            n = PERM_TILE // dil
            for t0 in range(0, INPROJ_TM, PERM_TILE):
                for res in range(dil):
                    for cc in range(D_MODEL // LANES):
                        rows = uf_scr[cc, pl.ds(t0 + res, n, stride=dil), :]
                        u_scr[g, t0 + res * n:t0 + (res + 1) * n, cc * LANES:(cc + 1) * LANES] = rows.astype(BF16)

    o_ref[...] = _dot(u_scr[j], w_ref[...]).astype(BF16)


def _inproj_attn(x2, mod3, w_qkv, seq):
    t = x2.shape[0]
    tn = 3 * GROUP_WIDTH
    per_seq = seq // INPROJ_TM
    return pl.pallas_call(
        _inproj_attn_kernel,
        grid=(t // INPROJ_TM, ATTN_GROUPS),
        in_specs=[pl.BlockSpec((INPROJ_TM, D_MODEL), lambda i, j: (i, 0)),
                  pl.BlockSpec((None, 6, D_MODEL), lambda i, j: (i // per_seq, 0, 0)),
                  pl.BlockSpec((D_MODEL, tn), lambda i, j: (0, j))],
        out_specs=pl.BlockSpec((INPROJ_TM, tn), lambda i, j: (i, j)),
        out_shape=jax.ShapeDtypeStruct((t, ATTN_GROUPS * tn), BF16),
        scratch_shapes=[pltpu.VMEM((D_MODEL // LANES, INPROJ_TM, LANES), F32),
                        pltpu.VMEM((ATTN_GROUPS, INPROJ_TM, D_MODEL), BF16)],
        compiler_params=_cparams(("arbitrary", "arbitrary")),
        name="inproj_attn",
    )(x2, mod3, w_qkv)


def _inproj_rest_kernel(x_ref, mod_ref, w_ref, o_ref, u_scr):
    @pl.when(pl.program_id(1) == 0)
    def _():
        for r0 in range(0, INPROJ_TM, LN_ROWS):
            u_scr[r0:r0 + LN_ROWS, :] = _modulated_ln(x_ref, mod_ref, r0, LN_ROWS).astype(BF16)

    o_ref[...] = _dot(u_scr[...], w_ref[...]).astype(BF16)


def _inproj_rest(x2, mod3, w_rest, seq):
    t = x2.shape[0]
    n = w_rest.shape[1]
    tn = 1024
    per_seq = seq // INPROJ_TM
    return pl.pallas_call(
        _inproj_rest_kernel,
        grid=(t // INPROJ_TM, n // tn),
        in_specs=[pl.BlockSpec((INPROJ_TM, D_MODEL), lambda i, j: (i, 0)),
                  pl.BlockSpec((None, 6, D_MODEL), lambda i, j: (i // per_seq, 0, 0)),
                  pl.BlockSpec((D_MODEL, tn), lambda i, j: (0, j))],
        out_specs=pl.BlockSpec((INPROJ_TM, tn), lambda i, j: (i, j)),
        out_shape=jax.ShapeDtypeStruct((t, n), BF16),
        scratch_shapes=[pltpu.VMEM((INPROJ_TM, D_MODEL), BF16)],
        compiler_params=_cparams(("arbitrary", "arbitrary")),
        name="inproj_rest",
    )(x2, mod3, w_rest)


ATTN_OFFSETS = ATTN_KB // ATTN_HALF - 1


def _attn_kernel(q_ref, k_ref, v_ref, o_ref, lse_ref, bias_scr, *scratch, group, dil, nt, n):
    n_sub = nt * n
    if nt == 1:
        qs, ks, vs, os_, ls = q_ref.at[0], k_ref.at[0], v_ref.at[0], o_ref.at[0], lse_ref.at[0]
    else:
        qs, ks, vs, os_, ls = scratch
        for t in range(nt):
            qs[t * n:(t + 1) * n, :] = q_ref[t]
            ks[t * n:(t + 1) * n, :] = k_ref[t]
            vs[t * n:(t + 1) * n, :] = v_ref[t]

    @pl.when((pl.program_id(0) == 0) & (pl.program_id(1) == 0))
    def _():
        base = (lax.broadcasted_iota(I32, (ATTN_QB, ATTN_KB), 1)
                - lax.broadcasted_iota(I32, (ATTN_QB, ATTN_KB), 0))
        for j in range(ATTN_OFFSETS):
            dist = jnp.abs(base - j * ATTN_HALF)
            for hh in range(HEADS_PER_GROUP):
                slope = dil * 2.0 ** (-ALIBI_MAX_EXP * (group * HEADS_PER_GROUP + hh + 1) / ATTN_HEADS)
                bias_scr[hh * ATTN_OFFSETS + j] = jnp.where(dist <= ATTN_HALF, -slope * dist.astype(F32), NEG)

    scale = HEAD_DIM ** -0.5
    lane = lax.broadcasted_iota(I32, (ATTN_QB, LANES), 1)

    def block(blk, carry):
        q0 = pl.multiple_of(blk * ATTN_QB, ATTN_QB)
        start = pl.multiple_of(jnp.clip(q0 - ATTN_HALF, 0, n_sub - ATTN_KB), ATTN_HALF)
        j = (q0 - start) // ATTN_HALF
        lse_tile = jnp.zeros((ATTN_QB, LANES), F32)
        for hh in range(HEADS_PER_GROUP):
            cs = slice(hh * HEAD_DIM, (hh + 1) * HEAD_DIM)
            qb = qs[pl.ds(q0, ATTN_QB), cs]
            kb = ks[pl.ds(start, ATTN_KB), cs]
            vb = vs[pl.ds(start, ATTN_KB), cs]
            s = lax.dot_general(qb, kb, (((1,), (1,)), ((), ())), preferred_element_type=F32) * scale
            s = s + bias_scr[hh * ATTN_OFFSETS + j]
            m = jnp.max(s, axis=-1, keepdims=True)
            p = jnp.exp(s - m)
            l = jnp.sum(p, axis=-1, keepdims=True)
            o = _dot(p.astype(BF16), vb) * (1.0 / l)
            os_[pl.ds(q0, ATTN_QB), cs] = o.astype(BF16)
            lse_tile = jnp.where(lane == hh, m + jnp.log(l), lse_tile)
        ls[pl.ds(q0, ATTN_QB), :] = lse_tile
        return carry

    lax.fori_loop(0, n_sub // ATTN_QB, block, 0)

    if nt > 1:
        for t in range(nt):
            o_ref[t] = os_[t * n:(t + 1) * n, :]
            lse_ref[t] = ls[t * n:(t + 1) * n, :]


def _attention(qkv, group, batch, seq):
    dil = ATTN_PATTERNS[group][1]
    if dil == 1:
        nt, n = 1, seq
    else:
        nt, n = seq // PERM_TILE, PERM_TILE // dil
    n_sub = nt * n
    t = batch * seq
    qkv5 = qkv.reshape(batch, nt, dil, n, qkv.shape[1])
    cb = group * 3
    blk = (None, nt, None, n, GROUP_WIDTH)
    scratch = [pltpu.VMEM((HEADS_PER_GROUP * ATTN_OFFSETS, ATTN_QB, ATTN_KB), F32)]
    if nt > 1:
        scratch += [pltpu.VMEM((n_sub, GROUP_WIDTH), BF16)] * 4 + [pltpu.VMEM((n_sub, LANES), F32)]
    out, lse = pl.pallas_call(
        functools.partial(_attn_kernel, group=group, dil=dil, nt=nt, n=n),
        grid=(batch, dil),
        in_specs=[pl.BlockSpec(blk, lambda b, r: (b, 0, r, 0, cb)),
                  pl.BlockSpec(blk, lambda b, r: (b, 0, r, 0, cb + 1)),
                  pl.BlockSpec(blk, lambda b, r: (b, 0, r, 0, cb + 2))],
        out_specs=[pl.BlockSpec(blk, lambda b, r: (b, 0, r, 0, 0)),
                   pl.BlockSpec((None, nt, None, n, LANES), lambda b, r: (b, 0, r, 0, 0))],
        out_shape=[jax.ShapeDtypeStruct((batch, nt, dil, n, GROUP_WIDTH), BF16),
                   jax.ShapeDtypeStruct((batch, nt, dil, n, LANES), F32)],
        scratch_shapes=scratch,
        compiler_params=_cparams(("arbitrary", "arbitrary")),
        name=f"attn_g{group}",
    )(qkv5, qkv5, qkv5)
    return out.reshape(t, GROUP_WIDTH), lse.reshape(t, LANES)


def _log_sigmoid(x):
    return jnp.minimum(x, 0.0) - jnp.log(1.0 + jnp.exp(-jnp.abs(x)))


def _retention_kernel(decay_ref, q_ref, k_ref, v_ref, g_ref, gain_ref, o_ref, state, ybwd, *, nc):
    c = RET_CHUNK
    h = pl.program_id(1)
    i = pl.program_id(2)
    kscale = RET_QK ** -0.5

    def chunk_step(lg, forward):
        q = q_ref[...]
        k = k_ref[...]
        v = v_ref[...]
        row = lax.broadcasted_iota(I32, (c, c), 0)
        col = lax.broadcasted_iota(I32, (c, c), 1)
        pos = lax.broadcasted_iota(I32, (c, 1), 0).astype(F32)
        if forward:
            gap = row - col
            keep = gap >= 0
            key_pow = (c - 1.0) - pos
            query_pow = pos + 1.0
        else:
            gap = col - row
            keep = gap > 0
            key_pow = pos
            query_pow = c - pos
        decay = jnp.where(keep, jnp.exp(lg * jnp.maximum(gap, 0).astype(F32)) * kscale, 0.0)
        inner = lax.dot_general(q, k, (((1,), (1,)), ((), ())), preferred_element_type=F32) * decay
        y = _dot(inner.astype(BF16), v)
        y = y + _dot(q, state[...].astype(BF16)) * jnp.exp(lg * query_pow)
        kd = (k.astype(F32) * (jnp.exp(lg * key_pow) * kscale)).astype(BF16)
        kv = lax.dot_general(kd, v, (((0,), (0,)), ((), ())), preferred_element_type=F32)
        state[...] = state[...] * jnp.exp(lg * float(c)) + kv
        return y

    @pl.when((i == 0) | (i == nc))
    def _():
        state[...] = jnp.zeros_like(state)

    @pl.when(i < nc)
    def _():
        lg = _log_sigmoid(jnp.zeros((1, 1), F32) + decay_ref[RET_HEADS + h])
        ci = nc - 1 - i
        ybwd[pl.ds(pl.multiple_of(ci * c, c), c), :] = chunk_step(lg, False)

    @pl.when(i >= nc)
    def _():
        lg = _log_sigmoid(jnp.zeros((1, 1), F32) + decay_ref[h])
        ci = i - nc
        y = chunk_step(lg, True) + ybwd[pl.ds(pl.multiple_of(ci * c, c), c), :]
        g = g_ref[...].astype(F32)
        o_ref[...] = (g * _sigmoid(g) * (_ln(y) * gain_ref[...])).astype(BF16)


def _retention(rest, decays, gn_gain, batch, seq):
    c = RET_CHUNK
    nc = seq // c
    rest3 = rest.reshape(batch, seq, rest.shape[1])

    def chunk(i):
        return jnp.where(i < nc, nc - 1 - i, i - nc)

    grid_spec = pltpu.PrefetchScalarGridSpec(
        num_scalar_prefetch=1,
        grid=(batch, RET_HEADS, 2 * nc),
        in_specs=[pl.BlockSpec((None, c, RET_QK), lambda b, h, i, d: (b, chunk(i), h)),
                  pl.BlockSpec((None, c, RET_QK), lambda b, h, i, d: (b, chunk(i), RET_HEADS + h)),
                  pl.BlockSpec((None, c, RET_V), lambda b, h, i, d: (b, chunk(i), RET_HEADS + h)),
                  pl.BlockSpec((None, c, RET_V), lambda b, h, i, d: (b, chunk(i), 2 * RET_HEADS + h)),
                  pl.BlockSpec((1, RET_V), lambda b, h, i, d: (0, h))],
        out_specs=pl.BlockSpec((None, c, RET_V), lambda b, h, i, d: (b, jnp.maximum(i - nc, 0), h)),
        scratch_shapes=[pltpu.VMEM((RET_QK, RET_V), F32), pltpu.VMEM((seq, RET_V), F32)],
    )
    out = pl.pallas_call(
        functools.partial(_retention_kernel, nc=nc),
        grid_spec=grid_spec,
        out_shape=jax.ShapeDtypeStruct((batch, seq, RET_HEADS * RET_V), BF16),
        compiler_params=_cparams(("arbitrary", "arbitrary", "arbitrary")),
        name="retention",
    )(decays, rest3, rest3, rest3, rest3, gn_gain)
    return out.reshape(batch * seq, RET_HEADS * RET_V)


MERGE_TM = PERM_TILE


def _merge_kernel(o0_ref, o1_ref, o2_ref, l0_ref, l1_ref, l2_ref, retg_ref, ga_ref, gr_ref, x_ref, mod_ref,
                  wa_ref, wr_ref, wo_ref, ln_ref, wrh_ref, wrl_ref, rb_ref,
                  h1_ref, u2_ref, lg_ref, on_scr, ln_scr):
    for g in (1, 2):
        dil = ATTN_PATTERNS[g][1]
        n = MERGE_TM // dil
        o_ref, l_ref = ((o1_ref, l1_ref), (o2_ref, l2_ref))[g - 1]
        for res in range(dil):
            rows = o_ref[res * n:(res + 1) * n, :].astype(F32)
            for hh in range(HEADS_PER_GROUP):
                on_scr[g - 1, hh, pl.ds(res, n, stride=dil), :] = rows[:, hh * HEAD_DIM:(hh + 1) * HEAD_DIM]
            ln_scr[g - 1, pl.ds(res, n, stride=dil), :] = l_ref[res * n:(res + 1) * n, :]

    l0, l1, l2 = l0_ref[...], ln_scr[0], ln_scr[1]
    lm = jnp.maximum(jnp.maximum(l0, l1), l2)
    e0, e1, e2 = jnp.exp(l0 - lm), jnp.exp(l1 - lm), jnp.exp(l2 - lm)
    inv = 1.0 / (e0 + e1 + e2)
    parts = []
    for hh in range(HEADS_PER_GROUP):
        sl = slice(hh * HEAD_DIM, (hh + 1) * HEAD_DIM)
        acc = (e0[:, hh:hh + 1] * o0_ref[:, sl].astype(F32)
               + e1[:, hh:hh + 1] * on_scr[0, hh]
               + e2[:, hh:hh + 1] * on_scr[1, hh])
        parts.append((acc * inv[:, hh:hh + 1]).astype(BF16))
    attn = jnp.concatenate(parts, axis=1)

    branch_a = _dot(attn, wa_ref[...])
    branch_r = _dot(retg_ref[...], wr_ref[...])
    merged = (_sigmoid(ga_ref[...].astype(F32)) * branch_a + _sigmoid(gr_ref[...].astype(F32)) * branch_r)
    y = _dot(merged.astype(BF16), wo_ref[...])

    h1 = _ln(DEEPNORM_ALPHA * x_ref[...] + mod_ref[2:3, :] * y) * ln_ref[0:1, :] + ln_ref[1:2, :]
    h1_ref[...] = h1
    u2 = _ln(h1) * (1.0 + mod_ref[4:5, :]) + mod_ref[3:4, :]
    u2_ref[...] = u2
    uh, ul = _split_bf16(u2)
    lg_ref[...] = _dot(uh, wrh_ref[...]) + _dot(uh, wrl_ref[...]) + _dot(ul, wrh_ref[...]) + rb_ref[...]


def _merge(outs, lses, retg, rest, x2, mod3, wa, wr, wo, ln1, wr_hi, wr_lo, rbias, seq):
    t = x2.shape[0]
    tm = MERGE_TM
    per_seq = seq // tm
    row = lambda w: pl.BlockSpec((tm, w), lambda i: (i, 0))
    full = lambda a: pl.BlockSpec(a.shape, lambda i: (0,) * a.ndim)
    return pl.pallas_call(
        _merge_kernel,
        grid=(t // tm,),
        in_specs=[row(GROUP_WIDTH)] * 3 + [row(LANES)] * 3 + [
            row(RET_HEADS * RET_V),
            pl.BlockSpec((tm, D_MODEL), lambda i: (i, 6)),
            pl.BlockSpec((tm, D_MODEL), lambda i: (i, 7)),
            row(D_MODEL),
            pl.BlockSpec((None, 6, D_MODEL), lambda i: (i // per_seq, 0, 0)),
            full(wa), full(wr), full(wo), full(ln1), full(wr_hi), full(wr_lo), full(rbias)],
        out_specs=[row(D_MODEL), row(D_MODEL), row(LANES)],
        out_shape=[jax.ShapeDtypeStruct((t, D_MODEL), F32),
                   jax.ShapeDtypeStruct((t, D_MODEL), F32),
                   jax.ShapeDtypeStruct((t, LANES), F32)],
        scratch_shapes=[pltpu.VMEM((2, HEADS_PER_GROUP, tm, HEAD_DIM), F32), pltpu.VMEM((2, tm, LANES), F32)],
        compiler_params=_cparams(("arbitrary",)),
        name="merge",
    )(*outs, *lses, retg, rest, rest, x2, mod3, wa, wr, wo, ln1, wr_hi, wr_lo, rbias)


ROUTE_TM = 512
BIG = 1 << 20


def _route_kernel(lg_ref, cols_ref, ints_ref, cnt_ref, carry):
    i = pl.program_id(0)

    @pl.when(i == 0)
    def _():
        carry[...] = jnp.zeros_like(carry)

    tm = ROUTE_TM
    lg = lg_ref[...]
    lane = lax.broadcasted_iota(I32, (tm, LANES), 1)
    lane_f = lane.astype(F32)
    first = lambda mask: jnp.min(jnp.where(mask, lane_f, float(BIG)), axis=-1, keepdims=True).astype(I32)

    coarse = jnp.where(lane < N_GROUPS, lg, NEG)
    cmax = jnp.max(coarse, axis=-1, keepdims=True)
    gsel = first(coarse == cmax)
    p_group = 1.0 / jnp.sum(jnp.exp(coarse - cmax), axis=-1, keepdims=True)

    lo = N_GROUPS + EXPERTS_PER_GROUP * gsel
    fine = jnp.where((lane >= lo) & (lane < lo + EXPERTS_PER_GROUP), lg, NEG)
    v1 = jnp.max(fine, axis=-1, keepdims=True)
    i1 = first(fine == v1)
    fine2 = jnp.where(lane == i1, NEG, fine)
    v2 = jnp.max(fine2, axis=-1, keepdims=True)
    i2 = first(fine2 == v2)
    ex = jnp.exp(v2 - v1)
    den = 1.0 / (1.0 + ex)
    gate1 = p_group * den
    gate2 = p_group * (ex * den)
    e1 = i1 - N_GROUPS
    e2 = i2 - N_GROUPS

    oh1 = lane == e1
    oh2 = lane == e2
    cnt = jnp.where(oh1 | oh2, 1.0, 0.0)
    r_i = lax.broadcasted_iota(I32, (tm, tm), 0)
    c_i = lax.broadcasted_iota(I32, (tm, tm), 1)
    tri = jnp.where(r_i > c_i, 1.0, 0.0).astype(BF16)
    rank = _dot(tri, cnt.astype(BF16)) + carry[...]
    r1 = jnp.sum(jnp.where(oh1, rank, 0.0), axis=-1, keepdims=True)
    r2 = jnp.sum(jnp.where(oh2, rank, 0.0), axis=-1, keepdims=True)
    carry[...] = carry[...] + jnp.sum(cnt, axis=0, keepdims=True)
    cnt_ref[...] = jnp.broadcast_to(carry[...], cnt_ref.shape)

    cols_ref[...] = jnp.where(lane == 0, gate1, jnp.where(lane == 1, gate2, 0.0))
    packed = jnp.where(lane == 0, e1.astype(F32),
                       jnp.where(lane == 1, e2.astype(F32),
                                 jnp.where(lane == 2, r1, jnp.where(lane == 3, r2, 0.0))))
    ints_ref[...] = packed.T[0:8, :].astype(I32)


def _route(logits):
    t = logits.shape[0]
    tm = ROUTE_TM
    return pl.pallas_call(
        _route_kernel,
        grid=(t // tm,),
        in_specs=[pl.BlockSpec((tm, LANES), lambda i: (i, 0))],
        out_specs=[pl.BlockSpec((tm, LANES), lambda i: (i, 0)),
                   pl.BlockSpec((8, tm), lambda i: (0, i)),
                   pl.BlockSpec((8, LANES), lambda i: (0, 0))],
        out_shape=[jax.ShapeDtypeStruct((t, LANES), F32),
                   jax.ShapeDtypeStruct((8, t), I32),
                   jax.ShapeDtypeStruct((8, LANES), F32)],
        scratch_shapes=[pltpu.VMEM((1, LANES), F32)],
        compiler_params=_cparams(("arbitrary",)),
        name="route",
    )(logits)


def _plan_kernel(ints_ref, cnt_ref, dest_ref, meta_ref, *, n_blocks_pad):
    sub = lax.broadcasted_iota(I32, (LANES, LANES), 0)
    lane = lax.broadcasted_iota(I32, (LANES, LANES), 1)
    cnt = cnt_ref[0:1, :]
    nblk_row = jnp.floor((cnt + (MOE_BLK - 1.0)) * (1.0 / MOE_BLK))
    nblk_mat = jnp.broadcast_to(nblk_row, (LANES, LANES))
    start_col = jnp.sum(jnp.where(lane < sub, nblk_mat, 0.0), axis=-1, keepdims=True)
    nblk_col = jnp.sum(jnp.where(lane == sub, nblk_mat, 0.0), axis=-1, keepdims=True)
    end_col = start_col + nblk_col

    ints = ints_ref[...]
    base = jnp.zeros(ints.shape, F32)
    for e in range(N_EXPERTS):
        base = jnp.where(ints == e, start_col[e:e + 1, :] * float(MOE_BLK), base)
    dest = base[0:2, :].astype(I32) + ints[2:4, :]
    dest_ref[...] = jnp.concatenate([dest, jnp.zeros((6, ints.shape[1]), I32)], axis=0)

    blk = lax.broadcasted_iota(I32, (LANES, n_blocks_pad), 1).astype(F32)
    e_sub = lax.broadcasted_iota(I32, (LANES, n_blocks_pad), 0)
    done = jnp.where((e_sub < N_EXPERTS) & (end_col <= blk), 1.0, 0.0)
    bexp = jnp.minimum(jnp.sum(done, axis=0, keepdims=True), N_EXPERTS - 1.0)
    used = jnp.sum(nblk_row, axis=-1, keepdims=True)
    row = lax.broadcasted_iota(I32, (8, n_blocks_pad), 0)
    meta = jnp.where(row == 0, bexp, jnp.where(row == 1, used, 0.0))
    meta_ref[...] = meta.astype(I32)


def _plan(ints, counts, n_blocks_pad):
    t = ints.shape[1]
    return pl.pallas_call(
        functools.partial(_plan_kernel, n_blocks_pad=n_blocks_pad),
        out_shape=[jax.ShapeDtypeStruct((8, t), I32), jax.ShapeDtypeStruct((8, n_blocks_pad), I32)],
        compiler_params=pltpu.CompilerParams(vmem_limit_bytes=VMEM_LIMIT),
        name="plan",
    )(ints, counts)


DISPATCH_TM = 512
DMA_UNROLL = 8


def _dispatch_kernel(dest_ref, u2_ref, rows_in_ref, rows_ref, sem):
    del rows_in_ref
    def copy(t, d):
        return pltpu.make_async_copy(u2_ref.at[pl.ds(t, 1)], rows_ref.at[pl.ds(d, 1)], sem)

    def issue(t, carry):
        copy(t, dest_ref[0, t]).start()
        copy(t, dest_ref[1, t]).start()
        return carry

    lax.fori_loop(0, DISPATCH_TM, issue, 0, unroll=DMA_UNROLL)
    for _ in range(2):
        pltpu.make_async_copy(u2_ref, rows_ref.at[pl.ds(0, DISPATCH_TM)], sem).wait()


def _dispatch(dest, u2, n_rows):
    t = u2.shape[0]
    rows0 = jnp.zeros((n_rows, D_MODEL), F32)
    return pl.pallas_call(
        _dispatch_kernel,
        grid=(t // DISPATCH_TM,),
        in_specs=[pl.BlockSpec((8, DISPATCH_TM), lambda i: (0, i), memory_space=pltpu.SMEM),
                  pl.BlockSpec((DISPATCH_TM, D_MODEL), lambda i: (i, 0)),
                  pl.BlockSpec(memory_space=pl.ANY)],
        out_specs=pl.BlockSpec(memory_space=pl.ANY),
        out_shape=jax.ShapeDtypeStruct((n_rows, D_MODEL), F32),
        scratch_shapes=[pltpu.SemaphoreType.DMA(())],
        input_output_aliases={2: 0},
        compiler_params=_cparams(("arbitrary",)),
        name="dispatch",
    )(dest, u2, rows0)


def _experts_kernel(bexp_ref, used_ref, x_ref, w1_ref, w3_ref, w2_ref, y_ref, w1s, w3s, w2s):
    i = pl.program_id(0)
    prev = bexp_ref[jnp.maximum(i - 1, 0)]
    active = i < used_ref[0]

    @pl.when(active & ((i == 0) | (bexp_ref[i] != prev)))
    def _():
        w1s[...] = w1_ref[...].astype(BF16)
        w3s[...] = w3_ref[...].astype(BF16)
        w2s[...] = w2_ref[...].astype(BF16)

    @pl.when(active)
    def _():
        xb = x_ref[...].astype(BF16)
        a = _dot(xb, w1s[...])
        b = _dot(xb, w3s[...])
        hdn = (a * _sigmoid(a) * b).astype(BF16)
        y_ref[...] = _dot(hdn, w2s[...])

    @pl.when(jnp.logical_not(active))
    def _():
        y_ref[...] = jnp.zeros_like(y_ref)


def _experts(bexp, used, rows, w1, w3, w2):
    n_rows = rows.shape[0]
    n_blocks = n_rows // MOE_BLK
    grid_spec = pltpu.PrefetchScalarGridSpec(
        num_scalar_prefetch=2,
        grid=(n_blocks,),
        in_specs=[pl.BlockSpec((MOE_BLK, D_MODEL), lambda i, be, nu: (i, 0)),
                  pl.BlockSpec((None, D_MODEL, EXPERT_FF), lambda i, be, nu: (be[i], 0, 0)),
                  pl.BlockSpec((None, D_MODEL, EXPERT_FF), lambda i, be, nu: (be[i], 0, 0)),
                  pl.BlockSpec((None, EXPERT_FF, D_MODEL), lambda i, be, nu: (be[i], 0, 0))],
        out_specs=pl.BlockSpec((MOE_BLK, D_MODEL), lambda i, be, nu: (i, 0)),
        scratch_shapes=[pltpu.VMEM((D_MODEL, EXPERT_FF), BF16), pltpu.VMEM((D_MODEL, EXPERT_FF), BF16),
                        pltpu.VMEM((EXPERT_FF, D_MODEL), BF16)],
    )
    return pl.pallas_call(
        _experts_kernel,
        grid_spec=grid_spec,
        out_shape=jax.ShapeDtypeStruct((n_rows, D_MODEL), F32),
        compiler_params=_cparams(("arbitrary",)),
        name="experts",
    )(bexp, used, rows, w1, w3, w2)


COMBINE_TM = 256


def _combine_kernel(dest_ref, y_ref, cols_ref, h1_ref, mod_ref, ln_ref, o_ref, ya, yb, sem):
    def copy(d, buf, t):
        return pltpu.make_async_copy(y_ref.at[pl.ds(d, 1)], buf.at[pl.ds(t, 1)], sem)

    def issue(t, carry):
        copy(dest_ref[0, t], ya, t).start()
        copy(dest_ref[1, t], yb, t).start()
        return carry

    lax.fori_loop(0, COMBINE_TM, issue, 0, unroll=DMA_UNROLL)
    for buf in (ya, yb):
        pltpu.make_async_copy(y_ref.at[pl.ds(0, COMBINE_TM)], buf, sem).wait()

    cols = cols_ref[...]
    moe = cols[:, 0:1] * ya[...] + cols[:, 1:2] * yb[...]
    pre = DEEPNORM_ALPHA * h1_ref[...] + mod_ref[5:6, :] * moe
    o_ref[...] = _ln(pre) * ln_ref[0:1, :] + ln_ref[1:2, :]


def _combine(dest, y_rows, cols, h1, mod3, ln2, seq):
    t = h1.shape[0]
    tm = COMBINE_TM
    per_seq = seq // tm
    return pl.pallas_call(
        _combine_kernel,
        grid=(t // tm,),
        in_specs=[pl.BlockSpec((8, tm), lambda i: (0, i), memory_space=pltpu.SMEM),
                  pl.BlockSpec(memory_space=pl.ANY),
                  pl.BlockSpec((tm, LANES), lambda i: (i, 0)),
                  pl.BlockSpec((tm, D_MODEL), lambda i: (i, 0)),
                  pl.BlockSpec((None, 6, D_MODEL), lambda i: (i // per_seq, 0, 0)),
                  pl.BlockSpec((2, D_MODEL), lambda i: (0, 0))],
        out_specs=pl.BlockSpec((tm, D_MODEL), lambda i: (i, 0)),
        out_shape=jax.ShapeDtypeStruct((t, D_MODEL), F32),
        scratch_shapes=[pltpu.VMEM((tm, D_MODEL), F32), pltpu.VMEM((tm, D_MODEL), F32),
                        pltpu.SemaphoreType.DMA(())],
        compiler_params=_cparams(("arbitrary",)),
        name="combine",
    )(dest, y_rows, cols, h1, mod3, ln2)


def _layer(h, c8, w_ada, b_ada, w_in, w_attn_out, decay_f, decay_b, gn_gain, w_ret_out, w_out,
           ln1_gain, ln1_bias, w_coarse, b_coarse, w_fine, b_fine, w1, w3, w2, ln2_gain, ln2_bias):
    batch, seq, d = h.shape
    t = batch * seq
    x2 = h.reshape(t, d)

    mod = _ada(c8, w_ada, b_ada.reshape(1, -1))
    mod3 = mod[:batch].reshape(batch, 6, d)

    aw = ATTN_HEADS * HEAD_DIM
    wq, wk, wv = (w_in[:, s * aw:(s + 1) * aw] for s in range(3))
    gcols = lambda w, g: w[:, g * GROUP_WIDTH:(g + 1) * GROUP_WIDTH]
    w_qkv = jnp.concatenate([gcols(w, g) for g in range(ATTN_GROUPS) for w in (wq, wk, wv)], axis=1).astype(BF16)
    w_rest = w_in[:, 3 * aw:].astype(BF16)

    qkv = _inproj_attn(x2, mod3, w_qkv, seq)
    rest = _inproj_rest(x2, mod3, w_rest, seq)

    outs, lses = zip(*[_attention(qkv, g, batch, seq) for g in range(ATTN_GROUPS)])
    decays = jnp.concatenate([decay_f, decay_b]).astype(F32)
    retg = _retention(rest, decays, gn_gain.reshape(1, -1), batch, seq)

    w_route = jnp.concatenate([w_coarse] + [w_fine[g] for g in range(N_GROUPS)], axis=1)
    n_route = w_route.shape[1]
    w_route = jnp.pad(w_route, ((0, 0), (0, LANES - n_route)))
    wr_hi, wr_lo = _split_bf16(w_route)
    rbias = jnp.pad(jnp.concatenate([b_coarse, b_fine.reshape(-1)]), (0, LANES - n_route)).reshape(1, LANES)
    ln1 = jnp.stack([ln1_gain, ln1_bias])
    h1, u2, logits = _merge(outs, lses, retg, rest, x2, mod3, w_attn_out.astype(BF16), w_ret_out.astype(BF16),
                            w_out.astype(BF16), ln1, wr_hi, wr_lo, rbias, seq)

    cols, ints, counts = _route(logits)
    n_blocks = 2 * t // MOE_BLK + N_EXPERTS
    n_blocks_pad = -(-n_blocks // LANES) * LANES
    dest, meta = _plan(ints, counts, n_blocks_pad)
    rows = _dispatch(dest, u2, n_blocks * MOE_BLK)
    y_rows = _experts(meta[0, :n_blocks], meta[1, :1], rows, w1, w3, w2)
    out = _combine(dest, y_rows, cols, h1, mod3, jnp.stack([ln2_gain, ln2_bias]), seq)
    return out.reshape(batch, seq, d)


def kernel(x, c, w_ada, b_ada, w_in, w_attn_out, ret_decay_fwd, ret_decay_bwd, ret_gn_gain, w_ret_out, w_out,
           ln1_gain, ln1_bias, w_coarse, b_coarse, w_fine, b_fine, w1, w3, w2, ln2_gain, ln2_bias):
    batch = x.shape[0]
    assert batch <= 8 and x.shape[1] % (2 * INPROJ_TM) == 0 and x.shape[2] == D_MODEL
    c8 = jnp.pad(c, ((0, 8 - batch), (0, 0)))
    h = x
    for l in range(w_ada.shape[0]):
        h = _layer(h, c8, w_ada[l], b_ada[l], w_in[l], w_attn_out[l], ret_decay_fwd[l], ret_decay_bwd[l],
                   ret_gn_gain[l], w_ret_out[l], w_out[l], ln1_gain[l], ln1_bias[l], w_coarse[l], b_coarse[l],
                   w_fine[l], b_fine[l], w1[l], w3[l], w2[l], ln2_gain[l], ln2_bias[l])
    return h
```

```python
import functools
import math

import jax
import jax.numpy as jnp
from jax import lax
from jax.experimental import pallas as pl
from jax.experimental.pallas import tpu as pltpu

F32 = jnp.float32
BF16 = jnp.bfloat16
I32 = jnp.int32

D_MODEL = 1024
ATTN_GROUPS = 3
HEADS_PER_GROUP = 4
HEAD_DIM = 128
ATTN_HEADS = ATTN_GROUPS * HEADS_PER_GROUP
GROUP_WIDTH = HEADS_PER_GROUP * HEAD_DIM
ATTN_PATTERNS = ((128, 1), (512, 4), (2048, 16))
ALIBI_MAX_EXP = 8.0
NEG = -1e30
RET_HEADS = 4
RET_QK = 256
RET_V = 512
N_GROUPS = 4
EXPERTS_PER_GROUP = 8
N_EXPERTS = N_GROUPS * EXPERTS_PER_GROUP
EXPERT_FF = 512
DEPTH = 1
DEEPNORM_ALPHA = (2.0 * DEPTH) ** 0.25
LN_EPS = 1e-5

LANES = 128
PERM_TILE = 512
ATTN_QB = 128
ATTN_HALF = 64
ATTN_KB = ATTN_QB + 2 * ATTN_HALF
RET_CHUNK = 256
MOE_BLK = 256
VMEM_LIMIT = 56 * 1024 * 1024


def _cparams(sem):
    return pltpu.CompilerParams(dimension_semantics=sem, vmem_limit_bytes=VMEM_LIMIT)


def _split_bf16(a):
    hi = a.astype(BF16)
    lo = (a - hi.astype(F32)).astype(BF16)
    return hi, lo


def _dot(a, b):
    return jnp.dot(a, b, preferred_element_type=F32)


def _dot3(a, b):
    ah, al = _split_bf16(a)
    bh, bl = _split_bf16(b)
    return _dot(ah, bh) + _dot(ah, bl) + _dot(al, bh)


def _ln(x):
    mu = jnp.mean(x, axis=-1, keepdims=True)
    xc = x - mu
    var = jnp.mean(xc * xc, axis=-1, keepdims=True)
    return xc * lax.rsqrt(var + LN_EPS)


def _sigmoid(x):
    return 1.0 / (1.0 + jnp.exp(-x))


ROW_TILE = D_MODEL // LANES


def _store_token_tiles(ref, val):
    n = val.shape[0]
    for cc in range(ROW_TILE):
        ref[pl.ds(cc, n, stride=ROW_TILE), :] = val[:, cc * LANES:(cc + 1) * LANES]


def _load_token_tiles(ref, n):
    return jnp.concatenate([ref[pl.ds(cc, n, stride=ROW_TILE), :] for cc in range(ROW_TILE)], axis=1)


def _ada_kernel(c_ref, w_ref, b_ref, o_ref):
    o_ref[...] = _dot3(c_ref[...], w_ref[...]) + b_ref[...]


def _ada(c8, w_ada, b_ada):
    n = w_ada.shape[1]
    return pl.pallas_call(
        _ada_kernel,
        grid=(n // D_MODEL,),
        in_specs=[pl.BlockSpec((8, D_MODEL), lambda j: (0, 0)),
                  pl.BlockSpec((D_MODEL, D_MODEL), lambda j: (0, j)),
                  pl.BlockSpec((1, D_MODEL), lambda j: (0, j))],
        out_specs=pl.BlockSpec((8, D_MODEL), lambda j: (0, j)),
        out_shape=jax.ShapeDtypeStruct((8, n), F32),
        compiler_params=_cparams(("arbitrary",)),
        name="ada",
    )(c8, w_ada, b_ada)


INPROJ_TM = 1024
LN_ROWS = 256


def _modulated_ln(x_ref, mod_ref, r0, rows):
    x = x_ref[r0:r0 + rows, :]
    return _ln(x) * (1.0 + mod_ref[1:2, :]) + mod_ref[0:1, :]


def _inproj_attn_kernel(x_ref, mod_ref, w_ref, o_ref, uf_scr, u_scr):
    j = pl.program_id(1)

    @pl.when(j == 0)
    def _():
        for r0 in range(0, INPROJ_TM, LN_ROWS):
            u = _modulated_ln(x_ref, mod_ref, r0, LN_ROWS)
            for cc in range(D_MODEL // LANES):
                uf_scr[cc, r0:r0 + LN_ROWS, :] = u[:, cc * LANES:(cc + 1) * LANES]
            u_scr[0, r0:r0 + LN_ROWS, :] = u.astype(BF16)
        for g in (1, 2):
            dil = ATTN_PATTERNS[g][1]
            n = PERM_TILE // dil
            for t0 in range(0, INPROJ_TM, PERM_TILE):
                for res in range(dil):
                    for cc in range(D_MODEL // LANES):
                        rows = uf_scr[cc, pl.ds(t0 + res, n, stride=dil), :]
                        u_scr[g, t0 + res * n:t0 + (res + 1) * n, cc * LANES:(cc + 1) * LANES] = rows.astype(BF16)

    o_ref[...] = _dot(u_scr[j], w_ref[...]).astype(BF16)


def _inproj_attn(x2, mod3, w_qkv, seq):
    t = x2.shape[0]
    tn = 3 * GROUP_WIDTH
    per_seq = seq // INPROJ_TM
    return pl.pallas_call(
        _inproj_attn_kernel,
        grid=(t // INPROJ_TM, ATTN_GROUPS),
        in_specs=[pl.BlockSpec((INPROJ_TM, D_MODEL), lambda i, j: (i, 0)),
                  pl.BlockSpec((None, 6, D_MODEL), lambda i, j: (i // per_seq, 0, 0)),
                  pl.BlockSpec((D_MODEL, tn), lambda i, j: (0, j))],
        out_specs=pl.BlockSpec((INPROJ_TM, tn), lambda i, j: (i, j)),
        out_shape=jax.ShapeDtypeStruct((t, ATTN_GROUPS * tn), BF16),
        scratch_shapes=[pltpu.VMEM((D_MODEL // LANES, INPROJ_TM, LANES), F32),
                        pltpu.VMEM((ATTN_GROUPS, INPROJ_TM, D_MODEL), BF16)],
        compiler_params=_cparams(("arbitrary", "arbitrary")),
        name="inproj_attn",
    )(x2, mod3, w_qkv)


def _inproj_rest_kernel(x_ref, mod_ref, w_ref, o_ref, u_scr):
    @pl.when(pl.program_id(1) == 0)
    def _():
        for r0 in range(0, INPROJ_TM, LN_ROWS):
            u_scr[r0:r0 + LN_ROWS, :] = _modulated_ln(x_ref, mod_ref, r0, LN_ROWS).astype(BF16)

    o_ref[...] = _dot(u_scr[...], w_ref[...]).astype(BF16)


def _inproj_rest(x2, mod3, w_rest, seq):
    t = x2.shape[0]
    n = w_rest.shape[1]
    tn = 1024
    per_seq = seq // INPROJ_TM
    return pl.pallas_call(
        _inproj_rest_kernel,
        grid=(t // INPROJ_TM, n // tn),
        in_specs=[pl.BlockSpec((INPROJ_TM, D_MODEL), lambda i, j: (i, 0)),
                  pl.BlockSpec((None, 6, D_MODEL), lambda i, j: (i // per_seq, 0, 0)),
                  pl.BlockSpec((D_MODEL, tn), lambda i, j: (0, j))],
        out_specs=pl.BlockSpec((INPROJ_TM, tn), lambda i, j: (i, j)),
        out_shape=jax.ShapeDtypeStruct((t, n), BF16),
        scratch_shapes=[pltpu.VMEM((INPROJ_TM, D_MODEL), BF16)],
        compiler_params=_cparams(("arbitrary", "arbitrary")),
        name="inproj_rest",
    )(x2, mod3, w_rest)


ATTN_OFFSETS = ATTN_KB // ATTN_HALF - 1


def _attn_kernel(q_ref, k_ref, v_ref, o_ref, lse_ref, bias_scr, *scratch, group, dil, nt, n):
    n_sub = nt * n
    if nt == 1:
        qs, ks, vs, os_, ls = q_ref.at[0], k_ref.at[0], v_ref.at[0], o_ref.at[0], lse_ref.at[0]
    else:
        qs, ks, vs, os_, ls = scratch
        for t in range(nt):
            qs[t * n:(t + 1) * n, :] = q_ref[t]
            ks[t * n:(t + 1) * n, :] = k_ref[t]
            vs[t * n:(t + 1) * n, :] = v_ref[t]

    @pl.when((pl.program_id(0) == 0) & (pl.program_id(1) == 0))
    def _():
        base = (lax.broadcasted_iota(I32, (ATTN_QB, ATTN_KB), 1)
                - lax.broadcasted_iota(I32, (ATTN_QB, ATTN_KB), 0))
        for j in range(ATTN_OFFSETS):
            dist = jnp.abs(base - j * ATTN_HALF)
            for hh in range(HEADS_PER_GROUP):
                slope = dil * 2.0 ** (-ALIBI_MAX_EXP * (group * HEADS_PER_GROUP + hh + 1) / ATTN_HEADS)
                bias_scr[hh * ATTN_OFFSETS + j] = jnp.where(dist <= ATTN_HALF, -slope * dist.astype(F32), NEG)

    scale = HEAD_DIM ** -0.5
    lane = lax.broadcasted_iota(I32, (ATTN_QB, LANES), 1)

    def block(blk, carry):
        q0 = pl.multiple_of(blk * ATTN_QB, ATTN_QB)
        start = pl.multiple_of(jnp.clip(q0 - ATTN_HALF, 0, n_sub - ATTN_KB), ATTN_HALF)
        j = (q0 - start) // ATTN_HALF
        lse_tile = jnp.zeros((ATTN_QB, LANES), F32)
        for hh in range(HEADS_PER_GROUP):
            cs = slice(hh * HEAD_DIM, (hh + 1) * HEAD_DIM)
            qb = qs[pl.ds(q0, ATTN_QB), cs]
            kb = ks[pl.ds(start, ATTN_KB), cs]
            vb = vs[pl.ds(start, ATTN_KB), cs]
            s = lax.dot_general(qb, kb, (((1,), (1,)), ((), ())), preferred_element_type=F32) * scale
            s = s + bias_scr[hh * ATTN_OFFSETS + j]
            m = jnp.max(s, axis=-1, keepdims=True)
            p = jnp.exp(s - m)
            l = jnp.sum(p, axis=-1, keepdims=True)
            o = _dot(p.astype(BF16), vb) * (1.0 / l)
            os_[pl.ds(q0, ATTN_QB), cs] = o.astype(BF16)
            lse_tile = jnp.where(lane == hh, m + jnp.log(l), lse_tile)
        ls[pl.ds(q0, ATTN_QB), :] = lse_tile
        return carry

    lax.fori_loop(0, n_sub // ATTN_QB, block, 0)

    if nt > 1:
        for t in range(nt):
            o_ref[t] = os_[t * n:(t + 1) * n, :]
            lse_ref[t] = ls[t * n:(t + 1) * n, :]


def _attention(qkv, group, batch, seq):
    dil = ATTN_PATTERNS[group][1]
    if dil == 1:
        nt, n = 1, seq
    else:
        nt, n = seq // PERM_TILE, PERM_TILE // dil
    n_sub = nt * n
    t = batch * seq
    qkv5 = qkv.reshape(batch, nt, dil, n, qkv.shape[1])
    cb = group * 3
    blk = (None, nt, None, n, GROUP_WIDTH)
    scratch = [pltpu.VMEM((HEADS_PER_GROUP * ATTN_OFFSETS, ATTN_QB, ATTN_KB), F32)]
    if nt > 1:
        scratch += [pltpu.VMEM((n_sub, GROUP_WIDTH), BF16)] * 4 + [pltpu.VMEM((n_sub, LANES), F32)]
    out, lse = pl.pallas_call(
        functools.partial(_attn_kernel, group=group, dil=dil, nt=nt, n=n),
        grid=(batch, dil),
        in_specs=[pl.BlockSpec(blk, lambda b, r: (b, 0, r, 0, cb)),
                  pl.BlockSpec(blk, lambda b, r: (b, 0, r, 0, cb + 1)),
                  pl.BlockSpec(blk, lambda b, r: (b, 0, r, 0, cb + 2))],
        out_specs=[pl.BlockSpec(blk, lambda b, r: (b, 0, r, 0, 0)),
                   pl.BlockSpec((None, nt, None, n, LANES), lambda b, r: (b, 0, r, 0, 0))],
        out_shape=[jax.ShapeDtypeStruct((batch, nt, dil, n, GROUP_WIDTH), BF16),
                   jax.ShapeDtypeStruct((batch, nt, dil, n, LANES), F32)],
        scratch_shapes=scratch,
        compiler_params=_cparams(("arbitrary", "arbitrary")),
        name=f"attn_g{group}",
    )(qkv5, qkv5, qkv5)
    return out.reshape(t, GROUP_WIDTH), lse.reshape(t, LANES)


def _log_sigmoid(x):
    return jnp.minimum(x, 0.0) - jnp.log(1.0 + jnp.exp(-jnp.abs(x)))


RET_HEAD_COLS = 2 * RET_QK + RET_V


def _retention_kernel(decay_ref, qkv_ref, g_ref, gain_ref, o_ref, state, ybwd, dmat, kdec, qdec, *, nc):
    c = RET_CHUNK
    h = pl.program_id(1)
    i = pl.program_id(2)
    kscale = RET_QK ** -0.5

    def set_decays(lg, forward):
        row = lax.broadcasted_iota(I32, (c, c), 0)
        col = lax.broadcasted_iota(I32, (c, c), 1)
        pos = lax.broadcasted_iota(I32, (c, LANES), 0).astype(F32)
        if forward:
            gap, key_pow, query_pow = row - col, (c - 1.0) - pos, pos + 1.0
            keep = gap >= 0
        else:
            gap, key_pow, query_pow = col - row, pos, c - pos
            keep = gap > 0
        dmat[...] = jnp.where(keep, jnp.exp(lg * jnp.maximum(gap, 0).astype(F32)) * kscale, 0.0)
        kdec[...] = jnp.exp(lg * key_pow) * kscale
        qdec[...] = jnp.exp(lg * query_pow)

    def chunk_step(lg):
        q = qkv_ref[:, 0:RET_QK]
        k = qkv_ref[:, RET_QK:2 * RET_QK]
        v = qkv_ref[:, 2 * RET_QK:]
        inner = lax.dot_general(q, k, (((1,), (1,)), ((), ())), preferred_element_type=F32) * dmat[...]
        y = _dot(inner.astype(BF16), v)
        qd = jnp.concatenate([qdec[...]] * (RET_V // LANES), axis=1)
        y = y + _dot(q, state[...].astype(BF16)) * qd
        kd = (k.astype(F32) * jnp.concatenate([kdec[...]] * (RET_QK // LANES), axis=1)).astype(BF16)
        kv = lax.dot_general(kd, v, (((0,), (0,)), ((), ())), preferred_element_type=F32)
        state[...] = state[...] * jnp.exp(lg * float(c)) + kv
        return y

    @pl.when((i == 0) | (i == nc))
    def _():
        state[...] = jnp.zeros_like(state)

    @pl.when(i < nc)
    def _():
        lg = _log_sigmoid(jnp.zeros((1, 1), F32) + decay_ref[RET_HEADS + h])

        @pl.when(i == 0)
        def _():
            set_decays(lg, False)

        ci = nc - 1 - i
        ybwd[pl.ds(pl.multiple_of(ci * c, c), c), :] = chunk_step(lg)

    @pl.when(i >= nc)
    def _():
        lg = _log_sigmoid(jnp.zeros((1, 1), F32) + decay_ref[h])

        @pl.when(i == nc)
        def _():
            set_decays(lg, True)

        ci = i - nc
        y = chunk_step(lg) + ybwd[pl.ds(pl.multiple_of(ci * c, c), c), :]
        g = g_ref[...].astype(F32)
        o_ref[...] = (g * _sigmoid(g) * (_ln(y) * gain_ref[...])).astype(BF16)


def _retention(rest, decays, gn_gain, batch, seq):
    c = RET_CHUNK
    nc = seq // c
    rest3 = rest.reshape(batch, seq, rest.shape[1])
    gate_blk = RET_HEADS * RET_HEAD_COLS // RET_V

    def chunk(i):
        return jnp.where(i < nc, nc - 1 - i, i - nc)

    grid_spec = pltpu.PrefetchScalarGridSpec(
        num_scalar_prefetch=1,
        grid=(batch, RET_HEADS, 2 * nc),
        in_specs=[pl.BlockSpec((None, c, RET_HEAD_COLS), lambda b, h, i, d: (b, chunk(i), h)),
                  pl.BlockSpec((None, c, RET_V), lambda b, h, i, d: (b, jnp.maximum(i - nc, 0), gate_blk + h)),
                  pl.BlockSpec((1, RET_V), lambda b, h, i, d: (0, h))],
        out_specs=pl.BlockSpec((None, c, RET_V), lambda b, h, i, d: (b, jnp.maximum(i - nc, 0), h)),
        scratch_shapes=[pltpu.VMEM((RET_QK, RET_V), F32), pltpu.VMEM((seq, RET_V), F32),
                        pltpu.VMEM((c, c), F32), pltpu.VMEM((c, LANES), F32), pltpu.VMEM((c, LANES), F32)],
    )
    out = pl.pallas_call(
        functools.partial(_retention_kernel, nc=nc),
        grid_spec=grid_spec,
        out_shape=jax.ShapeDtypeStruct((batch, seq, RET_HEADS * RET_V), BF16),
        compiler_params=_cparams(("arbitrary", "arbitrary", "arbitrary")),
        name="retention",
    )(decays, rest3, rest3, gn_gain)
    return out.reshape(batch * seq, RET_HEADS * RET_V)


MERGE_TM = PERM_TILE


def _merge_kernel(o0_ref, o1_ref, o2_ref, l0_ref, l1_ref, l2_ref, retg_ref, ga_ref, gr_ref, x_ref, mod_ref,
                  wa_ref, wr_ref, wo_ref, ln_ref, wrh_ref, wrl_ref, rb_ref,
                  h1_ref, u2_ref, lg_ref, on_scr, ln_scr):
    for g in (1, 2):
        dil = ATTN_PATTERNS[g][1]
        n = MERGE_TM // dil
        o_ref, l_ref = ((o1_ref, l1_ref), (o2_ref, l2_ref))[g - 1]
        for res in range(dil):
            rows = o_ref[res * n:(res + 1) * n, :].astype(F32)
            for hh in range(HEADS_PER_GROUP):
                on_scr[g - 1, hh, pl.ds(res, n, stride=dil), :] = rows[:, hh * HEAD_DIM:(hh + 1) * HEAD_DIM]
            ln_scr[g - 1, pl.ds(res, n, stride=dil), :] = l_ref[res * n:(res + 1) * n, :]

    l0, l1, l2 = l0_ref[...], ln_scr[0], ln_scr[1]
    lm = jnp.maximum(jnp.maximum(l0, l1), l2)
    e0, e1, e2 = jnp.exp(l0 - lm), jnp.exp(l1 - lm), jnp.exp(l2 - lm)
    inv = 1.0 / (e0 + e1 + e2)
    parts = []
    for hh in range(HEADS_PER_GROUP):
        sl = slice(hh * HEAD_DIM, (hh + 1) * HEAD_DIM)
        acc = (e0[:, hh:hh + 1] * o0_ref[:, sl].astype(F32)
               + e1[:, hh:hh + 1] * on_scr[0, hh]
               + e2[:, hh:hh + 1] * on_scr[1, hh])
        parts.append((acc * inv[:, hh:hh + 1]).astype(BF16))
    attn = jnp.concatenate(parts, axis=1)

    branch_a = _dot(attn, wa_ref[...])
    branch_r = _dot(retg_ref[...], wr_ref[...])
    merged = (_sigmoid(ga_ref[...].astype(F32)) * branch_a + _sigmoid(gr_ref[...].astype(F32)) * branch_r)
    y = _dot(merged.astype(BF16), wo_ref[...])

    h1 = _ln(DEEPNORM_ALPHA * x_ref[...] + mod_ref[2:3, :] * y) * ln_ref[0:1, :] + ln_ref[1:2, :]
    h1_ref[...] = h1
    u2 = _ln(h1) * (1.0 + mod_ref[4:5, :]) + mod_ref[3:4, :]
    _store_token_tiles(u2_ref, u2)
    uh, ul = _split_bf16(u2)
    lg_ref[...] = _dot(uh, wrh_ref[...]) + _dot(uh, wrl_ref[...]) + _dot(ul, wrh_ref[...]) + rb_ref[...]


def _merge(outs, lses, retg, rest, x2, mod3, wa, wr, wo, ln1, wr_hi, wr_lo, rbias, seq):
    t = x2.shape[0]
    tm = MERGE_TM
    per_seq = seq // tm
    row = lambda w: pl.BlockSpec((tm, w), lambda i: (i, 0))
    full = lambda a: pl.BlockSpec(a.shape, lambda i: (0,) * a.ndim)
    return pl.pallas_call(
        _merge_kernel,
        grid=(t // tm,),
        in_specs=[row(GROUP_WIDTH)] * 3 + [row(LANES)] * 3 + [
            row(RET_HEADS * RET_V),
            pl.BlockSpec((tm, D_MODEL), lambda i: (i, 6)),
            pl.BlockSpec((tm, D_MODEL), lambda i: (i, 7)),
            row(D_MODEL),
            pl.BlockSpec((None, 6, D_MODEL), lambda i: (i // per_seq, 0, 0)),
            full(wa), full(wr), full(wo), full(ln1), full(wr_hi), full(wr_lo), full(rbias)],
        out_specs=[row(D_MODEL), pl.BlockSpec((tm * ROW_TILE, LANES), lambda i: (i, 0)), row(LANES)],
        out_shape=[jax.ShapeDtypeStruct((t, D_MODEL), F32),
                   jax.ShapeDtypeStruct((t * ROW_TILE, LANES), F32),
                   jax.ShapeDtypeStruct((t, LANES), F32)],
        scratch_shapes=[pltpu.VMEM((2, HEADS_PER_GROUP, tm, HEAD_DIM), F32), pltpu.VMEM((2, tm, LANES), F32)],
        compiler_params=_cparams(("arbitrary",)),
        name="merge",
    )(*outs, *lses, retg, rest, rest, x2, mod3, wa, wr, wo, ln1, wr_hi, wr_lo, rbias)


ROUTE_TM = 512
BIG = 1 << 20


def _route_kernel(lg_ref, cols_ref, ints_ref, cnt_ref, carry):
    i = pl.program_id(0)

    @pl.when(i == 0)
    def _():
        carry[...] = jnp.zeros_like(carry)

    tm = ROUTE_TM
    lg = lg_ref[...]
    lane = lax.broadcasted_iota(I32, (tm, LANES), 1)
    lane_f = lane.astype(F32)
    first = lambda mask: jnp.min(jnp.where(mask, lane_f, float(BIG)), axis=-1, keepdims=True).astype(I32)

    coarse = jnp.where(lane < N_GROUPS, lg, NEG)
    cmax = jnp.max(coarse, axis=-1, keepdims=True)
    gsel = first(coarse == cmax)
    p_group = 1.0 / jnp.sum(jnp.exp(coarse - cmax), axis=-1, keepdims=True)

    lo = N_GROUPS + EXPERTS_PER_GROUP * gsel
    fine = jnp.where((lane >= lo) & (lane < lo + EXPERTS_PER_GROUP), lg, NEG)
    v1 = jnp.max(fine, axis=-1, keepdims=True)
    i1 = first(fine == v1)
    fine2 = jnp.where(lane == i1, NEG, fine)
    v2 = jnp.max(fine2, axis=-1, keepdims=True)
    i2 = first(fine2 == v2)
    ex = jnp.exp(v2 - v1)
    den = 1.0 / (1.0 + ex)
    gate1 = p_group * den
    gate2 = p_group * (ex * den)
    e1 = i1 - N_GROUPS
    e2 = i2 - N_GROUPS

    oh1 = lane == e1
    oh2 = lane == e2
    cnt = jnp.where(oh1 | oh2, 1.0, 0.0)
    r_i = lax.broadcasted_iota(I32, (tm, tm), 0)
    c_i = lax.broadcasted_iota(I32, (tm, tm), 1)
    tri = jnp.where(r_i > c_i, 1.0, 0.0).astype(BF16)
    rank = _dot(tri, cnt.astype(BF16)) + carry[...]
    r1 = jnp.sum(jnp.where(oh1, rank, 0.0), axis=-1, keepdims=True)
    r2 = jnp.sum(jnp.where(oh2, rank, 0.0), axis=-1, keepdims=True)
    carry[...] = carry[...] + jnp.sum(cnt, axis=0, keepdims=True)
    cnt_ref[...] = jnp.broadcast_to(carry[...], cnt_ref.shape)

    cols_ref[...] = jnp.where(lane == 0, gate1, jnp.where(lane == 1, gate2, 0.0))
    packed = jnp.where(lane == 0, e1.astype(F32),
                       jnp.where(lane == 1, e2.astype(F32),
                                 jnp.where(lane == 2, r1, jnp.where(lane == 3, r2, 0.0))))
    ints_ref[...] = packed.T[0:8, :].astype(I32)


def _route(logits):
    t = logits.shape[0]
    tm = ROUTE_TM
    return pl.pallas_call(
        _route_kernel,
        grid=(t // tm,),
        in_specs=[pl.BlockSpec((tm, LANES), lambda i: (i, 0))],
        out_specs=[pl.BlockSpec((tm, LANES), lambda i: (i, 0)),
                   pl.BlockSpec((8, tm), lambda i: (0, i)),
                   pl.BlockSpec((8, LANES), lambda i: (0, 0))],
        out_shape=[jax.ShapeDtypeStruct((t, LANES), F32),
                   jax.ShapeDtypeStruct((8, t), I32),
                   jax.ShapeDtypeStruct((8, LANES), F32)],
        scratch_shapes=[pltpu.VMEM((1, LANES), F32)],
        compiler_params=_cparams(("arbitrary",)),
        name="route",
    )(logits)


def _plan_kernel(ints_ref, cnt_ref, dest_ref, meta_ref, *, n_blocks_pad):
    sub = lax.broadcasted_iota(I32, (LANES, LANES), 0)
    lane = lax.broadcasted_iota(I32, (LANES, LANES), 1)
    cnt = cnt_ref[0:1, :]
    nblk_row = jnp.floor((cnt + (MOE_BLK - 1.0)) * (1.0 / MOE_BLK))
    nblk_mat = jnp.broadcast_to(nblk_row, (LANES, LANES))
    start_col = jnp.sum(jnp.where(lane < sub, nblk_mat, 0.0), axis=-1, keepdims=True)
    nblk_col = jnp.sum(jnp.where(lane == sub, nblk_mat, 0.0), axis=-1, keepdims=True)
    end_col = start_col + nblk_col

    ints = ints_ref[...]
    base = jnp.zeros(ints.shape, F32)
    for e in range(N_EXPERTS):
        base = jnp.where(ints == e, start_col[e:e + 1, :] * float(MOE_BLK), base)
    dest = base[0:2, :].astype(I32) + ints[2:4, :]
    dest_ref[...] = jnp.concatenate([dest, jnp.zeros((6, ints.shape[1]), I32)], axis=0)

    blk = lax.broadcasted_iota(I32, (LANES, n_blocks_pad), 1).astype(F32)
    e_sub = lax.broadcasted_iota(I32, (LANES, n_blocks_pad), 0)
    done = jnp.where((e_sub < N_EXPERTS) & (end_col <= blk), 1.0, 0.0)
    bexp = jnp.minimum(jnp.sum(done, axis=0, keepdims=True), N_EXPERTS - 1.0)
    used = jnp.sum(nblk_row, axis=-1, keepdims=True)
    row = lax.broadcasted_iota(I32, (8, n_blocks_pad), 0)
    meta = jnp.where(row == 0, bexp, jnp.where(row == 1, used, 0.0))
    meta_ref[...] = meta.astype(I32)


def _plan(ints, counts, n_blocks_pad):
    t = ints.shape[1]
    return pl.pallas_call(
        functools.partial(_plan_kernel, n_blocks_pad=n_blocks_pad),
        out_shape=[jax.ShapeDtypeStruct((8, t), I32), jax.ShapeDtypeStruct((8, n_blocks_pad), I32)],
        compiler_params=pltpu.CompilerParams(vmem_limit_bytes=VMEM_LIMIT),
        name="plan",
    )(ints, counts)


DISPATCH_TM = 512
DMA_UNROLL = 8


def _row_tile(ref, r):
    return ref.at[pl.ds(pl.multiple_of(r * ROW_TILE, ROW_TILE), ROW_TILE)]


def _dispatch_kernel(d0_ref, d1_ref, u2_ref, rows_in_ref, rows_ref, sem):
    del rows_in_ref

    def issue(t, carry):
        src = _row_tile(u2_ref, t)
        pltpu.make_async_copy(src, _row_tile(rows_ref, d0_ref[t]), sem).start(priority=0)
        pltpu.make_async_copy(src, _row_tile(rows_ref, d1_ref[t]), sem).start(priority=1)
        return carry

    lax.fori_loop(0, DISPATCH_TM, issue, 0, unroll=DMA_UNROLL)
    for _ in range(2):
        pltpu.make_async_copy(u2_ref, rows_ref.at[pl.ds(0, DISPATCH_TM * ROW_TILE)], sem).wait()


def _dispatch(dest0, dest1, u2, n_rows):
    t = u2.shape[0] // ROW_TILE
    rows0 = jnp.zeros((n_rows * ROW_TILE, LANES), F32)
    idx = pl.BlockSpec((DISPATCH_TM,), lambda i: (i,), memory_space=pltpu.SMEM)
    return pl.pallas_call(
        _dispatch_kernel,
        grid=(t // DISPATCH_TM,),
        in_specs=[idx, idx,
                  pl.BlockSpec((DISPATCH_TM * ROW_TILE, LANES), lambda i: (i, 0)),
                  pl.BlockSpec(memory_space=pl.ANY)],
        out_specs=pl.BlockSpec(memory_space=pl.ANY),
        out_shape=jax.ShapeDtypeStruct((n_rows * ROW_TILE, LANES), F32),
        scratch_shapes=[pltpu.SemaphoreType.DMA(())],
        input_output_aliases={3: 0},
        compiler_params=_cparams(("arbitrary",)),
        name="dispatch",
    )(dest0, dest1, u2, rows0)


def _experts_kernel(bexp_ref, used_ref, x_ref, w1_ref, w3_ref, w2_ref, y_ref, w1s, w3s, w2s):
    i = pl.program_id(0)
    prev = bexp_ref[jnp.maximum(i - 1, 0)]
    active = i < used_ref[0]

    @pl.when(active & ((i == 0) | (bexp_ref[i] != prev)))
    def _():
        w1s[...] = w1_ref[...].astype(BF16)
        w3s[...] = w3_ref[...].astype(BF16)
        w2s[...] = w2_ref[...].astype(BF16)

    @pl.when(active)
    def _():
        xb = _load_token_tiles(x_ref, MOE_BLK).astype(BF16)
        a = _dot(xb, w1s[...])
        b = _dot(xb, w3s[...])
        hdn = (a * _sigmoid(a) * b).astype(BF16)
        _store_token_tiles(y_ref, _dot(hdn, w2s[...]))

    @pl.when(jnp.logical_not(active))
    def _():
        y_ref[...] = jnp.zeros_like(y_ref)


def _experts(bexp, used, rows, w1, w3, w2):
    n_blocks = rows.shape[0] // (MOE_BLK * ROW_TILE)
    row_blk = pl.BlockSpec((MOE_BLK * ROW_TILE, LANES), lambda i, be, nu: (i, 0))
    grid_spec = pltpu.PrefetchScalarGridSpec(
        num_scalar_prefetch=2,
        grid=(n_blocks,),
        in_specs=[row_blk,
                  pl.BlockSpec((None, D_MODEL, EXPERT_FF), lambda i, be, nu: (be[i], 0, 0)),
                  pl.BlockSpec((None, D_MODEL, EXPERT_FF), lambda i, be, nu: (be[i], 0, 0)),
                  pl.BlockSpec((None, EXPERT_FF, D_MODEL), lambda i, be, nu: (be[i], 0, 0))],
        out_specs=row_blk,
        scratch_shapes=[pltpu.VMEM((D_MODEL, EXPERT_FF), BF16), pltpu.VMEM((D_MODEL, EXPERT_FF), BF16),
                        pltpu.VMEM((EXPERT_FF, D_MODEL), BF16)],
    )
    return pl.pallas_call(
        _experts_kernel,
        grid_spec=grid_spec,
        out_shape=jax.ShapeDtypeStruct(rows.shape, F32),
        compiler_params=_cparams(("arbitrary",)),
        name="experts",
    )(bexp, used, rows, w1, w3, w2)


COMBINE_TM = 256


def _combine_kernel(d0_ref, d1_ref, y_ref, cols_ref, h1_ref, mod_ref, ln_ref, o_ref, ya, yb, sem):
    def issue(t, carry):
        pltpu.make_async_copy(_row_tile(y_ref, d0_ref[t]), _row_tile(ya, t), sem).start(priority=0)
        pltpu.make_async_copy(_row_tile(y_ref, d1_ref[t]), _row_tile(yb, t), sem).start(priority=1)
        return carry

    lax.fori_loop(0, COMBINE_TM, issue, 0, unroll=DMA_UNROLL)
    for buf in (ya, yb):
        pltpu.make_async_copy(y_ref.at[pl.ds(0, COMBINE_TM * ROW_TILE)], buf, sem).wait()

    cols = cols_ref[...]
    moe = (cols[:, 0:1] * _load_token_tiles(ya, COMBINE_TM)
           + cols[:, 1:2] * _load_token_tiles(yb, COMBINE_TM))
    pre = DEEPNORM_ALPHA * h1_ref[...] + mod_ref[5:6, :] * moe
    o_ref[...] = _ln(pre) * ln_ref[0:1, :] + ln_ref[1:2, :]


def _combine(dest0, dest1, y_rows, cols, h1, mod3, ln2, seq):
    t = h1.shape[0]
    tm = COMBINE_TM
    per_seq = seq // tm
    idx = pl.BlockSpec((tm,), lambda i: (i,), memory_space=pltpu.SMEM)
    return pl.pallas_call(
        _combine_kernel,
        grid=(t // tm,),
        in_specs=[idx, idx,
                  pl.BlockSpec(memory_space=pl.ANY),
                  pl.BlockSpec((tm, LANES), lambda i: (i, 0)),
                  pl.BlockSpec((tm, D_MODEL), lambda i: (i, 0)),
                  pl.BlockSpec((None, 6, D_MODEL), lambda i: (i // per_seq, 0, 0)),
                  pl.BlockSpec((2, D_MODEL), lambda i: (0, 0))],
        out_specs=pl.BlockSpec((tm, D_MODEL), lambda i: (i, 0)),
        out_shape=jax.ShapeDtypeStruct((t, D_MODEL), F32),
        scratch_shapes=[pltpu.VMEM((tm * ROW_TILE, LANES), F32), pltpu.VMEM((tm * ROW_TILE, LANES), F32),
                        pltpu.SemaphoreType.DMA(())],
        compiler_params=_cparams(("arbitrary",)),
        name="combine",
    )(dest0, dest1, y_rows, cols, h1, mod3, ln2)


def _layer(h, c8, w_ada, b_ada, w_in, w_attn_out, decay_f, decay_b, gn_gain, w_ret_out, w_out,
           ln1_gain, ln1_bias, w_coarse, b_coarse, w_fine, b_fine, w1, w3, w2, ln2_gain, ln2_bias):
    batch, seq, d = h.shape
    t = batch * seq
    x2 = h.reshape(t, d)

    mod = _ada(c8, w_ada, b_ada.reshape(1, -1))
    mod3 = mod[:batch].reshape(batch, 6, d)

    aw = ATTN_HEADS * HEAD_DIM
    wq, wk, wv = (w_in[:, s * aw:(s + 1) * aw] for s in range(3))
    gcols = lambda w, g: w[:, g * GROUP_WIDTH:(g + 1) * GROUP_WIDTH]
    w_qkv = jnp.concatenate([gcols(w, g) for g in range(ATTN_GROUPS) for w in (wq, wk, wv)], axis=1).astype(BF16)
    rq0, rk0, rv0, tail0 = 3 * aw, 3 * aw + RET_HEADS * RET_QK, 3 * aw + 2 * RET_HEADS * RET_QK, \
        3 * aw + 2 * RET_HEADS * RET_QK + RET_HEADS * RET_V
    head_cols = [w_in[:, o + hh * wd:o + (hh + 1) * wd]
                 for hh in range(RET_HEADS) for o, wd in ((rq0, RET_QK), (rk0, RET_QK), (rv0, RET_V))]
    w_rest = jnp.concatenate(head_cols + [w_in[:, tail0:]], axis=1).astype(BF16)

    qkv = _inproj_attn(x2, mod3, w_qkv, seq)
    rest = _inproj_rest(x2, mod3, w_rest, seq)

    outs, lses = zip(*[_attention(qkv, g, batch, seq) for g in range(ATTN_GROUPS)])
    decays = jnp.concatenate([decay_f, decay_b]).astype(F32)
    retg = _retention(rest, decays, gn_gain.reshape(1, -1), batch, seq)

    w_route = jnp.concatenate([w_coarse] + [w_fine[g] for g in range(N_GROUPS)], axis=1)
    n_route = w_route.shape[1]
    w_route = jnp.pad(w_route, ((0, 0), (0, LANES - n_route)))
    wr_hi, wr_lo = _split_bf16(w_route)
    rbias = jnp.pad(jnp.concatenate([b_coarse, b_fine.reshape(-1)]), (0, LANES - n_route)).reshape(1, LANES)
    ln1 = jnp.stack([ln1_gain, ln1_bias])
    h1, u2, logits = _merge(outs, lses, retg, rest, x2, mod3, w_attn_out.astype(BF16), w_ret_out.astype(BF16),
                            w_out.astype(BF16), ln1, wr_hi, wr_lo, rbias, seq)

    cols, ints, counts = _route(logits)
    n_blocks = 2 * t // MOE_BLK + N_EXPERTS
    n_blocks_pad = -(-n_blocks // LANES) * LANES
    dest, meta = _plan(ints, counts, n_blocks_pad)
    dest0, dest1 = dest[0], dest[1]
    rows = _dispatch(dest0, dest1, u2, n_blocks * MOE_BLK)
    y_rows = _experts(meta[0, :n_blocks], meta[1, :1], rows, w1, w3, w2)
    out = _combine(dest0, dest1, y_rows, cols, h1, mod3, jnp.stack([ln2_gain, ln2_bias]), seq)
    return out.reshape(batch, seq, d)


def kernel(x, c, w_ada, b_ada, w_in, w_attn_out, ret_decay_fwd, ret_decay_bwd, ret_gn_gain, w_ret_out, w_out,
           ln1_gain, ln1_bias, w_coarse, b_coarse, w_fine, b_fine, w1, w3, w2, ln2_gain, ln2_bias):
    batch = x.shape[0]
    assert batch <= 8 and x.shape[1] % (2 * INPROJ_TM) == 0 and x.shape[2] == D_MODEL
    c8 = jnp.pad(c, ((0, 8 - batch), (0, 0)))
    h = x
    for l in range(w_ada.shape[0]):
        h = _layer(h, c8, w_ada[l], b_ada[l], w_in[l], w_attn_out[l], ret_decay_fwd[l], ret_decay_bwd[l],
                   ret_gn_gain[l], w_ret_out[l], w_out[l], ln1_gain[l], ln1_bias[l], w_coarse[l], b_coarse[l],
                   w_fine[l], b_fine[l], w1[l], w3[l], w2[l], ln2_gain[l], ln2_bias[l])
    return h
```

```python
import functools
import math

import jax
import jax.numpy as jnp
from jax import lax
from jax.experimental import pallas as pl
from jax.experimental.pallas import tpu as pltpu

F32 = jnp.float32
BF16 = jnp.bfloat16
I32 = jnp.int32

D_MODEL = 1024
ATTN_GROUPS = 3
HEADS_PER_GROUP = 4
HEAD_DIM = 128
ATTN_HEADS = ATTN_GROUPS * HEADS_PER_GROUP
GROUP_WIDTH = HEADS_PER_GROUP * HEAD_DIM
ATTN_PATTERNS = ((128, 1), (512, 4), (2048, 16))
ALIBI_MAX_EXP = 8.0
NEG = -1e30
RET_HEADS = 4
RET_QK = 256
RET_V = 512
N_GROUPS = 4
EXPERTS_PER_GROUP = 8
N_EXPERTS = N_GROUPS * EXPERTS_PER_GROUP
EXPERT_FF = 512
DEPTH = 1
DEEPNORM_ALPHA = (2.0 * DEPTH) ** 0.25
LN_EPS = 1e-5

LANES = 128
PERM_TILE = 512
ATTN_QB = 128
ATTN_HALF = 64
ATTN_KB = ATTN_QB + 2 * ATTN_HALF
RET_CHUNK = 256
RET_STEP = 1024
MOE_BLK = 256
VMEM_LIMIT = 56 * 1024 * 1024


def _cparams(sem):
    return pltpu.CompilerParams(dimension_semantics=sem, vmem_limit_bytes=VMEM_LIMIT)


def _split_bf16(a):
    hi = a.astype(BF16)
    lo = (a - hi.astype(F32)).astype(BF16)
    return hi, lo


def _dot(a, b):
    return jnp.dot(a, b, preferred_element_type=F32)


def _dot3(a, b):
    ah, al = _split_bf16(a)
    bh, bl = _split_bf16(b)
    return _dot(ah, bh) + _dot(ah, bl) + _dot(al, bh)


def _ln(x):
    mu = jnp.mean(x, axis=-1, keepdims=True)
    xc = x - mu
    var = jnp.mean(xc * xc, axis=-1, keepdims=True)
    return xc * lax.rsqrt(var + LN_EPS)


def _sigmoid(x):
    return 1.0 / (1.0 + jnp.exp(-x))


ROW_TILE = D_MODEL // LANES


def _store_token_tiles(ref, val):
    n = val.shape[0]
    for cc in range(ROW_TILE):
        ref[pl.ds(cc, n, stride=ROW_TILE), :] = val[:, cc * LANES:(cc + 1) * LANES]


def _load_token_tiles(ref, n):
    return jnp.concatenate([ref[pl.ds(cc, n, stride=ROW_TILE), :] for cc in range(ROW_TILE)], axis=1)


def _ada_kernel(c_ref, w_ref, b_ref, o_ref):
    o_ref[...] = _dot3(c_ref[...], w_ref[...]) + b_ref[...]


def _ada(c8, w_ada, b_ada):
    n = w_ada.shape[1]
    return pl.pallas_call(
        _ada_kernel,
        grid=(n // D_MODEL,),
        in_specs=[pl.BlockSpec((8, D_MODEL), lambda j: (0, 0)),
                  pl.BlockSpec((D_MODEL, D_MODEL), lambda j: (0, j)),
                  pl.BlockSpec((1, D_MODEL), lambda j: (0, j))],
        out_specs=pl.BlockSpec((8, D_MODEL), lambda j: (0, j)),
        out_shape=jax.ShapeDtypeStruct((8, n), F32),
        compiler_params=_cparams(("arbitrary",)),
        name="ada",
    )(c8, w_ada, b_ada)


INPROJ_TM = 1024
LN_ROWS = 256


def _modulated_ln(x_ref, mod_ref, r0, rows):
    x = x_ref[r0:r0 + rows, :]
    return _ln(x) * (1.0 + mod_ref[1:2, :]) + mod_ref[0:1, :]


def _inproj_attn_kernel(x_ref, mod_ref, w_ref, o_ref, uf_scr, u_scr):
    j = pl.program_id(1)

    @pl.when(j == 0)
    def _():
        for r0 in range(0, INPROJ_TM, LN_ROWS):
            u = _modulated_ln(x_ref, mod_ref, r0, LN_ROWS)
            for cc in range(D_MODEL // LANES):
                uf_scr[cc, r0:r0 + LN_ROWS, :] = u[:, cc * LANES:(cc + 1) * LANES]
            u_scr[0, r0:r0 + LN_ROWS, :] = u.astype(BF16)
        for g in (1, 2):
            dil = ATTN_PATTERNS[g][1]
            n = PERM_TILE // dil
            for t0 in range(0, INPROJ_TM, PERM_TILE):
                for res in range(dil):
                    for cc in range(D_MODEL // LANES):
                        rows = uf_scr[cc, pl.ds(t0 + res, n, stride=dil), :]
                        u_scr[g, t0 + res * n:t0 + (res + 1) * n, cc * LANES:(cc + 1) * LANES] = rows.astype(BF16)

    o_ref[...] = _dot(u_scr[j], w_ref[...]).astype(BF16)


def _inproj_attn(x2, mod3, w_qkv, seq):
    t = x2.shape[0]
    tn = 3 * GROUP_WIDTH
    per_seq = seq // INPROJ_TM
    return pl.pallas_call(
        _inproj_attn_kernel,
        grid=(t // INPROJ_TM, ATTN_GROUPS),
        in_specs=[pl.BlockSpec((INPROJ_TM, D_MODEL), lambda i, j: (i, 0)),
                  pl.BlockSpec((None, 6, D_MODEL), lambda i, j: (i // per_seq, 0, 0)),
                  pl.BlockSpec((D_MODEL, tn), lambda i, j: (0, j))],
        out_specs=pl.BlockSpec((INPROJ_TM, tn), lambda i, j: (i, j)),
        out_shape=jax.ShapeDtypeStruct((t, ATTN_GROUPS * tn), BF16),
        scratch_shapes=[pltpu.VMEM((D_MODEL // LANES, INPROJ_TM, LANES), F32),
                        pltpu.VMEM((ATTN_GROUPS, INPROJ_TM, D_MODEL), BF16)],
        compiler_params=_cparams(("arbitrary", "arbitrary")),
        name="inproj_attn",
    )(x2, mod3, w_qkv)


def _inproj_rest_kernel(x_ref, mod_ref, w_ref, o_ref, u_scr):
    @pl.when(pl.program_id(1) == 0)
    def _():
        for r0 in range(0, INPROJ_TM, LN_ROWS):
            u_scr[r0:r0 + LN_ROWS, :] = _modulated_ln(x_ref, mod_ref, r0, LN_ROWS).astype(BF16)

    o_ref[...] = _dot(u_scr[...], w_ref[...]).astype(BF16)


def _inproj_rest(x2, mod3, w_rest, seq):
    t = x2.shape[0]
    n = w_rest.shape[1]
    tn = 1024
    per_seq = seq // INPROJ_TM
    return pl.pallas_call(
        _inproj_rest_kernel,
        grid=(t // INPROJ_TM, n // tn),
        in_specs=[pl.BlockSpec((INPROJ_TM, D_MODEL), lambda i, j: (i, 0)),
                  pl.BlockSpec((None, 6, D_MODEL), lambda i, j: (i // per_seq, 0, 0)),
                  pl.BlockSpec((D_MODEL, tn), lambda i, j: (0, j))],
        out_specs=pl.BlockSpec((INPROJ_TM, tn), lambda i, j: (i, j)),
        out_shape=jax.ShapeDtypeStruct((t, n), BF16),
        scratch_shapes=[pltpu.VMEM((INPROJ_TM, D_MODEL), BF16)],
        compiler_params=_cparams(("arbitrary", "arbitrary")),
        name="inproj_rest",
    )(x2, mod3, w_rest)


ATTN_OFFSETS = ATTN_KB // ATTN_HALF - 1


def _attn_kernel(q_ref, k_ref, v_ref, o_ref, lse_ref, bias_scr, *scratch, group, dil, nt, n):
    n_sub = nt * n
    if nt == 1:
        qs, ks, vs, os_, ls = q_ref.at[0], k_ref.at[0], v_ref.at[0], o_ref.at[0], lse_ref.at[0]
    else:
        qs, ks, vs, os_, ls = scratch
        for t in range(nt):
            qs[t * n:(t + 1) * n, :] = q_ref[t]
            ks[t * n:(t + 1) * n, :] = k_ref[t]
            vs[t * n:(t + 1) * n, :] = v_ref[t]

    @pl.when((pl.program_id(0) == 0) & (pl.program_id(1) == 0))
    def _():
        base = (lax.broadcasted_iota(I32, (ATTN_QB, ATTN_KB), 1)
                - lax.broadcasted_iota(I32, (ATTN_QB, ATTN_KB), 0))
        for j in range(ATTN_OFFSETS):
            dist = jnp.abs(base - j * ATTN_HALF)
            for hh in range(HEADS_PER_GROUP):
                slope = dil * 2.0 ** (-ALIBI_MAX_EXP * (group * HEADS_PER_GROUP + hh + 1) / ATTN_HEADS)
                bias_scr[hh * ATTN_OFFSETS + j] = jnp.where(dist <= ATTN_HALF, -slope * dist.astype(F32), NEG)

    scale = HEAD_DIM ** -0.5
    lane = lax.broadcasted_iota(I32, (ATTN_QB, LANES), 1)

    def block(blk, carry):
        q0 = pl.multiple_of(blk * ATTN_QB, ATTN_QB)
        start = pl.multiple_of(jnp.clip(q0 - ATTN_HALF, 0, n_sub - ATTN_KB), ATTN_HALF)
        j = (q0 - start) // ATTN_HALF
        lse_tile = jnp.zeros((ATTN_QB, LANES), F32)
        for hh in range(HEADS_PER_GROUP):
            cs = slice(hh * HEAD_DIM, (hh + 1) * HEAD_DIM)
            qb = qs[pl.ds(q0, ATTN_QB), cs]
            kb = ks[pl.ds(start, ATTN_KB), cs]
            vb = vs[pl.ds(start, ATTN_KB), cs]
            s = lax.dot_general(qb, kb, (((1,), (1,)), ((), ())), preferred_element_type=F32) * scale
            s = s + bias_scr[hh * ATTN_OFFSETS + j]
            m = jnp.max(s, axis=-1, keepdims=True)
            p = jnp.exp(s - m)
            l = jnp.sum(p, axis=-1, keepdims=True)
            o = _dot(p.astype(BF16), vb) * (1.0 / l)
            os_[pl.ds(q0, ATTN_QB), cs] = o.astype(BF16)
            lse_tile = jnp.where(lane == hh, m + jnp.log(l), lse_tile)
        ls[pl.ds(q0, ATTN_QB), :] = lse_tile
        return carry

    lax.fori_loop(0, n_sub // ATTN_QB, block, 0)

    if nt > 1:
        for t in range(nt):
            o_ref[t] = os_[t * n:(t + 1) * n, :]
            lse_ref[t] = ls[t * n:(t + 1) * n, :]


def _attention(qkv, group, batch, seq):
    dil = ATTN_PATTERNS[group][1]
    if dil == 1:
        nt, n = 1, seq
    else:
        nt, n = seq // PERM_TILE, PERM_TILE // dil
    n_sub = nt * n
    t = batch * seq
    qkv5 = qkv.reshape(batch, nt, dil, n, qkv.shape[1])
    cb = group * 3
    blk = (None, nt, None, n, GROUP_WIDTH)
    scratch = [pltpu.VMEM((HEADS_PER_GROUP * ATTN_OFFSETS, ATTN_QB, ATTN_KB), F32)]
    if nt > 1:
        scratch += [pltpu.VMEM((n_sub, GROUP_WIDTH), BF16)] * 4 + [pltpu.VMEM((n_sub, LANES), F32)]
    out, lse = pl.pallas_call(
        functools.partial(_attn_kernel, group=group, dil=dil, nt=nt, n=n),
        grid=(batch, dil),
        in_specs=[pl.BlockSpec(blk, lambda b, r: (b, 0, r, 0, cb)),
                  pl.BlockSpec(blk, lambda b, r: (b, 0, r, 0, cb + 1)),
                  pl.BlockSpec(blk, lambda b, r: (b, 0, r, 0, cb + 2))],
        out_specs=[pl.BlockSpec(blk, lambda b, r: (b, 0, r, 0, 0)),
                   pl.BlockSpec((None, nt, None, n, LANES), lambda b, r: (b, 0, r, 0, 0))],
        out_shape=[jax.ShapeDtypeStruct((batch, nt, dil, n, GROUP_WIDTH), BF16),
                   jax.ShapeDtypeStruct((batch, nt, dil, n, LANES), F32)],
        scratch_shapes=scratch,
        compiler_params=_cparams(("arbitrary", "arbitrary")),
        name=f"attn_g{group}",
    )(qkv5, qkv5, qkv5)
    return out.reshape(t, GROUP_WIDTH), lse.reshape(t, LANES)


def _log_sigmoid(x):
    return jnp.minimum(x, 0.0) - jnp.log(1.0 + jnp.exp(-jnp.abs(x)))


RET_HEAD_COLS = 2 * RET_QK + RET_V


def _retention_kernel(decay_ref, qkv_ref, g_ref, gain_ref, o_ref, state, ybwd, dmat, kdec, qdec, *, nc):
    c = RET_CHUNK
    h = pl.program_id(1)
    i = pl.program_id(2)
    kscale = RET_QK ** -0.5

    def set_decays(lg, forward):
        row = lax.broadcasted_iota(I32, (c, c), 0)
        col = lax.broadcasted_iota(I32, (c, c), 1)
        pos = lax.broadcasted_iota(I32, (c, LANES), 0).astype(F32)
        if forward:
            gap, key_pow, query_pow = row - col, (c - 1.0) - pos, pos + 1.0
            keep = gap >= 0
        else:
            gap, key_pow, query_pow = col - row, pos, c - pos
            keep = gap > 0
        dmat[...] = jnp.where(keep, jnp.exp(lg * jnp.maximum(gap, 0).astype(F32)) * kscale, 0.0)
        kdec[...] = jnp.exp(lg * key_pow) * kscale
        qdec[...] = jnp.exp(lg * query_pow)

    def chunk_step(lg, r0):
        q = qkv_ref[r0:r0 + c, 0:RET_QK]
        k = qkv_ref[r0:r0 + c, RET_QK:2 * RET_QK]
        v = qkv_ref[r0:r0 + c, 2 * RET_QK:]
        inner = lax.dot_general(q, k, (((1,), (1,)), ((), ())), preferred_element_type=F32) * dmat[...]
        y = _dot(inner.astype(BF16), v)
        qd = jnp.concatenate([qdec[...]] * (RET_V // LANES), axis=1)
        y = y + _dot(q, state[...].astype(BF16)) * qd
        kd = (k.astype(F32) * jnp.concatenate([kdec[...]] * (RET_QK // LANES), axis=1)).astype(BF16)
        kv = lax.dot_general(kd, v, (((0,), (0,)), ((), ())), preferred_element_type=F32)
        state[...] = state[...] * jnp.exp(lg * float(c)) + kv
        return y

    @pl.when((i == 0) | (i == nc))
    def _():
        state[...] = jnp.zeros_like(state)

    @pl.when(i < nc)
    def _():
        lg = _log_sigmoid(jnp.zeros((1, 1), F32) + decay_ref[RET_HEADS + h])

        @pl.when(i == 0)
        def _():
            set_decays(lg, False)

        base = pl.multiple_of((nc - 1 - i) * RET_STEP, RET_STEP)
        for r0 in reversed(range(0, RET_STEP, c)):
            ybwd[pl.ds(base + r0, c), :] = chunk_step(lg, r0)

    @pl.when(i >= nc)
    def _():
        lg = _log_sigmoid(jnp.zeros((1, 1), F32) + decay_ref[h])

        @pl.when(i == nc)
        def _():
            set_decays(lg, True)

        base = pl.multiple_of((i - nc) * RET_STEP, RET_STEP)
        for r0 in range(0, RET_STEP, c):
            y = chunk_step(lg, r0) + ybwd[pl.ds(base + r0, c), :]
            g = g_ref[r0:r0 + c, :].astype(F32)
            o_ref[r0:r0 + c, :] = (g * _sigmoid(g) * (_ln(y) * gain_ref[...])).astype(BF16)


def _retention(rest, decays, gn_gain, batch, seq):
    c = RET_STEP
    nc = seq // c
    rest3 = rest.reshape(batch, seq, rest.shape[1])
    gate_blk = RET_HEADS * RET_HEAD_COLS // RET_V

    def chunk(i):
        return jnp.where(i < nc, nc - 1 - i, i - nc)

    grid_spec = pltpu.PrefetchScalarGridSpec(
        num_scalar_prefetch=1,
        grid=(batch, RET_HEADS, 2 * nc),
        in_specs=[pl.BlockSpec((None, c, RET_HEAD_COLS), lambda b, h, i, d: (b, chunk(i), h)),
                  pl.BlockSpec((None, c, RET_V), lambda b, h, i, d: (b, jnp.maximum(i - nc, 0), gate_blk + h)),
                  pl.BlockSpec((1, RET_V), lambda b, h, i, d: (0, h))],
        out_specs=pl.BlockSpec((None, c, RET_V), lambda b, h, i, d: (b, jnp.maximum(i - nc, 0), h)),
        scratch_shapes=[pltpu.VMEM((RET_QK, RET_V), F32), pltpu.VMEM((seq, RET_V), F32),
                        pltpu.VMEM((RET_CHUNK, RET_CHUNK), F32), pltpu.VMEM((RET_CHUNK, LANES), F32),
                        pltpu.VMEM((RET_CHUNK, LANES), F32)],
    )
    out = pl.pallas_call(
        functools.partial(_retention_kernel, nc=nc),
        grid_spec=grid_spec,
        out_shape=jax.ShapeDtypeStruct((batch, seq, RET_HEADS * RET_V), BF16),
        compiler_params=_cparams(("arbitrary", "arbitrary", "arbitrary")),
        name="retention",
    )(decays, rest3, rest3, gn_gain)
    return out.reshape(batch * seq, RET_HEADS * RET_V)


MERGE_TM = PERM_TILE


def _merge_kernel(o0_ref, o1_ref, o2_ref, l0_ref, l1_ref, l2_ref, retg_ref, ga_ref, gr_ref, x_ref, mod_ref,
                  wa_ref, wr_ref, wo_ref, ln_ref, wrh_ref, wrl_ref, rb_ref,
                  h1_ref, u2_ref, lg_ref, on_scr, ln_scr):
    for g in (1, 2):
        dil = ATTN_PATTERNS[g][1]
        n = MERGE_TM // dil
        o_ref, l_ref = ((o1_ref, l1_ref), (o2_ref, l2_ref))[g - 1]
        for res in range(dil):
            rows = o_ref[res * n:(res + 1) * n, :].astype(F32)
            for hh in range(HEADS_PER_GROUP):
                on_scr[g - 1, hh, pl.ds(res, n, stride=dil), :] = rows[:, hh * HEAD_DIM:(hh + 1) * HEAD_DIM]
            ln_scr[g - 1, pl.ds(res, n, stride=dil), :] = l_ref[res * n:(res + 1) * n, :]

    l0, l1, l2 = l0_ref[...], ln_scr[0], ln_scr[1]
    lm = jnp.maximum(jnp.maximum(l0, l1), l2)
    e0, e1, e2 = jnp.exp(l0 - lm), jnp.exp(l1 - lm), jnp.exp(l2 - lm)
    inv = 1.0 / (e0 + e1 + e2)
    parts = []
    for hh in range(HEADS_PER_GROUP):
        sl = slice(hh * HEAD_DIM, (hh + 1) * HEAD_DIM)
        acc = (e0[:, hh:hh + 1] * o0_ref[:, sl].astype(F32)
               + e1[:, hh:hh + 1] * on_scr[0, hh]
               + e2[:, hh:hh + 1] * on_scr[1, hh])
        parts.append((acc * inv[:, hh:hh + 1]).astype(BF16))
    attn = jnp.concatenate(parts, axis=1)

    branch_a = _dot(attn, wa_ref[...])
    branch_r = _dot(retg_ref[...], wr_ref[...])
    merged = (_sigmoid(ga_ref[...].astype(F32)) * branch_a + _sigmoid(gr_ref[...].astype(F32)) * branch_r)
    y = _dot(merged.astype(BF16), wo_ref[...])

    h1 = _ln(DEEPNORM_ALPHA * x_ref[...] + mod_ref[2:3, :] * y) * ln_ref[0:1, :] + ln_ref[1:2, :]
    h1_ref[...] = h1
    u2 = _ln(h1) * (1.0 + mod_ref[4:5, :]) + mod_ref[3:4, :]
    _store_token_tiles(u2_ref, u2)
    uh, ul = _split_bf16(u2)
    lg_ref[...] = _dot(uh, wrh_ref[...]) + _dot(uh, wrl_ref[...]) + _dot(ul, wrh_ref[...]) + rb_ref[...]


def _merge(outs, lses, retg, rest, x2, mod3, wa, wr, wo, ln1, wr_hi, wr_lo, rbias, seq):
    t = x2.shape[0]
    tm = MERGE_TM
    per_seq = seq // tm
    row = lambda w: pl.BlockSpec((tm, w), lambda i: (i, 0))
    full = lambda a: pl.BlockSpec(a.shape, lambda i: (0,) * a.ndim)
    return pl.pallas_call(
        _merge_kernel,
        grid=(t // tm,),
        in_specs=[row(GROUP_WIDTH)] * 3 + [row(LANES)] * 3 + [
            row(RET_HEADS * RET_V),
            pl.BlockSpec((tm, D_MODEL), lambda i: (i, 6)),
            pl.BlockSpec((tm, D_MODEL), lambda i: (i, 7)),
            row(D_MODEL),
            pl.BlockSpec((None, 6, D_MODEL), lambda i: (i // per_seq, 0, 0)),
            full(wa), full(wr), full(wo), full(ln1), full(wr_hi), full(wr_lo), full(rbias)],
        out_specs=[row(D_MODEL), pl.BlockSpec((tm * ROW_TILE, LANES), lambda i: (i, 0)), row(LANES)],
        out_shape=[jax.ShapeDtypeStruct((t, D_MODEL), F32),
                   jax.ShapeDtypeStruct((t * ROW_TILE, LANES), F32),
                   jax.ShapeDtypeStruct((t, LANES), F32)],
        scratch_shapes=[pltpu.VMEM((2, HEADS_PER_GROUP, tm, HEAD_DIM), F32), pltpu.VMEM((2, tm, LANES), F32)],
        compiler_params=_cparams(("arbitrary",)),
        name="merge",
    )(*outs, *lses, retg, rest, rest, x2, mod3, wa, wr, wo, ln1, wr_hi, wr_lo, rbias)


ROUTE_TM = 512
BIG = 1 << 20


def _route_kernel(lg_ref, cols_ref, ints_ref, cnt_ref, carry):
    i = pl.program_id(0)

    @pl.when(i == 0)
    def _():
        carry[...] = jnp.zeros_like(carry)

    tm = ROUTE_TM
    lg = lg_ref[...]
    lane = lax.broadcasted_iota(I32, (tm, LANES), 1)
    lane_f = lane.astype(F32)
    first = lambda mask: jnp.min(jnp.where(mask, lane_f, float(BIG)), axis=-1, keepdims=True).astype(I32)

    coarse = jnp.where(lane < N_GROUPS, lg, NEG)
    cmax = jnp.max(coarse, axis=-1, keepdims=True)
    gsel = first(coarse == cmax)
    p_group = 1.0 / jnp.sum(jnp.exp(coarse - cmax), axis=-1, keepdims=True)

    lo = N_GROUPS + EXPERTS_PER_GROUP * gsel
    fine = jnp.where((lane >= lo) & (lane < lo + EXPERTS_PER_GROUP), lg, NEG)
    v1 = jnp.max(fine, axis=-1, keepdims=True)
    i1 = first(fine == v1)
    fine2 = jnp.where(lane == i1, NEG, fine)
    v2 = jnp.max(fine2, axis=-1, keepdims=True)
    i2 = first(fine2 == v2)
    ex = jnp.exp(v2 - v1)
    den = 1.0 / (1.0 + ex)
    gate1 = p_group * den
    gate2 = p_group * (ex * den)
    e1 = i1 - N_GROUPS
    e2 = i2 - N_GROUPS

    oh1 = lane == e1
    oh2 = lane == e2
    cnt = jnp.where(oh1 | oh2, 1.0, 0.0)
    r_i = lax.broadcasted_iota(I32, (tm, tm), 0)
    c_i = lax.broadcasted_iota(I32, (tm, tm), 1)
    tri = jnp.where(r_i > c_i, 1.0, 0.0).astype(BF16)
    rank = _dot(tri, cnt.astype(BF16)) + carry[...]
    r1 = jnp.sum(jnp.where(oh1, rank, 0.0), axis=-1, keepdims=True)
    r2 = jnp.sum(jnp.where(oh2, rank, 0.0), axis=-1, keepdims=True)
    carry[...] = carry[...] + jnp.sum(cnt, axis=0, keepdims=True)
    cnt_ref[...] = jnp.broadcast_to(carry[...], cnt_ref.shape)

    cols_ref[...] = jnp.where(lane == 0, gate1, jnp.where(lane == 1, gate2, 0.0))
    packed = jnp.where(lane == 0, e1.astype(F32),
                       jnp.where(lane == 1, e2.astype(F32),
                                 jnp.where(lane == 2, r1, jnp.where(lane == 3, r2, 0.0))))
    ints_ref[...] = packed.T[0:8, :].astype(I32)


def _route(logits):
    t = logits.shape[0]
    tm = ROUTE_TM
    return pl.pallas_call(
        _route_kernel,
        grid=(t // tm,),
        in_specs=[pl.BlockSpec((tm, LANES), lambda i: (i, 0))],
        out_specs=[pl.BlockSpec((tm, LANES), lambda i: (i, 0)),
                   pl.BlockSpec((8, tm), lambda i: (0, i)),
                   pl.BlockSpec((8, LANES), lambda i: (0, 0))],
        out_shape=[jax.ShapeDtypeStruct((t, LANES), F32),
                   jax.ShapeDtypeStruct((8, t), I32),
                   jax.ShapeDtypeStruct((8, LANES), F32)],
        scratch_shapes=[pltpu.VMEM((1, LANES), F32)],
        compiler_params=_cparams(("arbitrary",)),
        name="route",
    )(logits)


def _plan_kernel(ints_ref, cnt_ref, dest_ref, meta_ref, *, n_blocks_pad):
    sub = lax.broadcasted_iota(I32, (LANES, LANES), 0)
    lane = lax.broadcasted_iota(I32, (LANES, LANES), 1)
    cnt = cnt_ref[0:1, :]
    nblk_row = jnp.floor((cnt + (MOE_BLK - 1.0)) * (1.0 / MOE_BLK))
    nblk_mat = jnp.broadcast_to(nblk_row, (LANES, LANES))
    start_col = jnp.sum(jnp.where(lane < sub, nblk_mat, 0.0), axis=-1, keepdims=True)
    nblk_col = jnp.sum(jnp.where(lane == sub, nblk_mat, 0.0), axis=-1, keepdims=True)
    end_col = start_col + nblk_col

    ints = ints_ref[...]
    base = jnp.zeros(ints.shape, F32)
    for e in range(N_EXPERTS):
        base = jnp.where(ints == e, start_col[e:e + 1, :] * float(MOE_BLK), base)
    dest = base[0:2, :].astype(I32) + ints[2:4, :]
    dest_ref[...] = jnp.concatenate([dest, jnp.zeros((6, ints.shape[1]), I32)], axis=0)

    blk = lax.broadcasted_iota(I32, (LANES, n_blocks_pad), 1).astype(F32)
    e_sub = lax.broadcasted_iota(I32, (LANES, n_blocks_pad), 0)
    done = jnp.where((e_sub < N_EXPERTS) & (end_col <= blk), 1.0, 0.0)
    bexp = jnp.minimum(jnp.sum(done, axis=0, keepdims=True), N_EXPERTS - 1.0)
    used = jnp.sum(nblk_row, axis=-1, keepdims=True)
    row = lax.broadcasted_iota(I32, (8, n_blocks_pad), 0)
    meta = jnp.where(row == 0, bexp, jnp.where(row == 1, used, 0.0))
    meta_ref[...] = meta.astype(I32)


def _plan(ints, counts, n_blocks_pad):
    t = ints.shape[1]
    return pl.pallas_call(
        functools.partial(_plan_kernel, n_blocks_pad=n_blocks_pad),
        out_shape=[jax.ShapeDtypeStruct((8, t), I32), jax.ShapeDtypeStruct((8, n_blocks_pad), I32)],
        compiler_params=pltpu.CompilerParams(vmem_limit_bytes=VMEM_LIMIT),
        name="plan",
    )(ints, counts)


DISPATCH_TM = 512
DMA_UNROLL = 8


def _row_tile(ref, r):
    return ref.at[pl.ds(pl.multiple_of(r * ROW_TILE, ROW_TILE), ROW_TILE)]


def _dispatch_kernel(d0_ref, d1_ref, u2_ref, rows_in_ref, rows_ref, sem):
    del rows_in_ref

    def issue(t, carry):
        src = _row_tile(u2_ref, t)
        pltpu.make_async_copy(src, _row_tile(rows_ref, d0_ref[t]), sem).start(priority=0)
        pltpu.make_async_copy(src, _row_tile(rows_ref, d1_ref[t]), sem).start(priority=1)
        return carry

    lax.fori_loop(0, DISPATCH_TM, issue, 0, unroll=DMA_UNROLL)
    for _ in range(2):
        pltpu.make_async_copy(u2_ref, rows_ref.at[pl.ds(0, DISPATCH_TM * ROW_TILE)], sem).wait()


def _dispatch(dest0, dest1, u2, n_rows):
    t = u2.shape[0] // ROW_TILE
    rows0 = jnp.zeros((n_rows * ROW_TILE, LANES), F32)
    idx = pl.BlockSpec((DISPATCH_TM,), lambda i: (i,), memory_space=pltpu.SMEM)
    return pl.pallas_call(
        _dispatch_kernel,
        grid=(t // DISPATCH_TM,),
        in_specs=[idx, idx,
                  pl.BlockSpec((DISPATCH_TM * ROW_TILE, LANES), lambda i: (i, 0)),
                  pl.BlockSpec(memory_space=pl.ANY)],
        out_specs=pl.BlockSpec(memory_space=pl.ANY),
        out_shape=jax.ShapeDtypeStruct((n_rows * ROW_TILE, LANES), F32),
        scratch_shapes=[pltpu.SemaphoreType.DMA(())],
        input_output_aliases={3: 0},
        compiler_params=_cparams(("arbitrary",)),
        name="dispatch",
    )(dest0, dest1, u2, rows0)


def _experts_kernel(bexp_ref, used_ref, x_ref, w1_ref, w3_ref, w2_ref, y_ref, w1s, w3s, w2s):
    i = pl.program_id(0)
    prev = bexp_ref[jnp.maximum(i - 1, 0)]
    active = i < used_ref[0]

    @pl.when(active & ((i == 0) | (bexp_ref[i] != prev)))
    def _():
        w1s[...] = w1_ref[...].astype(BF16)
        w3s[...] = w3_ref[...].astype(BF16)
        w2s[...] = w2_ref[...].astype(BF16)

    @pl.when(active)
    def _():
        xb = _load_token_tiles(x_ref, MOE_BLK).astype(BF16)
        a = _dot(xb, w1s[...])
        b = _dot(xb, w3s[...])
        hdn = (a * _sigmoid(a) * b).astype(BF16)
        _store_token_tiles(y_ref, _dot(hdn, w2s[...]))

    @pl.when(jnp.logical_not(active))
    def _():
        y_ref[...] = jnp.zeros_like(y_ref)


def _experts(bexp, used, rows, w1, w3, w2):
    n_blocks = rows.shape[0] // (MOE_BLK * ROW_TILE)
    row_blk = pl.BlockSpec((MOE_BLK * ROW_TILE, LANES), lambda i, be, nu: (i, 0))
    grid_spec = pltpu.PrefetchScalarGridSpec(
        num_scalar_prefetch=2,
        grid=(n_blocks,),
        in_specs=[row_blk,
                  pl.BlockSpec((None, D_MODEL, EXPERT_FF), lambda i, be, nu: (be[i], 0, 0)),
                  pl.BlockSpec((None, D_MODEL, EXPERT_FF), lambda i, be, nu: (be[i], 0, 0)),
                  pl.BlockSpec((None, EXPERT_FF, D_MODEL), lambda i, be, nu: (be[i], 0, 0))],
        out_specs=row_blk,
        scratch_shapes=[pltpu.VMEM((D_MODEL, EXPERT_FF), BF16), pltpu.VMEM((D_MODEL, EXPERT_FF), BF16),
                        pltpu.VMEM((EXPERT_FF, D_MODEL), BF16)],
    )
    return pl.pallas_call(
        _experts_kernel,
        grid_spec=grid_spec,
        out_shape=jax.ShapeDtypeStruct(rows.shape, F32),
        compiler_params=_cparams(("arbitrary",)),
        name="experts",
    )(bexp, used, rows, w1, w3, w2)


COMBINE_TM = 256


def _combine_kernel(d0_ref, d1_ref, y_ref, cols_ref, h1_ref, mod_ref, ln_ref, o_ref, ya, yb, sem):
    def issue(t, carry):
        pltpu.make_async_copy(_row_tile(y_ref, d0_ref[t]), _row_tile(ya, t), sem).start(priority=0)
        pltpu.make_async_copy(_row_tile(y_ref, d1_ref[t]), _row_tile(yb, t), sem).start(priority=1)
        return carry

    lax.fori_loop(0, COMBINE_TM, issue, 0, unroll=DMA_UNROLL)
    for buf in (ya, yb):
        pltpu.make_async_copy(y_ref.at[pl.ds(0, COMBINE_TM * ROW_TILE)], buf, sem).wait()

    cols = cols_ref[...]
    moe = (cols[:, 0:1] * _load_token_tiles(ya, COMBINE_TM)
           + cols[:, 1:2] * _load_token_tiles(yb, COMBINE_TM))
    pre = DEEPNORM_ALPHA * h1_ref[...] + mod_ref[5:6, :] * moe
    o_ref[...] = _ln(pre) * ln_ref[0:1, :] + ln_ref[1:2, :]


def _combine(dest0, dest1, y_rows, cols, h1, mod3, ln2, seq):
    t = h1.shape[0]
    tm = COMBINE_TM
    per_seq = seq // tm
    idx = pl.BlockSpec((tm,), lambda i: (i,), memory_space=pltpu.SMEM)
    return pl.pallas_call(
        _combine_kernel,
        grid=(t // tm,),
        in_specs=[idx, idx,
                  pl.BlockSpec(memory_space=pl.ANY),
                  pl.BlockSpec((tm, LANES), lambda i: (i, 0)),
                  pl.BlockSpec((tm, D_MODEL), lambda i: (i, 0)),
                  pl.BlockSpec((None, 6, D_MODEL), lambda i: (i // per_seq, 0, 0)),
                  pl.BlockSpec((2, D_MODEL), lambda i: (0, 0))],
        out_specs=pl.BlockSpec((tm, D_MODEL), lambda i: (i, 0)),
        out_shape=jax.ShapeDtypeStruct((t, D_MODEL), F32),
        scratch_shapes=[pltpu.VMEM((tm * ROW_TILE, LANES), F32), pltpu.VMEM((tm * ROW_TILE, LANES), F32),
                        pltpu.SemaphoreType.DMA(())],
        compiler_params=_cparams(("arbitrary",)),
        name="combine",
    )(dest0, dest1, y_rows, cols, h1, mod3, ln2)


def _layer(h, c8, w_ada, b_ada, w_in, w_attn_out, decay_f, decay_b, gn_gain, w_ret_out, w_out,
           ln1_gain, ln1_bias, w_coarse, b_coarse, w_fine, b_fine, w1, w3, w2, ln2_gain, ln2_bias):
    batch, seq, d = h.shape
    t = batch * seq
    x2 = h.reshape(t, d)

    mod = _ada(c8, w_ada, b_ada.reshape(1, -1))
    mod3 = mod[:batch].reshape(batch, 6, d)

    aw = ATTN_HEADS * HEAD_DIM
    wq, wk, wv = (w_in[:, s * aw:(s + 1) * aw] for s in range(3))
    gcols = lambda w, g: w[:, g * GROUP_WIDTH:(g + 1) * GROUP_WIDTH]
    w_qkv = jnp.concatenate([gcols(w, g) for g in range(ATTN_GROUPS) for w in (wq, wk, wv)], axis=1).astype(BF16)
    rq0, rk0, rv0, tail0 = 3 * aw, 3 * aw + RET_HEADS * RET_QK, 3 * aw + 2 * RET_HEADS * RET_QK, \
        3 * aw + 2 * RET_HEADS * RET_QK + RET_HEADS * RET_V
    head_cols = [w_in[:, o + hh * wd:o + (hh + 1) * wd]
                 for hh in range(RET_HEADS) for o, wd in ((rq0, RET_QK), (rk0, RET_QK), (rv0, RET_V))]
    w_rest = jnp.concatenate(head_cols + [w_in[:, tail0:]], axis=1).astype(BF16)

    qkv = _inproj_attn(x2, mod3, w_qkv, seq)
    rest = _inproj_rest(x2, mod3, w_rest, seq)

    outs, lses = zip(*[_attention(qkv, g, batch, seq) for g in range(ATTN_GROUPS)])
    decays = jnp.concatenate([decay_f, decay_b]).astype(F32)
    retg = _retention(rest, decays, gn_gain.reshape(1, -1), batch, seq)

    w_route = jnp.concatenate([w_coarse] + [w_fine[g] for g in range(N_GROUPS)], axis=1)
    n_route = w_route.shape[1]
    w_route = jnp.pad(w_route, ((0, 0), (0, LANES - n_route)))
    wr_hi, wr_lo = _split_bf16(w_route)
    rbias = jnp.pad(jnp.concatenate([b_coarse, b_fine.reshape(-1)]), (0, LANES - n_route)).reshape(1, LANES)
    ln1 = jnp.stack([ln1_gain, ln1_bias])
    h1, u2, logits = _merge(outs, lses, retg, rest, x2, mod3, w_attn_out.astype(BF16), w_ret_out.astype(BF16),
                            w_out.astype(BF16), ln1, wr_hi, wr_lo, rbias, seq)

    cols, ints, counts = _route(logits)
    n_blocks = 2 * t // MOE_BLK + N_EXPERTS
    n_blocks_pad = -(-n_blocks // LANES) * LANES
    dest, meta = _plan(ints, counts, n_blocks_pad)
    dest0, dest1 = dest[0], dest[1]
    rows = _dispatch(dest0, dest1, u2, n_blocks * MOE_BLK)
    y_rows = _experts(meta[0, :n_blocks], meta[1, :1], rows, w1, w3, w2)
    out = _combine(dest0, dest1, y_rows, cols, h1, mod3, jnp.stack([ln2_gain, ln2_bias]), seq)
    return out.reshape(batch, seq, d)


def kernel(x, c, w_ada, b_ada, w_in, w_attn_out, ret_decay_fwd, ret_decay_bwd, ret_gn_gain, w_ret_out, w_out,
           ln1_gain, ln1_bias, w_coarse, b_coarse, w_fine, b_fine, w1, w3, w2, ln2_gain, ln2_bias):
    batch = x.shape[0]
    assert batch <= 8 and x.shape[1] % (2 * INPROJ_TM) == 0 and x.shape[2] == D_MODEL
    c8 = jnp.pad(c, ((0, 8 - batch), (0, 0)))
    h = x
    for l in range(w_ada.shape[0]):
        h = _layer(h, c8, w_ada[l], b_ada[l], w_in[l], w_attn_out[l], ret_decay_fwd[l], ret_decay_bwd[l],
                   ret_gn_gain[l], w_ret_out[l], w_out[l], ln1_gain[l], ln1_bias[l], w_coarse[l], b_coarse[l],
                   w_fine[l], b_fine[l], w1[l], w3[l], w2[l], ln2_gain[l], ln2_bias[l])
    return h
```

```python
import functools
import math

import jax
import jax.numpy as jnp
from jax import lax
from jax.experimental import pallas as pl
from jax.experimental.pallas import tpu as pltpu

F32 = jnp.float32
BF16 = jnp.bfloat16
I32 = jnp.int32

D_MODEL = 1024
ATTN_GROUPS = 3
HEADS_PER_GROUP = 4
HEAD_DIM = 128
ATTN_HEADS = ATTN_GROUPS * HEADS_PER_GROUP
GROUP_WIDTH = HEADS_PER_GROUP * HEAD_DIM
ATTN_PATTERNS = ((128, 1), (512, 4), (2048, 16))
ALIBI_MAX_EXP = 8.0
NEG = -1e30
RET_HEADS = 4
RET_QK = 256
RET_V = 512
N_GROUPS = 4
EXPERTS_PER_GROUP = 8
N_EXPERTS = N_GROUPS * EXPERTS_PER_GROUP
EXPERT_FF = 512
DEPTH = 1
DEEPNORM_ALPHA = (2.0 * DEPTH) ** 0.25
LN_EPS = 1e-5

LANES = 128
PERM_TILE = 512
ATTN_QB = 128
ATTN_HALF = 64
ATTN_KB = ATTN_QB + 2 * ATTN_HALF
RET_CHUNK = 256
RET_STEP = 1024
MOE_BLK = 512
VMEM_LIMIT = 56 * 1024 * 1024


def _cparams(sem):
    return pltpu.CompilerParams(dimension_semantics=sem, vmem_limit_bytes=VMEM_LIMIT)


def _split_bf16(a):
    hi = a.astype(BF16)
    lo = (a - hi.astype(F32)).astype(BF16)
    return hi, lo


def _dot(a, b):
    return jnp.dot(a, b, preferred_element_type=F32)


def _dot3(a, b):
    ah, al = _split_bf16(a)
    bh, bl = _split_bf16(b)
    return _dot(ah, bh) + _dot(ah, bl) + _dot(al, bh)


def _ln(x):
    mu = jnp.mean(x, axis=-1, keepdims=True)
    xc = x - mu
    var = jnp.mean(xc * xc, axis=-1, keepdims=True)
    return xc * lax.rsqrt(var + LN_EPS)


def _sigmoid(x):
    return 1.0 / (1.0 + jnp.exp(-x))


ROW_TILE = D_MODEL // LANES


def _store_token_tiles(ref, val):
    n = val.shape[0]
    for cc in range(ROW_TILE):
        ref[pl.ds(cc, n, stride=ROW_TILE), :] = val[:, cc * LANES:(cc + 1) * LANES]


def _load_token_tiles(ref, n):
    return jnp.concatenate([ref[pl.ds(cc, n, stride=ROW_TILE), :] for cc in range(ROW_TILE)], axis=1)


def _ada_kernel(c_ref, w_ref, b_ref, o_ref):
    o_ref[...] = _dot3(c_ref[...], w_ref[...]) + b_ref[...]


def _ada(c8, w_ada, b_ada):
    n = w_ada.shape[1]
    return pl.pallas_call(
        _ada_kernel,
        grid=(n // D_MODEL,),
        in_specs=[pl.BlockSpec((8, D_MODEL), lambda j: (0, 0)),
                  pl.BlockSpec((D_MODEL, D_MODEL), lambda j: (0, j)),
                  pl.BlockSpec((1, D_MODEL), lambda j: (0, j))],
        out_specs=pl.BlockSpec((8, D_MODEL), lambda j: (0, j)),
        out_shape=jax.ShapeDtypeStruct((8, n), F32),
        compiler_params=_cparams(("arbitrary",)),
        name="ada",
    )(c8, w_ada, b_ada)


INPROJ_TM = 1024
LN_ROWS = 256


def _modulated_ln(x_ref, mod_ref, r0, rows):
    x = x_ref[r0:r0 + rows, :]
    return _ln(x) * (1.0 + mod_ref[1:2, :]) + mod_ref[0:1, :]


def _inproj_attn_kernel(x_ref, mod_ref, w_ref, o_ref, uf_scr, u_scr):
    j = pl.program_id(1)

    @pl.when(j == 0)
    def _():
        for r0 in range(0, INPROJ_TM, LN_ROWS):
            u = _modulated_ln(x_ref, mod_ref, r0, LN_ROWS)
            for cc in range(D_MODEL // LANES):
                uf_scr[cc, r0:r0 + LN_ROWS, :] = u[:, cc * LANES:(cc + 1) * LANES]
            u_scr[0, r0:r0 + LN_ROWS, :] = u.astype(BF16)
        for g in (1, 2):
            dil = ATTN_PATTERNS[g][1]
            n = PERM_TILE // dil
            for t0 in range(0, INPROJ_TM, PERM_TILE):
                for res in range(dil):
                    for cc in range(D_MODEL // LANES):
                        rows = uf_scr[cc, pl.ds(t0 + res, n, stride=dil), :]
                        u_scr[g, t0 + res * n:t0 + (res + 1) * n, cc * LANES:(cc + 1) * LANES] = rows.astype(BF16)

    o_ref[...] = _dot(u_scr[j], w_ref[...]).astype(BF16)


def _inproj_attn(x2, mod3, w_qkv, seq):
    t = x2.shape[0]
    tn = 3 * GROUP_WIDTH
    per_seq = seq // INPROJ_TM
    return pl.pallas_call(
        _inproj_attn_kernel,
        grid=(t // INPROJ_TM, ATTN_GROUPS),
        in_specs=[pl.BlockSpec((INPROJ_TM, D_MODEL), lambda i, j: (i, 0)),
                  pl.BlockSpec((None, 6, D_MODEL), lambda i, j: (i // per_seq, 0, 0)),
                  pl.BlockSpec((D_MODEL, tn), lambda i, j: (0, j))],
        out_specs=pl.BlockSpec((INPROJ_TM, tn), lambda i, j: (i, j)),
        out_shape=jax.ShapeDtypeStruct((t, ATTN_GROUPS * tn), BF16),
        scratch_shapes=[pltpu.VMEM((D_MODEL // LANES, INPROJ_TM, LANES), F32),
                        pltpu.VMEM((ATTN_GROUPS, INPROJ_TM, D_MODEL), BF16)],
        compiler_params=_cparams(("arbitrary", "arbitrary")),
        name="inproj_attn",
    )(x2, mod3, w_qkv)


def _inproj_rest_kernel(x_ref, mod_ref, w_ref, o_ref, u_scr):
    @pl.when(pl.program_id(1) == 0)
    def _():
        for r0 in range(0, INPROJ_TM, LN_ROWS):
            u_scr[r0:r0 + LN_ROWS, :] = _modulated_ln(x_ref, mod_ref, r0, LN_ROWS).astype(BF16)

    o_ref[...] = _dot(u_scr[...], w_ref[...]).astype(BF16)


def _inproj_rest(x2, mod3, w_rest, seq):
    t = x2.shape[0]
    n = w_rest.shape[1]
    tn = 1024
    per_seq = seq // INPROJ_TM
    return pl.pallas_call(
        _inproj_rest_kernel,
        grid=(t // INPROJ_TM, n // tn),
        in_specs=[pl.BlockSpec((INPROJ_TM, D_MODEL), lambda i, j: (i, 0)),
                  pl.BlockSpec((None, 6, D_MODEL), lambda i, j: (i // per_seq, 0, 0)),
                  pl.BlockSpec((D_MODEL, tn), lambda i, j: (0, j))],
        out_specs=pl.BlockSpec((INPROJ_TM, tn), lambda i, j: (i, j)),
        out_shape=jax.ShapeDtypeStruct((t, n), BF16),
        scratch_shapes=[pltpu.VMEM((INPROJ_TM, D_MODEL), BF16)],
        compiler_params=_cparams(("arbitrary", "arbitrary")),
        name="inproj_rest",
    )(x2, mod3, w_rest)


ATTN_OFFSETS = ATTN_KB // ATTN_HALF - 1


def _attn_kernel(q_ref, k_ref, v_ref, o_ref, lse_ref, bias_scr, *scratch, group, dil, nt, n):
    n_sub = nt * n
    if nt == 1:
        qs, ks, vs, os_, ls = q_ref.at[0], k_ref.at[0], v_ref.at[0], o_ref.at[0], lse_ref.at[0]
    else:
        qs, ks, vs, os_, ls = scratch
        for t in range(nt):
            qs[t * n:(t + 1) * n, :] = q_ref[t]
            ks[t * n:(t + 1) * n, :] = k_ref[t]
            vs[t * n:(t + 1) * n, :] = v_ref[t]

    @pl.when((pl.program_id(0) == 0) & (pl.program_id(1) == 0))
    def _():
        base = (lax.broadcasted_iota(I32, (ATTN_QB, ATTN_KB), 1)
                - lax.broadcasted_iota(I32, (ATTN_QB, ATTN_KB), 0))
        for j in range(ATTN_OFFSETS):
            dist = jnp.abs(base - j * ATTN_HALF)
            for hh in range(HEADS_PER_GROUP):
                slope = dil * 2.0 ** (-ALIBI_MAX_EXP * (group * HEADS_PER_GROUP + hh + 1) / ATTN_HEADS)
                bias_scr[hh * ATTN_OFFSETS + j] = jnp.where(dist <= ATTN_HALF, -slope * dist.astype(F32), NEG)

    scale = HEAD_DIM ** -0.5
    lane = lax.broadcasted_iota(I32, (ATTN_QB, LANES), 1)

    def block(blk, carry):
        q0 = pl.multiple_of(blk * ATTN_QB, ATTN_QB)
        start = pl.multiple_of(jnp.clip(q0 - ATTN_HALF, 0, n_sub - ATTN_KB), ATTN_HALF)
        j = (q0 - start) // ATTN_HALF
        lse_tile = jnp.zeros((ATTN_QB, LANES), F32)
        for hh in range(HEADS_PER_GROUP):
            cs = slice(hh * HEAD_DIM, (hh + 1) * HEAD_DIM)
            qb = qs[pl.ds(q0, ATTN_QB), cs]
            kb = ks[pl.ds(start, ATTN_KB), cs]
            vb = vs[pl.ds(start, ATTN_KB), cs]
            s = lax.dot_general(qb, kb, (((1,), (1,)), ((), ())), preferred_element_type=F32) * scale
            s = s + bias_scr[hh * ATTN_OFFSETS + j]
            m = jnp.max(s, axis=-1, keepdims=True)
            p = jnp.exp(s - m)
            l = jnp.sum(p, axis=-1, keepdims=True)
            o = _dot(p.astype(BF16), vb) * (1.0 / l)
            os_[pl.ds(q0, ATTN_QB), cs] = o.astype(BF16)
            lse_tile = jnp.where(lane == hh, m + jnp.log(l), lse_tile)
        ls[pl.ds(q0, ATTN_QB), :] = lse_tile
        return carry

    lax.fori_loop(0, n_sub // ATTN_QB, block, 0, unroll=min(4, n_sub // ATTN_QB))

    if nt > 1:
        for t in range(nt):
            o_ref[t] = os_[t * n:(t + 1) * n, :]
            lse_ref[t] = ls[t * n:(t + 1) * n, :]


def _attention(qkv, group, batch, seq):
    dil = ATTN_PATTERNS[group][1]
    if dil == 1:
        nt, n = 1, seq
    else:
        nt, n = seq // PERM_TILE, PERM_TILE // dil
    n_sub = nt * n
    t = batch * seq
    qkv5 = qkv.reshape(batch, nt, dil, n, qkv.shape[1])
    cb = group * 3
    blk = (None, nt, None, n, GROUP_WIDTH)
    scratch = [pltpu.VMEM((HEADS_PER_GROUP * ATTN_OFFSETS, ATTN_QB, ATTN_KB), F32)]
    if nt > 1:
        scratch += [pltpu.VMEM((n_sub, GROUP_WIDTH), BF16)] * 4 + [pltpu.VMEM((n_sub, LANES), F32)]
    out, lse = pl.pallas_call(
        functools.partial(_attn_kernel, group=group, dil=dil, nt=nt, n=n),
        grid=(batch, dil),
        in_specs=[pl.BlockSpec(blk, lambda b, r: (b, 0, r, 0, cb)),
                  pl.BlockSpec(blk, lambda b, r: (b, 0, r, 0, cb + 1)),
                  pl.BlockSpec(blk, lambda b, r: (b, 0, r, 0, cb + 2))],
        out_specs=[pl.BlockSpec(blk, lambda b, r: (b, 0, r, 0, 0)),
                   pl.BlockSpec((None, nt, None, n, LANES), lambda b, r: (b, 0, r, 0, 0))],
        out_shape=[jax.ShapeDtypeStruct((batch, nt, dil, n, GROUP_WIDTH), BF16),
                   jax.ShapeDtypeStruct((batch, nt, dil, n, LANES), F32)],
        scratch_shapes=scratch,
        compiler_params=_cparams(("arbitrary", "arbitrary")),
        name=f"attn_g{group}",
    )(qkv5, qkv5, qkv5)
    return out.reshape(t, GROUP_WIDTH), lse.reshape(t, LANES)


def _log_sigmoid(x):
    return jnp.minimum(x, 0.0) - jnp.log(1.0 + jnp.exp(-jnp.abs(x)))


RET_HEAD_COLS = 2 * RET_QK + RET_V


def _retention_kernel(decay_ref, qkv_ref, g_ref, gain_ref, o_ref, state, ybwd, dmat, kdec, qdec, *, nc):
    c = RET_CHUNK
    h = pl.program_id(1)
    i = pl.program_id(2)
    kscale = RET_QK ** -0.5

    def set_decays(lg, forward):
        row = lax.broadcasted_iota(I32, (c, c), 0)
        col = lax.broadcasted_iota(I32, (c, c), 1)
        pos = lax.broadcasted_iota(I32, (c, LANES), 0).astype(F32)
        if forward:
            gap, key_pow, query_pow = row - col, (c - 1.0) - pos, pos + 1.0
            keep = gap >= 0
        else:
            gap, key_pow, query_pow = col - row, pos, c - pos
            keep = gap > 0
        dmat[...] = jnp.where(keep, jnp.exp(lg * jnp.maximum(gap, 0).astype(F32)) * kscale, 0.0)
        kdec[...] = jnp.exp(lg * key_pow) * kscale
        qdec[...] = jnp.exp(lg * query_pow)

    def chunk_step(lg, r0):
        q = qkv_ref[r0:r0 + c, 0:RET_QK]
        k = qkv_ref[r0:r0 + c, RET_QK:2 * RET_QK]
        v = qkv_ref[r0:r0 + c, 2 * RET_QK:]
        inner = lax.dot_general(q, k, (((1,), (1,)), ((), ())), preferred_element_type=F32) * dmat[...]
        y = _dot(inner.astype(BF16), v)
        qd = jnp.concatenate([qdec[...]] * (RET_V // LANES), axis=1)
        y = y + _dot(q, state[...].astype(BF16)) * qd
        kd = (k.astype(F32) * jnp.concatenate([kdec[...]] * (RET_QK // LANES), axis=1)).astype(BF16)
        kv = lax.dot_general(kd, v, (((0,), (0,)), ((), ())), preferred_element_type=F32)
        state[...] = state[...] * jnp.exp(lg * float(c)) + kv
        return y

    @pl.when((i == 0) | (i == nc))
    def _():
        state[...] = jnp.zeros_like(state)

    @pl.when(i < nc)
    def _():
        lg = _log_sigmoid(jnp.zeros((1, 1), F32) + decay_ref[RET_HEADS + h])

        @pl.when(i == 0)
        def _():
            set_decays(lg, False)

        base = pl.multiple_of((nc - 1 - i) * RET_STEP, RET_STEP)
        for r0 in reversed(range(0, RET_STEP, c)):
            ybwd[pl.ds(base + r0, c), :] = chunk_step(lg, r0)

    @pl.when(i >= nc)
    def _():
        lg = _log_sigmoid(jnp.zeros((1, 1), F32) + decay_ref[h])

        @pl.when(i == nc)
        def _():
            set_decays(lg, True)

        base = pl.multiple_of((i - nc) * RET_STEP, RET_STEP)
        for r0 in range(0, RET_STEP, c):
            y = chunk_step(lg, r0) + ybwd[pl.ds(base + r0, c), :]
            g = g_ref[r0:r0 + c, :].astype(F32)
            o_ref[r0:r0 + c, :] = (g * _sigmoid(g) * (_ln(y) * gain_ref[...])).astype(BF16)


def _retention(rest, decays, gn_gain, batch, seq):
    c = RET_STEP
    nc = seq // c
    rest3 = rest.reshape(batch, seq, rest.shape[1])
    gate_blk = RET_HEADS * RET_HEAD_COLS // RET_V

    def chunk(i):
        return jnp.where(i < nc, nc - 1 - i, i - nc)

    grid_spec = pltpu.PrefetchScalarGridSpec(
        num_scalar_prefetch=1,
        grid=(batch, RET_HEADS, 2 * nc),
        in_specs=[pl.BlockSpec((None, c, RET_HEAD_COLS), lambda b, h, i, d: (b, chunk(i), h)),
                  pl.BlockSpec((None, c, RET_V), lambda b, h, i, d: (b, jnp.maximum(i - nc, 0), gate_blk + h)),
                  pl.BlockSpec((1, RET_V), lambda b, h, i, d: (0, h))],
        out_specs=pl.BlockSpec((None, c, RET_V), lambda b, h, i, d: (b, jnp.maximum(i - nc, 0), h)),
        scratch_shapes=[pltpu.VMEM((RET_QK, RET_V), F32), pltpu.VMEM((seq, RET_V), F32),
                        pltpu.VMEM((RET_CHUNK, RET_CHUNK), F32), pltpu.VMEM((RET_CHUNK, LANES), F32),
                        pltpu.VMEM((RET_CHUNK, LANES), F32)],
    )
    out = pl.pallas_call(
        functools.partial(_retention_kernel, nc=nc),
        grid_spec=grid_spec,
        out_shape=jax.ShapeDtypeStruct((batch, seq, RET_HEADS * RET_V), BF16),
        compiler_params=_cparams(("arbitrary", "arbitrary", "arbitrary")),
        name="retention",
    )(decays, rest3, rest3, gn_gain)
    return out.reshape(batch * seq, RET_HEADS * RET_V)


MERGE_TM = PERM_TILE


def _merge_kernel(o0_ref, o1_ref, o2_ref, l0_ref, l1_ref, l2_ref, retg_ref, ga_ref, gr_ref, x_ref, mod_ref,
                  wa_ref, wr_ref, wo_ref, ln_ref, wrh_ref, wrl_ref, rb_ref,
                  h1_ref, u2_ref, lg_ref, on_scr, ln_scr):
    for g in (1, 2):
        dil = ATTN_PATTERNS[g][1]
        n = MERGE_TM // dil
        o_ref, l_ref = ((o1_ref, l1_ref), (o2_ref, l2_ref))[g - 1]
        for res in range(dil):
            rows = o_ref[res * n:(res + 1) * n, :].astype(F32)
            for hh in range(HEADS_PER_GROUP):
                on_scr[g - 1, hh, pl.ds(res, n, stride=dil), :] = rows[:, hh * HEAD_DIM:(hh + 1) * HEAD_DIM]
            ln_scr[g - 1, pl.ds(res, n, stride=dil), :] = l_ref[res * n:(res + 1) * n, :]

    l0, l1, l2 = l0_ref[...], ln_scr[0], ln_scr[1]
    lm = jnp.maximum(jnp.maximum(l0, l1), l2)
    e0, e1, e2 = jnp.exp(l0 - lm), jnp.exp(l1 - lm), jnp.exp(l2 - lm)
    inv = 1.0 / (e0 + e1 + e2)
    parts = []
    for hh in range(HEADS_PER_GROUP):
        sl = slice(hh * HEAD_DIM, (hh + 1) * HEAD_DIM)
        acc = (e0[:, hh:hh + 1] * o0_ref[:, sl].astype(F32)
               + e1[:, hh:hh + 1] * on_scr[0, hh]
               + e2[:, hh:hh + 1] * on_scr[1, hh])
        parts.append((acc * inv[:, hh:hh + 1]).astype(BF16))
    attn = jnp.concatenate(parts, axis=1)

    branch_a = _dot(attn, wa_ref[...])
    branch_r = _dot(retg_ref[...], wr_ref[...])
    merged = (_sigmoid(ga_ref[...].astype(F32)) * branch_a + _sigmoid(gr_ref[...].astype(F32)) * branch_r)
    y = _dot(merged.astype(BF16), wo_ref[...])

    h1 = _ln(DEEPNORM_ALPHA * x_ref[...] + mod_ref[2:3, :] * y) * ln_ref[0:1, :] + ln_ref[1:2, :]
    h1_ref[...] = h1
    u2 = _ln(h1) * (1.0 + mod_ref[4:5, :]) + mod_ref[3:4, :]
    _store_token_tiles(u2_ref, u2)
    uh, ul = _split_bf16(u2)
    lg_ref[...] = _dot(uh, wrh_ref[...]) + _dot(uh, wrl_ref[...]) + _dot(ul, wrh_ref[...]) + rb_ref[...]


def _merge(outs, lses, retg, rest, x2, mod3, wa, wr, wo, ln1, wr_hi, wr_lo, rbias, seq):
    t = x2.shape[0]
    tm = MERGE_TM
    per_seq = seq // tm
    row = lambda w: pl.BlockSpec((tm, w), lambda i: (i, 0))
    full = lambda a: pl.BlockSpec(a.shape, lambda i: (0,) * a.ndim)
    return pl.pallas_call(
        _merge_kernel,
        grid=(t // tm,),
        in_specs=[row(GROUP_WIDTH)] * 3 + [row(LANES)] * 3 + [
            row(RET_HEADS * RET_V),
            pl.BlockSpec((tm, D_MODEL), lambda i: (i, 6)),
            pl.BlockSpec((tm, D_MODEL), lambda i: (i, 7)),
            row(D_MODEL),
            pl.BlockSpec((None, 6, D_MODEL), lambda i: (i // per_seq, 0, 0)),
            full(wa), full(wr), full(wo), full(ln1), full(wr_hi), full(wr_lo), full(rbias)],
        out_specs=[row(D_MODEL), pl.BlockSpec((tm * ROW_TILE, LANES), lambda i: (i, 0)), row(LANES)],
        out_shape=[jax.ShapeDtypeStruct((t, D_MODEL), F32),
                   jax.ShapeDtypeStruct((t * ROW_TILE, LANES), F32),
                   jax.ShapeDtypeStruct((t, LANES), F32)],
        scratch_shapes=[pltpu.VMEM((2, HEADS_PER_GROUP, tm, HEAD_DIM), F32), pltpu.VMEM((2, tm, LANES), F32)],
        compiler_params=_cparams(("arbitrary",)),
        name="merge",
    )(*outs, *lses, retg, rest, rest, x2, mod3, wa, wr, wo, ln1, wr_hi, wr_lo, rbias)


ROUTE_TM = 512
BIG = 1 << 20


def _route_kernel(lg_ref, cols_ref, ints_ref, cnt_ref, carry):
    i = pl.program_id(0)

    @pl.when(i == 0)
    def _():
        carry[...] = jnp.zeros_like(carry)

    tm = ROUTE_TM
    lg = lg_ref[...]
    lane = lax.broadcasted_iota(I32, (tm, LANES), 1)
    lane_f = lane.astype(F32)
    first = lambda mask: jnp.min(jnp.where(mask, lane_f, float(BIG)), axis=-1, keepdims=True).astype(I32)

    coarse = jnp.where(lane < N_GROUPS, lg, NEG)
    cmax = jnp.max(coarse, axis=-1, keepdims=True)
    gsel = first(coarse == cmax)
    p_group = 1.0 / jnp.sum(jnp.exp(coarse - cmax), axis=-1, keepdims=True)

    lo = N_GROUPS + EXPERTS_PER_GROUP * gsel
    fine = jnp.where((lane >= lo) & (lane < lo + EXPERTS_PER_GROUP), lg, NEG)
    v1 = jnp.max(fine, axis=-1, keepdims=True)
    i1 = first(fine == v1)
    fine2 = jnp.where(lane == i1, NEG, fine)
    v2 = jnp.max(fine2, axis=-1, keepdims=True)
    i2 = first(fine2 == v2)
    ex = jnp.exp(v2 - v1)
    den = 1.0 / (1.0 + ex)
    gate1 = p_group * den
    gate2 = p_group * (ex * den)
    e1 = i1 - N_GROUPS
    e2 = i2 - N_GROUPS

    oh1 = lane == e1
    oh2 = lane == e2
    cnt = jnp.where(oh1 | oh2, 1.0, 0.0)
    r_i = lax.broadcasted_iota(I32, (tm, tm), 0)
    c_i = lax.broadcasted_iota(I32, (tm, tm), 1)
    tri = jnp.where(r_i > c_i, 1.0, 0.0).astype(BF16)
    rank = _dot(tri, cnt.astype(BF16)) + carry[...]
    r1 = jnp.sum(jnp.where(oh1, rank, 0.0), axis=-1, keepdims=True)
    r2 = jnp.sum(jnp.where(oh2, rank, 0.0), axis=-1, keepdims=True)
    carry[...] = carry[...] + jnp.sum(cnt, axis=0, keepdims=True)
    cnt_ref[...] = jnp.broadcast_to(carry[...], cnt_ref.shape)

    cols_ref[...] = jnp.where(lane == 0, gate1, jnp.where(lane == 1, gate2, 0.0))
    packed = jnp.where(lane == 0, e1.astype(F32),
                       jnp.where(lane == 1, e2.astype(F32),
                                 jnp.where(lane == 2, r1, jnp.where(lane == 3, r2, 0.0))))
    ints_ref[...] = packed.T[0:8, :].astype(I32)


def _route(logits):
    t = logits.shape[0]
    tm = ROUTE_TM
    return pl.pallas_call(
        _route_kernel,
        grid=(t // tm,),
        in_specs=[pl.BlockSpec((tm, LANES), lambda i: (i, 0))],
        out_specs=[pl.BlockSpec((tm, LANES), lambda i: (i, 0)),
                   pl.BlockSpec((8, tm), lambda i: (0, i)),
                   pl.BlockSpec((8, LANES), lambda i: (0, 0))],
        out_shape=[jax.ShapeDtypeStruct((t, LANES), F32),
                   jax.ShapeDtypeStruct((8, t), I32),
                   jax.ShapeDtypeStruct((8, LANES), F32)],
        scratch_shapes=[pltpu.VMEM((1, LANES), F32)],
        compiler_params=_cparams(("arbitrary",)),
        name="route",
    )(logits)


def _plan_kernel(ints_ref, cnt_ref, dest_ref, meta_ref, *, n_blocks_pad):
    sub = lax.broadcasted_iota(I32, (LANES, LANES), 0)
    lane = lax.broadcasted_iota(I32, (LANES, LANES), 1)
    cnt = cnt_ref[0:1, :]
    nblk_row = jnp.floor((cnt + (MOE_BLK - 1.0)) * (1.0 / MOE_BLK))
    nblk_mat = jnp.broadcast_to(nblk_row, (LANES, LANES))
    start_col = jnp.sum(jnp.where(lane < sub, nblk_mat, 0.0), axis=-1, keepdims=True)
    nblk_col = jnp.sum(jnp.where(lane == sub, nblk_mat, 0.0), axis=-1, keepdims=True)
    end_col = start_col + nblk_col

    ints = ints_ref[...]
    base = jnp.zeros(ints.shape, F32)
    for e in range(N_EXPERTS):
        base = jnp.where(ints == e, start_col[e:e + 1, :] * float(MOE_BLK), base)
    dest = base[0:2, :].astype(I32) + ints[2:4, :]
    dest_ref[...] = jnp.concatenate([dest, jnp.zeros((6, ints.shape[1]), I32)], axis=0)

    blk = lax.broadcasted_iota(I32, (LANES, n_blocks_pad), 1).astype(F32)
    e_sub = lax.broadcasted_iota(I32, (LANES, n_blocks_pad), 0)
    done = jnp.where((e_sub < N_EXPERTS) & (end_col <= blk), 1.0, 0.0)
    bexp = jnp.minimum(jnp.sum(done, axis=0, keepdims=True), N_EXPERTS - 1.0)
    used = jnp.sum(nblk_row, axis=-1, keepdims=True)
    row = lax.broadcasted_iota(I32, (8, n_blocks_pad), 0)
    meta = jnp.where(row == 0, bexp, jnp.where(row == 1, used, 0.0))
    meta_ref[...] = meta.astype(I32)


def _plan(ints, counts, n_blocks_pad):
    t = ints.shape[1]
    return pl.pallas_call(
        functools.partial(_plan_kernel, n_blocks_pad=n_blocks_pad),
        out_shape=[jax.ShapeDtypeStruct((8, t), I32), jax.ShapeDtypeStruct((8, n_blocks_pad), I32)],
        compiler_params=pltpu.CompilerParams(vmem_limit_bytes=VMEM_LIMIT),
        name="plan",
    )(ints, counts)


DISPATCH_TM = 512
DMA_UNROLL = 8


def _row_tile(ref, r):
    return ref.at[pl.ds(pl.multiple_of(r * ROW_TILE, ROW_TILE), ROW_TILE)]


def _dispatch_kernel(d0_ref, d1_ref, u2_ref, rows_in_ref, rows_ref, sem):
    del rows_in_ref

    def issue(t, carry):
        src = _row_tile(u2_ref, t)
        pltpu.make_async_copy(src, _row_tile(rows_ref, d0_ref[t]), sem).start(priority=0)
        pltpu.make_async_copy(src, _row_tile(rows_ref, d1_ref[t]), sem).start(priority=1)
        return carry

    lax.fori_loop(0, DISPATCH_TM, issue, 0, unroll=DMA_UNROLL)
    for _ in range(2):
        pltpu.make_async_copy(u2_ref, rows_ref.at[pl.ds(0, DISPATCH_TM * ROW_TILE)], sem).wait()


def _dispatch(dest0, dest1, u2, n_rows):
    t = u2.shape[0] // ROW_TILE
    rows0 = jnp.zeros((n_rows * ROW_TILE, LANES), F32)
    idx = pl.BlockSpec((DISPATCH_TM,), lambda i: (i,), memory_space=pltpu.SMEM)
    return pl.pallas_call(
        _dispatch_kernel,
        grid=(t // DISPATCH_TM,),
        in_specs=[idx, idx,
                  pl.BlockSpec((DISPATCH_TM * ROW_TILE, LANES), lambda i: (i, 0)),
                  pl.BlockSpec(memory_space=pl.ANY)],
        out_specs=pl.BlockSpec(memory_space=pl.ANY),
        out_shape=jax.ShapeDtypeStruct((n_rows * ROW_TILE, LANES), F32),
        scratch_shapes=[pltpu.SemaphoreType.DMA(())],
        input_output_aliases={3: 0},
        compiler_params=_cparams(("arbitrary",)),
        name="dispatch",
    )(dest0, dest1, u2, rows0)


def _experts_kernel(bexp_ref, used_ref, x_ref, w1_ref, w3_ref, w2_ref, y_ref, w1s, w3s, w2s):
    i = pl.program_id(0)
    prev = bexp_ref[jnp.maximum(i - 1, 0)]
    active = i < used_ref[0]

    @pl.when(active & ((i == 0) | (bexp_ref[i] != prev)))
    def _():
        w1s[...] = w1_ref[...].astype(BF16)
        w3s[...] = w3_ref[...].astype(BF16)
        w2s[...] = w2_ref[...].astype(BF16)

    @pl.when(active)
    def _():
        xb = _load_token_tiles(x_ref, MOE_BLK).astype(BF16)
        a = _dot(xb, w1s[...])
        b = _dot(xb, w3s[...])
        hdn = (a * _sigmoid(a) * b).astype(BF16)
        _store_token_tiles(y_ref, _dot(hdn, w2s[...]))

    @pl.when(jnp.logical_not(active))
    def _():
        y_ref[...] = jnp.zeros_like(y_ref)


def _experts(bexp, used, rows, w1, w3, w2):
    n_blocks = rows.shape[0] // (MOE_BLK * ROW_TILE)
    row_blk = pl.BlockSpec((MOE_BLK * ROW_TILE, LANES), lambda i, be, nu: (i, 0))
    grid_spec = pltpu.PrefetchScalarGridSpec(
        num_scalar_prefetch=2,
        grid=(n_blocks,),
        in_specs=[row_blk,
                  pl.BlockSpec((None, D_MODEL, EXPERT_FF), lambda i, be, nu: (be[i], 0, 0)),
                  pl.BlockSpec((None, D_MODEL, EXPERT_FF), lambda i, be, nu: (be[i], 0, 0)),
                  pl.BlockSpec((None, EXPERT_FF, D_MODEL), lambda i, be, nu: (be[i], 0, 0))],
        out_specs=row_blk,
        scratch_shapes=[pltpu.VMEM((D_MODEL, EXPERT_FF), BF16), pltpu.VMEM((D_MODEL, EXPERT_FF), BF16),
                        pltpu.VMEM((EXPERT_FF, D_MODEL), BF16)],
    )
    return pl.pallas_call(
        _experts_kernel,
        grid_spec=grid_spec,
        out_shape=jax.ShapeDtypeStruct(rows.shape, F32),
        compiler_params=_cparams(("arbitrary",)),
        name="experts",
    )(bexp, used, rows, w1, w3, w2)


COMBINE_TM = 256


def _combine_kernel(d0_ref, d1_ref, y_ref, cols_ref, h1_ref, mod_ref, ln_ref, o_ref, ya, yb, sem):
    def issue(t, carry):
        pltpu.make_async_copy(_row_tile(y_ref, d0_ref[t]), _row_tile(ya, t), sem).start(priority=0)
        pltpu.make_async_copy(_row_tile(y_ref, d1_ref[t]), _row_tile(yb, t), sem).start(priority=1)
        return carry

    lax.fori_loop(0, COMBINE_TM, issue, 0, unroll=DMA_UNROLL)
    for buf in (ya, yb):
        pltpu.make_async_copy(y_ref.at[pl.ds(0, COMBINE_TM * ROW_TILE)], buf, sem).wait()

    cols = cols_ref[...]
    moe = (cols[:, 0:1] * _load_token_tiles(ya, COMBINE_TM)
           + cols[:, 1:2] * _load_token_tiles(yb, COMBINE_TM))
    pre = DEEPNORM_ALPHA * h1_ref[...] + mod_ref[5:6, :] * moe
    o_ref[...] = _ln(pre) * ln_ref[0:1, :] + ln_ref[1:2, :]


def _combine(dest0, dest1, y_rows, cols, h1, mod3, ln2, seq):
    t = h1.shape[0]
    tm = COMBINE_TM
    per_seq = seq // tm
    idx = pl.BlockSpec((tm,), lambda i: (i,), memory_space=pltpu.SMEM)
    return pl.pallas_call(
        _combine_kernel,
        grid=(t // tm,),
        in_specs=[idx, idx,
                  pl.BlockSpec(memory_space=pl.ANY),
                  pl.BlockSpec((tm, LANES), lambda i: (i, 0)),
                  pl.BlockSpec((tm, D_MODEL), lambda i: (i, 0)),
                  pl.BlockSpec((None, 6, D_MODEL), lambda i: (i // per_seq, 0, 0)),
                  pl.BlockSpec((2, D_MODEL), lambda i: (0, 0))],
        out_specs=pl.BlockSpec((tm, D_MODEL), lambda i: (i, 0)),
        out_shape=jax.ShapeDtypeStruct((t, D_MODEL), F32),
        scratch_shapes=[pltpu.VMEM((tm * ROW_TILE, LANES), F32), pltpu.VMEM((tm * ROW_TILE, LANES), F32),
                        pltpu.SemaphoreType.DMA(())],
        compiler_params=_cparams(("arbitrary",)),
        name="combine",
    )(dest0, dest1, y_rows, cols, h1, mod3, ln2)


def _layer(h, c8, w_ada, b_ada, w_in, w_attn_out, decay_f, decay_b, gn_gain, w_ret_out, w_out,
           ln1_gain, ln1_bias, w_coarse, b_coarse, w_fine, b_fine, w1, w3, w2, ln2_gain, ln2_bias):
    batch, seq, d = h.shape
    t = batch * seq
    x2 = h.reshape(t, d)

    mod = _ada(c8, w_ada, b_ada.reshape(1, -1))
    mod3 = mod[:batch].reshape(batch, 6, d)

    aw = ATTN_HEADS * HEAD_DIM
    wq, wk, wv = (w_in[:, s * aw:(s + 1) * aw] for s in range(3))
    gcols = lambda w, g: w[:, g * GROUP_WIDTH:(g + 1) * GROUP_WIDTH]
    w_qkv = jnp.concatenate([gcols(w, g) for g in range(ATTN_GROUPS) for w in (wq, wk, wv)], axis=1).astype(BF16)
    rq0, rk0, rv0, tail0 = 3 * aw, 3 * aw + RET_HEADS * RET_QK, 3 * aw + 2 * RET_HEADS * RET_QK, \
        3 * aw + 2 * RET_HEADS * RET_QK + RET_HEADS * RET_V
    head_cols = [w_in[:, o + hh * wd:o + (hh + 1) * wd]
                 for hh in range(RET_HEADS) for o, wd in ((rq0, RET_QK), (rk0, RET_QK), (rv0, RET_V))]
    w_rest = jnp.concatenate(head_cols + [w_in[:, tail0:]], axis=1).astype(BF16)

    qkv = _inproj_attn(x2, mod3, w_qkv, seq)
    rest = _inproj_rest(x2, mod3, w_rest, seq)

    outs, lses = zip(*[_attention(qkv, g, batch, seq) for g in range(ATTN_GROUPS)])
    decays = jnp.concatenate([decay_f, decay_b]).astype(F32)
    retg = _retention(rest, decays, gn_gain.reshape(1, -1), batch, seq)

    w_route = jnp.concatenate([w_coarse] + [w_fine[g] for g in range(N_GROUPS)], axis=1)
    n_route = w_route.shape[1]
    w_route = jnp.pad(w_route, ((0, 0), (0, LANES - n_route)))
    wr_hi, wr_lo = _split_bf16(w_route)
    rbias = jnp.pad(jnp.concatenate([b_coarse, b_fine.reshape(-1)]), (0, LANES - n_route)).reshape(1, LANES)
    ln1 = jnp.stack([ln1_gain, ln1_bias])
    h1, u2, logits = _merge(outs, lses, retg, rest, x2, mod3, w_attn_out.astype(BF16), w_ret_out.astype(BF16),
                            w_out.astype(BF16), ln1, wr_hi, wr_lo, rbias, seq)

    cols, ints, counts = _route(logits)
    n_blocks = 2 * t // MOE_BLK + N_EXPERTS
    n_blocks_pad = -(-n_blocks // LANES) * LANES
    dest, meta = _plan(ints, counts, n_blocks_pad)
    dest0, dest1 = dest[0], dest[1]
    rows = _dispatch(dest0, dest1, u2, n_blocks * MOE_BLK)
    y_rows = _experts(meta[0, :n_blocks], meta[1, :1], rows, w1, w3, w2)
    out = _combine(dest0, dest1, y_rows, cols, h1, mod3, jnp.stack([ln2_gain, ln2_bias]), seq)
    return out.reshape(batch, seq, d)


def kernel(x, c, w_ada, b_ada, w_in, w_attn_out, ret_decay_fwd, ret_decay_bwd, ret_gn_gain, w_ret_out, w_out,
           ln1_gain, ln1_bias, w_coarse, b_coarse, w_fine, b_fine, w1, w3, w2, ln2_gain, ln2_bias):
    batch = x.shape[0]
    assert batch <= 8 and x.shape[1] % (2 * INPROJ_TM) == 0 and x.shape[2] == D_MODEL
    c8 = jnp.pad(c, ((0, 8 - batch), (0, 0)))
    h = x
    for l in range(w_ada.shape[0]):
        h = _layer(h, c8, w_ada[l], b_ada[l], w_in[l], w_attn_out[l], ret_decay_fwd[l], ret_decay_bwd[l],
                   ret_gn_gain[l], w_ret_out[l], w_out[l], ln1_gain[l], ln1_bias[l], w_coarse[l], b_coarse[l],
                   w_fine[l], b_fine[l], w1[l], w3[l], w2[l], ln2_gain[l], ln2_bias[l])
    return h
```

```python
import functools
import math

import jax
import jax.numpy as jnp
from jax import lax
from jax.experimental import pallas as pl
from jax.experimental.pallas import tpu as pltpu

F32 = jnp.float32
BF16 = jnp.bfloat16
I32 = jnp.int32

D_MODEL = 1024
ATTN_GROUPS = 3
HEADS_PER_GROUP = 4
HEAD_DIM = 128
ATTN_HEADS = ATTN_GROUPS * HEADS_PER_GROUP
GROUP_WIDTH = HEADS_PER_GROUP * HEAD_DIM
ATTN_PATTERNS = ((128, 1), (512, 4), (2048, 16))
ALIBI_MAX_EXP = 8.0
NEG = -1e30
RET_HEADS = 4
RET_QK = 256
RET_V = 512
N_GROUPS = 4
EXPERTS_PER_GROUP = 8
N_EXPERTS = N_GROUPS * EXPERTS_PER_GROUP
EXPERT_FF = 512
DEPTH = 1
DEEPNORM_ALPHA = (2.0 * DEPTH) ** 0.25
LN_EPS = 1e-5

LANES = 128
PERM_TILE = 512
ATTN_QB = 128
ATTN_HALF = 64
ATTN_KB = ATTN_QB + 2 * ATTN_HALF
RET_CHUNK = 256
RET_STEP = 1024
MOE_BLK = 512
VMEM_LIMIT = 56 * 1024 * 1024


def _cparams(sem):
    return pltpu.CompilerParams(dimension_semantics=sem, vmem_limit_bytes=VMEM_LIMIT)


def _split_bf16(a):
    hi = a.astype(BF16)
    lo = (a - hi.astype(F32)).astype(BF16)
    return hi, lo


def _dot(a, b):
    return jnp.dot(a, b, preferred_element_type=F32)


def _dot3(a, b):
    ah, al = _split_bf16(a)
    bh, bl = _split_bf16(b)
    return _dot(ah, bh) + _dot(ah, bl) + _dot(al, bh)


def _ln(x):
    mu = jnp.mean(x, axis=-1, keepdims=True)
    xc = x - mu
    var = jnp.mean(xc * xc, axis=-1, keepdims=True)
    return xc * lax.rsqrt(var + LN_EPS)


def _sigmoid(x):
    return 1.0 / (1.0 + jnp.exp(-x))


ROW_TILE = D_MODEL // LANES


def _store_token_tiles(ref, val, row0=0):
    n = val.shape[0]
    for cc in range(ROW_TILE):
        ref[pl.ds(row0 * ROW_TILE + cc, n, stride=ROW_TILE), :] = val[:, cc * LANES:(cc + 1) * LANES]


def _load_token_tiles(ref, n):
    return jnp.concatenate([ref[pl.ds(cc, n, stride=ROW_TILE), :] for cc in range(ROW_TILE)], axis=1)


def _ada_kernel(c_ref, w_ref, b_ref, o_ref):
    o_ref[...] = _dot3(c_ref[...], w_ref[...]) + b_ref[...]


def _ada(c8, w_ada, b_ada):
    n = w_ada.shape[1]
    return pl.pallas_call(
        _ada_kernel,
        grid=(n // D_MODEL,),
        in_specs=[pl.BlockSpec((8, D_MODEL), lambda j: (0, 0)),
                  pl.BlockSpec((D_MODEL, D_MODEL), lambda j: (0, j)),
                  pl.BlockSpec((1, D_MODEL), lambda j: (0, j))],
        out_specs=pl.BlockSpec((8, D_MODEL), lambda j: (0, j)),
        out_shape=jax.ShapeDtypeStruct((8, n), F32),
        compiler_params=_cparams(("arbitrary",)),
        name="ada",
    )(c8, w_ada, b_ada)


INPROJ_TM = 1024
LN_ROWS = 256


def _modulated_ln(x_ref, mod_ref, r0, rows):
    x = x_ref[r0:r0 + rows, :]
    return _ln(x) * (1.0 + mod_ref[1:2, :]) + mod_ref[0:1, :]


def _inproj_attn_kernel(x_ref, mod_ref, w_ref, o_ref, uf_scr, u_scr):
    j = pl.program_id(1)

    @pl.when(j == 0)
    def _():
        for r0 in range(0, INPROJ_TM, LN_ROWS):
            u = _modulated_ln(x_ref, mod_ref, r0, LN_ROWS)
            for cc in range(D_MODEL // LANES):
                uf_scr[cc, r0:r0 + LN_ROWS, :] = u[:, cc * LANES:(cc + 1) * LANES]
            u_scr[0, r0:r0 + LN_ROWS, :] = u.astype(BF16)
        for g in (1, 2):
            dil = ATTN_PATTERNS[g][1]
            n = PERM_TILE // dil
            for t0 in range(0, INPROJ_TM, PERM_TILE):
                for res in range(dil):
                    for cc in range(D_MODEL // LANES):
                        rows = uf_scr[cc, pl.ds(t0 + res, n, stride=dil), :]
                        u_scr[g, t0 + res * n:t0 + (res + 1) * n, cc * LANES:(cc + 1) * LANES] = rows.astype(BF16)

    o_ref[...] = _dot(u_scr[j], w_ref[...]).astype(BF16)


def _inproj_attn(x2, mod3, w_qkv, seq):
    t = x2.shape[0]
    tn = 3 * GROUP_WIDTH
    per_seq = seq // INPROJ_TM
    return pl.pallas_call(
        _inproj_attn_kernel,
        grid=(t // INPROJ_TM, ATTN_GROUPS),
        in_specs=[pl.BlockSpec((INPROJ_TM, D_MODEL), lambda i, j: (i, 0)),
                  pl.BlockSpec((None, 6, D_MODEL), lambda i, j: (i // per_seq, 0, 0)),
                  pl.BlockSpec((D_MODEL, tn), lambda i, j: (0, j))],
        out_specs=pl.BlockSpec((INPROJ_TM, tn), lambda i, j: (i, j)),
        out_shape=jax.ShapeDtypeStruct((t, ATTN_GROUPS * tn), BF16),
        scratch_shapes=[pltpu.VMEM((D_MODEL // LANES, INPROJ_TM, LANES), F32),
                        pltpu.VMEM((ATTN_GROUPS, INPROJ_TM, D_MODEL), BF16)],
        compiler_params=_cparams(("arbitrary", "arbitrary")),
        name="inproj_attn",
    )(x2, mod3, w_qkv)


def _inproj_rest_kernel(x_ref, mod_ref, w_ref, o_ref, u_scr):
    @pl.when(pl.program_id(1) == 0)
    def _():
        for r0 in range(0, INPROJ_TM, LN_ROWS):
            u_scr[r0:r0 + LN_ROWS, :] = _modulated_ln(x_ref, mod_ref, r0, LN_ROWS).astype(BF16)

    o_ref[...] = _dot(u_scr[...], w_ref[...]).astype(BF16)


def _inproj_rest(x2, mod3, w_rest, seq):
    t = x2.shape[0]
    n = w_rest.shape[1]
    tn = 1024
    per_seq = seq // INPROJ_TM
    return pl.pallas_call(
        _inproj_rest_kernel,
        grid=(t // INPROJ_TM, n // tn),
        in_specs=[pl.BlockSpec((INPROJ_TM, D_MODEL), lambda i, j: (i, 0)),
                  pl.BlockSpec((None, 6, D_MODEL), lambda i, j: (i // per_seq, 0, 0)),
                  pl.BlockSpec((D_MODEL, tn), lambda i, j: (0, j))],
        out_specs=pl.BlockSpec((INPROJ_TM, tn), lambda i, j: (i, j)),
        out_shape=jax.ShapeDtypeStruct((t, n), BF16),
        scratch_shapes=[pltpu.VMEM((INPROJ_TM, D_MODEL), BF16)],
        compiler_params=_cparams(("arbitrary", "arbitrary")),
        name="inproj_rest",
    )(x2, mod3, w_rest)


ATTN_OFFSETS = ATTN_KB // ATTN_HALF - 1


def _attn_kernel(q_ref, k_ref, v_ref, o_ref, lse_ref, bias_scr, *scratch, group, dil, nt, n):
    n_sub = nt * n
    if nt == 1:
        qs, ks, vs, os_, ls = q_ref.at[0], k_ref.at[0], v_ref.at[0], o_ref.at[0], lse_ref.at[0]
    else:
        qs, ks, vs, os_, ls = scratch
        for t in range(nt):
            qs[t * n:(t + 1) * n, :] = q_ref[t]
            ks[t * n:(t + 1) * n, :] = k_ref[t]
            vs[t * n:(t + 1) * n, :] = v_ref[t]

    @pl.when((pl.program_id(0) == 0) & (pl.program_id(1) == 0))
    def _():
        base = (lax.broadcasted_iota(I32, (ATTN_QB, ATTN_KB), 1)
                - lax.broadcasted_iota(I32, (ATTN_QB, ATTN_KB), 0))
        for j in range(ATTN_OFFSETS):
            dist = jnp.abs(base - j * ATTN_HALF)
            for hh in range(HEADS_PER_GROUP):
                slope = dil * 2.0 ** (-ALIBI_MAX_EXP * (group * HEADS_PER_GROUP + hh + 1) / ATTN_HEADS)
                bias_scr[hh * ATTN_OFFSETS + j] = jnp.where(dist <= ATTN_HALF, -slope * dist.astype(F32), NEG)

    scale = HEAD_DIM ** -0.5
    lane = lax.broadcasted_iota(I32, (ATTN_QB, LANES), 1)

    def block(blk, carry):
        q0 = pl.multiple_of(blk * ATTN_QB, ATTN_QB)
        start = pl.multiple_of(jnp.clip(q0 - ATTN_HALF, 0, n_sub - ATTN_KB), ATTN_HALF)
        j = (q0 - start) // ATTN_HALF
        lse_tile = jnp.zeros((ATTN_QB, LANES), F32)
        for hh in range(HEADS_PER_GROUP):
            cs = slice(hh * HEAD_DIM, (hh + 1) * HEAD_DIM)
            qb = qs[pl.ds(q0, ATTN_QB), cs]
            kb = ks[pl.ds(start, ATTN_KB), cs]
            vb = vs[pl.ds(start, ATTN_KB), cs]
            s = lax.dot_general(qb, kb, (((1,), (1,)), ((), ())), preferred_element_type=F32) * scale
            s = s + bias_scr[hh * ATTN_OFFSETS + j]
            m = jnp.max(s, axis=-1, keepdims=True)
            p = jnp.exp(s - m)
            l = jnp.sum(p, axis=-1, keepdims=True)
            o = _dot(p.astype(BF16), vb) * (1.0 / l)
            os_[pl.ds(q0, ATTN_QB), cs] = o.astype(BF16)
            lse_tile = jnp.where(lane == hh, m + jnp.log(l), lse_tile)
        ls[pl.ds(q0, ATTN_QB), :] = lse_tile
        return carry

    lax.fori_loop(0, n_sub // ATTN_QB, block, 0, unroll=min(4, n_sub // ATTN_QB))

    if nt > 1:
        for t in range(nt):
            o_ref[t] = os_[t * n:(t + 1) * n, :]
            lse_ref[t] = ls[t * n:(t + 1) * n, :]


def _attention(qkv, group, batch, seq):
    dil = ATTN_PATTERNS[group][1]
    if dil == 1:
        nt, n = 1, seq
    else:
        nt, n = seq // PERM_TILE, PERM_TILE // dil
    n_sub = nt * n
    t = batch * seq
    qkv5 = qkv.reshape(batch, nt, dil, n, qkv.shape[1])
    cb = group * 3
    blk = (None, nt, None, n, GROUP_WIDTH)
    scratch = [pltpu.VMEM((HEADS_PER_GROUP * ATTN_OFFSETS, ATTN_QB, ATTN_KB), F32)]
    if nt > 1:
        scratch += [pltpu.VMEM((n_sub, GROUP_WIDTH), BF16)] * 4 + [pltpu.VMEM((n_sub, LANES), F32)]
    out, lse = pl.pallas_call(
        functools.partial(_attn_kernel, group=group, dil=dil, nt=nt, n=n),
        grid=(batch, dil),
        in_specs=[pl.BlockSpec(blk, lambda b, r: (b, 0, r, 0, cb)),
                  pl.BlockSpec(blk, lambda b, r: (b, 0, r, 0, cb + 1)),
                  pl.BlockSpec(blk, lambda b, r: (b, 0, r, 0, cb + 2))],
        out_specs=[pl.BlockSpec(blk, lambda b, r: (b, 0, r, 0, 0)),
                   pl.BlockSpec((None, nt, None, n, LANES), lambda b, r: (b, 0, r, 0, 0))],
        out_shape=[jax.ShapeDtypeStruct((batch, nt, dil, n, GROUP_WIDTH), BF16),
                   jax.ShapeDtypeStruct((batch, nt, dil, n, LANES), F32)],
        scratch_shapes=scratch,
        compiler_params=_cparams(("arbitrary", "arbitrary")),
        name=f"attn_g{group}",
    )(qkv5, qkv5, qkv5)
    return out.reshape(t, GROUP_WIDTH), lse.reshape(t, LANES)


def _log_sigmoid(x):
    return jnp.minimum(x, 0.0) - jnp.log(1.0 + jnp.exp(-jnp.abs(x)))


RET_HEAD_COLS = 2 * RET_QK + RET_V


def _retention_kernel(decay_ref, qkv_ref, g_ref, gain_ref, o_ref, state, ybwd, dmat, kdec, qdec, *, nc):
    c = RET_CHUNK
    h = pl.program_id(1)
    i = pl.program_id(2)
    kscale = RET_QK ** -0.5

    def set_decays(lg, forward):
        row = lax.broadcasted_iota(I32, (c, c), 0)
        col = lax.broadcasted_iota(I32, (c, c), 1)
        pos = lax.broadcasted_iota(I32, (c, LANES), 0).astype(F32)
        if forward:
            gap, key_pow, query_pow = row - col, (c - 1.0) - pos, pos + 1.0
            keep = gap >= 0
        else:
            gap, key_pow, query_pow = col - row, pos, c - pos
            keep = gap > 0
        dmat[...] = jnp.where(keep, jnp.exp(lg * jnp.maximum(gap, 0).astype(F32)) * kscale, 0.0)
        kdec[...] = jnp.exp(lg * key_pow) * kscale
        qdec[...] = jnp.exp(lg * query_pow)

    def chunk_step(lg, r0):
        q = qkv_ref[r0:r0 + c, 0:RET_QK]
        k = qkv_ref[r0:r0 + c, RET_QK:2 * RET_QK]
        v = qkv_ref[r0:r0 + c, 2 * RET_QK:]
        inner = lax.dot_general(q, k, (((1,), (1,)), ((), ())), preferred_element_type=F32) * dmat[...]
        y = _dot(inner.astype(BF16), v)
        qd = jnp.concatenate([qdec[...]] * (RET_V // LANES), axis=1)
        y = y + _dot(q, state[...].astype(BF16)) * qd
        kd = (k.astype(F32) * jnp.concatenate([kdec[...]] * (RET_QK // LANES), axis=1)).astype(BF16)
        kv = lax.dot_general(kd, v, (((0,), (0,)), ((), ())), preferred_element_type=F32)
        state[...] = state[...] * jnp.exp(lg * float(c)) + kv
        return y

    @pl.when((i == 0) | (i == nc))
    def _():
        state[...] = jnp.zeros_like(state)

    @pl.when(i < nc)
    def _():
        lg = _log_sigmoid(jnp.zeros((1, 1), F32) + decay_ref[RET_HEADS + h])

        @pl.when(i == 0)
        def _():
            set_decays(lg, False)

        base = pl.multiple_of((nc - 1 - i) * RET_STEP, RET_STEP)
        for r0 in reversed(range(0, RET_STEP, c)):
            ybwd[pl.ds(base + r0, c), :] = chunk_step(lg, r0)

    @pl.when(i >= nc)
    def _():
        lg = _log_sigmoid(jnp.zeros((1, 1), F32) + decay_ref[h])

        @pl.when(i == nc)
        def _():
            set_decays(lg, True)

        base = pl.multiple_of((i - nc) * RET_STEP, RET_STEP)
        for r0 in range(0, RET_STEP, c):
            y = chunk_step(lg, r0) + ybwd[pl.ds(base + r0, c), :]
            g = g_ref[r0:r0 + c, :].astype(F32)
            o_ref[r0:r0 + c, :] = (g * _sigmoid(g) * (_ln(y) * gain_ref[...])).astype(BF16)


def _retention(rest, decays, gn_gain, batch, seq):
    c = RET_STEP
    nc = seq // c
    rest3 = rest.reshape(batch, seq, rest.shape[1])
    gate_blk = RET_HEADS * RET_HEAD_COLS // RET_V

    def chunk(i):
        return jnp.where(i < nc, nc - 1 - i, i - nc)

    grid_spec = pltpu.PrefetchScalarGridSpec(
        num_scalar_prefetch=1,
        grid=(batch, RET_HEADS, 2 * nc),
        in_specs=[pl.BlockSpec((None, c, RET_HEAD_COLS), lambda b, h, i, d: (b, chunk(i), h)),
                  pl.BlockSpec((None, c, RET_V), lambda b, h, i, d: (b, jnp.maximum(i - nc, 0), gate_blk + h)),
                  pl.BlockSpec((1, RET_V), lambda b, h, i, d: (0, h))],
        out_specs=pl.BlockSpec((None, c, RET_V), lambda b, h, i, d: (b, jnp.maximum(i - nc, 0), h)),
        scratch_shapes=[pltpu.VMEM((RET_QK, RET_V), F32), pltpu.VMEM((seq, RET_V), F32),
                        pltpu.VMEM((RET_CHUNK, RET_CHUNK), F32), pltpu.VMEM((RET_CHUNK, LANES), F32),
                        pltpu.VMEM((RET_CHUNK, LANES), F32)],
    )
    out = pl.pallas_call(
        functools.partial(_retention_kernel, nc=nc),
        grid_spec=grid_spec,
        out_shape=jax.ShapeDtypeStruct((batch, seq, RET_HEADS * RET_V), BF16),
        compiler_params=_cparams(("arbitrary", "arbitrary", "arbitrary")),
        name="retention",
    )(decays, rest3, rest3, gn_gain)
    return out.reshape(batch * seq, RET_HEADS * RET_V)


MERGE_TM = PERM_TILE
MERGE_SUB = MERGE_TM


def _merge_kernel(o0_ref, o1_ref, o2_ref, l0_ref, l1_ref, l2_ref, retg_ref, ga_ref, gr_ref, x_ref, mod_ref,
                  wa_ref, wr_ref, wo_ref, ln_ref, wrh_ref, wrl_ref, rb_ref,
                  h1_ref, u2_ref, lg_ref, on_scr, ln_scr):
    for g in (1, 2):
        dil = ATTN_PATTERNS[g][1]
        n = MERGE_TM // dil
        o_ref, l_ref = ((o1_ref, l1_ref), (o2_ref, l2_ref))[g - 1]
        for res in range(dil):
            rows = o_ref[res * n:(res + 1) * n, :].astype(F32)
            for hh in range(HEADS_PER_GROUP):
                on_scr[g - 1, hh, pl.ds(res, n, stride=dil), :] = rows[:, hh * HEAD_DIM:(hh + 1) * HEAD_DIM]
            ln_scr[g - 1, pl.ds(res, n, stride=dil), :] = l_ref[res * n:(res + 1) * n, :]

    for r0 in range(0, MERGE_TM, MERGE_SUB):
        rs = slice(r0, r0 + MERGE_SUB)
        l0, l1, l2 = l0_ref[rs, :], ln_scr[0, rs, :], ln_scr[1, rs, :]
        lm = jnp.maximum(jnp.maximum(l0, l1), l2)
        e0, e1, e2 = jnp.exp(l0 - lm), jnp.exp(l1 - lm), jnp.exp(l2 - lm)
        inv = 1.0 / (e0 + e1 + e2)
        parts = []
        for hh in range(HEADS_PER_GROUP):
            sl = slice(hh * HEAD_DIM, (hh + 1) * HEAD_DIM)
            acc = (e0[:, hh:hh + 1] * o0_ref[rs, sl].astype(F32)
                   + e1[:, hh:hh + 1] * on_scr[0, hh, rs, :]
                   + e2[:, hh:hh + 1] * on_scr[1, hh, rs, :])
            parts.append((acc * inv[:, hh:hh + 1]).astype(BF16))
        attn = jnp.concatenate(parts, axis=1)

        branch_a = _dot(attn, wa_ref[...])
        branch_r = _dot(retg_ref[rs, :], wr_ref[...])
        merged = (_sigmoid(ga_ref[rs, :].astype(F32)) * branch_a
                  + _sigmoid(gr_ref[rs, :].astype(F32)) * branch_r)
        y = _dot(merged.astype(BF16), wo_ref[...])

        h1 = _ln(DEEPNORM_ALPHA * x_ref[rs, :] + mod_ref[2:3, :] * y) * ln_ref[0:1, :] + ln_ref[1:2, :]
        h1_ref[rs, :] = h1
        u2 = _ln(h1) * (1.0 + mod_ref[4:5, :]) + mod_ref[3:4, :]
        _store_token_tiles(u2_ref, u2, r0)
        uh, ul = _split_bf16(u2)
        lg_ref[rs, :] = (_dot(uh, wrh_ref[...]) + _dot(uh, wrl_ref[...]) + _dot(ul, wrh_ref[...])
                         + rb_ref[...])


def _merge(outs, lses, retg, rest, x2, mod3, wa, wr, wo, ln1, wr_hi, wr_lo, rbias, seq):
    t = x2.shape[0]
    tm = MERGE_TM
    per_seq = seq // tm
    row = lambda w: pl.BlockSpec((tm, w), lambda i: (i, 0))
    full = lambda a: pl.BlockSpec(a.shape, lambda i: (0,) * a.ndim)
    return pl.pallas_call(
        _merge_kernel,
        grid=(t // tm,),
        in_specs=[row(GROUP_WIDTH)] * 3 + [row(LANES)] * 3 + [
            row(RET_HEADS * RET_V),
            pl.BlockSpec((tm, D_MODEL), lambda i: (i, 6)),
            pl.BlockSpec((tm, D_MODEL), lambda i: (i, 7)),
            row(D_MODEL),
            pl.BlockSpec((None, 6, D_MODEL), lambda i: (i // per_seq, 0, 0)),
            full(wa), full(wr), full(wo), full(ln1), full(wr_hi), full(wr_lo), full(rbias)],
        out_specs=[row(D_MODEL), pl.BlockSpec((tm * ROW_TILE, LANES), lambda i: (i, 0)), row(LANES)],
        out_shape=[jax.ShapeDtypeStruct((t, D_MODEL), F32),
                   jax.ShapeDtypeStruct((t * ROW_TILE, LANES), F32),
                   jax.ShapeDtypeStruct((t, LANES), F32)],
        scratch_shapes=[pltpu.VMEM((2, HEADS_PER_GROUP, tm, HEAD_DIM), F32), pltpu.VMEM((2, tm, LANES), F32)],
        compiler_params=_cparams(("arbitrary",)),
        name="merge",
    )(*outs, *lses, retg, rest, rest, x2, mod3, wa, wr, wo, ln1, wr_hi, wr_lo, rbias)


ROUTE_TM = 512
BIG = 1 << 20


def _route_kernel(lg_ref, cols_ref, ints_ref, cnt_ref, carry):
    i = pl.program_id(0)

    @pl.when(i == 0)
    def _():
        carry[...] = jnp.zeros_like(carry)

    tm = ROUTE_TM
    lg = lg_ref[...]
    lane = lax.broadcasted_iota(I32, (tm, LANES), 1)
    lane_f = lane.astype(F32)
    first = lambda mask: jnp.min(jnp.where(mask, lane_f, float(BIG)), axis=-1, keepdims=True).astype(I32)

    coarse = jnp.where(lane < N_GROUPS, lg, NEG)
    cmax = jnp.max(coarse, axis=-1, keepdims=True)
    gsel = first(coarse == cmax)
    p_group = 1.0 / jnp.sum(jnp.exp(coarse - cmax), axis=-1, keepdims=True)

    lo = N_GROUPS + EXPERTS_PER_GROUP * gsel
    fine = jnp.where((lane >= lo) & (lane < lo + EXPERTS_PER_GROUP), lg, NEG)
    v1 = jnp.max(fine, axis=-1, keepdims=True)
    i1 = first(fine == v1)
    fine2 = jnp.where(lane == i1, NEG, fine)
    v2 = jnp.max(fine2, axis=-1, keepdims=True)
    i2 = first(fine2 == v2)
    ex = jnp.exp(v2 - v1)
    den = 1.0 / (1.0 + ex)
    gate1 = p_group * den
    gate2 = p_group * (ex * den)
    e1 = i1 - N_GROUPS
    e2 = i2 - N_GROUPS

    oh1 = lane == e1
    oh2 = lane == e2
    cnt = jnp.where(oh1 | oh2, 1.0, 0.0)
    r_i = lax.broadcasted_iota(I32, (tm, tm), 0)
    c_i = lax.broadcasted_iota(I32, (tm, tm), 1)
    tri = jnp.where(r_i > c_i, 1.0, 0.0).astype(BF16)
    rank = _dot(tri, cnt.astype(BF16)) + carry[...]
    r1 = jnp.sum(jnp.where(oh1, rank, 0.0), axis=-1, keepdims=True)
    r2 = jnp.sum(jnp.where(oh2, rank, 0.0), axis=-1, keepdims=True)
    carry[...] = carry[...] + jnp.sum(cnt, axis=0, keepdims=True)
    cnt_ref[...] = jnp.broadcast_to(carry[...], cnt_ref.shape)

    cols_ref[...] = jnp.where(lane == 0, gate1, jnp.where(lane == 1, gate2, 0.0))
    packed = jnp.where(lane == 0, e1.astype(F32),
                       jnp.where(lane == 1, e2.astype(F32),
                                 jnp.where(lane == 2, r1, jnp.where(lane == 3, r2, 0.0))))
    ints_ref[...] = packed.T[0:8, :].astype(I32)


def _route(logits):
    t = logits.shape[0]
    tm = ROUTE_TM
    return pl.pallas_call(
        _route_kernel,
        grid=(t // tm,),
        in_specs=[pl.BlockSpec((tm, LANES), lambda i: (i, 0))],
        out_specs=[pl.BlockSpec((tm, LANES), lambda i: (i, 0)),
                   pl.BlockSpec((8, tm), lambda i: (0, i)),
                   pl.BlockSpec((8, LANES), lambda i: (0, 0))],
        out_shape=[jax.ShapeDtypeStruct((t, LANES), F32),
                   jax.ShapeDtypeStruct((8, t), I32),
                   jax.ShapeDtypeStruct((8, LANES), F32)],
        scratch_shapes=[pltpu.VMEM((1, LANES), F32)],
        compiler_params=_cparams(("arbitrary",)),
        name="route",
    )(logits)


def _plan_kernel(ints_ref, cnt_ref, dest_ref, meta_ref, *, n_blocks_pad):
    sub = lax.broadcasted_iota(I32, (LANES, LANES), 0)
    lane = lax.broadcasted_iota(I32, (LANES, LANES), 1)
    cnt = cnt_ref[0:1, :]
    nblk_row = jnp.floor((cnt + (MOE_BLK - 1.0)) * (1.0 / MOE_BLK))
    nblk_mat = jnp.broadcast_to(nblk_row, (LANES, LANES))
    start_col = jnp.sum(jnp.where(lane < sub, nblk_mat, 0.0), axis=-1, keepdims=True)
    nblk_col = jnp.sum(jnp.where(lane == sub, nblk_mat, 0.0), axis=-1, keepdims=True)
    end_col = start_col + nblk_col

    ints = ints_ref[...]
    base = jnp.zeros(ints.shape, F32)
    for e in range(N_EXPERTS):
        base = jnp.where(ints == e, start_col[e:e + 1, :] * float(MOE_BLK), base)
    dest = base[0:2, :].astype(I32) + ints[2:4, :]
    dest_ref[...] = jnp.concatenate([dest, jnp.zeros((6, ints.shape[1]), I32)], axis=0)

    blk = lax.broadcasted_iota(I32, (LANES, n_blocks_pad), 1).astype(F32)
    e_sub = lax.broadcasted_iota(I32, (LANES, n_blocks_pad), 0)
    done = jnp.where((e_sub < N_EXPERTS) & (end_col <= blk), 1.0, 0.0)
    bexp = jnp.minimum(jnp.sum(done, axis=0, keepdims=True), N_EXPERTS - 1.0)
    used = jnp.sum(nblk_row, axis=-1, keepdims=True)
    row = lax.broadcasted_iota(I32, (8, n_blocks_pad), 0)
    meta = jnp.where(row == 0, bexp, jnp.where(row == 1, used, 0.0))
    meta_ref[...] = meta.astype(I32)


def _plan(ints, counts, n_blocks_pad):
    t = ints.shape[1]
    return pl.pallas_call(
        functools.partial(_plan_kernel, n_blocks_pad=n_blocks_pad),
        out_shape=[jax.ShapeDtypeStruct((8, t), I32), jax.ShapeDtypeStruct((8, n_blocks_pad), I32)],
        compiler_params=pltpu.CompilerParams(vmem_limit_bytes=VMEM_LIMIT),
        name="plan",
    )(ints, counts)


DISPATCH_TM = 512
DMA_UNROLL = 8


def _row_tile(ref, r):
    return ref.at[pl.ds(pl.multiple_of(r * ROW_TILE, ROW_TILE), ROW_TILE)]


def _dispatch_kernel(d0_ref, d1_ref, u2_ref, rows_in_ref, rows_ref, sem):
    del rows_in_ref

    def issue(t, carry):
        src = _row_tile(u2_ref, t)
        pltpu.make_async_copy(src, _row_tile(rows_ref, d0_ref[t]), sem).start(priority=0)
        pltpu.make_async_copy(src, _row_tile(rows_ref, d1_ref[t]), sem).start(priority=1)
        return carry

    lax.fori_loop(0, DISPATCH_TM, issue, 0, unroll=DMA_UNROLL)
    for _ in range(2):
        pltpu.make_async_copy(u2_ref, rows_ref.at[pl.ds(0, DISPATCH_TM * ROW_TILE)], sem).wait()


def _dispatch(dest0, dest1, u2, n_rows):
    t = u2.shape[0] // ROW_TILE
    rows0 = jnp.zeros((n_rows * ROW_TILE, LANES), F32)
    idx = pl.BlockSpec((DISPATCH_TM,), lambda i: (i,), memory_space=pltpu.SMEM)
    return pl.pallas_call(
        _dispatch_kernel,
        grid=(t // DISPATCH_TM,),
        in_specs=[idx, idx,
                  pl.BlockSpec((DISPATCH_TM * ROW_TILE, LANES), lambda i: (i, 0)),
                  pl.BlockSpec(memory_space=pl.ANY)],
        out_specs=pl.BlockSpec(memory_space=pl.ANY),
        out_shape=jax.ShapeDtypeStruct((n_rows * ROW_TILE, LANES), F32),
        scratch_shapes=[pltpu.SemaphoreType.DMA(())],
        input_output_aliases={3: 0},
        compiler_params=_cparams(("arbitrary",)),
        name="dispatch",
    )(dest0, dest1, u2, rows0)


def _experts_kernel(bexp_ref, used_ref, x_ref, w1_ref, w3_ref, w2_ref, y_ref, w1s, w3s, w2s):
    i = pl.program_id(0)
    prev = bexp_ref[jnp.maximum(i - 1, 0)]
    active = i < used_ref[0]

    @pl.when(active & ((i == 0) | (bexp_ref[i] != prev)))
    def _():
        w1s[...] = w1_ref[...].astype(BF16)
        w3s[...] = w3_ref[...].astype(BF16)
        w2s[...] = w2_ref[...].astype(BF16)

    @pl.when(active)
    def _():
        xb = _load_token_tiles(x_ref, MOE_BLK).astype(BF16)
        a = _dot(xb, w1s[...])
        b = _dot(xb, w3s[...])
        hdn = (a * _sigmoid(a) * b).astype(BF16)
        _store_token_tiles(y_ref, _dot(hdn, w2s[...]))

    @pl.when(jnp.logical_not(active))
    def _():
        y_ref[...] = jnp.zeros_like(y_ref)


def _experts(bexp, used, rows, w1, w3, w2):
    n_blocks = rows.shape[0] // (MOE_BLK * ROW_TILE)
    last = lambda i, nu: jnp.minimum(i, nu[0] - 1)
    row_blk = pl.BlockSpec((MOE_BLK * ROW_TILE, LANES), lambda i, be, nu: (i, 0))
    w_idx = lambda i, be, nu: (be[last(i, nu)], 0, 0)
    grid_spec = pltpu.PrefetchScalarGridSpec(
        num_scalar_prefetch=2,
        grid=(n_blocks,),
        in_specs=[pl.BlockSpec((MOE_BLK * ROW_TILE, LANES), lambda i, be, nu: (last(i, nu), 0)),
                  pl.BlockSpec((None, D_MODEL, EXPERT_FF), w_idx),
                  pl.BlockSpec((None, D_MODEL, EXPERT_FF), w_idx),
                  pl.BlockSpec((None, EXPERT_FF, D_MODEL), w_idx)],
        out_specs=row_blk,
        scratch_shapes=[pltpu.VMEM((D_MODEL, EXPERT_FF), BF16), pltpu.VMEM((D_MODEL, EXPERT_FF), BF16),
                        pltpu.VMEM((EXPERT_FF, D_MODEL), BF16)],
    )
    return pl.pallas_call(
        _experts_kernel,
        grid_spec=grid_spec,
        out_shape=jax.ShapeDtypeStruct(rows.shape, F32),
        compiler_params=_cparams(("arbitrary",)),
        name="experts",
    )(bexp, used, rows, w1, w3, w2)


COMBINE_TM = 256


def _combine_kernel(d0_ref, d1_ref, n0_ref, n1_ref, y_ref, cols_ref, h1_ref, mod_ref, ln_ref, o_ref, ya, yb, sem):
    i = pl.program_id(0)
    slot = i % 2

    def gather(i0_ref, i1_ref, s):
        def issue(t, carry):
            pltpu.make_async_copy(_row_tile(y_ref, i0_ref[t]), _row_tile(ya.at[s], t), sem.at[s]).start(priority=0)
            pltpu.make_async_copy(_row_tile(y_ref, i1_ref[t]), _row_tile(yb.at[s], t), sem.at[s]).start(priority=1)
            return carry

        lax.fori_loop(0, COMBINE_TM, issue, 0, unroll=DMA_UNROLL)

    @pl.when(i == 0)
    def _():
        gather(d0_ref, d1_ref, 0)

    @pl.when(i + 1 < pl.num_programs(0))
    def _():
        gather(n0_ref, n1_ref, 1 - slot)

    for buf in (ya, yb):
        pltpu.make_async_copy(y_ref.at[pl.ds(0, COMBINE_TM * ROW_TILE)], buf.at[slot], sem.at[slot]).wait()

    cols = cols_ref[...]
    moe = (cols[:, 0:1] * _load_token_tiles(ya.at[slot], COMBINE_TM)
           + cols[:, 1:2] * _load_token_tiles(yb.at[slot], COMBINE_TM))
    pre = DEEPNORM_ALPHA * h1_ref[...] + mod_ref[5:6, :] * moe
    o_ref[...] = _ln(pre) * ln_ref[0:1, :] + ln_ref[1:2, :]


def _combine(dest0, dest1, y_rows, cols, h1, mod3, ln2, seq):
    t = h1.shape[0]
    tm = COMBINE_TM
    per_seq = seq // tm
    idx = pl.BlockSpec((tm,), lambda i: (i,), memory_space=pltpu.SMEM)
    idx_next = pl.BlockSpec((tm,), lambda i: (jnp.minimum(i + 1, t // tm - 1),), memory_space=pltpu.SMEM)
    return pl.pallas_call(
        _combine_kernel,
        grid=(t // tm,),
        in_specs=[idx, idx, idx_next, idx_next,
                  pl.BlockSpec(memory_space=pl.ANY),
                  pl.BlockSpec((tm, LANES), lambda i: (i, 0)),
                  pl.BlockSpec((tm, D_MODEL), lambda i: (i, 0)),
                  pl.BlockSpec((None, 6, D_MODEL), lambda i: (i // per_seq, 0, 0)),
                  pl.BlockSpec((2, D_MODEL), lambda i: (0, 0))],
        out_specs=pl.BlockSpec((tm, D_MODEL), lambda i: (i, 0)),
        out_shape=jax.ShapeDtypeStruct((t, D_MODEL), F32),
        scratch_shapes=[pltpu.VMEM((2, tm * ROW_TILE, LANES), F32), pltpu.VMEM((2, tm * ROW_TILE, LANES), F32),
                        pltpu.SemaphoreType.DMA((2,))],
        compiler_params=_cparams(("arbitrary",)),
        name="combine",
    )(dest0, dest1, dest0, dest1, y_rows, cols, h1, mod3, ln2)


def _layer(h, c8, w_ada, b_ada, w_in, w_attn_out, decay_f, decay_b, gn_gain, w_ret_out, w_out,
           ln1_gain, ln1_bias, w_coarse, b_coarse, w_fine, b_fine, w1, w3, w2, ln2_gain, ln2_bias):
    batch, seq, d = h.shape
    t = batch * seq
    x2 = h.reshape(t, d)

    mod = _ada(c8, w_ada, b_ada.reshape(1, -1))
    mod3 = mod[:batch].reshape(batch, 6, d)

    aw = ATTN_HEADS * HEAD_DIM
    wq, wk, wv = (w_in[:, s * aw:(s + 1) * aw] for s in range(3))
    gcols = lambda w, g: w[:, g * GROUP_WIDTH:(g + 1) * GROUP_WIDTH]
    w_qkv = jnp.concatenate([gcols(w, g) for g in range(ATTN_GROUPS) for w in (wq, wk, wv)], axis=1).astype(BF16)
    rq0, rk0, rv0, tail0 = 3 * aw, 3 * aw + RET_HEADS * RET_QK, 3 * aw + 2 * RET_HEADS * RET_QK, \
        3 * aw + 2 * RET_HEADS * RET_QK + RET_HEADS * RET_V
    head_cols = [w_in[:, o + hh * wd:o + (hh + 1) * wd]
                 for hh in range(RET_HEADS) for o, wd in ((rq0, RET_QK), (rk0, RET_QK), (rv0, RET_V))]
    w_rest = jnp.concatenate(head_cols + [w_in[:, tail0:]], axis=1).astype(BF16)

    qkv = _inproj_attn(x2, mod3, w_qkv, seq)
    rest = _inproj_rest(x2, mod3, w_rest, seq)

    outs, lses = zip(*[_attention(qkv, g, batch, seq) for g in range(ATTN_GROUPS)])
    decays = jnp.concatenate([decay_f, decay_b]).astype(F32)
    retg = _retention(rest, decays, gn_gain.reshape(1, -1), batch, seq)

    w_route = jnp.concatenate([w_coarse] + [w_fine[g] for g in range(N_GROUPS)], axis=1)
    n_route = w_route.shape[1]
    w_route = jnp.pad(w_route, ((0, 0), (0, LANES - n_route)))
    wr_hi, wr_lo = _split_bf16(w_route)
    rbias = jnp.pad(jnp.concatenate([b_coarse, b_fine.reshape(-1)]), (0, LANES - n_route)).reshape(1, LANES)
    ln1 = jnp.stack([ln1_gain, ln1_bias])
    h1, u2, logits = _merge(outs, lses, retg, rest, x2, mod3, w_attn_out.astype(BF16), w_ret_out.astype(BF16),
                            w_out.astype(BF16), ln1, wr_hi, wr_lo, rbias, seq)

    cols, ints, counts = _route(logits)
    n_blocks = 2 * t // MOE_BLK + N_EXPERTS
    n_blocks_pad = -(-n_blocks // LANES) * LANES
    dest, meta = _plan(ints, counts, n_blocks_pad)
    dest0, dest1 = dest[0], dest[1]
    rows = _dispatch(dest0, dest1, u2, n_blocks * MOE_BLK)
    y_rows = _experts(meta[0, :n_blocks], meta[1, :1], rows, w1, w3, w2)
    out = _combine(dest0, dest1, y_rows, cols, h1, mod3, jnp.stack([ln2_gain, ln2_bias]), seq)
    return out.reshape(batch, seq, d)


def kernel(x, c, w_ada, b_ada, w_in, w_attn_out, ret_decay_fwd, ret_decay_bwd, ret_gn_gain, w_ret_out, w_out,
           ln1_gain, ln1_bias, w_coarse, b_coarse, w_fine, b_fine, w1, w3, w2, ln2_gain, ln2_bias):
    batch = x.shape[0]
    assert batch <= 8 and x.shape[1] % (2 * INPROJ_TM) == 0 and x.shape[2] == D_MODEL
    c8 = jnp.pad(c, ((0, 8 - batch), (0, 0)))
    h = x
    for l in range(w_ada.shape[0]):
        h = _layer(h, c8, w_ada[l], b_ada[l], w_in[l], w_attn_out[l], ret_decay_fwd[l], ret_decay_bwd[l],
                   ret_gn_gain[l], w_ret_out[l], w_out[l], ln1_gain[l], ln1_bias[l], w_coarse[l], b_coarse[l],
                   w_fine[l], b_fine[l], w1[l], w3[l], w2[l], ln2_gain[l], ln2_bias[l])
    return h
```

```python
import functools
import math

import jax
import jax.numpy as jnp
from jax import lax
from jax.experimental import pallas as pl
from jax.experimental.pallas import tpu as pltpu

F32 = jnp.float32
BF16 = jnp.bfloat16
I32 = jnp.int32

D_MODEL = 1024
ATTN_GROUPS = 3
HEADS_PER_GROUP = 4
HEAD_DIM = 128
ATTN_HEADS = ATTN_GROUPS * HEADS_PER_GROUP
GROUP_WIDTH = HEADS_PER_GROUP * HEAD_DIM
ATTN_PATTERNS = ((128, 1), (512, 4), (2048, 16))
ALIBI_MAX_EXP = 8.0
NEG = -1e30
RET_HEADS = 4
RET_QK = 256
RET_V = 512
N_GROUPS = 4
EXPERTS_PER_GROUP = 8
N_EXPERTS = N_GROUPS * EXPERTS_PER_GROUP
EXPERT_FF = 512
DEPTH = 1
DEEPNORM_ALPHA = (2.0 * DEPTH) ** 0.25
LN_EPS = 1e-5

LANES = 128
PERM_TILE = 512
ATTN_QB = 128
ATTN_HALF = 64
ATTN_KB = ATTN_QB + 2 * ATTN_HALF
RET_CHUNK = 256
RET_STEP = 1024
MOE_BLK = 512
VMEM_LIMIT = 56 * 1024 * 1024


def _cparams(sem):
    return pltpu.CompilerParams(dimension_semantics=sem, vmem_limit_bytes=VMEM_LIMIT)


def _split_bf16(a):
    hi = a.astype(BF16)
    lo = (a - hi.astype(F32)).astype(BF16)
    return hi, lo


def _dot(a, b):
    return jnp.dot(a, b, preferred_element_type=F32)


def _dot3(a, b):
    ah, al = _split_bf16(a)
    bh, bl = _split_bf16(b)
    return _dot(ah, bh) + _dot(ah, bl) + _dot(al, bh)


def _ln(x):
    mu = jnp.mean(x, axis=-1, keepdims=True)
    xc = x - mu
    var = jnp.mean(xc * xc, axis=-1, keepdims=True)
    return xc * lax.rsqrt(var + LN_EPS)


def _sigmoid(x):
    return 1.0 / (1.0 + jnp.exp(-x))


ROW_TILE = D_MODEL // LANES


def _store_token_tiles(ref, val, row0=0):
    n = val.shape[0]
    for cc in range(ROW_TILE):
        ref[pl.ds(row0 * ROW_TILE + cc, n, stride=ROW_TILE), :] = val[:, cc * LANES:(cc + 1) * LANES]


def _load_token_tiles(ref, n):
    return jnp.concatenate([ref[pl.ds(cc, n, stride=ROW_TILE), :] for cc in range(ROW_TILE)], axis=1)


def _ada_kernel(c_ref, w_ref, b_ref, o_ref):
    o_ref[...] = _dot3(c_ref[...], w_ref[...]) + b_ref[...]


def _ada(c8, w_ada, b_ada):
    n = w_ada.shape[1]
    return pl.pallas_call(
        _ada_kernel,
        grid=(n // D_MODEL,),
        in_specs=[pl.BlockSpec((8, D_MODEL), lambda j: (0, 0)),
                  pl.BlockSpec((D_MODEL, D_MODEL), lambda j: (0, j)),
                  pl.BlockSpec((1, D_MODEL), lambda j: (0, j))],
        out_specs=pl.BlockSpec((8, D_MODEL), lambda j: (0, j)),
        out_shape=jax.ShapeDtypeStruct((8, n), F32),
        compiler_params=_cparams(("arbitrary",)),
        name="ada",
    )(c8, w_ada, b_ada)


INPROJ_TM = 1024
LN_ROWS = 256


def _modulated_ln(x_ref, mod_ref, r0, rows):
    x = x_ref[r0:r0 + rows, :]
    return _ln(x) * (1.0 + mod_ref[1:2, :]) + mod_ref[0:1, :]


def _inproj_attn_kernel(x_ref, mod_ref, w_ref, o_ref, uf_scr, u_scr):
    j = pl.program_id(1)

    @pl.when(j == 0)
    def _():
        for r0 in range(0, INPROJ_TM, LN_ROWS):
            u = _modulated_ln(x_ref, mod_ref, r0, LN_ROWS)
            for cc in range(D_MODEL // LANES):
                uf_scr[cc, r0:r0 + LN_ROWS, :] = u[:, cc * LANES:(cc + 1) * LANES]
            u_scr[0, r0:r0 + LN_ROWS, :] = u.astype(BF16)
        for g in (1, 2):
            dil = ATTN_PATTERNS[g][1]
            n = PERM_TILE // dil
            for t0 in range(0, INPROJ_TM, PERM_TILE):
                for res in range(dil):
                    for cc in range(D_MODEL // LANES):
                        rows = uf_scr[cc, pl.ds(t0 + res, n, stride=dil), :]
                        u_scr[g, t0 + res * n:t0 + (res + 1) * n, cc * LANES:(cc + 1) * LANES] = rows.astype(BF16)

    o_ref[...] = _dot(u_scr[j], w_ref[...]).astype(BF16)


def _inproj_attn(x2, mod3, w_qkv, seq):
    t = x2.shape[0]
    tn = 3 * GROUP_WIDTH
    per_seq = seq // INPROJ_TM
    return pl.pallas_call(
        _inproj_attn_kernel,
        grid=(t // INPROJ_TM, ATTN_GROUPS),
        in_specs=[pl.BlockSpec((INPROJ_TM, D_MODEL), lambda i, j: (i, 0)),
                  pl.BlockSpec((None, 6, D_MODEL), lambda i, j: (i // per_seq, 0, 0)),
                  pl.BlockSpec((D_MODEL, tn), lambda i, j: (0, j))],
        out_specs=pl.BlockSpec((INPROJ_TM, tn), lambda i, j: (i, j)),
        out_shape=jax.ShapeDtypeStruct((t, ATTN_GROUPS * tn), BF16),
        scratch_shapes=[pltpu.VMEM((D_MODEL // LANES, INPROJ_TM, LANES), F32),
                        pltpu.VMEM((ATTN_GROUPS, INPROJ_TM, D_MODEL), BF16)],
        compiler_params=_cparams(("arbitrary", "arbitrary")),
        name="inproj_attn",
    )(x2, mod3, w_qkv)


def _inproj_rest_kernel(x_ref, mod_ref, w_ref, o_ref, u_scr):
    @pl.when(pl.program_id(1) == 0)
    def _():
        for r0 in range(0, INPROJ_TM, LN_ROWS):
            u_scr[r0:r0 + LN_ROWS, :] = _modulated_ln(x_ref, mod_ref, r0, LN_ROWS).astype(BF16)

    o_ref[...] = _dot(u_scr[...], w_ref[...]).astype(BF16)


def _inproj_rest(x2, mod3, w_rest, seq):
    t = x2.shape[0]
    n = w_rest.shape[1]
    tn = 1024
    per_seq = seq // INPROJ_TM
    return pl.pallas_call(
        _inproj_rest_kernel,
        grid=(t // INPROJ_TM, n // tn),
        in_specs=[pl.BlockSpec((INPROJ_TM, D_MODEL), lambda i, j: (i, 0)),
                  pl.BlockSpec((None, 6, D_MODEL), lambda i, j: (i // per_seq, 0, 0)),
                  pl.BlockSpec((D_MODEL, tn), lambda i, j: (0, j))],
        out_specs=pl.BlockSpec((INPROJ_TM, tn), lambda i, j: (i, j)),
        out_shape=jax.ShapeDtypeStruct((t, n), BF16),
        scratch_shapes=[pltpu.VMEM((INPROJ_TM, D_MODEL), BF16)],
        compiler_params=_cparams(("arbitrary", "arbitrary")),
        name="inproj_rest",
    )(x2, mod3, w_rest)


ATTN_OFFSETS = ATTN_KB // ATTN_HALF - 1


def _attn_kernel(q_ref, k_ref, v_ref, o_ref, lse_ref, bias_scr, *scratch, group, dil, nt, n):
    n_sub = nt * n
    if nt == 1:
        qs, ks, vs, os_, ls = q_ref.at[0], k_ref.at[0], v_ref.at[0], o_ref.at[0], lse_ref.at[0]
    else:
        qs, ks, vs, os_, ls = scratch
        for t in range(nt):
            qs[t * n:(t + 1) * n, :] = q_ref[t]
            ks[t * n:(t + 1) * n, :] = k_ref[t]
            vs[t * n:(t + 1) * n, :] = v_ref[t]

    @pl.when((pl.program_id(0) == 0) & (pl.program_id(1) == 0))
    def _():
        base = (lax.broadcasted_iota(I32, (ATTN_QB, ATTN_KB), 1)
                - lax.broadcasted_iota(I32, (ATTN_QB, ATTN_KB), 0))
        for j in range(ATTN_OFFSETS):
            dist = jnp.abs(base - j * ATTN_HALF)
            for hh in range(HEADS_PER_GROUP):
                slope = dil * 2.0 ** (-ALIBI_MAX_EXP * (group * HEADS_PER_GROUP + hh + 1) / ATTN_HEADS)
                bias_scr[hh * ATTN_OFFSETS + j] = jnp.where(dist <= ATTN_HALF, -slope * dist.astype(F32), NEG)

    scale = HEAD_DIM ** -0.5
    lane = lax.broadcasted_iota(I32, (ATTN_QB, LANES), 1)

    def block(blk, carry):
        q0 = pl.multiple_of(blk * ATTN_QB, ATTN_QB)
        start = pl.multiple_of(jnp.clip(q0 - ATTN_HALF, 0, n_sub - ATTN_KB), ATTN_HALF)
        j = (q0 - start) // ATTN_HALF
        lse_tile = jnp.zeros((ATTN_QB, LANES), F32)
        for hh in range(HEADS_PER_GROUP):
            cs = slice(hh * HEAD_DIM, (hh + 1) * HEAD_DIM)
            qb = qs[pl.ds(q0, ATTN_QB), cs]
            kb = ks[pl.ds(start, ATTN_KB), cs]
            vb = vs[pl.ds(start, ATTN_KB), cs]
            s = lax.dot_general(qb, kb, (((1,), (1,)), ((), ())), preferred_element_type=F32) * scale
            s = s + bias_scr[hh * ATTN_OFFSETS + j]
            m = jnp.max(s, axis=-1, keepdims=True)
            p = jnp.exp(s - m)
            l = jnp.sum(p, axis=-1, keepdims=True)
            o = _dot(p.astype(BF16), vb) * (1.0 / l)
            os_[pl.ds(q0, ATTN_QB), cs] = o.astype(BF16)
            lse_tile = jnp.where(lane == hh, m + jnp.log(l), lse_tile)
        ls[pl.ds(q0, ATTN_QB), :] = lse_tile
        return carry

    lax.fori_loop(0, n_sub // ATTN_QB, block, 0, unroll=min(4, n_sub // ATTN_QB))

    if nt > 1:
        for t in range(nt):
            o_ref[t] = os_[t * n:(t + 1) * n, :]
            lse_ref[t] = ls[t * n:(t + 1) * n, :]


def _attention(qkv, group, batch, seq):
    dil = ATTN_PATTERNS[group][1]
    if dil == 1:
        nt, n = 1, seq
    else:
        nt, n = seq // PERM_TILE, PERM_TILE // dil
    n_sub = nt * n
    t = batch * seq
    qkv5 = qkv.reshape(batch, nt, dil, n, qkv.shape[1])
    cb = group * 3
    blk = (None, nt, None, n, GROUP_WIDTH)
    scratch = [pltpu.VMEM((HEADS_PER_GROUP * ATTN_OFFSETS, ATTN_QB, ATTN_KB), F32)]
    if nt > 1:
        scratch += [pltpu.VMEM((n_sub, GROUP_WIDTH), BF16)] * 4 + [pltpu.VMEM((n_sub, LANES), F32)]
    out, lse = pl.pallas_call(
        functools.partial(_attn_kernel, group=group, dil=dil, nt=nt, n=n),
        grid=(batch, dil),
        in_specs=[pl.BlockSpec(blk, lambda b, r: (b, 0, r, 0, cb)),
                  pl.BlockSpec(blk, lambda b, r: (b, 0, r, 0, cb + 1)),
                  pl.BlockSpec(blk, lambda b, r: (b, 0, r, 0, cb + 2))],
        out_specs=[pl.BlockSpec(blk, lambda b, r: (b, 0, r, 0, 0)),
                   pl.BlockSpec((None, nt, None, n, LANES), lambda b, r: (b, 0, r, 0, 0))],
        out_shape=[jax.ShapeDtypeStruct((batch, nt, dil, n, GROUP_WIDTH), BF16),
                   jax.ShapeDtypeStruct((batch, nt, dil, n, LANES), F32)],
        scratch_shapes=scratch,
        compiler_params=_cparams(("arbitrary", "arbitrary")),
        name=f"attn_g{group}",
    )(qkv5, qkv5, qkv5)
    return out.reshape(t, GROUP_WIDTH), lse.reshape(t, LANES)


def _log_sigmoid(x):
    return jnp.minimum(x, 0.0) - jnp.log(1.0 + jnp.exp(-jnp.abs(x)))


RET_HEAD_COLS = 2 * RET_QK + RET_V


def _retention_kernel(decay_ref, qkv_ref, g_ref, gain_ref, o_ref, state, ybwd, dmat, kdec, qdec, *, nc):
    c = RET_CHUNK
    h = pl.program_id(1)
    i = pl.program_id(2)
    kscale = RET_QK ** -0.5

    def set_decays(lg, forward):
        row = lax.broadcasted_iota(I32, (c, c), 0)
        col = lax.broadcasted_iota(I32, (c, c), 1)
        pos = lax.broadcasted_iota(I32, (c, LANES), 0).astype(F32)
        if forward:
            gap, key_pow, query_pow = row - col, (c - 1.0) - pos, pos + 1.0
            keep = gap >= 0
        else:
            gap, key_pow, query_pow = col - row, pos, c - pos
            keep = gap > 0
        dmat[...] = jnp.where(keep, jnp.exp(lg * jnp.maximum(gap, 0).astype(F32)) * kscale, 0.0)
        kdec[...] = jnp.exp(lg * key_pow) * kscale
        qdec[...] = jnp.exp(lg * query_pow)

    def chunk_step(lg, r0):
        q = qkv_ref[r0:r0 + c, 0:RET_QK]
        k = qkv_ref[r0:r0 + c, RET_QK:2 * RET_QK]
        v = qkv_ref[r0:r0 + c, 2 * RET_QK:]
        inner = lax.dot_general(q, k, (((1,), (1,)), ((), ())), preferred_element_type=F32) * dmat[...]
        y = _dot(inner.astype(BF16), v)
        qd = jnp.concatenate([qdec[...]] * (RET_V // LANES), axis=1)
        y = y + _dot(q, state[...].astype(BF16)) * qd
        kd = (k.astype(F32) * jnp.concatenate([kdec[...]] * (RET_QK // LANES), axis=1)).astype(BF16)
        kv = lax.dot_general(kd, v, (((0,), (0,)), ((), ())), preferred_element_type=F32)
        state[...] = state[...] * jnp.exp(lg * float(c)) + kv
        return y

    @pl.when((i == 0) | (i == nc))
    def _():
        state[...] = jnp.zeros_like(state)

    @pl.when(i < nc)
    def _():
        lg = _log_sigmoid(jnp.zeros((1, 1), F32) + decay_ref[RET_HEADS + h])

        @pl.when(i == 0)
        def _():
            set_decays(lg, False)

        base = pl.multiple_of((nc - 1 - i) * RET_STEP, RET_STEP)
        for r0 in reversed(range(0, RET_STEP, c)):
            ybwd[pl.ds(base + r0, c), :] = chunk_step(lg, r0)

    @pl.when(i >= nc)
    def _():
        lg = _log_sigmoid(jnp.zeros((1, 1), F32) + decay_ref[h])

        @pl.when(i == nc)
        def _():
            set_decays(lg, True)

        base = pl.multiple_of((i - nc) * RET_STEP, RET_STEP)
        for r0 in range(0, RET_STEP, c):
            y = chunk_step(lg, r0) + ybwd[pl.ds(base + r0, c), :]
            g = g_ref[r0:r0 + c, :].astype(F32)
            o_ref[r0:r0 + c, :] = (g * _sigmoid(g) * (_ln(y) * gain_ref[...])).astype(BF16)


def _retention(rest, decays, gn_gain, batch, seq):
    c = RET_STEP
    nc = seq // c
    rest3 = rest.reshape(batch, seq, rest.shape[1])
    gate_blk = RET_HEADS * RET_HEAD_COLS // RET_V

    def chunk(i):
        return jnp.where(i < nc, nc - 1 - i, i - nc)

    grid_spec = pltpu.PrefetchScalarGridSpec(
        num_scalar_prefetch=1,
        grid=(batch, RET_HEADS, 2 * nc),
        in_specs=[pl.BlockSpec((None, c, RET_HEAD_COLS), lambda b, h, i, d: (b, chunk(i), h)),
                  pl.BlockSpec((None, c, RET_V), lambda b, h, i, d: (b, jnp.maximum(i - nc, 0), gate_blk + h)),
                  pl.BlockSpec((1, RET_V), lambda b, h, i, d: (0, h))],
        out_specs=pl.BlockSpec((None, c, RET_V), lambda b, h, i, d: (b, jnp.maximum(i - nc, 0), h)),
        scratch_shapes=[pltpu.VMEM((RET_QK, RET_V), F32), pltpu.VMEM((seq, RET_V), F32),
                        pltpu.VMEM((RET_CHUNK, RET_CHUNK), F32), pltpu.VMEM((RET_CHUNK, LANES), F32),
                        pltpu.VMEM((RET_CHUNK, LANES), F32)],
    )
    out = pl.pallas_call(
        functools.partial(_retention_kernel, nc=nc),
        grid_spec=grid_spec,
        out_shape=jax.ShapeDtypeStruct((batch, seq, RET_HEADS * RET_V), BF16),
        compiler_params=_cparams(("arbitrary", "arbitrary", "arbitrary")),
        name="retention",
    )(decays, rest3, rest3, gn_gain)
    return out.reshape(batch * seq, RET_HEADS * RET_V)


MERGE_TM = PERM_TILE
MERGE_SUB = MERGE_TM


def _merge_kernel(o0_ref, o1_ref, o2_ref, l0_ref, l1_ref, l2_ref, retg_ref, ga_ref, gr_ref, x_ref, mod_ref,
                  wa_ref, wr_ref, wo_ref, ln_ref, wrh_ref, wrl_ref, rb_ref,
                  h1_ref, u2_ref, lg_ref, on_scr, ln_scr):
    for g in (1, 2):
        dil = ATTN_PATTERNS[g][1]
        n = MERGE_TM // dil
        o_ref, l_ref = ((o1_ref, l1_ref), (o2_ref, l2_ref))[g - 1]
        for res in range(dil):
            rows = o_ref[res * n:(res + 1) * n, :].astype(F32)
            for hh in range(HEADS_PER_GROUP):
                on_scr[g - 1, hh, pl.ds(res, n, stride=dil), :] = rows[:, hh * HEAD_DIM:(hh + 1) * HEAD_DIM]
            ln_scr[g - 1, pl.ds(res, n, stride=dil), :] = l_ref[res * n:(res + 1) * n, :]

    for r0 in range(0, MERGE_TM, MERGE_SUB):
        rs = slice(r0, r0 + MERGE_SUB)
        l0, l1, l2 = l0_ref[rs, :], ln_scr[0, rs, :], ln_scr[1, rs, :]
        lm = jnp.maximum(jnp.maximum(l0, l1), l2)
        e0, e1, e2 = jnp.exp(l0 - lm), jnp.exp(l1 - lm), jnp.exp(l2 - lm)
        inv = 1.0 / (e0 + e1 + e2)
        parts = []
        for hh in range(HEADS_PER_GROUP):
            sl = slice(hh * HEAD_DIM, (hh + 1) * HEAD_DIM)
            acc = (e0[:, hh:hh + 1] * o0_ref[rs, sl].astype(F32)
                   + e1[:, hh:hh + 1] * on_scr[0, hh, rs, :]
                   + e2[:, hh:hh + 1] * on_scr[1, hh, rs, :])
            parts.append((acc * inv[:, hh:hh + 1]).astype(BF16))
        attn = jnp.concatenate(parts, axis=1)

        branch_a = _dot(attn, wa_ref[...])
        branch_r = _dot(retg_ref[rs, :], wr_ref[...])
        merged = (_sigmoid(ga_ref[rs, :].astype(F32)) * branch_a
                  + _sigmoid(gr_ref[rs, :].astype(F32)) * branch_r)
        y = _dot(merged.astype(BF16), wo_ref[...])

        h1 = _ln(DEEPNORM_ALPHA * x_ref[rs, :] + mod_ref[2:3, :] * y) * ln_ref[0:1, :] + ln_ref[1:2, :]
        h1_ref[rs, :] = h1
        u2 = _ln(h1) * (1.0 + mod_ref[4:5, :]) + mod_ref[3:4, :]
        _store_token_tiles(u2_ref, u2, r0)
        uh, ul = _split_bf16(u2)
        lg_ref[rs, :] = (_dot(uh, wrh_ref[...]) + _dot(uh, wrl_ref[...]) + _dot(ul, wrh_ref[...])
                         + rb_ref[...])


def _merge(outs, lses, retg, rest, x2, mod3, wa, wr, wo, ln1, wr_hi, wr_lo, rbias, seq):
    t = x2.shape[0]
    tm = MERGE_TM
    per_seq = seq // tm
    row = lambda w: pl.BlockSpec((tm, w), lambda i: (i, 0))
    full = lambda a: pl.BlockSpec(a.shape, lambda i: (0,) * a.ndim)
    return pl.pallas_call(
        _merge_kernel,
        grid=(t // tm,),
        in_specs=[row(GROUP_WIDTH)] * 3 + [row(LANES)] * 3 + [
            row(RET_HEADS * RET_V),
            pl.BlockSpec((tm, D_MODEL), lambda i: (i, 6)),
            pl.BlockSpec((tm, D_MODEL), lambda i: (i, 7)),
            row(D_MODEL),
            pl.BlockSpec((None, 6, D_MODEL), lambda i: (i // per_seq, 0, 0)),
            full(wa), full(wr), full(wo), full(ln1), full(wr_hi), full(wr_lo), full(rbias)],
        out_specs=[row(D_MODEL), pl.BlockSpec((tm * ROW_TILE, LANES), lambda i: (i, 0)), row(LANES)],
        out_shape=[jax.ShapeDtypeStruct((t, D_MODEL), F32),
                   jax.ShapeDtypeStruct((t * ROW_TILE, LANES), F32),
                   jax.ShapeDtypeStruct((t, LANES), F32)],
        scratch_shapes=[pltpu.VMEM((2, HEADS_PER_GROUP, tm, HEAD_DIM), F32), pltpu.VMEM((2, tm, LANES), F32)],
        compiler_params=_cparams(("arbitrary",)),
        name="merge",
    )(*outs, *lses, retg, rest, rest, x2, mod3, wa, wr, wo, ln1, wr_hi, wr_lo, rbias)


ROUTE_TM = 512
BIG = 1 << 20


def _route_kernel(lg_ref, cols_ref, ints_ref, cnt_ref, carry):
    i = pl.program_id(0)

    @pl.when(i == 0)
    def _():
        carry[...] = jnp.zeros_like(carry)

    tm = ROUTE_TM
    lg = lg_ref[...]
    lane = lax.broadcasted_iota(I32, (tm, LANES), 1)
    lane_f = lane.astype(F32)
    first = lambda mask: jnp.min(jnp.where(mask, lane_f, float(BIG)), axis=-1, keepdims=True).astype(I32)

    coarse = jnp.where(lane < N_GROUPS, lg, NEG)
    cmax = jnp.max(coarse, axis=-1, keepdims=True)
    gsel = first(coarse == cmax)
    p_group = 1.0 / jnp.sum(jnp.exp(coarse - cmax), axis=-1, keepdims=True)

    lo = N_GROUPS + EXPERTS_PER_GROUP * gsel
    fine = jnp.where((lane >= lo) & (lane < lo + EXPERTS_PER_GROUP), lg, NEG)
    v1 = jnp.max(fine, axis=-1, keepdims=True)
    i1 = first(fine == v1)
    fine2 = jnp.where(lane == i1, NEG, fine)
    v2 = jnp.max(fine2, axis=-1, keepdims=True)
    i2 = first(fine2 == v2)
    ex = jnp.exp(v2 - v1)
    den = 1.0 / (1.0 + ex)
    gate1 = p_group * den
    gate2 = p_group * (ex * den)
    e1 = i1 - N_GROUPS
    e2 = i2 - N_GROUPS

    oh1 = lane == e1
    oh2 = lane == e2
    cnt = jnp.where(oh1 | oh2, 1.0, 0.0)
    r_i = lax.broadcasted_iota(I32, (tm, tm), 0)
    c_i = lax.broadcasted_iota(I32, (tm, tm), 1)
    tri = jnp.where(r_i > c_i, 1.0, 0.0).astype(BF16)
    rank = _dot(tri, cnt.astype(BF16)) + carry[...]
    r1 = jnp.sum(jnp.where(oh1, rank, 0.0), axis=-1, keepdims=True)
    r2 = jnp.sum(jnp.where(oh2, rank, 0.0), axis=-1, keepdims=True)
    carry[...] = carry[...] + jnp.sum(cnt, axis=0, keepdims=True)
    cnt_ref[...] = jnp.broadcast_to(carry[...], cnt_ref.shape)

    cols_ref[...] = jnp.where(lane == 0, gate1, jnp.where(lane == 1, gate2, 0.0))
    packed = jnp.where(lane == 0, e1.astype(F32),
                       jnp.where(lane == 1, e2.astype(F32),
                                 jnp.where(lane == 2, r1, jnp.where(lane == 3, r2, 0.0))))
    ints_ref[...] = packed.T[0:8, :].astype(I32)


def _route(logits):
    t = logits.shape[0]
    tm = ROUTE_TM
    return pl.pallas_call(
        _route_kernel,
        grid=(t // tm,),
        in_specs=[pl.BlockSpec((tm, LANES), lambda i: (i, 0))],
        out_specs=[pl.BlockSpec((tm, LANES), lambda i: (i, 0)),
                   pl.BlockSpec((8, tm), lambda i: (0, i)),
                   pl.BlockSpec((8, LANES), lambda i: (0, 0))],
        out_shape=[jax.ShapeDtypeStruct((t, LANES), F32),
                   jax.ShapeDtypeStruct((8, t), I32),
                   jax.ShapeDtypeStruct((8, LANES), F32)],
        scratch_shapes=[pltpu.VMEM((1, LANES), F32)],
        compiler_params=_cparams(("arbitrary",)),
        name="route",
    )(logits)


def _plan_kernel(ints_ref, cnt_ref, dest_ref, meta_ref, *, n_blocks_pad):
    sub = lax.broadcasted_iota(I32, (LANES, LANES), 0)
    lane = lax.broadcasted_iota(I32, (LANES, LANES), 1)
    cnt = cnt_ref[0:1, :]
    nblk_row = jnp.floor((cnt + (MOE_BLK - 1.0)) * (1.0 / MOE_BLK))
    nblk_mat = jnp.broadcast_to(nblk_row, (LANES, LANES))
    start_col = jnp.sum(jnp.where(lane < sub, nblk_mat, 0.0), axis=-1, keepdims=True)
    nblk_col = jnp.sum(jnp.where(lane == sub, nblk_mat, 0.0), axis=-1, keepdims=True)
    end_col = start_col + nblk_col

    ints = ints_ref[...]
    base = jnp.zeros(ints.shape, F32)
    for e in range(N_EXPERTS):
        base = jnp.where(ints == e, start_col[e:e + 1, :] * float(MOE_BLK), base)
    dest = base[0:2, :].astype(I32) + ints[2:4, :]
    dest_ref[...] = jnp.concatenate([dest, jnp.zeros((6, ints.shape[1]), I32)], axis=0)

    blk = lax.broadcasted_iota(I32, (LANES, n_blocks_pad), 1).astype(F32)
    e_sub = lax.broadcasted_iota(I32, (LANES, n_blocks_pad), 0)
    done = jnp.where((e_sub < N_EXPERTS) & (end_col <= blk), 1.0, 0.0)
    bexp = jnp.minimum(jnp.sum(done, axis=0, keepdims=True), N_EXPERTS - 1.0)
    used = jnp.sum(nblk_row, axis=-1, keepdims=True)
    row = lax.broadcasted_iota(I32, (8, n_blocks_pad), 0)
    meta = jnp.where(row == 0, bexp, jnp.where(row == 1, used, 0.0))
    meta_ref[...] = meta.astype(I32)


def _plan(ints, counts, n_blocks_pad):
    t = ints.shape[1]
    return pl.pallas_call(
        functools.partial(_plan_kernel, n_blocks_pad=n_blocks_pad),
        out_shape=[jax.ShapeDtypeStruct((8, t), I32), jax.ShapeDtypeStruct((8, n_blocks_pad), I32)],
        compiler_params=pltpu.CompilerParams(vmem_limit_bytes=VMEM_LIMIT),
        name="plan",
    )(ints, counts)


DISPATCH_TM = 512
DMA_UNROLL = 8


def _row_tile(ref, r):
    return ref.at[pl.ds(pl.multiple_of(r * ROW_TILE, ROW_TILE), ROW_TILE)]


def _dispatch_kernel(d0_ref, d1_ref, u2_ref, rows_in_ref, rows_ref, sem):
    del rows_in_ref

    def issue(t, carry):
        src = _row_tile(u2_ref, t)
        pltpu.make_async_copy(src, _row_tile(rows_ref, d0_ref[t]), sem).start(priority=0)
        pltpu.make_async_copy(src, _row_tile(rows_ref, d1_ref[t]), sem).start(priority=1)
        return carry

    lax.fori_loop(0, DISPATCH_TM, issue, 0, unroll=DMA_UNROLL)
    for _ in range(2):
        pltpu.make_async_copy(u2_ref, rows_ref.at[pl.ds(0, DISPATCH_TM * ROW_TILE)], sem).wait()


def _dispatch(dest0, dest1, u2, n_rows):
    t = u2.shape[0] // ROW_TILE
    rows0 = jnp.zeros((n_rows * ROW_TILE, LANES), F32)
    idx = pl.BlockSpec((DISPATCH_TM,), lambda i: (i,), memory_space=pltpu.SMEM)
    return pl.pallas_call(
        _dispatch_kernel,
        grid=(t // DISPATCH_TM,),
        in_specs=[idx, idx,
                  pl.BlockSpec((DISPATCH_TM * ROW_TILE, LANES), lambda i: (i, 0)),
                  pl.BlockSpec(memory_space=pl.ANY)],
        out_specs=pl.BlockSpec(memory_space=pl.ANY),
        out_shape=jax.ShapeDtypeStruct((n_rows * ROW_TILE, LANES), F32),
        scratch_shapes=[pltpu.SemaphoreType.DMA(())],
        input_output_aliases={3: 0},
        compiler_params=_cparams(("arbitrary",)),
        name="dispatch",
    )(dest0, dest1, u2, rows0)


def _expert_runs(bexp, used):
    n = bexp.shape[0]
    idx = jnp.arange(n, dtype=I32)
    first = (idx < used[0]) & ((idx == 0) | (bexp != jnp.roll(bexp, 1)))
    slot = (jnp.cumsum(first.astype(I32)) - 1) % 2
    first_at_or_after = lax.cummin(jnp.where(first, idx, n)[::-1])[::-1]
    first_after = jnp.concatenate([first_at_or_after[1:], jnp.full((1,), n, I32)])
    nxt = jnp.where(first_after < n, bexp[jnp.minimum(first_after, n - 1)], -1)
    return first.astype(I32), slot.astype(I32), nxt.astype(I32)


def _experts_kernel(bexp_ref, used_ref, first_ref, slot_ref, nxt_ref, x_ref, w1_hbm, w3_hbm, w2_hbm, y_ref,
                    wb1, wb3, wb2, w1s, w3s, w2s, sem):
    i = pl.program_id(0)
    active = i < used_ref[0]
    slot = slot_ref[i]

    def fetch(e, s):
        return [pltpu.make_async_copy(w.at[e], buf.at[s], sem.at[s])
                for w, buf in ((w1_hbm, wb1), (w3_hbm, wb3), (w2_hbm, wb2))]

    @pl.when(i == 0)
    def _():
        for cp in fetch(bexp_ref[0], 0):
            cp.start()

    @pl.when(first_ref[i] == 1)
    def _():
        @pl.when(nxt_ref[i] >= 0)
        def _():
            for cp in fetch(nxt_ref[i], 1 - slot):
                cp.start()

        for cp in fetch(bexp_ref[i], slot):
            cp.wait()
        w1s[...] = wb1[slot].astype(BF16)
        w3s[...] = wb3[slot].astype(BF16)
        w2s[...] = wb2[slot].astype(BF16)

    @pl.when(active)
    def _():
        xb = _load_token_tiles(x_ref, MOE_BLK).astype(BF16)
        a = _dot(xb, w1s[...])
        b = _dot(xb, w3s[...])
        hdn = (a * _sigmoid(a) * b).astype(BF16)
        _store_token_tiles(y_ref, _dot(hdn, w2s[...]))

    @pl.when(jnp.logical_not(active))
    def _():
        y_ref[...] = jnp.zeros_like(y_ref)


def _experts(bexp, used, rows, w1, w3, w2):
    n_blocks = rows.shape[0] // (MOE_BLK * ROW_TILE)
    first, slot, nxt = _expert_runs(bexp, used)
    any_space = pl.BlockSpec(memory_space=pl.ANY)
    grid_spec = pltpu.PrefetchScalarGridSpec(
        num_scalar_prefetch=5,
        grid=(n_blocks,),
        in_specs=[pl.BlockSpec((MOE_BLK * ROW_TILE, LANES), lambda i, be, nu, *_: (jnp.minimum(i, nu[0] - 1), 0)),
                  any_space, any_space, any_space],
        out_specs=pl.BlockSpec((MOE_BLK * ROW_TILE, LANES), lambda i, *_: (i, 0)),
        scratch_shapes=[pltpu.VMEM((2, D_MODEL, EXPERT_FF), F32), pltpu.VMEM((2, D_MODEL, EXPERT_FF), F32),
                        pltpu.VMEM((2, EXPERT_FF, D_MODEL), F32),
                        pltpu.VMEM((D_MODEL, EXPERT_FF), BF16), pltpu.VMEM((D_MODEL, EXPERT_FF), BF16),
                        pltpu.VMEM((EXPERT_FF, D_MODEL), BF16), pltpu.SemaphoreType.DMA((2,))],
    )
    return pl.pallas_call(
        _experts_kernel,
        grid_spec=grid_spec,
        out_shape=jax.ShapeDtypeStruct(rows.shape, F32),
        compiler_params=_cparams(("arbitrary",)),
        name="experts",
    )(bexp, used, first, slot, nxt, rows, w1, w3, w2)


COMBINE_TM = 256


def _combine_kernel(d0_ref, d1_ref, n0_ref, n1_ref, y_ref, cols_ref, h1_ref, mod_ref, ln_ref, o_ref, ya, yb, sem):
    i = pl.program_id(0)
    slot = i % 2

    def gather(i0_ref, i1_ref, s):
        def issue(t, carry):
            pltpu.make_async_copy(_row_tile(y_ref, i0_ref[t]), _row_tile(ya.at[s], t), sem.at[s]).start(priority=0)
            pltpu.make_async_copy(_row_tile(y_ref, i1_ref[t]), _row_tile(yb.at[s], t), sem.at[s]).start(priority=1)
            return carry

        lax.fori_loop(0, COMBINE_TM, issue, 0, unroll=DMA_UNROLL)

    @pl.when(i == 0)
    def _():
        gather(d0_ref, d1_ref, 0)

    @pl.when(i + 1 < pl.num_programs(0))
    def _():
        gather(n0_ref, n1_ref, 1 - slot)

    for buf in (ya, yb):
        pltpu.make_async_copy(y_ref.at[pl.ds(0, COMBINE_TM * ROW_TILE)], buf.at[slot], sem.at[slot]).wait()

    cols = cols_ref[...]
    moe = (cols[:, 0:1] * _load_token_tiles(ya.at[slot], COMBINE_TM)
           + cols[:, 1:2] * _load_token_tiles(yb.at[slot], COMBINE_TM))
    pre = DEEPNORM_ALPHA * h1_ref[...] + mod_ref[5:6, :] * moe
    o_ref[...] = _ln(pre) * ln_ref[0:1, :] + ln_ref[1:2, :]


def _combine(dest0, dest1, y_rows, cols, h1, mod3, ln2, seq):
    t = h1.shape[0]
    tm = COMBINE_TM
    per_seq = seq // tm
    idx = pl.BlockSpec((tm,), lambda i: (i,), memory_space=pltpu.SMEM)
    idx_next = pl.BlockSpec((tm,), lambda i: (jnp.minimum(i + 1, t // tm - 1),), memory_space=pltpu.SMEM)
    return pl.pallas_call(
        _combine_kernel,
        grid=(t // tm,),
        in_specs=[idx, idx, idx_next, idx_next,
                  pl.BlockSpec(memory_space=pl.ANY),
                  pl.BlockSpec((tm, LANES), lambda i: (i, 0)),
                  pl.BlockSpec((tm, D_MODEL), lambda i: (i, 0)),
                  pl.BlockSpec((None, 6, D_MODEL), lambda i: (i // per_seq, 0, 0)),
                  pl.BlockSpec((2, D_MODEL), lambda i: (0, 0))],
        out_specs=pl.BlockSpec((tm, D_MODEL), lambda i: (i, 0)),
        out_shape=jax.ShapeDtypeStruct((t, D_MODEL), F32),
        scratch_shapes=[pltpu.VMEM((2, tm * ROW_TILE, LANES), F32), pltpu.VMEM((2, tm * ROW_TILE, LANES), F32),
                        pltpu.SemaphoreType.DMA((2,))],
        compiler_params=_cparams(("arbitrary",)),
        name="combine",
    )(dest0, dest1, dest0, dest1, y_rows, cols, h1, mod3, ln2)


def _layer(h, c8, w_ada, b_ada, w_in, w_attn_out, decay_f, decay_b, gn_gain, w_ret_out, w_out,
           ln1_gain, ln1_bias, w_coarse, b_coarse, w_fine, b_fine, w1, w3, w2, ln2_gain, ln2_bias):
    batch, seq, d = h.shape
    t = batch * seq
    x2 = h.reshape(t, d)

    mod = _ada(c8, w_ada, b_ada.reshape(1, -1))
    mod3 = mod[:batch].reshape(batch, 6, d)

    aw = ATTN_HEADS * HEAD_DIM
    wq, wk, wv = (w_in[:, s * aw:(s + 1) * aw] for s in range(3))
    gcols = lambda w, g: w[:, g * GROUP_WIDTH:(g + 1) * GROUP_WIDTH]
    w_qkv = jnp.concatenate([gcols(w, g) for g in range(ATTN_GROUPS) for w in (wq, wk, wv)], axis=1).astype(BF16)
    rq0, rk0, rv0, tail0 = 3 * aw, 3 * aw + RET_HEADS * RET_QK, 3 * aw + 2 * RET_HEADS * RET_QK, \
        3 * aw + 2 * RET_HEADS * RET_QK + RET_HEADS * RET_V
    head_cols = [w_in[:, o + hh * wd:o + (hh + 1) * wd]
                 for hh in range(RET_HEADS) for o, wd in ((rq0, RET_QK), (rk0, RET_QK), (rv0, RET_V))]
    w_rest = jnp.concatenate(head_cols + [w_in[:, tail0:]], axis=1).astype(BF16)

    qkv = _inproj_attn(x2, mod3, w_qkv, seq)
    rest = _inproj_rest(x2, mod3, w_rest, seq)

    outs, lses = zip(*[_attention(qkv, g, batch, seq) for g in range(ATTN_GROUPS)])
    decays = jnp.concatenate([decay_f, decay_b]).astype(F32)
    retg = _retention(rest, decays, gn_gain.reshape(1, -1), batch, seq)

    w_route = jnp.concatenate([w_coarse] + [w_fine[g] for g in range(N_GROUPS)], axis=1)
    n_route = w_route.shape[1]
    w_route = jnp.pad(w_route, ((0, 0), (0, LANES - n_route)))
    wr_hi, wr_lo = _split_bf16(w_route)
    rbias = jnp.pad(jnp.concatenate([b_coarse, b_fine.reshape(-1)]), (0, LANES - n_route)).reshape(1, LANES)
    ln1 = jnp.stack([ln1_gain, ln1_bias])
    h1, u2, logits = _merge(outs, lses, retg, rest, x2, mod3, w_attn_out.astype(BF16), w_ret_out.astype(BF16),
                            w_out.astype(BF16), ln1, wr_hi, wr_lo, rbias, seq)

    cols, ints, counts = _route(logits)
    n_blocks = 2 * t // MOE_BLK + N_EXPERTS
    n_blocks_pad = -(-n_blocks // LANES) * LANES
    dest, meta = _plan(ints, counts, n_blocks_pad)
    dest0, dest1 = dest[0], dest[1]
    rows = _dispatch(dest0, dest1, u2, n_blocks * MOE_BLK)
    y_rows = _experts(meta[0, :n_blocks], meta[1, :1], rows, w1, w3, w2)
    out = _combine(dest0, dest1, y_rows, cols, h1, mod3, jnp.stack([ln2_gain, ln2_bias]), seq)
    return out.reshape(batch, seq, d)


def kernel(x, c, w_ada, b_ada, w_in, w_attn_out, ret_decay_fwd, ret_decay_bwd, ret_gn_gain, w_ret_out, w_out,
           ln1_gain, ln1_bias, w_coarse, b_coarse, w_fine, b_fine, w1, w3, w2, ln2_gain, ln2_bias):
    batch = x.shape[0]
    assert batch <= 8 and x.shape[1] % (2 * INPROJ_TM) == 0 and x.shape[2] == D_MODEL
    c8 = jnp.pad(c, ((0, 8 - batch), (0, 0)))
    h = x
    for l in range(w_ada.shape[0]):
        h = _layer(h, c8, w_ada[l], b_ada[l], w_in[l], w_attn_out[l], ret_decay_fwd[l], ret_decay_bwd[l],
                   ret_gn_gain[l], w_ret_out[l], w_out[l], ln1_gain[l], ln1_bias[l], w_coarse[l], b_coarse[l],
                   w_fine[l], b_fine[l], w1[l], w3[l], w2[l], ln2_gain[l], ln2_bias[l])
    return h
```

```python
import functools
import math

import jax
import jax.numpy as jnp
from jax import lax
from jax.experimental import pallas as pl
from jax.experimental.pallas import tpu as pltpu

F32 = jnp.float32
BF16 = jnp.bfloat16
I32 = jnp.int32

D_MODEL = 1024
ATTN_GROUPS = 3
HEADS_PER_GROUP = 4
HEAD_DIM = 128
ATTN_HEADS = ATTN_GROUPS * HEADS_PER_GROUP
GROUP_WIDTH = HEADS_PER_GROUP * HEAD_DIM
ATTN_PATTERNS = ((128, 1), (512, 4), (2048, 16))
ALIBI_MAX_EXP = 8.0
NEG = -1e30
RET_HEADS = 4
RET_QK = 256
RET_V = 512
N_GROUPS = 4
EXPERTS_PER_GROUP = 8
N_EXPERTS = N_GROUPS * EXPERTS_PER_GROUP
EXPERT_FF = 512
DEPTH = 1
DEEPNORM_ALPHA = (2.0 * DEPTH) ** 0.25
LN_EPS = 1e-5

LANES = 128
PERM_TILE = 512
ATTN_QB = 128
ATTN_HALF = 64
ATTN_KB = ATTN_QB + 2 * ATTN_HALF
RET_CHUNK = 256
RET_STEP = 1024
MOE_BLK = 512
VMEM_LIMIT = 56 * 1024 * 1024


def _cparams(sem):
    return pltpu.CompilerParams(dimension_semantics=sem, vmem_limit_bytes=VMEM_LIMIT)


def _split_bf16(a):
    hi = a.astype(BF16)
    lo = (a - hi.astype(F32)).astype(BF16)
    return hi, lo


def _dot(a, b):
    return jnp.dot(a, b, preferred_element_type=F32)


def _dot3(a, b):
    ah, al = _split_bf16(a)
    bh, bl = _split_bf16(b)
    return _dot(ah, bh) + _dot(ah, bl) + _dot(al, bh)


def _ln(x):
    mu = jnp.mean(x, axis=-1, keepdims=True)
    xc = x - mu
    var = jnp.mean(xc * xc, axis=-1, keepdims=True)
    return xc * lax.rsqrt(var + LN_EPS)


def _sigmoid(x):
    return 1.0 / (1.0 + jnp.exp(-x))


ROW_TILE = D_MODEL // LANES


def _store_token_tiles(ref, val, row0=0):
    n = val.shape[0]
    for cc in range(ROW_TILE):
        ref[pl.ds(row0 * ROW_TILE + cc, n, stride=ROW_TILE), :] = val[:, cc * LANES:(cc + 1) * LANES]


def _load_token_tiles(ref, n):
    return jnp.concatenate([ref[pl.ds(cc, n, stride=ROW_TILE), :] for cc in range(ROW_TILE)], axis=1)


def _ada_kernel(c_ref, w_ref, b_ref, o_ref):
    o_ref[...] = _dot3(c_ref[...], w_ref[...]) + b_ref[...]


def _ada(c8, w_ada, b_ada):
    n = w_ada.shape[1]
    return pl.pallas_call(
        _ada_kernel,
        grid=(n // D_MODEL,),
        in_specs=[pl.BlockSpec((8, D_MODEL), lambda j: (0, 0)),
                  pl.BlockSpec((D_MODEL, D_MODEL), lambda j: (0, j)),
                  pl.BlockSpec((1, D_MODEL), lambda j: (0, j))],
        out_specs=pl.BlockSpec((8, D_MODEL), lambda j: (0, j)),
        out_shape=jax.ShapeDtypeStruct((8, n), F32),
        compiler_params=_cparams(("arbitrary",)),
        name="ada",
    )(c8, w_ada, b_ada)


INPROJ_TM = 1024
LN_ROWS = 256


def _modulated_ln(x_ref, mod_ref, r0, rows):
    x = x_ref[r0:r0 + rows, :]
    return _ln(x) * (1.0 + mod_ref[1:2, :]) + mod_ref[0:1, :]


def _inproj_attn_kernel(x_ref, mod_ref, w_ref, o_ref, uf_scr, u_scr):
    j = pl.program_id(1)

    @pl.when(j == 0)
    def _():
        for r0 in range(0, INPROJ_TM, LN_ROWS):
            u = _modulated_ln(x_ref, mod_ref, r0, LN_ROWS)
            for cc in range(D_MODEL // LANES):
                uf_scr[cc, r0:r0 + LN_ROWS, :] = u[:, cc * LANES:(cc + 1) * LANES]
            u_scr[0, r0:r0 + LN_ROWS, :] = u.astype(BF16)
        for g in (1, 2):
            dil = ATTN_PATTERNS[g][1]
            n = PERM_TILE // dil
            for t0 in range(0, INPROJ_TM, PERM_TILE):
                for res in range(dil):
                    for cc in range(D_MODEL // LANES):
                        rows = uf_scr[cc, pl.ds(t0 + res, n, stride=dil), :]
                        u_scr[g, t0 + res * n:t0 + (res + 1) * n, cc * LANES:(cc + 1) * LANES] = rows.astype(BF16)

    o_ref[...] = _dot(u_scr[j], w_ref[...]).astype(BF16)


def _inproj_attn(x2, mod3, w_qkv, seq):
    t = x2.shape[0]
    tn = 3 * GROUP_WIDTH
    per_seq = seq // INPROJ_TM
    return pl.pallas_call(
        _inproj_attn_kernel,
        grid=(t // INPROJ_TM, ATTN_GROUPS),
        in_specs=[pl.BlockSpec((INPROJ_TM, D_MODEL), lambda i, j: (i, 0)),
                  pl.BlockSpec((None, 6, D_MODEL), lambda i, j: (i // per_seq, 0, 0)),
                  pl.BlockSpec((D_MODEL, tn), lambda i, j: (0, j))],
        out_specs=pl.BlockSpec((INPROJ_TM, tn), lambda i, j: (i, j)),
        out_shape=jax.ShapeDtypeStruct((t, ATTN_GROUPS * tn), BF16),
        scratch_shapes=[pltpu.VMEM((D_MODEL // LANES, INPROJ_TM, LANES), F32),
                        pltpu.VMEM((ATTN_GROUPS, INPROJ_TM, D_MODEL), BF16)],
        compiler_params=_cparams(("arbitrary", "arbitrary")),
        name="inproj_attn",
    )(x2, mod3, w_qkv)


def _inproj_rest_kernel(x_ref, mod_ref, w_ref, o_ref, u_scr):
    @pl.when(pl.program_id(1) == 0)
    def _():
        for r0 in range(0, INPROJ_TM, LN_ROWS):
            u_scr[r0:r0 + LN_ROWS, :] = _modulated_ln(x_ref, mod_ref, r0, LN_ROWS).astype(BF16)

    o_ref[...] = _dot(u_scr[...], w_ref[...]).astype(BF16)


def _inproj_rest(x2, mod3, w_rest, seq):
    t = x2.shape[0]
    n = w_rest.shape[1]
    tn = 1024
    per_seq = seq // INPROJ_TM
    return pl.pallas_call(
        _inproj_rest_kernel,
        grid=(t // INPROJ_TM, n // tn),
        in_specs=[pl.BlockSpec((INPROJ_TM, D_MODEL), lambda i, j: (i, 0)),
                  pl.BlockSpec((None, 6, D_MODEL), lambda i, j: (i // per_seq, 0, 0)),
                  pl.BlockSpec((D_MODEL, tn), lambda i, j: (0, j))],
        out_specs=pl.BlockSpec((INPROJ_TM, tn), lambda i, j: (i, j)),
        out_shape=jax.ShapeDtypeStruct((t, n), BF16),
        scratch_shapes=[pltpu.VMEM((INPROJ_TM, D_MODEL), BF16)],
        compiler_params=_cparams(("arbitrary", "arbitrary")),
        name="inproj_rest",
    )(x2, mod3, w_rest)


ATTN_OFFSETS = ATTN_KB // ATTN_HALF - 1
ATTN_UNROLL = 4
ATTN_MIN_BLOCKS = 4


def _attn_kernel(q_ref, k_ref, v_ref, o_ref, lse_ref, bias_scr, *scratch, group, dil, nt, n, nres):
    n_sub = nt * n
    if nt == 1:
        seqs = [(q_ref.at[0, rr], k_ref.at[0, rr], v_ref.at[0, rr], o_ref.at[0, rr], lse_ref.at[0, rr])
                for rr in range(nres)]
    else:
        seqs = [tuple(s.at[rr] for s in scratch) for rr in range(nres)]
        for rr, (qs, ks, vs, _, _) in enumerate(seqs):
            for t in range(nt):
                qs[t * n:(t + 1) * n, :] = q_ref[t, rr]
                ks[t * n:(t + 1) * n, :] = k_ref[t, rr]
                vs[t * n:(t + 1) * n, :] = v_ref[t, rr]

    @pl.when((pl.program_id(0) == 0) & (pl.program_id(1) == 0))
    def _():
        base = (lax.broadcasted_iota(I32, (ATTN_QB, ATTN_KB), 1)
                - lax.broadcasted_iota(I32, (ATTN_QB, ATTN_KB), 0))
        for j in range(ATTN_OFFSETS):
            dist = jnp.abs(base - j * ATTN_HALF)
            for hh in range(HEADS_PER_GROUP):
                slope = dil * 2.0 ** (-ALIBI_MAX_EXP * (group * HEADS_PER_GROUP + hh + 1) / ATTN_HEADS)
                bias_scr[hh * ATTN_OFFSETS + j] = jnp.where(dist <= ATTN_HALF, -slope * dist.astype(F32), NEG)

    scale = HEAD_DIM ** -0.5
    lane = lax.broadcasted_iota(I32, (ATTN_QB, LANES), 1)

    def block(blk, carry):
        q0 = pl.multiple_of(blk * ATTN_QB, ATTN_QB)
        start = pl.multiple_of(jnp.clip(q0 - ATTN_HALF, 0, n_sub - ATTN_KB), ATTN_HALF)
        j = (q0 - start) // ATTN_HALF
        for qs, ks, vs, os_, ls in seqs:
            lse_tile = jnp.zeros((ATTN_QB, LANES), F32)
            for hh in range(HEADS_PER_GROUP):
                cs = slice(hh * HEAD_DIM, (hh + 1) * HEAD_DIM)
                qb = qs[pl.ds(q0, ATTN_QB), cs]
                kb = ks[pl.ds(start, ATTN_KB), cs]
                vb = vs[pl.ds(start, ATTN_KB), cs]
                s = lax.dot_general(qb, kb, (((1,), (1,)), ((), ())), preferred_element_type=F32) * scale
                s = s + bias_scr[hh * ATTN_OFFSETS + j]
                m = jnp.max(s, axis=-1, keepdims=True)
                p = jnp.exp(s - m)
                l = jnp.sum(p, axis=-1, keepdims=True)
                o = _dot(p.astype(BF16), vb) * (1.0 / l)
                os_[pl.ds(q0, ATTN_QB), cs] = o.astype(BF16)
                lse_tile = jnp.where(lane == hh, m + jnp.log(l), lse_tile)
            ls[pl.ds(q0, ATTN_QB), :] = lse_tile
        return carry

    n_blk = n_sub // ATTN_QB
    lax.fori_loop(0, n_blk, block, 0, unroll=min(max(ATTN_UNROLL // nres, 1), n_blk))

    if nt > 1:
        for rr, (_, _, _, os_, ls) in enumerate(seqs):
            for t in range(nt):
                o_ref[t, rr] = os_[t * n:(t + 1) * n, :]
                lse_ref[t, rr] = ls[t * n:(t + 1) * n, :]


def _attention(qkv, group, batch, seq):
    dil = ATTN_PATTERNS[group][1]
    if dil == 1:
        nt, n = 1, seq
    else:
        nt, n = seq // PERM_TILE, PERM_TILE // dil
    n_sub = nt * n
    t = batch * seq
    qkv5 = qkv.reshape(batch, nt, dil, n, qkv.shape[1])
    cb = group * 3
    nres = min(dil, max(1, ATTN_MIN_BLOCKS * ATTN_QB // n_sub))
    blk = (None, nt, nres, n, GROUP_WIDTH)
    scratch = [pltpu.VMEM((HEADS_PER_GROUP * ATTN_OFFSETS, ATTN_QB, ATTN_KB), F32)]
    if nt > 1:
        scratch += [pltpu.VMEM((nres, n_sub, GROUP_WIDTH), BF16)] * 4 + [pltpu.VMEM((nres, n_sub, LANES), F32)]
    out, lse = pl.pallas_call(
        functools.partial(_attn_kernel, group=group, dil=dil, nt=nt, n=n, nres=nres),
        grid=(batch, dil // nres),
        in_specs=[pl.BlockSpec(blk, lambda b, r: (b, 0, r, 0, cb)),
                  pl.BlockSpec(blk, lambda b, r: (b, 0, r, 0, cb + 1)),
                  pl.BlockSpec(blk, lambda b, r: (b, 0, r, 0, cb + 2))],
        out_specs=[pl.BlockSpec(blk, lambda b, r: (b, 0, r, 0, 0)),
                   pl.BlockSpec((None, nt, nres, n, LANES), lambda b, r: (b, 0, r, 0, 0))],
        out_shape=[jax.ShapeDtypeStruct((batch, nt, dil, n, GROUP_WIDTH), BF16),
                   jax.ShapeDtypeStruct((batch, nt, dil, n, LANES), F32)],
        scratch_shapes=scratch,
        compiler_params=_cparams(("arbitrary", "arbitrary")),
        name=f"attn_g{group}",
    )(qkv5, qkv5, qkv5)
    return out.reshape(t, GROUP_WIDTH), lse.reshape(t, LANES)


def _log_sigmoid(x):
    return jnp.minimum(x, 0.0) - jnp.log(1.0 + jnp.exp(-jnp.abs(x)))


RET_HEAD_COLS = 2 * RET_QK + RET_V


def _retention_kernel(decay_ref, qkv_ref, g_ref, gain_ref, o_ref, state, ybwd, dmat, kdec, qdec, *, nc):
    c = RET_CHUNK
    h = pl.program_id(1)
    i = pl.program_id(2)
    kscale = RET_QK ** -0.5

    def set_decays(lg, forward):
        row = lax.broadcasted_iota(I32, (c, c), 0)
        col = lax.broadcasted_iota(I32, (c, c), 1)
        pos = lax.broadcasted_iota(I32, (c, LANES), 0).astype(F32)
        if forward:
            gap, key_pow, query_pow = row - col, (c - 1.0) - pos, pos + 1.0
            keep = gap >= 0
        else:
            gap, key_pow, query_pow = col - row, pos, c - pos
            keep = gap > 0
        dmat[...] = jnp.where(keep, jnp.exp(lg * jnp.maximum(gap, 0).astype(F32)) * kscale, 0.0)
        kdec[...] = jnp.exp(lg * key_pow) * kscale
        qdec[...] = jnp.exp(lg * query_pow)

    def chunk_step(lg, r0):
        q = qkv_ref[r0:r0 + c, 0:RET_QK]
        k = qkv_ref[r0:r0 + c, RET_QK:2 * RET_QK]
        v = qkv_ref[r0:r0 + c, 2 * RET_QK:]
        inner = lax.dot_general(q, k, (((1,), (1,)), ((), ())), preferred_element_type=F32) * dmat[...]
        y = _dot(inner.astype(BF16), v)
        qd = jnp.concatenate([qdec[...]] * (RET_V // LANES), axis=1)
        y = y + _dot(q, state[...].astype(BF16)) * qd
        kd = (k.astype(F32) * jnp.concatenate([kdec[...]] * (RET_QK // LANES), axis=1)).astype(BF16)
        kv = lax.dot_general(kd, v, (((0,), (0,)), ((), ())), preferred_element_type=F32)
        state[...] = state[...] * jnp.exp(lg * float(c)) + kv
        return y

    @pl.when((i == 0) | (i == nc))
    def _():
        state[...] = jnp.zeros_like(state)

    @pl.when(i < nc)
    def _():
        lg = _log_sigmoid(jnp.zeros((1, 1), F32) + decay_ref[RET_HEADS + h])

        @pl.when(i == 0)
        def _():
            set_decays(lg, False)

        base = pl.multiple_of((nc - 1 - i) * RET_STEP, RET_STEP)
        for r0 in reversed(range(0, RET_STEP, c)):
            ybwd[pl.ds(base + r0, c), :] = chunk_step(lg, r0)

    @pl.when(i >= nc)
    def _():
        lg = _log_sigmoid(jnp.zeros((1, 1), F32) + decay_ref[h])

        @pl.when(i == nc)
        def _():
            set_decays(lg, True)

        base = pl.multiple_of((i - nc) * RET_STEP, RET_STEP)
        for r0 in range(0, RET_STEP, c):
            y = chunk_step(lg, r0) + ybwd[pl.ds(base + r0, c), :]
            g = g_ref[r0:r0 + c, :].astype(F32)
            o_ref[r0:r0 + c, :] = (g * _sigmoid(g) * (_ln(y) * gain_ref[...])).astype(BF16)


def _retention(rest, decays, gn_gain, batch, seq):
    c = RET_STEP
    nc = seq // c
    rest3 = rest.reshape(batch, seq, rest.shape[1])
    gate_blk = RET_HEADS * RET_HEAD_COLS // RET_V

    def chunk(i):
        return jnp.where(i < nc, nc - 1 - i, i - nc)

    grid_spec = pltpu.PrefetchScalarGridSpec(
        num_scalar_prefetch=1,
        grid=(batch, RET_HEADS, 2 * nc),
        in_specs=[pl.BlockSpec((None, c, RET_HEAD_COLS), lambda b, h, i, d: (b, chunk(i), h)),
                  pl.BlockSpec((None, c, RET_V), lambda b, h, i, d: (b, jnp.maximum(i - nc, 0), gate_blk + h)),
                  pl.BlockSpec((1, RET_V), lambda b, h, i, d: (0, h))],
        out_specs=pl.BlockSpec((None, c, RET_V), lambda b, h, i, d: (b, jnp.maximum(i - nc, 0), h)),
        scratch_shapes=[pltpu.VMEM((RET_QK, RET_V), F32), pltpu.VMEM((seq, RET_V), F32),
                        pltpu.VMEM((RET_CHUNK, RET_CHUNK), F32), pltpu.VMEM((RET_CHUNK, LANES), F32),
                        pltpu.VMEM((RET_CHUNK, LANES), F32)],
    )
    out = pl.pallas_call(
        functools.partial(_retention_kernel, nc=nc),
        grid_spec=grid_spec,
        out_shape=jax.ShapeDtypeStruct((batch, seq, RET_HEADS * RET_V), BF16),
        compiler_params=_cparams(("arbitrary", "arbitrary", "arbitrary")),
        name="retention",
    )(decays, rest3, rest3, gn_gain)
    return out.reshape(batch * seq, RET_HEADS * RET_V)


MERGE_TM = PERM_TILE
MERGE_SUB = MERGE_TM
ZERO_SPLIT = 4


def _merge_kernel(o0_ref, o1_ref, o2_ref, l0_ref, l1_ref, l2_ref, retg_ref, ga_ref, gr_ref, x_ref, mod_ref,
                  wa_ref, wr_ref, wo_ref, ln_ref, wrh_ref, wrl_ref, rb_ref,
                  h1_ref, u2_ref, lg_ref, zrows_ref, on_scr, ln_scr, zbuf, zsem):
    i = pl.program_id(0)

    @pl.when(i == 0)
    def _():
        zbuf[...] = jnp.zeros_like(zbuf)

    zrows = zbuf.shape[0]
    zero_copies = [pltpu.make_async_copy(zbuf, zrows_ref.at[pl.ds((i * ZERO_SPLIT + k) * zrows, zrows)], zsem)
                   for k in range(ZERO_SPLIT)]
    for cp in zero_copies:
        cp.start()

    for g in (1, 2):
        dil = ATTN_PATTERNS[g][1]
        n = MERGE_TM // dil
        o_ref, l_ref = ((o1_ref, l1_ref), (o2_ref, l2_ref))[g - 1]
        for res in range(dil):
            rows = o_ref[res * n:(res + 1) * n, :].astype(F32)
            for hh in range(HEADS_PER_GROUP):
                on_scr[g - 1, hh, pl.ds(res, n, stride=dil), :] = rows[:, hh * HEAD_DIM:(hh + 1) * HEAD_DIM]
            ln_scr[g - 1, pl.ds(res, n, stride=dil), :] = l_ref[res * n:(res + 1) * n, :]

    for r0 in range(0, MERGE_TM, MERGE_SUB):
        rs = slice(r0, r0 + MERGE_SUB)
        l0, l1, l2 = l0_ref[rs, :], ln_scr[0, rs, :], ln_scr[1, rs, :]
        lm = jnp.maximum(jnp.maximum(l0, l1), l2)
        e0, e1, e2 = jnp.exp(l0 - lm), jnp.exp(l1 - lm), jnp.exp(l2 - lm)
        inv = 1.0 / (e0 + e1 + e2)
        parts = []
        for hh in range(HEADS_PER_GROUP):
            sl = slice(hh * HEAD_DIM, (hh + 1) * HEAD_DIM)
            acc = (e0[:, hh:hh + 1] * o0_ref[rs, sl].astype(F32)
                   + e1[:, hh:hh + 1] * on_scr[0, hh, rs, :]
                   + e2[:, hh:hh + 1] * on_scr[1, hh, rs, :])
            parts.append((acc * inv[:, hh:hh + 1]).astype(BF16))
        attn = jnp.concatenate(parts, axis=1)

        branch_a = _dot(attn, wa_ref[...])
        branch_r = _dot(retg_ref[rs, :], wr_ref[...])
        merged = (_sigmoid(ga_ref[rs, :].astype(F32)) * branch_a
                  + _sigmoid(gr_ref[rs, :].astype(F32)) * branch_r)
        y = _dot(merged.astype(BF16), wo_ref[...])

        h1 = _ln(DEEPNORM_ALPHA * x_ref[rs, :] + mod_ref[2:3, :] * y) * ln_ref[0:1, :] + ln_ref[1:2, :]
        h1_ref[rs, :] = h1
        u2 = _ln(h1) * (1.0 + mod_ref[4:5, :]) + mod_ref[3:4, :]
        _store_token_tiles(u2_ref, u2, r0)
        uh, ul = _split_bf16(u2)
        lg_ref[rs, :] = (_dot(uh, wrh_ref[...]) + _dot(uh, wrl_ref[...]) + _dot(ul, wrh_ref[...])
                         + rb_ref[...])

    for cp in zero_copies:
        cp.wait()


def _merge(outs, lses, retg, rest, x2, mod3, wa, wr, wo, ln1, wr_hi, wr_lo, rbias, seq, n_rows):
    t = x2.shape[0]
    tm = MERGE_TM
    per_seq = seq // tm
    zrows, rem = divmod(n_rows * ROW_TILE, (t // tm) * ZERO_SPLIT)
    assert rem == 0 and zrows % 8 == 0
    row = lambda w: pl.BlockSpec((tm, w), lambda i: (i, 0))
    full = lambda a: pl.BlockSpec(a.shape, lambda i: (0,) * a.ndim)
    return pl.pallas_call(
        _merge_kernel,
        grid=(t // tm,),
        in_specs=[row(GROUP_WIDTH)] * 3 + [row(LANES)] * 3 + [
            row(RET_HEADS * RET_V),
            pl.BlockSpec((tm, D_MODEL), lambda i: (i, 6)),
            pl.BlockSpec((tm, D_MODEL), lambda i: (i, 7)),
            row(D_MODEL),
            pl.BlockSpec((None, 6, D_MODEL), lambda i: (i // per_seq, 0, 0)),
            full(wa), full(wr), full(wo), full(ln1), full(wr_hi), full(wr_lo), full(rbias)],
        out_specs=[row(D_MODEL), pl.BlockSpec((tm * ROW_TILE, LANES), lambda i: (i, 0)), row(LANES),
                   pl.BlockSpec(memory_space=pl.ANY)],
        out_shape=[jax.ShapeDtypeStruct((t, D_MODEL), F32),
                   jax.ShapeDtypeStruct((t * ROW_TILE, LANES), F32),
                   jax.ShapeDtypeStruct((t, LANES), F32),
                   jax.ShapeDtypeStruct((n_rows * ROW_TILE, LANES), F32)],
        scratch_shapes=[pltpu.VMEM((2, HEADS_PER_GROUP, tm, HEAD_DIM), F32), pltpu.VMEM((2, tm, LANES), F32),
                        pltpu.VMEM((zrows, LANES), F32), pltpu.SemaphoreType.DMA(())],
        compiler_params=_cparams(("arbitrary",)),
        name="merge",
    )(*outs, *lses, retg, rest, rest, x2, mod3, wa, wr, wo, ln1, wr_hi, wr_lo, rbias)


ROUTE_TM = 512
BIG = 1 << 20


def _route_kernel(lg_ref, cols_ref, ints_ref, cnt_ref, carry):
    i = pl.program_id(0)

    @pl.when(i == 0)
    def _():
        carry[...] = jnp.zeros_like(carry)

    tm = ROUTE_TM
    lg = lg_ref[...]
    lane = lax.broadcasted_iota(I32, (tm, LANES), 1)
    lane_f = lane.astype(F32)
    first = lambda mask: jnp.min(jnp.where(mask, lane_f, float(BIG)), axis=-1, keepdims=True).astype(I32)

    coarse = jnp.where(lane < N_GROUPS, lg, NEG)
    cmax = jnp.max(coarse, axis=-1, keepdims=True)
    gsel = first(coarse == cmax)
    p_group = 1.0 / jnp.sum(jnp.exp(coarse - cmax), axis=-1, keepdims=True)

    lo = N_GROUPS + EXPERTS_PER_GROUP * gsel
    fine = jnp.where((lane >= lo) & (lane < lo + EXPERTS_PER_GROUP), lg, NEG)
    v1 = jnp.max(fine, axis=-1, keepdims=True)
    i1 = first(fine == v1)
    fine2 = jnp.where(lane == i1, NEG, fine)
    v2 = jnp.max(fine2, axis=-1, keepdims=True)
    i2 = first(fine2 == v2)
    ex = jnp.exp(v2 - v1)
    den = 1.0 / (1.0 + ex)
    gate1 = p_group * den
    gate2 = p_group * (ex * den)
    e1 = i1 - N_GROUPS
    e2 = i2 - N_GROUPS

    oh1 = lane == e1
    oh2 = lane == e2
    cnt = jnp.where(oh1 | oh2, 1.0, 0.0)
    r_i = lax.broadcasted_iota(I32, (tm, tm), 0)
    c_i = lax.broadcasted_iota(I32, (tm, tm), 1)
    tri = jnp.where(r_i > c_i, 1.0, 0.0).astype(BF16)
    rank = _dot(tri, cnt.astype(BF16)) + carry[...]
    r1 = jnp.sum(jnp.where(oh1, rank, 0.0), axis=-1, keepdims=True)
    r2 = jnp.sum(jnp.where(oh2, rank, 0.0), axis=-1, keepdims=True)
    carry[...] = carry[...] + jnp.sum(cnt, axis=0, keepdims=True)
    cnt_ref[...] = jnp.broadcast_to(carry[...], cnt_ref.shape)

    cols_ref[...] = jnp.where(lane == 0, gate1, jnp.where(lane == 1, gate2, 0.0))
    packed = jnp.where(lane == 0, e1.astype(F32),
                       jnp.where(lane == 1, e2.astype(F32),
                                 jnp.where(lane == 2, r1, jnp.where(lane == 3, r2, 0.0))))
    ints_ref[...] = packed.T[0:8, :].astype(I32)


def _route(logits):
    t = logits.shape[0]
    tm = ROUTE_TM
    return pl.pallas_call(
        _route_kernel,
        grid=(t // tm,),
        in_specs=[pl.BlockSpec((tm, LANES), lambda i: (i, 0))],
        out_specs=[pl.BlockSpec((tm, LANES), lambda i: (i, 0)),
                   pl.BlockSpec((8, tm), lambda i: (0, i)),
                   pl.BlockSpec((8, LANES), lambda i: (0, 0))],
        out_shape=[jax.ShapeDtypeStruct((t, LANES), F32),
                   jax.ShapeDtypeStruct((8, t), I32),
                   jax.ShapeDtypeStruct((8, LANES), F32)],
        scratch_shapes=[pltpu.VMEM((1, LANES), F32)],
        compiler_params=_cparams(("arbitrary",)),
        name="route",
    )(logits)


def _plan_kernel(ints_ref, cnt_ref, dest_ref, meta_ref, *, n_blocks_pad):
    sub = lax.broadcasted_iota(I32, (LANES, LANES), 0)
    lane = lax.broadcasted_iota(I32, (LANES, LANES), 1)
    cnt = cnt_ref[0:1, :]
    nblk_row = jnp.floor((cnt + (MOE_BLK - 1.0)) * (1.0 / MOE_BLK))
    nblk_mat = jnp.broadcast_to(nblk_row, (LANES, LANES))
    start_col = jnp.sum(jnp.where(lane < sub, nblk_mat, 0.0), axis=-1, keepdims=True)
    nblk_col = jnp.sum(jnp.where(lane == sub, nblk_mat, 0.0), axis=-1, keepdims=True)
    end_col = start_col + nblk_col

    ints = ints_ref[...]
    base = jnp.zeros(ints.shape, F32)
    for e in range(N_EXPERTS):
        base = jnp.where(ints == e, start_col[e:e + 1, :] * float(MOE_BLK), base)
    dest = base[0:2, :].astype(I32) + ints[2:4, :]
    dest_ref[...] = jnp.concatenate([dest, jnp.zeros((6, ints.shape[1]), I32)], axis=0)

    blk = lax.broadcasted_iota(I32, (LANES, n_blocks_pad), 1).astype(F32)
    e_sub = lax.broadcasted_iota(I32, (LANES, n_blocks_pad), 0)
    done = jnp.where((e_sub < N_EXPERTS) & (end_col <= blk), 1.0, 0.0)
    bexp = jnp.minimum(jnp.sum(done, axis=0, keepdims=True), N_EXPERTS - 1.0)
    used = jnp.sum(nblk_row, axis=-1, keepdims=True)
    row = lax.broadcasted_iota(I32, (8, n_blocks_pad), 0)
    meta = jnp.where(row == 0, bexp, jnp.where(row == 1, used, 0.0))
    meta_ref[...] = meta.astype(I32)


def _plan(ints, counts, n_blocks_pad):
    t = ints.shape[1]
    return pl.pallas_call(
        functools.partial(_plan_kernel, n_blocks_pad=n_blocks_pad),
        out_shape=[jax.ShapeDtypeStruct((8, t), I32), jax.ShapeDtypeStruct((8, n_blocks_pad), I32)],
        compiler_params=pltpu.CompilerParams(vmem_limit_bytes=VMEM_LIMIT),
        name="plan",
    )(ints, counts)


DISPATCH_TM = 512
DMA_UNROLL = 8


def _row_tile(ref, r):
    return ref.at[pl.ds(pl.multiple_of(r * ROW_TILE, ROW_TILE), ROW_TILE)]


def _dispatch_kernel(d0_ref, d1_ref, u2_ref, rows_in_ref, rows_ref, sem):
    del rows_in_ref

    def issue(t, carry):
        src = _row_tile(u2_ref, t)
        pltpu.make_async_copy(src, _row_tile(rows_ref, d0_ref[t]), sem).start(priority=0)
        pltpu.make_async_copy(src, _row_tile(rows_ref, d1_ref[t]), sem).start(priority=1)
        return carry

    lax.fori_loop(0, DISPATCH_TM, issue, 0, unroll=DMA_UNROLL)
    for _ in range(2):
        pltpu.make_async_copy(u2_ref, rows_ref.at[pl.ds(0, DISPATCH_TM * ROW_TILE)], sem).wait()


def _dispatch(dest0, dest1, u2, rows0):
    t = u2.shape[0] // ROW_TILE
    idx = pl.BlockSpec((DISPATCH_TM,), lambda i: (i,), memory_space=pltpu.SMEM)
    return pl.pallas_call(
        _dispatch_kernel,
        grid=(t // DISPATCH_TM,),
        in_specs=[idx, idx,
                  pl.BlockSpec((DISPATCH_TM * ROW_TILE, LANES), lambda i: (i, 0)),
                  pl.BlockSpec(memory_space=pl.ANY)],
        out_specs=pl.BlockSpec(memory_space=pl.ANY),
        out_shape=jax.ShapeDtypeStruct(rows0.shape, F32),
        scratch_shapes=[pltpu.SemaphoreType.DMA(())],
        input_output_aliases={3: 0},
        compiler_params=_cparams(("arbitrary",)),
        name="dispatch",
    )(dest0, dest1, u2, rows0)


def _expert_runs(bexp, used):
    n = bexp.shape[0]
    idx = jnp.arange(n, dtype=I32)
    first = (idx < used[0]) & ((idx == 0) | (bexp != jnp.roll(bexp, 1)))
    slot = (jnp.cumsum(first.astype(I32)) - 1) % 2
    first_at_or_after = lax.cummin(jnp.where(first, idx, n)[::-1])[::-1]
    first_after = jnp.concatenate([first_at_or_after[1:], jnp.full((1,), n, I32)])
    nxt = jnp.where(first_after < n, bexp[jnp.minimum(first_after, n - 1)], -1)
    return first.astype(I32), slot.astype(I32), nxt.astype(I32)


def _experts_kernel(bexp_ref, used_ref, first_ref, slot_ref, nxt_ref, x_ref, w1_hbm, w3_hbm, w2_hbm, y_ref,
                    wb1, wb3, wb2, w1s, w3s, w2s, sem):
    i = pl.program_id(0)
    active = i < used_ref[0]
    slot = slot_ref[i]

    def fetch(e, s):
        return [pltpu.make_async_copy(w.at[e], buf.at[s], sem.at[s])
                for w, buf in ((w1_hbm, wb1), (w3_hbm, wb3), (w2_hbm, wb2))]

    @pl.when(i == 0)
    def _():
        for cp in fetch(bexp_ref[0], 0):
            cp.start()

    @pl.when(first_ref[i] == 1)
    def _():
        @pl.when(nxt_ref[i] >= 0)
        def _():
            for cp in fetch(nxt_ref[i], 1 - slot):
                cp.start()

        for cp in fetch(bexp_ref[i], slot):
            cp.wait()
        w1s[...] = wb1[slot].astype(BF16)
        w3s[...] = wb3[slot].astype(BF16)
        w2s[...] = wb2[slot].astype(BF16)

    @pl.when(active)
    def _():
        xb = _load_token_tiles(x_ref, MOE_BLK).astype(BF16)
        a = _dot(xb, w1s[...])
        b = _dot(xb, w3s[...])
        hdn = (a * _sigmoid(a) * b).astype(BF16)
        _store_token_tiles(y_ref, _dot(hdn, w2s[...]))

    @pl.when(jnp.logical_not(active))
    def _():
        y_ref[...] = jnp.zeros_like(y_ref)


def _experts(bexp, used, rows, w1, w3, w2):
    n_blocks = rows.shape[0] // (MOE_BLK * ROW_TILE)
    first, slot, nxt = _expert_runs(bexp, used)
    any_space = pl.BlockSpec(memory_space=pl.ANY)
    grid_spec = pltpu.PrefetchScalarGridSpec(
        num_scalar_prefetch=5,
        grid=(n_blocks,),
        in_specs=[pl.BlockSpec((MOE_BLK * ROW_TILE, LANES), lambda i, be, nu, *_: (jnp.minimum(i, nu[0] - 1), 0)),
                  any_space, any_space, any_space],
        out_specs=pl.BlockSpec((MOE_BLK * ROW_TILE, LANES), lambda i, *_: (i, 0)),
        scratch_shapes=[pltpu.VMEM((2, D_MODEL, EXPERT_FF), F32), pltpu.VMEM((2, D_MODEL, EXPERT_FF), F32),
                        pltpu.VMEM((2, EXPERT_FF, D_MODEL), F32),
                        pltpu.VMEM((D_MODEL, EXPERT_FF), BF16), pltpu.VMEM((D_MODEL, EXPERT_FF), BF16),
                        pltpu.VMEM((EXPERT_FF, D_MODEL), BF16), pltpu.SemaphoreType.DMA((2,))],
    )
    return pl.pallas_call(
        _experts_kernel,
        grid_spec=grid_spec,
        out_shape=jax.ShapeDtypeStruct(rows.shape, F32),
        compiler_params=_cparams(("arbitrary",)),
        name="experts",
    )(bexp, used, first, slot, nxt, rows, w1, w3, w2)


COMBINE_TM = 256


def _combine_kernel(d0_ref, d1_ref, n0_ref, n1_ref, y_ref, cols_ref, h1_ref, mod_ref, ln_ref, o_ref, ya, yb, sem):
    i = pl.program_id(0)
    slot = i % 2

    def gather(i0_ref, i1_ref, s):
        def issue(t, carry):
            pltpu.make_async_copy(_row_tile(y_ref, i0_ref[t]), _row_tile(ya.at[s], t), sem.at[s]).start(priority=0)
            pltpu.make_async_copy(_row_tile(y_ref, i1_ref[t]), _row_tile(yb.at[s], t), sem.at[s]).start(priority=1)
            return carry

        lax.fori_loop(0, COMBINE_TM, issue, 0, unroll=DMA_UNROLL)

    @pl.when(i == 0)
    def _():
        gather(d0_ref, d1_ref, 0)

    @pl.when(i + 1 < pl.num_programs(0))
    def _():
        gather(n0_ref, n1_ref, 1 - slot)

    for buf in (ya, yb):
        pltpu.make_async_copy(y_ref.at[pl.ds(0, COMBINE_TM * ROW_TILE)], buf.at[slot], sem.at[slot]).wait()

    cols = cols_ref[...]
    moe = (cols[:, 0:1] * _load_token_tiles(ya.at[slot], COMBINE_TM)
           + cols[:, 1:2] * _load_token_tiles(yb.at[slot], COMBINE_TM))
    pre = DEEPNORM_ALPHA * h1_ref[...] + mod_ref[5:6, :] * moe
    o_ref[...] = _ln(pre) * ln_ref[0:1, :] + ln_ref[1:2, :]


def _combine(dest0, dest1, y_rows, cols, h1, mod3, ln2, seq):
    t = h1.shape[0]
    tm = COMBINE_TM
    per_seq = seq // tm
    idx = pl.BlockSpec((tm,), lambda i: (i,), memory_space=pltpu.SMEM)
    idx_next = pl.BlockSpec((tm,), lambda i: (jnp.minimum(i + 1, t // tm - 1),), memory_space=pltpu.SMEM)
    return pl.pallas_call(
        _combine_kernel,
        grid=(t // tm,),
        in_specs=[idx, idx, idx_next, idx_next,
                  pl.BlockSpec(memory_space=pl.ANY),
                  pl.BlockSpec((tm, LANES), lambda i: (i, 0)),
                  pl.BlockSpec((tm, D_MODEL), lambda i: (i, 0)),
                  pl.BlockSpec((None, 6, D_MODEL), lambda i: (i // per_seq, 0, 0)),
                  pl.BlockSpec((2, D_MODEL), lambda i: (0, 0))],
        out_specs=pl.BlockSpec((tm, D_MODEL), lambda i: (i, 0)),
        out_shape=jax.ShapeDtypeStruct((t, D_MODEL), F32),
        scratch_shapes=[pltpu.VMEM((2, tm * ROW_TILE, LANES), F32), pltpu.VMEM((2, tm * ROW_TILE, LANES), F32),
                        pltpu.SemaphoreType.DMA((2,))],
        compiler_params=_cparams(("arbitrary",)),
        name="combine",
    )(dest0, dest1, dest0, dest1, y_rows, cols, h1, mod3, ln2)


def _layer(h, c8, w_ada, b_ada, w_in, w_attn_out, decay_f, decay_b, gn_gain, w_ret_out, w_out,
           ln1_gain, ln1_bias, w_coarse, b_coarse, w_fine, b_fine, w1, w3, w2, ln2_gain, ln2_bias):
    batch, seq, d = h.shape
    t = batch * seq
    x2 = h.reshape(t, d)

    mod = _ada(c8, w_ada, b_ada.reshape(1, -1))
    mod3 = mod[:batch].reshape(batch, 6, d)

    aw = ATTN_HEADS * HEAD_DIM
    wq, wk, wv = (w_in[:, s * aw:(s + 1) * aw] for s in range(3))
    gcols = lambda w, g: w[:, g * GROUP_WIDTH:(g + 1) * GROUP_WIDTH]
    w_qkv = jnp.concatenate([gcols(w, g) for g in range(ATTN_GROUPS) for w in (wq, wk, wv)], axis=1).astype(BF16)
    rq0, rk0, rv0, tail0 = 3 * aw, 3 * aw + RET_HEADS * RET_QK, 3 * aw + 2 * RET_HEADS * RET_QK, \
        3 * aw + 2 * RET_HEADS * RET_QK + RET_HEADS * RET_V
    head_cols = [w_in[:, o + hh * wd:o + (hh + 1) * wd]
                 for hh in range(RET_HEADS) for o, wd in ((rq0, RET_QK), (rk0, RET_QK), (rv0, RET_V))]
    w_rest = jnp.concatenate(head_cols + [w_in[:, tail0:]], axis=1).astype(BF16)

    qkv = _inproj_attn(x2, mod3, w_qkv, seq)
    rest = _inproj_rest(x2, mod3, w_rest, seq)

    outs, lses = zip(*[_attention(qkv, g, batch, seq) for g in range(ATTN_GROUPS)])
    decays = jnp.concatenate([decay_f, decay_b]).astype(F32)
    retg = _retention(rest, decays, gn_gain.reshape(1, -1), batch, seq)

    w_route = jnp.concatenate([w_coarse] + [w_fine[g] for g in range(N_GROUPS)], axis=1)
    n_route = w_route.shape[1]
    w_route = jnp.pad(w_route, ((0, 0), (0, LANES - n_route)))
    wr_hi, wr_lo = _split_bf16(w_route)
    rbias = jnp.pad(jnp.concatenate([b_coarse, b_fine.reshape(-1)]), (0, LANES - n_route)).reshape(1, LANES)
    ln1 = jnp.stack([ln1_gain, ln1_bias])
    n_blocks = 2 * t // MOE_BLK + N_EXPERTS
    h1, u2, logits, rows0 = _merge(outs, lses, retg, rest, x2, mod3, w_attn_out.astype(BF16),
                                   w_ret_out.astype(BF16), w_out.astype(BF16), ln1, wr_hi, wr_lo, rbias, seq,
                                   n_blocks * MOE_BLK)

    cols, ints, counts = _route(logits)
    n_blocks_pad = -(-n_blocks // LANES) * LANES
    dest, meta = _plan(ints, counts, n_blocks_pad)
    dest0, dest1 = dest[0], dest[1]
    rows = _dispatch(dest0, dest1, u2, rows0)
    y_rows = _experts(meta[0, :n_blocks], meta[1, :1], rows, w1, w3, w2)
    out = _combine(dest0, dest1, y_rows, cols, h1, mod3, jnp.stack([ln2_gain, ln2_bias]), seq)
    return out.reshape(batch, seq, d)


def kernel(x, c, w_ada, b_ada, w_in, w_attn_out, ret_decay_fwd, ret_decay_bwd, ret_gn_gain, w_ret_out, w_out,
           ln1_gain, ln1_bias, w_coarse, b_coarse, w_fine, b_fine, w1, w3, w2, ln2_gain, ln2_bias):
    batch = x.shape[0]
    assert batch <= 8 and x.shape[1] % (2 * INPROJ_TM) == 0 and x.shape[2] == D_MODEL
    c8 = jnp.pad(c, ((0, 8 - batch), (0, 0)))
    h = x
    for l in range(w_ada.shape[0]):
        h = _layer(h, c8, w_ada[l], b_ada[l], w_in[l], w_attn_out[l], ret_decay_fwd[l], ret_decay_bwd[l],
                   ret_gn_gain[l], w_ret_out[l], w_out[l], ln1_gain[l], ln1_bias[l], w_coarse[l], b_coarse[l],
                   w_fine[l], b_fine[l], w1[l], w3[l], w2[l], ln2_gain[l], ln2_bias[l])
    return h
```

```python
import functools
import math

import jax
import jax.numpy as jnp
from jax import lax
from jax.experimental import pallas as pl
from jax.experimental.pallas import tpu as pltpu

F32 = jnp.float32
BF16 = jnp.bfloat16
I32 = jnp.int32

D_MODEL = 1024
ATTN_GROUPS = 3
HEADS_PER_GROUP = 4
HEAD_DIM = 128
ATTN_HEADS = ATTN_GROUPS * HEADS_PER_GROUP
GROUP_WIDTH = HEADS_PER_GROUP * HEAD_DIM
ATTN_PATTERNS = ((128, 1), (512, 4), (2048, 16))
ALIBI_MAX_EXP = 8.0
NEG = -1e30
RET_HEADS = 4
RET_QK = 256
RET_V = 512
N_GROUPS = 4
EXPERTS_PER_GROUP = 8
N_EXPERTS = N_GROUPS * EXPERTS_PER_GROUP
EXPERT_FF = 512
DEPTH = 1
DEEPNORM_ALPHA = (2.0 * DEPTH) ** 0.25
LN_EPS = 1e-5

LANES = 128
PERM_TILE = 512
ATTN_QB = 128
ATTN_HALF = 64
ATTN_KB = ATTN_QB + 2 * ATTN_HALF
RET_CHUNK = 256
RET_STEP = 1024
MOE_BLK = 512
MOE_SUB = 256
VMEM_LIMIT = 56 * 1024 * 1024


def _cparams(sem):
    return pltpu.CompilerParams(dimension_semantics=sem, vmem_limit_bytes=VMEM_LIMIT)


def _split_bf16(a):
    hi = a.astype(BF16)
    lo = (a - hi.astype(F32)).astype(BF16)
    return hi, lo


def _dot(a, b):
    return jnp.dot(a, b, preferred_element_type=F32)


def _dot3(a, b):
    ah, al = _split_bf16(a)
    bh, bl = _split_bf16(b)
    return _dot(ah, bh) + _dot(ah, bl) + _dot(al, bh)


def _ln(x):
    mu = jnp.mean(x, axis=-1, keepdims=True)
    xc = x - mu
    var = jnp.mean(xc * xc, axis=-1, keepdims=True)
    return xc * lax.rsqrt(var + LN_EPS)


def _sigmoid(x):
    return 1.0 / (1.0 + jnp.exp(-x))


ROW_TILE = D_MODEL // LANES


def _store_token_tiles(ref, val, row0=0):
    n = val.shape[0]
    for cc in range(ROW_TILE):
        ref[pl.ds(row0 * ROW_TILE + cc, n, stride=ROW_TILE), :] = val[:, cc * LANES:(cc + 1) * LANES]


def _load_token_tiles(ref, n, row0=0):
    return jnp.concatenate([ref[pl.ds(row0 * ROW_TILE + cc, n, stride=ROW_TILE), :] for cc in range(ROW_TILE)],
                           axis=1)


def _interleave(phased):
    pending = []
    for gen in phased:
        pending.append(gen)
        pending = [g for g in pending if next(g, StopIteration) is not StopIteration]
    while pending:
        pending = [g for g in pending if next(g, StopIteration) is not StopIteration]


def _ada_kernel(c_ref, w_ref, b_ref, o_ref):
    o_ref[...] = _dot3(c_ref[...], w_ref[...]) + b_ref[...]


def _ada(c8, w_ada, b_ada):
    n = w_ada.shape[1]
    return pl.pallas_call(
        _ada_kernel,
        grid=(n // D_MODEL,),
        in_specs=[pl.BlockSpec((8, D_MODEL), lambda j: (0, 0)),
                  pl.BlockSpec((D_MODEL, D_MODEL), lambda j: (0, j)),
                  pl.BlockSpec((1, D_MODEL), lambda j: (0, j))],
        out_specs=pl.BlockSpec((8, D_MODEL), lambda j: (0, j)),
        out_shape=jax.ShapeDtypeStruct((8, n), F32),
        compiler_params=_cparams(("arbitrary",)),
        name="ada",
    )(c8, w_ada, b_ada)


INPROJ_TM = 1024
LN_ROWS = 256
LN_UNROLL = 4


def _modulated_ln_rows(x_ref, mod_ref, emit):
    gain = 1.0 + mod_ref[1:2, :]
    shift = mod_ref[0:1, :]

    def chunk(c, carry):
        r0 = pl.multiple_of(c * LN_ROWS, LN_ROWS)
        emit(r0, _ln(x_ref[pl.ds(r0, LN_ROWS), :]) * gain + shift)
        return carry

    lax.fori_loop(0, INPROJ_TM // LN_ROWS, chunk, 0, unroll=LN_UNROLL)


def _inproj_attn_kernel(x_ref, mod_ref, w_ref, o_ref, uf_scr, u_scr):
    j = pl.program_id(1)

    @pl.when(j == 0)
    def _():
        def emit(r0, u):
            for cc in range(D_MODEL // LANES):
                uf_scr[cc, pl.ds(r0, LN_ROWS), :] = u[:, cc * LANES:(cc + 1) * LANES]
            u_scr[0, pl.ds(r0, LN_ROWS), :] = u.astype(BF16)

        _modulated_ln_rows(x_ref, mod_ref, emit)
        for g in (1, 2):
            dil = ATTN_PATTERNS[g][1]
            n = PERM_TILE // dil
            for t0 in range(0, INPROJ_TM, PERM_TILE):
                for res in range(dil):
                    for cc in range(D_MODEL // LANES):
                        rows = uf_scr[cc, pl.ds(t0 + res, n, stride=dil), :]
                        u_scr[g, t0 + res * n:t0 + (res + 1) * n, cc * LANES:(cc + 1) * LANES] = rows.astype(BF16)

    o_ref[...] = _dot(u_scr[j], w_ref[...]).astype(BF16)


def _inproj_attn(x2, mod3, w_qkv, seq):
    t = x2.shape[0]
    tn = 3 * GROUP_WIDTH
    per_seq = seq // INPROJ_TM
    return pl.pallas_call(
        _inproj_attn_kernel,
        grid=(t // INPROJ_TM, ATTN_GROUPS),
        in_specs=[pl.BlockSpec((INPROJ_TM, D_MODEL), lambda i, j: (i, 0)),
                  pl.BlockSpec((None, 6, D_MODEL), lambda i, j: (i // per_seq, 0, 0)),
                  pl.BlockSpec((D_MODEL, tn), lambda i, j: (0, j))],
        out_specs=pl.BlockSpec((INPROJ_TM, tn), lambda i, j: (i, j)),
        out_shape=jax.ShapeDtypeStruct((t, ATTN_GROUPS * tn), BF16),
        scratch_shapes=[pltpu.VMEM((D_MODEL // LANES, INPROJ_TM, LANES), F32),
                        pltpu.VMEM((ATTN_GROUPS, INPROJ_TM, D_MODEL), BF16)],
        compiler_params=_cparams(("arbitrary", "arbitrary")),
        name="inproj_attn",
    )(x2, mod3, w_qkv)


def _inproj_rest_kernel(x_ref, mod_ref, w_ref, o_ref, u_scr):
    @pl.when(pl.program_id(1) == 0)
    def _():
        def emit(r0, u):
            u_scr[pl.ds(r0, LN_ROWS), :] = u.astype(BF16)

        _modulated_ln_rows(x_ref, mod_ref, emit)

    o_ref[...] = _dot(u_scr[...], w_ref[...]).astype(BF16)


def _inproj_rest(x2, mod3, w_rest, seq):
    t = x2.shape[0]
    n = w_rest.shape[1]
    tn = 1024
    per_seq = seq // INPROJ_TM
    return pl.pallas_call(
        _inproj_rest_kernel,
        grid=(t // INPROJ_TM, n // tn),
        in_specs=[pl.BlockSpec((INPROJ_TM, D_MODEL), lambda i, j: (i, 0)),
                  pl.BlockSpec((None, 6, D_MODEL), lambda i, j: (i // per_seq, 0, 0)),
                  pl.BlockSpec((D_MODEL, tn), lambda i, j: (0, j))],
        out_specs=pl.BlockSpec((INPROJ_TM, tn), lambda i, j: (i, j)),
        out_shape=jax.ShapeDtypeStruct((t, n), BF16),
        scratch_shapes=[pltpu.VMEM((INPROJ_TM, D_MODEL), BF16)],
        compiler_params=_cparams(("arbitrary", "arbitrary")),
        name="inproj_rest",
    )(x2, mod3, w_rest)


ATTN_OFFSETS = ATTN_KB // ATTN_HALF - 1
ATTN_UNROLL = 4
ATTN_MIN_BLOCKS = 4


def _attn_kernel(q_ref, k_ref, v_ref, o_ref, lse_ref, bias_scr, *scratch, group, dil, nt, n, nres):
    n_sub = nt * n
    if nt == 1:
        seqs = [(q_ref.at[0, rr], k_ref.at[0, rr], v_ref.at[0, rr], o_ref.at[0, rr], lse_ref.at[0, rr])
                for rr in range(nres)]
    else:
        seqs = [tuple(s.at[rr] for s in scratch) for rr in range(nres)]
        for rr, (qs, ks, vs, _, _) in enumerate(seqs):
            for t in range(nt):
                qs[t * n:(t + 1) * n, :] = q_ref[t, rr]
                ks[t * n:(t + 1) * n, :] = k_ref[t, rr]
                vs[t * n:(t + 1) * n, :] = v_ref[t, rr]

    @pl.when((pl.program_id(0) == 0) & (pl.program_id(1) == 0))
    def _():
        base = (lax.broadcasted_iota(I32, (ATTN_QB, ATTN_KB), 1)
                - lax.broadcasted_iota(I32, (ATTN_QB, ATTN_KB), 0))
        for j in range(ATTN_OFFSETS):
            dist = jnp.abs(base - j * ATTN_HALF)
            for hh in range(HEADS_PER_GROUP):
                slope = dil * 2.0 ** (-ALIBI_MAX_EXP * (group * HEADS_PER_GROUP + hh + 1) / ATTN_HEADS)
                bias_scr[hh * ATTN_OFFSETS + j] = jnp.where(dist <= ATTN_HALF, -slope * dist.astype(F32), NEG)

    scale = HEAD_DIM ** -0.5
    lane = lax.broadcasted_iota(I32, (ATTN_QB, LANES), 1)

    def block(blk, carry):
        q0 = pl.multiple_of(blk * ATTN_QB, ATTN_QB)
        start = pl.multiple_of(jnp.clip(q0 - ATTN_HALF, 0, n_sub - ATTN_KB), ATTN_HALF)
        j = (q0 - start) // ATTN_HALF
        for qs, ks, vs, os_, ls in seqs:
            lse_tile = jnp.zeros((ATTN_QB, LANES), F32)
            for hh in range(HEADS_PER_GROUP):
                cs = slice(hh * HEAD_DIM, (hh + 1) * HEAD_DIM)
                qb = qs[pl.ds(q0, ATTN_QB), cs]
                kb = ks[pl.ds(start, ATTN_KB), cs]
                vb = vs[pl.ds(start, ATTN_KB), cs]
                s = lax.dot_general(qb, kb, (((1,), (1,)), ((), ())), preferred_element_type=F32) * scale
                s = s + bias_scr[hh * ATTN_OFFSETS + j]
                m = jnp.max(s, axis=-1, keepdims=True)
                p = jnp.exp(s - m)
                l = jnp.sum(p, axis=-1, keepdims=True)
                o = _dot(p.astype(BF16), vb) * (1.0 / l)
                os_[pl.ds(q0, ATTN_QB), cs] = o.astype(BF16)
                lse_tile = jnp.where(lane == hh, m + jnp.log(l), lse_tile)
            ls[pl.ds(q0, ATTN_QB), :] = lse_tile
        return carry

    n_blk = n_sub // ATTN_QB
    lax.fori_loop(0, n_blk, block, 0, unroll=min(max(ATTN_UNROLL // nres, 1), n_blk))

    if nt > 1:
        for rr, (_, _, _, os_, ls) in enumerate(seqs):
            for t in range(nt):
                o_ref[t, rr] = os_[t * n:(t + 1) * n, :]
                lse_ref[t, rr] = ls[t * n:(t + 1) * n, :]


def _attention(qkv, group, batch, seq):
    dil = ATTN_PATTERNS[group][1]
    if dil == 1:
        nt, n = 1, seq
    else:
        nt, n = seq // PERM_TILE, PERM_TILE // dil
    n_sub = nt * n
    t = batch * seq
    qkv5 = qkv.reshape(batch, nt, dil, n, qkv.shape[1])
    cb = group * 3
    nres = min(dil, max(1, ATTN_MIN_BLOCKS * ATTN_QB // n_sub))
    blk = (None, nt, nres, n, GROUP_WIDTH)
    scratch = [pltpu.VMEM((HEADS_PER_GROUP * ATTN_OFFSETS, ATTN_QB, ATTN_KB), F32)]
    if nt > 1:
        scratch += [pltpu.VMEM((nres, n_sub, GROUP_WIDTH), BF16)] * 4 + [pltpu.VMEM((nres, n_sub, LANES), F32)]
    out, lse = pl.pallas_call(
        functools.partial(_attn_kernel, group=group, dil=dil, nt=nt, n=n, nres=nres),
        grid=(batch, dil // nres),
        in_specs=[pl.BlockSpec(blk, lambda b, r: (b, 0, r, 0, cb)),
                  pl.BlockSpec(blk, lambda b, r: (b, 0, r, 0, cb + 1)),
                  pl.BlockSpec(blk, lambda b, r: (b, 0, r, 0, cb + 2))],
        out_specs=[pl.BlockSpec(blk, lambda b, r: (b, 0, r, 0, 0)),
                   pl.BlockSpec((None, nt, nres, n, LANES), lambda b, r: (b, 0, r, 0, 0))],
        out_shape=[jax.ShapeDtypeStruct((batch, nt, dil, n, GROUP_WIDTH), BF16),
                   jax.ShapeDtypeStruct((batch, nt, dil, n, LANES), F32)],
        scratch_shapes=scratch,
        compiler_params=_cparams(("arbitrary", "arbitrary")),
        name=f"attn_g{group}",
    )(qkv5, qkv5, qkv5)
    return out.reshape(t, GROUP_WIDTH), lse.reshape(t, LANES)


def _log_sigmoid(x):
    return jnp.minimum(x, 0.0) - jnp.log(1.0 + jnp.exp(-jnp.abs(x)))


RET_HEAD_COLS = 2 * RET_QK + RET_V


def _retention_kernel(decay_ref, qkv_ref, g_ref, gain_ref, o_ref, state, ybwd, dmat, kdec, qdec, *, nc):
    c = RET_CHUNK
    h = pl.program_id(1)
    i = pl.program_id(2)
    kscale = RET_QK ** -0.5

    def set_decays(lg, forward):
        row = lax.broadcasted_iota(I32, (c, c), 0)
        col = lax.broadcasted_iota(I32, (c, c), 1)
        pos = lax.broadcasted_iota(I32, (c, LANES), 0).astype(F32)
        if forward:
            gap, key_pow, query_pow = row - col, (c - 1.0) - pos, pos + 1.0
            keep = gap >= 0
        else:
            gap, key_pow, query_pow = col - row, pos, c - pos
            keep = gap > 0
        dmat[...] = jnp.where(keep, jnp.exp(lg * jnp.maximum(gap, 0).astype(F32)) * kscale, 0.0)
        kdec[...] = jnp.exp(lg * key_pow) * kscale
        qdec[...] = jnp.exp(lg * query_pow)

    def chunk_step(lg, r0, finish):
        q = qkv_ref[r0:r0 + c, 0:RET_QK]
        k = qkv_ref[r0:r0 + c, RET_QK:2 * RET_QK]
        v = qkv_ref[r0:r0 + c, 2 * RET_QK:]
        inner = lax.dot_general(q, k, (((1,), (1,)), ((), ())), preferred_element_type=F32) * dmat[...]
        y = _dot(inner.astype(BF16), v)
        kd = (k.astype(F32) * jnp.concatenate([kdec[...]] * (RET_QK // LANES), axis=1)).astype(BF16)
        kv = lax.dot_general(kd, v, (((0,), (0,)), ((), ())), preferred_element_type=F32)
        yield
        qd = jnp.concatenate([qdec[...]] * (RET_V // LANES), axis=1)
        y = y + _dot(q, state[...].astype(BF16)) * qd
        state[...] = state[...] * jnp.exp(lg * float(c)) + kv
        yield
        finish(r0, y)

    @pl.when((i == 0) | (i == nc))
    def _():
        state[...] = jnp.zeros_like(state)

    @pl.when(i < nc)
    def _():
        lg = _log_sigmoid(jnp.zeros((1, 1), F32) + decay_ref[RET_HEADS + h])

        @pl.when(i == 0)
        def _():
            set_decays(lg, False)

        base = pl.multiple_of((nc - 1 - i) * RET_STEP, RET_STEP)

        def finish(r0, y):
            ybwd[pl.ds(base + r0, c), :] = y

        _interleave(chunk_step(lg, r0, finish) for r0 in reversed(range(0, RET_STEP, c)))

    @pl.when(i >= nc)
    def _():
        lg = _log_sigmoid(jnp.zeros((1, 1), F32) + decay_ref[h])

        @pl.when(i == nc)
        def _():
            set_decays(lg, True)

        base = pl.multiple_of((i - nc) * RET_STEP, RET_STEP)

        def finish(r0, y):
            y = y + ybwd[pl.ds(base + r0, c), :]
            g = g_ref[r0:r0 + c, :].astype(F32)
            o_ref[r0:r0 + c, :] = (g * _sigmoid(g) * (_ln(y) * gain_ref[...])).astype(BF16)

        _interleave(chunk_step(lg, r0, finish) for r0 in range(0, RET_STEP, c))


def _retention(rest, decays, gn_gain, batch, seq):
    c = RET_STEP
    nc = seq // c
    rest3 = rest.reshape(batch, seq, rest.shape[1])
    gate_blk = RET_HEADS * RET_HEAD_COLS // RET_V

    def chunk(i):
        return jnp.where(i < nc, nc - 1 - i, i - nc)

    grid_spec = pltpu.PrefetchScalarGridSpec(
        num_scalar_prefetch=1,
        grid=(batch, RET_HEADS, 2 * nc),
        in_specs=[pl.BlockSpec((None, c, RET_HEAD_COLS), lambda b, h, i, d: (b, chunk(i), h)),
                  pl.BlockSpec((None, c, RET_V), lambda b, h, i, d: (b, jnp.maximum(i - nc, 0), gate_blk + h)),
                  pl.BlockSpec((1, RET_V), lambda b, h, i, d: (0, h))],
        out_specs=pl.BlockSpec((None, c, RET_V), lambda b, h, i, d: (b, jnp.maximum(i - nc, 0), h)),
        scratch_shapes=[pltpu.VMEM((RET_QK, RET_V), F32), pltpu.VMEM((seq, RET_V), F32),
                        pltpu.VMEM((RET_CHUNK, RET_CHUNK), F32), pltpu.VMEM((RET_CHUNK, LANES), F32),
                        pltpu.VMEM((RET_CHUNK, LANES), F32)],
    )
    out = pl.pallas_call(
        functools.partial(_retention_kernel, nc=nc),
        grid_spec=grid_spec,
        out_shape=jax.ShapeDtypeStruct((batch, seq, RET_HEADS * RET_V), BF16),
        compiler_params=_cparams(("arbitrary", "arbitrary", "arbitrary")),
        name="retention",
    )(decays, rest3, rest3, gn_gain)
    return out.reshape(batch * seq, RET_HEADS * RET_V)


MERGE_TM = PERM_TILE
MERGE_SUB = 256
ZERO_SPLIT = 4


def _merge_kernel(o0_ref, o1_ref, o2_ref, l0_ref, l1_ref, l2_ref, retg_ref, ga_ref, gr_ref, x_ref, mod_ref,
                  wa_ref, wr_ref, wo_ref, ln_ref, wrh_ref, wrl_ref, rb_ref,
                  h1_ref, u2_ref, lg_ref, zrows_ref, on_scr, ln_scr, zbuf, zsem):
    i = pl.program_id(0)

    @pl.when(i == 0)
    def _():
        zbuf[...] = jnp.zeros_like(zbuf)

    zrows = zbuf.shape[0]
    zero_copies = [pltpu.make_async_copy(zbuf, zrows_ref.at[pl.ds((i * ZERO_SPLIT + k) * zrows, zrows)], zsem)
                   for k in range(ZERO_SPLIT)]
    for cp in zero_copies:
        cp.start()

    for g in (1, 2):
        dil = ATTN_PATTERNS[g][1]
        n = MERGE_TM // dil
        o_ref, l_ref = ((o1_ref, l1_ref), (o2_ref, l2_ref))[g - 1]
        for res in range(dil):
            rows = o_ref[res * n:(res + 1) * n, :].astype(F32)
            for hh in range(HEADS_PER_GROUP):
                on_scr[g - 1, hh, pl.ds(res, n, stride=dil), :] = rows[:, hh * HEAD_DIM:(hh + 1) * HEAD_DIM]
            ln_scr[g - 1, pl.ds(res, n, stride=dil), :] = l_ref[res * n:(res + 1) * n, :]

    def sub_tile(r0):
        rs = slice(r0, r0 + MERGE_SUB)
        l0, l1, l2 = l0_ref[rs, :], ln_scr[0, rs, :], ln_scr[1, rs, :]
        lm = jnp.maximum(jnp.maximum(l0, l1), l2)
        e0, e1, e2 = jnp.exp(l0 - lm), jnp.exp(l1 - lm), jnp.exp(l2 - lm)
        inv = 1.0 / (e0 + e1 + e2)
        parts = []
        for hh in range(HEADS_PER_GROUP):
            sl = slice(hh * HEAD_DIM, (hh + 1) * HEAD_DIM)
            acc = (e0[:, hh:hh + 1] * o0_ref[rs, sl].astype(F32)
                   + e1[:, hh:hh + 1] * on_scr[0, hh, rs, :]
                   + e2[:, hh:hh + 1] * on_scr[1, hh, rs, :])
            parts.append((acc * inv[:, hh:hh + 1]).astype(BF16))
        attn = jnp.concatenate(parts, axis=1)
        yield

        branch_a = _dot(attn, wa_ref[...])
        branch_r = _dot(retg_ref[rs, :], wr_ref[...])
        yield
        merged = (_sigmoid(ga_ref[rs, :].astype(F32)) * branch_a
                  + _sigmoid(gr_ref[rs, :].astype(F32)) * branch_r)
        yield
        y = _dot(merged.astype(BF16), wo_ref[...])
        yield

        h1 = _ln(DEEPNORM_ALPHA * x_ref[rs, :] + mod_ref[2:3, :] * y) * ln_ref[0:1, :] + ln_ref[1:2, :]
        h1_ref[rs, :] = h1
        u2 = _ln(h1) * (1.0 + mod_ref[4:5, :]) + mod_ref[3:4, :]
        _store_token_tiles(u2_ref, u2, r0)
        uh, ul = _split_bf16(u2)
        yield
        lg_ref[rs, :] = (_dot(uh, wrh_ref[...]) + _dot(uh, wrl_ref[...]) + _dot(ul, wrh_ref[...])
                         + rb_ref[...])

    _interleave(sub_tile(r0) for r0 in range(0, MERGE_TM, MERGE_SUB))

    for cp in zero_copies:
        cp.wait()


def _merge(outs, lses, retg, rest, x2, mod3, wa, wr, wo, ln1, wr_hi, wr_lo, rbias, seq, n_rows):
    t = x2.shape[0]
    tm = MERGE_TM
    per_seq = seq // tm
    zrows, rem = divmod(n_rows * ROW_TILE, (t // tm) * ZERO_SPLIT)
    assert rem == 0 and zrows % 8 == 0
    row = lambda w: pl.BlockSpec((tm, w), lambda i: (i, 0))
    full = lambda a: pl.BlockSpec(a.shape, lambda i: (0,) * a.ndim)
    return pl.pallas_call(
        _merge_kernel,
        grid=(t // tm,),
        in_specs=[row(GROUP_WIDTH)] * 3 + [row(LANES)] * 3 + [
            row(RET_HEADS * RET_V),
            pl.BlockSpec((tm, D_MODEL), lambda i: (i, 6)),
            pl.BlockSpec((tm, D_MODEL), lambda i: (i, 7)),
            row(D_MODEL),
            pl.BlockSpec((None, 6, D_MODEL), lambda i: (i // per_seq, 0, 0)),
            full(wa), full(wr), full(wo), full(ln1), full(wr_hi), full(wr_lo), full(rbias)],
        out_specs=[row(D_MODEL), pl.BlockSpec((tm * ROW_TILE, LANES), lambda i: (i, 0)), row(LANES),
                   pl.BlockSpec(memory_space=pl.ANY)],
        out_shape=[jax.ShapeDtypeStruct((t, D_MODEL), F32),
                   jax.ShapeDtypeStruct((t * ROW_TILE, LANES), F32),
                   jax.ShapeDtypeStruct((t, LANES), F32),
                   jax.ShapeDtypeStruct((n_rows * ROW_TILE, LANES), F32)],
        scratch_shapes=[pltpu.VMEM((2, HEADS_PER_GROUP, tm, HEAD_DIM), F32), pltpu.VMEM((2, tm, LANES), F32),
                        pltpu.VMEM((zrows, LANES), F32), pltpu.SemaphoreType.DMA(())],
        compiler_params=_cparams(("arbitrary",)),
        name="merge",
    )(*outs, *lses, retg, rest, rest, x2, mod3, wa, wr, wo, ln1, wr_hi, wr_lo, rbias)


ROUTE_TM = 512
BIG = 1 << 20


def _route_kernel(lg_ref, cols_ref, ints_ref, cnt_ref, carry):
    i = pl.program_id(0)

    @pl.when(i == 0)
    def _():
        carry[...] = jnp.zeros_like(carry)

    tm = ROUTE_TM
    lg = lg_ref[...]
    lane = lax.broadcasted_iota(I32, (tm, LANES), 1)
    lane_f = lane.astype(F32)
    first = lambda mask: jnp.min(jnp.where(mask, lane_f, float(BIG)), axis=-1, keepdims=True).astype(I32)

    coarse = jnp.where(lane < N_GROUPS, lg, NEG)
    cmax = jnp.max(coarse, axis=-1, keepdims=True)
    gsel = first(coarse == cmax)
    p_group = 1.0 / jnp.sum(jnp.exp(coarse - cmax), axis=-1, keepdims=True)

    lo = N_GROUPS + EXPERTS_PER_GROUP * gsel
    fine = jnp.where((lane >= lo) & (lane < lo + EXPERTS_PER_GROUP), lg, NEG)
    v1 = jnp.max(fine, axis=-1, keepdims=True)
    i1 = first(fine == v1)
    fine2 = jnp.where(lane == i1, NEG, fine)
    v2 = jnp.max(fine2, axis=-1, keepdims=True)
    i2 = first(fine2 == v2)
    ex = jnp.exp(v2 - v1)
    den = 1.0 / (1.0 + ex)
    gate1 = p_group * den
    gate2 = p_group * (ex * den)
    e1 = i1 - N_GROUPS
    e2 = i2 - N_GROUPS

    oh1 = lane == e1
    oh2 = lane == e2
    cnt = jnp.where(oh1 | oh2, 1.0, 0.0)
    r_i = lax.broadcasted_iota(I32, (tm, tm), 0)
    c_i = lax.broadcasted_iota(I32, (tm, tm), 1)
    tri = jnp.where(r_i > c_i, 1.0, 0.0).astype(BF16)
    rank = _dot(tri, cnt.astype(BF16)) + carry[...]
    r1 = jnp.sum(jnp.where(oh1, rank, 0.0), axis=-1, keepdims=True)
    r2 = jnp.sum(jnp.where(oh2, rank, 0.0), axis=-1, keepdims=True)
    carry[...] = carry[...] + jnp.sum(cnt, axis=0, keepdims=True)
    cnt_ref[...] = jnp.broadcast_to(carry[...], cnt_ref.shape)

    cols_ref[...] = jnp.where(lane == 0, gate1, jnp.where(lane == 1, gate2, 0.0))
    packed = jnp.where(lane == 0, e1.astype(F32),
                       jnp.where(lane == 1, e2.astype(F32),
                                 jnp.where(lane == 2, r1, jnp.where(lane == 3, r2, 0.0))))
    ints_ref[...] = packed.T[0:8, :].astype(I32)


def _route(logits):
    t = logits.shape[0]
    tm = ROUTE_TM
    return pl.pallas_call(
        _route_kernel,
        grid=(t // tm,),
        in_specs=[pl.BlockSpec((tm, LANES), lambda i: (i, 0))],
        out_specs=[pl.BlockSpec((tm, LANES), lambda i: (i, 0)),
                   pl.BlockSpec((8, tm), lambda i: (0, i)),
                   pl.BlockSpec((8, LANES), lambda i: (0, 0))],
        out_shape=[jax.ShapeDtypeStruct((t, LANES), F32),
                   jax.ShapeDtypeStruct((8, t), I32),
                   jax.ShapeDtypeStruct((8, LANES), F32)],
        scratch_shapes=[pltpu.VMEM((1, LANES), F32)],
        compiler_params=_cparams(("arbitrary",)),
        name="route",
    )(logits)


def _plan_kernel(ints_ref, cnt_ref, dest_ref, meta_ref, *, n_blocks_pad):
    sub = lax.broadcasted_iota(I32, (LANES, LANES), 0)
    lane = lax.broadcasted_iota(I32, (LANES, LANES), 1)
    cnt = cnt_ref[0:1, :]
    nblk_row = jnp.floor((cnt + (MOE_BLK - 1.0)) * (1.0 / MOE_BLK))
    nblk_mat = jnp.broadcast_to(nblk_row, (LANES, LANES))
    start_col = jnp.sum(jnp.where(lane < sub, nblk_mat, 0.0), axis=-1, keepdims=True)
    nblk_col = jnp.sum(jnp.where(lane == sub, nblk_mat, 0.0), axis=-1, keepdims=True)
    end_col = start_col + nblk_col

    ints = ints_ref[...]
    base = jnp.zeros(ints.shape, F32)
    for e in range(N_EXPERTS):
        base = jnp.where(ints == e, start_col[e:e + 1, :] * float(MOE_BLK), base)
    dest = base[0:2, :].astype(I32) + ints[2:4, :]
    dest_ref[...] = jnp.concatenate([dest, jnp.zeros((6, ints.shape[1]), I32)], axis=0)

    blk = lax.broadcasted_iota(I32, (LANES, n_blocks_pad), 1).astype(F32)
    e_sub = lax.broadcasted_iota(I32, (LANES, n_blocks_pad), 0)
    done = jnp.where((e_sub < N_EXPERTS) & (end_col <= blk), 1.0, 0.0)
    bexp = jnp.minimum(jnp.sum(done, axis=0, keepdims=True), N_EXPERTS - 1.0)
    used = jnp.sum(nblk_row, axis=-1, keepdims=True)
    row = lax.broadcasted_iota(I32, (8, n_blocks_pad), 0)
    meta = jnp.where(row == 0, bexp, jnp.where(row == 1, used, 0.0))
    meta_ref[...] = meta.astype(I32)


def _plan(ints, counts, n_blocks_pad):
    t = ints.shape[1]
    return pl.pallas_call(
        functools.partial(_plan_kernel, n_blocks_pad=n_blocks_pad),
        out_shape=[jax.ShapeDtypeStruct((8, t), I32), jax.ShapeDtypeStruct((8, n_blocks_pad), I32)],
        compiler_params=pltpu.CompilerParams(vmem_limit_bytes=VMEM_LIMIT),
        name="plan",
    )(ints, counts)


DISPATCH_TM = 512
DMA_UNROLL = 8


def _row_tile(ref, r):
    return ref.at[pl.ds(pl.multiple_of(r * ROW_TILE, ROW_TILE), ROW_TILE)]


def _dispatch_kernel(d0_ref, d1_ref, u2_ref, rows_in_ref, rows_ref, sem):
    del rows_in_ref

    def issue(t, carry):
        src = _row_tile(u2_ref, t)
        pltpu.make_async_copy(src, _row_tile(rows_ref, d0_ref[t]), sem).start(priority=0)
        pltpu.make_async_copy(src, _row_tile(rows_ref, d1_ref[t]), sem).start(priority=1)
        return carry

    lax.fori_loop(0, DISPATCH_TM, issue, 0, unroll=DMA_UNROLL)
    for _ in range(2):
        pltpu.make_async_copy(u2_ref, rows_ref.at[pl.ds(0, DISPATCH_TM * ROW_TILE)], sem).wait()


def _dispatch(dest0, dest1, u2, rows0):
    t = u2.shape[0] // ROW_TILE
    idx = pl.BlockSpec((DISPATCH_TM,), lambda i: (i,), memory_space=pltpu.SMEM)
    return pl.pallas_call(
        _dispatch_kernel,
        grid=(t // DISPATCH_TM,),
        in_specs=[idx, idx,
                  pl.BlockSpec((DISPATCH_TM * ROW_TILE, LANES), lambda i: (i, 0)),
                  pl.BlockSpec(memory_space=pl.ANY)],
        out_specs=pl.BlockSpec(memory_space=pl.ANY),
        out_shape=jax.ShapeDtypeStruct(rows0.shape, F32),
        scratch_shapes=[pltpu.SemaphoreType.DMA(())],
        input_output_aliases={3: 0},
        compiler_params=_cparams(("arbitrary",)),
        name="dispatch",
    )(dest0, dest1, u2, rows0)


def _expert_runs(bexp, used):
    n = bexp.shape[0]
    idx = jnp.arange(n, dtype=I32)
    first = (idx < used[0]) & ((idx == 0) | (bexp != jnp.roll(bexp, 1)))
    slot = (jnp.cumsum(first.astype(I32)) - 1) % 2
    first_at_or_after = lax.cummin(jnp.where(first, idx, n)[::-1])[::-1]
    first_after = jnp.concatenate([first_at_or_after[1:], jnp.full((1,), n, I32)])
    nxt = jnp.where(first_after < n, bexp[jnp.minimum(first_after, n - 1)], -1)
    return first.astype(I32), slot.astype(I32), nxt.astype(I32)


def _experts_kernel(bexp_ref, used_ref, first_ref, slot_ref, nxt_ref, x_ref, w1_hbm, w3_hbm, w2_hbm, y_ref,
                    wb1, wb3, wb2, w1s, w3s, w2s, sem):
    i = pl.program_id(0)
    active = i < used_ref[0]
    slot = slot_ref[i]

    def fetch(e, s):
        return [pltpu.make_async_copy(w.at[e], buf.at[s], sem.at[s])
                for w, buf in ((w1_hbm, wb1), (w3_hbm, wb3), (w2_hbm, wb2))]

    @pl.when(i == 0)
    def _():
        for cp in fetch(bexp_ref[0], 0):
            cp.start()

    @pl.when(first_ref[i] == 1)
    def _():
        @pl.when(nxt_ref[i] >= 0)
        def _():
            for cp in fetch(nxt_ref[i], 1 - slot):
                cp.start()

        for cp in fetch(bexp_ref[i], slot):
            cp.wait()
        w1s[...] = wb1[slot].astype(BF16)
        w3s[...] = wb3[slot].astype(BF16)
        w2s[...] = wb2[slot].astype(BF16)

    @pl.when(active)
    def _():
        def sub_block(r0):
            xb = _load_token_tiles(x_ref, MOE_SUB, r0).astype(BF16)
            yield
            a = _dot(xb, w1s[...])
            b = _dot(xb, w3s[...])
            yield
            hdn = (a * _sigmoid(a) * b).astype(BF16)
            yield
            _store_token_tiles(y_ref, _dot(hdn, w2s[...]), r0)

        _interleave(sub_block(r0) for r0 in range(0, MOE_BLK, MOE_SUB))

    @pl.when(jnp.logical_not(active))
    def _():
        y_ref[...] = jnp.zeros_like(y_ref)


def _experts(bexp, used, rows, w1, w3, w2):
    n_blocks = rows.shape[0] // (MOE_BLK * ROW_TILE)
    first, slot, nxt = _expert_runs(bexp, used)
    any_space = pl.BlockSpec(memory_space=pl.ANY)
    grid_spec = pltpu.PrefetchScalarGridSpec(
        num_scalar_prefetch=5,
        grid=(n_blocks,),
        in_specs=[pl.BlockSpec((MOE_BLK * ROW_TILE, LANES), lambda i, be, nu, *_: (jnp.minimum(i, nu[0] - 1), 0)),
                  any_space, any_space, any_space],
        out_specs=pl.BlockSpec((MOE_BLK * ROW_TILE, LANES), lambda i, *_: (i, 0)),
        scratch_shapes=[pltpu.VMEM((2, D_MODEL, EXPERT_FF), F32), pltpu.VMEM((2, D_MODEL, EXPERT_FF), F32),
                        pltpu.VMEM((2, EXPERT_FF, D_MODEL), F32),
                        pltpu.VMEM((D_MODEL, EXPERT_FF), BF16), pltpu.VMEM((D_MODEL, EXPERT_FF), BF16),
                        pltpu.VMEM((EXPERT_FF, D_MODEL), BF16), pltpu.SemaphoreType.DMA((2,))],
    )
    return pl.pallas_call(
        _experts_kernel,
        grid_spec=grid_spec,
        out_shape=jax.ShapeDtypeStruct(rows.shape, F32),
        compiler_params=_cparams(("arbitrary",)),
        name="experts",
    )(bexp, used, first, slot, nxt, rows, w1, w3, w2)


COMBINE_TM = 256


def _combine_kernel(d0_ref, d1_ref, n0_ref, n1_ref, y_ref, cols_ref, h1_ref, mod_ref, ln_ref, o_ref, ya, yb, sem):
    i = pl.program_id(0)
    slot = i % 2

    def gather(i0_ref, i1_ref, s):
        def issue(t, carry):
            pltpu.make_async_copy(_row_tile(y_ref, i0_ref[t]), _row_tile(ya.at[s], t), sem.at[s]).start(priority=0)
            pltpu.make_async_copy(_row_tile(y_ref, i1_ref[t]), _row_tile(yb.at[s], t), sem.at[s]).start(priority=1)
            return carry

        lax.fori_loop(0, COMBINE_TM, issue, 0, unroll=DMA_UNROLL)

    @pl.when(i == 0)
    def _():
        gather(d0_ref, d1_ref, 0)

    @pl.when(i + 1 < pl.num_programs(0))
    def _():
        gather(n0_ref, n1_ref, 1 - slot)

    for buf in (ya, yb):
        pltpu.make_async_copy(y_ref.at[pl.ds(0, COMBINE_TM * ROW_TILE)], buf.at[slot], sem.at[slot]).wait()

    cols = cols_ref[...]
    moe = (cols[:, 0:1] * _load_token_tiles(ya.at[slot], COMBINE_TM)
           + cols[:, 1:2] * _load_token_tiles(yb.at[slot], COMBINE_TM))
    pre = DEEPNORM_ALPHA * h1_ref[...] + mod_ref[5:6, :] * moe
    o_ref[...] = _ln(pre) * ln_ref[0:1, :] + ln_ref[1:2, :]


def _combine(dest0, dest1, y_rows, cols, h1, mod3, ln2, seq):
    t = h1.shape[0]
    tm = COMBINE_TM
    per_seq = seq // tm
    idx = pl.BlockSpec((tm,), lambda i: (i,), memory_space=pltpu.SMEM)
    idx_next = pl.BlockSpec((tm,), lambda i: (jnp.minimum(i + 1, t // tm - 1),), memory_space=pltpu.SMEM)
    return pl.pallas_call(
        _combine_kernel,
        grid=(t // tm,),
        in_specs=[idx, idx, idx_next, idx_next,
                  pl.BlockSpec(memory_space=pl.ANY),
                  pl.BlockSpec((tm, LANES), lambda i: (i, 0)),
                  pl.BlockSpec((tm, D_MODEL), lambda i: (i, 0)),
                  pl.BlockSpec((None, 6, D_MODEL), lambda i: (i // per_seq, 0, 0)),
                  pl.BlockSpec((2, D_MODEL), lambda i: (0, 0))],
        out_specs=pl.BlockSpec((tm, D_MODEL), lambda i: (i, 0)),
        out_shape=jax.ShapeDtypeStruct((t, D_MODEL), F32),
        scratch_shapes=[pltpu.VMEM((2, tm * ROW_TILE, LANES), F32), pltpu.VMEM((2, tm * ROW_TILE, LANES), F32),
                        pltpu.SemaphoreType.DMA((2,))],
        compiler_params=_cparams(("arbitrary",)),
        name="combine",
    )(dest0, dest1, dest0, dest1, y_rows, cols, h1, mod3, ln2)


def _layer(h, c8, w_ada, b_ada, w_in, w_attn_out, decay_f, decay_b, gn_gain, w_ret_out, w_out,
           ln1_gain, ln1_bias, w_coarse, b_coarse, w_fine, b_fine, w1, w3, w2, ln2_gain, ln2_bias):
    batch, seq, d = h.shape
    t = batch * seq
    x2 = h.reshape(t, d)

    mod = _ada(c8, w_ada, b_ada.reshape(1, -1))
    mod3 = mod[:batch].reshape(batch, 6, d)

    aw = ATTN_HEADS * HEAD_DIM
    wb = w_in.astype(BF16)
    w_qkv = wb[:, :3 * aw].reshape(d, 3, ATTN_GROUPS, GROUP_WIDTH).transpose(0, 2, 1, 3).reshape(d, 3 * aw)
    rq0, rk0, rv0, tail0 = 3 * aw, 3 * aw + RET_HEADS * RET_QK, 3 * aw + 2 * RET_HEADS * RET_QK, \
        3 * aw + 2 * RET_HEADS * RET_QK + RET_HEADS * RET_V
    per_head = [wb[:, o:o + RET_HEADS * wd].reshape(d, RET_HEADS, wd)
                for o, wd in ((rq0, RET_QK), (rk0, RET_QK), (rv0, RET_V))]
    w_ret = jnp.concatenate(per_head, axis=2).reshape(d, RET_HEADS * RET_HEAD_COLS)
    w_rest = jnp.concatenate([w_ret, wb[:, tail0:]], axis=1)

    qkv = _inproj_attn(x2, mod3, w_qkv, seq)
    rest = _inproj_rest(x2, mod3, w_rest, seq)

    outs, lses = zip(*[_attention(qkv, g, batch, seq) for g in range(ATTN_GROUPS)])
    decays = jnp.concatenate([decay_f, decay_b]).astype(F32)
    retg = _retention(rest, decays, gn_gain.reshape(1, -1), batch, seq)

    w_route = jnp.concatenate([w_coarse, w_fine.transpose(1, 0, 2).reshape(d, N_EXPERTS)], axis=1)
    n_route = w_route.shape[1]
    w_route = jnp.pad(w_route, ((0, 0), (0, LANES - n_route)))
    wr_hi, wr_lo = _split_bf16(w_route)
    rbias = jnp.pad(jnp.concatenate([b_coarse, b_fine.reshape(-1)]), (0, LANES - n_route)).reshape(1, LANES)
    ln1 = jnp.stack([ln1_gain, ln1_bias])
    n_blocks = 2 * t // MOE_BLK + N_EXPERTS
    h1, u2, logits, rows0 = _merge(outs, lses, retg, rest, x2, mod3, w_attn_out.astype(BF16),
                                   w_ret_out.astype(BF16), w_out.astype(BF16), ln1, wr_hi, wr_lo, rbias, seq,
                                   n_blocks * MOE_BLK)

    cols, ints, counts = _route(logits)
    n_blocks_pad = -(-n_blocks // LANES) * LANES
    dest, meta = _plan(ints, counts, n_blocks_pad)
    dest0, dest1 = dest[0], dest[1]
    rows = _dispatch(dest0, dest1, u2, rows0)
    y_rows = _experts(meta[0, :n_blocks], meta[1, :1], rows, w1, w3, w2)
    out = _combine(dest0, dest1, y_rows, cols, h1, mod3, jnp.stack([ln2_gain, ln2_bias]), seq)
    return out.reshape(batch, seq, d)


def kernel(x, c, w_ada, b_ada, w_in, w_attn_out, ret_decay_fwd, ret_decay_bwd, ret_gn_gain, w_ret_out, w_out,
           ln1_gain, ln1_bias, w_coarse, b_coarse, w_fine, b_fine, w1, w3, w2, ln2_gain, ln2_bias):
    batch = x.shape[0]
    assert batch <= 8 and x.shape[1] % (2 * INPROJ_TM) == 0 and x.shape[2] == D_MODEL
    c8 = jnp.pad(c, ((0, 8 - batch), (0, 0)))
    h = x
    for l in range(w_ada.shape[0]):
        h = _layer(h, c8, w_ada[l], b_ada[l], w_in[l], w_attn_out[l], ret_decay_fwd[l], ret_decay_bwd[l],
                   ret_gn_gain[l], w_ret_out[l], w_out[l], ln1_gain[l], ln1_bias[l], w_coarse[l], b_coarse[l],
                   w_fine[l], b_fine[l], w1[l], w3[l], w2[l], ln2_gain[l], ln2_bias[l])
    return h
```

```python
import functools
import math

import jax
import jax.numpy as jnp
from jax import lax
from jax.experimental import pallas as pl
from jax.experimental.pallas import tpu as pltpu

F32 = jnp.float32
BF16 = jnp.bfloat16
I32 = jnp.int32

D_MODEL = 1024
ATTN_GROUPS = 3
HEADS_PER_GROUP = 4
HEAD_DIM = 128
ATTN_HEADS = ATTN_GROUPS * HEADS_PER_GROUP
GROUP_WIDTH = HEADS_PER_GROUP * HEAD_DIM
ATTN_PATTERNS = ((128, 1), (512, 4), (2048, 16))
ALIBI_MAX_EXP = 8.0
NEG = -1e30
RET_HEADS = 4
RET_QK = 256
RET_V = 512
N_GROUPS = 4
EXPERTS_PER_GROUP = 8
N_EXPERTS = N_GROUPS * EXPERTS_PER_GROUP
EXPERT_FF = 512
DEPTH = 1
DEEPNORM_ALPHA = (2.0 * DEPTH) ** 0.25
LN_EPS = 1e-5

LANES = 128
PERM_TILE = 512
ATTN_QB = 128
ATTN_HALF = 64
ATTN_KB = ATTN_QB + 2 * ATTN_HALF
RET_CHUNK = 256
RET_STEP = 1024
MOE_BLK = 512
MOE_SUB = 256
VMEM_LIMIT = 56 * 1024 * 1024


def _cparams(sem):
    return pltpu.CompilerParams(dimension_semantics=sem, vmem_limit_bytes=VMEM_LIMIT)


def _split_bf16(a):
    hi = a.astype(BF16)
    lo = (a - hi.astype(F32)).astype(BF16)
    return hi, lo


def _dot(a, b):
    return jnp.dot(a, b, preferred_element_type=F32)


def _dot3(a, b):
    ah, al = _split_bf16(a)
    bh, bl = _split_bf16(b)
    return _dot(ah, bh) + _dot(ah, bl) + _dot(al, bh)


def _ln(x):
    mu = jnp.mean(x, axis=-1, keepdims=True)
    xc = x - mu
    var = jnp.mean(xc * xc, axis=-1, keepdims=True)
    return xc * lax.rsqrt(var + LN_EPS)


def _sigmoid(x):
    return 1.0 / (1.0 + jnp.exp(-x))


ROW_TILE = D_MODEL // LANES


def _store_token_tiles(ref, val, row0=0):
    n = val.shape[0]
    for cc in range(ROW_TILE):
        ref[pl.ds(row0 * ROW_TILE + cc, n, stride=ROW_TILE), :] = val[:, cc * LANES:(cc + 1) * LANES]


def _load_token_tiles(ref, n, row0=0):
    return jnp.concatenate([ref[pl.ds(row0 * ROW_TILE + cc, n, stride=ROW_TILE), :] for cc in range(ROW_TILE)],
                           axis=1)


def _interleave(phased):
    pending = []
    for gen in phased:
        pending.append(gen)
        pending = [g for g in pending if next(g, StopIteration) is not StopIteration]
    while pending:
        pending = [g for g in pending if next(g, StopIteration) is not StopIteration]


def _ada_kernel(c_ref, w_ref, b_ref, o_ref):
    o_ref[...] = _dot3(c_ref[...], w_ref[...]) + b_ref[...]


def _ada(c8, w_ada, b_ada):
    n = w_ada.shape[1]
    return pl.pallas_call(
        _ada_kernel,
        grid=(n // D_MODEL,),
        in_specs=[pl.BlockSpec((8, D_MODEL), lambda j: (0, 0)),
                  pl.BlockSpec((D_MODEL, D_MODEL), lambda j: (0, j)),
                  pl.BlockSpec((1, D_MODEL), lambda j: (0, j))],
        out_specs=pl.BlockSpec((8, D_MODEL), lambda j: (0, j)),
        out_shape=jax.ShapeDtypeStruct((8, n), F32),
        compiler_params=_cparams(("arbitrary",)),
        name="ada",
    )(c8, w_ada, b_ada)


INPROJ_TM = 1024
LN_ROWS = 256
LN_UNROLL = 4


def _modulated_ln_rows(x_ref, mod_ref, emit):
    gain = 1.0 + mod_ref[1:2, :]
    shift = mod_ref[0:1, :]

    def chunk(c, carry):
        r0 = pl.multiple_of(c * LN_ROWS, LN_ROWS)
        emit(r0, _ln(x_ref[pl.ds(r0, LN_ROWS), :]) * gain + shift)
        return carry

    lax.fori_loop(0, INPROJ_TM // LN_ROWS, chunk, 0, unroll=LN_UNROLL)


def _inproj_attn_kernel(x_ref, mod_ref, wq_ref, wk_ref, wv_ref, o_ref, uf_scr, u_scr):
    j = pl.program_id(1)

    @pl.when(j == 0)
    def _():
        def emit(r0, u):
            for cc in range(D_MODEL // LANES):
                uf_scr[cc, pl.ds(r0, LN_ROWS), :] = u[:, cc * LANES:(cc + 1) * LANES]
            u_scr[0, pl.ds(r0, LN_ROWS), :] = u.astype(BF16)

        _modulated_ln_rows(x_ref, mod_ref, emit)
        for g in (1, 2):
            dil = ATTN_PATTERNS[g][1]
            n = PERM_TILE // dil
            for t0 in range(0, INPROJ_TM, PERM_TILE):
                for res in range(dil):
                    for cc in range(D_MODEL // LANES):
                        rows = uf_scr[cc, pl.ds(t0 + res, n, stride=dil), :]
                        u_scr[g, t0 + res * n:t0 + (res + 1) * n, cc * LANES:(cc + 1) * LANES] = rows.astype(BF16)

    u = u_scr[j]
    for s, w_ref in enumerate((wq_ref, wk_ref, wv_ref)):
        o_ref[:, s * GROUP_WIDTH:(s + 1) * GROUP_WIDTH] = _dot(u, w_ref[...]).astype(BF16)


ATTN_WIDTH = ATTN_HEADS * HEAD_DIM
W_RQ0 = 3 * ATTN_WIDTH
W_RK0 = W_RQ0 + RET_HEADS * RET_QK
W_RV0 = W_RK0 + RET_HEADS * RET_QK
W_TAIL0 = W_RV0 + RET_HEADS * RET_V


def _inproj_attn(x2, mod3, wb, seq):
    t = x2.shape[0]
    tn = 3 * GROUP_WIDTH
    per_seq = seq // INPROJ_TM
    groups_per_range = ATTN_WIDTH // GROUP_WIDTH
    wspec = lambda s: pl.BlockSpec((D_MODEL, GROUP_WIDTH), lambda i, j: (0, s * groups_per_range + j))
    return pl.pallas_call(
        _inproj_attn_kernel,
        grid=(t // INPROJ_TM, ATTN_GROUPS),
        in_specs=[pl.BlockSpec((INPROJ_TM, D_MODEL), lambda i, j: (i, 0)),
                  pl.BlockSpec((None, 6, D_MODEL), lambda i, j: (i // per_seq, 0, 0)),
                  wspec(0), wspec(1), wspec(2)],
        out_specs=pl.BlockSpec((INPROJ_TM, tn), lambda i, j: (i, j)),
        out_shape=jax.ShapeDtypeStruct((t, ATTN_GROUPS * tn), BF16),
        scratch_shapes=[pltpu.VMEM((D_MODEL // LANES, INPROJ_TM, LANES), F32),
                        pltpu.VMEM((ATTN_GROUPS, INPROJ_TM, D_MODEL), BF16)],
        compiler_params=_cparams(("arbitrary", "arbitrary")),
        name="inproj_attn",
    )(x2, mod3, wb, wb, wb)


REST_TN = RET_HEAD_COLS = 2 * RET_QK + RET_V
REST_TAIL_TILES = (RET_HEADS * RET_V + 2 * D_MODEL) // REST_TN


def _inproj_rest_kernel(x_ref, mod_ref, wq_ref, wk_ref, wv_ref, wt0_ref, wt1_ref, o_ref, u_scr):
    j = pl.program_id(1)

    @pl.when(j == 0)
    def _():
        def emit(r0, u):
            u_scr[pl.ds(r0, LN_ROWS), :] = u.astype(BF16)

        _modulated_ln_rows(x_ref, mod_ref, emit)

    def project(parts):
        u = u_scr[...]
        c0 = 0
        for w_ref in parts:
            wd = w_ref.shape[1]
            o_ref[:, c0:c0 + wd] = _dot(u, w_ref[...]).astype(BF16)
            c0 += wd

    @pl.when(j < RET_HEADS)
    def _():
        project((wq_ref, wk_ref, wv_ref))

    @pl.when(j >= RET_HEADS)
    def _():
        project((wt0_ref, wt1_ref))


def _inproj_rest(x2, mod3, wb, seq):
    t = x2.shape[0]
    per_seq = seq // INPROJ_TM
    half = REST_TN // 2
    assert W_TAIL0 % half == 0 and W_RV0 % RET_V == 0
    head = lambda j: jnp.minimum(j, RET_HEADS - 1)
    tail = lambda j: jnp.maximum(j - RET_HEADS, 0)
    return pl.pallas_call(
        _inproj_rest_kernel,
        grid=(t // INPROJ_TM, RET_HEADS + REST_TAIL_TILES),
        in_specs=[pl.BlockSpec((INPROJ_TM, D_MODEL), lambda i, j: (i, 0)),
                  pl.BlockSpec((None, 6, D_MODEL), lambda i, j: (i // per_seq, 0, 0)),
                  pl.BlockSpec((D_MODEL, RET_QK), lambda i, j: (0, W_RQ0 // RET_QK + head(j))),
                  pl.BlockSpec((D_MODEL, RET_QK), lambda i, j: (0, W_RK0 // RET_QK + head(j))),
                  pl.BlockSpec((D_MODEL, RET_V), lambda i, j: (0, W_RV0 // RET_V + head(j))),
                  pl.BlockSpec((D_MODEL, half), lambda i, j: (0, W_TAIL0 // half + 2 * tail(j))),
                  pl.BlockSpec((D_MODEL, half), lambda i, j: (0, W_TAIL0 // half + 2 * tail(j) + 1))],
        out_specs=pl.BlockSpec((INPROJ_TM, REST_TN), lambda i, j: (i, j)),
        out_shape=jax.ShapeDtypeStruct((t, (RET_HEADS + REST_TAIL_TILES) * REST_TN), BF16),
        scratch_shapes=[pltpu.VMEM((INPROJ_TM, D_MODEL), BF16)],
        compiler_params=_cparams(("arbitrary", "arbitrary")),
        name="inproj_rest",
    )(x2, mod3, wb, wb, wb, wb, wb)


ATTN_OFFSETS = ATTN_KB // ATTN_HALF - 1
ATTN_UNROLL = 4
ATTN_MIN_BLOCKS = 4


def _attn_kernel(q_ref, k_ref, v_ref, o_ref, lse_ref, bias_scr, *scratch, group, dil, nt, n, nres):
    n_sub = nt * n
    if nt == 1:
        seqs = [(q_ref.at[0, rr], k_ref.at[0, rr], v_ref.at[0, rr], o_ref.at[0, rr], lse_ref.at[0, rr])
                for rr in range(nres)]
    else:
        seqs = [tuple(s.at[rr] for s in scratch) for rr in range(nres)]
        for rr, (qs, ks, vs, _, _) in enumerate(seqs):
            for t in range(nt):
                qs[t * n:(t + 1) * n, :] = q_ref[t, rr]
                ks[t * n:(t + 1) * n, :] = k_ref[t, rr]
                vs[t * n:(t + 1) * n, :] = v_ref[t, rr]

    @pl.when((pl.program_id(0) == 0) & (pl.program_id(1) == 0))
    def _():
        base = (lax.broadcasted_iota(I32, (ATTN_QB, ATTN_KB), 1)
                - lax.broadcasted_iota(I32, (ATTN_QB, ATTN_KB), 0))
        for j in range(ATTN_OFFSETS):
            dist = jnp.abs(base - j * ATTN_HALF)
            for hh in range(HEADS_PER_GROUP):
                slope = dil * 2.0 ** (-ALIBI_MAX_EXP * (group * HEADS_PER_GROUP + hh + 1) / ATTN_HEADS)
                bias_scr[hh * ATTN_OFFSETS + j] = jnp.where(dist <= ATTN_HALF, -slope * dist.astype(F32), NEG)

    scale = HEAD_DIM ** -0.5
    lane = lax.broadcasted_iota(I32, (ATTN_QB, LANES), 1)

    def block(blk, carry):
        q0 = pl.multiple_of(blk * ATTN_QB, ATTN_QB)
        start = pl.multiple_of(jnp.clip(q0 - ATTN_HALF, 0, n_sub - ATTN_KB), ATTN_HALF)
        j = (q0 - start) // ATTN_HALF
        for qs, ks, vs, os_, ls in seqs:
            lse_tile = jnp.zeros((ATTN_QB, LANES), F32)
            for hh in range(HEADS_PER_GROUP):
                cs = slice(hh * HEAD_DIM, (hh + 1) * HEAD_DIM)
                qb = qs[pl.ds(q0, ATTN_QB), cs]
                kb = ks[pl.ds(start, ATTN_KB), cs]
                vb = vs[pl.ds(start, ATTN_KB), cs]
                s = lax.dot_general(qb, kb, (((1,), (1,)), ((), ())), preferred_element_type=F32) * scale
                s = s + bias_scr[hh * ATTN_OFFSETS + j]
                m = jnp.max(s, axis=-1, keepdims=True)
                p = jnp.exp(s - m)
                l = jnp.sum(p, axis=-1, keepdims=True)
                o = _dot(p.astype(BF16), vb) * (1.0 / l)
                os_[pl.ds(q0, ATTN_QB), cs] = o.astype(BF16)
                lse_tile = jnp.where(lane == hh, m + jnp.log(l), lse_tile)
            ls[pl.ds(q0, ATTN_QB), :] = lse_tile
        return carry

    n_blk = n_sub // ATTN_QB
    lax.fori_loop(0, n_blk, block, 0, unroll=min(max(ATTN_UNROLL // nres, 1), n_blk))

    if nt > 1:
        for rr, (_, _, _, os_, ls) in enumerate(seqs):
            for t in range(nt):
                o_ref[t, rr] = os_[t * n:(t + 1) * n, :]
                lse_ref[t, rr] = ls[t * n:(t + 1) * n, :]


def _attention(qkv, group, batch, seq):
    dil = ATTN_PATTERNS[group][1]
    if dil == 1:
        nt, n = 1, seq
    else:
        nt, n = seq // PERM_TILE, PERM_TILE // dil
    n_sub = nt * n
    t = batch * seq
    qkv5 = qkv.reshape(batch, nt, dil, n, qkv.shape[1])
    cb = group * 3
    nres = min(dil, max(1, ATTN_MIN_BLOCKS * ATTN_QB // n_sub))
    blk = (None, nt, nres, n, GROUP_WIDTH)
    scratch = [pltpu.VMEM((HEADS_PER_GROUP * ATTN_OFFSETS, ATTN_QB, ATTN_KB), F32)]
    if nt > 1:
        scratch += [pltpu.VMEM((nres, n_sub, GROUP_WIDTH), BF16)] * 4 + [pltpu.VMEM((nres, n_sub, LANES), F32)]
    out, lse = pl.pallas_call(
        functools.partial(_attn_kernel, group=group, dil=dil, nt=nt, n=n, nres=nres),
        grid=(batch, dil // nres),
        in_specs=[pl.BlockSpec(blk, lambda b, r: (b, 0, r, 0, cb)),
                  pl.BlockSpec(blk, lambda b, r: (b, 0, r, 0, cb + 1)),
                  pl.BlockSpec(blk, lambda b, r: (b, 0, r, 0, cb + 2))],
        out_specs=[pl.BlockSpec(blk, lambda b, r: (b, 0, r, 0, 0)),
                   pl.BlockSpec((None, nt, nres, n, LANES), lambda b, r: (b, 0, r, 0, 0))],
        out_shape=[jax.ShapeDtypeStruct((batch, nt, dil, n, GROUP_WIDTH), BF16),
                   jax.ShapeDtypeStruct((batch, nt, dil, n, LANES), F32)],
        scratch_shapes=scratch,
        compiler_params=_cparams(("arbitrary", "arbitrary")),
        name=f"attn_g{group}",
    )(qkv5, qkv5, qkv5)
    return out.reshape(t, GROUP_WIDTH), lse.reshape(t, LANES)


def _log_sigmoid(x):
    return jnp.minimum(x, 0.0) - jnp.log(1.0 + jnp.exp(-jnp.abs(x)))


def _retention_kernel(decay_ref, qkv_ref, g_ref, gain_ref, o_ref, state, ybwd, dmat, kdec, qdec, *, nc):
    c = RET_CHUNK
    h = pl.program_id(1)
    i = pl.program_id(2)
    kscale = RET_QK ** -0.5

    def set_decays(lg, forward):
        row = lax.broadcasted_iota(I32, (c, c), 0)
        col = lax.broadcasted_iota(I32, (c, c), 1)
        pos = lax.broadcasted_iota(I32, (c, LANES), 0).astype(F32)
        if forward:
            gap, key_pow, query_pow = row - col, (c - 1.0) - pos, pos + 1.0
            keep = gap >= 0
        else:
            gap, key_pow, query_pow = col - row, pos, c - pos
            keep = gap > 0
        dmat[...] = jnp.where(keep, jnp.exp(lg * jnp.maximum(gap, 0).astype(F32)) * kscale, 0.0)
        kdec[...] = jnp.exp(lg * key_pow) * kscale
        qdec[...] = jnp.exp(lg * query_pow)

    def chunk_step(lg, r0, finish):
        q = qkv_ref[r0:r0 + c, 0:RET_QK]
        k = qkv_ref[r0:r0 + c, RET_QK:2 * RET_QK]
        v = qkv_ref[r0:r0 + c, 2 * RET_QK:]
        inner = lax.dot_general(q, k, (((1,), (1,)), ((), ())), preferred_element_type=F32) * dmat[...]
        y = _dot(inner.astype(BF16), v)
        kd = (k.astype(F32) * jnp.concatenate([kdec[...]] * (RET_QK // LANES), axis=1)).astype(BF16)
        kv = lax.dot_general(kd, v, (((0,), (0,)), ((), ())), preferred_element_type=F32)
        yield
        qd = jnp.concatenate([qdec[...]] * (RET_V // LANES), axis=1)
        y = y + _dot(q, state[...].astype(BF16)) * qd
        state[...] = state[...] * jnp.exp(lg * float(c)) + kv
        yield
        finish(r0, y)

    @pl.when((i == 0) | (i == nc))
    def _():
        state[...] = jnp.zeros_like(state)

    @pl.when(i < nc)
    def _():
        lg = _log_sigmoid(jnp.zeros((1, 1), F32) + decay_ref[RET_HEADS + h])

        @pl.when(i == 0)
        def _():
            set_decays(lg, False)

        base = pl.multiple_of((nc - 1 - i) * RET_STEP, RET_STEP)

        def finish(r0, y):
            ybwd[pl.ds(base + r0, c), :] = y

        _interleave(chunk_step(lg, r0, finish) for r0 in reversed(range(0, RET_STEP, c)))

    @pl.when(i >= nc)
    def _():
        lg = _log_sigmoid(jnp.zeros((1, 1), F32) + decay_ref[h])

        @pl.when(i == nc)
        def _():
            set_decays(lg, True)

        base = pl.multiple_of((i - nc) * RET_STEP, RET_STEP)

        def finish(r0, y):
            y = y + ybwd[pl.ds(base + r0, c), :]
            g = g_ref[r0:r0 + c, :].astype(F32)
            o_ref[r0:r0 + c, :] = (g * _sigmoid(g) * (_ln(y) * gain_ref[...])).astype(BF16)

        _interleave(chunk_step(lg, r0, finish) for r0 in range(0, RET_STEP, c))


def _retention(rest, decays, gn_gain, batch, seq):
    c = RET_STEP
    nc = seq // c
    rest3 = rest.reshape(batch, seq, rest.shape[1])
    gate_blk = RET_HEADS * RET_HEAD_COLS // RET_V

    def chunk(i):
        return jnp.where(i < nc, nc - 1 - i, i - nc)

    grid_spec = pltpu.PrefetchScalarGridSpec(
        num_scalar_prefetch=1,
        grid=(batch, RET_HEADS, 2 * nc),
        in_specs=[pl.BlockSpec((None, c, RET_HEAD_COLS), lambda b, h, i, d: (b, chunk(i), h)),
                  pl.BlockSpec((None, c, RET_V), lambda b, h, i, d: (b, jnp.maximum(i - nc, 0), gate_blk + h)),
                  pl.BlockSpec((1, RET_V), lambda b, h, i, d: (0, h))],
        out_specs=pl.BlockSpec((None, c, RET_V), lambda b, h, i, d: (b, jnp.maximum(i - nc, 0), h)),
        scratch_shapes=[pltpu.VMEM((RET_QK, RET_V), F32), pltpu.VMEM((seq, RET_V), F32),
                        pltpu.VMEM((RET_CHUNK, RET_CHUNK), F32), pltpu.VMEM((RET_CHUNK, LANES), F32),
                        pltpu.VMEM((RET_CHUNK, LANES), F32)],
    )
    out = pl.pallas_call(
        functools.partial(_retention_kernel, nc=nc),
        grid_spec=grid_spec,
        out_shape=jax.ShapeDtypeStruct((batch, seq, RET_HEADS * RET_V), BF16),
        compiler_params=_cparams(("arbitrary", "arbitrary", "arbitrary")),
        name="retention",
    )(decays, rest3, rest3, gn_gain)
    return out.reshape(batch * seq, RET_HEADS * RET_V)


MERGE_TM = PERM_TILE
MERGE_SUB = 256
ZERO_SPLIT = 4


def _merge_kernel(o0_ref, o1_ref, o2_ref, l0_ref, l1_ref, l2_ref, retg_ref, ga_ref, gr_ref, x_ref, mod_ref,
                  wa_ref, wr_ref, wo_ref, ln_ref, wrh_ref, wrl_ref, rb_ref,
                  h1_ref, u2_ref, lg_ref, zrows_ref, on_scr, ln_scr, zbuf, zsem):
    i = pl.program_id(0)

    @pl.when(i == 0)
    def _():
        zbuf[...] = jnp.zeros_like(zbuf)

    zrows = zbuf.shape[0]
    zero_copies = [pltpu.make_async_copy(zbuf, zrows_ref.at[pl.ds((i * ZERO_SPLIT + k) * zrows, zrows)], zsem)
                   for k in range(ZERO_SPLIT)]
    for cp in zero_copies:
        cp.start()

    for g in (1, 2):
        dil = ATTN_PATTERNS[g][1]
        n = MERGE_TM // dil
        o_ref, l_ref = ((o1_ref, l1_ref), (o2_ref, l2_ref))[g - 1]
        for res in range(dil):
            rows = o_ref[res * n:(res + 1) * n, :].astype(F32)
            for hh in range(HEADS_PER_GROUP):
                on_scr[g - 1, hh, pl.ds(res, n, stride=dil), :] = rows[:, hh * HEAD_DIM:(hh + 1) * HEAD_DIM]
            ln_scr[g - 1, pl.ds(res, n, stride=dil), :] = l_ref[res * n:(res + 1) * n, :]

    def sub_tile(r0):
        rs = slice(r0, r0 + MERGE_SUB)
        l0, l1, l2 = l0_ref[rs, :], ln_scr[0, rs, :], ln_scr[1, rs, :]
        lm = jnp.maximum(jnp.maximum(l0, l1), l2)
        e0, e1, e2 = jnp.exp(l0 - lm), jnp.exp(l1 - lm), jnp.exp(l2 - lm)
        inv = 1.0 / (e0 + e1 + e2)
        parts = []
        for hh in range(HEADS_PER_GROUP):
            sl = slice(hh * HEAD_DIM, (hh + 1) * HEAD_DIM)
            acc = (e0[:, hh:hh + 1] * o0_ref[rs, sl].astype(F32)
                   + e1[:, hh:hh + 1] * on_scr[0, hh, rs, :]
                   + e2[:, hh:hh + 1] * on_scr[1, hh, rs, :])
            parts.append((acc * inv[:, hh:hh + 1]).astype(BF16))
        attn = jnp.concatenate(parts, axis=1)
        yield

        branch_a = _dot(attn, wa_ref[...])
        branch_r = _dot(retg_ref[rs, :], wr_ref[...])
        yield
        merged = (_sigmoid(ga_ref[rs, :].astype(F32)) * branch_a
                  + _sigmoid(gr_ref[rs, :].astype(F32)) * branch_r)
        yield
        y = _dot(merged.astype(BF16), wo_ref[...])
        yield

        h1 = _ln(DEEPNORM_ALPHA * x_ref[rs, :] + mod_ref[2:3, :] * y) * ln_ref[0:1, :] + ln_ref[1:2, :]
        h1_ref[rs, :] = h1
        u2 = _ln(h1) * (1.0 + mod_ref[4:5, :]) + mod_ref[3:4, :]
        _store_token_tiles(u2_ref, u2, r0)
        uh, ul = _split_bf16(u2)
        yield
        lg_ref[rs, :] = (_dot(uh, wrh_ref[...]) + _dot(uh, wrl_ref[...]) + _dot(ul, wrh_ref[...])
                         + rb_ref[...])

    _interleave(sub_tile(r0) for r0 in range(0, MERGE_TM, MERGE_SUB))

    for cp in zero_copies:
        cp.wait()


def _merge(outs, lses, retg, rest, x2, mod3, wa, wr, wo, ln1, wr_hi, wr_lo, rbias, seq, n_rows):
    t = x2.shape[0]
    tm = MERGE_TM
    per_seq = seq // tm
    zrows, rem = divmod(n_rows * ROW_TILE, (t // tm) * ZERO_SPLIT)
    assert rem == 0 and zrows % 8 == 0
    row = lambda w: pl.BlockSpec((tm, w), lambda i: (i, 0))
    full = lambda a: pl.BlockSpec(a.shape, lambda i: (0,) * a.ndim)
    return pl.pallas_call(
        _merge_kernel,
        grid=(t // tm,),
        in_specs=[row(GROUP_WIDTH)] * 3 + [row(LANES)] * 3 + [
            row(RET_HEADS * RET_V),
            pl.BlockSpec((tm, D_MODEL), lambda i: (i, 6)),
            pl.BlockSpec((tm, D_MODEL), lambda i: (i, 7)),
            row(D_MODEL),
            pl.BlockSpec((None, 6, D_MODEL), lambda i: (i // per_seq, 0, 0)),
            full(wa), full(wr), full(wo), full(ln1), full(wr_hi), full(wr_lo), full(rbias)],
        out_specs=[row(D_MODEL), pl.BlockSpec((tm * ROW_TILE, LANES), lambda i: (i, 0)), row(LANES),
                   pl.BlockSpec(memory_space=pl.ANY)],
        out_shape=[jax.ShapeDtypeStruct((t, D_MODEL), F32),
                   jax.ShapeDtypeStruct((t * ROW_TILE, LANES), F32),
                   jax.ShapeDtypeStruct((t, LANES), F32),
                   jax.ShapeDtypeStruct((n_rows * ROW_TILE, LANES), F32)],
        scratch_shapes=[pltpu.VMEM((2, HEADS_PER_GROUP, tm, HEAD_DIM), F32), pltpu.VMEM((2, tm, LANES), F32),
                        pltpu.VMEM((zrows, LANES), F32), pltpu.SemaphoreType.DMA(())],
        compiler_params=_cparams(("arbitrary",)),
        name="merge",
    )(*outs, *lses, retg, rest, rest, x2, mod3, wa, wr, wo, ln1, wr_hi, wr_lo, rbias)


ROUTE_TM = 512
BIG = 1 << 20


def _route_kernel(lg_ref, cols_ref, ints_ref, cnt_ref, carry):
    i = pl.program_id(0)

    @pl.when(i == 0)
    def _():
        carry[...] = jnp.zeros_like(carry)

    tm = ROUTE_TM
    lg = lg_ref[...]
    lane = lax.broadcasted_iota(I32, (tm, LANES), 1)
    lane_f = lane.astype(F32)
    first = lambda mask: jnp.min(jnp.where(mask, lane_f, float(BIG)), axis=-1, keepdims=True).astype(I32)

    coarse = jnp.where(lane < N_GROUPS, lg, NEG)
    cmax = jnp.max(coarse, axis=-1, keepdims=True)
    gsel = first(coarse == cmax)
    p_group = 1.0 / jnp.sum(jnp.exp(coarse - cmax), axis=-1, keepdims=True)

    lo = N_GROUPS + EXPERTS_PER_GROUP * gsel
    fine = jnp.where((lane >= lo) & (lane < lo + EXPERTS_PER_GROUP), lg, NEG)
    v1 = jnp.max(fine, axis=-1, keepdims=True)
    i1 = first(fine == v1)
    fine2 = jnp.where(lane == i1, NEG, fine)
    v2 = jnp.max(fine2, axis=-1, keepdims=True)
    i2 = first(fine2 == v2)
    ex = jnp.exp(v2 - v1)
    den = 1.0 / (1.0 + ex)
    gate1 = p_group * den
    gate2 = p_group * (ex * den)
    e1 = i1 - N_GROUPS
    e2 = i2 - N_GROUPS

    oh1 = lane == e1
    oh2 = lane == e2
    cnt = jnp.where(oh1 | oh2, 1.0, 0.0)
    r_i = lax.broadcasted_iota(I32, (tm, tm), 0)
    c_i = lax.broadcasted_iota(I32, (tm, tm), 1)
    tri = jnp.where(r_i > c_i, 1.0, 0.0).astype(BF16)
    rank = _dot(tri, cnt.astype(BF16)) + carry[...]
    r1 = jnp.sum(jnp.where(oh1, rank, 0.0), axis=-1, keepdims=True)
    r2 = jnp.sum(jnp.where(oh2, rank, 0.0), axis=-1, keepdims=True)
    carry[...] = carry[...] + jnp.sum(cnt, axis=0, keepdims=True)
    cnt_ref[...] = jnp.broadcast_to(carry[...], cnt_ref.shape)

    cols_ref[...] = jnp.where(lane == 0, gate1, jnp.where(lane == 1, gate2, 0.0))
    packed = jnp.where(lane == 0, e1.astype(F32),
                       jnp.where(lane == 1, e2.astype(F32),
                                 jnp.where(lane == 2, r1, jnp.where(lane == 3, r2, 0.0))))
    ints_ref[...] = packed.T[0:8, :].astype(I32)


def _route(logits):
    t = logits.shape[0]
    tm = ROUTE_TM
    return pl.pallas_call(
        _route_kernel,
        grid=(t // tm,),
        in_specs=[pl.BlockSpec((tm, LANES), lambda i: (i, 0))],
        out_specs=[pl.BlockSpec((tm, LANES), lambda i: (i, 0)),
                   pl.BlockSpec((8, tm), lambda i: (0, i)),
                   pl.BlockSpec((8, LANES), lambda i: (0, 0))],
        out_shape=[jax.ShapeDtypeStruct((t, LANES), F32),
                   jax.ShapeDtypeStruct((8, t), I32),
                   jax.ShapeDtypeStruct((8, LANES), F32)],
        scratch_shapes=[pltpu.VMEM((1, LANES), F32)],
        compiler_params=_cparams(("arbitrary",)),
        name="route",
    )(logits)


def _plan_kernel(ints_ref, cnt_ref, dest_ref, meta_ref, *, n_blocks_pad):
    sub = lax.broadcasted_iota(I32, (LANES, LANES), 0)
    lane = lax.broadcasted_iota(I32, (LANES, LANES), 1)
    cnt = cnt_ref[0:1, :]
    nblk_row = jnp.floor((cnt + (MOE_BLK - 1.0)) * (1.0 / MOE_BLK))
    nblk_mat = jnp.broadcast_to(nblk_row, (LANES, LANES))
    start_col = jnp.sum(jnp.where(lane < sub, nblk_mat, 0.0), axis=-1, keepdims=True)
    nblk_col = jnp.sum(jnp.where(lane == sub, nblk_mat, 0.0), axis=-1, keepdims=True)
    end_col = start_col + nblk_col

    ints = ints_ref[...]
    base = jnp.zeros(ints.shape, F32)
    for e in range(N_EXPERTS):
        base = jnp.where(ints == e, start_col[e:e + 1, :] * float(MOE_BLK), base)
    dest = base[0:2, :].astype(I32) + ints[2:4, :]
    dest_ref[...] = jnp.concatenate([dest, jnp.zeros((6, ints.shape[1]), I32)], axis=0)

    blk = lax.broadcasted_iota(I32, (LANES, n_blocks_pad), 1).astype(F32)
    e_sub = lax.broadcasted_iota(I32, (LANES, n_blocks_pad), 0)
    done = jnp.where((e_sub < N_EXPERTS) & (end_col <= blk), 1.0, 0.0)
    bexp = jnp.minimum(jnp.sum(done, axis=0, keepdims=True), N_EXPERTS - 1.0)
    used = jnp.sum(nblk_row, axis=-1, keepdims=True)
    row = lax.broadcasted_iota(I32, (8, n_blocks_pad), 0)
    meta = jnp.where(row == 0, bexp, jnp.where(row == 1, used, 0.0))
    meta_ref[...] = meta.astype(I32)


def _plan(ints, counts, n_blocks_pad):
    t = ints.shape[1]
    return pl.pallas_call(
        functools.partial(_plan_kernel, n_blocks_pad=n_blocks_pad),
        out_shape=[jax.ShapeDtypeStruct((8, t), I32), jax.ShapeDtypeStruct((8, n_blocks_pad), I32)],
        compiler_params=pltpu.CompilerParams(vmem_limit_bytes=VMEM_LIMIT),
        name="plan",
    )(ints, counts)


DISPATCH_TM = 512
DMA_UNROLL = 8


def _row_tile(ref, r):
    return ref.at[pl.ds(pl.multiple_of(r * ROW_TILE, ROW_TILE), ROW_TILE)]


def _dispatch_kernel(d0_ref, d1_ref, u2_ref, rows_in_ref, rows_ref, sem):
    del rows_in_ref

    def issue(t, carry):
        src = _row_tile(u2_ref, t)
        pltpu.make_async_copy(src, _row_tile(rows_ref, d0_ref[t]), sem).start(priority=0)
        pltpu.make_async_copy(src, _row_tile(rows_ref, d1_ref[t]), sem).start(priority=1)
        return carry

    lax.fori_loop(0, DISPATCH_TM, issue, 0, unroll=DMA_UNROLL)
    for _ in range(2):
        pltpu.make_async_copy(u2_ref, rows_ref.at[pl.ds(0, DISPATCH_TM * ROW_TILE)], sem).wait()


def _dispatch(dest0, dest1, u2, rows0):
    t = u2.shape[0] // ROW_TILE
    idx = pl.BlockSpec((DISPATCH_TM,), lambda i: (i,), memory_space=pltpu.SMEM)
    return pl.pallas_call(
        _dispatch_kernel,
        grid=(t // DISPATCH_TM,),
        in_specs=[idx, idx,
                  pl.BlockSpec((DISPATCH_TM * ROW_TILE, LANES), lambda i: (i, 0)),
                  pl.BlockSpec(memory_space=pl.ANY)],
        out_specs=pl.BlockSpec(memory_space=pl.ANY),
        out_shape=jax.ShapeDtypeStruct(rows0.shape, F32),
        scratch_shapes=[pltpu.SemaphoreType.DMA(())],
        input_output_aliases={3: 0},
        compiler_params=_cparams(("arbitrary",)),
        name="dispatch",
    )(dest0, dest1, u2, rows0)


def _expert_runs(bexp, used):
    n = bexp.shape[0]
    idx = jnp.arange(n, dtype=I32)
    first = (idx < used[0]) & ((idx == 0) | (bexp != jnp.roll(bexp, 1)))
    slot = (jnp.cumsum(first.astype(I32)) - 1) % 2
    first_at_or_after = lax.cummin(jnp.where(first, idx, n)[::-1])[::-1]
    first_after = jnp.concatenate([first_at_or_after[1:], jnp.full((1,), n, I32)])
    nxt = jnp.where(first_after < n, bexp[jnp.minimum(first_after, n - 1)], -1)
    return first.astype(I32), slot.astype(I32), nxt.astype(I32)


def _experts_kernel(bexp_ref, used_ref, first_ref, slot_ref, nxt_ref, x_ref, w1_hbm, w3_hbm, w2_hbm, y_ref,
                    wb1, wb3, wb2, w1s, w3s, w2s, sem):
    i = pl.program_id(0)
    active = i < used_ref[0]
    slot = slot_ref[i]

    def fetch(e, s):
        return [pltpu.make_async_copy(w.at[e], buf.at[s], sem.at[s])
                for w, buf in ((w1_hbm, wb1), (w3_hbm, wb3), (w2_hbm, wb2))]

    @pl.when(i == 0)
    def _():
        for cp in fetch(bexp_ref[0], 0):
            cp.start()

    @pl.when(first_ref[i] == 1)
    def _():
        @pl.when(nxt_ref[i] >= 0)
        def _():
            for cp in fetch(nxt_ref[i], 1 - slot):
                cp.start()

        for cp in fetch(bexp_ref[i], slot):
            cp.wait()
        w1s[...] = wb1[slot].astype(BF16)
        w3s[...] = wb3[slot].astype(BF16)
        w2s[...] = wb2[slot].astype(BF16)

    @pl.when(active)
    def _():
        def sub_block(r0):
            xb = _load_token_tiles(x_ref, MOE_SUB, r0).astype(BF16)
            yield
            a = _dot(xb, w1s[...])
            b = _dot(xb, w3s[...])
            yield
            hdn = (a * _sigmoid(a) * b).astype(BF16)
            yield
            _store_token_tiles(y_ref, _dot(hdn, w2s[...]), r0)

        _interleave(sub_block(r0) for r0 in range(0, MOE_BLK, MOE_SUB))

    @pl.when(jnp.logical_not(active))
    def _():
        y_ref[...] = jnp.zeros_like(y_ref)


def _experts(bexp, used, rows, w1, w3, w2):
    n_blocks = rows.shape[0] // (MOE_BLK * ROW_TILE)
    first, slot, nxt = _expert_runs(bexp, used)
    any_space = pl.BlockSpec(memory_space=pl.ANY)
    grid_spec = pltpu.PrefetchScalarGridSpec(
        num_scalar_prefetch=5,
        grid=(n_blocks,),
        in_specs=[pl.BlockSpec((MOE_BLK * ROW_TILE, LANES), lambda i, be, nu, *_: (jnp.minimum(i, nu[0] - 1), 0)),
                  any_space, any_space, any_space],
        out_specs=pl.BlockSpec((MOE_BLK * ROW_TILE, LANES), lambda i, *_: (i, 0)),
        scratch_shapes=[pltpu.VMEM((2, D_MODEL, EXPERT_FF), F32), pltpu.VMEM((2, D_MODEL, EXPERT_FF), F32),
                        pltpu.VMEM((2, EXPERT_FF, D_MODEL), F32),
                        pltpu.VMEM((D_MODEL, EXPERT_FF), BF16), pltpu.VMEM((D_MODEL, EXPERT_FF), BF16),
                        pltpu.VMEM((EXPERT_FF, D_MODEL), BF16), pltpu.SemaphoreType.DMA((2,))],
    )
    return pl.pallas_call(
        _experts_kernel,
        grid_spec=grid_spec,
        out_shape=jax.ShapeDtypeStruct(rows.shape, F32),
        compiler_params=_cparams(("arbitrary",)),
        name="experts",
    )(bexp, used, first, slot, nxt, rows, w1, w3, w2)


COMBINE_TM = 256


def _combine_kernel(d0_ref, d1_ref, n0_ref, n1_ref, y_ref, cols_ref, h1_ref, mod_ref, ln_ref, o_ref, ya, yb, sem):
    i = pl.program_id(0)
    slot = i % 2

    def gather(i0_ref, i1_ref, s):
        def issue(t, carry):
            pltpu.make_async_copy(_row_tile(y_ref, i0_ref[t]), _row_tile(ya.at[s], t), sem.at[s]).start(priority=0)
            pltpu.make_async_copy(_row_tile(y_ref, i1_ref[t]), _row_tile(yb.at[s], t), sem.at[s]).start(priority=1)
            return carry

        lax.fori_loop(0, COMBINE_TM, issue, 0, unroll=DMA_UNROLL)

    @pl.when(i == 0)
    def _():
        gather(d0_ref, d1_ref, 0)

    @pl.when(i + 1 < pl.num_programs(0))
    def _():
        gather(n0_ref, n1_ref, 1 - slot)

    for buf in (ya, yb):
        pltpu.make_async_copy(y_ref.at[pl.ds(0, COMBINE_TM * ROW_TILE)], buf.at[slot], sem.at[slot]).wait()

    cols = cols_ref[...]
    moe = (cols[:, 0:1] * _load_token_tiles(ya.at[slot], COMBINE_TM)
           + cols[:, 1:2] * _load_token_tiles(yb.at[slot], COMBINE_TM))
    pre = DEEPNORM_ALPHA * h1_ref[...] + mod_ref[5:6, :] * moe
    o_ref[...] = _ln(pre) * ln_ref[0:1, :] + ln_ref[1:2, :]


def _combine(dest0, dest1, y_rows, cols, h1, mod3, ln2, seq):
    t = h1.shape[0]
    tm = COMBINE_TM
    per_seq = seq // tm
    idx = pl.BlockSpec((tm,), lambda i: (i,), memory_space=pltpu.SMEM)
    idx_next = pl.BlockSpec((tm,), lambda i: (jnp.minimum(i + 1, t // tm - 1),), memory_space=pltpu.SMEM)
    return pl.pallas_call(
        _combine_kernel,
        grid=(t // tm,),
        in_specs=[idx, idx, idx_next, idx_next,
                  pl.BlockSpec(memory_space=pl.ANY),
                  pl.BlockSpec((tm, LANES), lambda i: (i, 0)),
                  pl.BlockSpec((tm, D_MODEL), lambda i: (i, 0)),
                  pl.BlockSpec((None, 6, D_MODEL), lambda i: (i // per_seq, 0, 0)),
                  pl.BlockSpec((2, D_MODEL), lambda i: (0, 0))],
        out_specs=pl.BlockSpec((tm, D_MODEL), lambda i: (i, 0)),
        out_shape=jax.ShapeDtypeStruct((t, D_MODEL), F32),
        scratch_shapes=[pltpu.VMEM((2, tm * ROW_TILE, LANES), F32), pltpu.VMEM((2, tm * ROW_TILE, LANES), F32),
                        pltpu.SemaphoreType.DMA((2,))],
        compiler_params=_cparams(("arbitrary",)),
        name="combine",
    )(dest0, dest1, dest0, dest1, y_rows, cols, h1, mod3, ln2)


def _layer(h, c8, w_ada, b_ada, w_in, w_attn_out, decay_f, decay_b, gn_gain, w_ret_out, w_out,
           ln1_gain, ln1_bias, w_coarse, b_coarse, w_fine, b_fine, w1, w3, w2, ln2_gain, ln2_bias):
    batch, seq, d = h.shape
    t = batch * seq
    x2 = h.reshape(t, d)

    mod = _ada(c8, w_ada, b_ada.reshape(1, -1))
    mod3 = mod[:batch].reshape(batch, 6, d)

    wb = w_in.astype(BF16)
    qkv = _inproj_attn(x2, mod3, wb, seq)
    rest = _inproj_rest(x2, mod3, wb, seq)

    outs, lses = zip(*[_attention(qkv, g, batch, seq) for g in range(ATTN_GROUPS)])
    decays = jnp.concatenate([decay_f, decay_b]).astype(F32)
    retg = _retention(rest, decays, gn_gain.reshape(1, -1), batch, seq)

    w_route = jnp.concatenate([w_coarse, w_fine.transpose(1, 0, 2).reshape(d, N_EXPERTS)], axis=1)
    n_route = w_route.shape[1]
    w_route = jnp.pad(w_route, ((0, 0), (0, LANES - n_route)))
    wr_hi, wr_lo = _split_bf16(w_route)
    rbias = jnp.pad(jnp.concatenate([b_coarse, b_fine.reshape(-1)]), (0, LANES - n_route)).reshape(1, LANES)
    ln1 = jnp.stack([ln1_gain, ln1_bias])
    n_blocks = 2 * t // MOE_BLK + N_EXPERTS
    h1, u2, logits, rows0 = _merge(outs, lses, retg, rest, x2, mod3, w_attn_out.astype(BF16),
                                   w_ret_out.astype(BF16), w_out.astype(BF16), ln1, wr_hi, wr_lo, rbias, seq,
                                   n_blocks * MOE_BLK)

    cols, ints, counts = _route(logits)
    n_blocks_pad = -(-n_blocks // LANES) * LANES
    dest, meta = _plan(ints, counts, n_blocks_pad)
    dest0, dest1 = dest[0], dest[1]
    rows = _dispatch(dest0, dest1, u2, rows0)
    y_rows = _experts(meta[0, :n_blocks], meta[1, :1], rows, w1, w3, w2)
    out = _combine(dest0, dest1, y_rows, cols, h1, mod3, jnp.stack([ln2_gain, ln2_bias]), seq)
    return out.reshape(batch, seq, d)


def kernel(x, c, w_ada, b_ada, w_in, w_attn_out, ret_decay_fwd, ret_decay_bwd, ret_gn_gain, w_ret_out, w_out,
           ln1_gain, ln1_bias, w_coarse, b_coarse, w_fine, b_fine, w1, w3, w2, ln2_gain, ln2_bias):
    batch = x.shape[0]
    assert batch <= 8 and x.shape[1] % (2 * INPROJ_TM) == 0 and x.shape[2] == D_MODEL
    c8 = jnp.pad(c, ((0, 8 - batch), (0, 0)))
    h = x
    for l in range(w_ada.shape[0]):
        h = _layer(h, c8, w_ada[l], b_ada[l], w_in[l], w_attn_out[l], ret_decay_fwd[l], ret_decay_bwd[l],
                   ret_gn_gain[l], w_ret_out[l], w_out[l], ln1_gain[l], ln1_bias[l], w_coarse[l], b_coarse[l],
                   w_fine[l], b_fine[l], w1[l], w3[l], w2[l], ln2_gain[l], ln2_bias[l])
    return h
```

```python
import functools
import math

import jax
import jax.numpy as jnp
from jax import lax
from jax.experimental import pallas as pl
from jax.experimental.pallas import tpu as pltpu

F32 = jnp.float32
BF16 = jnp.bfloat16
I32 = jnp.int32

D_MODEL = 1024
ATTN_GROUPS = 3
HEADS_PER_GROUP = 4
HEAD_DIM = 128
ATTN_HEADS = ATTN_GROUPS * HEADS_PER_GROUP
GROUP_WIDTH = HEADS_PER_GROUP * HEAD_DIM
ATTN_PATTERNS = ((128, 1), (512, 4), (2048, 16))
ALIBI_MAX_EXP = 8.0
NEG = -1e30
RET_HEADS = 4
RET_QK = 256
RET_V = 512
N_GROUPS = 4
EXPERTS_PER_GROUP = 8
N_EXPERTS = N_GROUPS * EXPERTS_PER_GROUP
EXPERT_FF = 512
DEPTH = 1
DEEPNORM_ALPHA = (2.0 * DEPTH) ** 0.25
LN_EPS = 1e-5

LANES = 128
PERM_TILE = 512
ATTN_QB = 128
ATTN_HALF = 64
ATTN_KB = ATTN_QB + 2 * ATTN_HALF
RET_CHUNK = 256
RET_STEP = 1024
MOE_BLK = 512
MOE_SUB = 256
VMEM_LIMIT = 56 * 1024 * 1024


def _cparams(sem):
    return pltpu.CompilerParams(dimension_semantics=sem, vmem_limit_bytes=VMEM_LIMIT)


def _split_bf16(a):
    hi = a.astype(BF16)
    lo = (a - hi.astype(F32)).astype(BF16)
    return hi, lo


def _dot(a, b):
    return jnp.dot(a, b, preferred_element_type=F32)


def _dot3(a, b):
    ah, al = _split_bf16(a)
    bh, bl = _split_bf16(b)
    return _dot(ah, bh) + _dot(ah, bl) + _dot(al, bh)


def _ln(x):
    mu = jnp.mean(x, axis=-1, keepdims=True)
    xc = x - mu
    var = jnp.mean(xc * xc, axis=-1, keepdims=True)
    return xc * lax.rsqrt(var + LN_EPS)


def _sigmoid(x):
    return 1.0 / (1.0 + jnp.exp(-x))


ROW_TILE = D_MODEL // LANES


def _store_token_tiles(ref, val, row0=0):
    n = val.shape[0]
    for cc in range(ROW_TILE):
        ref[pl.ds(row0 * ROW_TILE + cc, n, stride=ROW_TILE), :] = val[:, cc * LANES:(cc + 1) * LANES]


def _load_token_tiles(ref, n, row0=0):
    return jnp.concatenate([ref[pl.ds(row0 * ROW_TILE + cc, n, stride=ROW_TILE), :] for cc in range(ROW_TILE)],
                           axis=1)


def _interleave(phased):
    pending = []
    for gen in phased:
        pending.append(gen)
        pending = [g for g in pending if next(g, StopIteration) is not StopIteration]
    while pending:
        pending = [g for g in pending if next(g, StopIteration) is not StopIteration]


def _ada_kernel(c_ref, w_ref, b_ref, o_ref):
    o_ref[...] = _dot3(c_ref[...], w_ref[...]) + b_ref[...]


def _ada(c8, w_ada, b_ada):
    n = w_ada.shape[1]
    return pl.pallas_call(
        _ada_kernel,
        grid=(n // D_MODEL,),
        in_specs=[pl.BlockSpec((8, D_MODEL), lambda j: (0, 0)),
                  pl.BlockSpec((D_MODEL, D_MODEL), lambda j: (0, j)),
                  pl.BlockSpec((1, D_MODEL), lambda j: (0, j))],
        out_specs=pl.BlockSpec((8, D_MODEL), lambda j: (0, j)),
        out_shape=jax.ShapeDtypeStruct((8, n), F32),
        compiler_params=_cparams(("arbitrary",)),
        name="ada",
    )(c8, w_ada, b_ada)


INPROJ_TM = 1024
LN_ROWS = 256
LN_UNROLL = 4


def _modulated_ln_rows(x_ref, mod_ref, emit):
    gain = 1.0 + mod_ref[1:2, :]
    shift = mod_ref[0:1, :]

    def chunk(c, carry):
        r0 = pl.multiple_of(c * LN_ROWS, LN_ROWS)
        emit(r0, _ln(x_ref[pl.ds(r0, LN_ROWS), :]) * gain + shift)
        return carry

    lax.fori_loop(0, INPROJ_TM // LN_ROWS, chunk, 0, unroll=LN_UNROLL)


def _inproj_attn_kernel(x_ref, mod_ref, wq_ref, wk_ref, wv_ref, o_ref, un_ref, uf_scr, u_scr):
    j = pl.program_id(1)

    @pl.when(j == 0)
    def _():
        def emit(r0, u):
            for cc in range(D_MODEL // LANES):
                uf_scr[cc, pl.ds(r0, LN_ROWS), :] = u[:, cc * LANES:(cc + 1) * LANES]
            ub = u.astype(BF16)
            u_scr[0, pl.ds(r0, LN_ROWS), :] = ub
            un_ref[pl.ds(r0, LN_ROWS), :] = ub

        _modulated_ln_rows(x_ref, mod_ref, emit)
        for g in (1, 2):
            dil = ATTN_PATTERNS[g][1]
            n = PERM_TILE // dil
            for t0 in range(0, INPROJ_TM, PERM_TILE):
                for res in range(dil):
                    for cc in range(D_MODEL // LANES):
                        rows = uf_scr[cc, pl.ds(t0 + res, n, stride=dil), :]
                        u_scr[g, t0 + res * n:t0 + (res + 1) * n, cc * LANES:(cc + 1) * LANES] = rows.astype(BF16)

    u = u_scr[j]
    for s, w_ref in enumerate((wq_ref, wk_ref, wv_ref)):
        o_ref[:, s * GROUP_WIDTH:(s + 1) * GROUP_WIDTH] = _dot(u, w_ref[...]).astype(BF16)


ATTN_WIDTH = ATTN_HEADS * HEAD_DIM
W_RQ0 = 3 * ATTN_WIDTH
W_RK0 = W_RQ0 + RET_HEADS * RET_QK
W_RV0 = W_RK0 + RET_HEADS * RET_QK
W_TAIL0 = W_RV0 + RET_HEADS * RET_V


def _inproj_attn(x2, mod3, wb, seq):
    t = x2.shape[0]
    tn = 3 * GROUP_WIDTH
    per_seq = seq // INPROJ_TM
    groups_per_range = ATTN_WIDTH // GROUP_WIDTH
    wspec = lambda s: pl.BlockSpec((D_MODEL, GROUP_WIDTH), lambda i, j: (0, s * groups_per_range + j))
    return pl.pallas_call(
        _inproj_attn_kernel,
        grid=(t // INPROJ_TM, ATTN_GROUPS),
        in_specs=[pl.BlockSpec((INPROJ_TM, D_MODEL), lambda i, j: (i, 0)),
                  pl.BlockSpec((None, 6, D_MODEL), lambda i, j: (i // per_seq, 0, 0)),
                  wspec(0), wspec(1), wspec(2)],
        out_specs=[pl.BlockSpec((INPROJ_TM, tn), lambda i, j: (i, j)),
                   pl.BlockSpec((INPROJ_TM, D_MODEL), lambda i, j: (i, 0))],
        out_shape=[jax.ShapeDtypeStruct((t, ATTN_GROUPS * tn), BF16),
                   jax.ShapeDtypeStruct((t, D_MODEL), BF16)],
        scratch_shapes=[pltpu.VMEM((D_MODEL // LANES, INPROJ_TM, LANES), F32),
                        pltpu.VMEM((ATTN_GROUPS, INPROJ_TM, D_MODEL), BF16)],
        compiler_params=_cparams(("arbitrary", "arbitrary")),
        name="inproj_attn",
    )(x2, mod3, wb, wb, wb)


REST_TN = RET_HEAD_COLS = 2 * RET_QK + RET_V
REST_TAIL_TILES = (RET_HEADS * RET_V + 2 * D_MODEL) // REST_TN


def _inproj_rest_kernel(u_ref, wq_ref, wk_ref, wv_ref, wt0_ref, wt1_ref, o_ref):
    j = pl.program_id(0)

    def project(parts):
        u = u_ref[...]
        c0 = 0
        for w_ref in parts:
            wd = w_ref.shape[1]
            o_ref[:, c0:c0 + wd] = _dot(u, w_ref[...]).astype(BF16)
            c0 += wd

    @pl.when(j < RET_HEADS)
    def _():
        project((wq_ref, wk_ref, wv_ref))

    @pl.when(j >= RET_HEADS)
    def _():
        project((wt0_ref, wt1_ref))


def _inproj_rest(u, wb):
    t = u.shape[0]
    half = REST_TN // 2
    assert W_TAIL0 % half == 0 and W_RV0 % RET_V == 0
    head = lambda j: jnp.minimum(j, RET_HEADS - 1)
    tail = lambda j: jnp.maximum(j - RET_HEADS, 0)
    return pl.pallas_call(
        _inproj_rest_kernel,
        grid=(RET_HEADS + REST_TAIL_TILES, t // INPROJ_TM),
        in_specs=[pl.BlockSpec((INPROJ_TM, D_MODEL), lambda j, i: (i, 0)),
                  pl.BlockSpec((D_MODEL, RET_QK), lambda j, i: (0, W_RQ0 // RET_QK + head(j))),
                  pl.BlockSpec((D_MODEL, RET_QK), lambda j, i: (0, W_RK0 // RET_QK + head(j))),
                  pl.BlockSpec((D_MODEL, RET_V), lambda j, i: (0, W_RV0 // RET_V + head(j))),
                  pl.BlockSpec((D_MODEL, half), lambda j, i: (0, W_TAIL0 // half + 2 * tail(j))),
                  pl.BlockSpec((D_MODEL, half), lambda j, i: (0, W_TAIL0 // half + 2 * tail(j) + 1))],
        out_specs=pl.BlockSpec((INPROJ_TM, REST_TN), lambda j, i: (i, j)),
        out_shape=jax.ShapeDtypeStruct((t, (RET_HEADS + REST_TAIL_TILES) * REST_TN), BF16),
        compiler_params=_cparams(("arbitrary", "arbitrary")),
        name="inproj_rest",
    )(u, wb, wb, wb, wb, wb)


ATTN_OFFSETS = ATTN_KB // ATTN_HALF - 1
ATTN_UNROLL = 4
ATTN_MIN_BLOCKS = 4


def _attn_kernel(q_ref, k_ref, v_ref, o_ref, lse_ref, bias_scr, *scratch, group, dil, nt, n, nres):
    n_sub = nt * n
    if nt == 1:
        seqs = [(q_ref.at[0, rr], k_ref.at[0, rr], v_ref.at[0, rr], o_ref.at[0, rr], lse_ref.at[0, rr])
                for rr in range(nres)]
    else:
        seqs = [tuple(s.at[rr] for s in scratch) for rr in range(nres)]
        for rr, (qs, ks, vs, _, _) in enumerate(seqs):
            for t in range(nt):
                qs[t * n:(t + 1) * n, :] = q_ref[t, rr]
                ks[t * n:(t + 1) * n, :] = k_ref[t, rr]
                vs[t * n:(t + 1) * n, :] = v_ref[t, rr]

    @pl.when((pl.program_id(0) == 0) & (pl.program_id(1) == 0))
    def _():
        base = (lax.broadcasted_iota(I32, (ATTN_QB, ATTN_KB), 1)
                - lax.broadcasted_iota(I32, (ATTN_QB, ATTN_KB), 0))
        for j in range(ATTN_OFFSETS):
            dist = jnp.abs(base - j * ATTN_HALF)
            for hh in range(HEADS_PER_GROUP):
                slope = dil * 2.0 ** (-ALIBI_MAX_EXP * (group * HEADS_PER_GROUP + hh + 1) / ATTN_HEADS)
                bias_scr[hh * ATTN_OFFSETS + j] = jnp.where(dist <= ATTN_HALF, -slope * dist.astype(F32), NEG)

    scale = HEAD_DIM ** -0.5
    lane = lax.broadcasted_iota(I32, (ATTN_QB, LANES), 1)

    def block(blk, carry):
        q0 = pl.multiple_of(blk * ATTN_QB, ATTN_QB)
        start = pl.multiple_of(jnp.clip(q0 - ATTN_HALF, 0, n_sub - ATTN_KB), ATTN_HALF)
        j = (q0 - start) // ATTN_HALF
        for qs, ks, vs, os_, ls in seqs:
            lse_tile = jnp.zeros((ATTN_QB, LANES), F32)
            for hh in range(HEADS_PER_GROUP):
                cs = slice(hh * HEAD_DIM, (hh + 1) * HEAD_DIM)
                qb = qs[pl.ds(q0, ATTN_QB), cs]
                kb = ks[pl.ds(start, ATTN_KB), cs]
                vb = vs[pl.ds(start, ATTN_KB), cs]
                s = lax.dot_general(qb, kb, (((1,), (1,)), ((), ())), preferred_element_type=F32) * scale
                s = s + bias_scr[hh * ATTN_OFFSETS + j]
                m = jnp.max(s, axis=-1, keepdims=True)
                p = jnp.exp(s - m)
                l = jnp.sum(p, axis=-1, keepdims=True)
                o = _dot(p.astype(BF16), vb) * (1.0 / l)
                os_[pl.ds(q0, ATTN_QB), cs] = o.astype(BF16)
                lse_tile = jnp.where(lane == hh, m + jnp.log(l), lse_tile)
            ls[pl.ds(q0, ATTN_QB), :] = lse_tile
        return carry

    n_blk = n_sub // ATTN_QB
    lax.fori_loop(0, n_blk, block, 0, unroll=min(max(ATTN_UNROLL // nres, 1), n_blk))

    if nt > 1:
        for rr, (_, _, _, os_, ls) in enumerate(seqs):
            for t in range(nt):
                o_ref[t, rr] = os_[t * n:(t + 1) * n, :]
                lse_ref[t, rr] = ls[t * n:(t + 1) * n, :]


def _attention(qkv, group, batch, seq):
    dil = ATTN_PATTERNS[group][1]
    if dil == 1:
        nt, n = 1, seq
    else:
        nt, n = seq // PERM_TILE, PERM_TILE // dil
    n_sub = nt * n
    t = batch * seq
    qkv5 = qkv.reshape(batch, nt, dil, n, qkv.shape[1])
    cb = group * 3
    nres = min(dil, max(1, ATTN_MIN_BLOCKS * ATTN_QB // n_sub))
    blk = (None, nt, nres, n, GROUP_WIDTH)
    scratch = [pltpu.VMEM((HEADS_PER_GROUP * ATTN_OFFSETS, ATTN_QB, ATTN_KB), F32)]
    if nt > 1:
        scratch += [pltpu.VMEM((nres, n_sub, GROUP_WIDTH), BF16)] * 4 + [pltpu.VMEM((nres, n_sub, LANES), F32)]
    out, lse = pl.pallas_call(
        functools.partial(_attn_kernel, group=group, dil=dil, nt=nt, n=n, nres=nres),
        grid=(batch, dil // nres),
        in_specs=[pl.BlockSpec(blk, lambda b, r: (b, 0, r, 0, cb)),
                  pl.BlockSpec(blk, lambda b, r: (b, 0, r, 0, cb + 1)),
                  pl.BlockSpec(blk, lambda b, r: (b, 0, r, 0, cb + 2))],
        out_specs=[pl.BlockSpec(blk, lambda b, r: (b, 0, r, 0, 0)),
                   pl.BlockSpec((None, nt, nres, n, LANES), lambda b, r: (b, 0, r, 0, 0))],
        out_shape=[jax.ShapeDtypeStruct((batch, nt, dil, n, GROUP_WIDTH), BF16),
                   jax.ShapeDtypeStruct((batch, nt, dil, n, LANES), F32)],
        scratch_shapes=scratch,
        compiler_params=_cparams(("arbitrary", "arbitrary")),
        name=f"attn_g{group}",
    )(qkv5, qkv5, qkv5)
    return out.reshape(t, GROUP_WIDTH), lse.reshape(t, LANES)


def _log_sigmoid(x):
    return jnp.minimum(x, 0.0) - jnp.log(1.0 + jnp.exp(-jnp.abs(x)))


def _retention_kernel(decay_ref, qkv_ref, g_ref, gain_ref, o_ref, state, ybwd, dmat, kdec, qdec, *, nc):
    c = RET_CHUNK
    h = pl.program_id(1)
    i = pl.program_id(2)
    kscale = RET_QK ** -0.5

    def set_decays(lg, forward):
        row = lax.broadcasted_iota(I32, (c, c), 0)
        col = lax.broadcasted_iota(I32, (c, c), 1)
        pos = lax.broadcasted_iota(I32, (c, LANES), 0).astype(F32)
        if forward:
            gap, key_pow, query_pow = row - col, (c - 1.0) - pos, pos + 1.0
            keep = gap >= 0
        else:
            gap, key_pow, query_pow = col - row, pos, c - pos
            keep = gap > 0
        dmat[...] = jnp.where(keep, jnp.exp(lg * jnp.maximum(gap, 0).astype(F32)) * kscale, 0.0)
        kdec[...] = jnp.exp(lg * key_pow) * kscale
        qdec[...] = jnp.exp(lg * query_pow)

    def chunk_step(lg, r0, finish):
        q = qkv_ref[r0:r0 + c, 0:RET_QK]
        k = qkv_ref[r0:r0 + c, RET_QK:2 * RET_QK]
        v = qkv_ref[r0:r0 + c, 2 * RET_QK:]
        inner = lax.dot_general(q, k, (((1,), (1,)), ((), ())), preferred_element_type=F32) * dmat[...]
        y = _dot(inner.astype(BF16), v)
        kd = (k.astype(F32) * jnp.concatenate([kdec[...]] * (RET_QK // LANES), axis=1)).astype(BF16)
        kv = lax.dot_general(kd, v, (((0,), (0,)), ((), ())), preferred_element_type=F32)
        yield
        qd = jnp.concatenate([qdec[...]] * (RET_V // LANES), axis=1)
        y = y + _dot(q, state[...].astype(BF16)) * qd
        state[...] = state[...] * jnp.exp(lg * float(c)) + kv
        yield
        finish(r0, y)

    @pl.when((i == 0) | (i == nc))
    def _():
        state[...] = jnp.zeros_like(state)

    @pl.when(i < nc)
    def _():
        lg = _log_sigmoid(jnp.zeros((1, 1), F32) + decay_ref[RET_HEADS + h])

        @pl.when(i == 0)
        def _():
            set_decays(lg, False)

        base = pl.multiple_of((nc - 1 - i) * RET_STEP, RET_STEP)

        def finish(r0, y):
            ybwd[pl.ds(base + r0, c), :] = y

        _interleave(chunk_step(lg, r0, finish) for r0 in reversed(range(0, RET_STEP, c)))

    @pl.when(i >= nc)
    def _():
        lg = _log_sigmoid(jnp.zeros((1, 1), F32) + decay_ref[h])

        @pl.when(i == nc)
        def _():
            set_decays(lg, True)

        base = pl.multiple_of((i - nc) * RET_STEP, RET_STEP)

        def finish(r0, y):
            y = y + ybwd[pl.ds(base + r0, c), :]
            g = g_ref[r0:r0 + c, :].astype(F32)
            o_ref[r0:r0 + c, :] = (g * _sigmoid(g) * (_ln(y) * gain_ref[...])).astype(BF16)

        _interleave(chunk_step(lg, r0, finish) for r0 in range(0, RET_STEP, c))


def _retention(rest, decays, gn_gain, batch, seq):
    c = RET_STEP
    nc = seq // c
    rest3 = rest.reshape(batch, seq, rest.shape[1])
    gate_blk = RET_HEADS * RET_HEAD_COLS // RET_V

    def chunk(i):
        return jnp.where(i < nc, nc - 1 - i, i - nc)

    grid_spec = pltpu.PrefetchScalarGridSpec(
        num_scalar_prefetch=1,
        grid=(batch, RET_HEADS, 2 * nc),
        in_specs=[pl.BlockSpec((None, c, RET_HEAD_COLS), lambda b, h, i, d: (b, chunk(i), h)),
                  pl.BlockSpec((None, c, RET_V), lambda b, h, i, d: (b, jnp.maximum(i - nc, 0), gate_blk + h)),
                  pl.BlockSpec((1, RET_V), lambda b, h, i, d: (0, h))],
        out_specs=pl.BlockSpec((None, c, RET_V), lambda b, h, i, d: (b, jnp.maximum(i - nc, 0), h)),
        scratch_shapes=[pltpu.VMEM((RET_QK, RET_V), F32), pltpu.VMEM((seq, RET_V), F32),
                        pltpu.VMEM((RET_CHUNK, RET_CHUNK), F32), pltpu.VMEM((RET_CHUNK, LANES), F32),
                        pltpu.VMEM((RET_CHUNK, LANES), F32)],
    )
    out = pl.pallas_call(
        functools.partial(_retention_kernel, nc=nc),
        grid_spec=grid_spec,
        out_shape=jax.ShapeDtypeStruct((batch, seq, RET_HEADS * RET_V), BF16),
        compiler_params=_cparams(("arbitrary", "arbitrary", "arbitrary")),
        name="retention",
    )(decays, rest3, rest3, gn_gain)
    return out.reshape(batch * seq, RET_HEADS * RET_V)


MERGE_TM = PERM_TILE
MERGE_SUB = 256
ZERO_SPLIT = 4


def _merge_kernel(o0_ref, o1_ref, o2_ref, l0_ref, l1_ref, l2_ref, retg_ref, ga_ref, gr_ref, x_ref, mod_ref,
                  wa_ref, wr_ref, wo_ref, ln_ref, wrh_ref, wrl_ref, rb_ref,
                  h1_ref, u2_ref, lg_ref, zrows_ref, on_scr, ln_scr, zbuf, zsem):
    i = pl.program_id(0)

    @pl.when(i == 0)
    def _():
        zbuf[...] = jnp.zeros_like(zbuf)

    zrows = zbuf.shape[0]
    zero_copies = [pltpu.make_async_copy(zbuf, zrows_ref.at[pl.ds((i * ZERO_SPLIT + k) * zrows, zrows)], zsem)
                   for k in range(ZERO_SPLIT)]
    for cp in zero_copies:
        cp.start()

    for g in (1, 2):
        dil = ATTN_PATTERNS[g][1]
        n = MERGE_TM // dil
        o_ref, l_ref = ((o1_ref, l1_ref), (o2_ref, l2_ref))[g - 1]
        for res in range(dil):
            rows = o_ref[res * n:(res + 1) * n, :].astype(F32)
            for hh in range(HEADS_PER_GROUP):
                on_scr[g - 1, hh, pl.ds(res, n, stride=dil), :] = rows[:, hh * HEAD_DIM:(hh + 1) * HEAD_DIM]
            ln_scr[g - 1, pl.ds(res, n, stride=dil), :] = l_ref[res * n:(res + 1) * n, :]

    def sub_tile(r0):
        rs = slice(r0, r0 + MERGE_SUB)
        l0, l1, l2 = l0_ref[rs, :], ln_scr[0, rs, :], ln_scr[1, rs, :]
        lm = jnp.maximum(jnp.maximum(l0, l1), l2)
        e0, e1, e2 = jnp.exp(l0 - lm), jnp.exp(l1 - lm), jnp.exp(l2 - lm)
        inv = 1.0 / (e0 + e1 + e2)
        parts = []
        for hh in range(HEADS_PER_GROUP):
            sl = slice(hh * HEAD_DIM, (hh + 1) * HEAD_DIM)
            acc = (e0[:, hh:hh + 1] * o0_ref[rs, sl].astype(F32)
                   + e1[:, hh:hh + 1] * on_scr[0, hh, rs, :]
                   + e2[:, hh:hh + 1] * on_scr[1, hh, rs, :])
            parts.append((acc * inv[:, hh:hh + 1]).astype(BF16))
        attn = jnp.concatenate(parts, axis=1)
        yield

        branch_a = _dot(attn, wa_ref[...])
        branch_r = _dot(retg_ref[rs, :], wr_ref[...])
        yield
        merged = (_sigmoid(ga_ref[rs, :].astype(F32)) * branch_a
                  + _sigmoid(gr_ref[rs, :].astype(F32)) * branch_r)
        yield
        y = _dot(merged.astype(BF16), wo_ref[...])
        yield

        h1 = _ln(DEEPNORM_ALPHA * x_ref[rs, :] + mod_ref[2:3, :] * y) * ln_ref[0:1, :] + ln_ref[1:2, :]
        h1_ref[rs, :] = h1
        u2 = _ln(h1) * (1.0 + mod_ref[4:5, :]) + mod_ref[3:4, :]
        _store_token_tiles(u2_ref, u2, r0)
        uh, ul = _split_bf16(u2)
        yield
        lg_ref[rs, :] = (_dot(uh, wrh_ref[...]) + _dot(uh, wrl_ref[...]) + _dot(ul, wrh_ref[...])
                         + rb_ref[...])

    _interleave(sub_tile(r0) for r0 in range(0, MERGE_TM, MERGE_SUB))

    for cp in zero_copies:
        cp.wait()


def _merge(outs, lses, retg, rest, x2, mod3, wa, wr, wo, ln1, wr_hi, wr_lo, rbias, seq, n_rows):
    t = x2.shape[0]
    tm = MERGE_TM
    per_seq = seq // tm
    zrows, rem = divmod(n_rows * ROW_TILE, (t // tm) * ZERO_SPLIT)
    assert rem == 0 and zrows % 8 == 0
    row = lambda w: pl.BlockSpec((tm, w), lambda i: (i, 0))
    full = lambda a: pl.BlockSpec(a.shape, lambda i: (0,) * a.ndim)
    return pl.pallas_call(
        _merge_kernel,
        grid=(t // tm,),
        in_specs=[row(GROUP_WIDTH)] * 3 + [row(LANES)] * 3 + [
            row(RET_HEADS * RET_V),
            pl.BlockSpec((tm, D_MODEL), lambda i: (i, 6)),
            pl.BlockSpec((tm, D_MODEL), lambda i: (i, 7)),
            row(D_MODEL),
            pl.BlockSpec((None, 6, D_MODEL), lambda i: (i // per_seq, 0, 0)),
            full(wa), full(wr), full(wo), full(ln1), full(wr_hi), full(wr_lo), full(rbias)],
        out_specs=[row(D_MODEL), pl.BlockSpec((tm * ROW_TILE, LANES), lambda i: (i, 0)), row(LANES),
                   pl.BlockSpec(memory_space=pl.ANY)],
        out_shape=[jax.ShapeDtypeStruct((t, D_MODEL), F32),
                   jax.ShapeDtypeStruct((t * ROW_TILE, LANES), F32),
                   jax.ShapeDtypeStruct((t, LANES), F32),
                   jax.ShapeDtypeStruct((n_rows * ROW_TILE, LANES), F32)],
        scratch_shapes=[pltpu.VMEM((2, HEADS_PER_GROUP, tm, HEAD_DIM), F32), pltpu.VMEM((2, tm, LANES), F32),
                        pltpu.VMEM((zrows, LANES), F32), pltpu.SemaphoreType.DMA(())],
        compiler_params=_cparams(("arbitrary",)),
        name="merge",
    )(*outs, *lses, retg, rest, rest, x2, mod3, wa, wr, wo, ln1, wr_hi, wr_lo, rbias)


ROUTE_TM = 512
BIG = 1 << 20


def _route_kernel(lg_ref, cols_ref, ints_ref, cnt_ref, carry):
    i = pl.program_id(0)

    @pl.when(i == 0)
    def _():
        carry[...] = jnp.zeros_like(carry)

    tm = ROUTE_TM
    lg = lg_ref[...]
    lane = lax.broadcasted_iota(I32, (tm, LANES), 1)
    lane_f = lane.astype(F32)
    first = lambda mask: jnp.min(jnp.where(mask, lane_f, float(BIG)), axis=-1, keepdims=True).astype(I32)

    coarse = jnp.where(lane < N_GROUPS, lg, NEG)
    cmax = jnp.max(coarse, axis=-1, keepdims=True)
    gsel = first(coarse == cmax)
    p_group = 1.0 / jnp.sum(jnp.exp(coarse - cmax), axis=-1, keepdims=True)

    lo = N_GROUPS + EXPERTS_PER_GROUP * gsel
    fine = jnp.where((lane >= lo) & (lane < lo + EXPERTS_PER_GROUP), lg, NEG)
    v1 = jnp.max(fine, axis=-1, keepdims=True)
    i1 = first(fine == v1)
    fine2 = jnp.where(lane == i1, NEG, fine)
    v2 = jnp.max(fine2, axis=-1, keepdims=True)
    i2 = first(fine2 == v2)
    ex = jnp.exp(v2 - v1)
    den = 1.0 / (1.0 + ex)
    gate1 = p_group * den
    gate2 = p_group * (ex * den)
    e1 = i1 - N_GROUPS
    e2 = i2 - N_GROUPS

    oh1 = lane == e1
    oh2 = lane == e2
    cnt = jnp.where(oh1 | oh2, 1.0, 0.0)
    r_i = lax.broadcasted_iota(I32, (tm, tm), 0)
    c_i = lax.broadcasted_iota(I32, (tm, tm), 1)
    tri = jnp.where(r_i > c_i, 1.0, 0.0).astype(BF16)
    rank = _dot(tri, cnt.astype(BF16)) + carry[...]
    r1 = jnp.sum(jnp.where(oh1, rank, 0.0), axis=-1, keepdims=True)
    r2 = jnp.sum(jnp.where(oh2, rank, 0.0), axis=-1, keepdims=True)
    carry[...] = carry[...] + jnp.sum(cnt, axis=0, keepdims=True)
    cnt_ref[...] = jnp.broadcast_to(carry[...], cnt_ref.shape)

    cols_ref[...] = jnp.where(lane == 0, gate1, jnp.where(lane == 1, gate2, 0.0))
    packed = jnp.where(lane == 0, e1.astype(F32),
                       jnp.where(lane == 1, e2.astype(F32),
                                 jnp.where(lane == 2, r1, jnp.where(lane == 3, r2, 0.0))))
    ints_ref[...] = packed.T[0:8, :].astype(I32)


def _route(logits):
    t = logits.shape[0]
    tm = ROUTE_TM
    return pl.pallas_call(
        _route_kernel,
        grid=(t // tm,),
        in_specs=[pl.BlockSpec((tm, LANES), lambda i: (i, 0))],
        out_specs=[pl.BlockSpec((tm, LANES), lambda i: (i, 0)),
                   pl.BlockSpec((8, tm), lambda i: (0, i)),
                   pl.BlockSpec((8, LANES), lambda i: (0, 0))],
        out_shape=[jax.ShapeDtypeStruct((t, LANES), F32),
                   jax.ShapeDtypeStruct((8, t), I32),
                   jax.ShapeDtypeStruct((8, LANES), F32)],
        scratch_shapes=[pltpu.VMEM((1, LANES), F32)],
        compiler_params=_cparams(("arbitrary",)),
        name="route",
    )(logits)


def _plan_kernel(ints_ref, cnt_ref, dest_ref, meta_ref, *, n_blocks_pad):
    sub = lax.broadcasted_iota(I32, (LANES, LANES), 0)
    lane = lax.broadcasted_iota(I32, (LANES, LANES), 1)
    cnt = cnt_ref[0:1, :]
    nblk_row = jnp.floor((cnt + (MOE_BLK - 1.0)) * (1.0 / MOE_BLK))
    nblk_mat = jnp.broadcast_to(nblk_row, (LANES, LANES))
    start_col = jnp.sum(jnp.where(lane < sub, nblk_mat, 0.0), axis=-1, keepdims=True)
    nblk_col = jnp.sum(jnp.where(lane == sub, nblk_mat, 0.0), axis=-1, keepdims=True)
    end_col = start_col + nblk_col

    ints = ints_ref[...]
    base = jnp.zeros(ints.shape, F32)
    for e in range(N_EXPERTS):
        base = jnp.where(ints == e, start_col[e:e + 1, :] * float(MOE_BLK), base)
    dest = base[0:2, :].astype(I32) + ints[2:4, :]
    dest_ref[...] = jnp.concatenate([dest, jnp.zeros((6, ints.shape[1]), I32)], axis=0)

    blk = lax.broadcasted_iota(I32, (LANES, n_blocks_pad), 1).astype(F32)
    e_sub = lax.broadcasted_iota(I32, (LANES, n_blocks_pad), 0)
    done = jnp.where((e_sub < N_EXPERTS) & (end_col <= blk), 1.0, 0.0)
    bexp = jnp.minimum(jnp.sum(done, axis=0, keepdims=True), N_EXPERTS - 1.0)
    used = jnp.sum(nblk_row, axis=-1, keepdims=True)
    row = lax.broadcasted_iota(I32, (8, n_blocks_pad), 0)
    meta = jnp.where(row == 0, bexp, jnp.where(row == 1, used, 0.0))
    meta_ref[...] = meta.astype(I32)


def _plan(ints, counts, n_blocks_pad):
    t = ints.shape[1]
    return pl.pallas_call(
        functools.partial(_plan_kernel, n_blocks_pad=n_blocks_pad),
        out_shape=[jax.ShapeDtypeStruct((8, t), I32), jax.ShapeDtypeStruct((8, n_blocks_pad), I32)],
        compiler_params=pltpu.CompilerParams(vmem_limit_bytes=VMEM_LIMIT),
        name="plan",
    )(ints, counts)


DISPATCH_TM = 512
DMA_UNROLL = 8


def _row_tile(ref, r):
    return ref.at[pl.ds(pl.multiple_of(r * ROW_TILE, ROW_TILE), ROW_TILE)]


def _dispatch_kernel(d0_ref, d1_ref, u2_ref, rows_in_ref, rows_ref, sem):
    del rows_in_ref

    def issue(t, carry):
        src = _row_tile(u2_ref, t)
        pltpu.make_async_copy(src, _row_tile(rows_ref, d0_ref[t]), sem).start(priority=0)
        pltpu.make_async_copy(src, _row_tile(rows_ref, d1_ref[t]), sem).start(priority=1)
        return carry

    lax.fori_loop(0, DISPATCH_TM, issue, 0, unroll=DMA_UNROLL)
    for _ in range(2):
        pltpu.make_async_copy(u2_ref, rows_ref.at[pl.ds(0, DISPATCH_TM * ROW_TILE)], sem).wait()


def _dispatch(dest0, dest1, u2, rows0):
    t = u2.shape[0] // ROW_TILE
    idx = pl.BlockSpec((DISPATCH_TM,), lambda i: (i,), memory_space=pltpu.SMEM)
    return pl.pallas_call(
        _dispatch_kernel,
        grid=(t // DISPATCH_TM,),
        in_specs=[idx, idx,
                  pl.BlockSpec((DISPATCH_TM * ROW_TILE, LANES), lambda i: (i, 0)),
                  pl.BlockSpec(memory_space=pl.ANY)],
        out_specs=pl.BlockSpec(memory_space=pl.ANY),
        out_shape=jax.ShapeDtypeStruct(rows0.shape, F32),
        scratch_shapes=[pltpu.SemaphoreType.DMA(())],
        input_output_aliases={3: 0},
        compiler_params=_cparams(("arbitrary",)),
        name="dispatch",
    )(dest0, dest1, u2, rows0)


def _expert_runs(bexp, used):
    n = bexp.shape[0]
    idx = jnp.arange(n, dtype=I32)
    first = (idx < used[0]) & ((idx == 0) | (bexp != jnp.roll(bexp, 1)))
    slot = (jnp.cumsum(first.astype(I32)) - 1) % 2
    first_at_or_after = lax.cummin(jnp.where(first, idx, n)[::-1])[::-1]
    first_after = jnp.concatenate([first_at_or_after[1:], jnp.full((1,), n, I32)])
    nxt = jnp.where(first_after < n, bexp[jnp.minimum(first_after, n - 1)], -1)
    return first.astype(I32), slot.astype(I32), nxt.astype(I32)


def _experts_kernel(bexp_ref, used_ref, first_ref, slot_ref, nxt_ref, x_ref, w1_hbm, w3_hbm, w2_hbm, y_ref,
                    wb1, wb3, wb2, w1s, w3s, w2s, sem):
    i = pl.program_id(0)
    active = i < used_ref[0]
    slot = slot_ref[i]

    def fetch(e, s):
        return [pltpu.make_async_copy(w.at[e], buf.at[s], sem.at[s])
                for w, buf in ((w1_hbm, wb1), (w3_hbm, wb3), (w2_hbm, wb2))]

    @pl.when(i == 0)
    def _():
        for cp in fetch(bexp_ref[0], 0):
            cp.start()

    @pl.when(first_ref[i] == 1)
    def _():
        @pl.when(nxt_ref[i] >= 0)
        def _():
            for cp in fetch(nxt_ref[i], 1 - slot):
                cp.start()

        for cp in fetch(bexp_ref[i], slot):
            cp.wait()
        w1s[...] = wb1[slot].astype(BF16)
        w3s[...] = wb3[slot].astype(BF16)
        w2s[...] = wb2[slot].astype(BF16)

    @pl.when(active)
    def _():
        def sub_block(r0):
            xb = _load_token_tiles(x_ref, MOE_SUB, r0).astype(BF16)
            yield
            a = _dot(xb, w1s[...])
            b = _dot(xb, w3s[...])
            yield
            hdn = (a * _sigmoid(a) * b).astype(BF16)
            yield
            _store_token_tiles(y_ref, _dot(hdn, w2s[...]), r0)

        _interleave(sub_block(r0) for r0 in range(0, MOE_BLK, MOE_SUB))

    @pl.when(jnp.logical_not(active))
    def _():
        y_ref[...] = jnp.zeros_like(y_ref)


def _experts(bexp, used, rows, w1, w3, w2):
    n_blocks = rows.shape[0] // (MOE_BLK * ROW_TILE)
    first, slot, nxt = _expert_runs(bexp, used)
    any_space = pl.BlockSpec(memory_space=pl.ANY)
    grid_spec = pltpu.PrefetchScalarGridSpec(
        num_scalar_prefetch=5,
        grid=(n_blocks,),
        in_specs=[pl.BlockSpec((MOE_BLK * ROW_TILE, LANES), lambda i, be, nu, *_: (jnp.minimum(i, nu[0] - 1), 0)),
                  any_space, any_space, any_space],
        out_specs=pl.BlockSpec((MOE_BLK * ROW_TILE, LANES), lambda i, *_: (i, 0)),
        scratch_shapes=[pltpu.VMEM((2, D_MODEL, EXPERT_FF), F32), pltpu.VMEM((2, D_MODEL, EXPERT_FF), F32),
                        pltpu.VMEM((2, EXPERT_FF, D_MODEL), F32),
                        pltpu.VMEM((D_MODEL, EXPERT_FF), BF16), pltpu.VMEM((D_MODEL, EXPERT_FF), BF16),
                        pltpu.VMEM((EXPERT_FF, D_MODEL), BF16), pltpu.SemaphoreType.DMA((2,))],
    )
    return pl.pallas_call(
        _experts_kernel,
        grid_spec=grid_spec,
        out_shape=jax.ShapeDtypeStruct(rows.shape, F32),
        compiler_params=_cparams(("arbitrary",)),
        name="experts",
    )(bexp, used, first, slot, nxt, rows, w1, w3, w2)


COMBINE_TM = 256


def _combine_kernel(d0_ref, d1_ref, n0_ref, n1_ref, y_ref, cols_ref, h1_ref, mod_ref, ln_ref, o_ref, ya, yb, sem):
    i = pl.program_id(0)
    slot = i % 2

    def gather(i0_ref, i1_ref, s):
        def issue(t, carry):
            pltpu.make_async_copy(_row_tile(y_ref, i0_ref[t]), _row_tile(ya.at[s], t), sem.at[s]).start(priority=0)
            pltpu.make_async_copy(_row_tile(y_ref, i1_ref[t]), _row_tile(yb.at[s], t), sem.at[s]).start(priority=1)
            return carry

        lax.fori_loop(0, COMBINE_TM, issue, 0, unroll=DMA_UNROLL)

    @pl.when(i == 0)
    def _():
        gather(d0_ref, d1_ref, 0)

    @pl.when(i + 1 < pl.num_programs(0))
    def _():
        gather(n0_ref, n1_ref, 1 - slot)

    for buf in (ya, yb):
        pltpu.make_async_copy(y_ref.at[pl.ds(0, COMBINE_TM * ROW_TILE)], buf.at[slot], sem.at[slot]).wait()

    cols = cols_ref[...]
    moe = (cols[:, 0:1] * _load_token_tiles(ya.at[slot], COMBINE_TM)
           + cols[:, 1:2] * _load_token_tiles(yb.at[slot], COMBINE_TM))
    pre = DEEPNORM_ALPHA * h1_ref[...] + mod_ref[5:6, :] * moe
    o_ref[...] = _ln(pre) * ln_ref[0:1, :] + ln_ref[1:2, :]


def _combine(dest0, dest1, y_rows, cols, h1, mod3, ln2, seq):
    t = h1.shape[0]
    tm = COMBINE_TM
    per_seq = seq // tm
    idx = pl.BlockSpec((tm,), lambda i: (i,), memory_space=pltpu.SMEM)
    idx_next = pl.BlockSpec((tm,), lambda i: (jnp.minimum(i + 1, t // tm - 1),), memory_space=pltpu.SMEM)
    return pl.pallas_call(
        _combine_kernel,
        grid=(t // tm,),
        in_specs=[idx, idx, idx_next, idx_next,
                  pl.BlockSpec(memory_space=pl.ANY),
                  pl.BlockSpec((tm, LANES), lambda i: (i, 0)),
                  pl.BlockSpec((tm, D_MODEL), lambda i: (i, 0)),
                  pl.BlockSpec((None, 6, D_MODEL), lambda i: (i // per_seq, 0, 0)),
                  pl.BlockSpec((2, D_MODEL), lambda i: (0, 0))],
        out_specs=pl.BlockSpec((tm, D_MODEL), lambda i: (i, 0)),
        out_shape=jax.ShapeDtypeStruct((t, D_MODEL), F32),
        scratch_shapes=[pltpu.VMEM((2, tm * ROW_TILE, LANES), F32), pltpu.VMEM((2, tm * ROW_TILE, LANES), F32),
                        pltpu.SemaphoreType.DMA((2,))],
        compiler_params=_cparams(("arbitrary",)),
        name="combine",
    )(dest0, dest1, dest0, dest1, y_rows, cols, h1, mod3, ln2)


def _layer(h, c8, w_ada, b_ada, w_in, w_attn_out, decay_f, decay_b, gn_gain, w_ret_out, w_out,
           ln1_gain, ln1_bias, w_coarse, b_coarse, w_fine, b_fine, w1, w3, w2, ln2_gain, ln2_bias):
    batch, seq, d = h.shape
    t = batch * seq
    x2 = h.reshape(t, d)

    mod = _ada(c8, w_ada, b_ada.reshape(1, -1))
    mod3 = mod[:batch].reshape(batch, 6, d)

    wb = w_in.astype(BF16)
    qkv, u1 = _inproj_attn(x2, mod3, wb, seq)
    rest = _inproj_rest(u1, wb)

    outs, lses = zip(*[_attention(qkv, g, batch, seq) for g in range(ATTN_GROUPS)])
    decays = jnp.concatenate([decay_f, decay_b]).astype(F32)
    retg = _retention(rest, decays, gn_gain.reshape(1, -1), batch, seq)

    w_route = jnp.concatenate([w_coarse, w_fine.transpose(1, 0, 2).reshape(d, N_EXPERTS)], axis=1)
    n_route = w_route.shape[1]
    w_route = jnp.pad(w_route, ((0, 0), (0, LANES - n_route)))
    wr_hi, wr_lo = _split_bf16(w_route)
    rbias = jnp.pad(jnp.concatenate([b_coarse, b_fine.reshape(-1)]), (0, LANES - n_route)).reshape(1, LANES)
    ln1 = jnp.stack([ln1_gain, ln1_bias])
    n_blocks = 2 * t // MOE_BLK + N_EXPERTS
    h1, u2, logits, rows0 = _merge(outs, lses, retg, rest, x2, mod3, w_attn_out.astype(BF16),
                                   w_ret_out.astype(BF16), w_out.astype(BF16), ln1, wr_hi, wr_lo, rbias, seq,
                                   n_blocks * MOE_BLK)

    cols, ints, counts = _route(logits)
    n_blocks_pad = -(-n_blocks // LANES) * LANES
    dest, meta = _plan(ints, counts, n_blocks_pad)
    dest0, dest1 = dest[0], dest[1]
    rows = _dispatch(dest0, dest1, u2, rows0)
    y_rows = _experts(meta[0, :n_blocks], meta[1, :1], rows, w1, w3, w2)
    out = _combine(dest0, dest1, y_rows, cols, h1, mod3, jnp.stack([ln2_gain, ln2_bias]), seq)
    return out.reshape(batch, seq, d)


def kernel(x, c, w_ada, b_ada, w_in, w_attn_out, ret_decay_fwd, ret_decay_bwd, ret_gn_gain, w_ret_out, w_out,
           ln1_gain, ln1_bias, w_coarse, b_coarse, w_fine, b_fine, w1, w3, w2, ln2_gain, ln2_bias):
    batch = x.shape[0]
    assert batch <= 8 and x.shape[1] % (2 * INPROJ_TM) == 0 and x.shape[2] == D_MODEL
    c8 = jnp.pad(c, ((0, 8 - batch), (0, 0)))
    h = x
    for l in range(w_ada.shape[0]):
        h = _layer(h, c8, w_ada[l], b_ada[l], w_in[l], w_attn_out[l], ret_decay_fwd[l], ret_decay_bwd[l],
                   ret_gn_gain[l], w_ret_out[l], w_out[l], ln1_gain[l], ln1_bias[l], w_coarse[l], b_coarse[l],
                   w_fine[l], b_fine[l], w1[l], w3[l], w2[l], ln2_gain[l], ln2_bias[l])
    return h
```

```python
import functools
import math

import jax
import jax.numpy as jnp
from jax import lax
from jax.experimental import pallas as pl
from jax.experimental.pallas import tpu as pltpu

F32 = jnp.float32
BF16 = jnp.bfloat16
I32 = jnp.int32

D_MODEL = 1024
ATTN_GROUPS = 3
HEADS_PER_GROUP = 4
HEAD_DIM = 128
ATTN_HEADS = ATTN_GROUPS * HEADS_PER_GROUP
GROUP_WIDTH = HEADS_PER_GROUP * HEAD_DIM
ATTN_PATTERNS = ((128, 1), (512, 4), (2048, 16))
ALIBI_MAX_EXP = 8.0
NEG = -1e30
RET_HEADS = 4
RET_QK = 256
RET_V = 512
N_GROUPS = 4
EXPERTS_PER_GROUP = 8
N_EXPERTS = N_GROUPS * EXPERTS_PER_GROUP
EXPERT_FF = 512
DEPTH = 1
DEEPNORM_ALPHA = (2.0 * DEPTH) ** 0.25
LN_EPS = 1e-5

LANES = 128
PERM_TILE = 512
ATTN_QB = 128
ATTN_HALF = 64
ATTN_KB = ATTN_QB + 2 * ATTN_HALF
RET_CHUNK = 256
RET_STEP = 1024
MOE_BLK = 512
MOE_SUB = 256
VMEM_LIMIT = 56 * 1024 * 1024


def _cparams(sem):
    return pltpu.CompilerParams(dimension_semantics=sem, vmem_limit_bytes=VMEM_LIMIT)


def _split_bf16(a):
    hi = a.astype(BF16)
    lo = (a - hi.astype(F32)).astype(BF16)
    return hi, lo


def _dot(a, b):
    return jnp.dot(a, b, preferred_element_type=F32)


def _dot3(a, b):
    ah, al = _split_bf16(a)
    bh, bl = _split_bf16(b)
    return _dot(ah, bh) + _dot(ah, bl) + _dot(al, bh)


def _ln(x):
    mu = jnp.mean(x, axis=-1, keepdims=True)
    xc = x - mu
    var = jnp.mean(xc * xc, axis=-1, keepdims=True)
    return xc * lax.rsqrt(var + LN_EPS)


def _sigmoid(x):
    return 1.0 / (1.0 + jnp.exp(-x))


ROW_TILE = D_MODEL // LANES


def _store_token_tiles(ref, val, row0=0):
    n = val.shape[0]
    for cc in range(ROW_TILE):
        ref[pl.ds(row0 * ROW_TILE + cc, n, stride=ROW_TILE), :] = val[:, cc * LANES:(cc + 1) * LANES]


def _load_token_tiles(ref, n, row0=0):
    return jnp.concatenate([ref[pl.ds(row0 * ROW_TILE + cc, n, stride=ROW_TILE), :] for cc in range(ROW_TILE)],
                           axis=1)


def _interleave(phased):
    pending = []
    for gen in phased:
        pending.append(gen)
        pending = [g for g in pending if next(g, StopIteration) is not StopIteration]
    while pending:
        pending = [g for g in pending if next(g, StopIteration) is not StopIteration]


def _ada_kernel(c_ref, w_ref, b_ref, o_ref):
    o_ref[...] = _dot3(c_ref[...], w_ref[...]) + b_ref[...]


def _ada(c8, w_ada, b_ada):
    n = w_ada.shape[1]
    return pl.pallas_call(
        _ada_kernel,
        grid=(n // D_MODEL,),
        in_specs=[pl.BlockSpec((8, D_MODEL), lambda j: (0, 0)),
                  pl.BlockSpec((D_MODEL, D_MODEL), lambda j: (0, j)),
                  pl.BlockSpec((1, D_MODEL), lambda j: (0, j))],
        out_specs=pl.BlockSpec((8, D_MODEL), lambda j: (0, j)),
        out_shape=jax.ShapeDtypeStruct((8, n), F32),
        compiler_params=_cparams(("arbitrary",)),
        name="ada",
    )(c8, w_ada, b_ada)


INPROJ_TM = 1024
LN_ROWS = 256
LN_UNROLL = 4


def _modulated_ln_rows(x_ref, mod_ref, emit):
    gain = 1.0 + mod_ref[1:2, :]
    shift = mod_ref[0:1, :]

    def chunk(c, carry):
        r0 = pl.multiple_of(c * LN_ROWS, LN_ROWS)
        emit(r0, _ln(x_ref[pl.ds(r0, LN_ROWS), :]) * gain + shift)
        return carry

    lax.fori_loop(0, INPROJ_TM // LN_ROWS, chunk, 0, unroll=LN_UNROLL)


def _inproj_attn_kernel(x_ref, mod_ref, wq_ref, wk_ref, wv_ref, o_ref, un_ref, uf_scr, u_scr):
    j = pl.program_id(1)

    @pl.when(j == 0)
    def _():
        def emit(r0, u):
            for cc in range(D_MODEL // LANES):
                uf_scr[cc, pl.ds(r0, LN_ROWS), :] = u[:, cc * LANES:(cc + 1) * LANES]
            ub = u.astype(BF16)
            u_scr[0, pl.ds(r0, LN_ROWS), :] = ub
            un_ref[pl.ds(r0, LN_ROWS), :] = ub

        _modulated_ln_rows(x_ref, mod_ref, emit)
        for g in (1, 2):
            dil = ATTN_PATTERNS[g][1]
            n = PERM_TILE // dil
            for t0 in range(0, INPROJ_TM, PERM_TILE):
                for res in range(dil):
                    for cc in range(D_MODEL // LANES):
                        rows = uf_scr[cc, pl.ds(t0 + res, n, stride=dil), :]
                        u_scr[g, t0 + res * n:t0 + (res + 1) * n, cc * LANES:(cc + 1) * LANES] = rows.astype(BF16)

    u = u_scr[j]
    for s, w_ref in enumerate((wq_ref, wk_ref, wv_ref)):
        o_ref[:, s * GROUP_WIDTH:(s + 1) * GROUP_WIDTH] = _dot(u, w_ref[...]).astype(BF16)


ATTN_WIDTH = ATTN_HEADS * HEAD_DIM
W_RQ0 = 3 * ATTN_WIDTH
W_RK0 = W_RQ0 + RET_HEADS * RET_QK
W_RV0 = W_RK0 + RET_HEADS * RET_QK
W_TAIL0 = W_RV0 + RET_HEADS * RET_V


def _inproj_attn(x2, mod3, wb, seq):
    t = x2.shape[0]
    tn = 3 * GROUP_WIDTH
    per_seq = seq // INPROJ_TM
    groups_per_range = ATTN_WIDTH // GROUP_WIDTH
    wspec = lambda s: pl.BlockSpec((D_MODEL, GROUP_WIDTH), lambda i, j: (0, s * groups_per_range + j))
    return pl.pallas_call(
        _inproj_attn_kernel,
        grid=(t // INPROJ_TM, ATTN_GROUPS),
        in_specs=[pl.BlockSpec((INPROJ_TM, D_MODEL), lambda i, j: (i, 0)),
                  pl.BlockSpec((None, 6, D_MODEL), lambda i, j: (i // per_seq, 0, 0)),
                  wspec(0), wspec(1), wspec(2)],
        out_specs=[pl.BlockSpec((INPROJ_TM, tn), lambda i, j: (i, j)),
                   pl.BlockSpec((INPROJ_TM, D_MODEL), lambda i, j: (i, 0))],
        out_shape=[jax.ShapeDtypeStruct((t, ATTN_GROUPS * tn), BF16),
                   jax.ShapeDtypeStruct((t, D_MODEL), BF16)],
        scratch_shapes=[pltpu.VMEM((D_MODEL // LANES, INPROJ_TM, LANES), F32),
                        pltpu.VMEM((ATTN_GROUPS, INPROJ_TM, D_MODEL), BF16)],
        compiler_params=_cparams(("arbitrary", "arbitrary")),
        name="inproj_attn",
    )(x2, mod3, wb, wb, wb)


REST_TN = RET_HEAD_COLS = 2 * RET_QK + RET_V
REST_TAIL_TILES = (RET_HEADS * RET_V + 2 * D_MODEL) // REST_TN


def _inproj_rest_kernel(u_ref, wq_ref, wk_ref, wv_ref, wt0_ref, wt1_ref, o_ref, w_scr):
    j = pl.program_id(0)
    first = pl.program_id(1) == 0

    def project(parts):
        @pl.when(first)
        def _():
            c0 = 0
            for w_ref in parts:
                wd = w_ref.shape[1]
                w_scr[:, c0:c0 + wd] = w_ref[...].astype(BF16)
                c0 += wd

        o_ref[...] = _dot(u_ref[...], w_scr[...]).astype(BF16)

    @pl.when(j < RET_HEADS)
    def _():
        project((wq_ref, wk_ref, wv_ref))

    @pl.when(j >= RET_HEADS)
    def _():
        project((wt0_ref, wt1_ref))


def _inproj_rest(u, w_in):
    t = u.shape[0]
    half = REST_TN // 2
    assert W_TAIL0 % half == 0 and W_RV0 % RET_V == 0
    head = lambda j: jnp.minimum(j, RET_HEADS - 1)
    tail = lambda j: jnp.maximum(j - RET_HEADS, 0)
    return pl.pallas_call(
        _inproj_rest_kernel,
        grid=(RET_HEADS + REST_TAIL_TILES, t // INPROJ_TM),
        in_specs=[pl.BlockSpec((INPROJ_TM, D_MODEL), lambda j, i: (i, 0)),
                  pl.BlockSpec((D_MODEL, RET_QK), lambda j, i: (0, W_RQ0 // RET_QK + head(j))),
                  pl.BlockSpec((D_MODEL, RET_QK), lambda j, i: (0, W_RK0 // RET_QK + head(j))),
                  pl.BlockSpec((D_MODEL, RET_V), lambda j, i: (0, W_RV0 // RET_V + head(j))),
                  pl.BlockSpec((D_MODEL, half), lambda j, i: (0, W_TAIL0 // half + 2 * tail(j))),
                  pl.BlockSpec((D_MODEL, half), lambda j, i: (0, W_TAIL0 // half + 2 * tail(j) + 1))],
        out_specs=pl.BlockSpec((INPROJ_TM, REST_TN), lambda j, i: (i, j)),
        out_shape=jax.ShapeDtypeStruct((t, (RET_HEADS + REST_TAIL_TILES) * REST_TN), BF16),
        scratch_shapes=[pltpu.VMEM((D_MODEL, REST_TN), BF16)],
        compiler_params=_cparams(("arbitrary", "arbitrary")),
        name="inproj_rest",
    )(u, w_in, w_in, w_in, w_in, w_in)


ATTN_OFFSETS = ATTN_KB // ATTN_HALF - 1
ATTN_UNROLL = 4
ATTN_MIN_BLOCKS = 4


def _attn_kernel(q_ref, k_ref, v_ref, o_ref, lse_ref, bias_scr, *scratch, group, dil, nt, n, nres):
    n_sub = nt * n
    if nt == 1:
        seqs = [(q_ref.at[0, rr], k_ref.at[0, rr], v_ref.at[0, rr], o_ref.at[0, rr], lse_ref.at[0, rr])
                for rr in range(nres)]
    else:
        seqs = [tuple(s.at[rr] for s in scratch) for rr in range(nres)]
        for rr, (qs, ks, vs, _, _) in enumerate(seqs):
            for t in range(nt):
                qs[t * n:(t + 1) * n, :] = q_ref[t, rr]
                ks[t * n:(t + 1) * n, :] = k_ref[t, rr]
                vs[t * n:(t + 1) * n, :] = v_ref[t, rr]

    @pl.when((pl.program_id(0) == 0) & (pl.program_id(1) == 0))
    def _():
        base = (lax.broadcasted_iota(I32, (ATTN_QB, ATTN_KB), 1)
                - lax.broadcasted_iota(I32, (ATTN_QB, ATTN_KB), 0))
        for j in range(ATTN_OFFSETS):
            dist = jnp.abs(base - j * ATTN_HALF)
            for hh in range(HEADS_PER_GROUP):
                slope = dil * 2.0 ** (-ALIBI_MAX_EXP * (group * HEADS_PER_GROUP + hh + 1) / ATTN_HEADS)
                bias_scr[hh * ATTN_OFFSETS + j] = jnp.where(dist <= ATTN_HALF, -slope * dist.astype(F32), NEG)

    scale = HEAD_DIM ** -0.5
    lane = lax.broadcasted_iota(I32, (ATTN_QB, LANES), 1)

    def block(blk, carry):
        q0 = pl.multiple_of(blk * ATTN_QB, ATTN_QB)
        start = pl.multiple_of(jnp.clip(q0 - ATTN_HALF, 0, n_sub - ATTN_KB), ATTN_HALF)
        j = (q0 - start) // ATTN_HALF
        for qs, ks, vs, os_, ls in seqs:
            lse_tile = jnp.zeros((ATTN_QB, LANES), F32)
            for hh in range(HEADS_PER_GROUP):
                cs = slice(hh * HEAD_DIM, (hh + 1) * HEAD_DIM)
                qb = qs[pl.ds(q0, ATTN_QB), cs]
                kb = ks[pl.ds(start, ATTN_KB), cs]
                vb = vs[pl.ds(start, ATTN_KB), cs]
                s = lax.dot_general(qb, kb, (((1,), (1,)), ((), ())), preferred_element_type=F32) * scale
                s = s + bias_scr[hh * ATTN_OFFSETS + j]
                m = jnp.max(s, axis=-1, keepdims=True)
                p = jnp.exp(s - m)
                l = jnp.sum(p, axis=-1, keepdims=True)
                o = _dot(p.astype(BF16), vb) * (1.0 / l)
                os_[pl.ds(q0, ATTN_QB), cs] = o.astype(BF16)
                lse_tile = jnp.where(lane == hh, m + jnp.log(l), lse_tile)
            ls[pl.ds(q0, ATTN_QB), :] = lse_tile
        return carry

    n_blk = n_sub // ATTN_QB
    lax.fori_loop(0, n_blk, block, 0, unroll=min(max(ATTN_UNROLL // nres, 1), n_blk))

    if nt > 1:
        for rr, (_, _, _, os_, ls) in enumerate(seqs):
            for t in range(nt):
                o_ref[t, rr] = os_[t * n:(t + 1) * n, :]
                lse_ref[t, rr] = ls[t * n:(t + 1) * n, :]


def _attention(qkv, group, batch, seq):
    dil = ATTN_PATTERNS[group][1]
    if dil == 1:
        nt, n = 1, seq
    else:
        nt, n = seq // PERM_TILE, PERM_TILE // dil
    n_sub = nt * n
    t = batch * seq
    qkv5 = qkv.reshape(batch, nt, dil, n, qkv.shape[1])
    cb = group * 3
    nres = min(dil, max(1, ATTN_MIN_BLOCKS * ATTN_QB // n_sub))
    blk = (None, nt, nres, n, GROUP_WIDTH)
    scratch = [pltpu.VMEM((HEADS_PER_GROUP * ATTN_OFFSETS, ATTN_QB, ATTN_KB), F32)]
    if nt > 1:
        scratch += [pltpu.VMEM((nres, n_sub, GROUP_WIDTH), BF16)] * 4 + [pltpu.VMEM((nres, n_sub, LANES), F32)]
    out, lse = pl.pallas_call(
        functools.partial(_attn_kernel, group=group, dil=dil, nt=nt, n=n, nres=nres),
        grid=(batch, dil // nres),
        in_specs=[pl.BlockSpec(blk, lambda b, r: (b, 0, r, 0, cb)),
                  pl.BlockSpec(blk, lambda b, r: (b, 0, r, 0, cb + 1)),
                  pl.BlockSpec(blk, lambda b, r: (b, 0, r, 0, cb + 2))],
        out_specs=[pl.BlockSpec(blk, lambda b, r: (b, 0, r, 0, 0)),
                   pl.BlockSpec((None, nt, nres, n, LANES), lambda b, r: (b, 0, r, 0, 0))],
        out_shape=[jax.ShapeDtypeStruct((batch, nt, dil, n, GROUP_WIDTH), BF16),
                   jax.ShapeDtypeStruct((batch, nt, dil, n, LANES), F32)],
        scratch_shapes=scratch,
        compiler_params=_cparams(("arbitrary", "arbitrary")),
        name=f"attn_g{group}",
    )(qkv5, qkv5, qkv5)
    return out.reshape(t, GROUP_WIDTH), lse.reshape(t, LANES)


def _log_sigmoid(x):
    return jnp.minimum(x, 0.0) - jnp.log(1.0 + jnp.exp(-jnp.abs(x)))


def _retention_kernel(decay_ref, qkv_ref, g_ref, gain_ref, o_ref, state, ybwd, dmat, kdec, qdec, *, nc):
    c = RET_CHUNK
    h = pl.program_id(1)
    i = pl.program_id(2)
    kscale = RET_QK ** -0.5

    def set_decays(lg, forward):
        row = lax.broadcasted_iota(I32, (c, c), 0)
        col = lax.broadcasted_iota(I32, (c, c), 1)
        pos = lax.broadcasted_iota(I32, (c, LANES), 0).astype(F32)
        if forward:
            gap, key_pow, query_pow = row - col, (c - 1.0) - pos, pos + 1.0
            keep = gap >= 0
        else:
            gap, key_pow, query_pow = col - row, pos, c - pos
            keep = gap > 0
        dmat[...] = jnp.where(keep, jnp.exp(lg * jnp.maximum(gap, 0).astype(F32)) * kscale, 0.0)
        kdec[...] = jnp.exp(lg * key_pow) * kscale
        qdec[...] = jnp.exp(lg * query_pow)

    def chunk_step(lg, r0, finish):
        q = qkv_ref[r0:r0 + c, 0:RET_QK]
        k = qkv_ref[r0:r0 + c, RET_QK:2 * RET_QK]
        v = qkv_ref[r0:r0 + c, 2 * RET_QK:]
        inner = lax.dot_general(q, k, (((1,), (1,)), ((), ())), preferred_element_type=F32) * dmat[...]
        y = _dot(inner.astype(BF16), v)
        kd = (k.astype(F32) * jnp.concatenate([kdec[...]] * (RET_QK // LANES), axis=1)).astype(BF16)
        kv = lax.dot_general(kd, v, (((0,), (0,)), ((), ())), preferred_element_type=F32)
        yield
        qd = jnp.concatenate([qdec[...]] * (RET_V // LANES), axis=1)
        y = y + _dot(q, state[...].astype(BF16)) * qd
        state[...] = state[...] * jnp.exp(lg * float(c)) + kv
        yield
        finish(r0, y)

    @pl.when((i == 0) | (i == nc))
    def _():
        state[...] = jnp.zeros_like(state)

    @pl.when(i < nc)
    def _():
        lg = _log_sigmoid(jnp.zeros((1, 1), F32) + decay_ref[RET_HEADS + h])

        @pl.when(i == 0)
        def _():
            set_decays(lg, False)

        base = pl.multiple_of((nc - 1 - i) * RET_STEP, RET_STEP)

        def finish(r0, y):
            ybwd[pl.ds(base + r0, c), :] = y

        _interleave(chunk_step(lg, r0, finish) for r0 in reversed(range(0, RET_STEP, c)))

    @pl.when(i >= nc)
    def _():
        lg = _log_sigmoid(jnp.zeros((1, 1), F32) + decay_ref[h])

        @pl.when(i == nc)
        def _():
            set_decays(lg, True)

        base = pl.multiple_of((i - nc) * RET_STEP, RET_STEP)

        def finish(r0, y):
            y = y + ybwd[pl.ds(base + r0, c), :]
            g = g_ref[r0:r0 + c, :].astype(F32)
            o_ref[r0:r0 + c, :] = (g * _sigmoid(g) * (_ln(y) * gain_ref[...])).astype(BF16)

        _interleave(chunk_step(lg, r0, finish) for r0 in range(0, RET_STEP, c))


def _retention(rest, decays, gn_gain, batch, seq):
    c = RET_STEP
    nc = seq // c
    rest3 = rest.reshape(batch, seq, rest.shape[1])
    gate_blk = RET_HEADS * RET_HEAD_COLS // RET_V

    def chunk(i):
        return jnp.where(i < nc, nc - 1 - i, i - nc)

    grid_spec = pltpu.PrefetchScalarGridSpec(
        num_scalar_prefetch=1,
        grid=(batch, RET_HEADS, 2 * nc),
        in_specs=[pl.BlockSpec((None, c, RET_HEAD_COLS), lambda b, h, i, d: (b, chunk(i), h)),
                  pl.BlockSpec((None, c, RET_V), lambda b, h, i, d: (b, jnp.maximum(i - nc, 0), gate_blk + h)),
                  pl.BlockSpec((1, RET_V), lambda b, h, i, d: (0, h))],
        out_specs=pl.BlockSpec((None, c, RET_V), lambda b, h, i, d: (b, jnp.maximum(i - nc, 0), h)),
        scratch_shapes=[pltpu.VMEM((RET_QK, RET_V), F32), pltpu.VMEM((seq, RET_V), F32),
                        pltpu.VMEM((RET_CHUNK, RET_CHUNK), F32), pltpu.VMEM((RET_CHUNK, LANES), F32),
                        pltpu.VMEM((RET_CHUNK, LANES), F32)],
    )
    out = pl.pallas_call(
        functools.partial(_retention_kernel, nc=nc),
        grid_spec=grid_spec,
        out_shape=jax.ShapeDtypeStruct((batch, seq, RET_HEADS * RET_V), BF16),
        compiler_params=_cparams(("arbitrary", "arbitrary", "arbitrary")),
        name="retention",
    )(decays, rest3, rest3, gn_gain)
    return out.reshape(batch * seq, RET_HEADS * RET_V)


MERGE_TM = PERM_TILE
MERGE_SUB = 256
ZERO_SPLIT = 4


def _merge_kernel(o0_ref, o1_ref, o2_ref, l0_ref, l1_ref, l2_ref, retg_ref, ga_ref, gr_ref, x_ref, mod_ref,
                  wa_ref, wr_ref, wo_ref, ln_ref, wrh_ref, wrl_ref, rb_ref,
                  h1_ref, u2_ref, lg_ref, zrows_ref, on_scr, ln_scr, zbuf, zsem):
    i = pl.program_id(0)

    @pl.when(i == 0)
    def _():
        zbuf[...] = jnp.zeros_like(zbuf)

    zrows = zbuf.shape[0]
    zero_copies = [pltpu.make_async_copy(zbuf, zrows_ref.at[pl.ds((i * ZERO_SPLIT + k) * zrows, zrows)], zsem)
                   for k in range(ZERO_SPLIT)]
    for cp in zero_copies:
        cp.start()

    for g in (1, 2):
        dil = ATTN_PATTERNS[g][1]
        n = MERGE_TM // dil
        o_ref, l_ref = ((o1_ref, l1_ref), (o2_ref, l2_ref))[g - 1]
        for res in range(dil):
            rows = o_ref[res * n:(res + 1) * n, :].astype(F32)
            for hh in range(HEADS_PER_GROUP):
                on_scr[g - 1, hh, pl.ds(res, n, stride=dil), :] = rows[:, hh * HEAD_DIM:(hh + 1) * HEAD_DIM]
            ln_scr[g - 1, pl.ds(res, n, stride=dil), :] = l_ref[res * n:(res + 1) * n, :]

    def sub_tile(r0):
        rs = slice(r0, r0 + MERGE_SUB)
        l0, l1, l2 = l0_ref[rs, :], ln_scr[0, rs, :], ln_scr[1, rs, :]
        lm = jnp.maximum(jnp.maximum(l0, l1), l2)
        e0, e1, e2 = jnp.exp(l0 - lm), jnp.exp(l1 - lm), jnp.exp(l2 - lm)
        inv = 1.0 / (e0 + e1 + e2)
        parts = []
        for hh in range(HEADS_PER_GROUP):
            sl = slice(hh * HEAD_DIM, (hh + 1) * HEAD_DIM)
            acc = (e0[:, hh:hh + 1] * o0_ref[rs, sl].astype(F32)
                   + e1[:, hh:hh + 1] * on_scr[0, hh, rs, :]
                   + e2[:, hh:hh + 1] * on_scr[1, hh, rs, :])
            parts.append((acc * inv[:, hh:hh + 1]).astype(BF16))
        attn = jnp.concatenate(parts, axis=1)
        yield

        branch_a = _dot(attn, wa_ref[...])
        branch_r = _dot(retg_ref[rs, :], wr_ref[...])
        yield
        merged = (_sigmoid(ga_ref[rs, :].astype(F32)) * branch_a
                  + _sigmoid(gr_ref[rs, :].astype(F32)) * branch_r)
        yield
        y = _dot(merged.astype(BF16), wo_ref[...])
        yield

        h1 = _ln(DEEPNORM_ALPHA * x_ref[rs, :] + mod_ref[2:3, :] * y) * ln_ref[0:1, :] + ln_ref[1:2, :]
        h1_ref[rs, :] = h1
        u2 = _ln(h1) * (1.0 + mod_ref[4:5, :]) + mod_ref[3:4, :]
        _store_token_tiles(u2_ref, u2, r0)
        uh, ul = _split_bf16(u2)
        yield
        lg_ref[rs, :] = (_dot(uh, wrh_ref[...]) + _dot(uh, wrl_ref[...]) + _dot(ul, wrh_ref[...])
                         + rb_ref[...])

    _interleave(sub_tile(r0) for r0 in range(0, MERGE_TM, MERGE_SUB))

    for cp in zero_copies:
        cp.wait()


def _merge(outs, lses, retg, rest, x2, mod3, wa, wr, wo, ln1, wr_hi, wr_lo, rbias, seq, n_rows):
    t = x2.shape[0]
    tm = MERGE_TM
    per_seq = seq // tm
    zrows, rem = divmod(n_rows * ROW_TILE, (t // tm) * ZERO_SPLIT)
    assert rem == 0 and zrows % 8 == 0
    row = lambda w: pl.BlockSpec((tm, w), lambda i: (i, 0))
    full = lambda a: pl.BlockSpec(a.shape, lambda i: (0,) * a.ndim)
    return pl.pallas_call(
        _merge_kernel,
        grid=(t // tm,),
        in_specs=[row(GROUP_WIDTH)] * 3 + [row(LANES)] * 3 + [
            row(RET_HEADS * RET_V),
            pl.BlockSpec((tm, D_MODEL), lambda i: (i, 6)),
            pl.BlockSpec((tm, D_MODEL), lambda i: (i, 7)),
            row(D_MODEL),
            pl.BlockSpec((None, 6, D_MODEL), lambda i: (i // per_seq, 0, 0)),
            full(wa), full(wr), full(wo), full(ln1), full(wr_hi), full(wr_lo), full(rbias)],
        out_specs=[row(D_MODEL), pl.BlockSpec((tm * ROW_TILE, LANES), lambda i: (i, 0)), row(LANES),
                   pl.BlockSpec(memory_space=pl.ANY)],
        out_shape=[jax.ShapeDtypeStruct((t, D_MODEL), F32),
                   jax.ShapeDtypeStruct((t * ROW_TILE, LANES), F32),
                   jax.ShapeDtypeStruct((t, LANES), F32),
                   jax.ShapeDtypeStruct((n_rows * ROW_TILE, LANES), F32)],
        scratch_shapes=[pltpu.VMEM((2, HEADS_PER_GROUP, tm, HEAD_DIM), F32), pltpu.VMEM((2, tm, LANES), F32),
                        pltpu.VMEM((zrows, LANES), F32), pltpu.SemaphoreType.DMA(())],
        compiler_params=_cparams(("arbitrary",)),
        name="merge",
    )(*outs, *lses, retg, rest, rest, x2, mod3, wa, wr, wo, ln1, wr_hi, wr_lo, rbias)


ROUTE_TM = 512
BIG = 1 << 20


def _route_kernel(lg_ref, cols_ref, ints_ref, cnt_ref, carry):
    i = pl.program_id(0)

    @pl.when(i == 0)
    def _():
        carry[...] = jnp.zeros_like(carry)

    tm = ROUTE_TM
    lg = lg_ref[...]
    lane = lax.broadcasted_iota(I32, (tm, LANES), 1)
    lane_f = lane.astype(F32)
    first = lambda mask: jnp.min(jnp.where(mask, lane_f, float(BIG)), axis=-1, keepdims=True).astype(I32)

    coarse = jnp.where(lane < N_GROUPS, lg, NEG)
    cmax = jnp.max(coarse, axis=-1, keepdims=True)
    gsel = first(coarse == cmax)
    p_group = 1.0 / jnp.sum(jnp.exp(coarse - cmax), axis=-1, keepdims=True)

    lo = N_GROUPS + EXPERTS_PER_GROUP * gsel
    fine = jnp.where((lane >= lo) & (lane < lo + EXPERTS_PER_GROUP), lg, NEG)
    v1 = jnp.max(fine, axis=-1, keepdims=True)
    i1 = first(fine == v1)
    fine2 = jnp.where(lane == i1, NEG, fine)
    v2 = jnp.max(fine2, axis=-1, keepdims=True)
    i2 = first(fine2 == v2)
    ex = jnp.exp(v2 - v1)
    den = 1.0 / (1.0 + ex)
    gate1 = p_group * den
    gate2 = p_group * (ex * den)
    e1 = i1 - N_GROUPS
    e2 = i2 - N_GROUPS

    oh1 = lane == e1
    oh2 = lane == e2
    cnt = jnp.where(oh1 | oh2, 1.0, 0.0)
    r_i = lax.broadcasted_iota(I32, (tm, tm), 0)
    c_i = lax.broadcasted_iota(I32, (tm, tm), 1)
    tri = jnp.where(r_i > c_i, 1.0, 0.0).astype(BF16)
    rank = _dot(tri, cnt.astype(BF16)) + carry[...]
    r1 = jnp.sum(jnp.where(oh1, rank, 0.0), axis=-1, keepdims=True)
    r2 = jnp.sum(jnp.where(oh2, rank, 0.0), axis=-1, keepdims=True)
    carry[...] = carry[...] + jnp.sum(cnt, axis=0, keepdims=True)
    cnt_ref[...] = jnp.broadcast_to(carry[...], cnt_ref.shape)

    cols_ref[...] = jnp.where(lane == 0, gate1, jnp.where(lane == 1, gate2, 0.0))
    packed = jnp.where(lane == 0, e1.astype(F32),
                       jnp.where(lane == 1, e2.astype(F32),
                                 jnp.where(lane == 2, r1, jnp.where(lane == 3, r2, 0.0))))
    ints_ref[...] = packed.T[0:8, :].astype(I32)


def _route(logits):
    t = logits.shape[0]
    tm = ROUTE_TM
    return pl.pallas_call(
        _route_kernel,
        grid=(t // tm,),
        in_specs=[pl.BlockSpec((tm, LANES), lambda i: (i, 0))],
        out_specs=[pl.BlockSpec((tm, LANES), lambda i: (i, 0)),
                   pl.BlockSpec((8, tm), lambda i: (0, i)),
                   pl.BlockSpec((8, LANES), lambda i: (0, 0))],
        out_shape=[jax.ShapeDtypeStruct((t, LANES), F32),
                   jax.ShapeDtypeStruct((8, t), I32),
                   jax.ShapeDtypeStruct((8, LANES), F32)],
        scratch_shapes=[pltpu.VMEM((1, LANES), F32)],
        compiler_params=_cparams(("arbitrary",)),
        name="route",
    )(logits)


def _plan_kernel(ints_ref, cnt_ref, dest_ref, meta_ref, *, n_blocks_pad):
    sub = lax.broadcasted_iota(I32, (LANES, LANES), 0)
    lane = lax.broadcasted_iota(I32, (LANES, LANES), 1)
    cnt = cnt_ref[0:1, :]
    nblk_row = jnp.floor((cnt + (MOE_BLK - 1.0)) * (1.0 / MOE_BLK))
    nblk_mat = jnp.broadcast_to(nblk_row, (LANES, LANES))
    start_col = jnp.sum(jnp.where(lane < sub, nblk_mat, 0.0), axis=-1, keepdims=True)
    nblk_col = jnp.sum(jnp.where(lane == sub, nblk_mat, 0.0), axis=-1, keepdims=True)
    end_col = start_col + nblk_col

    ints = ints_ref[...]
    base = jnp.zeros(ints.shape, F32)
    for e in range(N_EXPERTS):
        base = jnp.where(ints == e, start_col[e:e + 1, :] * float(MOE_BLK), base)
    dest = base[0:2, :].astype(I32) + ints[2:4, :]
    dest_ref[...] = jnp.concatenate([dest, jnp.zeros((6, ints.shape[1]), I32)], axis=0)

    blk = lax.broadcasted_iota(I32, (LANES, n_blocks_pad), 1).astype(F32)
    e_sub = lax.broadcasted_iota(I32, (LANES, n_blocks_pad), 0)
    done = jnp.where((e_sub < N_EXPERTS) & (end_col <= blk), 1.0, 0.0)
    bexp = jnp.minimum(jnp.sum(done, axis=0, keepdims=True), N_EXPERTS - 1.0)
    used = jnp.sum(nblk_row, axis=-1, keepdims=True)
    row = lax.broadcasted_iota(I32, (8, n_blocks_pad), 0)
    meta = jnp.where(row == 0, bexp, jnp.where(row == 1, used, 0.0))
    meta_ref[...] = meta.astype(I32)


def _plan(ints, counts, n_blocks_pad):
    t = ints.shape[1]
    return pl.pallas_call(
        functools.partial(_plan_kernel, n_blocks_pad=n_blocks_pad),
        out_shape=[jax.ShapeDtypeStruct((8, t), I32), jax.ShapeDtypeStruct((8, n_blocks_pad), I32)],
        compiler_params=pltpu.CompilerParams(vmem_limit_bytes=VMEM_LIMIT),
        name="plan",
    )(ints, counts)


DISPATCH_TM = 512
DMA_UNROLL = 8


def _row_tile(ref, r):
    return ref.at[pl.ds(pl.multiple_of(r * ROW_TILE, ROW_TILE), ROW_TILE)]


def _dispatch_kernel(d0_ref, d1_ref, u2_ref, rows_in_ref, rows_ref, sem):
    del rows_in_ref

    def issue(t, carry):
        src = _row_tile(u2_ref, t)
        pltpu.make_async_copy(src, _row_tile(rows_ref, d0_ref[t]), sem).start(priority=0)
        pltpu.make_async_copy(src, _row_tile(rows_ref, d1_ref[t]), sem).start(priority=1)
        return carry

    lax.fori_loop(0, DISPATCH_TM, issue, 0, unroll=DMA_UNROLL)
    for _ in range(2):
        pltpu.make_async_copy(u2_ref, rows_ref.at[pl.ds(0, DISPATCH_TM * ROW_TILE)], sem).wait()


def _dispatch(dest0, dest1, u2, rows0):
    t = u2.shape[0] // ROW_TILE
    idx = pl.BlockSpec((DISPATCH_TM,), lambda i: (i,), memory_space=pltpu.SMEM)
    return pl.pallas_call(
        _dispatch_kernel,
        grid=(t // DISPATCH_TM,),
        in_specs=[idx, idx,
                  pl.BlockSpec((DISPATCH_TM * ROW_TILE, LANES), lambda i: (i, 0)),
                  pl.BlockSpec(memory_space=pl.ANY)],
        out_specs=pl.BlockSpec(memory_space=pl.ANY),
        out_shape=jax.ShapeDtypeStruct(rows0.shape, F32),
        scratch_shapes=[pltpu.SemaphoreType.DMA(())],
        input_output_aliases={3: 0},
        compiler_params=_cparams(("arbitrary",)),
        name="dispatch",
    )(dest0, dest1, u2, rows0)


def _expert_runs(bexp, used):
    n = bexp.shape[0]
    idx = jnp.arange(n, dtype=I32)
    first = (idx < used[0]) & ((idx == 0) | (bexp != jnp.roll(bexp, 1)))
    slot = (jnp.cumsum(first.astype(I32)) - 1) % 2
    first_at_or_after = lax.cummin(jnp.where(first, idx, n)[::-1])[::-1]
    first_after = jnp.concatenate([first_at_or_after[1:], jnp.full((1,), n, I32)])
    nxt = jnp.where(first_after < n, bexp[jnp.minimum(first_after, n - 1)], -1)
    return first.astype(I32), slot.astype(I32), nxt.astype(I32)


def _experts_kernel(bexp_ref, used_ref, first_ref, slot_ref, nxt_ref, x_ref, w1_hbm, w3_hbm, w2_hbm, y_ref,
                    wb1, wb3, wb2, w1s, w3s, w2s, sem):
    i = pl.program_id(0)
    active = i < used_ref[0]
    slot = slot_ref[i]

    def fetch(e, s):
        return [pltpu.make_async_copy(w.at[e], buf.at[s], sem.at[s])
                for w, buf in ((w1_hbm, wb1), (w3_hbm, wb3), (w2_hbm, wb2))]

    @pl.when(i == 0)
    def _():
        for cp in fetch(bexp_ref[0], 0):
            cp.start()

    @pl.when(first_ref[i] == 1)
    def _():
        @pl.when(nxt_ref[i] >= 0)
        def _():
            for cp in fetch(nxt_ref[i], 1 - slot):
                cp.start()

        for cp in fetch(bexp_ref[i], slot):
            cp.wait()
        w1s[...] = wb1[slot].astype(BF16)
        w3s[...] = wb3[slot].astype(BF16)
        w2s[...] = wb2[slot].astype(BF16)

    @pl.when(active)
    def _():
        def sub_block(r0):
            xb = _load_token_tiles(x_ref, MOE_SUB, r0).astype(BF16)
            yield
            a = _dot(xb, w1s[...])
            b = _dot(xb, w3s[...])
            yield
            hdn = (a * _sigmoid(a) * b).astype(BF16)
            yield
            _store_token_tiles(y_ref, _dot(hdn, w2s[...]), r0)

        _interleave(sub_block(r0) for r0 in range(0, MOE_BLK, MOE_SUB))

    @pl.when(jnp.logical_not(active))
    def _():
        y_ref[...] = jnp.zeros_like(y_ref)


def _experts(bexp, used, rows, w1, w3, w2):
    n_blocks = rows.shape[0] // (MOE_BLK * ROW_TILE)
    first, slot, nxt = _expert_runs(bexp, used)
    any_space = pl.BlockSpec(memory_space=pl.ANY)
    grid_spec = pltpu.PrefetchScalarGridSpec(
        num_scalar_prefetch=5,
        grid=(n_blocks,),
        in_specs=[pl.BlockSpec((MOE_BLK * ROW_TILE, LANES), lambda i, be, nu, *_: (jnp.minimum(i, nu[0] - 1), 0)),
                  any_space, any_space, any_space],
        out_specs=pl.BlockSpec((MOE_BLK * ROW_TILE, LANES), lambda i, *_: (i, 0)),
        scratch_shapes=[pltpu.VMEM((2, D_MODEL, EXPERT_FF), F32), pltpu.VMEM((2, D_MODEL, EXPERT_FF), F32),
                        pltpu.VMEM((2, EXPERT_FF, D_MODEL), F32),
                        pltpu.VMEM((D_MODEL, EXPERT_FF), BF16), pltpu.VMEM((D_MODEL, EXPERT_FF), BF16),
                        pltpu.VMEM((EXPERT_FF, D_MODEL), BF16), pltpu.SemaphoreType.DMA((2,))],
    )
    return pl.pallas_call(
        _experts_kernel,
        grid_spec=grid_spec,
        out_shape=jax.ShapeDtypeStruct(rows.shape, F32),
        compiler_params=_cparams(("arbitrary",)),
        name="experts",
    )(bexp, used, first, slot, nxt, rows, w1, w3, w2)


COMBINE_TM = 512


def _combine_kernel(d0_ref, d1_ref, n0_ref, n1_ref, y_ref, cols_ref, h1_ref, mod_ref, ln_ref, o_ref, ya, yb, sem):
    i = pl.program_id(0)
    slot = i % 2

    def gather(i0_ref, i1_ref, s):
        def issue(t, carry):
            pltpu.make_async_copy(_row_tile(y_ref, i0_ref[t]), _row_tile(ya.at[s], t), sem.at[s]).start(priority=0)
            pltpu.make_async_copy(_row_tile(y_ref, i1_ref[t]), _row_tile(yb.at[s], t), sem.at[s]).start(priority=1)
            return carry

        lax.fori_loop(0, COMBINE_TM, issue, 0, unroll=DMA_UNROLL)

    @pl.when(i == 0)
    def _():
        gather(d0_ref, d1_ref, 0)

    @pl.when(i + 1 < pl.num_programs(0))
    def _():
        gather(n0_ref, n1_ref, 1 - slot)

    for buf in (ya, yb):
        pltpu.make_async_copy(y_ref.at[pl.ds(0, COMBINE_TM * ROW_TILE)], buf.at[slot], sem.at[slot]).wait()

    cols = cols_ref[...]
    moe = (cols[:, 0:1] * _load_token_tiles(ya.at[slot], COMBINE_TM)
           + cols[:, 1:2] * _load_token_tiles(yb.at[slot], COMBINE_TM))
    pre = DEEPNORM_ALPHA * h1_ref[...] + mod_ref[5:6, :] * moe
    o_ref[...] = _ln(pre) * ln_ref[0:1, :] + ln_ref[1:2, :]


def _combine(dest0, dest1, y_rows, cols, h1, mod3, ln2, seq):
    t = h1.shape[0]
    tm = COMBINE_TM
    per_seq = seq // tm
    idx = pl.BlockSpec((tm,), lambda i: (i,), memory_space=pltpu.SMEM)
    idx_next = pl.BlockSpec((tm,), lambda i: (jnp.minimum(i + 1, t // tm - 1),), memory_space=pltpu.SMEM)
    return pl.pallas_call(
        _combine_kernel,
        grid=(t // tm,),
        in_specs=[idx, idx, idx_next, idx_next,
                  pl.BlockSpec(memory_space=pl.ANY),
                  pl.BlockSpec((tm, LANES), lambda i: (i, 0)),
                  pl.BlockSpec((tm, D_MODEL), lambda i: (i, 0)),
                  pl.BlockSpec((None, 6, D_MODEL), lambda i: (i // per_seq, 0, 0)),
                  pl.BlockSpec((2, D_MODEL), lambda i: (0, 0))],
        out_specs=pl.BlockSpec((tm, D_MODEL), lambda i: (i, 0)),
        out_shape=jax.ShapeDtypeStruct((t, D_MODEL), F32),
        scratch_shapes=[pltpu.VMEM((2, tm * ROW_TILE, LANES), F32), pltpu.VMEM((2, tm * ROW_TILE, LANES), F32),
                        pltpu.SemaphoreType.DMA((2,))],
        compiler_params=_cparams(("arbitrary",)),
        name="combine",
    )(dest0, dest1, dest0, dest1, y_rows, cols, h1, mod3, ln2)


def _layer(h, c8, w_ada, b_ada, w_in, w_attn_out, decay_f, decay_b, gn_gain, w_ret_out, w_out,
           ln1_gain, ln1_bias, w_coarse, b_coarse, w_fine, b_fine, w1, w3, w2, ln2_gain, ln2_bias):
    batch, seq, d = h.shape
    t = batch * seq
    x2 = h.reshape(t, d)

    mod = _ada(c8, w_ada, b_ada.reshape(1, -1))
    mod3 = mod[:batch].reshape(batch, 6, d)

    qkv, u1 = _inproj_attn(x2, mod3, w_in[:, :3 * ATTN_WIDTH].astype(BF16), seq)
    rest = _inproj_rest(u1, w_in)

    outs, lses = zip(*[_attention(qkv, g, batch, seq) for g in range(ATTN_GROUPS)])
    decays = jnp.concatenate([decay_f, decay_b]).astype(F32)
    retg = _retention(rest, decays, gn_gain.reshape(1, -1), batch, seq)

    w_route = jnp.concatenate([w_coarse, w_fine.transpose(1, 0, 2).reshape(d, N_EXPERTS)], axis=1)
    n_route = w_route.shape[1]
    w_route = jnp.pad(w_route, ((0, 0), (0, LANES - n_route)))
    wr_hi, wr_lo = _split_bf16(w_route)
    rbias = jnp.pad(jnp.concatenate([b_coarse, b_fine.reshape(-1)]), (0, LANES - n_route)).reshape(1, LANES)
    ln1 = jnp.stack([ln1_gain, ln1_bias])
    n_blocks = 2 * t // MOE_BLK + N_EXPERTS
    h1, u2, logits, rows0 = _merge(outs, lses, retg, rest, x2, mod3, w_attn_out.astype(BF16),
                                   w_ret_out.astype(BF16), w_out.astype(BF16), ln1, wr_hi, wr_lo, rbias, seq,
                                   n_blocks * MOE_BLK)

    cols, ints, counts = _route(logits)
    n_blocks_pad = -(-n_blocks // LANES) * LANES
    dest, meta = _plan(ints, counts, n_blocks_pad)
    dest0, dest1 = dest[0], dest[1]
    rows = _dispatch(dest0, dest1, u2, rows0)
    y_rows = _experts(meta[0, :n_blocks], meta[1, :1], rows, w1, w3, w2)
    out = _combine(dest0, dest1, y_rows, cols, h1, mod3, jnp.stack([ln2_gain, ln2_bias]), seq)
    return out.reshape(batch, seq, d)


def kernel(x, c, w_ada, b_ada, w_in, w_attn_out, ret_decay_fwd, ret_decay_bwd, ret_gn_gain, w_ret_out, w_out,
           ln1_gain, ln1_bias, w_coarse, b_coarse, w_fine, b_fine, w1, w3, w2, ln2_gain, ln2_bias):
    batch = x.shape[0]
    assert batch <= 8 and x.shape[1] % (2 * INPROJ_TM) == 0 and x.shape[2] == D_MODEL
    c8 = jnp.pad(c, ((0, 8 - batch), (0, 0)))
    h = x
    for l in range(w_ada.shape[0]):
        h = _layer(h, c8, w_ada[l], b_ada[l], w_in[l], w_attn_out[l], ret_decay_fwd[l], ret_decay_bwd[l],
                   ret_gn_gain[l], w_ret_out[l], w_out[l], ln1_gain[l], ln1_bias[l], w_coarse[l], b_coarse[l],
                   w_fine[l], b_fine[l], w1[l], w3[l], w2[l], ln2_gain[l], ln2_bias[l])
    return h
```

```python
import functools
import math

import jax
import jax.numpy as jnp
from jax import lax
from jax.experimental import pallas as pl
from jax.experimental.pallas import tpu as pltpu

F32 = jnp.float32
BF16 = jnp.bfloat16
I32 = jnp.int32

D_MODEL = 1024
ATTN_GROUPS = 3
HEADS_PER_GROUP = 4
HEAD_DIM = 128
ATTN_HEADS = ATTN_GROUPS * HEADS_PER_GROUP
GROUP_WIDTH = HEADS_PER_GROUP * HEAD_DIM
ATTN_PATTERNS = ((128, 1), (512, 4), (2048, 16))
ALIBI_MAX_EXP = 8.0
NEG = -1e30
RET_HEADS = 4
RET_QK = 256
RET_V = 512
N_GROUPS = 4
EXPERTS_PER_GROUP = 8
N_EXPERTS = N_GROUPS * EXPERTS_PER_GROUP
EXPERT_FF = 512
DEPTH = 1
DEEPNORM_ALPHA = (2.0 * DEPTH) ** 0.25
LN_EPS = 1e-5

LANES = 128
PERM_TILE = 512
ATTN_QB = 128
ATTN_HALF = 64
ATTN_KB = ATTN_QB + 2 * ATTN_HALF
RET_CHUNK = 256
RET_STEP = 1024
MOE_BLK = 512
MOE_SUB = 256
VMEM_LIMIT = 56 * 1024 * 1024


def _cparams(sem):
    return pltpu.CompilerParams(dimension_semantics=sem, vmem_limit_bytes=VMEM_LIMIT)


def _split_bf16(a):
    hi = a.astype(BF16)
    lo = (a - hi.astype(F32)).astype(BF16)
    return hi, lo


def _dot(a, b):
    return jnp.dot(a, b, preferred_element_type=F32)


def _dot3(a, b):
    ah, al = _split_bf16(a)
    bh, bl = _split_bf16(b)
    return _dot(ah, bh) + _dot(ah, bl) + _dot(al, bh)


def _ln(x):
    mu = jnp.mean(x, axis=-1, keepdims=True)
    xc = x - mu
    var = jnp.mean(xc * xc, axis=-1, keepdims=True)
    return xc * lax.rsqrt(var + LN_EPS)


def _sigmoid(x):
    return 1.0 / (1.0 + jnp.exp(-x))


ROW_TILE = D_MODEL // LANES


def _store_token_tiles(ref, val, row0=0):
    n = val.shape[0]
    for cc in range(ROW_TILE):
        ref[pl.ds(row0 * ROW_TILE + cc, n, stride=ROW_TILE), :] = val[:, cc * LANES:(cc + 1) * LANES]


def _load_token_tiles(ref, n, row0=0):
    return jnp.concatenate([ref[pl.ds(row0 * ROW_TILE + cc, n, stride=ROW_TILE), :] for cc in range(ROW_TILE)],
                           axis=1)


def _interleave(phased):
    pending = []
    for gen in phased:
        pending.append(gen)
        pending = [g for g in pending if next(g, StopIteration) is not StopIteration]
    while pending:
        pending = [g for g in pending if next(g, StopIteration) is not StopIteration]


def _ada_kernel(c_ref, w_ref, b_ref, o_ref):
    o_ref[...] = _dot3(c_ref[...], w_ref[...]) + b_ref[...]


def _ada(c8, w_ada, b_ada):
    n = w_ada.shape[1]
    return pl.pallas_call(
        _ada_kernel,
        grid=(n // D_MODEL,),
        in_specs=[pl.BlockSpec((8, D_MODEL), lambda j: (0, 0)),
                  pl.BlockSpec((D_MODEL, D_MODEL), lambda j: (0, j)),
                  pl.BlockSpec((1, D_MODEL), lambda j: (0, j))],
        out_specs=pl.BlockSpec((8, D_MODEL), lambda j: (0, j)),
        out_shape=jax.ShapeDtypeStruct((8, n), F32),
        compiler_params=_cparams(("arbitrary",)),
        name="ada",
    )(c8, w_ada, b_ada)


INPROJ_TM = 1024
LN_ROWS = 256
LN_UNROLL = 4


def _modulated_ln_rows(x_ref, mod_ref, emit):
    gain = 1.0 + mod_ref[1:2, :]
    shift = mod_ref[0:1, :]

    def chunk(c, carry):
        r0 = pl.multiple_of(c * LN_ROWS, LN_ROWS)
        emit(r0, _ln(x_ref[pl.ds(r0, LN_ROWS), :]) * gain + shift)
        return carry

    lax.fori_loop(0, INPROJ_TM // LN_ROWS, chunk, 0, unroll=LN_UNROLL)


def _inproj_attn_kernel(x_ref, mod_ref, wq_ref, wk_ref, wv_ref, o_ref, un_ref, uf_scr, u_scr):
    j = pl.program_id(1)

    @pl.when(j == 0)
    def _():
        def emit(r0, u):
            for cc in range(D_MODEL // LANES):
                uf_scr[cc, pl.ds(r0, LN_ROWS), :] = u[:, cc * LANES:(cc + 1) * LANES]
            ub = u.astype(BF16)
            u_scr[0, pl.ds(r0, LN_ROWS), :] = ub
            un_ref[pl.ds(r0, LN_ROWS), :] = ub

        _modulated_ln_rows(x_ref, mod_ref, emit)
        for g in (1, 2):
            dil = ATTN_PATTERNS[g][1]
            n = PERM_TILE // dil
            for t0 in range(0, INPROJ_TM, PERM_TILE):
                for res in range(dil):
                    for cc in range(D_MODEL // LANES):
                        rows = uf_scr[cc, pl.ds(t0 + res, n, stride=dil), :]
                        u_scr[g, t0 + res * n:t0 + (res + 1) * n, cc * LANES:(cc + 1) * LANES] = rows.astype(BF16)

    u = u_scr[j]
    for s, w_ref in enumerate((wq_ref, wk_ref, wv_ref)):
        o_ref[:, s * GROUP_WIDTH:(s + 1) * GROUP_WIDTH] = _dot(u, w_ref[...]).astype(BF16)


ATTN_WIDTH = ATTN_HEADS * HEAD_DIM
W_RQ0 = 3 * ATTN_WIDTH
W_RK0 = W_RQ0 + RET_HEADS * RET_QK
W_RV0 = W_RK0 + RET_HEADS * RET_QK
W_TAIL0 = W_RV0 + RET_HEADS * RET_V


def _inproj_attn(x2, mod3, wb, seq):
    t = x2.shape[0]
    tn = 3 * GROUP_WIDTH
    per_seq = seq // INPROJ_TM
    groups_per_range = ATTN_WIDTH // GROUP_WIDTH
    wspec = lambda s: pl.BlockSpec((D_MODEL, GROUP_WIDTH), lambda i, j: (0, s * groups_per_range + j))
    return pl.pallas_call(
        _inproj_attn_kernel,
        grid=(t // INPROJ_TM, ATTN_GROUPS),
        in_specs=[pl.BlockSpec((INPROJ_TM, D_MODEL), lambda i, j: (i, 0)),
                  pl.BlockSpec((None, 6, D_MODEL), lambda i, j: (i // per_seq, 0, 0)),
                  wspec(0), wspec(1), wspec(2)],
        out_specs=[pl.BlockSpec((INPROJ_TM, tn), lambda i, j: (i, j)),
                   pl.BlockSpec((INPROJ_TM, D_MODEL), lambda i, j: (i, 0))],
        out_shape=[jax.ShapeDtypeStruct((t, ATTN_GROUPS * tn), BF16),
                   jax.ShapeDtypeStruct((t, D_MODEL), BF16)],
        scratch_shapes=[pltpu.VMEM((D_MODEL // LANES, INPROJ_TM, LANES), F32),
                        pltpu.VMEM((ATTN_GROUPS, INPROJ_TM, D_MODEL), BF16)],
        compiler_params=_cparams(("arbitrary", "arbitrary")),
        name="inproj_attn",
    )(x2, mod3, wb, wb, wb)


REST_TN = RET_HEAD_COLS = 2 * RET_QK + RET_V
REST_TAIL_TILES = (RET_HEADS * RET_V + 2 * D_MODEL) // REST_TN
REST_TM = 2048


def _inproj_rest_kernel(u_ref, wq_ref, wk_ref, wv_ref, wt0_ref, wt1_ref, o_ref):
    j = pl.program_id(0)

    def project(parts):
        u = u_ref[...]
        c0 = 0
        for w_ref in parts:
            wd = w_ref.shape[1]
            o_ref[:, c0:c0 + wd] = _dot(u, w_ref[...]).astype(BF16)
            c0 += wd

    @pl.when(j < RET_HEADS)
    def _():
        project((wq_ref, wk_ref, wv_ref))

    @pl.when(j >= RET_HEADS)
    def _():
        project((wt0_ref, wt1_ref))


def _inproj_rest(u, wb):
    t = u.shape[0]
    half = REST_TN // 2
    assert W_TAIL0 % half == 0 and W_RV0 % RET_V == 0
    head = lambda j: jnp.minimum(j, RET_HEADS - 1)
    tail = lambda j: jnp.maximum(j - RET_HEADS, 0)
    return pl.pallas_call(
        _inproj_rest_kernel,
        grid=(RET_HEADS + REST_TAIL_TILES, t // REST_TM),
        in_specs=[pl.BlockSpec((REST_TM, D_MODEL), lambda j, i: (i, 0)),
                  pl.BlockSpec((D_MODEL, RET_QK), lambda j, i: (0, W_RQ0 // RET_QK + head(j))),
                  pl.BlockSpec((D_MODEL, RET_QK), lambda j, i: (0, W_RK0 // RET_QK + head(j))),
                  pl.BlockSpec((D_MODEL, RET_V), lambda j, i: (0, W_RV0 // RET_V + head(j))),
                  pl.BlockSpec((D_MODEL, half), lambda j, i: (0, W_TAIL0 // half + 2 * tail(j))),
                  pl.BlockSpec((D_MODEL, half), lambda j, i: (0, W_TAIL0 // half + 2 * tail(j) + 1))],
        out_specs=pl.BlockSpec((REST_TM, REST_TN), lambda j, i: (i, j)),
        out_shape=jax.ShapeDtypeStruct((t, (RET_HEADS + REST_TAIL_TILES) * REST_TN), BF16),
        compiler_params=_cparams(("arbitrary", "arbitrary")),
        name="inproj_rest",
    )(u, wb, wb, wb, wb, wb)


ATTN_OFFSETS = ATTN_KB // ATTN_HALF - 1
ATTN_UNROLL = 4
ATTN_MIN_BLOCKS = 4


def _attn_kernel(q_ref, k_ref, v_ref, o_ref, lse_ref, bias_scr, *scratch, group, dil, nt, n, nres):
    n_sub = nt * n
    if nt == 1:
        seqs = [(q_ref.at[0, rr], k_ref.at[0, rr], v_ref.at[0, rr], o_ref.at[0, rr], lse_ref.at[0, rr])
                for rr in range(nres)]
    else:
        seqs = [tuple(s.at[rr] for s in scratch) for rr in range(nres)]
        for rr, (qs, ks, vs, _, _) in enumerate(seqs):
            for t in range(nt):
                qs[t * n:(t + 1) * n, :] = q_ref[t, rr]
                ks[t * n:(t + 1) * n, :] = k_ref[t, rr]
                vs[t * n:(t + 1) * n, :] = v_ref[t, rr]

    @pl.when((pl.program_id(0) == 0) & (pl.program_id(1) == 0))
    def _():
        base = (lax.broadcasted_iota(I32, (ATTN_QB, ATTN_KB), 1)
                - lax.broadcasted_iota(I32, (ATTN_QB, ATTN_KB), 0))
        for j in range(ATTN_OFFSETS):
            dist = jnp.abs(base - j * ATTN_HALF)
            for hh in range(HEADS_PER_GROUP):
                slope = dil * 2.0 ** (-ALIBI_MAX_EXP * (group * HEADS_PER_GROUP + hh + 1) / ATTN_HEADS)
                bias_scr[hh * ATTN_OFFSETS + j] = jnp.where(dist <= ATTN_HALF, -slope * dist.astype(F32), NEG)

    scale = HEAD_DIM ** -0.5
    lane = lax.broadcasted_iota(I32, (ATTN_QB, LANES), 1)

    def block(blk, carry):
        q0 = pl.multiple_of(blk * ATTN_QB, ATTN_QB)
        start = pl.multiple_of(jnp.clip(q0 - ATTN_HALF, 0, n_sub - ATTN_KB), ATTN_HALF)
        j = (q0 - start) // ATTN_HALF
        for qs, ks, vs, os_, ls in seqs:
            lse_tile = jnp.zeros((ATTN_QB, LANES), F32)
            for hh in range(HEADS_PER_GROUP):
                cs = slice(hh * HEAD_DIM, (hh + 1) * HEAD_DIM)
                qb = qs[pl.ds(q0, ATTN_QB), cs]
                kb = ks[pl.ds(start, ATTN_KB), cs]
                vb = vs[pl.ds(start, ATTN_KB), cs]
                s = lax.dot_general(qb, kb, (((1,), (1,)), ((), ())), preferred_element_type=F32) * scale
                s = s + bias_scr[hh * ATTN_OFFSETS + j]
                m = jnp.max(s, axis=-1, keepdims=True)
                p = jnp.exp(s - m)
                l = jnp.sum(p, axis=-1, keepdims=True)
                o = _dot(p.astype(BF16), vb) * (1.0 / l)
                os_[pl.ds(q0, ATTN_QB), cs] = o.astype(BF16)
                lse_tile = jnp.where(lane == hh, m + jnp.log(l), lse_tile)
            ls[pl.ds(q0, ATTN_QB), :] = lse_tile
        return carry

    n_blk = n_sub // ATTN_QB
    lax.fori_loop(0, n_blk, block, 0, unroll=min(max(ATTN_UNROLL // nres, 1), n_blk))

    if nt > 1:
        for rr, (_, _, _, os_, ls) in enumerate(seqs):
            for t in range(nt):
                o_ref[t, rr] = os_[t * n:(t + 1) * n, :]
                lse_ref[t, rr] = ls[t * n:(t + 1) * n, :]


def _attention(qkv, group, batch, seq):
    dil = ATTN_PATTERNS[group][1]
    if dil == 1:
        nt, n = 1, seq
    else:
        nt, n = seq // PERM_TILE, PERM_TILE // dil
    n_sub = nt * n
    t = batch * seq
    qkv5 = qkv.reshape(batch, nt, dil, n, qkv.shape[1])
    cb = group * 3
    nres = min(dil, max(1, ATTN_MIN_BLOCKS * ATTN_QB // n_sub))
    blk = (None, nt, nres, n, GROUP_WIDTH)
    scratch = [pltpu.VMEM((HEADS_PER_GROUP * ATTN_OFFSETS, ATTN_QB, ATTN_KB), F32)]
    if nt > 1:
        scratch += [pltpu.VMEM((nres, n_sub, GROUP_WIDTH), BF16)] * 4 + [pltpu.VMEM((nres, n_sub, LANES), F32)]
    out, lse = pl.pallas_call(
        functools.partial(_attn_kernel, group=group, dil=dil, nt=nt, n=n, nres=nres),
        grid=(batch, dil // nres),
        in_specs=[pl.BlockSpec(blk, lambda b, r: (b, 0, r, 0, cb)),
                  pl.BlockSpec(blk, lambda b, r: (b, 0, r, 0, cb + 1)),
                  pl.BlockSpec(blk, lambda b, r: (b, 0, r, 0, cb + 2))],
        out_specs=[pl.BlockSpec(blk, lambda b, r: (b, 0, r, 0, 0)),
                   pl.BlockSpec((None, nt, nres, n, LANES), lambda b, r: (b, 0, r, 0, 0))],
        out_shape=[jax.ShapeDtypeStruct((batch, nt, dil, n, GROUP_WIDTH), BF16),
                   jax.ShapeDtypeStruct((batch, nt, dil, n, LANES), F32)],
        scratch_shapes=scratch,
        compiler_params=_cparams(("arbitrary", "arbitrary")),
        name=f"attn_g{group}",
    )(qkv5, qkv5, qkv5)
    return out.reshape(t, GROUP_WIDTH), lse.reshape(t, LANES)


def _log_sigmoid(x):
    return jnp.minimum(x, 0.0) - jnp.log(1.0 + jnp.exp(-jnp.abs(x)))


def _retention_kernel(decay_ref, qkv_ref, g_ref, gain_ref, o_ref, state, ybwd, dmat, kdec, qdec, *, nc):
    c = RET_CHUNK
    h = pl.program_id(1)
    i = pl.program_id(2)
    kscale = RET_QK ** -0.5

    def set_decays(lg, forward):
        row = lax.broadcasted_iota(I32, (c, c), 0)
        col = lax.broadcasted_iota(I32, (c, c), 1)
        pos = lax.broadcasted_iota(I32, (c, LANES), 0).astype(F32)
        if forward:
            gap, key_pow, query_pow = row - col, (c - 1.0) - pos, pos + 1.0
            keep = gap >= 0
        else:
            gap, key_pow, query_pow = col - row, pos, c - pos
            keep = gap > 0
        dmat[...] = jnp.where(keep, jnp.exp(lg * jnp.maximum(gap, 0).astype(F32)) * kscale, 0.0)
        kdec[...] = jnp.exp(lg * key_pow) * kscale
        qdec[...] = jnp.exp(lg * query_pow)

    def chunk_step(lg, r0, finish):
        q = qkv_ref[r0:r0 + c, 0:RET_QK]
        k = qkv_ref[r0:r0 + c, RET_QK:2 * RET_QK]
        v = qkv_ref[r0:r0 + c, 2 * RET_QK:]
        inner = lax.dot_general(q, k, (((1,), (1,)), ((), ())), preferred_element_type=F32) * dmat[...]
        y = _dot(inner.astype(BF16), v)
        kd = (k.astype(F32) * jnp.concatenate([kdec[...]] * (RET_QK // LANES), axis=1)).astype(BF16)
        kv = lax.dot_general(kd, v, (((0,), (0,)), ((), ())), preferred_element_type=F32)
        yield
        qd = jnp.concatenate([qdec[...]] * (RET_V // LANES), axis=1)
        y = y + _dot(q, state[...].astype(BF16)) * qd
        state[...] = state[...] * jnp.exp(lg * float(c)) + kv
        yield
        finish(r0, y)

    @pl.when((i == 0) | (i == nc))
    def _():
        state[...] = jnp.zeros_like(state)

    @pl.when(i < nc)
    def _():
        lg = _log_sigmoid(jnp.zeros((1, 1), F32) + decay_ref[RET_HEADS + h])

        @pl.when(i == 0)
        def _():
            set_decays(lg, False)

        base = pl.multiple_of((nc - 1 - i) * RET_STEP, RET_STEP)

        def finish(r0, y):
            ybwd[pl.ds(base + r0, c), :] = y

        _interleave(chunk_step(lg, r0, finish) for r0 in reversed(range(0, RET_STEP, c)))

    @pl.when(i >= nc)
    def _():
        lg = _log_sigmoid(jnp.zeros((1, 1), F32) + decay_ref[h])

        @pl.when(i == nc)
        def _():
            set_decays(lg, True)

        base = pl.multiple_of((i - nc) * RET_STEP, RET_STEP)

        def finish(r0, y):
            y = y + ybwd[pl.ds(base + r0, c), :]
            g = g_ref[r0:r0 + c, :].astype(F32)
            o_ref[r0:r0 + c, :] = (g * _sigmoid(g) * (_ln(y) * gain_ref[...])).astype(BF16)

        _interleave(chunk_step(lg, r0, finish) for r0 in range(0, RET_STEP, c))


def _retention(rest, decays, gn_gain, batch, seq):
    c = RET_STEP
    nc = seq // c
    rest3 = rest.reshape(batch, seq, rest.shape[1])
    gate_blk = RET_HEADS * RET_HEAD_COLS // RET_V

    def chunk(i):
        return jnp.where(i < nc, nc - 1 - i, i - nc)

    grid_spec = pltpu.PrefetchScalarGridSpec(
        num_scalar_prefetch=1,
        grid=(batch, RET_HEADS, 2 * nc),
        in_specs=[pl.BlockSpec((None, c, RET_HEAD_COLS), lambda b, h, i, d: (b, chunk(i), h)),
                  pl.BlockSpec((None, c, RET_V), lambda b, h, i, d: (b, jnp.maximum(i - nc, 0), gate_blk + h)),
                  pl.BlockSpec((1, RET_V), lambda b, h, i, d: (0, h))],
        out_specs=pl.BlockSpec((None, c, RET_V), lambda b, h, i, d: (b, jnp.maximum(i - nc, 0), h)),
        scratch_shapes=[pltpu.VMEM((RET_QK, RET_V), F32), pltpu.VMEM((seq, RET_V), F32),
                        pltpu.VMEM((RET_CHUNK, RET_CHUNK), F32), pltpu.VMEM((RET_CHUNK, LANES), F32),
                        pltpu.VMEM((RET_CHUNK, LANES), F32)],
    )
    out = pl.pallas_call(
        functools.partial(_retention_kernel, nc=nc),
        grid_spec=grid_spec,
        out_shape=jax.ShapeDtypeStruct((batch, seq, RET_HEADS * RET_V), BF16),
        compiler_params=_cparams(("arbitrary", "arbitrary", "arbitrary")),
        name="retention",
    )(decays, rest3, rest3, gn_gain)
    return out.reshape(batch * seq, RET_HEADS * RET_V)


MERGE_TM = PERM_TILE
MERGE_SUB = 256
ZERO_SPLIT = 4


def _merge_kernel(o0_ref, o1_ref, o2_ref, l0_ref, l1_ref, l2_ref, retg_ref, ga_ref, gr_ref, x_ref, mod_ref,
                  wa_ref, wr_ref, wo_ref, ln_ref, wrh_ref, wrl_ref, rb_ref,
                  h1_ref, u2_ref, lg_ref, zrows_ref, on_scr, ln_scr, zbuf, zsem):
    i = pl.program_id(0)

    @pl.when(i == 0)
    def _():
        zbuf[...] = jnp.zeros_like(zbuf)

    zrows = zbuf.shape[0]
    zero_copies = [pltpu.make_async_copy(zbuf, zrows_ref.at[pl.ds((i * ZERO_SPLIT + k) * zrows, zrows)], zsem)
                   for k in range(ZERO_SPLIT)]
    for cp in zero_copies:
        cp.start()

    for g in (1, 2):
        dil = ATTN_PATTERNS[g][1]
        n = MERGE_TM // dil
        o_ref, l_ref = ((o1_ref, l1_ref), (o2_ref, l2_ref))[g - 1]
        for res in range(dil):
            rows = o_ref[res * n:(res + 1) * n, :].astype(F32)
            for hh in range(HEADS_PER_GROUP):
                on_scr[g - 1, hh, pl.ds(res, n, stride=dil), :] = rows[:, hh * HEAD_DIM:(hh + 1) * HEAD_DIM]
            ln_scr[g - 1, pl.ds(res, n, stride=dil), :] = l_ref[res * n:(res + 1) * n, :]

    def sub_tile(r0):
        rs = slice(r0, r0 + MERGE_SUB)
        l0, l1, l2 = l0_ref[rs, :], ln_scr[0, rs, :], ln_scr[1, rs, :]
        lm = jnp.maximum(jnp.maximum(l0, l1), l2)
        e0, e1, e2 = jnp.exp(l0 - lm), jnp.exp(l1 - lm), jnp.exp(l2 - lm)
        inv = 1.0 / (e0 + e1 + e2)
        parts = []
        for hh in range(HEADS_PER_GROUP):
            sl = slice(hh * HEAD_DIM, (hh + 1) * HEAD_DIM)
            acc = (e0[:, hh:hh + 1] * o0_ref[rs, sl].astype(F32)
                   + e1[:, hh:hh + 1] * on_scr[0, hh, rs, :]
                   + e2[:, hh:hh + 1] * on_scr[1, hh, rs, :])
            parts.append((acc * inv[:, hh:hh + 1]).astype(BF16))
        attn = jnp.concatenate(parts, axis=1)
        yield

        branch_a = _dot(attn, wa_ref[...])
        branch_r = _dot(retg_ref[rs, :], wr_ref[...])
        yield
        merged = (_sigmoid(ga_ref[rs, :].astype(F32)) * branch_a
                  + _sigmoid(gr_ref[rs, :].astype(F32)) * branch_r)
        yield
        y = _dot(merged.astype(BF16), wo_ref[...])
        yield

        h1 = _ln(DEEPNORM_ALPHA * x_ref[rs, :] + mod_ref[2:3, :] * y) * ln_ref[0:1, :] + ln_ref[1:2, :]
        h1_ref[rs, :] = h1
        u2 = _ln(h1) * (1.0 + mod_ref[4:5, :]) + mod_ref[3:4, :]
        _store_token_tiles(u2_ref, u2, r0)
        uh, ul = _split_bf16(u2)
        yield
        lg_ref[rs, :] = (_dot(uh, wrh_ref[...]) + _dot(uh, wrl_ref[...]) + _dot(ul, wrh_ref[...])
                         + rb_ref[...])

    _interleave(sub_tile(r0) for r0 in range(0, MERGE_TM, MERGE_SUB))

    for cp in zero_copies:
        cp.wait()


def _merge(outs, lses, retg, rest, x2, mod3, wa, wr, wo, ln1, wr_hi, wr_lo, rbias, seq, n_rows):
    t = x2.shape[0]
    tm = MERGE_TM
    per_seq = seq // tm
    zrows, rem = divmod(n_rows * ROW_TILE, (t // tm) * ZERO_SPLIT)
    assert rem == 0 and zrows % 8 == 0
    row = lambda w: pl.BlockSpec((tm, w), lambda i: (i, 0))
    full = lambda a: pl.BlockSpec(a.shape, lambda i: (0,) * a.ndim)
    return pl.pallas_call(
        _merge_kernel,
        grid=(t // tm,),
        in_specs=[row(GROUP_WIDTH)] * 3 + [row(LANES)] * 3 + [
            row(RET_HEADS * RET_V),
            pl.BlockSpec((tm, D_MODEL), lambda i: (i, 6)),
            pl.BlockSpec((tm, D_MODEL), lambda i: (i, 7)),
            row(D_MODEL),
            pl.BlockSpec((None, 6, D_MODEL), lambda i: (i // per_seq, 0, 0)),
            full(wa), full(wr), full(wo), full(ln1), full(wr_hi), full(wr_lo), full(rbias)],
        out_specs=[row(D_MODEL), pl.BlockSpec((tm * ROW_TILE, LANES), lambda i: (i, 0)), row(LANES),
                   pl.BlockSpec(memory_space=pl.ANY)],
        out_shape=[jax.ShapeDtypeStruct((t, D_MODEL), F32),
                   jax.ShapeDtypeStruct((t * ROW_TILE, LANES), F32),
                   jax.ShapeDtypeStruct((t, LANES), F32),
                   jax.ShapeDtypeStruct((n_rows * ROW_TILE, LANES), F32)],
        scratch_shapes=[pltpu.VMEM((2, HEADS_PER_GROUP, tm, HEAD_DIM), F32), pltpu.VMEM((2, tm, LANES), F32),
                        pltpu.VMEM((zrows, LANES), F32), pltpu.SemaphoreType.DMA(())],
        compiler_params=_cparams(("arbitrary",)),
        name="merge",
    )(*outs, *lses, retg, rest, rest, x2, mod3, wa, wr, wo, ln1, wr_hi, wr_lo, rbias)


ROUTE_TM = 512
BIG = 1 << 20


def _route_kernel(lg_ref, cols_ref, ints_ref, cnt_ref, carry):
    i = pl.program_id(0)

    @pl.when(i == 0)
    def _():
        carry[...] = jnp.zeros_like(carry)

    tm = ROUTE_TM
    lg = lg_ref[...]
    lane = lax.broadcasted_iota(I32, (tm, LANES), 1)
    lane_f = lane.astype(F32)
    first = lambda mask: jnp.min(jnp.where(mask, lane_f, float(BIG)), axis=-1, keepdims=True).astype(I32)

    coarse = jnp.where(lane < N_GROUPS, lg, NEG)
    cmax = jnp.max(coarse, axis=-1, keepdims=True)
    gsel = first(coarse == cmax)
    p_group = 1.0 / jnp.sum(jnp.exp(coarse - cmax), axis=-1, keepdims=True)

    lo = N_GROUPS + EXPERTS_PER_GROUP * gsel
    fine = jnp.where((lane >= lo) & (lane < lo + EXPERTS_PER_GROUP), lg, NEG)
    v1 = jnp.max(fine, axis=-1, keepdims=True)
    i1 = first(fine == v1)
    fine2 = jnp.where(lane == i1, NEG, fine)
    v2 = jnp.max(fine2, axis=-1, keepdims=True)
    i2 = first(fine2 == v2)
    ex = jnp.exp(v2 - v1)
    den = 1.0 / (1.0 + ex)
    gate1 = p_group * den
    gate2 = p_group * (ex * den)
    e1 = i1 - N_GROUPS
    e2 = i2 - N_GROUPS

    oh1 = lane == e1
    oh2 = lane == e2
    cnt = jnp.where(oh1 | oh2, 1.0, 0.0)
    r_i = lax.broadcasted_iota(I32, (tm, tm), 0)
    c_i = lax.broadcasted_iota(I32, (tm, tm), 1)
    tri = jnp.where(r_i > c_i, 1.0, 0.0).astype(BF16)
    rank = _dot(tri, cnt.astype(BF16)) + carry[...]
    r1 = jnp.sum(jnp.where(oh1, rank, 0.0), axis=-1, keepdims=True)
    r2 = jnp.sum(jnp.where(oh2, rank, 0.0), axis=-1, keepdims=True)
    carry[...] = carry[...] + jnp.sum(cnt, axis=0, keepdims=True)
    cnt_ref[...] = jnp.broadcast_to(carry[...], cnt_ref.shape)

    cols_ref[...] = jnp.where(lane == 0, gate1, jnp.where(lane == 1, gate2, 0.0))
    packed = jnp.where(lane == 0, e1.astype(F32),
                       jnp.where(lane == 1, e2.astype(F32),
                                 jnp.where(lane == 2, r1, jnp.where(lane == 3, r2, 0.0))))
    ints_ref[...] = packed.T[0:8, :].astype(I32)


def _route(logits):
    t = logits.shape[0]
    tm = ROUTE_TM
    return pl.pallas_call(
        _route_kernel,
        grid=(t // tm,),
        in_specs=[pl.BlockSpec((tm, LANES), lambda i: (i, 0))],
        out_specs=[pl.BlockSpec((tm, LANES), lambda i: (i, 0)),
                   pl.BlockSpec((8, tm), lambda i: (0, i)),
                   pl.BlockSpec((8, LANES), lambda i: (0, 0))],
        out_shape=[jax.ShapeDtypeStruct((t, LANES), F32),
                   jax.ShapeDtypeStruct((8, t), I32),
                   jax.ShapeDtypeStruct((8, LANES), F32)],
        scratch_shapes=[pltpu.VMEM((1, LANES), F32)],
        compiler_params=_cparams(("arbitrary",)),
        name="route",
    )(logits)


def _plan_kernel(ints_ref, cnt_ref, dest_ref, meta_ref, *, n_blocks_pad):
    sub = lax.broadcasted_iota(I32, (LANES, LANES), 0)
    lane = lax.broadcasted_iota(I32, (LANES, LANES), 1)
    cnt = cnt_ref[0:1, :]
    nblk_row = jnp.floor((cnt + (MOE_BLK - 1.0)) * (1.0 / MOE_BLK))
    nblk_mat = jnp.broadcast_to(nblk_row, (LANES, LANES))
    start_col = jnp.sum(jnp.where(lane < sub, nblk_mat, 0.0), axis=-1, keepdims=True)
    nblk_col = jnp.sum(jnp.where(lane == sub, nblk_mat, 0.0), axis=-1, keepdims=True)
    end_col = start_col + nblk_col

    ints = ints_ref[...]
    base = jnp.zeros(ints.shape, F32)
    for e in range(N_EXPERTS):
        base = jnp.where(ints == e, start_col[e:e + 1, :] * float(MOE_BLK), base)
    dest = base[0:2, :].astype(I32) + ints[2:4, :]
    dest_ref[...] = jnp.concatenate([dest, jnp.zeros((6, ints.shape[1]), I32)], axis=0)

    blk = lax.broadcasted_iota(I32, (LANES, n_blocks_pad), 1).astype(F32)
    e_sub = lax.broadcasted_iota(I32, (LANES, n_blocks_pad), 0)
    done = jnp.where((e_sub < N_EXPERTS) & (end_col <= blk), 1.0, 0.0)
    bexp = jnp.minimum(jnp.sum(done, axis=0, keepdims=True), N_EXPERTS - 1.0)
    used = jnp.sum(nblk_row, axis=-1, keepdims=True)
    row = lax.broadcasted_iota(I32, (8, n_blocks_pad), 0)
    meta = jnp.where(row == 0, bexp, jnp.where(row == 1, used, 0.0))
    meta_ref[...] = meta.astype(I32)


def _plan(ints, counts, n_blocks_pad):
    t = ints.shape[1]
    return pl.pallas_call(
        functools.partial(_plan_kernel, n_blocks_pad=n_blocks_pad),
        out_shape=[jax.ShapeDtypeStruct((8, t), I32), jax.ShapeDtypeStruct((8, n_blocks_pad), I32)],
        compiler_params=pltpu.CompilerParams(vmem_limit_bytes=VMEM_LIMIT),
        name="plan",
    )(ints, counts)


DISPATCH_TM = 1024
DMA_UNROLL = 8


def _row_tile(ref, r):
    return ref.at[pl.ds(pl.multiple_of(r * ROW_TILE, ROW_TILE), ROW_TILE)]


def _dispatch_kernel(d0_ref, d1_ref, u2_ref, rows_in_ref, rows_ref, sem):
    del rows_in_ref

    def issue(t, carry):
        src = _row_tile(u2_ref, t)
        pltpu.make_async_copy(src, _row_tile(rows_ref, d0_ref[t]), sem).start(priority=0)
        pltpu.make_async_copy(src, _row_tile(rows_ref, d1_ref[t]), sem).start(priority=1)
        return carry

    lax.fori_loop(0, DISPATCH_TM, issue, 0, unroll=DMA_UNROLL)
    for _ in range(2):
        pltpu.make_async_copy(u2_ref, rows_ref.at[pl.ds(0, DISPATCH_TM * ROW_TILE)], sem).wait()


def _dispatch(dest0, dest1, u2, rows0):
    t = u2.shape[0] // ROW_TILE
    idx = pl.BlockSpec((DISPATCH_TM,), lambda i: (i,), memory_space=pltpu.SMEM)
    return pl.pallas_call(
        _dispatch_kernel,
        grid=(t // DISPATCH_TM,),
        in_specs=[idx, idx,
                  pl.BlockSpec((DISPATCH_TM * ROW_TILE, LANES), lambda i: (i, 0)),
                  pl.BlockSpec(memory_space=pl.ANY)],
        out_specs=pl.BlockSpec(memory_space=pl.ANY),
        out_shape=jax.ShapeDtypeStruct(rows0.shape, F32),
        scratch_shapes=[pltpu.SemaphoreType.DMA(())],
        input_output_aliases={3: 0},
        compiler_params=_cparams(("arbitrary",)),
        name="dispatch",
    )(dest0, dest1, u2, rows0)


def _expert_runs(bexp, used):
    n = bexp.shape[0]
    idx = jnp.arange(n, dtype=I32)
    first = (idx < used[0]) & ((idx == 0) | (bexp != jnp.roll(bexp, 1)))
    slot = (jnp.cumsum(first.astype(I32)) - 1) % 2
    first_at_or_after = lax.cummin(jnp.where(first, idx, n)[::-1])[::-1]
    first_after = jnp.concatenate([first_at_or_after[1:], jnp.full((1,), n, I32)])
    nxt = jnp.where(first_after < n, bexp[jnp.minimum(first_after, n - 1)], -1)
    return first.astype(I32), slot.astype(I32), nxt.astype(I32)


def _experts_kernel(bexp_ref, used_ref, first_ref, slot_ref, nxt_ref, x_ref, w1_hbm, w3_hbm, w2_hbm, y_ref,
                    wb1, wb3, wb2, w1s, w3s, w2s, sem):
    i = pl.program_id(0)
    active = i < used_ref[0]
    slot = slot_ref[i]

    def fetch(e, s):
        return [pltpu.make_async_copy(w.at[e], buf.at[s], sem.at[s])
                for w, buf in ((w1_hbm, wb1), (w3_hbm, wb3), (w2_hbm, wb2))]

    @pl.when(i == 0)
    def _():
        for cp in fetch(bexp_ref[0], 0):
            cp.start()

    @pl.when(first_ref[i] == 1)
    def _():
        @pl.when(nxt_ref[i] >= 0)
        def _():
            for cp in fetch(nxt_ref[i], 1 - slot):
                cp.start()

        for cp in fetch(bexp_ref[i], slot):
            cp.wait()
        w1s[...] = wb1[slot].astype(BF16)
        w3s[...] = wb3[slot].astype(BF16)
        w2s[...] = wb2[slot].astype(BF16)

    @pl.when(active)
    def _():
        def sub_block(r0):
            xb = _load_token_tiles(x_ref, MOE_SUB, r0).astype(BF16)
            yield
            a = _dot(xb, w1s[...])
            b = _dot(xb, w3s[...])
            yield
            hdn = (a * _sigmoid(a) * b).astype(BF16)
            yield
            _store_token_tiles(y_ref, _dot(hdn, w2s[...]), r0)

        _interleave(sub_block(r0) for r0 in range(0, MOE_BLK, MOE_SUB))

    @pl.when(jnp.logical_not(active))
    def _():
        y_ref[...] = jnp.zeros_like(y_ref)


def _experts(bexp, used, rows, w1, w3, w2):
    n_blocks = rows.shape[0] // (MOE_BLK * ROW_TILE)
    first, slot, nxt = _expert_runs(bexp, used)
    any_space = pl.BlockSpec(memory_space=pl.ANY)
    grid_spec = pltpu.PrefetchScalarGridSpec(
        num_scalar_prefetch=5,
        grid=(n_blocks,),
        in_specs=[pl.BlockSpec((MOE_BLK * ROW_TILE, LANES), lambda i, be, nu, *_: (jnp.minimum(i, nu[0] - 1), 0)),
                  any_space, any_space, any_space],
        out_specs=pl.BlockSpec((MOE_BLK * ROW_TILE, LANES), lambda i, *_: (i, 0)),
        scratch_shapes=[pltpu.VMEM((2, D_MODEL, EXPERT_FF), F32), pltpu.VMEM((2, D_MODEL, EXPERT_FF), F32),
                        pltpu.VMEM((2, EXPERT_FF, D_MODEL), F32),
                        pltpu.VMEM((D_MODEL, EXPERT_FF), BF16), pltpu.VMEM((D_MODEL, EXPERT_FF), BF16),
                        pltpu.VMEM((EXPERT_FF, D_MODEL), BF16), pltpu.SemaphoreType.DMA((2,))],
    )
    return pl.pallas_call(
        _experts_kernel,
        grid_spec=grid_spec,
        out_shape=jax.ShapeDtypeStruct(rows.shape, F32),
        compiler_params=_cparams(("arbitrary",)),
        name="experts",
    )(bexp, used, first, slot, nxt, rows, w1, w3, w2)


COMBINE_TM = 256


def _combine_kernel(d0_ref, d1_ref, n0_ref, n1_ref, y_ref, cols_ref, h1_ref, mod_ref, ln_ref, o_ref, ya, yb, sem):
    i = pl.program_id(0)
    slot = i % 2

    def gather(i0_ref, i1_ref, s):
        def issue(t, carry):
            pltpu.make_async_copy(_row_tile(y_ref, i0_ref[t]), _row_tile(ya.at[s], t), sem.at[s]).start(priority=0)
            pltpu.make_async_copy(_row_tile(y_ref, i1_ref[t]), _row_tile(yb.at[s], t), sem.at[s]).start(priority=1)
            return carry

        lax.fori_loop(0, COMBINE_TM, issue, 0, unroll=DMA_UNROLL)

    @pl.when(i == 0)
    def _():
        gather(d0_ref, d1_ref, 0)

    @pl.when(i + 1 < pl.num_programs(0))
    def _():
        gather(n0_ref, n1_ref, 1 - slot)

    for buf in (ya, yb):
        pltpu.make_async_copy(y_ref.at[pl.ds(0, COMBINE_TM * ROW_TILE)], buf.at[slot], sem.at[slot]).wait()

    cols = cols_ref[...]
    moe = (cols[:, 0:1] * _load_token_tiles(ya.at[slot], COMBINE_TM)
           + cols[:, 1:2] * _load_token_tiles(yb.at[slot], COMBINE_TM))
    pre = DEEPNORM_ALPHA * h1_ref[...] + mod_ref[5:6, :] * moe
    o_ref[...] = _ln(pre) * ln_ref[0:1, :] + ln_ref[1:2, :]


def _combine(dest0, dest1, y_rows, cols, h1, mod3, ln2, seq):
    t = h1.shape[0]
    tm = COMBINE_TM
    per_seq = seq // tm
    idx = pl.BlockSpec((tm,), lambda i: (i,), memory_space=pltpu.SMEM)
    idx_next = pl.BlockSpec((tm,), lambda i: (jnp.minimum(i + 1, t // tm - 1),), memory_space=pltpu.SMEM)
    return pl.pallas_call(
        _combine_kernel,
        grid=(t // tm,),
        in_specs=[idx, idx, idx_next, idx_next,
                  pl.BlockSpec(memory_space=pl.ANY),
                  pl.BlockSpec((tm, LANES), lambda i: (i, 0)),
                  pl.BlockSpec((tm, D_MODEL), lambda i: (i, 0)),
                  pl.BlockSpec((None, 6, D_MODEL), lambda i: (i // per_seq, 0, 0)),
                  pl.BlockSpec((2, D_MODEL), lambda i: (0, 0))],
        out_specs=pl.BlockSpec((tm, D_MODEL), lambda i: (i, 0)),
        out_shape=jax.ShapeDtypeStruct((t, D_MODEL), F32),
        scratch_shapes=[pltpu.VMEM((2, tm * ROW_TILE, LANES), F32), pltpu.VMEM((2, tm * ROW_TILE, LANES), F32),
                        pltpu.SemaphoreType.DMA((2,))],
        compiler_params=_cparams(("arbitrary",)),
        name="combine",
    )(dest0, dest1, dest0, dest1, y_rows, cols, h1, mod3, ln2)


def _layer(h, c8, w_ada, b_ada, w_in, w_attn_out, decay_f, decay_b, gn_gain, w_ret_out, w_out,
           ln1_gain, ln1_bias, w_coarse, b_coarse, w_fine, b_fine, w1, w3, w2, ln2_gain, ln2_bias):
    batch, seq, d = h.shape
    t = batch * seq
    x2 = h.reshape(t, d)

    mod = _ada(c8, w_ada, b_ada.reshape(1, -1))
    mod3 = mod[:batch].reshape(batch, 6, d)

    wb = w_in.astype(BF16)
    qkv, u1 = _inproj_attn(x2, mod3, wb, seq)
    rest = _inproj_rest(u1, wb)

    outs, lses = zip(*[_attention(qkv, g, batch, seq) for g in range(ATTN_GROUPS)])
    decays = jnp.concatenate([decay_f, decay_b]).astype(F32)
    retg = _retention(rest, decays, gn_gain.reshape(1, -1), batch, seq)

    w_route = jnp.concatenate([w_coarse, w_fine.transpose(1, 0, 2).reshape(d, N_EXPERTS)], axis=1)
    n_route = w_route.shape[1]
    w_route = jnp.pad(w_route, ((0, 0), (0, LANES - n_route)))
    wr_hi, wr_lo = _split_bf16(w_route)
    rbias = jnp.pad(jnp.concatenate([b_coarse, b_fine.reshape(-1)]), (0, LANES - n_route)).reshape(1, LANES)
    ln1 = jnp.stack([ln1_gain, ln1_bias])
    n_blocks = 2 * t // MOE_BLK + N_EXPERTS
    h1, u2, logits, rows0 = _merge(outs, lses, retg, rest, x2, mod3, w_attn_out.astype(BF16),
                                   w_ret_out.astype(BF16), w_out.astype(BF16), ln1, wr_hi, wr_lo, rbias, seq,
                                   n_blocks * MOE_BLK)

    cols, ints, counts = _route(logits)
    n_blocks_pad = -(-n_blocks // LANES) * LANES
    dest, meta = _plan(ints, counts, n_blocks_pad)
    dest0, dest1 = dest[0], dest[1]
    rows = _dispatch(dest0, dest1, u2, rows0)
    y_rows = _experts(meta[0, :n_blocks], meta[1, :1], rows, w1, w3, w2)
    out = _combine(dest0, dest1, y_rows, cols, h1, mod3, jnp.stack([ln2_gain, ln2_bias]), seq)
    return out.reshape(batch, seq, d)


def kernel(x, c, w_ada, b_ada, w_in, w_attn_out, ret_decay_fwd, ret_decay_bwd, ret_gn_gain, w_ret_out, w_out,
           ln1_gain, ln1_bias, w_coarse, b_coarse, w_fine, b_fine, w1, w3, w2, ln2_gain, ln2_bias):
    batch = x.shape[0]
    assert batch <= 8 and x.shape[1] % (2 * INPROJ_TM) == 0 and x.shape[2] == D_MODEL
    c8 = jnp.pad(c, ((0, 8 - batch), (0, 0)))
    h = x
    for l in range(w_ada.shape[0]):
        h = _layer(h, c8, w_ada[l], b_ada[l], w_in[l], w_attn_out[l], ret_decay_fwd[l], ret_decay_bwd[l],
                   ret_gn_gain[l], w_ret_out[l], w_out[l], ln1_gain[l], ln1_bias[l], w_coarse[l], b_coarse[l],
                   w_fine[l], b_fine[l], w1[l], w3[l], w2[l], ln2_gain[l], ln2_bias[l])
    return h
```

```python
import functools
import math

import jax
import jax.numpy as jnp
from jax import lax
from jax.experimental import pallas as pl
from jax.experimental.pallas import tpu as pltpu

F32 = jnp.float32
BF16 = jnp.bfloat16
I32 = jnp.int32

D_MODEL = 1024
ATTN_GROUPS = 3
HEADS_PER_GROUP = 4
HEAD_DIM = 128
ATTN_HEADS = ATTN_GROUPS * HEADS_PER_GROUP
GROUP_WIDTH = HEADS_PER_GROUP * HEAD_DIM
ATTN_PATTERNS = ((128, 1), (512, 4), (2048, 16))
ALIBI_MAX_EXP = 8.0
NEG = -1e30
RET_HEADS = 4
RET_QK = 256
RET_V = 512
N_GROUPS = 4
EXPERTS_PER_GROUP = 8
N_EXPERTS = N_GROUPS * EXPERTS_PER_GROUP
EXPERT_FF = 512
DEPTH = 1
DEEPNORM_ALPHA = (2.0 * DEPTH) ** 0.25
LN_EPS = 1e-5

LANES = 128
PERM_TILE = 512
ATTN_QB = 128
ATTN_HALF = 64
ATTN_KB = ATTN_QB + 2 * ATTN_HALF
RET_CHUNK = 256
RET_STEP = 2048
MOE_BLK = 512
MOE_SUB = 256
VMEM_LIMIT = 56 * 1024 * 1024


def _cparams(sem):
    return pltpu.CompilerParams(dimension_semantics=sem, vmem_limit_bytes=VMEM_LIMIT)


def _split_bf16(a):
    hi = a.astype(BF16)
    lo = (a - hi.astype(F32)).astype(BF16)
    return hi, lo


def _dot(a, b):
    return jnp.dot(a, b, preferred_element_type=F32)


def _dot3(a, b):
    ah, al = _split_bf16(a)
    bh, bl = _split_bf16(b)
    return _dot(ah, bh) + _dot(ah, bl) + _dot(al, bh)


def _ln(x):
    mu = jnp.mean(x, axis=-1, keepdims=True)
    xc = x - mu
    var = jnp.mean(xc * xc, axis=-1, keepdims=True)
    return xc * lax.rsqrt(var + LN_EPS)


def _sigmoid(x):
    return 1.0 / (1.0 + jnp.exp(-x))


ROW_TILE = D_MODEL // LANES


def _store_token_tiles(ref, val, row0=0):
    n = val.shape[0]
    for cc in range(ROW_TILE):
        ref[pl.ds(row0 * ROW_TILE + cc, n, stride=ROW_TILE), :] = val[:, cc * LANES:(cc + 1) * LANES]


def _load_token_tiles(ref, n, row0=0):
    return jnp.concatenate([ref[pl.ds(row0 * ROW_TILE + cc, n, stride=ROW_TILE), :] for cc in range(ROW_TILE)],
                           axis=1)


def _interleave(phased):
    pending = []
    for gen in phased:
        pending.append(gen)
        pending = [g for g in pending if next(g, StopIteration) is not StopIteration]
    while pending:
        pending = [g for g in pending if next(g, StopIteration) is not StopIteration]


def _ada_kernel(c_ref, w_ref, b_ref, o_ref):
    o_ref[...] = _dot3(c_ref[...], w_ref[...]) + b_ref[...]


def _ada(c8, w_ada, b_ada):
    n = w_ada.shape[1]
    return pl.pallas_call(
        _ada_kernel,
        grid=(n // D_MODEL,),
        in_specs=[pl.BlockSpec((8, D_MODEL), lambda j: (0, 0)),
                  pl.BlockSpec((D_MODEL, D_MODEL), lambda j: (0, j)),
                  pl.BlockSpec((1, D_MODEL), lambda j: (0, j))],
        out_specs=pl.BlockSpec((8, D_MODEL), lambda j: (0, j)),
        out_shape=jax.ShapeDtypeStruct((8, n), F32),
        compiler_params=_cparams(("arbitrary",)),
        name="ada",
    )(c8, w_ada, b_ada)


INPROJ_TM = 1024
LN_ROWS = 256
LN_UNROLL = 4


def _modulated_ln_rows(x_ref, mod_ref, emit):
    gain = 1.0 + mod_ref[1:2, :]
    shift = mod_ref[0:1, :]

    def chunk(c, carry):
        r0 = pl.multiple_of(c * LN_ROWS, LN_ROWS)
        emit(r0, _ln(x_ref[pl.ds(r0, LN_ROWS), :]) * gain + shift)
        return carry

    lax.fori_loop(0, INPROJ_TM // LN_ROWS, chunk, 0, unroll=LN_UNROLL)


def _inproj_attn_kernel(x_ref, mod_ref, wq_ref, wk_ref, wv_ref, o_ref, un_ref, uf_scr, u_scr):
    j = pl.program_id(1)

    @pl.when(j == 0)
    def _():
        def emit(r0, u):
            for cc in range(D_MODEL // LANES):
                uf_scr[cc, pl.ds(r0, LN_ROWS), :] = u[:, cc * LANES:(cc + 1) * LANES]
            ub = u.astype(BF16)
            u_scr[0, pl.ds(r0, LN_ROWS), :] = ub
            un_ref[pl.ds(r0, LN_ROWS), :] = ub

        _modulated_ln_rows(x_ref, mod_ref, emit)
        for g in (1, 2):
            dil = ATTN_PATTERNS[g][1]
            n = PERM_TILE // dil
            for t0 in range(0, INPROJ_TM, PERM_TILE):
                for res in range(dil):
                    for cc in range(D_MODEL // LANES):
                        rows = uf_scr[cc, pl.ds(t0 + res, n, stride=dil), :]
                        u_scr[g, t0 + res * n:t0 + (res + 1) * n, cc * LANES:(cc + 1) * LANES] = rows.astype(BF16)

    u = u_scr[j]
    for s, w_ref in enumerate((wq_ref, wk_ref, wv_ref)):
        o_ref[:, s * GROUP_WIDTH:(s + 1) * GROUP_WIDTH] = _dot(u, w_ref[...]).astype(BF16)


ATTN_WIDTH = ATTN_HEADS * HEAD_DIM
W_RQ0 = 3 * ATTN_WIDTH
W_RK0 = W_RQ0 + RET_HEADS * RET_QK
W_RV0 = W_RK0 + RET_HEADS * RET_QK
W_TAIL0 = W_RV0 + RET_HEADS * RET_V


def _inproj_attn(x2, mod3, wb, seq):
    t = x2.shape[0]
    tn = 3 * GROUP_WIDTH
    per_seq = seq // INPROJ_TM
    groups_per_range = ATTN_WIDTH // GROUP_WIDTH
    wspec = lambda s: pl.BlockSpec((D_MODEL, GROUP_WIDTH), lambda i, j: (0, s * groups_per_range + j))
    return pl.pallas_call(
        _inproj_attn_kernel,
        grid=(t // INPROJ_TM, ATTN_GROUPS),
        in_specs=[pl.BlockSpec((INPROJ_TM, D_MODEL), lambda i, j: (i, 0)),
                  pl.BlockSpec((None, 6, D_MODEL), lambda i, j: (i // per_seq, 0, 0)),
                  wspec(0), wspec(1), wspec(2)],
        out_specs=[pl.BlockSpec((INPROJ_TM, tn), lambda i, j: (i, j)),
                   pl.BlockSpec((INPROJ_TM, D_MODEL), lambda i, j: (i, 0))],
        out_shape=[jax.ShapeDtypeStruct((t, ATTN_GROUPS * tn), BF16),
                   jax.ShapeDtypeStruct((t, D_MODEL), BF16)],
        scratch_shapes=[pltpu.VMEM((D_MODEL // LANES, INPROJ_TM, LANES), F32),
                        pltpu.VMEM((ATTN_GROUPS, INPROJ_TM, D_MODEL), BF16)],
        compiler_params=_cparams(("arbitrary", "arbitrary")),
        name="inproj_attn",
    )(x2, mod3, wb, wb, wb)


REST_TN = RET_HEAD_COLS = 2 * RET_QK + RET_V
REST_TAIL_TILES = (RET_HEADS * RET_V + 2 * D_MODEL) // REST_TN
REST_TM = 2048


def _inproj_rest_kernel(u_ref, wq_ref, wk_ref, wv_ref, wt0_ref, wt1_ref, o_ref):
    j = pl.program_id(0)

    def project(parts):
        u = u_ref[...]
        c0 = 0
        for w_ref in parts:
            wd = w_ref.shape[1]
            o_ref[:, c0:c0 + wd] = _dot(u, w_ref[...]).astype(BF16)
            c0 += wd

    @pl.when(j < RET_HEADS)
    def _():
        project((wq_ref, wk_ref, wv_ref))

    @pl.when(j >= RET_HEADS)
    def _():
        project((wt0_ref, wt1_ref))


def _inproj_rest(u, wb):
    t = u.shape[0]
    half = REST_TN // 2
    assert W_TAIL0 % half == 0 and W_RV0 % RET_V == 0
    head = lambda j: jnp.minimum(j, RET_HEADS - 1)
    tail = lambda j: jnp.maximum(j - RET_HEADS, 0)
    return pl.pallas_call(
        _inproj_rest_kernel,
        grid=(RET_HEADS + REST_TAIL_TILES, t // REST_TM),
        in_specs=[pl.BlockSpec((REST_TM, D_MODEL), lambda j, i: (i, 0)),
                  pl.BlockSpec((D_MODEL, RET_QK), lambda j, i: (0, W_RQ0 // RET_QK + head(j))),
                  pl.BlockSpec((D_MODEL, RET_QK), lambda j, i: (0, W_RK0 // RET_QK + head(j))),
                  pl.BlockSpec((D_MODEL, RET_V), lambda j, i: (0, W_RV0 // RET_V + head(j))),
                  pl.BlockSpec((D_MODEL, half), lambda j, i: (0, W_TAIL0 // half + 2 * tail(j))),
                  pl.BlockSpec((D_MODEL, half), lambda j, i: (0, W_TAIL0 // half + 2 * tail(j) + 1))],
        out_specs=pl.BlockSpec((REST_TM, REST_TN), lambda j, i: (i, j)),
        out_shape=jax.ShapeDtypeStruct((t, (RET_HEADS + REST_TAIL_TILES) * REST_TN), BF16),
        compiler_params=_cparams(("arbitrary", "arbitrary")),
        name="inproj_rest",
    )(u, wb, wb, wb, wb, wb)


ATTN_OFFSETS = ATTN_KB // ATTN_HALF - 1
ATTN_UNROLL = 4
ATTN_MIN_BLOCKS = 4


def _attn_kernel(q_ref, k_ref, v_ref, o_ref, lse_ref, bias_scr, *scratch, group, dil, nt, n, nres):
    n_sub = nt * n
    if nt == 1:
        seqs = [(q_ref.at[0, rr], k_ref.at[0, rr], v_ref.at[0, rr], o_ref.at[0, rr], lse_ref.at[0, rr])
                for rr in range(nres)]
    else:
        seqs = [tuple(s.at[rr] for s in scratch) for rr in range(nres)]
        for rr, (qs, ks, vs, _, _) in enumerate(seqs):
            for t in range(nt):
                qs[t * n:(t + 1) * n, :] = q_ref[t, rr]
                ks[t * n:(t + 1) * n, :] = k_ref[t, rr]
                vs[t * n:(t + 1) * n, :] = v_ref[t, rr]

    @pl.when((pl.program_id(0) == 0) & (pl.program_id(1) == 0))
    def _():
        base = (lax.broadcasted_iota(I32, (ATTN_QB, ATTN_KB), 1)
                - lax.broadcasted_iota(I32, (ATTN_QB, ATTN_KB), 0))
        for j in range(ATTN_OFFSETS):
            dist = jnp.abs(base - j * ATTN_HALF)
            for hh in range(HEADS_PER_GROUP):
                slope = dil * 2.0 ** (-ALIBI_MAX_EXP * (group * HEADS_PER_GROUP + hh + 1) / ATTN_HEADS)
                bias_scr[hh * ATTN_OFFSETS + j] = jnp.where(dist <= ATTN_HALF, -slope * dist.astype(F32), NEG)

    scale = HEAD_DIM ** -0.5
    lane = lax.broadcasted_iota(I32, (ATTN_QB, LANES), 1)

    def block(blk, carry):
        q0 = pl.multiple_of(blk * ATTN_QB, ATTN_QB)
        start = pl.multiple_of(jnp.clip(q0 - ATTN_HALF, 0, n_sub - ATTN_KB), ATTN_HALF)
        j = (q0 - start) // ATTN_HALF
        for qs, ks, vs, os_, ls in seqs:
            lse_tile = jnp.zeros((ATTN_QB, LANES), F32)
            for hh in range(HEADS_PER_GROUP):
                cs = slice(hh * HEAD_DIM, (hh + 1) * HEAD_DIM)
                qb = qs[pl.ds(q0, ATTN_QB), cs]
                kb = ks[pl.ds(start, ATTN_KB), cs]
                vb = vs[pl.ds(start, ATTN_KB), cs]
                s = lax.dot_general(qb, kb, (((1,), (1,)), ((), ())), preferred_element_type=F32) * scale
                s = s + bias_scr[hh * ATTN_OFFSETS + j]
                m = jnp.max(s, axis=-1, keepdims=True)
                p = jnp.exp(s - m)
                l = jnp.sum(p, axis=-1, keepdims=True)
                o = _dot(p.astype(BF16), vb) * (1.0 / l)
                os_[pl.ds(q0, ATTN_QB), cs] = o.astype(BF16)
                lse_tile = jnp.where(lane == hh, m + jnp.log(l), lse_tile)
            ls[pl.ds(q0, ATTN_QB), :] = lse_tile
        return carry

    n_blk = n_sub // ATTN_QB
    lax.fori_loop(0, n_blk, block, 0, unroll=min(max(ATTN_UNROLL // nres, 1), n_blk))

    if nt > 1:
        for rr, (_, _, _, os_, ls) in enumerate(seqs):
            for t in range(nt):
                o_ref[t, rr] = os_[t * n:(t + 1) * n, :]
                lse_ref[t, rr] = ls[t * n:(t + 1) * n, :]


def _attention(qkv, group, batch, seq):
    dil = ATTN_PATTERNS[group][1]
    if dil == 1:
        nt, n = 1, seq
    else:
        nt, n = seq // PERM_TILE, PERM_TILE // dil
    n_sub = nt * n
    t = batch * seq
    qkv5 = qkv.reshape(batch, nt, dil, n, qkv.shape[1])
    cb = group * 3
    nres = min(dil, max(1, ATTN_MIN_BLOCKS * ATTN_QB // n_sub))
    blk = (None, nt, nres, n, GROUP_WIDTH)
    scratch = [pltpu.VMEM((HEADS_PER_GROUP * ATTN_OFFSETS, ATTN_QB, ATTN_KB), F32)]
    if nt > 1:
        scratch += [pltpu.VMEM((nres, n_sub, GROUP_WIDTH), BF16)] * 4 + [pltpu.VMEM((nres, n_sub, LANES), F32)]
    out, lse = pl.pallas_call(
        functools.partial(_attn_kernel, group=group, dil=dil, nt=nt, n=n, nres=nres),
        grid=(batch, dil // nres),
        in_specs=[pl.BlockSpec(blk, lambda b, r: (b, 0, r, 0, cb)),
                  pl.BlockSpec(blk, lambda b, r: (b, 0, r, 0, cb + 1)),
                  pl.BlockSpec(blk, lambda b, r: (b, 0, r, 0, cb + 2))],
        out_specs=[pl.BlockSpec(blk, lambda b, r: (b, 0, r, 0, 0)),
                   pl.BlockSpec((None, nt, nres, n, LANES), lambda b, r: (b, 0, r, 0, 0))],
        out_shape=[jax.ShapeDtypeStruct((batch, nt, dil, n, GROUP_WIDTH), BF16),
                   jax.ShapeDtypeStruct((batch, nt, dil, n, LANES), F32)],
        scratch_shapes=scratch,
        compiler_params=_cparams(("arbitrary", "arbitrary")),
        name=f"attn_g{group}",
    )(qkv5, qkv5, qkv5)
    return out.reshape(t, GROUP_WIDTH), lse.reshape(t, LANES)


def _log_sigmoid(x):
    return jnp.minimum(x, 0.0) - jnp.log(1.0 + jnp.exp(-jnp.abs(x)))


def _retention_kernel(decay_ref, qkv_ref, g_ref, gain_ref, o_ref, state, ybwd, dmat, kdec, qdec, *, nc):
    c = RET_CHUNK
    h = pl.program_id(1)
    i = pl.program_id(2)
    kscale = RET_QK ** -0.5

    def set_decays(lg, forward):
        row = lax.broadcasted_iota(I32, (c, c), 0)
        col = lax.broadcasted_iota(I32, (c, c), 1)
        pos = lax.broadcasted_iota(I32, (c, LANES), 0).astype(F32)
        if forward:
            gap, key_pow, query_pow = row - col, (c - 1.0) - pos, pos + 1.0
            keep = gap >= 0
        else:
            gap, key_pow, query_pow = col - row, pos, c - pos
            keep = gap > 0
        dmat[...] = jnp.where(keep, jnp.exp(lg * jnp.maximum(gap, 0).astype(F32)) * kscale, 0.0)
        kdec[...] = jnp.exp(lg * key_pow) * kscale
        qdec[...] = jnp.exp(lg * query_pow)

    def chunk_step(lg, r0, finish):
        q = qkv_ref[r0:r0 + c, 0:RET_QK]
        k = qkv_ref[r0:r0 + c, RET_QK:2 * RET_QK]
        v = qkv_ref[r0:r0 + c, 2 * RET_QK:]
        inner = lax.dot_general(q, k, (((1,), (1,)), ((), ())), preferred_element_type=F32) * dmat[...]
        y = _dot(inner.astype(BF16), v)
        kd = (k.astype(F32) * jnp.concatenate([kdec[...]] * (RET_QK // LANES), axis=1)).astype(BF16)
        kv = lax.dot_general(kd, v, (((0,), (0,)), ((), ())), preferred_element_type=F32)
        yield
        qd = jnp.concatenate([qdec[...]] * (RET_V // LANES), axis=1)
        y = y + _dot(q, state[...].astype(BF16)) * qd
        state[...] = state[...] * jnp.exp(lg * float(c)) + kv
        yield
        finish(r0, y)

    @pl.when((i == 0) | (i == nc))
    def _():
        state[...] = jnp.zeros_like(state)

    @pl.when(i < nc)
    def _():
        lg = _log_sigmoid(jnp.zeros((1, 1), F32) + decay_ref[RET_HEADS + h])

        @pl.when(i == 0)
        def _():
            set_decays(lg, False)

        base = pl.multiple_of((nc - 1 - i) * RET_STEP, RET_STEP)

        def finish(r0, y):
            ybwd[pl.ds(base + r0, c), :] = y

        _interleave(chunk_step(lg, r0, finish) for r0 in reversed(range(0, RET_STEP, c)))

    @pl.when(i >= nc)
    def _():
        lg = _log_sigmoid(jnp.zeros((1, 1), F32) + decay_ref[h])

        @pl.when(i == nc)
        def _():
            set_decays(lg, True)

        base = pl.multiple_of((i - nc) * RET_STEP, RET_STEP)

        def finish(r0, y):
            y = y + ybwd[pl.ds(base + r0, c), :]
            g = g_ref[r0:r0 + c, :].astype(F32)
            o_ref[r0:r0 + c, :] = (g * _sigmoid(g) * (_ln(y) * gain_ref[...])).astype(BF16)

        _interleave(chunk_step(lg, r0, finish) for r0 in range(0, RET_STEP, c))


def _retention(rest, decays, gn_gain, batch, seq):
    c = RET_STEP
    nc = seq // c
    rest3 = rest.reshape(batch, seq, rest.shape[1])
    gate_blk = RET_HEADS * RET_HEAD_COLS // RET_V

    def chunk(i):
        return jnp.where(i < nc, nc - 1 - i, i - nc)

    grid_spec = pltpu.PrefetchScalarGridSpec(
        num_scalar_prefetch=1,
        grid=(batch, RET_HEADS, 2 * nc),
        in_specs=[pl.BlockSpec((None, c, RET_HEAD_COLS), lambda b, h, i, d: (b, chunk(i), h)),
                  pl.BlockSpec((None, c, RET_V), lambda b, h, i, d: (b, jnp.maximum(i - nc, 0), gate_blk + h)),
                  pl.BlockSpec((1, RET_V), lambda b, h, i, d: (0, h))],
        out_specs=pl.BlockSpec((None, c, RET_V), lambda b, h, i, d: (b, jnp.maximum(i - nc, 0), h)),
        scratch_shapes=[pltpu.VMEM((RET_QK, RET_V), F32), pltpu.VMEM((seq, RET_V), F32),
                        pltpu.VMEM((RET_CHUNK, RET_CHUNK), F32), pltpu.VMEM((RET_CHUNK, LANES), F32),
                        pltpu.VMEM((RET_CHUNK, LANES), F32)],
    )
    out = pl.pallas_call(
        functools.partial(_retention_kernel, nc=nc),
        grid_spec=grid_spec,
        out_shape=jax.ShapeDtypeStruct((batch, seq, RET_HEADS * RET_V), BF16),
        compiler_params=_cparams(("arbitrary", "arbitrary", "arbitrary")),
        name="retention",
    )(decays, rest3, rest3, gn_gain)
    return out.reshape(batch * seq, RET_HEADS * RET_V)


MERGE_TM = PERM_TILE
MERGE_SUB = 256
ZERO_SPLIT = 4


def _merge_kernel(o0_ref, o1_ref, o2_ref, l0_ref, l1_ref, l2_ref, retg_ref, ga_ref, gr_ref, x_ref, mod_ref,
                  wa_ref, wr_ref, wo_ref, ln_ref, wrh_ref, wrl_ref, rb_ref,
                  h1_ref, u2_ref, lg_ref, zrows_ref, on_scr, ln_scr, zbuf, zsem):
    i = pl.program_id(0)

    @pl.when(i == 0)
    def _():
        zbuf[...] = jnp.zeros_like(zbuf)

    zrows = zbuf.shape[0]
    zero_copies = [pltpu.make_async_copy(zbuf, zrows_ref.at[pl.ds((i * ZERO_SPLIT + k) * zrows, zrows)], zsem)
                   for k in range(ZERO_SPLIT)]
    for cp in zero_copies:
        cp.start()

    for g in (1, 2):
        dil = ATTN_PATTERNS[g][1]
        n = MERGE_TM // dil
        o_ref, l_ref = ((o1_ref, l1_ref), (o2_ref, l2_ref))[g - 1]
        for res in range(dil):
            rows = o_ref[res * n:(res + 1) * n, :].astype(F32)
            for hh in range(HEADS_PER_GROUP):
                on_scr[g - 1, hh, pl.ds(res, n, stride=dil), :] = rows[:, hh * HEAD_DIM:(hh + 1) * HEAD_DIM]
            ln_scr[g - 1, pl.ds(res, n, stride=dil), :] = l_ref[res * n:(res + 1) * n, :]

    def sub_tile(r0):
        rs = slice(r0, r0 + MERGE_SUB)
        l0, l1, l2 = l0_ref[rs, :], ln_scr[0, rs, :], ln_scr[1, rs, :]
        lm = jnp.maximum(jnp.maximum(l0, l1), l2)
        e0, e1, e2 = jnp.exp(l0 - lm), jnp.exp(l1 - lm), jnp.exp(l2 - lm)
        inv = 1.0 / (e0 + e1 + e2)
        parts = []
        for hh in range(HEADS_PER_GROUP):
            sl = slice(hh * HEAD_DIM, (hh + 1) * HEAD_DIM)
            acc = (e0[:, hh:hh + 1] * o0_ref[rs, sl].astype(F32)
                   + e1[:, hh:hh + 1] * on_scr[0, hh, rs, :]
                   + e2[:, hh:hh + 1] * on_scr[1, hh, rs, :])
            parts.append((acc * inv[:, hh:hh + 1]).astype(BF16))
        attn = jnp.concatenate(parts, axis=1)
        yield

        branch_a = _dot(attn, wa_ref[...])
        branch_r = _dot(retg_ref[rs, :], wr_ref[...])
        yield
        merged = (_sigmoid(ga_ref[rs, :].astype(F32)) * branch_a
                  + _sigmoid(gr_ref[rs, :].astype(F32)) * branch_r)
        yield
        y = _dot(merged.astype(BF16), wo_ref[...])
        yield

        h1 = _ln(DEEPNORM_ALPHA * x_ref[rs, :] + mod_ref[2:3, :] * y) * ln_ref[0:1, :] + ln_ref[1:2, :]
        h1_ref[rs, :] = h1
        u2 = _ln(h1) * (1.0 + mod_ref[4:5, :]) + mod_ref[3:4, :]
        _store_token_tiles(u2_ref, u2, r0)
        uh, ul = _split_bf16(u2)
        yield
        lg_ref[rs, :] = (_dot(uh, wrh_ref[...]) + _dot(uh, wrl_ref[...]) + _dot(ul, wrh_ref[...])
                         + rb_ref[...])

    _interleave(sub_tile(r0) for r0 in range(0, MERGE_TM, MERGE_SUB))

    for cp in zero_copies:
        cp.wait()


def _merge(outs, lses, retg, rest, x2, mod3, wa, wr, wo, ln1, wr_hi, wr_lo, rbias, seq, n_rows):
    t = x2.shape[0]
    tm = MERGE_TM
    per_seq = seq // tm
    zrows, rem = divmod(n_rows * ROW_TILE, (t // tm) * ZERO_SPLIT)
    assert rem == 0 and zrows % 8 == 0
    row = lambda w: pl.BlockSpec((tm, w), lambda i: (i, 0))
    full = lambda a: pl.BlockSpec(a.shape, lambda i: (0,) * a.ndim)
    return pl.pallas_call(
        _merge_kernel,
        grid=(t // tm,),
        in_specs=[row(GROUP_WIDTH)] * 3 + [row(LANES)] * 3 + [
            row(RET_HEADS * RET_V),
            pl.BlockSpec((tm, D_MODEL), lambda i: (i, 6)),
            pl.BlockSpec((tm, D_MODEL), lambda i: (i, 7)),
            row(D_MODEL),
            pl.BlockSpec((None, 6, D_MODEL), lambda i: (i // per_seq, 0, 0)),
            full(wa), full(wr), full(wo), full(ln1), full(wr_hi), full(wr_lo), full(rbias)],
        out_specs=[row(D_MODEL), pl.BlockSpec((tm * ROW_TILE, LANES), lambda i: (i, 0)), row(LANES),
                   pl.BlockSpec(memory_space=pl.ANY)],
        out_shape=[jax.ShapeDtypeStruct((t, D_MODEL), F32),
                   jax.ShapeDtypeStruct((t * ROW_TILE, LANES), F32),
                   jax.ShapeDtypeStruct((t, LANES), F32),
                   jax.ShapeDtypeStruct((n_rows * ROW_TILE, LANES), F32)],
        scratch_shapes=[pltpu.VMEM((2, HEADS_PER_GROUP, tm, HEAD_DIM), F32), pltpu.VMEM((2, tm, LANES), F32),
                        pltpu.VMEM((zrows, LANES), F32), pltpu.SemaphoreType.DMA(())],
        compiler_params=_cparams(("arbitrary",)),
        name="merge",
    )(*outs, *lses, retg, rest, rest, x2, mod3, wa, wr, wo, ln1, wr_hi, wr_lo, rbias)


ROUTE_TM = 512
BIG = 1 << 20


def _route_kernel(lg_ref, cols_ref, ints_ref, cnt_ref, carry):
    i = pl.program_id(0)

    @pl.when(i == 0)
    def _():
        carry[...] = jnp.zeros_like(carry)

    tm = ROUTE_TM
    lg = lg_ref[...]
    lane = lax.broadcasted_iota(I32, (tm, LANES), 1)
    lane_f = lane.astype(F32)
    first = lambda mask: jnp.min(jnp.where(mask, lane_f, float(BIG)), axis=-1, keepdims=True).astype(I32)

    coarse = jnp.where(lane < N_GROUPS, lg, NEG)
    cmax = jnp.max(coarse, axis=-1, keepdims=True)
    gsel = first(coarse == cmax)
    p_group = 1.0 / jnp.sum(jnp.exp(coarse - cmax), axis=-1, keepdims=True)

    lo = N_GROUPS + EXPERTS_PER_GROUP * gsel
    fine = jnp.where((lane >= lo) & (lane < lo + EXPERTS_PER_GROUP), lg, NEG)
    v1 = jnp.max(fine, axis=-1, keepdims=True)
    i1 = first(fine == v1)
    fine2 = jnp.where(lane == i1, NEG, fine)
    v2 = jnp.max(fine2, axis=-1, keepdims=True)
    i2 = first(fine2 == v2)
    ex = jnp.exp(v2 - v1)
    den = 1.0 / (1.0 + ex)
    gate1 = p_group * den
    gate2 = p_group * (ex * den)
    e1 = i1 - N_GROUPS
    e2 = i2 - N_GROUPS

    oh1 = lane == e1
    oh2 = lane == e2
    cnt = jnp.where(oh1 | oh2, 1.0, 0.0)
    r_i = lax.broadcasted_iota(I32, (tm, tm), 0)
    c_i = lax.broadcasted_iota(I32, (tm, tm), 1)
    tri = jnp.where(r_i > c_i, 1.0, 0.0).astype(BF16)
    rank = _dot(tri, cnt.astype(BF16)) + carry[...]
    r1 = jnp.sum(jnp.where(oh1, rank, 0.0), axis=-1, keepdims=True)
    r2 = jnp.sum(jnp.where(oh2, rank, 0.0), axis=-1, keepdims=True)
    carry[...] = carry[...] + jnp.sum(cnt, axis=0, keepdims=True)
    cnt_ref[...] = jnp.broadcast_to(carry[...], cnt_ref.shape)

    cols_ref[...] = jnp.where(lane == 0, gate1, jnp.where(lane == 1, gate2, 0.0))
    packed = jnp.where(lane == 0, e1.astype(F32),
                       jnp.where(lane == 1, e2.astype(F32),
                                 jnp.where(lane == 2, r1, jnp.where(lane == 3, r2, 0.0))))
    ints_ref[...] = packed.T[0:8, :].astype(I32)


def _route(logits):
    t = logits.shape[0]
    tm = ROUTE_TM
    return pl.pallas_call(
        _route_kernel,
        grid=(t // tm,),
        in_specs=[pl.BlockSpec((tm, LANES), lambda i: (i, 0))],
        out_specs=[pl.BlockSpec((tm, LANES), lambda i: (i, 0)),
                   pl.BlockSpec((8, tm), lambda i: (0, i)),
                   pl.BlockSpec((8, LANES), lambda i: (0, 0))],
        out_shape=[jax.ShapeDtypeStruct((t, LANES), F32),
                   jax.ShapeDtypeStruct((8, t), I32),
                   jax.ShapeDtypeStruct((8, LANES), F32)],
        scratch_shapes=[pltpu.VMEM((1, LANES), F32)],
        compiler_params=_cparams(("arbitrary",)),
        name="route",
    )(logits)


def _plan_kernel(ints_ref, cnt_ref, dest_ref, meta_ref, *, n_blocks_pad):
    sub = lax.broadcasted_iota(I32, (LANES, LANES), 0)
    lane = lax.broadcasted_iota(I32, (LANES, LANES), 1)
    cnt = cnt_ref[0:1, :]
    nblk_row = jnp.floor((cnt + (MOE_BLK - 1.0)) * (1.0 / MOE_BLK))
    nblk_mat = jnp.broadcast_to(nblk_row, (LANES, LANES))
    start_col = jnp.sum(jnp.where(lane < sub, nblk_mat, 0.0), axis=-1, keepdims=True)
    nblk_col = jnp.sum(jnp.where(lane == sub, nblk_mat, 0.0), axis=-1, keepdims=True)
    end_col = start_col + nblk_col

    ints = ints_ref[...]
    base = jnp.zeros(ints.shape, F32)
    for e in range(N_EXPERTS):
        base = jnp.where(ints == e, start_col[e:e + 1, :] * float(MOE_BLK), base)
    dest = base[0:2, :].astype(I32) + ints[2:4, :]
    dest_ref[...] = jnp.concatenate([dest, jnp.zeros((6, ints.shape[1]), I32)], axis=0)

    blk = lax.broadcasted_iota(I32, (LANES, n_blocks_pad), 1).astype(F32)
    e_sub = lax.broadcasted_iota(I32, (LANES, n_blocks_pad), 0)
    done = jnp.where((e_sub < N_EXPERTS) & (end_col <= blk), 1.0, 0.0)
    bexp = jnp.minimum(jnp.sum(done, axis=0, keepdims=True), N_EXPERTS - 1.0)
    used = jnp.sum(nblk_row, axis=-1, keepdims=True)
    row = lax.broadcasted_iota(I32, (8, n_blocks_pad), 0)
    meta = jnp.where(row == 0, bexp, jnp.where(row == 1, used, 0.0))
    meta_ref[...] = meta.astype(I32)


def _plan(ints, counts, n_blocks_pad):
    t = ints.shape[1]
    return pl.pallas_call(
        functools.partial(_plan_kernel, n_blocks_pad=n_blocks_pad),
        out_shape=[jax.ShapeDtypeStruct((8, t), I32), jax.ShapeDtypeStruct((8, n_blocks_pad), I32)],
        compiler_params=pltpu.CompilerParams(vmem_limit_bytes=VMEM_LIMIT),
        name="plan",
    )(ints, counts)


DISPATCH_TM = 2048
DMA_UNROLL = 8


def _row_tile(ref, r):
    return ref.at[pl.ds(pl.multiple_of(r * ROW_TILE, ROW_TILE), ROW_TILE)]


def _dispatch_kernel(d0_ref, d1_ref, u2_ref, rows_in_ref, rows_ref, sem):
    del rows_in_ref

    def issue(t, carry):
        src = _row_tile(u2_ref, t)
        pltpu.make_async_copy(src, _row_tile(rows_ref, d0_ref[t]), sem).start(priority=0)
        pltpu.make_async_copy(src, _row_tile(rows_ref, d1_ref[t]), sem).start(priority=1)
        return carry

    lax.fori_loop(0, DISPATCH_TM, issue, 0, unroll=DMA_UNROLL)
    for _ in range(2):
        pltpu.make_async_copy(u2_ref, rows_ref.at[pl.ds(0, DISPATCH_TM * ROW_TILE)], sem).wait()


def _dispatch(dest0, dest1, u2, rows0):
    t = u2.shape[0] // ROW_TILE
    idx = pl.BlockSpec((DISPATCH_TM,), lambda i: (i,), memory_space=pltpu.SMEM)
    return pl.pallas_call(
        _dispatch_kernel,
        grid=(t // DISPATCH_TM,),
        in_specs=[idx, idx,
                  pl.BlockSpec((DISPATCH_TM * ROW_TILE, LANES), lambda i: (i, 0)),
                  pl.BlockSpec(memory_space=pl.ANY)],
        out_specs=pl.BlockSpec(memory_space=pl.ANY),
        out_shape=jax.ShapeDtypeStruct(rows0.shape, F32),
        scratch_shapes=[pltpu.SemaphoreType.DMA(())],
        input_output_aliases={3: 0},
        compiler_params=_cparams(("arbitrary",)),
        name="dispatch",
    )(dest0, dest1, u2, rows0)


def _expert_runs(bexp, used):
    n = bexp.shape[0]
    idx = jnp.arange(n, dtype=I32)
    first = (idx < used[0]) & ((idx == 0) | (bexp != jnp.roll(bexp, 1)))
    slot = (jnp.cumsum(first.astype(I32)) - 1) % 2
    first_at_or_after = lax.cummin(jnp.where(first, idx, n)[::-1])[::-1]
    first_after = jnp.concatenate([first_at_or_after[1:], jnp.full((1,), n, I32)])
    nxt = jnp.where(first_after < n, bexp[jnp.minimum(first_after, n - 1)], -1)
    return first.astype(I32), slot.astype(I32), nxt.astype(I32)


def _experts_kernel(bexp_ref, used_ref, first_ref, slot_ref, nxt_ref, x_ref, w1_hbm, w3_hbm, w2_hbm, y_ref,
                    wb1, wb3, wb2, w1s, w3s, w2s, sem):
    i = pl.program_id(0)
    active = i < used_ref[0]
    slot = slot_ref[i]

    def fetch(e, s):
        return [pltpu.make_async_copy(w.at[e], buf.at[s], sem.at[s])
                for w, buf in ((w1_hbm, wb1), (w3_hbm, wb3), (w2_hbm, wb2))]

    @pl.when(i == 0)
    def _():
        for cp in fetch(bexp_ref[0], 0):
            cp.start()

    @pl.when(first_ref[i] == 1)
    def _():
        @pl.when(nxt_ref[i] >= 0)
        def _():
            for cp in fetch(nxt_ref[i], 1 - slot):
                cp.start()

        for cp in fetch(bexp_ref[i], slot):
            cp.wait()
        w1s[...] = wb1[slot].astype(BF16)
        w3s[...] = wb3[slot].astype(BF16)
        w2s[...] = wb2[slot].astype(BF16)

    @pl.when(active)
    def _():
        def sub_block(r0):
            xb = _load_token_tiles(x_ref, MOE_SUB, r0).astype(BF16)
            yield
            a = _dot(xb, w1s[...])
            b = _dot(xb, w3s[...])
            yield
            hdn = (a * _sigmoid(a) * b).astype(BF16)
            yield
            _store_token_tiles(y_ref, _dot(hdn, w2s[...]), r0)

        _interleave(sub_block(r0) for r0 in range(0, MOE_BLK, MOE_SUB))

    @pl.when(jnp.logical_not(active))
    def _():
        y_ref[...] = jnp.zeros_like(y_ref)


def _experts(bexp, used, rows, w1, w3, w2):
    n_blocks = rows.shape[0] // (MOE_BLK * ROW_TILE)
    first, slot, nxt = _expert_runs(bexp, used)
    any_space = pl.BlockSpec(memory_space=pl.ANY)
    grid_spec = pltpu.PrefetchScalarGridSpec(
        num_scalar_prefetch=5,
        grid=(n_blocks,),
        in_specs=[pl.BlockSpec((MOE_BLK * ROW_TILE, LANES), lambda i, be, nu, *_: (jnp.minimum(i, nu[0] - 1), 0)),
                  any_space, any_space, any_space],
        out_specs=pl.BlockSpec((MOE_BLK * ROW_TILE, LANES), lambda i, *_: (i, 0)),
        scratch_shapes=[pltpu.VMEM((2, D_MODEL, EXPERT_FF), F32), pltpu.VMEM((2, D_MODEL, EXPERT_FF), F32),
                        pltpu.VMEM((2, EXPERT_FF, D_MODEL), F32),
                        pltpu.VMEM((D_MODEL, EXPERT_FF), BF16), pltpu.VMEM((D_MODEL, EXPERT_FF), BF16),
                        pltpu.VMEM((EXPERT_FF, D_MODEL), BF16), pltpu.SemaphoreType.DMA((2,))],
    )
    return pl.pallas_call(
        _experts_kernel,
        grid_spec=grid_spec,
        out_shape=jax.ShapeDtypeStruct(rows.shape, F32),
        compiler_params=_cparams(("arbitrary",)),
        name="experts",
    )(bexp, used, first, slot, nxt, rows, w1, w3, w2)


COMBINE_TM = 256


def _combine_kernel(d0_ref, d1_ref, n0_ref, n1_ref, y_ref, cols_ref, h1_ref, mod_ref, ln_ref, o_ref, ya, yb, sem):
    i = pl.program_id(0)
    slot = i % 2

    def gather(i0_ref, i1_ref, s):
        def issue(t, carry):
            pltpu.make_async_copy(_row_tile(y_ref, i0_ref[t]), _row_tile(ya.at[s], t), sem.at[s]).start(priority=0)
            pltpu.make_async_copy(_row_tile(y_ref, i1_ref[t]), _row_tile(yb.at[s], t), sem.at[s]).start(priority=1)
            return carry

        lax.fori_loop(0, COMBINE_TM, issue, 0, unroll=DMA_UNROLL)

    @pl.when(i == 0)
    def _():
        gather(d0_ref, d1_ref, 0)

    @pl.when(i + 1 < pl.num_programs(0))
    def _():
        gather(n0_ref, n1_ref, 1 - slot)

    for buf in (ya, yb):
        pltpu.make_async_copy(y_ref.at[pl.ds(0, COMBINE_TM * ROW_TILE)], buf.at[slot], sem.at[slot]).wait()

    cols = cols_ref[...]
    moe = (cols[:, 0:1] * _load_token_tiles(ya.at[slot], COMBINE_TM)
           + cols[:, 1:2] * _load_token_tiles(yb.at[slot], COMBINE_TM))
    pre = DEEPNORM_ALPHA * h1_ref[...] + mod_ref[5:6, :] * moe
    o_ref[...] = _ln(pre) * ln_ref[0:1, :] + ln_ref[1:2, :]


def _combine(dest0, dest1, y_rows, cols, h1, mod3, ln2, seq):
    t = h1.shape[0]
    tm = COMBINE_TM
    per_seq = seq // tm
    idx = pl.BlockSpec((tm,), lambda i: (i,), memory_space=pltpu.SMEM)
    idx_next = pl.BlockSpec((tm,), lambda i: (jnp.minimum(i + 1, t // tm - 1),), memory_space=pltpu.SMEM)
    return pl.pallas_call(
        _combine_kernel,
        grid=(t // tm,),
        in_specs=[idx, idx, idx_next, idx_next,
                  pl.BlockSpec(memory_space=pl.ANY),
                  pl.BlockSpec((tm, LANES), lambda i: (i, 0)),
                  pl.BlockSpec((tm, D_MODEL), lambda i: (i, 0)),
                  pl.BlockSpec((None, 6, D_MODEL), lambda i: (i // per_seq, 0, 0)),
                  pl.BlockSpec((2, D_MODEL), lambda i: (0, 0))],
        out_specs=pl.BlockSpec((tm, D_MODEL), lambda i: (i, 0)),
        out_shape=jax.ShapeDtypeStruct((t, D_MODEL), F32),
        scratch_shapes=[pltpu.VMEM((2, tm * ROW_TILE, LANES), F32), pltpu.VMEM((2, tm * ROW_TILE, LANES), F32),
                        pltpu.SemaphoreType.DMA((2,))],
        compiler_params=_cparams(("arbitrary",)),
        name="combine",
    )(dest0, dest1, dest0, dest1, y_rows, cols, h1, mod3, ln2)


def _layer(h, c8, w_ada, b_ada, w_in, w_attn_out, decay_f, decay_b, gn_gain, w_ret_out, w_out,
           ln1_gain, ln1_bias, w_coarse, b_coarse, w_fine, b_fine, w1, w3, w2, ln2_gain, ln2_bias):
    batch, seq, d = h.shape
    t = batch * seq
    x2 = h.reshape(t, d)

    mod = _ada(c8, w_ada, b_ada.reshape(1, -1))
    mod3 = mod[:batch].reshape(batch, 6, d)

    wb = w_in.astype(BF16)
    qkv, u1 = _inproj_attn(x2, mod3, wb, seq)
    rest = _inproj_rest(u1, wb)

    outs, lses = zip(*[_attention(qkv, g, batch, seq) for g in range(ATTN_GROUPS)])
    decays = jnp.concatenate([decay_f, decay_b]).astype(F32)
    retg = _retention(rest, decays, gn_gain.reshape(1, -1), batch, seq)

    w_route = jnp.concatenate([w_coarse, w_fine.transpose(1, 0, 2).reshape(d, N_EXPERTS)], axis=1)
    n_route = w_route.shape[1]
    w_route = jnp.pad(w_route, ((0, 0), (0, LANES - n_route)))
    wr_hi, wr_lo = _split_bf16(w_route)
    rbias = jnp.pad(jnp.concatenate([b_coarse, b_fine.reshape(-1)]), (0, LANES - n_route)).reshape(1, LANES)
    ln1 = jnp.stack([ln1_gain, ln1_bias])
    n_blocks = 2 * t // MOE_BLK + N_EXPERTS
    h1, u2, logits, rows0 = _merge(outs, lses, retg, rest, x2, mod3, w_attn_out.astype(BF16),
                                   w_ret_out.astype(BF16), w_out.astype(BF16), ln1, wr_hi, wr_lo, rbias, seq,
                                   n_blocks * MOE_BLK)

    cols, ints, counts = _route(logits)
    n_blocks_pad = -(-n_blocks // LANES) * LANES
    dest, meta = _plan(ints, counts, n_blocks_pad)
    dest0, dest1 = dest[0], dest[1]
    rows = _dispatch(dest0, dest1, u2, rows0)
    y_rows = _experts(meta[0, :n_blocks], meta[1, :1], rows, w1, w3, w2)
    out = _combine(dest0, dest1, y_rows, cols, h1, mod3, jnp.stack([ln2_gain, ln2_bias]), seq)
    return out.reshape(batch, seq, d)


def kernel(x, c, w_ada, b_ada, w_in, w_attn_out, ret_decay_fwd, ret_decay_bwd, ret_gn_gain, w_ret_out, w_out,
           ln1_gain, ln1_bias, w_coarse, b_coarse, w_fine, b_fine, w1, w3, w2, ln2_gain, ln2_bias):
    batch = x.shape[0]
    assert batch <= 8 and x.shape[1] % (2 * INPROJ_TM) == 0 and x.shape[2] == D_MODEL
    c8 = jnp.pad(c, ((0, 8 - batch), (0, 0)))
    h = x
    for l in range(w_ada.shape[0]):
        h = _layer(h, c8, w_ada[l], b_ada[l], w_in[l], w_attn_out[l], ret_decay_fwd[l], ret_decay_bwd[l],
                   ret_gn_gain[l], w_ret_out[l], w_out[l], ln1_gain[l], ln1_bias[l], w_coarse[l], b_coarse[l],
                   w_fine[l], b_fine[l], w1[l], w3[l], w2[l], ln2_gain[l], ln2_bias[l])
    return h
```

```python
import functools
import math

import jax
import jax.numpy as jnp
from jax import lax
from jax.experimental import pallas as pl
from jax.experimental.pallas import tpu as pltpu

F32 = jnp.float32
BF16 = jnp.bfloat16
I32 = jnp.int32

D_MODEL = 1024
ATTN_GROUPS = 3
HEADS_PER_GROUP = 4
HEAD_DIM = 128
ATTN_HEADS = ATTN_GROUPS * HEADS_PER_GROUP
GROUP_WIDTH = HEADS_PER_GROUP * HEAD_DIM
ATTN_PATTERNS = ((128, 1), (512, 4), (2048, 16))
ALIBI_MAX_EXP = 8.0
NEG = -1e30
RET_HEADS = 4
RET_QK = 256
RET_V = 512
N_GROUPS = 4
EXPERTS_PER_GROUP = 8
N_EXPERTS = N_GROUPS * EXPERTS_PER_GROUP
EXPERT_FF = 512
DEPTH = 1
DEEPNORM_ALPHA = (2.0 * DEPTH) ** 0.25
LN_EPS = 1e-5

LANES = 128
PERM_TILE = 512
ATTN_QB = 128
ATTN_HALF = 64
ATTN_KB = ATTN_QB + 2 * ATTN_HALF
RET_CHUNK = 256
RET_STEP = 2048
MOE_BLK = 512
MOE_SUB = 256
VMEM_LIMIT = 56 * 1024 * 1024


def _cparams(sem):
    return pltpu.CompilerParams(dimension_semantics=sem, vmem_limit_bytes=VMEM_LIMIT)


def _split_bf16(a):
    hi = a.astype(BF16)
    lo = (a - hi.astype(F32)).astype(BF16)
    return hi, lo


def _dot(a, b):
    return jnp.dot(a, b, preferred_element_type=F32)


def _dot3(a, b):
    ah, al = _split_bf16(a)
    bh, bl = _split_bf16(b)
    return _dot(ah, bh) + _dot(ah, bl) + _dot(al, bh)


def _ln(x):
    mu = jnp.mean(x, axis=-1, keepdims=True)
    xc = x - mu
    var = jnp.mean(xc * xc, axis=-1, keepdims=True)
    return xc * lax.rsqrt(var + LN_EPS)


def _sigmoid(x):
    return 1.0 / (1.0 + jnp.exp(-x))


ROW_TILE = D_MODEL // LANES


def _store_token_tiles(ref, val, row0=0):
    n = val.shape[0]
    for cc in range(ROW_TILE):
        ref[pl.ds(row0 * ROW_TILE + cc, n, stride=ROW_TILE), :] = val[:, cc * LANES:(cc + 1) * LANES]


def _load_token_tiles(ref, n, row0=0):
    return jnp.concatenate([ref[pl.ds(row0 * ROW_TILE + cc, n, stride=ROW_TILE), :] for cc in range(ROW_TILE)],
                           axis=1)


def _interleave(phased):
    pending = []
    for gen in phased:
        pending.append(gen)
        pending = [g for g in pending if next(g, StopIteration) is not StopIteration]
    while pending:
        pending = [g for g in pending if next(g, StopIteration) is not StopIteration]


def _ada_kernel(c_ref, w_ref, b_ref, o_ref):
    o_ref[...] = _dot3(c_ref[...], w_ref[...]) + b_ref[...]


def _ada(c8, w_ada, b_ada):
    n = w_ada.shape[1]
    return pl.pallas_call(
        _ada_kernel,
        grid=(n // D_MODEL,),
        in_specs=[pl.BlockSpec((8, D_MODEL), lambda j: (0, 0)),
                  pl.BlockSpec((D_MODEL, D_MODEL), lambda j: (0, j)),
                  pl.BlockSpec((1, D_MODEL), lambda j: (0, j))],
        out_specs=pl.BlockSpec((8, D_MODEL), lambda j: (0, j)),
        out_shape=jax.ShapeDtypeStruct((8, n), F32),
        compiler_params=_cparams(("arbitrary",)),
        name="ada",
    )(c8, w_ada, b_ada)


INPROJ_TM = 1024
LN_ROWS = 256
LN_UNROLL = 4


def _modulated_ln_rows(x_ref, mod_ref, emit):
    gain = 1.0 + mod_ref[1:2, :]
    shift = mod_ref[0:1, :]

    def chunk(c, carry):
        r0 = pl.multiple_of(c * LN_ROWS, LN_ROWS)
        emit(r0, _ln(x_ref[pl.ds(r0, LN_ROWS), :]) * gain + shift)
        return carry

    lax.fori_loop(0, INPROJ_TM // LN_ROWS, chunk, 0, unroll=LN_UNROLL)


def _inproj_attn_kernel(x_ref, mod_ref, wq_ref, wk_ref, wv_ref, o_ref, un_ref, uf_scr, u_scr):
    j = pl.program_id(1)

    @pl.when(j == 0)
    def _():
        def emit(r0, u):
            for cc in range(D_MODEL // LANES):
                uf_scr[cc, pl.ds(r0, LN_ROWS), :] = u[:, cc * LANES:(cc + 1) * LANES]
            ub = u.astype(BF16)
            u_scr[0, pl.ds(r0, LN_ROWS), :] = ub
            un_ref[pl.ds(r0, LN_ROWS), :] = ub

        _modulated_ln_rows(x_ref, mod_ref, emit)
        for g in (1, 2):
            dil = ATTN_PATTERNS[g][1]
            n = PERM_TILE // dil
            for t0 in range(0, INPROJ_TM, PERM_TILE):
                for res in range(dil):
                    for cc in range(D_MODEL // LANES):
                        rows = uf_scr[cc, pl.ds(t0 + res, n, stride=dil), :]
                        u_scr[g, t0 + res * n:t0 + (res + 1) * n, cc * LANES:(cc + 1) * LANES] = rows.astype(BF16)

    u = u_scr[j]
    for s, w_ref in enumerate((wq_ref, wk_ref, wv_ref)):
        o_ref[:, s * GROUP_WIDTH:(s + 1) * GROUP_WIDTH] = _dot(u, w_ref[...]).astype(BF16)


ATTN_WIDTH = ATTN_HEADS * HEAD_DIM
W_RQ0 = 3 * ATTN_WIDTH
W_RK0 = W_RQ0 + RET_HEADS * RET_QK
W_RV0 = W_RK0 + RET_HEADS * RET_QK
W_TAIL0 = W_RV0 + RET_HEADS * RET_V


def _inproj_attn(x2, mod3, wb, seq):
    t = x2.shape[0]
    tn = 3 * GROUP_WIDTH
    per_seq = seq // INPROJ_TM
    groups_per_range = ATTN_WIDTH // GROUP_WIDTH
    wspec = lambda s: pl.BlockSpec((D_MODEL, GROUP_WIDTH), lambda i, j: (0, s * groups_per_range + j))
    return pl.pallas_call(
        _inproj_attn_kernel,
        grid=(t // INPROJ_TM, ATTN_GROUPS),
        in_specs=[pl.BlockSpec((INPROJ_TM, D_MODEL), lambda i, j: (i, 0)),
                  pl.BlockSpec((None, 6, D_MODEL), lambda i, j: (i // per_seq, 0, 0)),
                  wspec(0), wspec(1), wspec(2)],
        out_specs=[pl.BlockSpec((INPROJ_TM, tn), lambda i, j: (i, j)),
                   pl.BlockSpec((INPROJ_TM, D_MODEL), lambda i, j: (i, 0))],
        out_shape=[jax.ShapeDtypeStruct((t, ATTN_GROUPS * tn), BF16),
                   jax.ShapeDtypeStruct((t, D_MODEL), BF16)],
        scratch_shapes=[pltpu.VMEM((D_MODEL // LANES, INPROJ_TM, LANES), F32),
                        pltpu.VMEM((ATTN_GROUPS, INPROJ_TM, D_MODEL), BF16)],
        compiler_params=_cparams(("arbitrary", "arbitrary")),
        name="inproj_attn",
    )(x2, mod3, wb, wb, wb)


REST_TN = RET_HEAD_COLS = 2 * RET_QK + RET_V
REST_TAIL_TILES = (RET_HEADS * RET_V + 2 * D_MODEL) // REST_TN
REST_TM = 2048


def _inproj_rest_kernel(u_ref, wq_ref, wk_ref, wv_ref, wt0_ref, wt1_ref, o_ref):
    j = pl.program_id(0)

    def project(parts):
        u = u_ref[...]
        c0 = 0
        for w_ref in parts:
            wd = w_ref.shape[1]
            o_ref[:, c0:c0 + wd] = _dot(u, w_ref[...]).astype(BF16)
            c0 += wd

    @pl.when(j < RET_HEADS)
    def _():
        project((wq_ref, wk_ref, wv_ref))

    @pl.when(j >= RET_HEADS)
    def _():
        project((wt0_ref, wt1_ref))


def _inproj_rest(u, wb):
    t = u.shape[0]
    half = REST_TN // 2
    assert W_TAIL0 % half == 0 and W_RV0 % RET_V == 0
    head = lambda j: jnp.minimum(j, RET_HEADS - 1)
    tail = lambda j: jnp.maximum(j - RET_HEADS, 0)
    return pl.pallas_call(
        _inproj_rest_kernel,
        grid=(RET_HEADS + REST_TAIL_TILES, t // REST_TM),
        in_specs=[pl.BlockSpec((REST_TM, D_MODEL), lambda j, i: (i, 0)),
                  pl.BlockSpec((D_MODEL, RET_QK), lambda j, i: (0, W_RQ0 // RET_QK + head(j))),
                  pl.BlockSpec((D_MODEL, RET_QK), lambda j, i: (0, W_RK0 // RET_QK + head(j))),
                  pl.BlockSpec((D_MODEL, RET_V), lambda j, i: (0, W_RV0 // RET_V + head(j))),
                  pl.BlockSpec((D_MODEL, half), lambda j, i: (0, W_TAIL0 // half + 2 * tail(j))),
                  pl.BlockSpec((D_MODEL, half), lambda j, i: (0, W_TAIL0 // half + 2 * tail(j) + 1))],
        out_specs=pl.BlockSpec((REST_TM, REST_TN), lambda j, i: (i, j)),
        out_shape=jax.ShapeDtypeStruct((t, (RET_HEADS + REST_TAIL_TILES) * REST_TN), BF16),
        compiler_params=_cparams(("arbitrary", "arbitrary")),
        name="inproj_rest",
    )(u, wb, wb, wb, wb, wb)


ATTN_OFFSETS = ATTN_KB // ATTN_HALF - 1
ATTN_UNROLL = 8
ATTN_MIN_BLOCKS = 4


def _attn_kernel(q_ref, k_ref, v_ref, o_ref, lse_ref, bias_scr, *scratch, group, dil, nt, n, nres):
    n_sub = nt * n
    if nt == 1:
        seqs = [(q_ref.at[0, rr], k_ref.at[0, rr], v_ref.at[0, rr], o_ref.at[0, rr], lse_ref.at[0, rr])
                for rr in range(nres)]
    else:
        seqs = [tuple(s.at[rr] for s in scratch) for rr in range(nres)]
        for rr, (qs, ks, vs, _, _) in enumerate(seqs):
            for t in range(nt):
                qs[t * n:(t + 1) * n, :] = q_ref[t, rr]
                ks[t * n:(t + 1) * n, :] = k_ref[t, rr]
                vs[t * n:(t + 1) * n, :] = v_ref[t, rr]

    @pl.when((pl.program_id(0) == 0) & (pl.program_id(1) == 0))
    def _():
        base = (lax.broadcasted_iota(I32, (ATTN_QB, ATTN_KB), 1)
                - lax.broadcasted_iota(I32, (ATTN_QB, ATTN_KB), 0))
        for j in range(ATTN_OFFSETS):
            dist = jnp.abs(base - j * ATTN_HALF)
            for hh in range(HEADS_PER_GROUP):
                slope = dil * 2.0 ** (-ALIBI_MAX_EXP * (group * HEADS_PER_GROUP + hh + 1) / ATTN_HEADS)
                bias_scr[hh * ATTN_OFFSETS + j] = jnp.where(dist <= ATTN_HALF, -slope * dist.astype(F32), NEG)

    scale = HEAD_DIM ** -0.5
    lane = lax.broadcasted_iota(I32, (ATTN_QB, LANES), 1)

    def block(blk, carry):
        q0 = pl.multiple_of(blk * ATTN_QB, ATTN_QB)
        start = pl.multiple_of(jnp.clip(q0 - ATTN_HALF, 0, n_sub - ATTN_KB), ATTN_HALF)
        j = (q0 - start) // ATTN_HALF
        for qs, ks, vs, os_, ls in seqs:
            lse_tile = jnp.zeros((ATTN_QB, LANES), F32)
            for hh in range(HEADS_PER_GROUP):
                cs = slice(hh * HEAD_DIM, (hh + 1) * HEAD_DIM)
                qb = qs[pl.ds(q0, ATTN_QB), cs]
                kb = ks[pl.ds(start, ATTN_KB), cs]
                vb = vs[pl.ds(start, ATTN_KB), cs]
                s = lax.dot_general(qb, kb, (((1,), (1,)), ((), ())), preferred_element_type=F32) * scale
                s = s + bias_scr[hh * ATTN_OFFSETS + j]
                m = jnp.max(s, axis=-1, keepdims=True)
                p = jnp.exp(s - m)
                l = jnp.sum(p, axis=-1, keepdims=True)
                o = _dot(p.astype(BF16), vb) * (1.0 / l)
                os_[pl.ds(q0, ATTN_QB), cs] = o.astype(BF16)
                lse_tile = jnp.where(lane == hh, m + jnp.log(l), lse_tile)
            ls[pl.ds(q0, ATTN_QB), :] = lse_tile
        return carry

    n_blk = n_sub // ATTN_QB
    lax.fori_loop(0, n_blk, block, 0, unroll=min(max(ATTN_UNROLL // nres, 1), n_blk))

    if nt > 1:
        for rr, (_, _, _, os_, ls) in enumerate(seqs):
            for t in range(nt):
                o_ref[t, rr] = os_[t * n:(t + 1) * n, :]
                lse_ref[t, rr] = ls[t * n:(t + 1) * n, :]


def _attention(qkv, group, batch, seq):
    dil = ATTN_PATTERNS[group][1]
    if dil == 1:
        nt, n = 1, seq
    else:
        nt, n = seq // PERM_TILE, PERM_TILE // dil
    n_sub = nt * n
    t = batch * seq
    qkv5 = qkv.reshape(batch, nt, dil, n, qkv.shape[1])
    cb = group * 3
    nres = min(dil, max(1, ATTN_MIN_BLOCKS * ATTN_QB // n_sub))
    blk = (None, nt, nres, n, GROUP_WIDTH)
    scratch = [pltpu.VMEM((HEADS_PER_GROUP * ATTN_OFFSETS, ATTN_QB, ATTN_KB), F32)]
    if nt > 1:
        scratch += [pltpu.VMEM((nres, n_sub, GROUP_WIDTH), BF16)] * 4 + [pltpu.VMEM((nres, n_sub, LANES), F32)]
    out, lse = pl.pallas_call(
        functools.partial(_attn_kernel, group=group, dil=dil, nt=nt, n=n, nres=nres),
        grid=(batch, dil // nres),
        in_specs=[pl.BlockSpec(blk, lambda b, r: (b, 0, r, 0, cb)),
                  pl.BlockSpec(blk, lambda b, r: (b, 0, r, 0, cb + 1)),
                  pl.BlockSpec(blk, lambda b, r: (b, 0, r, 0, cb + 2))],
        out_specs=[pl.BlockSpec(blk, lambda b, r: (b, 0, r, 0, 0)),
                   pl.BlockSpec((None, nt, nres, n, LANES), lambda b, r: (b, 0, r, 0, 0))],
        out_shape=[jax.ShapeDtypeStruct((batch, nt, dil, n, GROUP_WIDTH), BF16),
                   jax.ShapeDtypeStruct((batch, nt, dil, n, LANES), F32)],
        scratch_shapes=scratch,
        compiler_params=_cparams(("arbitrary", "arbitrary")),
        name=f"attn_g{group}",
    )(qkv5, qkv5, qkv5)
    return out.reshape(t, GROUP_WIDTH), lse.reshape(t, LANES)


def _log_sigmoid(x):
    return jnp.minimum(x, 0.0) - jnp.log(1.0 + jnp.exp(-jnp.abs(x)))


def _retention_kernel(decay_ref, qkv_ref, g_ref, gain_ref, o_ref, state, ybwd, dmat, kdec, qdec, *, nc):
    c = RET_CHUNK
    h = pl.program_id(1)
    i = pl.program_id(2)
    kscale = RET_QK ** -0.5

    def set_decays(lg, forward):
        row = lax.broadcasted_iota(I32, (c, c), 0)
        col = lax.broadcasted_iota(I32, (c, c), 1)
        pos = lax.broadcasted_iota(I32, (c, LANES), 0).astype(F32)
        if forward:
            gap, key_pow, query_pow = row - col, (c - 1.0) - pos, pos + 1.0
            keep = gap >= 0
        else:
            gap, key_pow, query_pow = col - row, pos, c - pos
            keep = gap > 0
        dmat[...] = jnp.where(keep, jnp.exp(lg * jnp.maximum(gap, 0).astype(F32)) * kscale, 0.0)
        kdec[...] = jnp.exp(lg * key_pow) * kscale
        qdec[...] = jnp.exp(lg * query_pow)

    def chunk_step(lg, r0, finish):
        q = qkv_ref[r0:r0 + c, 0:RET_QK]
        k = qkv_ref[r0:r0 + c, RET_QK:2 * RET_QK]
        v = qkv_ref[r0:r0 + c, 2 * RET_QK:]
        inner = lax.dot_general(q, k, (((1,), (1,)), ((), ())), preferred_element_type=F32) * dmat[...]
        y = _dot(inner.astype(BF16), v)
        kd = (k.astype(F32) * jnp.concatenate([kdec[...]] * (RET_QK // LANES), axis=1)).astype(BF16)
        kv = lax.dot_general(kd, v, (((0,), (0,)), ((), ())), preferred_element_type=F32)
        yield
        qd = jnp.concatenate([qdec[...]] * (RET_V // LANES), axis=1)
        y = y + _dot(q, state[...].astype(BF16)) * qd
        state[...] = state[...] * jnp.exp(lg * float(c)) + kv
        yield
        finish(r0, y)

    @pl.when((i == 0) | (i == nc))
    def _():
        state[...] = jnp.zeros_like(state)

    @pl.when(i < nc)
    def _():
        lg = _log_sigmoid(jnp.zeros((1, 1), F32) + decay_ref[RET_HEADS + h])

        @pl.when(i == 0)
        def _():
            set_decays(lg, False)

        base = pl.multiple_of((nc - 1 - i) * RET_STEP, RET_STEP)

        def finish(r0, y):
            ybwd[pl.ds(base + r0, c), :] = y

        _interleave(chunk_step(lg, r0, finish) for r0 in reversed(range(0, RET_STEP, c)))

    @pl.when(i >= nc)
    def _():
        lg = _log_sigmoid(jnp.zeros((1, 1), F32) + decay_ref[h])

        @pl.when(i == nc)
        def _():
            set_decays(lg, True)

        base = pl.multiple_of((i - nc) * RET_STEP, RET_STEP)

        def finish(r0, y):
            y = y + ybwd[pl.ds(base + r0, c), :]
            g = g_ref[r0:r0 + c, :].astype(F32)
            o_ref[r0:r0 + c, :] = (g * _sigmoid(g) * (_ln(y) * gain_ref[...])).astype(BF16)

        _interleave(chunk_step(lg, r0, finish) for r0 in range(0, RET_STEP, c))


def _retention(rest, decays, gn_gain, batch, seq):
    c = RET_STEP
    nc = seq // c
    rest3 = rest.reshape(batch, seq, rest.shape[1])
    gate_blk = RET_HEADS * RET_HEAD_COLS // RET_V

    def chunk(i):
        return jnp.where(i < nc, nc - 1 - i, i - nc)

    grid_spec = pltpu.PrefetchScalarGridSpec(
        num_scalar_prefetch=1,
        grid=(batch, RET_HEADS, 2 * nc),
        in_specs=[pl.BlockSpec((None, c, RET_HEAD_COLS), lambda b, h, i, d: (b, chunk(i), h)),
                  pl.BlockSpec((None, c, RET_V), lambda b, h, i, d: (b, jnp.maximum(i - nc, 0), gate_blk + h)),
                  pl.BlockSpec((1, RET_V), lambda b, h, i, d: (0, h))],
        out_specs=pl.BlockSpec((None, c, RET_V), lambda b, h, i, d: (b, jnp.maximum(i - nc, 0), h)),
        scratch_shapes=[pltpu.VMEM((RET_QK, RET_V), F32), pltpu.VMEM((seq, RET_V), F32),
                        pltpu.VMEM((RET_CHUNK, RET_CHUNK), F32), pltpu.VMEM((RET_CHUNK, LANES), F32),
                        pltpu.VMEM((RET_CHUNK, LANES), F32)],
    )
    out = pl.pallas_call(
        functools.partial(_retention_kernel, nc=nc),
        grid_spec=grid_spec,
        out_shape=jax.ShapeDtypeStruct((batch, seq, RET_HEADS * RET_V), BF16),
        compiler_params=_cparams(("arbitrary", "arbitrary", "arbitrary")),
        name="retention",
    )(decays, rest3, rest3, gn_gain)
    return out.reshape(batch * seq, RET_HEADS * RET_V)


MERGE_TM = PERM_TILE
MERGE_SUB = 256
ZERO_SPLIT = 4


def _merge_kernel(o0_ref, o1_ref, o2_ref, l0_ref, l1_ref, l2_ref, retg_ref, ga_ref, gr_ref, x_ref, mod_ref,
                  wa_ref, wr_ref, wo_ref, ln_ref, wrh_ref, wrl_ref, rb_ref,
                  h1_ref, u2_ref, lg_ref, zrows_ref, on_scr, ln_scr, zbuf, zsem):
    i = pl.program_id(0)

    @pl.when(i == 0)
    def _():
        zbuf[...] = jnp.zeros_like(zbuf)

    zrows = zbuf.shape[0]
    zero_copies = [pltpu.make_async_copy(zbuf, zrows_ref.at[pl.ds((i * ZERO_SPLIT + k) * zrows, zrows)], zsem)
                   for k in range(ZERO_SPLIT)]
    for cp in zero_copies:
        cp.start()

    for g in (1, 2):
        dil = ATTN_PATTERNS[g][1]
        n = MERGE_TM // dil
        o_ref, l_ref = ((o1_ref, l1_ref), (o2_ref, l2_ref))[g - 1]
        for res in range(dil):
            rows = o_ref[res * n:(res + 1) * n, :].astype(F32)
            for hh in range(HEADS_PER_GROUP):
                on_scr[g - 1, hh, pl.ds(res, n, stride=dil), :] = rows[:, hh * HEAD_DIM:(hh + 1) * HEAD_DIM]
            ln_scr[g - 1, pl.ds(res, n, stride=dil), :] = l_ref[res * n:(res + 1) * n, :]

    def sub_tile(r0):
        rs = slice(r0, r0 + MERGE_SUB)
        l0, l1, l2 = l0_ref[rs, :], ln_scr[0, rs, :], ln_scr[1, rs, :]
        lm = jnp.maximum(jnp.maximum(l0, l1), l2)
        e0, e1, e2 = jnp.exp(l0 - lm), jnp.exp(l1 - lm), jnp.exp(l2 - lm)
        inv = 1.0 / (e0 + e1 + e2)
        parts = []
        for hh in range(HEADS_PER_GROUP):
            sl = slice(hh * HEAD_DIM, (hh + 1) * HEAD_DIM)
            acc = (e0[:, hh:hh + 1] * o0_ref[rs, sl].astype(F32)
                   + e1[:, hh:hh + 1] * on_scr[0, hh, rs, :]
                   + e2[:, hh:hh + 1] * on_scr[1, hh, rs, :])
            parts.append((acc * inv[:, hh:hh + 1]).astype(BF16))
        attn = jnp.concatenate(parts, axis=1)
        yield

        branch_a = _dot(attn, wa_ref[...])
        branch_r = _dot(retg_ref[rs, :], wr_ref[...])
        yield
        merged = (_sigmoid(ga_ref[rs, :].astype(F32)) * branch_a
                  + _sigmoid(gr_ref[rs, :].astype(F32)) * branch_r)
        yield
        y = _dot(merged.astype(BF16), wo_ref[...])
        yield

        h1 = _ln(DEEPNORM_ALPHA * x_ref[rs, :] + mod_ref[2:3, :] * y) * ln_ref[0:1, :] + ln_ref[1:2, :]
        h1_ref[rs, :] = h1
        u2 = _ln(h1) * (1.0 + mod_ref[4:5, :]) + mod_ref[3:4, :]
        _store_token_tiles(u2_ref, u2, r0)
        uh, ul = _split_bf16(u2)
        yield
        lg_ref[rs, :] = (_dot(uh, wrh_ref[...]) + _dot(uh, wrl_ref[...]) + _dot(ul, wrh_ref[...])
                         + rb_ref[...])

    _interleave(sub_tile(r0) for r0 in range(0, MERGE_TM, MERGE_SUB))

    for cp in zero_copies:
        cp.wait()


def _merge(outs, lses, retg, rest, x2, mod3, wa, wr, wo, ln1, wr_hi, wr_lo, rbias, seq, n_rows):
    t = x2.shape[0]
    tm = MERGE_TM
    per_seq = seq // tm
    zrows, rem = divmod(n_rows * ROW_TILE, (t // tm) * ZERO_SPLIT)
    assert rem == 0 and zrows % 8 == 0
    row = lambda w: pl.BlockSpec((tm, w), lambda i: (i, 0))
    full = lambda a: pl.BlockSpec(a.shape, lambda i: (0,) * a.ndim)
    return pl.pallas_call(
        _merge_kernel,
        grid=(t // tm,),
        in_specs=[row(GROUP_WIDTH)] * 3 + [row(LANES)] * 3 + [
            row(RET_HEADS * RET_V),
            pl.BlockSpec((tm, D_MODEL), lambda i: (i, 6)),
            pl.BlockSpec((tm, D_MODEL), lambda i: (i, 7)),
            row(D_MODEL),
            pl.BlockSpec((None, 6, D_MODEL), lambda i: (i // per_seq, 0, 0)),
            full(wa), full(wr), full(wo), full(ln1), full(wr_hi), full(wr_lo), full(rbias)],
        out_specs=[row(D_MODEL), pl.BlockSpec((tm * ROW_TILE, LANES), lambda i: (i, 0)), row(LANES),
                   pl.BlockSpec(memory_space=pl.ANY)],
        out_shape=[jax.ShapeDtypeStruct((t, D_MODEL), F32),
                   jax.ShapeDtypeStruct((t * ROW_TILE, LANES), F32),
                   jax.ShapeDtypeStruct((t, LANES), F32),
                   jax.ShapeDtypeStruct((n_rows * ROW_TILE, LANES), F32)],
        scratch_shapes=[pltpu.VMEM((2, HEADS_PER_GROUP, tm, HEAD_DIM), F32), pltpu.VMEM((2, tm, LANES), F32),
                        pltpu.VMEM((zrows, LANES), F32), pltpu.SemaphoreType.DMA(())],
        compiler_params=_cparams(("arbitrary",)),
        name="merge",
    )(*outs, *lses, retg, rest, rest, x2, mod3, wa, wr, wo, ln1, wr_hi, wr_lo, rbias)


ROUTE_TM = 512
BIG = 1 << 20


def _route_kernel(lg_ref, cols_ref, ints_ref, cnt_ref, carry):
    i = pl.program_id(0)

    @pl.when(i == 0)
    def _():
        carry[...] = jnp.zeros_like(carry)

    tm = ROUTE_TM
    lg = lg_ref[...]
    lane = lax.broadcasted_iota(I32, (tm, LANES), 1)
    lane_f = lane.astype(F32)
    first = lambda mask: jnp.min(jnp.where(mask, lane_f, float(BIG)), axis=-1, keepdims=True).astype(I32)

    coarse = jnp.where(lane < N_GROUPS, lg, NEG)
    cmax = jnp.max(coarse, axis=-1, keepdims=True)
    gsel = first(coarse == cmax)
    p_group = 1.0 / jnp.sum(jnp.exp(coarse - cmax), axis=-1, keepdims=True)

    lo = N_GROUPS + EXPERTS_PER_GROUP * gsel
    fine = jnp.where((lane >= lo) & (lane < lo + EXPERTS_PER_GROUP), lg, NEG)
    v1 = jnp.max(fine, axis=-1, keepdims=True)
    i1 = first(fine == v1)
    fine2 = jnp.where(lane == i1, NEG, fine)
    v2 = jnp.max(fine2, axis=-1, keepdims=True)
    i2 = first(fine2 == v2)
    ex = jnp.exp(v2 - v1)
    den = 1.0 / (1.0 + ex)
    gate1 = p_group * den
    gate2 = p_group * (ex * den)
    e1 = i1 - N_GROUPS
    e2 = i2 - N_GROUPS

    oh1 = lane == e1
    oh2 = lane == e2
    cnt = jnp.where(oh1 | oh2, 1.0, 0.0)
    r_i = lax.broadcasted_iota(I32, (tm, tm), 0)
    c_i = lax.broadcasted_iota(I32, (tm, tm), 1)
    tri = jnp.where(r_i > c_i, 1.0, 0.0).astype(BF16)
    rank = _dot(tri, cnt.astype(BF16)) + carry[...]
    r1 = jnp.sum(jnp.where(oh1, rank, 0.0), axis=-1, keepdims=True)
    r2 = jnp.sum(jnp.where(oh2, rank, 0.0), axis=-1, keepdims=True)
    carry[...] = carry[...] + jnp.sum(cnt, axis=0, keepdims=True)
    cnt_ref[...] = jnp.broadcast_to(carry[...], cnt_ref.shape)

    cols_ref[...] = jnp.where(lane == 0, gate1, jnp.where(lane == 1, gate2, 0.0))
    packed = jnp.where(lane == 0, e1.astype(F32),
                       jnp.where(lane == 1, e2.astype(F32),
                                 jnp.where(lane == 2, r1, jnp.where(lane == 3, r2, 0.0))))
    ints_ref[...] = packed.T[0:8, :].astype(I32)


def _route(logits):
    t = logits.shape[0]
    tm = ROUTE_TM
    return pl.pallas_call(
        _route_kernel,
        grid=(t // tm,),
        in_specs=[pl.BlockSpec((tm, LANES), lambda i: (i, 0))],
        out_specs=[pl.BlockSpec((tm, LANES), lambda i: (i, 0)),
                   pl.BlockSpec((8, tm), lambda i: (0, i)),
                   pl.BlockSpec((8, LANES), lambda i: (0, 0))],
        out_shape=[jax.ShapeDtypeStruct((t, LANES), F32),
                   jax.ShapeDtypeStruct((8, t), I32),
                   jax.ShapeDtypeStruct((8, LANES), F32)],
        scratch_shapes=[pltpu.VMEM((1, LANES), F32)],
        compiler_params=_cparams(("arbitrary",)),
        name="route",
    )(logits)


def _plan_kernel(ints_ref, cnt_ref, dest_ref, meta_ref, *, n_blocks_pad):
    sub = lax.broadcasted_iota(I32, (LANES, LANES), 0)
    lane = lax.broadcasted_iota(I32, (LANES, LANES), 1)
    cnt = cnt_ref[0:1, :]
    nblk_row = jnp.floor((cnt + (MOE_BLK - 1.0)) * (1.0 / MOE_BLK))
    nblk_mat = jnp.broadcast_to(nblk_row, (LANES, LANES))
    start_col = jnp.sum(jnp.where(lane < sub, nblk_mat, 0.0), axis=-1, keepdims=True)
    nblk_col = jnp.sum(jnp.where(lane == sub, nblk_mat, 0.0), axis=-1, keepdims=True)
    end_col = start_col + nblk_col

    ints = ints_ref[...]
    base = jnp.zeros(ints.shape, F32)
    for e in range(N_EXPERTS):
        base = jnp.where(ints == e, start_col[e:e + 1, :] * float(MOE_BLK), base)
    dest = base[0:2, :].astype(I32) + ints[2:4, :]
    dest_ref[...] = jnp.concatenate([dest, jnp.zeros((6, ints.shape[1]), I32)], axis=0)

    blk = lax.broadcasted_iota(I32, (LANES, n_blocks_pad), 1).astype(F32)
    e_sub = lax.broadcasted_iota(I32, (LANES, n_blocks_pad), 0)
    done = jnp.where((e_sub < N_EXPERTS) & (end_col <= blk), 1.0, 0.0)
    bexp = jnp.minimum(jnp.sum(done, axis=0, keepdims=True), N_EXPERTS - 1.0)
    used = jnp.sum(nblk_row, axis=-1, keepdims=True)
    row = lax.broadcasted_iota(I32, (8, n_blocks_pad), 0)
    meta = jnp.where(row == 0, bexp, jnp.where(row == 1, used, 0.0))
    meta_ref[...] = meta.astype(I32)


def _plan(ints, counts, n_blocks_pad):
    t = ints.shape[1]
    return pl.pallas_call(
        functools.partial(_plan_kernel, n_blocks_pad=n_blocks_pad),
        out_shape=[jax.ShapeDtypeStruct((8, t), I32), jax.ShapeDtypeStruct((8, n_blocks_pad), I32)],
        compiler_params=pltpu.CompilerParams(vmem_limit_bytes=VMEM_LIMIT),
        name="plan",
    )(ints, counts)


DISPATCH_TM = 2048
DMA_UNROLL = 16


def _row_tile(ref, r):
    return ref.at[pl.ds(pl.multiple_of(r * ROW_TILE, ROW_TILE), ROW_TILE)]


def _dispatch_kernel(d0_ref, d1_ref, u2_ref, rows_in_ref, rows_ref, sem):
    del rows_in_ref

    def issue(t, carry):
        src = _row_tile(u2_ref, t)
        pltpu.make_async_copy(src, _row_tile(rows_ref, d0_ref[t]), sem).start(priority=0)
        pltpu.make_async_copy(src, _row_tile(rows_ref, d1_ref[t]), sem).start(priority=1)
        return carry

    lax.fori_loop(0, DISPATCH_TM, issue, 0, unroll=DMA_UNROLL)
    for _ in range(2):
        pltpu.make_async_copy(u2_ref, rows_ref.at[pl.ds(0, DISPATCH_TM * ROW_TILE)], sem).wait()


def _dispatch(dest0, dest1, u2, rows0):
    t = u2.shape[0] // ROW_TILE
    idx = pl.BlockSpec((DISPATCH_TM,), lambda i: (i,), memory_space=pltpu.SMEM)
    return pl.pallas_call(
        _dispatch_kernel,
        grid=(t // DISPATCH_TM,),
        in_specs=[idx, idx,
                  pl.BlockSpec((DISPATCH_TM * ROW_TILE, LANES), lambda i: (i, 0)),
                  pl.BlockSpec(memory_space=pl.ANY)],
        out_specs=pl.BlockSpec(memory_space=pl.ANY),
        out_shape=jax.ShapeDtypeStruct(rows0.shape, F32),
        scratch_shapes=[pltpu.SemaphoreType.DMA(())],
        input_output_aliases={3: 0},
        compiler_params=_cparams(("arbitrary",)),
        name="dispatch",
    )(dest0, dest1, u2, rows0)


def _expert_runs(bexp, used):
    n = bexp.shape[0]
    idx = jnp.arange(n, dtype=I32)
    first = (idx < used[0]) & ((idx == 0) | (bexp != jnp.roll(bexp, 1)))
    slot = (jnp.cumsum(first.astype(I32)) - 1) % 2
    first_at_or_after = lax.cummin(jnp.where(first, idx, n)[::-1])[::-1]
    first_after = jnp.concatenate([first_at_or_after[1:], jnp.full((1,), n, I32)])
    nxt = jnp.where(first_after < n, bexp[jnp.minimum(first_after, n - 1)], -1)
    return first.astype(I32), slot.astype(I32), nxt.astype(I32)


def _experts_kernel(bexp_ref, used_ref, first_ref, slot_ref, nxt_ref, x_ref, w1_hbm, w3_hbm, w2_hbm, y_ref,
                    wb1, wb3, wb2, w1s, w3s, w2s, sem):
    i = pl.program_id(0)
    active = i < used_ref[0]
    slot = slot_ref[i]

    def fetch(e, s):
        return [pltpu.make_async_copy(w.at[e], buf.at[s], sem.at[s])
                for w, buf in ((w1_hbm, wb1), (w3_hbm, wb3), (w2_hbm, wb2))]

    @pl.when(i == 0)
    def _():
        for cp in fetch(bexp_ref[0], 0):
            cp.start()

    @pl.when(first_ref[i] == 1)
    def _():
        @pl.when(nxt_ref[i] >= 0)
        def _():
            for cp in fetch(nxt_ref[i], 1 - slot):
                cp.start()

        for cp in fetch(bexp_ref[i], slot):
            cp.wait()
        w1s[...] = wb1[slot].astype(BF16)
        w3s[...] = wb3[slot].astype(BF16)
        w2s[...] = wb2[slot].astype(BF16)

    @pl.when(active)
    def _():
        def sub_block(r0):
            xb = _load_token_tiles(x_ref, MOE_SUB, r0).astype(BF16)
            yield
            a = _dot(xb, w1s[...])
            b = _dot(xb, w3s[...])
            yield
            hdn = (a * _sigmoid(a) * b).astype(BF16)
            yield
            _store_token_tiles(y_ref, _dot(hdn, w2s[...]), r0)

        _interleave(sub_block(r0) for r0 in range(0, MOE_BLK, MOE_SUB))

    @pl.when(jnp.logical_not(active))
    def _():
        y_ref[...] = jnp.zeros_like(y_ref)


def _experts(bexp, used, rows, w1, w3, w2):
    n_blocks = rows.shape[0] // (MOE_BLK * ROW_TILE)
    first, slot, nxt = _expert_runs(bexp, used)
    any_space = pl.BlockSpec(memory_space=pl.ANY)
    grid_spec = pltpu.PrefetchScalarGridSpec(
        num_scalar_prefetch=5,
        grid=(n_blocks,),
        in_specs=[pl.BlockSpec((MOE_BLK * ROW_TILE, LANES), lambda i, be, nu, *_: (jnp.minimum(i, nu[0] - 1), 0)),
                  any_space, any_space, any_space],
        out_specs=pl.BlockSpec((MOE_BLK * ROW_TILE, LANES), lambda i, *_: (i, 0)),
        scratch_shapes=[pltpu.VMEM((2, D_MODEL, EXPERT_FF), F32), pltpu.VMEM((2, D_MODEL, EXPERT_FF), F32),
                        pltpu.VMEM((2, EXPERT_FF, D_MODEL), F32),
                        pltpu.VMEM((D_MODEL, EXPERT_FF), BF16), pltpu.VMEM((D_MODEL, EXPERT_FF), BF16),
                        pltpu.VMEM((EXPERT_FF, D_MODEL), BF16), pltpu.SemaphoreType.DMA((2,))],
    )
    return pl.pallas_call(
        _experts_kernel,
        grid_spec=grid_spec,
        out_shape=jax.ShapeDtypeStruct(rows.shape, F32),
        compiler_params=_cparams(("arbitrary",)),
        name="experts",
    )(bexp, used, first, slot, nxt, rows, w1, w3, w2)


COMBINE_TM = 256


def _combine_kernel(d0_ref, d1_ref, n0_ref, n1_ref, y_ref, cols_ref, h1_ref, mod_ref, ln_ref, o_ref, ya, yb, sem):
    i = pl.program_id(0)
    slot = i % 2

    def gather(i0_ref, i1_ref, s):
        def issue(t, carry):
            pltpu.make_async_copy(_row_tile(y_ref, i0_ref[t]), _row_tile(ya.at[s], t), sem.at[s]).start(priority=0)
            pltpu.make_async_copy(_row_tile(y_ref, i1_ref[t]), _row_tile(yb.at[s], t), sem.at[s]).start(priority=1)
            return carry

        lax.fori_loop(0, COMBINE_TM, issue, 0, unroll=DMA_UNROLL)

    @pl.when(i == 0)
    def _():
        gather(d0_ref, d1_ref, 0)

    @pl.when(i + 1 < pl.num_programs(0))
    def _():
        gather(n0_ref, n1_ref, 1 - slot)

    for buf in (ya, yb):
        pltpu.make_async_copy(y_ref.at[pl.ds(0, COMBINE_TM * ROW_TILE)], buf.at[slot], sem.at[slot]).wait()

    cols = cols_ref[...]
    moe = (cols[:, 0:1] * _load_token_tiles(ya.at[slot], COMBINE_TM)
           + cols[:, 1:2] * _load_token_tiles(yb.at[slot], COMBINE_TM))
    pre = DEEPNORM_ALPHA * h1_ref[...] + mod_ref[5:6, :] * moe
    o_ref[...] = _ln(pre) * ln_ref[0:1, :] + ln_ref[1:2, :]


def _combine(dest0, dest1, y_rows, cols, h1, mod3, ln2, seq):
    t = h1.shape[0]
    tm = COMBINE_TM
    per_seq = seq // tm
    idx = pl.BlockSpec((tm,), lambda i: (i,), memory_space=pltpu.SMEM)
    idx_next = pl.BlockSpec((tm,), lambda i: (jnp.minimum(i + 1, t // tm - 1),), memory_space=pltpu.SMEM)
    return pl.pallas_call(
        _combine_kernel,
        grid=(t // tm,),
        in_specs=[idx, idx, idx_next, idx_next,
                  pl.BlockSpec(memory_space=pl.ANY),
                  pl.BlockSpec((tm, LANES), lambda i: (i, 0)),
                  pl.BlockSpec((tm, D_MODEL), lambda i: (i, 0)),
                  pl.BlockSpec((None, 6, D_MODEL), lambda i: (i // per_seq, 0, 0)),
                  pl.BlockSpec((2, D_MODEL), lambda i: (0, 0))],
        out_specs=pl.BlockSpec((tm, D_MODEL), lambda i: (i, 0)),
        out_shape=jax.ShapeDtypeStruct((t, D_MODEL), F32),
        scratch_shapes=[pltpu.VMEM((2, tm * ROW_TILE, LANES), F32), pltpu.VMEM((2, tm * ROW_TILE, LANES), F32),
                        pltpu.SemaphoreType.DMA((2,))],
        compiler_params=_cparams(("arbitrary",)),
        name="combine",
    )(dest0, dest1, dest0, dest1, y_rows, cols, h1, mod3, ln2)


def _layer(h, c8, w_ada, b_ada, w_in, w_attn_out, decay_f, decay_b, gn_gain, w_ret_out, w_out,
           ln1_gain, ln1_bias, w_coarse, b_coarse, w_fine, b_fine, w1, w3, w2, ln2_gain, ln2_bias):
    batch, seq, d = h.shape
    t = batch * seq
    x2 = h.reshape(t, d)

    mod = _ada(c8, w_ada, b_ada.reshape(1, -1))
    mod3 = mod[:batch].reshape(batch, 6, d)

    wb = w_in.astype(BF16)
    qkv, u1 = _inproj_attn(x2, mod3, wb, seq)
    rest = _inproj_rest(u1, wb)

    outs, lses = zip(*[_attention(qkv, g, batch, seq) for g in range(ATTN_GROUPS)])
    decays = jnp.concatenate([decay_f, decay_b]).astype(F32)
    retg = _retention(rest, decays, gn_gain.reshape(1, -1), batch, seq)

    w_route = jnp.concatenate([w_coarse, w_fine.transpose(1, 0, 2).reshape(d, N_EXPERTS)], axis=1)
    n_route = w_route.shape[1]
    w_route = jnp.pad(w_route, ((0, 0), (0, LANES - n_route)))
    wr_hi, wr_lo = _split_bf16(w_route)
    rbias = jnp.pad(jnp.concatenate([b_coarse, b_fine.reshape(-1)]), (0, LANES - n_route)).reshape(1, LANES)
    ln1 = jnp.stack([ln1_gain, ln1_bias])
    n_blocks = 2 * t // MOE_BLK + N_EXPERTS
    h1, u2, logits, rows0 = _merge(outs, lses, retg, rest, x2, mod3, w_attn_out.astype(BF16),
                                   w_ret_out.astype(BF16), w_out.astype(BF16), ln1, wr_hi, wr_lo, rbias, seq,
                                   n_blocks * MOE_BLK)

    cols, ints, counts = _route(logits)
    n_blocks_pad = -(-n_blocks // LANES) * LANES
    dest, meta = _plan(ints, counts, n_blocks_pad)
    dest0, dest1 = dest[0], dest[1]
    rows = _dispatch(dest0, dest1, u2, rows0)
    y_rows = _experts(meta[0, :n_blocks], meta[1, :1], rows, w1, w3, w2)
    out = _combine(dest0, dest1, y_rows, cols, h1, mod3, jnp.stack([ln2_gain, ln2_bias]), seq)
    return out.reshape(batch, seq, d)


def kernel(x, c, w_ada, b_ada, w_in, w_attn_out, ret_decay_fwd, ret_decay_bwd, ret_gn_gain, w_ret_out, w_out,
           ln1_gain, ln1_bias, w_coarse, b_coarse, w_fine, b_fine, w1, w3, w2, ln2_gain, ln2_bias):
    batch = x.shape[0]
    assert batch <= 8 and x.shape[1] % (2 * INPROJ_TM) == 0 and x.shape[2] == D_MODEL
    c8 = jnp.pad(c, ((0, 8 - batch), (0, 0)))
    h = x
    for l in range(w_ada.shape[0]):
        h = _layer(h, c8, w_ada[l], b_ada[l], w_in[l], w_attn_out[l], ret_decay_fwd[l], ret_decay_bwd[l],
                   ret_gn_gain[l], w_ret_out[l], w_out[l], ln1_gain[l], ln1_bias[l], w_coarse[l], b_coarse[l],
                   w_fine[l], b_fine[l], w1[l], w3[l], w2[l], ln2_gain[l], ln2_bias[l])
    return h
```

```python
import functools
import math

import jax
import jax.numpy as jnp
from jax import lax
from jax.experimental import pallas as pl
from jax.experimental.pallas import tpu as pltpu

F32 = jnp.float32
BF16 = jnp.bfloat16
I32 = jnp.int32

D_MODEL = 1024
ATTN_GROUPS = 3
HEADS_PER_GROUP = 4
HEAD_DIM = 128
ATTN_HEADS = ATTN_GROUPS * HEADS_PER_GROUP
GROUP_WIDTH = HEADS_PER_GROUP * HEAD_DIM
ATTN_PATTERNS = ((128, 1), (512, 4), (2048, 16))
ALIBI_MAX_EXP = 8.0
NEG = -1e30
RET_HEADS = 4
RET_QK = 256
RET_V = 512
N_GROUPS = 4
EXPERTS_PER_GROUP = 8
N_EXPERTS = N_GROUPS * EXPERTS_PER_GROUP
EXPERT_FF = 512
DEPTH = 1
DEEPNORM_ALPHA = (2.0 * DEPTH) ** 0.25
LN_EPS = 1e-5

LANES = 128
PERM_TILE = 512
ATTN_QB = 128
ATTN_HALF = 64
ATTN_KB = ATTN_QB + 2 * ATTN_HALF
RET_CHUNK = 256
RET_STEP = 2048
MOE_BLK = 512
MOE_SUB = 256
VMEM_LIMIT = 56 * 1024 * 1024


def _cparams(sem):
    return pltpu.CompilerParams(dimension_semantics=sem, vmem_limit_bytes=VMEM_LIMIT)


def _split_bf16(a):
    hi = a.astype(BF16)
    lo = (a - hi.astype(F32)).astype(BF16)
    return hi, lo


def _dot(a, b):
    return jnp.dot(a, b, preferred_element_type=F32)


def _dot3(a, b):
    ah, al = _split_bf16(a)
    bh, bl = _split_bf16(b)
    return _dot(ah, bh) + _dot(ah, bl) + _dot(al, bh)


def _ln(x):
    mu = jnp.mean(x, axis=-1, keepdims=True)
    xc = x - mu
    var = jnp.mean(xc * xc, axis=-1, keepdims=True)
    return xc * lax.rsqrt(var + LN_EPS)


def _sigmoid(x):
    return 1.0 / (1.0 + jnp.exp(-x))


ROW_TILE = D_MODEL // LANES


def _store_token_tiles(ref, val, row0=0):
    n = val.shape[0]
    for cc in range(ROW_TILE):
        ref[pl.ds(row0 * ROW_TILE + cc, n, stride=ROW_TILE), :] = val[:, cc * LANES:(cc + 1) * LANES]


def _load_token_tiles(ref, n, row0=0):
    return jnp.concatenate([ref[pl.ds(row0 * ROW_TILE + cc, n, stride=ROW_TILE), :] for cc in range(ROW_TILE)],
                           axis=1)


def _interleave(phased):
    pending = []
    for gen in phased:
        pending.append(gen)
        pending = [g for g in pending if next(g, StopIteration) is not StopIteration]
    while pending:
        pending = [g for g in pending if next(g, StopIteration) is not StopIteration]


def _ada_kernel(c_ref, w_ref, b_ref, o_ref):
    o_ref[...] = _dot3(c_ref[...], w_ref[...]) + b_ref[...]


def _ada(c8, w_ada, b_ada):
    n = w_ada.shape[1]
    return pl.pallas_call(
        _ada_kernel,
        grid=(n // D_MODEL,),
        in_specs=[pl.BlockSpec((8, D_MODEL), lambda j: (0, 0)),
                  pl.BlockSpec((D_MODEL, D_MODEL), lambda j: (0, j)),
                  pl.BlockSpec((1, D_MODEL), lambda j: (0, j))],
        out_specs=pl.BlockSpec((8, D_MODEL), lambda j: (0, j)),
        out_shape=jax.ShapeDtypeStruct((8, n), F32),
        compiler_params=_cparams(("arbitrary",)),
        name="ada",
    )(c8, w_ada, b_ada)


INPROJ_TM = 1024
LN_ROWS = 256
LN_UNROLL = 4


def _modulated_ln_rows(x_ref, mod_ref, emit):
    gain = 1.0 + mod_ref[1:2, :]
    shift = mod_ref[0:1, :]

    def chunk(c, carry):
        r0 = pl.multiple_of(c * LN_ROWS, LN_ROWS)
        emit(r0, _ln(x_ref[pl.ds(r0, LN_ROWS), :]) * gain + shift)
        return carry

    lax.fori_loop(0, INPROJ_TM // LN_ROWS, chunk, 0, unroll=LN_UNROLL)


def _inproj_attn_kernel(x_ref, mod_ref, wq_ref, wk_ref, wv_ref, o_ref, un_ref, uf_scr, u_scr):
    j = pl.program_id(1)

    @pl.when(j == 0)
    def _():
        def emit(r0, u):
            for cc in range(D_MODEL // LANES):
                uf_scr[cc, pl.ds(r0, LN_ROWS), :] = u[:, cc * LANES:(cc + 1) * LANES]
            ub = u.astype(BF16)
            u_scr[0, pl.ds(r0, LN_ROWS), :] = ub
            un_ref[pl.ds(r0, LN_ROWS), :] = ub

        _modulated_ln_rows(x_ref, mod_ref, emit)
        for g in (1, 2):
            dil = ATTN_PATTERNS[g][1]
            n = PERM_TILE // dil
            for t0 in range(0, INPROJ_TM, PERM_TILE):
                for res in range(dil):
                    for cc in range(D_MODEL // LANES):
                        rows = uf_scr[cc, pl.ds(t0 + res, n, stride=dil), :]
                        u_scr[g, t0 + res * n:t0 + (res + 1) * n, cc * LANES:(cc + 1) * LANES] = rows.astype(BF16)

    u = u_scr[j]
    for s, w_ref in enumerate((wq_ref, wk_ref, wv_ref)):
        o_ref[:, s * GROUP_WIDTH:(s + 1) * GROUP_WIDTH] = _dot(u, w_ref[...]).astype(BF16)


ATTN_WIDTH = ATTN_HEADS * HEAD_DIM
W_RQ0 = 3 * ATTN_WIDTH
W_RK0 = W_RQ0 + RET_HEADS * RET_QK
W_RV0 = W_RK0 + RET_HEADS * RET_QK
W_TAIL0 = W_RV0 + RET_HEADS * RET_V


def _inproj_attn(x2, mod3, wb, seq):
    t = x2.shape[0]
    tn = 3 * GROUP_WIDTH
    per_seq = seq // INPROJ_TM
    groups_per_range = ATTN_WIDTH // GROUP_WIDTH
    wspec = lambda s: pl.BlockSpec((D_MODEL, GROUP_WIDTH), lambda i, j: (0, s * groups_per_range + j))
    return pl.pallas_call(
        _inproj_attn_kernel,
        grid=(t // INPROJ_TM, ATTN_GROUPS),
        in_specs=[pl.BlockSpec((INPROJ_TM, D_MODEL), lambda i, j: (i, 0)),
                  pl.BlockSpec((None, 6, D_MODEL), lambda i, j: (i // per_seq, 0, 0)),
                  wspec(0), wspec(1), wspec(2)],
        out_specs=[pl.BlockSpec((INPROJ_TM, tn), lambda i, j: (i, j)),
                   pl.BlockSpec((INPROJ_TM, D_MODEL), lambda i, j: (i, 0))],
        out_shape=[jax.ShapeDtypeStruct((t, ATTN_GROUPS * tn), BF16),
                   jax.ShapeDtypeStruct((t, D_MODEL), BF16)],
        scratch_shapes=[pltpu.VMEM((D_MODEL // LANES, INPROJ_TM, LANES), F32),
                        pltpu.VMEM((ATTN_GROUPS, INPROJ_TM, D_MODEL), BF16)],
        compiler_params=_cparams(("arbitrary", "arbitrary")),
        name="inproj_attn",
    )(x2, mod3, wb, wb, wb)


REST_TN = RET_HEAD_COLS = 2 * RET_QK + RET_V
REST_TAIL_TILES = (RET_HEADS * RET_V + 2 * D_MODEL) // REST_TN
REST_TM = 2048


def _inproj_rest_kernel(u_ref, wq_ref, wk_ref, wv_ref, wt0_ref, wt1_ref, o_ref):
    j = pl.program_id(0)

    def project(parts):
        u = u_ref[...]
        c0 = 0
        for w_ref in parts:
            wd = w_ref.shape[1]
            o_ref[:, c0:c0 + wd] = _dot(u, w_ref[...]).astype(BF16)
            c0 += wd

    @pl.when(j < RET_HEADS)
    def _():
        project((wq_ref, wk_ref, wv_ref))

    @pl.when(j >= RET_HEADS)
    def _():
        project((wt0_ref, wt1_ref))


def _inproj_rest(u, wb):
    t = u.shape[0]
    half = REST_TN // 2
    assert W_TAIL0 % half == 0 and W_RV0 % RET_V == 0
    head = lambda j: jnp.minimum(j, RET_HEADS - 1)
    tail = lambda j: jnp.maximum(j - RET_HEADS, 0)
    return pl.pallas_call(
        _inproj_rest_kernel,
        grid=(RET_HEADS + REST_TAIL_TILES, t // REST_TM),
        in_specs=[pl.BlockSpec((REST_TM, D_MODEL), lambda j, i: (i, 0)),
                  pl.BlockSpec((D_MODEL, RET_QK), lambda j, i: (0, W_RQ0 // RET_QK + head(j))),
                  pl.BlockSpec((D_MODEL, RET_QK), lambda j, i: (0, W_RK0 // RET_QK + head(j))),
                  pl.BlockSpec((D_MODEL, RET_V), lambda j, i: (0, W_RV0 // RET_V + head(j))),
                  pl.BlockSpec((D_MODEL, half), lambda j, i: (0, W_TAIL0 // half + 2 * tail(j))),
                  pl.BlockSpec((D_MODEL, half), lambda j, i: (0, W_TAIL0 // half + 2 * tail(j) + 1))],
        out_specs=pl.BlockSpec((REST_TM, REST_TN), lambda j, i: (i, j)),
        out_shape=jax.ShapeDtypeStruct((t, (RET_HEADS + REST_TAIL_TILES) * REST_TN), BF16),
        compiler_params=_cparams(("arbitrary", "arbitrary")),
        name="inproj_rest",
    )(u, wb, wb, wb, wb, wb)


ATTN_OFFSETS = ATTN_KB // ATTN_HALF - 1
ATTN_UNROLL = 8
ATTN_MIN_BLOCKS = 4


def _attn_kernel(q_ref, k_ref, v_ref, o_ref, lse_ref, bias_scr, *scratch, group, dil, nt, n, nres):
    n_sub = nt * n
    if nt == 1:
        seqs = [(q_ref.at[0, rr], k_ref.at[0, rr], v_ref.at[0, rr], o_ref.at[0, rr], lse_ref.at[0, rr])
                for rr in range(nres)]
    else:
        seqs = [tuple(s.at[rr] for s in scratch) for rr in range(nres)]
        for rr, (qs, ks, vs, _, _) in enumerate(seqs):
            for t in range(nt):
                qs[t * n:(t + 1) * n, :] = q_ref[t, rr]
                ks[t * n:(t + 1) * n, :] = k_ref[t, rr]
                vs[t * n:(t + 1) * n, :] = v_ref[t, rr]

    @pl.when((pl.program_id(0) == 0) & (pl.program_id(1) == 0))
    def _():
        base = (lax.broadcasted_iota(I32, (ATTN_QB, ATTN_KB), 1)
                - lax.broadcasted_iota(I32, (ATTN_QB, ATTN_KB), 0))
        for j in range(ATTN_OFFSETS):
            dist = jnp.abs(base - j * ATTN_HALF)
            for hh in range(HEADS_PER_GROUP):
                slope = dil * 2.0 ** (-ALIBI_MAX_EXP * (group * HEADS_PER_GROUP + hh + 1) / ATTN_HEADS)
                bias_scr[hh * ATTN_OFFSETS + j] = jnp.where(dist <= ATTN_HALF, -slope * dist.astype(F32), NEG)

    scale = HEAD_DIM ** -0.5
    lane = lax.broadcasted_iota(I32, (ATTN_QB, LANES), 1)

    def block(blk, carry):
        q0 = pl.multiple_of(blk * ATTN_QB, ATTN_QB)
        start = pl.multiple_of(jnp.clip(q0 - ATTN_HALF, 0, n_sub - ATTN_KB), ATTN_HALF)
        j = (q0 - start) // ATTN_HALF
        for qs, ks, vs, os_, ls in seqs:
            lse_tile = jnp.zeros((ATTN_QB, LANES), F32)
            for hh in range(HEADS_PER_GROUP):
                cs = slice(hh * HEAD_DIM, (hh + 1) * HEAD_DIM)
                qb = qs[pl.ds(q0, ATTN_QB), cs]
                kb = ks[pl.ds(start, ATTN_KB), cs]
                vb = vs[pl.ds(start, ATTN_KB), cs]
                s = lax.dot_general(qb, kb, (((1,), (1,)), ((), ())), preferred_element_type=F32) * scale
                s = s + bias_scr[hh * ATTN_OFFSETS + j]
                m = jnp.max(s, axis=-1, keepdims=True)
                p = jnp.exp(s - m)
                l = jnp.sum(p, axis=-1, keepdims=True)
                o = _dot(p.astype(BF16), vb) * (1.0 / l)
                os_[pl.ds(q0, ATTN_QB), cs] = o.astype(BF16)
                lse_tile = jnp.where(lane == hh, m + jnp.log(l), lse_tile)
            ls[pl.ds(q0, ATTN_QB), :] = lse_tile
        return carry

    n_blk = n_sub // ATTN_QB
    lax.fori_loop(0, n_blk, block, 0, unroll=min(max(ATTN_UNROLL // nres, 1), n_blk))

    if nt > 1:
        for rr, (_, _, _, os_, ls) in enumerate(seqs):
            for t in range(nt):
                o_ref[t, rr] = os_[t * n:(t + 1) * n, :]
                lse_ref[t, rr] = ls[t * n:(t + 1) * n, :]


def _attention(qkv, group, batch, seq):
    dil = ATTN_PATTERNS[group][1]
    if dil == 1:
        nt, n = 1, seq
    else:
        nt, n = seq // PERM_TILE, PERM_TILE // dil
    n_sub = nt * n
    t = batch * seq
    qkv5 = qkv.reshape(batch, nt, dil, n, qkv.shape[1])
    cb = group * 3
    nres = min(dil, max(1, ATTN_MIN_BLOCKS * ATTN_QB // n_sub))
    blk = (None, nt, nres, n, GROUP_WIDTH)
    scratch = [pltpu.VMEM((HEADS_PER_GROUP * ATTN_OFFSETS, ATTN_QB, ATTN_KB), F32)]
    if nt > 1:
        scratch += [pltpu.VMEM((nres, n_sub, GROUP_WIDTH), BF16)] * 4 + [pltpu.VMEM((nres, n_sub, LANES), F32)]
    out, lse = pl.pallas_call(
        functools.partial(_attn_kernel, group=group, dil=dil, nt=nt, n=n, nres=nres),
        grid=(batch, dil // nres),
        in_specs=[pl.BlockSpec(blk, lambda b, r: (b, 0, r, 0, cb)),
                  pl.BlockSpec(blk, lambda b, r: (b, 0, r, 0, cb + 1)),
                  pl.BlockSpec(blk, lambda b, r: (b, 0, r, 0, cb + 2))],
        out_specs=[pl.BlockSpec(blk, lambda b, r: (b, 0, r, 0, 0)),
                   pl.BlockSpec((None, nt, nres, n, LANES), lambda b, r: (b, 0, r, 0, 0))],
        out_shape=[jax.ShapeDtypeStruct((batch, nt, dil, n, GROUP_WIDTH), BF16),
                   jax.ShapeDtypeStruct((batch, nt, dil, n, LANES), F32)],
        scratch_shapes=scratch,
        compiler_params=_cparams(("arbitrary", "arbitrary")),
        name=f"attn_g{group}",
    )(qkv5, qkv5, qkv5)
    return out.reshape(t, GROUP_WIDTH), lse.reshape(t, LANES)


def _log_sigmoid(x):
    return jnp.minimum(x, 0.0) - jnp.log(1.0 + jnp.exp(-jnp.abs(x)))


def _retention_kernel(decay_ref, qkv_ref, g_ref, gain_ref, o_ref, state, ybwd, dmat, kdec, qdec, *, nc):
    c = RET_CHUNK
    h = pl.program_id(1)
    i = pl.program_id(2)
    kscale = RET_QK ** -0.5

    def set_decays(lg, forward):
        row = lax.broadcasted_iota(I32, (c, c), 0)
        col = lax.broadcasted_iota(I32, (c, c), 1)
        pos = lax.broadcasted_iota(I32, (c, LANES), 0).astype(F32)
        if forward:
            gap, key_pow, query_pow = row - col, (c - 1.0) - pos, pos + 1.0
            keep = gap >= 0
        else:
            gap, key_pow, query_pow = col - row, pos, c - pos
            keep = gap > 0
        dmat[...] = jnp.where(keep, jnp.exp(lg * jnp.maximum(gap, 0).astype(F32)) * kscale, 0.0)
        kdec[...] = jnp.exp(lg * key_pow) * kscale
        qdec[...] = jnp.exp(lg * query_pow)

    def chunk_step(lg, r0, finish):
        q = qkv_ref[r0:r0 + c, 0:RET_QK]
        k = qkv_ref[r0:r0 + c, RET_QK:2 * RET_QK]
        v = qkv_ref[r0:r0 + c, 2 * RET_QK:]
        inner = lax.dot_general(q, k, (((1,), (1,)), ((), ())), preferred_element_type=F32) * dmat[...]
        y = _dot(inner.astype(BF16), v)
        kd = (k.astype(F32) * jnp.concatenate([kdec[...]] * (RET_QK // LANES), axis=1)).astype(BF16)
        kv = lax.dot_general(kd, v, (((0,), (0,)), ((), ())), preferred_element_type=F32)
        yield
        qd = jnp.concatenate([qdec[...]] * (RET_V // LANES), axis=1)
        y = y + _dot(q, state[...].astype(BF16)) * qd
        state[...] = state[...] * jnp.exp(lg * float(c)) + kv
        yield
        finish(r0, y)

    @pl.when((i == 0) | (i == nc))
    def _():
        state[...] = jnp.zeros_like(state)

    @pl.when(i < nc)
    def _():
        lg = _log_sigmoid(jnp.zeros((1, 1), F32) + decay_ref[RET_HEADS + h])

        @pl.when(i == 0)
        def _():
            set_decays(lg, False)

        base = pl.multiple_of((nc - 1 - i) * RET_STEP, RET_STEP)

        def finish(r0, y):
            ybwd[pl.ds(base + r0, c), :] = y

        _interleave(chunk_step(lg, r0, finish) for r0 in reversed(range(0, RET_STEP, c)))

    @pl.when(i >= nc)
    def _():
        lg = _log_sigmoid(jnp.zeros((1, 1), F32) + decay_ref[h])

        @pl.when(i == nc)
        def _():
            set_decays(lg, True)

        base = pl.multiple_of((i - nc) * RET_STEP, RET_STEP)

        def finish(r0, y):
            y = y + ybwd[pl.ds(base + r0, c), :]
            g = g_ref[r0:r0 + c, :].astype(F32)
            o_ref[r0:r0 + c, :] = (g * _sigmoid(g) * (_ln(y) * gain_ref[...])).astype(BF16)

        _interleave(chunk_step(lg, r0, finish) for r0 in range(0, RET_STEP, c))


def _retention(rest, decays, gn_gain, batch, seq):
    c = RET_STEP
    nc = seq // c
    rest3 = rest.reshape(batch, seq, rest.shape[1])
    gate_blk = RET_HEADS * RET_HEAD_COLS // RET_V

    def chunk(i):
        return jnp.where(i < nc, nc - 1 - i, i - nc)

    grid_spec = pltpu.PrefetchScalarGridSpec(
        num_scalar_prefetch=1,
        grid=(batch, RET_HEADS, 2 * nc),
        in_specs=[pl.BlockSpec((None, c, RET_HEAD_COLS), lambda b, h, i, d: (b, chunk(i), h)),
                  pl.BlockSpec((None, c, RET_V), lambda b, h, i, d: (b, jnp.maximum(i - nc, 0), gate_blk + h)),
                  pl.BlockSpec((1, RET_V), lambda b, h, i, d: (0, h))],
        out_specs=pl.BlockSpec((None, c, RET_V), lambda b, h, i, d: (b, jnp.maximum(i - nc, 0), h)),
        scratch_shapes=[pltpu.VMEM((RET_QK, RET_V), F32), pltpu.VMEM((seq, RET_V), F32),
                        pltpu.VMEM((RET_CHUNK, RET_CHUNK), F32), pltpu.VMEM((RET_CHUNK, LANES), F32),
                        pltpu.VMEM((RET_CHUNK, LANES), F32)],
    )
    out = pl.pallas_call(
        functools.partial(_retention_kernel, nc=nc),
        grid_spec=grid_spec,
        out_shape=jax.ShapeDtypeStruct((batch, seq, RET_HEADS * RET_V), BF16),
        compiler_params=_cparams(("arbitrary", "arbitrary", "arbitrary")),
        name="retention",
    )(decays, rest3, rest3, gn_gain)
    return out.reshape(batch * seq, RET_HEADS * RET_V)


MERGE_TM = PERM_TILE
MERGE_SUB = 256
ZERO_SPLIT = 4


def _merge_kernel(o0_ref, o1_ref, o2_ref, l0_ref, l1_ref, l2_ref, retg_ref, ga_ref, gr_ref, x_ref, mod_ref,
                  wa_ref, wr_ref, wo_ref, ln_ref, wrh_ref, wrl_ref, rb_ref,
                  h1_ref, u2_ref, lg_ref, zrows_ref, on_scr, ln_scr, zbuf, zsem):
    i = pl.program_id(0)

    @pl.when(i == 0)
    def _():
        zbuf[...] = jnp.zeros_like(zbuf)

    zrows = zbuf.shape[0]
    zero_copies = [pltpu.make_async_copy(zbuf, zrows_ref.at[pl.ds((i * ZERO_SPLIT + k) * zrows, zrows)], zsem)
                   for k in range(ZERO_SPLIT)]
    for cp in zero_copies:
        cp.start()

    for g in (1, 2):
        dil = ATTN_PATTERNS[g][1]
        n = MERGE_TM // dil
        o_ref, l_ref = ((o1_ref, l1_ref), (o2_ref, l2_ref))[g - 1]
        for res in range(dil):
            rows = o_ref[res * n:(res + 1) * n, :].astype(F32)
            for hh in range(HEADS_PER_GROUP):
                on_scr[g - 1, hh, pl.ds(res, n, stride=dil), :] = rows[:, hh * HEAD_DIM:(hh + 1) * HEAD_DIM]
            ln_scr[g - 1, pl.ds(res, n, stride=dil), :] = l_ref[res * n:(res + 1) * n, :]

    def sub_tile(r0):
        rs = slice(r0, r0 + MERGE_SUB)
        l0, l1, l2 = l0_ref[rs, :], ln_scr[0, rs, :], ln_scr[1, rs, :]
        lm = jnp.maximum(jnp.maximum(l0, l1), l2)
        e0, e1, e2 = jnp.exp(l0 - lm), jnp.exp(l1 - lm), jnp.exp(l2 - lm)
        inv = 1.0 / (e0 + e1 + e2)
        parts = []
        for hh in range(HEADS_PER_GROUP):
            sl = slice(hh * HEAD_DIM, (hh + 1) * HEAD_DIM)
            acc = (e0[:, hh:hh + 1] * o0_ref[rs, sl].astype(F32)
                   + e1[:, hh:hh + 1] * on_scr[0, hh, rs, :]
                   + e2[:, hh:hh + 1] * on_scr[1, hh, rs, :])
            parts.append((acc * inv[:, hh:hh + 1]).astype(BF16))
        attn = jnp.concatenate(parts, axis=1)
        yield

        branch_a = _dot(attn, wa_ref[...])
        branch_r = _dot(retg_ref[rs, :], wr_ref[...])
        yield
        merged = (_sigmoid(ga_ref[rs, :].astype(F32)) * branch_a
                  + _sigmoid(gr_ref[rs, :].astype(F32)) * branch_r)
        yield
        y = _dot(merged.astype(BF16), wo_ref[...])
        yield

        h1 = _ln(DEEPNORM_ALPHA * x_ref[rs, :] + mod_ref[2:3, :] * y) * ln_ref[0:1, :] + ln_ref[1:2, :]
        h1_ref[rs, :] = h1
        u2 = _ln(h1) * (1.0 + mod_ref[4:5, :]) + mod_ref[3:4, :]
        _store_token_tiles(u2_ref, u2, r0)
        uh, ul = _split_bf16(u2)
        yield
        lg_ref[rs, :] = (_dot(uh, wrh_ref[...]) + _dot(uh, wrl_ref[...]) + _dot(ul, wrh_ref[...])
                         + rb_ref[...])

    _interleave(sub_tile(r0) for r0 in range(0, MERGE_TM, MERGE_SUB))

    for cp in zero_copies:
        cp.wait()


def _merge(outs, lses, retg, rest, x2, mod3, wa, wr, wo, ln1, wr_hi, wr_lo, rbias, seq, n_rows):
    t = x2.shape[0]
    tm = MERGE_TM
    per_seq = seq // tm
    zrows, rem = divmod(n_rows * ROW_TILE, (t // tm) * ZERO_SPLIT)
    assert rem == 0 and zrows % 8 == 0
    row = lambda w: pl.BlockSpec((tm, w), lambda i: (i, 0))
    full = lambda a: pl.BlockSpec(a.shape, lambda i: (0,) * a.ndim)
    return pl.pallas_call(
        _merge_kernel,
        grid=(t // tm,),
        in_specs=[row(GROUP_WIDTH)] * 3 + [row(LANES)] * 3 + [
            row(RET_HEADS * RET_V),
            pl.BlockSpec((tm, D_MODEL), lambda i: (i, 6)),
            pl.BlockSpec((tm, D_MODEL), lambda i: (i, 7)),
            row(D_MODEL),
            pl.BlockSpec((None, 6, D_MODEL), lambda i: (i // per_seq, 0, 0)),
            full(wa), full(wr), full(wo), full(ln1), full(wr_hi), full(wr_lo), full(rbias)],
        out_specs=[row(D_MODEL), pl.BlockSpec((tm * ROW_TILE, LANES), lambda i: (i, 0)), row(LANES),
                   pl.BlockSpec(memory_space=pl.ANY)],
        out_shape=[jax.ShapeDtypeStruct((t, D_MODEL), F32),
                   jax.ShapeDtypeStruct((t * ROW_TILE, LANES), F32),
                   jax.ShapeDtypeStruct((t, LANES), F32),
                   jax.ShapeDtypeStruct((n_rows * ROW_TILE, LANES), F32)],
        scratch_shapes=[pltpu.VMEM((2, HEADS_PER_GROUP, tm, HEAD_DIM), F32), pltpu.VMEM((2, tm, LANES), F32),
                        pltpu.VMEM((zrows, LANES), F32), pltpu.SemaphoreType.DMA(())],
        compiler_params=_cparams(("arbitrary",)),
        name="merge",
    )(*outs, *lses, retg, rest, rest, x2, mod3, wa, wr, wo, ln1, wr_hi, wr_lo, rbias)


ROUTE_TM = 512
BIG = 1 << 20


def _route_kernel(lg_ref, cols_ref, ints_ref, cnt_ref, carry, tri):
    i = pl.program_id(0)
    tm = ROUTE_TM

    @pl.when(i == 0)
    def _():
        carry[...] = jnp.zeros_like(carry)
        r_i = lax.broadcasted_iota(I32, (tm, tm), 0)
        c_i = lax.broadcasted_iota(I32, (tm, tm), 1)
        tri[...] = jnp.where(r_i > c_i, 1.0, 0.0).astype(BF16)

    lg = lg_ref[...]
    lane = lax.broadcasted_iota(I32, (tm, LANES), 1)
    lane_f = lane.astype(F32)
    first = lambda mask: jnp.min(jnp.where(mask, lane_f, float(BIG)), axis=-1, keepdims=True).astype(I32)

    coarse = jnp.where(lane < N_GROUPS, lg, NEG)
    cmax = jnp.max(coarse, axis=-1, keepdims=True)
    gsel = first(coarse == cmax)
    p_group = 1.0 / jnp.sum(jnp.exp(coarse - cmax), axis=-1, keepdims=True)

    lo = N_GROUPS + EXPERTS_PER_GROUP * gsel
    fine = jnp.where((lane >= lo) & (lane < lo + EXPERTS_PER_GROUP), lg, NEG)
    v1 = jnp.max(fine, axis=-1, keepdims=True)
    i1 = first(fine == v1)
    fine2 = jnp.where(lane == i1, NEG, fine)
    v2 = jnp.max(fine2, axis=-1, keepdims=True)
    i2 = first(fine2 == v2)
    ex = jnp.exp(v2 - v1)
    den = 1.0 / (1.0 + ex)
    gate1 = p_group * den
    gate2 = p_group * (ex * den)
    e1 = i1 - N_GROUPS
    e2 = i2 - N_GROUPS

    oh1 = lane == e1
    oh2 = lane == e2
    cnt = jnp.where(oh1 | oh2, 1.0, 0.0)
    rank = _dot(tri[...], cnt.astype(BF16)) + carry[...]
    r1 = jnp.sum(jnp.where(oh1, rank, 0.0), axis=-1, keepdims=True)
    r2 = jnp.sum(jnp.where(oh2, rank, 0.0), axis=-1, keepdims=True)
    carry[...] = carry[...] + jnp.sum(cnt, axis=0, keepdims=True)
    cnt_ref[...] = jnp.broadcast_to(carry[...], cnt_ref.shape)

    cols_ref[...] = jnp.where(lane == 0, gate1, jnp.where(lane == 1, gate2, 0.0))
    packed = jnp.where(lane == 0, e1.astype(F32),
                       jnp.where(lane == 1, e2.astype(F32),
                                 jnp.where(lane == 2, r1, jnp.where(lane == 3, r2, 0.0))))
    ints_ref[...] = packed.T[0:8, :].astype(I32)


def _route(logits):
    t = logits.shape[0]
    tm = ROUTE_TM
    return pl.pallas_call(
        _route_kernel,
        grid=(t // tm,),
        in_specs=[pl.BlockSpec((tm, LANES), lambda i: (i, 0))],
        out_specs=[pl.BlockSpec((tm, LANES), lambda i: (i, 0)),
                   pl.BlockSpec((8, tm), lambda i: (0, i)),
                   pl.BlockSpec((8, LANES), lambda i: (0, 0))],
        out_shape=[jax.ShapeDtypeStruct((t, LANES), F32),
                   jax.ShapeDtypeStruct((8, t), I32),
                   jax.ShapeDtypeStruct((8, LANES), F32)],
        scratch_shapes=[pltpu.VMEM((1, LANES), F32), pltpu.VMEM((tm, tm), BF16)],
        compiler_params=_cparams(("arbitrary",)),
        name="route",
    )(logits)


def _plan_kernel(ints_ref, cnt_ref, dest_ref, meta_ref, *, n_blocks_pad):
    sub = lax.broadcasted_iota(I32, (LANES, LANES), 0)
    lane = lax.broadcasted_iota(I32, (LANES, LANES), 1)
    cnt = cnt_ref[0:1, :]
    nblk_row = jnp.floor((cnt + (MOE_BLK - 1.0)) * (1.0 / MOE_BLK))
    nblk_mat = jnp.broadcast_to(nblk_row, (LANES, LANES))
    start_col = jnp.sum(jnp.where(lane < sub, nblk_mat, 0.0), axis=-1, keepdims=True)
    nblk_col = jnp.sum(jnp.where(lane == sub, nblk_mat, 0.0), axis=-1, keepdims=True)
    end_col = start_col + nblk_col

    ints = ints_ref[...]
    base = jnp.zeros(ints.shape, F32)
    for e in range(N_EXPERTS):
        base = jnp.where(ints == e, start_col[e:e + 1, :] * float(MOE_BLK), base)
    dest = base[0:2, :].astype(I32) + ints[2:4, :]
    dest_ref[...] = jnp.concatenate([dest, jnp.zeros((6, ints.shape[1]), I32)], axis=0)

    blk = lax.broadcasted_iota(I32, (LANES, n_blocks_pad), 1).astype(F32)
    e_sub = lax.broadcasted_iota(I32, (LANES, n_blocks_pad), 0)
    done = jnp.where((e_sub < N_EXPERTS) & (end_col <= blk), 1.0, 0.0)
    bexp = jnp.minimum(jnp.sum(done, axis=0, keepdims=True), N_EXPERTS - 1.0)
    used = jnp.sum(nblk_row, axis=-1, keepdims=True)
    row = lax.broadcasted_iota(I32, (8, n_blocks_pad), 0)
    meta = jnp.where(row == 0, bexp, jnp.where(row == 1, used, 0.0))
    meta_ref[...] = meta.astype(I32)


def _plan(ints, counts, n_blocks_pad):
    t = ints.shape[1]
    return pl.pallas_call(
        functools.partial(_plan_kernel, n_blocks_pad=n_blocks_pad),
        out_shape=[jax.ShapeDtypeStruct((8, t), I32), jax.ShapeDtypeStruct((8, n_blocks_pad), I32)],
        compiler_params=pltpu.CompilerParams(vmem_limit_bytes=VMEM_LIMIT),
        name="plan",
    )(ints, counts)


DISPATCH_TM = 2048
DMA_UNROLL = 16


def _row_tile(ref, r):
    return ref.at[pl.ds(pl.multiple_of(r * ROW_TILE, ROW_TILE), ROW_TILE)]


def _dispatch_kernel(d0_ref, d1_ref, u2_ref, rows_in_ref, rows_ref, sem):
    del rows_in_ref

    def issue(t, carry):
        src = _row_tile(u2_ref, t)
        pltpu.make_async_copy(src, _row_tile(rows_ref, d0_ref[t]), sem).start(priority=0)
        pltpu.make_async_copy(src, _row_tile(rows_ref, d1_ref[t]), sem).start(priority=1)
        return carry

    lax.fori_loop(0, DISPATCH_TM, issue, 0, unroll=DMA_UNROLL)
    for _ in range(2):
        pltpu.make_async_copy(u2_ref, rows_ref.at[pl.ds(0, DISPATCH_TM * ROW_TILE)], sem).wait()


def _dispatch(dest0, dest1, u2, rows0):
    t = u2.shape[0] // ROW_TILE
    idx = pl.BlockSpec((DISPATCH_TM,), lambda i: (i,), memory_space=pltpu.SMEM)
    return pl.pallas_call(
        _dispatch_kernel,
        grid=(t // DISPATCH_TM,),
        in_specs=[idx, idx,
                  pl.BlockSpec((DISPATCH_TM * ROW_TILE, LANES), lambda i: (i, 0)),
                  pl.BlockSpec(memory_space=pl.ANY)],
        out_specs=pl.BlockSpec(memory_space=pl.ANY),
        out_shape=jax.ShapeDtypeStruct(rows0.shape, F32),
        scratch_shapes=[pltpu.SemaphoreType.DMA(())],
        input_output_aliases={3: 0},
        compiler_params=_cparams(("arbitrary",)),
        name="dispatch",
    )(dest0, dest1, u2, rows0)


def _expert_runs(bexp, used):
    n = bexp.shape[0]
    idx = jnp.arange(n, dtype=I32)
    first = (idx < used[0]) & ((idx == 0) | (bexp != jnp.roll(bexp, 1)))
    slot = (jnp.cumsum(first.astype(I32)) - 1) % 2
    first_at_or_after = lax.cummin(jnp.where(first, idx, n)[::-1])[::-1]
    first_after = jnp.concatenate([first_at_or_after[1:], jnp.full((1,), n, I32)])
    nxt = jnp.where(first_after < n, bexp[jnp.minimum(first_after, n - 1)], -1)
    return first.astype(I32), slot.astype(I32), nxt.astype(I32)


def _experts_kernel(bexp_ref, used_ref, first_ref, slot_ref, nxt_ref, x_ref, w1_hbm, w3_hbm, w2_hbm, y_ref,
                    wb1, wb3, wb2, w1s, w3s, w2s, sem):
    i = pl.program_id(0)
    active = i < used_ref[0]
    slot = slot_ref[i]

    def fetch(e, s):
        return [pltpu.make_async_copy(w.at[e], buf.at[s], sem.at[s])
                for w, buf in ((w1_hbm, wb1), (w3_hbm, wb3), (w2_hbm, wb2))]

    @pl.when(i == 0)
    def _():
        for cp in fetch(bexp_ref[0], 0):
            cp.start()

    @pl.when(first_ref[i] == 1)
    def _():
        @pl.when(nxt_ref[i] >= 0)
        def _():
            for cp in fetch(nxt_ref[i], 1 - slot):
                cp.start()

        for cp in fetch(bexp_ref[i], slot):
            cp.wait()
        w1s[...] = wb1[slot].astype(BF16)
        w3s[...] = wb3[slot].astype(BF16)
        w2s[...] = wb2[slot].astype(BF16)

    @pl.when(active)
    def _():
        def sub_block(r0):
            xb = _load_token_tiles(x_ref, MOE_SUB, r0).astype(BF16)
            yield
            a = _dot(xb, w1s[...])
            b = _dot(xb, w3s[...])
            yield
            hdn = (a * _sigmoid(a) * b).astype(BF16)
            yield
            _store_token_tiles(y_ref, _dot(hdn, w2s[...]), r0)

        _interleave(sub_block(r0) for r0 in range(0, MOE_BLK, MOE_SUB))


def _experts(bexp, used, rows, w1, w3, w2):
    n_blocks = rows.shape[0] // (MOE_BLK * ROW_TILE)
    first, slot, nxt = _expert_runs(bexp, used)
    any_space = pl.BlockSpec(memory_space=pl.ANY)
    row_blk = pl.BlockSpec((MOE_BLK * ROW_TILE, LANES), lambda i, be, nu, *_: (jnp.minimum(i, nu[0] - 1), 0))
    grid_spec = pltpu.PrefetchScalarGridSpec(
        num_scalar_prefetch=5,
        grid=(n_blocks,),
        in_specs=[row_blk, any_space, any_space, any_space],
        out_specs=row_blk,
        scratch_shapes=[pltpu.VMEM((2, D_MODEL, EXPERT_FF), F32), pltpu.VMEM((2, D_MODEL, EXPERT_FF), F32),
                        pltpu.VMEM((2, EXPERT_FF, D_MODEL), F32),
                        pltpu.VMEM((D_MODEL, EXPERT_FF), BF16), pltpu.VMEM((D_MODEL, EXPERT_FF), BF16),
                        pltpu.VMEM((EXPERT_FF, D_MODEL), BF16), pltpu.SemaphoreType.DMA((2,))],
    )
    return pl.pallas_call(
        _experts_kernel,
        grid_spec=grid_spec,
        out_shape=jax.ShapeDtypeStruct(rows.shape, F32),
        input_output_aliases={5: 0},
        compiler_params=_cparams(("arbitrary",)),
        name="experts",
    )(bexp, used, first, slot, nxt, rows, w1, w3, w2)


COMBINE_TM = 256


def _combine_kernel(d0_ref, d1_ref, n0_ref, n1_ref, y_ref, cols_ref, h1_ref, mod_ref, ln_ref, o_ref, ya, yb, sem):
    i = pl.program_id(0)
    slot = i % 2

    def gather(i0_ref, i1_ref, s):
        def issue(t, carry):
            pltpu.make_async_copy(_row_tile(y_ref, i0_ref[t]), _row_tile(ya.at[s], t), sem.at[s]).start(priority=0)
            pltpu.make_async_copy(_row_tile(y_ref, i1_ref[t]), _row_tile(yb.at[s], t), sem.at[s]).start(priority=1)
            return carry

        lax.fori_loop(0, COMBINE_TM, issue, 0, unroll=DMA_UNROLL)

    @pl.when(i == 0)
    def _():
        gather(d0_ref, d1_ref, 0)

    @pl.when(i + 1 < pl.num_programs(0))
    def _():
        gather(n0_ref, n1_ref, 1 - slot)

    for buf in (ya, yb):
        pltpu.make_async_copy(y_ref.at[pl.ds(0, COMBINE_TM * ROW_TILE)], buf.at[slot], sem.at[slot]).wait()

    cols = cols_ref[...]
    moe = (cols[:, 0:1] * _load_token_tiles(ya.at[slot], COMBINE_TM)
           + cols[:, 1:2] * _load_token_tiles(yb.at[slot], COMBINE_TM))
    pre = DEEPNORM_ALPHA * h1_ref[...] + mod_ref[5:6, :] * moe
    o_ref[...] = _ln(pre) * ln_ref[0:1, :] + ln_ref[1:2, :]


def _combine(dest0, dest1, y_rows, cols, h1, mod3, ln2, seq):
    t = h1.shape[0]
    tm = COMBINE_TM
    per_seq = seq // tm
    idx = pl.BlockSpec((tm,), lambda i: (i,), memory_space=pltpu.SMEM)
    idx_next = pl.BlockSpec((tm,), lambda i: (jnp.minimum(i + 1, t // tm - 1),), memory_space=pltpu.SMEM)
    return pl.pallas_call(
        _combine_kernel,
        grid=(t // tm,),
        in_specs=[idx, idx, idx_next, idx_next,
                  pl.BlockSpec(memory_space=pl.ANY),
                  pl.BlockSpec((tm, LANES), lambda i: (i, 0)),
                  pl.BlockSpec((tm, D_MODEL), lambda i: (i, 0)),
                  pl.BlockSpec((None, 6, D_MODEL), lambda i: (i // per_seq, 0, 0)),
                  pl.BlockSpec((2, D_MODEL), lambda i: (0, 0))],
        out_specs=pl.BlockSpec((tm, D_MODEL), lambda i: (i, 0)),
        out_shape=jax.ShapeDtypeStruct((t, D_MODEL), F32),
        scratch_shapes=[pltpu.VMEM((2, tm * ROW_TILE, LANES), F32), pltpu.VMEM((2, tm * ROW_TILE, LANES), F32),
                        pltpu.SemaphoreType.DMA((2,))],
        compiler_params=_cparams(("arbitrary",)),
        name="combine",
    )(dest0, dest1, dest0, dest1, y_rows, cols, h1, mod3, ln2)


def _layer(h, c8, w_ada, b_ada, w_in, w_attn_out, decay_f, decay_b, gn_gain, w_ret_out, w_out,
           ln1_gain, ln1_bias, w_coarse, b_coarse, w_fine, b_fine, w1, w3, w2, ln2_gain, ln2_bias):
    batch, seq, d = h.shape
    t = batch * seq
    x2 = h.reshape(t, d)

    mod = _ada(c8, w_ada, b_ada.reshape(1, -1))
    mod3 = mod[:batch].reshape(batch, 6, d)

    wb = w_in.astype(BF16)
    qkv, u1 = _inproj_attn(x2, mod3, wb, seq)
    rest = _inproj_rest(u1, wb)

    outs, lses = zip(*[_attention(qkv, g, batch, seq) for g in range(ATTN_GROUPS)])
    decays = jnp.concatenate([decay_f, decay_b]).astype(F32)
    retg = _retention(rest, decays, gn_gain.reshape(1, -1), batch, seq)

    w_route = jnp.concatenate([w_coarse, w_fine.transpose(1, 0, 2).reshape(d, N_EXPERTS)], axis=1)
    n_route = w_route.shape[1]
    w_route = jnp.pad(w_route, ((0, 0), (0, LANES - n_route)))
    wr_hi, wr_lo = _split_bf16(w_route)
    rbias = jnp.pad(jnp.concatenate([b_coarse, b_fine.reshape(-1)]), (0, LANES - n_route)).reshape(1, LANES)
    ln1 = jnp.stack([ln1_gain, ln1_bias])
    n_blocks = 2 * t // MOE_BLK + N_EXPERTS
    h1, u2, logits, rows0 = _merge(outs, lses, retg, rest, x2, mod3, w_attn_out.astype(BF16),
                                   w_ret_out.astype(BF16), w_out.astype(BF16), ln1, wr_hi, wr_lo, rbias, seq,
                                   n_blocks * MOE_BLK)

    cols, ints, counts = _route(logits)
    n_blocks_pad = -(-n_blocks // LANES) * LANES
    dest, meta = _plan(ints, counts, n_blocks_pad)
    dest0, dest1 = dest[0], dest[1]
    rows = _dispatch(dest0, dest1, u2, rows0)
    y_rows = _experts(meta[0, :n_blocks], meta[1, :1], rows, w1, w3, w2)
    out = _combine(dest0, dest1, y_rows, cols, h1, mod3, jnp.stack([ln2_gain, ln2_bias]), seq)
    return out.reshape(batch, seq, d)


def kernel(x, c, w_ada, b_ada, w_in, w_attn_out, ret_decay_fwd, ret_decay_bwd, ret_gn_gain, w_ret_out, w_out,
           ln1_gain, ln1_bias, w_coarse, b_coarse, w_fine, b_fine, w1, w3, w2, ln2_gain, ln2_bias):
    batch = x.shape[0]
    assert batch <= 8 and x.shape[1] % (2 * INPROJ_TM) == 0 and x.shape[2] == D_MODEL
    c8 = jnp.pad(c, ((0, 8 - batch), (0, 0)))
    h = x
    for l in range(w_ada.shape[0]):
        h = _layer(h, c8, w_ada[l], b_ada[l], w_in[l], w_attn_out[l], ret_decay_fwd[l], ret_decay_bwd[l],
                   ret_gn_gain[l], w_ret_out[l], w_out[l], ln1_gain[l], ln1_bias[l], w_coarse[l], b_coarse[l],
                   w_fine[l], b_fine[l], w1[l], w3[l], w2[l], ln2_gain[l], ln2_bias[l])
    return h
```

```python
import functools

import jax
import jax.numpy as jnp
from jax import lax
from jax.experimental import pallas as pl
from jax.experimental.pallas import tpu as pltpu

F32 = jnp.float32
BF16 = jnp.bfloat16
I32 = jnp.int32

D_MODEL = 1024
ATTN_GROUPS = 3
HEADS_PER_GROUP = 4
HEAD_DIM = 128
ATTN_HEADS = ATTN_GROUPS * HEADS_PER_GROUP
GROUP_WIDTH = HEADS_PER_GROUP * HEAD_DIM
ATTN_PATTERNS = ((128, 1), (512, 4), (2048, 16))
ALIBI_MAX_EXP = 8.0
NEG = -1e30
RET_HEADS = 4
RET_QK = 256
RET_V = 512
N_GROUPS = 4
EXPERTS_PER_GROUP = 8
N_EXPERTS = N_GROUPS * EXPERTS_PER_GROUP
EXPERT_FF = 512
DEPTH = 1
DEEPNORM_ALPHA = (2.0 * DEPTH) ** 0.25
LN_EPS = 1e-5

LANES = 128
SUBLANES = 8
PERM_TILE = 512
ATTN_QB = 128
ATTN_HALF = 64
ATTN_KB = ATTN_QB + 2 * ATTN_HALF
RET_CHUNK = 256
RET_STEP = 2048
MOE_BLK = 512
MOE_SUB = 256
VMEM_LIMIT = 56 * 1024 * 1024


def _cparams(sem):
    return pltpu.CompilerParams(dimension_semantics=sem, vmem_limit_bytes=VMEM_LIMIT)


def _split_bf16(a):
    hi = a.astype(BF16)
    lo = (a - hi.astype(F32)).astype(BF16)
    return hi, lo


def _dot(a, b):
    return jnp.dot(a, b, preferred_element_type=F32)


def _dot3(a, b):
    ah, al = _split_bf16(a)
    bh, bl = _split_bf16(b)
    return _dot(ah, bh) + _dot(ah, bl) + _dot(al, bh)


def _ln(x):
    mu = jnp.mean(x, axis=-1, keepdims=True)
    xc = x - mu
    var = jnp.mean(xc * xc, axis=-1, keepdims=True)
    return xc * lax.rsqrt(var + LN_EPS)


def _sigmoid(x):
    return 1.0 / (1.0 + jnp.exp(-x))


ROW_TILE = D_MODEL // LANES


def _store_token_tiles(ref, val, row0=0):
    n = val.shape[0]
    for cc in range(ROW_TILE):
        ref[pl.ds(row0 * ROW_TILE + cc, n, stride=ROW_TILE), :] = val[:, cc * LANES:(cc + 1) * LANES]


def _load_token_tiles(ref, n, row0=0):
    return jnp.concatenate([ref[pl.ds(row0 * ROW_TILE + cc, n, stride=ROW_TILE), :] for cc in range(ROW_TILE)],
                           axis=1)


def _interleave(phased):
    pending = []
    for gen in phased:
        pending.append(gen)
        pending = [g for g in pending if next(g, StopIteration) is not StopIteration]
    while pending:
        pending = [g for g in pending if next(g, StopIteration) is not StopIteration]


def _ada_kernel(c_ref, w_ref, b_ref, o_ref):
    o_ref[...] = _dot3(c_ref[...], w_ref[...]) + b_ref[...]


def _ada(c8, w_ada, b_ada):
    n = w_ada.shape[1]
    return pl.pallas_call(
        _ada_kernel,
        grid=(n // D_MODEL,),
        in_specs=[pl.BlockSpec((SUBLANES, D_MODEL), lambda j: (0, 0)),
                  pl.BlockSpec((D_MODEL, D_MODEL), lambda j: (0, j)),
                  pl.BlockSpec((1, D_MODEL), lambda j: (0, j))],
        out_specs=pl.BlockSpec((SUBLANES, D_MODEL), lambda j: (0, j)),
        out_shape=jax.ShapeDtypeStruct((SUBLANES, n), F32),
        compiler_params=_cparams(("arbitrary",)),
        name="ada",
    )(c8, w_ada, b_ada)


INPROJ_TM = 1024
LN_ROWS = 256
LN_UNROLL = 4


def _modulated_ln_rows(x_ref, mod_ref, emit):
    gain = 1.0 + mod_ref[1:2, :]
    shift = mod_ref[0:1, :]

    def chunk(c, carry):
        r0 = pl.multiple_of(c * LN_ROWS, LN_ROWS)
        emit(r0, _ln(x_ref[pl.ds(r0, LN_ROWS), :]) * gain + shift)
        return carry

    lax.fori_loop(0, INPROJ_TM // LN_ROWS, chunk, 0, unroll=LN_UNROLL)


def _inproj_attn_kernel(x_ref, mod_ref, wq_ref, wk_ref, wv_ref, o_ref, un_ref, uf_scr, u_scr):
    j = pl.program_id(1)

    @pl.when(j == 0)
    def _():
        def emit(r0, u):
            for cc in range(D_MODEL // LANES):
                uf_scr[cc, pl.ds(r0, LN_ROWS), :] = u[:, cc * LANES:(cc + 1) * LANES]
            ub = u.astype(BF16)
            u_scr[0, pl.ds(r0, LN_ROWS), :] = ub
            un_ref[pl.ds(r0, LN_ROWS), :] = ub

        _modulated_ln_rows(x_ref, mod_ref, emit)
        for g in (1, 2):
            dil = ATTN_PATTERNS[g][1]
            n = PERM_TILE // dil
            for t0 in range(0, INPROJ_TM, PERM_TILE):
                for res in range(dil):
                    for cc in range(D_MODEL // LANES):
                        rows = uf_scr[cc, pl.ds(t0 + res, n, stride=dil), :]
                        u_scr[g, t0 + res * n:t0 + (res + 1) * n, cc * LANES:(cc + 1) * LANES] = rows.astype(BF16)

    u = u_scr[j]
    for s, w_ref in enumerate((wq_ref, wk_ref, wv_ref)):
        o_ref[:, s * GROUP_WIDTH:(s + 1) * GROUP_WIDTH] = _dot(u, w_ref[...]).astype(BF16)


ATTN_WIDTH = ATTN_HEADS * HEAD_DIM
W_RQ0 = 3 * ATTN_WIDTH
W_RK0 = W_RQ0 + RET_HEADS * RET_QK
W_RV0 = W_RK0 + RET_HEADS * RET_QK
W_TAIL0 = W_RV0 + RET_HEADS * RET_V


def _inproj_attn(x2, mod3, wb, seq):
    t = x2.shape[0]
    tn = 3 * GROUP_WIDTH
    per_seq = seq // INPROJ_TM
    groups_per_range = ATTN_WIDTH // GROUP_WIDTH
    wspec = lambda s: pl.BlockSpec((D_MODEL, GROUP_WIDTH), lambda i, j: (0, s * groups_per_range + j))
    return pl.pallas_call(
        _inproj_attn_kernel,
        grid=(t // INPROJ_TM, ATTN_GROUPS),
        in_specs=[pl.BlockSpec((INPROJ_TM, D_MODEL), lambda i, j: (i, 0)),
                  pl.BlockSpec((None, 6, D_MODEL), lambda i, j: (i // per_seq, 0, 0)),
                  wspec(0), wspec(1), wspec(2)],
        out_specs=[pl.BlockSpec((INPROJ_TM, tn), lambda i, j: (i, j)),
                   pl.BlockSpec((INPROJ_TM, D_MODEL), lambda i, j: (i, 0))],
        out_shape=[jax.ShapeDtypeStruct((t, ATTN_GROUPS * tn), BF16),
                   jax.ShapeDtypeStruct((t, D_MODEL), BF16)],
        scratch_shapes=[pltpu.VMEM((D_MODEL // LANES, INPROJ_TM, LANES), F32),
                        pltpu.VMEM((ATTN_GROUPS, INPROJ_TM, D_MODEL), BF16)],
        compiler_params=_cparams(("arbitrary", "arbitrary")),
        name="inproj_attn",
    )(x2, mod3, wb, wb, wb)


REST_TN = RET_HEAD_COLS = 2 * RET_QK + RET_V
REST_TAIL_TILES = (RET_HEADS * RET_V + 2 * D_MODEL) // REST_TN
REST_TM = 2048


def _inproj_rest_kernel(u_ref, wq_ref, wk_ref, wv_ref, wt0_ref, wt1_ref, o_ref):
    j = pl.program_id(0)

    def project(parts):
        u = u_ref[...]
        c0 = 0
        for w_ref in parts:
            wd = w_ref.shape[1]
            o_ref[:, c0:c0 + wd] = _dot(u, w_ref[...]).astype(BF16)
            c0 += wd

    @pl.when(j < RET_HEADS)
    def _():
        project((wq_ref, wk_ref, wv_ref))

    @pl.when(j >= RET_HEADS)
    def _():
        project((wt0_ref, wt1_ref))


def _inproj_rest(u, wb):
    t = u.shape[0]
    half = REST_TN // 2
    assert W_TAIL0 % half == 0 and W_RV0 % RET_V == 0
    head = lambda j: jnp.minimum(j, RET_HEADS - 1)
    tail = lambda j: jnp.maximum(j - RET_HEADS, 0)
    return pl.pallas_call(
        _inproj_rest_kernel,
        grid=(RET_HEADS + REST_TAIL_TILES, t // REST_TM),
        in_specs=[pl.BlockSpec((REST_TM, D_MODEL), lambda j, i: (i, 0)),
                  pl.BlockSpec((D_MODEL, RET_QK), lambda j, i: (0, W_RQ0 // RET_QK + head(j))),
                  pl.BlockSpec((D_MODEL, RET_QK), lambda j, i: (0, W_RK0 // RET_QK + head(j))),
                  pl.BlockSpec((D_MODEL, RET_V), lambda j, i: (0, W_RV0 // RET_V + head(j))),
                  pl.BlockSpec((D_MODEL, half), lambda j, i: (0, W_TAIL0 // half + 2 * tail(j))),
                  pl.BlockSpec((D_MODEL, half), lambda j, i: (0, W_TAIL0 // half + 2 * tail(j) + 1))],
        out_specs=pl.BlockSpec((REST_TM, REST_TN), lambda j, i: (i, j)),
        out_shape=jax.ShapeDtypeStruct((t, (RET_HEADS + REST_TAIL_TILES) * REST_TN), BF16),
        compiler_params=_cparams(("arbitrary", "arbitrary")),
        name="inproj_rest",
    )(u, wb, wb, wb, wb, wb)


ATTN_OFFSETS = ATTN_KB // ATTN_HALF - 1
ATTN_UNROLL = 8
ATTN_MIN_BLOCKS = 4


def _attn_kernel(q_ref, k_ref, v_ref, o_ref, lse_ref, bias_scr, *scratch, group, dil, nt, n, nres):
    n_sub = nt * n
    if nt == 1:
        seqs = [(q_ref.at[0, rr], k_ref.at[0, rr], v_ref.at[0, rr], o_ref.at[0, rr], lse_ref.at[0, rr])
                for rr in range(nres)]
    else:
        seqs = [tuple(s.at[rr] for s in scratch) for rr in range(nres)]
        for rr, (qs, ks, vs, _, _) in enumerate(seqs):
            for t in range(nt):
                qs[t * n:(t + 1) * n, :] = q_ref[t, rr]
                ks[t * n:(t + 1) * n, :] = k_ref[t, rr]
                vs[t * n:(t + 1) * n, :] = v_ref[t, rr]

    @pl.when((pl.program_id(0) == 0) & (pl.program_id(1) == 0))
    def _():
        base = (lax.broadcasted_iota(I32, (ATTN_QB, ATTN_KB), 1)
                - lax.broadcasted_iota(I32, (ATTN_QB, ATTN_KB), 0))
        for j in range(ATTN_OFFSETS):
            dist = jnp.abs(base - j * ATTN_HALF)
            for hh in range(HEADS_PER_GROUP):
                slope = dil * 2.0 ** (-ALIBI_MAX_EXP * (group * HEADS_PER_GROUP + hh + 1) / ATTN_HEADS)
                bias_scr[hh * ATTN_OFFSETS + j] = jnp.where(dist <= ATTN_HALF, -slope * dist.astype(F32), NEG)

    scale = HEAD_DIM ** -0.5
    lane = lax.broadcasted_iota(I32, (ATTN_QB, LANES), 1)

    def block(blk, carry):
        q0 = pl.multiple_of(blk * ATTN_QB, ATTN_QB)
        start = pl.multiple_of(jnp.clip(q0 - ATTN_HALF, 0, n_sub - ATTN_KB), ATTN_HALF)
        j = (q0 - start) // ATTN_HALF
        for qs, ks, vs, os_, ls in seqs:
            lse_tile = jnp.zeros((ATTN_QB, LANES), F32)
            for hh in range(HEADS_PER_GROUP):
                cs = slice(hh * HEAD_DIM, (hh + 1) * HEAD_DIM)
                qb = qs[pl.ds(q0, ATTN_QB), cs]
                kb = ks[pl.ds(start, ATTN_KB), cs]
                vb = vs[pl.ds(start, ATTN_KB), cs]
                s = lax.dot_general(qb, kb, (((1,), (1,)), ((), ())), preferred_element_type=F32) * scale
                s = s + bias_scr[hh * ATTN_OFFSETS + j]
                m = jnp.max(s, axis=-1, keepdims=True)
                p = jnp.exp(s - m)
                l = jnp.sum(p, axis=-1, keepdims=True)
                o = _dot(p.astype(BF16), vb) * (1.0 / l)
                os_[pl.ds(q0, ATTN_QB), cs] = o.astype(BF16)
                lse_tile = jnp.where(lane == hh, m + jnp.log(l), lse_tile)
            ls[pl.ds(q0, ATTN_QB), :] = lse_tile
        return carry

    n_blk = n_sub // ATTN_QB
    lax.fori_loop(0, n_blk, block, 0, unroll=min(max(ATTN_UNROLL // nres, 1), n_blk))

    if nt > 1:
        for rr, (_, _, _, os_, ls) in enumerate(seqs):
            for t in range(nt):
                o_ref[t, rr] = os_[t * n:(t + 1) * n, :]
                lse_ref[t, rr] = ls[t * n:(t + 1) * n, :]


def _attention(qkv, group, batch, seq):
    dil = ATTN_PATTERNS[group][1]
    if dil == 1:
        nt, n = 1, seq
    else:
        nt, n = seq // PERM_TILE, PERM_TILE // dil
    n_sub = nt * n
    t = batch * seq
    qkv5 = qkv.reshape(batch, nt, dil, n, qkv.shape[1])
    cb = group * 3
    nres = min(dil, max(1, ATTN_MIN_BLOCKS * ATTN_QB // n_sub))
    blk = (None, nt, nres, n, GROUP_WIDTH)
    scratch = [pltpu.VMEM((HEADS_PER_GROUP * ATTN_OFFSETS, ATTN_QB, ATTN_KB), F32)]
    if nt > 1:
        scratch += [pltpu.VMEM((nres, n_sub, GROUP_WIDTH), BF16)] * 4 + [pltpu.VMEM((nres, n_sub, LANES), F32)]
    out, lse = pl.pallas_call(
        functools.partial(_attn_kernel, group=group, dil=dil, nt=nt, n=n, nres=nres),
        grid=(batch, dil // nres),
        in_specs=[pl.BlockSpec(blk, lambda b, r: (b, 0, r, 0, cb)),
                  pl.BlockSpec(blk, lambda b, r: (b, 0, r, 0, cb + 1)),
                  pl.BlockSpec(blk, lambda b, r: (b, 0, r, 0, cb + 2))],
        out_specs=[pl.BlockSpec(blk, lambda b, r: (b, 0, r, 0, 0)),
                   pl.BlockSpec((None, nt, nres, n, LANES), lambda b, r: (b, 0, r, 0, 0))],
        out_shape=[jax.ShapeDtypeStruct((batch, nt, dil, n, GROUP_WIDTH), BF16),
                   jax.ShapeDtypeStruct((batch, nt, dil, n, LANES), F32)],
        scratch_shapes=scratch,
        compiler_params=_cparams(("arbitrary", "arbitrary")),
        name=f"attn_g{group}",
    )(qkv5, qkv5, qkv5)
    return out.reshape(t, GROUP_WIDTH), lse.reshape(t, LANES)


def _log_sigmoid(x):
    return jnp.minimum(x, 0.0) - jnp.log(1.0 + jnp.exp(-jnp.abs(x)))


def _retention_kernel(decay_ref, qkv_ref, g_ref, gain_ref, o_ref, state, ybwd, dmat, kdec, qdec, *, nc):
    c = RET_CHUNK
    h = pl.program_id(1)
    i = pl.program_id(2)
    kscale = RET_QK ** -0.5

    def set_decays(lg, forward):
        row = lax.broadcasted_iota(I32, (c, c), 0)
        col = lax.broadcasted_iota(I32, (c, c), 1)
        pos = lax.broadcasted_iota(I32, (c, LANES), 0).astype(F32)
        if forward:
            gap, key_pow, query_pow = row - col, (c - 1.0) - pos, pos + 1.0
            keep = gap >= 0
        else:
            gap, key_pow, query_pow = col - row, pos, c - pos
            keep = gap > 0
        dmat[...] = jnp.where(keep, jnp.exp(lg * jnp.maximum(gap, 0).astype(F32)) * kscale, 0.0)
        kdec[...] = jnp.exp(lg * key_pow) * kscale
        qdec[...] = jnp.exp(lg * query_pow)

    def chunk_step(lg, r0, finish):
        q = qkv_ref[r0:r0 + c, 0:RET_QK]
        k = qkv_ref[r0:r0 + c, RET_QK:2 * RET_QK]
        v = qkv_ref[r0:r0 + c, 2 * RET_QK:]
        inner = lax.dot_general(q, k, (((1,), (1,)), ((), ())), preferred_element_type=F32) * dmat[...]
        y = _dot(inner.astype(BF16), v)
        kd = (k.astype(F32) * jnp.concatenate([kdec[...]] * (RET_QK // LANES), axis=1)).astype(BF16)
        kv = lax.dot_general(kd, v, (((0,), (0,)), ((), ())), preferred_element_type=F32)
        yield
        qd = jnp.concatenate([qdec[...]] * (RET_V // LANES), axis=1)
        y = y + _dot(q, state[...].astype(BF16)) * qd
        state[...] = state[...] * jnp.exp(lg * float(c)) + kv
        yield
        finish(r0, y)

    @pl.when((i == 0) | (i == nc))
    def _():
        state[...] = jnp.zeros_like(state)

    @pl.when(i < nc)
    def _():
        lg = _log_sigmoid(jnp.zeros((1, 1), F32) + decay_ref[RET_HEADS + h])

        @pl.when(i == 0)
        def _():
            set_decays(lg, False)

        base = pl.multiple_of((nc - 1 - i) * RET_STEP, RET_STEP)

        def finish(r0, y):
            ybwd[pl.ds(base + r0, c), :] = y

        _interleave(chunk_step(lg, r0, finish) for r0 in reversed(range(0, RET_STEP, c)))

    @pl.when(i >= nc)
    def _():
        lg = _log_sigmoid(jnp.zeros((1, 1), F32) + decay_ref[h])

        @pl.when(i == nc)
        def _():
            set_decays(lg, True)

        base = pl.multiple_of((i - nc) * RET_STEP, RET_STEP)

        def finish(r0, y):
            y = y + ybwd[pl.ds(base + r0, c), :]
            g = g_ref[r0:r0 + c, :].astype(F32)
            o_ref[r0:r0 + c, :] = (g * _sigmoid(g) * (_ln(y) * gain_ref[...])).astype(BF16)

        _interleave(chunk_step(lg, r0, finish) for r0 in range(0, RET_STEP, c))


def _retention(rest, decays, gn_gain, batch, seq):
    c = RET_STEP
    nc = seq // c
    rest3 = rest.reshape(batch, seq, rest.shape[1])
    gate_blk = RET_HEADS * RET_HEAD_COLS // RET_V

    def chunk(i):
        return jnp.where(i < nc, nc - 1 - i, i - nc)

    grid_spec = pltpu.PrefetchScalarGridSpec(
        num_scalar_prefetch=1,
        grid=(batch, RET_HEADS, 2 * nc),
        in_specs=[pl.BlockSpec((None, c, RET_HEAD_COLS), lambda b, h, i, d: (b, chunk(i), h)),
                  pl.BlockSpec((None, c, RET_V), lambda b, h, i, d: (b, jnp.maximum(i - nc, 0), gate_blk + h)),
                  pl.BlockSpec((1, RET_V), lambda b, h, i, d: (0, h))],
        out_specs=pl.BlockSpec((None, c, RET_V), lambda b, h, i, d: (b, jnp.maximum(i - nc, 0), h)),
        scratch_shapes=[pltpu.VMEM((RET_QK, RET_V), F32), pltpu.VMEM((seq, RET_V), F32),
                        pltpu.VMEM((RET_CHUNK, RET_CHUNK), F32), pltpu.VMEM((RET_CHUNK, LANES), F32),
                        pltpu.VMEM((RET_CHUNK, LANES), F32)],
    )
    out = pl.pallas_call(
        functools.partial(_retention_kernel, nc=nc),
        grid_spec=grid_spec,
        out_shape=jax.ShapeDtypeStruct((batch, seq, RET_HEADS * RET_V), BF16),
        compiler_params=_cparams(("arbitrary", "arbitrary", "arbitrary")),
        name="retention",
    )(decays, rest3, rest3, gn_gain)
    return out.reshape(batch * seq, RET_HEADS * RET_V)


MERGE_TM = PERM_TILE
MERGE_SUB = 256
ZERO_SPLIT = 4


def _merge_kernel(o0_ref, o1_ref, o2_ref, l0_ref, l1_ref, l2_ref, retg_ref, ga_ref, gr_ref, x_ref, mod_ref,
                  wa_ref, wr_ref, wo_ref, ln_ref, wrh_ref, wrl_ref, rb_ref,
                  h1_ref, u2_ref, lg_ref, zrows_ref, on_scr, ln_scr, zbuf, zsem):
    i = pl.program_id(0)

    @pl.when(i == 0)
    def _():
        zbuf[...] = jnp.zeros_like(zbuf)

    zrows = zbuf.shape[0]
    zero_copies = [pltpu.make_async_copy(zbuf, zrows_ref.at[pl.ds((i * ZERO_SPLIT + k) * zrows, zrows)], zsem)
                   for k in range(ZERO_SPLIT)]
    for cp in zero_copies:
        cp.start()

    for g in (1, 2):
        dil = ATTN_PATTERNS[g][1]
        n = MERGE_TM // dil
        o_ref, l_ref = ((o1_ref, l1_ref), (o2_ref, l2_ref))[g - 1]
        for res in range(dil):
            rows = o_ref[res * n:(res + 1) * n, :].astype(F32)
            for hh in range(HEADS_PER_GROUP):
                on_scr[g - 1, hh, pl.ds(res, n, stride=dil), :] = rows[:, hh * HEAD_DIM:(hh + 1) * HEAD_DIM]
            ln_scr[g - 1, pl.ds(res, n, stride=dil), :] = l_ref[res * n:(res + 1) * n, :]

    def sub_tile(r0):
        rs = slice(r0, r0 + MERGE_SUB)
        l0, l1, l2 = l0_ref[rs, :], ln_scr[0, rs, :], ln_scr[1, rs, :]
        lm = jnp.maximum(jnp.maximum(l0, l1), l2)
        e0, e1, e2 = jnp.exp(l0 - lm), jnp.exp(l1 - lm), jnp.exp(l2 - lm)
        inv = 1.0 / (e0 + e1 + e2)
        parts = []
        for hh in range(HEADS_PER_GROUP):
            sl = slice(hh * HEAD_DIM, (hh + 1) * HEAD_DIM)
            acc = (e0[:, hh:hh + 1] * o0_ref[rs, sl].astype(F32)
                   + e1[:, hh:hh + 1] * on_scr[0, hh, rs, :]
                   + e2[:, hh:hh + 1] * on_scr[1, hh, rs, :])
            parts.append((acc * inv[:, hh:hh + 1]).astype(BF16))
        attn = jnp.concatenate(parts, axis=1)
        yield

        branch_a = _dot(attn, wa_ref[...])
        branch_r = _dot(retg_ref[rs, :], wr_ref[...])
        yield
        merged = (_sigmoid(ga_ref[rs, :].astype(F32)) * branch_a
                  + _sigmoid(gr_ref[rs, :].astype(F32)) * branch_r)
        yield
        y = _dot(merged.astype(BF16), wo_ref[...])
        yield

        h1 = _ln(DEEPNORM_ALPHA * x_ref[rs, :] + mod_ref[2:3, :] * y) * ln_ref[0:1, :] + ln_ref[1:2, :]
        h1_ref[rs, :] = h1
        u2 = _ln(h1) * (1.0 + mod_ref[4:5, :]) + mod_ref[3:4, :]
        _store_token_tiles(u2_ref, u2, r0)
        uh, ul = _split_bf16(u2)
        yield
        lg_ref[rs, :] = (_dot(uh, wrh_ref[...]) + _dot(uh, wrl_ref[...]) + _dot(ul, wrh_ref[...])
                         + rb_ref[...])

    _interleave(sub_tile(r0) for r0 in range(0, MERGE_TM, MERGE_SUB))

    for cp in zero_copies:
        cp.wait()


def _merge(outs, lses, retg, rest, x2, mod3, wa, wr, wo, ln1, wr_hi, wr_lo, rbias, seq, n_rows):
    t = x2.shape[0]
    tm = MERGE_TM
    per_seq = seq // tm
    zrows, rem = divmod(n_rows * ROW_TILE, (t // tm) * ZERO_SPLIT)
    assert rem == 0 and zrows % SUBLANES == 0
    row = lambda w: pl.BlockSpec((tm, w), lambda i: (i, 0))
    full = lambda a: pl.BlockSpec(a.shape, lambda i: (0,) * a.ndim)
    return pl.pallas_call(
        _merge_kernel,
        grid=(t // tm,),
        in_specs=[row(GROUP_WIDTH)] * 3 + [row(LANES)] * 3 + [
            row(RET_HEADS * RET_V),
            pl.BlockSpec((tm, D_MODEL), lambda i: (i, 6)),
            pl.BlockSpec((tm, D_MODEL), lambda i: (i, 7)),
            row(D_MODEL),
            pl.BlockSpec((None, 6, D_MODEL), lambda i: (i // per_seq, 0, 0)),
            full(wa), full(wr), full(wo), full(ln1), full(wr_hi), full(wr_lo), full(rbias)],
        out_specs=[row(D_MODEL), pl.BlockSpec((tm * ROW_TILE, LANES), lambda i: (i, 0)), row(LANES),
                   pl.BlockSpec(memory_space=pl.ANY)],
        out_shape=[jax.ShapeDtypeStruct((t, D_MODEL), F32),
                   jax.ShapeDtypeStruct((t * ROW_TILE, LANES), F32),
                   jax.ShapeDtypeStruct((t, LANES), F32),
                   jax.ShapeDtypeStruct((n_rows * ROW_TILE, LANES), F32)],
        scratch_shapes=[pltpu.VMEM((2, HEADS_PER_GROUP, tm, HEAD_DIM), F32), pltpu.VMEM((2, tm, LANES), F32),
                        pltpu.VMEM((zrows, LANES), F32), pltpu.SemaphoreType.DMA(())],
        compiler_params=_cparams(("arbitrary",)),
        name="merge",
    )(*outs, *lses, retg, rest, rest, x2, mod3, wa, wr, wo, ln1, wr_hi, wr_lo, rbias)


ROUTE_TM = 512
BIG = 1 << 20


def _route_kernel(lg_ref, cols_ref, ints_ref, cnt_ref, carry, tri):
    i = pl.program_id(0)
    tm = ROUTE_TM

    @pl.when(i == 0)
    def _():
        carry[...] = jnp.zeros_like(carry)
        r_i = lax.broadcasted_iota(I32, (tm, tm), 0)
        c_i = lax.broadcasted_iota(I32, (tm, tm), 1)
        tri[...] = jnp.where(r_i > c_i, 1.0, 0.0).astype(BF16)

    lg = lg_ref[...]
    lane = lax.broadcasted_iota(I32, (tm, LANES), 1)
    lane_f = lane.astype(F32)
    first = lambda mask: jnp.min(jnp.where(mask, lane_f, float(BIG)), axis=-1, keepdims=True).astype(I32)

    coarse = jnp.where(lane < N_GROUPS, lg, NEG)
    cmax = jnp.max(coarse, axis=-1, keepdims=True)
    gsel = first(coarse == cmax)
    p_group = 1.0 / jnp.sum(jnp.exp(coarse - cmax), axis=-1, keepdims=True)

    lo = N_GROUPS + EXPERTS_PER_GROUP * gsel
    fine = jnp.where((lane >= lo) & (lane < lo + EXPERTS_PER_GROUP), lg, NEG)
    v1 = jnp.max(fine, axis=-1, keepdims=True)
    i1 = first(fine == v1)
    fine2 = jnp.where(lane == i1, NEG, fine)
    v2 = jnp.max(fine2, axis=-1, keepdims=True)
    i2 = first(fine2 == v2)
    ex = jnp.exp(v2 - v1)
    den = 1.0 / (1.0 + ex)
    gate1 = p_group * den
    gate2 = p_group * (ex * den)
    e1 = i1 - N_GROUPS
    e2 = i2 - N_GROUPS

    oh1 = lane == e1
    oh2 = lane == e2
    cnt = jnp.where(oh1 | oh2, 1.0, 0.0)
    rank = _dot(tri[...], cnt.astype(BF16)) + carry[...]
    r1 = jnp.sum(jnp.where(oh1, rank, 0.0), axis=-1, keepdims=True)
    r2 = jnp.sum(jnp.where(oh2, rank, 0.0), axis=-1, keepdims=True)
    carry[...] = carry[...] + jnp.sum(cnt, axis=0, keepdims=True)
    cnt_ref[...] = jnp.broadcast_to(carry[...], cnt_ref.shape)

    cols_ref[...] = jnp.where(lane == 0, gate1, jnp.where(lane == 1, gate2, 0.0))
    packed = jnp.where(lane == 0, e1.astype(F32),
                       jnp.where(lane == 1, e2.astype(F32),
                                 jnp.where(lane == 2, r1, jnp.where(lane == 3, r2, 0.0))))
    ints_ref[...] = packed.T[0:SUBLANES, :].astype(I32)


def _route(logits):
    t = logits.shape[0]
    tm = ROUTE_TM
    return pl.pallas_call(
        _route_kernel,
        grid=(t // tm,),
        in_specs=[pl.BlockSpec((tm, LANES), lambda i: (i, 0))],
        out_specs=[pl.BlockSpec((tm, LANES), lambda i: (i, 0)),
                   pl.BlockSpec((SUBLANES, tm), lambda i: (0, i)),
                   pl.BlockSpec((SUBLANES, LANES), lambda i: (0, 0))],
        out_shape=[jax.ShapeDtypeStruct((t, LANES), F32),
                   jax.ShapeDtypeStruct((SUBLANES, t), I32),
                   jax.ShapeDtypeStruct((SUBLANES, LANES), F32)],
        scratch_shapes=[pltpu.VMEM((1, LANES), F32), pltpu.VMEM((tm, tm), BF16)],
        compiler_params=_cparams(("arbitrary",)),
        name="route",
    )(logits)


def _plan_kernel(ints_ref, cnt_ref, dest_ref, meta_ref, *, n_blocks_pad):
    sub = lax.broadcasted_iota(I32, (LANES, LANES), 0)
    lane = lax.broadcasted_iota(I32, (LANES, LANES), 1)
    cnt = cnt_ref[0:1, :]
    nblk_row = jnp.floor((cnt + (MOE_BLK - 1.0)) * (1.0 / MOE_BLK))
    nblk_mat = jnp.broadcast_to(nblk_row, (LANES, LANES))
    start_col = jnp.sum(jnp.where(lane < sub, nblk_mat, 0.0), axis=-1, keepdims=True)
    nblk_col = jnp.sum(jnp.where(lane == sub, nblk_mat, 0.0), axis=-1, keepdims=True)
    end_col = start_col + nblk_col

    ints = ints_ref[...]
    base = jnp.zeros(ints.shape, F32)
    for e in range(N_EXPERTS):
        base = jnp.where(ints == e, start_col[e:e + 1, :] * float(MOE_BLK), base)
    dest = base[0:2, :].astype(I32) + ints[2:4, :]
    dest_ref[...] = jnp.concatenate([dest, jnp.zeros((SUBLANES - 2, ints.shape[1]), I32)], axis=0)

    blk = lax.broadcasted_iota(I32, (LANES, n_blocks_pad), 1).astype(F32)
    e_sub = lax.broadcasted_iota(I32, (LANES, n_blocks_pad), 0)
    done = jnp.where((e_sub < N_EXPERTS) & (end_col <= blk), 1.0, 0.0)
    bexp = jnp.minimum(jnp.sum(done, axis=0, keepdims=True), N_EXPERTS - 1.0)
    used = jnp.sum(nblk_row, axis=-1, keepdims=True)
    row = lax.broadcasted_iota(I32, (SUBLANES, n_blocks_pad), 0)
    meta = jnp.where(row == 0, bexp, jnp.where(row == 1, used, 0.0))
    meta_ref[...] = meta.astype(I32)


def _plan(ints, counts, n_blocks_pad):
    t = ints.shape[1]
    return pl.pallas_call(
        functools.partial(_plan_kernel, n_blocks_pad=n_blocks_pad),
        out_shape=[jax.ShapeDtypeStruct((SUBLANES, t), I32), jax.ShapeDtypeStruct((SUBLANES, n_blocks_pad), I32)],
        compiler_params=pltpu.CompilerParams(vmem_limit_bytes=VMEM_LIMIT),
        name="plan",
    )(ints, counts)


DISPATCH_TM = 2048
DMA_UNROLL = 16


def _row_tile(ref, r):
    return ref.at[pl.ds(pl.multiple_of(r * ROW_TILE, ROW_TILE), ROW_TILE)]


def _dispatch_kernel(d0_ref, d1_ref, u2_ref, rows_in_ref, rows_ref, sem):
    del rows_in_ref

    def issue(t, carry):
        src = _row_tile(u2_ref, t)
        pltpu.make_async_copy(src, _row_tile(rows_ref, d0_ref[t]), sem).start(priority=0)
        pltpu.make_async_copy(src, _row_tile(rows_ref, d1_ref[t]), sem).start(priority=1)
        return carry

    lax.fori_loop(0, DISPATCH_TM, issue, 0, unroll=DMA_UNROLL)
    for _ in range(2):
        pltpu.make_async_copy(u2_ref, rows_ref.at[pl.ds(0, DISPATCH_TM * ROW_TILE)], sem).wait()


def _dispatch(dest0, dest1, u2, rows0):
    t = u2.shape[0] // ROW_TILE
    idx = pl.BlockSpec((DISPATCH_TM,), lambda i: (i,), memory_space=pltpu.SMEM)
    return pl.pallas_call(
        _dispatch_kernel,
        grid=(t // DISPATCH_TM,),
        in_specs=[idx, idx,
                  pl.BlockSpec((DISPATCH_TM * ROW_TILE, LANES), lambda i: (i, 0)),
                  pl.BlockSpec(memory_space=pl.ANY)],
        out_specs=pl.BlockSpec(memory_space=pl.ANY),
        out_shape=jax.ShapeDtypeStruct(rows0.shape, F32),
        scratch_shapes=[pltpu.SemaphoreType.DMA(())],
        input_output_aliases={3: 0},
        compiler_params=_cparams(("arbitrary",)),
        name="dispatch",
    )(dest0, dest1, u2, rows0)


def _expert_runs(bexp, used):
    n = bexp.shape[0]
    idx = jnp.arange(n, dtype=I32)
    first = (idx < used[0]) & ((idx == 0) | (bexp != jnp.roll(bexp, 1)))
    slot = (jnp.cumsum(first.astype(I32)) - 1) % 2
    first_at_or_after = lax.cummin(jnp.where(first, idx, n)[::-1])[::-1]
    first_after = jnp.concatenate([first_at_or_after[1:], jnp.full((1,), n, I32)])
    nxt = jnp.where(first_after < n, bexp[jnp.minimum(first_after, n - 1)], -1)
    return first.astype(I32), slot.astype(I32), nxt.astype(I32)


def _experts_kernel(bexp_ref, used_ref, first_ref, slot_ref, nxt_ref, x_ref, w1_hbm, w3_hbm, w2_hbm, y_ref,
                    wb1, wb3, wb2, w1s, w3s, w2s, sem):
    i = pl.program_id(0)
    active = i < used_ref[0]
    slot = slot_ref[i]

    def fetch(e, s):
        return [pltpu.make_async_copy(w.at[e], buf.at[s], sem.at[s])
                for w, buf in ((w1_hbm, wb1), (w3_hbm, wb3), (w2_hbm, wb2))]

    @pl.when(i == 0)
    def _():
        for cp in fetch(bexp_ref[0], 0):
            cp.start()

    @pl.when(first_ref[i] == 1)
    def _():
        @pl.when(nxt_ref[i] >= 0)
        def _():
            for cp in fetch(nxt_ref[i], 1 - slot):
                cp.start()

        for cp in fetch(bexp_ref[i], slot):
            cp.wait()
        w1s[...] = wb1[slot].astype(BF16)
        w3s[...] = wb3[slot].astype(BF16)
        w2s[...] = wb2[slot].astype(BF16)

    @pl.when(active)
    def _():
        def sub_block(r0):
            xb = _load_token_tiles(x_ref, MOE_SUB, r0).astype(BF16)
            yield
            a = _dot(xb, w1s[...])
            b = _dot(xb, w3s[...])
            yield
            hdn = (a * _sigmoid(a) * b).astype(BF16)
            yield
            _store_token_tiles(y_ref, _dot(hdn, w2s[...]), r0)

        _interleave(sub_block(r0) for r0 in range(0, MOE_BLK, MOE_SUB))


def _experts(bexp, used, rows, w1, w3, w2):
    n_blocks = rows.shape[0] // (MOE_BLK * ROW_TILE)
    first, slot, nxt = _expert_runs(bexp, used)
    any_space = pl.BlockSpec(memory_space=pl.ANY)
    row_blk = pl.BlockSpec((MOE_BLK * ROW_TILE, LANES), lambda i, be, nu, *_: (jnp.minimum(i, nu[0] - 1), 0))
    grid_spec = pltpu.PrefetchScalarGridSpec(
        num_scalar_prefetch=5,
        grid=(n_blocks,),
        in_specs=[row_blk, any_space, any_space, any_space],
        out_specs=row_blk,
        scratch_shapes=[pltpu.VMEM((2, D_MODEL, EXPERT_FF), F32), pltpu.VMEM((2, D_MODEL, EXPERT_FF), F32),
                        pltpu.VMEM((2, EXPERT_FF, D_MODEL), F32),
                        pltpu.VMEM((D_MODEL, EXPERT_FF), BF16), pltpu.VMEM((D_MODEL, EXPERT_FF), BF16),
                        pltpu.VMEM((EXPERT_FF, D_MODEL), BF16), pltpu.SemaphoreType.DMA((2,))],
    )
    return pl.pallas_call(
        _experts_kernel,
        grid_spec=grid_spec,
        out_shape=jax.ShapeDtypeStruct(rows.shape, F32),
        input_output_aliases={5: 0},
        compiler_params=_cparams(("arbitrary",)),
        name="experts",
    )(bexp, used, first, slot, nxt, rows, w1, w3, w2)


COMBINE_TM = 256


def _combine_kernel(d0_ref, d1_ref, n0_ref, n1_ref, y_ref, cols_ref, h1_ref, mod_ref, ln_ref, o_ref, ya, yb, sem):
    i = pl.program_id(0)
    slot = i % 2

    def gather(i0_ref, i1_ref, s):
        def issue(t, carry):
            pltpu.make_async_copy(_row_tile(y_ref, i0_ref[t]), _row_tile(ya.at[s], t), sem.at[s]).start(priority=0)
            pltpu.make_async_copy(_row_tile(y_ref, i1_ref[t]), _row_tile(yb.at[s], t), sem.at[s]).start(priority=1)
            return carry

        lax.fori_loop(0, COMBINE_TM, issue, 0, unroll=DMA_UNROLL)

    @pl.when(i == 0)
    def _():
        gather(d0_ref, d1_ref, 0)

    @pl.when(i + 1 < pl.num_programs(0))
    def _():
        gather(n0_ref, n1_ref, 1 - slot)

    for buf in (ya, yb):
        pltpu.make_async_copy(y_ref.at[pl.ds(0, COMBINE_TM * ROW_TILE)], buf.at[slot], sem.at[slot]).wait()

    cols = cols_ref[...]
    moe = (cols[:, 0:1] * _load_token_tiles(ya.at[slot], COMBINE_TM)
           + cols[:, 1:2] * _load_token_tiles(yb.at[slot], COMBINE_TM))
    pre = DEEPNORM_ALPHA * h1_ref[...] + mod_ref[5:6, :] * moe
    o_ref[...] = _ln(pre) * ln_ref[0:1, :] + ln_ref[1:2, :]


def _combine(dest0, dest1, y_rows, cols, h1, mod3, ln2, seq):
    t = h1.shape[0]
    tm = COMBINE_TM
    per_seq = seq // tm
    idx = pl.BlockSpec((tm,), lambda i: (i,), memory_space=pltpu.SMEM)
    idx_next = pl.BlockSpec((tm,), lambda i: (jnp.minimum(i + 1, t // tm - 1),), memory_space=pltpu.SMEM)
    return pl.pallas_call(
        _combine_kernel,
        grid=(t // tm,),
        in_specs=[idx, idx, idx_next, idx_next,
                  pl.BlockSpec(memory_space=pl.ANY),
                  pl.BlockSpec((tm, LANES), lambda i: (i, 0)),
                  pl.BlockSpec((tm, D_MODEL), lambda i: (i, 0)),
                  pl.BlockSpec((None, 6, D_MODEL), lambda i: (i // per_seq, 0, 0)),
                  pl.BlockSpec((2, D_MODEL), lambda i: (0, 0))],
        out_specs=pl.BlockSpec((tm, D_MODEL), lambda i: (i, 0)),
        out_shape=jax.ShapeDtypeStruct((t, D_MODEL), F32),
        scratch_shapes=[pltpu.VMEM((2, tm * ROW_TILE, LANES), F32), pltpu.VMEM((2, tm * ROW_TILE, LANES), F32),
                        pltpu.SemaphoreType.DMA((2,))],
        compiler_params=_cparams(("arbitrary",)),
        name="combine",
    )(dest0, dest1, dest0, dest1, y_rows, cols, h1, mod3, ln2)


def _layer(h, c8, w_ada, b_ada, w_in, w_attn_out, decay_f, decay_b, gn_gain, w_ret_out, w_out,
           ln1_gain, ln1_bias, w_coarse, b_coarse, w_fine, b_fine, w1, w3, w2, ln2_gain, ln2_bias):
    batch, seq, d = h.shape
    t = batch * seq
    x2 = h.reshape(t, d)

    mod = _ada(c8, w_ada, b_ada.reshape(1, -1))
    mod3 = mod[:batch].reshape(batch, 6, d)

    wb = w_in.astype(BF16)
    qkv, u1 = _inproj_attn(x2, mod3, wb, seq)
    rest = _inproj_rest(u1, wb)

    outs, lses = zip(*[_attention(qkv, g, batch, seq) for g in range(ATTN_GROUPS)])
    decays = jnp.concatenate([decay_f, decay_b]).astype(F32)
    retg = _retention(rest, decays, gn_gain.reshape(1, -1), batch, seq)

    w_route = jnp.concatenate([w_coarse, w_fine.transpose(1, 0, 2).reshape(d, N_EXPERTS)], axis=1)
    n_route = w_route.shape[1]
    w_route = jnp.pad(w_route, ((0, 0), (0, LANES - n_route)))
    wr_hi, wr_lo = _split_bf16(w_route)
    rbias = jnp.pad(jnp.concatenate([b_coarse, b_fine.reshape(-1)]), (0, LANES - n_route)).reshape(1, LANES)
    ln1 = jnp.stack([ln1_gain, ln1_bias])
    n_blocks = 2 * t // MOE_BLK + N_EXPERTS
    h1, u2, logits, rows0 = _merge(outs, lses, retg, rest, x2, mod3, w_attn_out.astype(BF16),
                                   w_ret_out.astype(BF16), w_out.astype(BF16), ln1, wr_hi, wr_lo, rbias, seq,
                                   n_blocks * MOE_BLK)

    cols, ints, counts = _route(logits)
    n_blocks_pad = -(-n_blocks // LANES) * LANES
    dest, meta = _plan(ints, counts, n_blocks_pad)
    dest0, dest1 = dest[0], dest[1]
    rows = _dispatch(dest0, dest1, u2, rows0)
    y_rows = _experts(meta[0, :n_blocks], meta[1, :1], rows, w1, w3, w2)
    out = _combine(dest0, dest1, y_rows, cols, h1, mod3, jnp.stack([ln2_gain, ln2_bias]), seq)
    return out.reshape(batch, seq, d)


def kernel(x, c, w_ada, b_ada, w_in, w_attn_out, ret_decay_fwd, ret_decay_bwd, ret_gn_gain, w_ret_out, w_out,
           ln1_gain, ln1_bias, w_coarse, b_coarse, w_fine, b_fine, w1, w3, w2, ln2_gain, ln2_bias):
    batch = x.shape[0]
    max_dil = max(dil for _, dil in ATTN_PATTERNS)
    seq_tile = max(max_dil * ATTN_KB, RET_STEP, REST_TM, DISPATCH_TM)
    assert batch <= SUBLANES and x.shape[1] % seq_tile == 0 and x.shape[2] == D_MODEL
    c8 = jnp.pad(c, ((0, SUBLANES - batch), (0, 0)))
    h = x
    for l in range(w_ada.shape[0]):
        h = _layer(h, c8, w_ada[l], b_ada[l], w_in[l], w_attn_out[l], ret_decay_fwd[l], ret_decay_bwd[l],
                   ret_gn_gain[l], w_ret_out[l], w_out[l], ln1_gain[l], ln1_bias[l], w_coarse[l], b_coarse[l],
                   w_fine[l], b_fine[l], w1[l], w3[l], w2[l], ln2_gain[l], ln2_bias[l])
    return h
```

```python
import functools

import jax
import jax.numpy as jnp
from jax import lax
from jax.experimental import pallas as pl
from jax.experimental.pallas import tpu as pltpu

F32 = jnp.float32
BF16 = jnp.bfloat16
I32 = jnp.int32

D_MODEL = 1024
ATTN_GROUPS = 3
HEADS_PER_GROUP = 4
HEAD_DIM = 128
ATTN_HEADS = ATTN_GROUPS * HEADS_PER_GROUP
GROUP_WIDTH = HEADS_PER_GROUP * HEAD_DIM
ATTN_PATTERNS = ((128, 1), (512, 4), (2048, 16))
ALIBI_MAX_EXP = 8.0
NEG = -1e30
RET_HEADS = 4
RET_QK = 256
RET_V = 512
N_GROUPS = 4
EXPERTS_PER_GROUP = 8
N_EXPERTS = N_GROUPS * EXPERTS_PER_GROUP
EXPERT_FF = 512
DEPTH = 1
DEEPNORM_ALPHA = (2.0 * DEPTH) ** 0.25
LN_EPS = 1e-5

LANES = 128
SUBLANES = 8
PERM_TILE = 512
ATTN_QB = 128
ATTN_HALF = 64
ATTN_KB = ATTN_QB + 2 * ATTN_HALF
RET_CHUNK = 256
RET_STEP = 2048
MOE_BLK = 512
MOE_SUB = 256
VMEM_LIMIT = 56 * 1024 * 1024


def _cparams(sem):
    return pltpu.CompilerParams(dimension_semantics=sem, vmem_limit_bytes=VMEM_LIMIT)


def _split_bf16(a):
    hi = a.astype(BF16)
    lo = (a - hi.astype(F32)).astype(BF16)
    return hi, lo


def _dot(a, b):
    return jnp.dot(a, b, preferred_element_type=F32)


def _dot3(a, b):
    ah, al = _split_bf16(a)
    bh, bl = _split_bf16(b)
    return _dot(ah, bh) + _dot(ah, bl) + _dot(al, bh)


def _ln(x):
    mu = jnp.mean(x, axis=-1, keepdims=True)
    xc = x - mu
    var = jnp.mean(xc * xc, axis=-1, keepdims=True)
    return xc * lax.rsqrt(var + LN_EPS)


def _sigmoid(x):
    return 1.0 / (1.0 + jnp.exp(-x))


ROW_TILE = D_MODEL // LANES


def _store_token_tiles(ref, val, row0=0):
    n = val.shape[0]
    for cc in range(ROW_TILE):
        ref[pl.ds(row0 * ROW_TILE + cc, n, stride=ROW_TILE), :] = val[:, cc * LANES:(cc + 1) * LANES]


def _load_token_tiles(ref, n, row0=0):
    return jnp.concatenate([ref[pl.ds(row0 * ROW_TILE + cc, n, stride=ROW_TILE), :] for cc in range(ROW_TILE)],
                           axis=1)


def _interleave(phased):
    pending = []
    for gen in phased:
        pending.append(gen)
        pending = [g for g in pending if next(g, StopIteration) is not StopIteration]
    while pending:
        pending = [g for g in pending if next(g, StopIteration) is not StopIteration]


def _ada_kernel(c_ref, w_ref, b_ref, o_ref):
    o_ref[...] = _dot3(c_ref[...], w_ref[...]) + b_ref[...]


def _ada(c8, w_ada, b_ada):
    n = w_ada.shape[1]
    return pl.pallas_call(
        _ada_kernel,
        grid=(n // D_MODEL,),
        in_specs=[pl.BlockSpec((SUBLANES, D_MODEL), lambda j: (0, 0)),
                  pl.BlockSpec((D_MODEL, D_MODEL), lambda j: (0, j)),
                  pl.BlockSpec((1, D_MODEL), lambda j: (0, j))],
        out_specs=pl.BlockSpec((SUBLANES, D_MODEL), lambda j: (0, j)),
        out_shape=jax.ShapeDtypeStruct((SUBLANES, n), F32),
        compiler_params=_cparams(("arbitrary",)),
        name="ada",
    )(c8, w_ada, b_ada)


INPROJ_TM = 1024
LN_ROWS = 128
LN_UNROLL = 4


def _modulated_ln_rows(x_ref, mod_ref, emit):
    gain = 1.0 + mod_ref[1:2, :]
    shift = mod_ref[0:1, :]

    def chunk(c, carry):
        r0 = pl.multiple_of(c * LN_ROWS, LN_ROWS)
        emit(r0, _ln(x_ref[pl.ds(r0, LN_ROWS), :]) * gain + shift)
        return carry

    lax.fori_loop(0, INPROJ_TM // LN_ROWS, chunk, 0, unroll=LN_UNROLL)


def _inproj_attn_kernel(x_ref, mod_ref, wq_ref, wk_ref, wv_ref, o_ref, un_ref, uf_scr, u_scr):
    j = pl.program_id(1)

    @pl.when(j == 0)
    def _():
        def emit(r0, u):
            for cc in range(D_MODEL // LANES):
                uf_scr[cc, pl.ds(r0, LN_ROWS), :] = u[:, cc * LANES:(cc + 1) * LANES]
            ub = u.astype(BF16)
            u_scr[0, pl.ds(r0, LN_ROWS), :] = ub
            un_ref[pl.ds(r0, LN_ROWS), :] = ub

        _modulated_ln_rows(x_ref, mod_ref, emit)
        for g in (1, 2):
            dil = ATTN_PATTERNS[g][1]
            n = PERM_TILE // dil
            for t0 in range(0, INPROJ_TM, PERM_TILE):
                for res in range(dil):
                    for cc in range(D_MODEL // LANES):
                        rows = uf_scr[cc, pl.ds(t0 + res, n, stride=dil), :]
                        u_scr[g, t0 + res * n:t0 + (res + 1) * n, cc * LANES:(cc + 1) * LANES] = rows.astype(BF16)

    u = u_scr[j]
    for s, w_ref in enumerate((wq_ref, wk_ref, wv_ref)):
        o_ref[:, s * GROUP_WIDTH:(s + 1) * GROUP_WIDTH] = _dot(u, w_ref[...]).astype(BF16)


ATTN_WIDTH = ATTN_HEADS * HEAD_DIM
W_RQ0 = 3 * ATTN_WIDTH
W_RK0 = W_RQ0 + RET_HEADS * RET_QK
W_RV0 = W_RK0 + RET_HEADS * RET_QK
W_TAIL0 = W_RV0 + RET_HEADS * RET_V


def _inproj_attn(x2, mod3, wb, seq):
    t = x2.shape[0]
    tn = 3 * GROUP_WIDTH
    per_seq = seq // INPROJ_TM
    groups_per_range = ATTN_WIDTH // GROUP_WIDTH
    wspec = lambda s: pl.BlockSpec((D_MODEL, GROUP_WIDTH), lambda i, j: (0, s * groups_per_range + j))
    return pl.pallas_call(
        _inproj_attn_kernel,
        grid=(t // INPROJ_TM, ATTN_GROUPS),
        in_specs=[pl.BlockSpec((INPROJ_TM, D_MODEL), lambda i, j: (i, 0)),
                  pl.BlockSpec((None, 6, D_MODEL), lambda i, j: (i // per_seq, 0, 0)),
                  wspec(0), wspec(1), wspec(2)],
        out_specs=[pl.BlockSpec((INPROJ_TM, tn), lambda i, j: (i, j)),
                   pl.BlockSpec((INPROJ_TM, D_MODEL), lambda i, j: (i, 0))],
        out_shape=[jax.ShapeDtypeStruct((t, ATTN_GROUPS * tn), BF16),
                   jax.ShapeDtypeStruct((t, D_MODEL), BF16)],
        scratch_shapes=[pltpu.VMEM((D_MODEL // LANES, INPROJ_TM, LANES), F32),
                        pltpu.VMEM((ATTN_GROUPS, INPROJ_TM, D_MODEL), BF16)],
        compiler_params=_cparams(("arbitrary", "arbitrary")),
        name="inproj_attn",
    )(x2, mod3, wb, wb, wb)


REST_TN = RET_HEAD_COLS = 2 * RET_QK + RET_V
REST_TAIL_TILES = (RET_HEADS * RET_V + 2 * D_MODEL) // REST_TN
REST_TM = 2048


def _inproj_rest_kernel(u_ref, wq_ref, wk_ref, wv_ref, wt0_ref, wt1_ref, o_ref):
    j = pl.program_id(0)

    def project(parts):
        u = u_ref[...]
        c0 = 0
        for w_ref in parts:
            wd = w_ref.shape[1]
            o_ref[:, c0:c0 + wd] = _dot(u, w_ref[...]).astype(BF16)
            c0 += wd

    @pl.when(j < RET_HEADS)
    def _():
        project((wq_ref, wk_ref, wv_ref))

    @pl.when(j >= RET_HEADS)
    def _():
        project((wt0_ref, wt1_ref))


def _inproj_rest(u, wb):
    t = u.shape[0]
    half = REST_TN // 2
    assert W_TAIL0 % half == 0 and W_RV0 % RET_V == 0
    head = lambda j: jnp.minimum(j, RET_HEADS - 1)
    tail = lambda j: jnp.maximum(j - RET_HEADS, 0)
    return pl.pallas_call(
        _inproj_rest_kernel,
        grid=(RET_HEADS + REST_TAIL_TILES, t // REST_TM),
        in_specs=[pl.BlockSpec((REST_TM, D_MODEL), lambda j, i: (i, 0)),
                  pl.BlockSpec((D_MODEL, RET_QK), lambda j, i: (0, W_RQ0 // RET_QK + head(j))),
                  pl.BlockSpec((D_MODEL, RET_QK), lambda j, i: (0, W_RK0 // RET_QK + head(j))),
                  pl.BlockSpec((D_MODEL, RET_V), lambda j, i: (0, W_RV0 // RET_V + head(j))),
                  pl.BlockSpec((D_MODEL, half), lambda j, i: (0, W_TAIL0 // half + 2 * tail(j))),
                  pl.BlockSpec((D_MODEL, half), lambda j, i: (0, W_TAIL0 // half + 2 * tail(j) + 1))],
        out_specs=pl.BlockSpec((REST_TM, REST_TN), lambda j, i: (i, j)),
        out_shape=jax.ShapeDtypeStruct((t, (RET_HEADS + REST_TAIL_TILES) * REST_TN), BF16),
        compiler_params=_cparams(("arbitrary", "arbitrary")),
        name="inproj_rest",
    )(u, wb, wb, wb, wb, wb)


ATTN_OFFSETS = ATTN_KB // ATTN_HALF - 1
ATTN_UNROLL = 16
ATTN_MIN_BLOCKS = 4


def _attn_kernel(q_ref, k_ref, v_ref, o_ref, lse_ref, bias_scr, *scratch, group, dil, nt, n, nres):
    n_sub = nt * n
    if nt == 1:
        seqs = [(q_ref.at[0, rr], k_ref.at[0, rr], v_ref.at[0, rr], o_ref.at[0, rr], lse_ref.at[0, rr])
                for rr in range(nres)]
    else:
        seqs = [tuple(s.at[rr] for s in scratch) for rr in range(nres)]
        for rr, (qs, ks, vs, _, _) in enumerate(seqs):
            for t in range(nt):
                qs[t * n:(t + 1) * n, :] = q_ref[t, rr]
                ks[t * n:(t + 1) * n, :] = k_ref[t, rr]
                vs[t * n:(t + 1) * n, :] = v_ref[t, rr]

    @pl.when((pl.program_id(0) == 0) & (pl.program_id(1) == 0))
    def _():
        base = (lax.broadcasted_iota(I32, (ATTN_QB, ATTN_KB), 1)
                - lax.broadcasted_iota(I32, (ATTN_QB, ATTN_KB), 0))
        for j in range(ATTN_OFFSETS):
            dist = jnp.abs(base - j * ATTN_HALF)
            for hh in range(HEADS_PER_GROUP):
                slope = dil * 2.0 ** (-ALIBI_MAX_EXP * (group * HEADS_PER_GROUP + hh + 1) / ATTN_HEADS)
                bias_scr[hh * ATTN_OFFSETS + j] = jnp.where(dist <= ATTN_HALF, -slope * dist.astype(F32), NEG)

    scale = HEAD_DIM ** -0.5
    lane = lax.broadcasted_iota(I32, (ATTN_QB, LANES), 1)

    def block(blk, carry):
        q0 = pl.multiple_of(blk * ATTN_QB, ATTN_QB)
        start = pl.multiple_of(jnp.clip(q0 - ATTN_HALF, 0, n_sub - ATTN_KB), ATTN_HALF)
        j = (q0 - start) // ATTN_HALF
        for qs, ks, vs, os_, ls in seqs:
            lse_tile = jnp.zeros((ATTN_QB, LANES), F32)
            for hh in range(HEADS_PER_GROUP):
                cs = slice(hh * HEAD_DIM, (hh + 1) * HEAD_DIM)
                qb = qs[pl.ds(q0, ATTN_QB), cs]
                kb = ks[pl.ds(start, ATTN_KB), cs]
                vb = vs[pl.ds(start, ATTN_KB), cs]
                s = lax.dot_general(qb, kb, (((1,), (1,)), ((), ())), preferred_element_type=F32) * scale
                s = s + bias_scr[hh * ATTN_OFFSETS + j]
                m = jnp.max(s, axis=-1, keepdims=True)
                p = jnp.exp(s - m)
                l = jnp.sum(p, axis=-1, keepdims=True)
                o = _dot(p.astype(BF16), vb) * (1.0 / l)
                os_[pl.ds(q0, ATTN_QB), cs] = o.astype(BF16)
                lse_tile = jnp.where(lane == hh, m + jnp.log(l), lse_tile)
            ls[pl.ds(q0, ATTN_QB), :] = lse_tile
        return carry

    n_blk = n_sub // ATTN_QB
    lax.fori_loop(0, n_blk, block, 0, unroll=min(max(ATTN_UNROLL // nres, 1), n_blk))

    if nt > 1:
        for rr, (_, _, _, os_, ls) in enumerate(seqs):
            for t in range(nt):
                o_ref[t, rr] = os_[t * n:(t + 1) * n, :]
                lse_ref[t, rr] = ls[t * n:(t + 1) * n, :]


def _attention(qkv, group, batch, seq):
    dil = ATTN_PATTERNS[group][1]
    if dil == 1:
        nt, n = 1, seq
    else:
        nt, n = seq // PERM_TILE, PERM_TILE // dil
    n_sub = nt * n
    t = batch * seq
    qkv5 = qkv.reshape(batch, nt, dil, n, qkv.shape[1])
    cb = group * 3
    nres = min(dil, max(1, ATTN_MIN_BLOCKS * ATTN_QB // n_sub))
    blk = (None, nt, nres, n, GROUP_WIDTH)
    scratch = [pltpu.VMEM((HEADS_PER_GROUP * ATTN_OFFSETS, ATTN_QB, ATTN_KB), F32)]
    if nt > 1:
        scratch += [pltpu.VMEM((nres, n_sub, GROUP_WIDTH), BF16)] * 4 + [pltpu.VMEM((nres, n_sub, LANES), F32)]
    out, lse = pl.pallas_call(
        functools.partial(_attn_kernel, group=group, dil=dil, nt=nt, n=n, nres=nres),
        grid=(batch, dil // nres),
        in_specs=[pl.BlockSpec(blk, lambda b, r: (b, 0, r, 0, cb)),
                  pl.BlockSpec(blk, lambda b, r: (b, 0, r, 0, cb + 1)),
                  pl.BlockSpec(blk, lambda b, r: (b, 0, r, 0, cb + 2))],
        out_specs=[pl.BlockSpec(blk, lambda b, r: (b, 0, r, 0, 0)),
                   pl.BlockSpec((None, nt, nres, n, LANES), lambda b, r: (b, 0, r, 0, 0))],
        out_shape=[jax.ShapeDtypeStruct((batch, nt, dil, n, GROUP_WIDTH), BF16),
                   jax.ShapeDtypeStruct((batch, nt, dil, n, LANES), F32)],
        scratch_shapes=scratch,
        compiler_params=_cparams(("arbitrary", "arbitrary")),
        name=f"attn_g{group}",
    )(qkv5, qkv5, qkv5)
    return out.reshape(t, GROUP_WIDTH), lse.reshape(t, LANES)


def _log_sigmoid(x):
    return jnp.minimum(x, 0.0) - jnp.log(1.0 + jnp.exp(-jnp.abs(x)))


def _retention_kernel(decay_ref, qkv_ref, g_ref, gain_ref, o_ref, state, ybwd, dmat, kdec, qdec, *, nc):
    c = RET_CHUNK
    h = pl.program_id(1)
    i = pl.program_id(2)
    kscale = RET_QK ** -0.5

    def set_decays(lg, forward):
        row = lax.broadcasted_iota(I32, (c, c), 0)
        col = lax.broadcasted_iota(I32, (c, c), 1)
        pos = lax.broadcasted_iota(I32, (c, LANES), 0).astype(F32)
        if forward:
            gap, key_pow, query_pow = row - col, (c - 1.0) - pos, pos + 1.0
            keep = gap >= 0
        else:
            gap, key_pow, query_pow = col - row, pos, c - pos
            keep = gap > 0
        dmat[...] = jnp.where(keep, jnp.exp(lg * jnp.maximum(gap, 0).astype(F32)) * kscale, 0.0)
        kdec[...] = jnp.exp(lg * key_pow) * kscale
        qdec[...] = jnp.exp(lg * query_pow)

    def chunk_step(lg, r0, finish):
        q = qkv_ref[r0:r0 + c, 0:RET_QK]
        k = qkv_ref[r0:r0 + c, RET_QK:2 * RET_QK]
        v = qkv_ref[r0:r0 + c, 2 * RET_QK:]
        inner = lax.dot_general(q, k, (((1,), (1,)), ((), ())), preferred_element_type=F32) * dmat[...]
        y = _dot(inner.astype(BF16), v)
        kd = (k.astype(F32) * jnp.concatenate([kdec[...]] * (RET_QK // LANES), axis=1)).astype(BF16)
        kv = lax.dot_general(kd, v, (((0,), (0,)), ((), ())), preferred_element_type=F32)
        yield
        qd = jnp.concatenate([qdec[...]] * (RET_V // LANES), axis=1)
        y = y + _dot(q, state[...].astype(BF16)) * qd
        state[...] = state[...] * jnp.exp(lg * float(c)) + kv
        yield
        finish(r0, y)

    @pl.when((i == 0) | (i == nc))
    def _():
        state[...] = jnp.zeros_like(state)

    @pl.when(i < nc)
    def _():
        lg = _log_sigmoid(jnp.zeros((1, 1), F32) + decay_ref[RET_HEADS + h])

        @pl.when(i == 0)
        def _():
            set_decays(lg, False)

        base = pl.multiple_of((nc - 1 - i) * RET_STEP, RET_STEP)

        def finish(r0, y):
            ybwd[pl.ds(base + r0, c), :] = y

        _interleave(chunk_step(lg, r0, finish) for r0 in reversed(range(0, RET_STEP, c)))

    @pl.when(i >= nc)
    def _():
        lg = _log_sigmoid(jnp.zeros((1, 1), F32) + decay_ref[h])

        @pl.when(i == nc)
        def _():
            set_decays(lg, True)

        base = pl.multiple_of((i - nc) * RET_STEP, RET_STEP)

        def finish(r0, y):
            y = y + ybwd[pl.ds(base + r0, c), :]
            g = g_ref[r0:r0 + c, :].astype(F32)
            o_ref[r0:r0 + c, :] = (g * _sigmoid(g) * (_ln(y) * gain_ref[...])).astype(BF16)

        _interleave(chunk_step(lg, r0, finish) for r0 in range(0, RET_STEP, c))


def _retention(rest, decays, gn_gain, batch, seq):
    c = RET_STEP
    nc = seq // c
    rest3 = rest.reshape(batch, seq, rest.shape[1])
    gate_blk = RET_HEADS * RET_HEAD_COLS // RET_V

    def chunk(i):
        return jnp.where(i < nc, nc - 1 - i, i - nc)

    grid_spec = pltpu.PrefetchScalarGridSpec(
        num_scalar_prefetch=1,
        grid=(batch, RET_HEADS, 2 * nc),
        in_specs=[pl.BlockSpec((None, c, RET_HEAD_COLS), lambda b, h, i, d: (b, chunk(i), h)),
                  pl.BlockSpec((None, c, RET_V), lambda b, h, i, d: (b, jnp.maximum(i - nc, 0), gate_blk + h)),
                  pl.BlockSpec((1, RET_V), lambda b, h, i, d: (0, h))],
        out_specs=pl.BlockSpec((None, c, RET_V), lambda b, h, i, d: (b, jnp.maximum(i - nc, 0), h)),
        scratch_shapes=[pltpu.VMEM((RET_QK, RET_V), F32), pltpu.VMEM((seq, RET_V), F32),
                        pltpu.VMEM((RET_CHUNK, RET_CHUNK), F32), pltpu.VMEM((RET_CHUNK, LANES), F32),
                        pltpu.VMEM((RET_CHUNK, LANES), F32)],
    )
    out = pl.pallas_call(
        functools.partial(_retention_kernel, nc=nc),
        grid_spec=grid_spec,
        out_shape=jax.ShapeDtypeStruct((batch, seq, RET_HEADS * RET_V), BF16),
        compiler_params=_cparams(("arbitrary", "arbitrary", "arbitrary")),
        name="retention",
    )(decays, rest3, rest3, gn_gain)
    return out.reshape(batch * seq, RET_HEADS * RET_V)


MERGE_TM = PERM_TILE
MERGE_SUB = 256
ZERO_SPLIT = 4


def _merge_kernel(o0_ref, o1_ref, o2_ref, l0_ref, l1_ref, l2_ref, retg_ref, ga_ref, gr_ref, x_ref, mod_ref,
                  wa_ref, wr_ref, wo_ref, ln_ref, wrh_ref, wrl_ref, rb_ref,
                  h1_ref, u2_ref, lg_ref, zrows_ref, on_scr, ln_scr, zbuf, zsem):
    i = pl.program_id(0)

    @pl.when(i == 0)
    def _():
        zbuf[...] = jnp.zeros_like(zbuf)

    zrows = zbuf.shape[0]
    zero_copies = [pltpu.make_async_copy(zbuf, zrows_ref.at[pl.ds((i * ZERO_SPLIT + k) * zrows, zrows)], zsem)
                   for k in range(ZERO_SPLIT)]
    for cp in zero_copies:
        cp.start()

    for g in (1, 2):
        dil = ATTN_PATTERNS[g][1]
        n = MERGE_TM // dil
        o_ref, l_ref = ((o1_ref, l1_ref), (o2_ref, l2_ref))[g - 1]
        for res in range(dil):
            rows = o_ref[res * n:(res + 1) * n, :].astype(F32)
            for hh in range(HEADS_PER_GROUP):
                on_scr[g - 1, hh, pl.ds(res, n, stride=dil), :] = rows[:, hh * HEAD_DIM:(hh + 1) * HEAD_DIM]
            ln_scr[g - 1, pl.ds(res, n, stride=dil), :] = l_ref[res * n:(res + 1) * n, :]

    def sub_tile(r0):
        rs = slice(r0, r0 + MERGE_SUB)
        l0, l1, l2 = l0_ref[rs, :], ln_scr[0, rs, :], ln_scr[1, rs, :]
        lm = jnp.maximum(jnp.maximum(l0, l1), l2)
        e0, e1, e2 = jnp.exp(l0 - lm), jnp.exp(l1 - lm), jnp.exp(l2 - lm)
        inv = 1.0 / (e0 + e1 + e2)
        parts = []
        for hh in range(HEADS_PER_GROUP):
            sl = slice(hh * HEAD_DIM, (hh + 1) * HEAD_DIM)
            acc = (e0[:, hh:hh + 1] * o0_ref[rs, sl].astype(F32)
                   + e1[:, hh:hh + 1] * on_scr[0, hh, rs, :]
                   + e2[:, hh:hh + 1] * on_scr[1, hh, rs, :])
            parts.append((acc * inv[:, hh:hh + 1]).astype(BF16))
        attn = jnp.concatenate(parts, axis=1)
        yield

        branch_a = _dot(attn, wa_ref[...])
        branch_r = _dot(retg_ref[rs, :], wr_ref[...])
        yield
        merged = (_sigmoid(ga_ref[rs, :].astype(F32)) * branch_a
                  + _sigmoid(gr_ref[rs, :].astype(F32)) * branch_r)
        yield
        y = _dot(merged.astype(BF16), wo_ref[...])
        yield

        h1 = _ln(DEEPNORM_ALPHA * x_ref[rs, :] + mod_ref[2:3, :] * y) * ln_ref[0:1, :] + ln_ref[1:2, :]
        h1_ref[rs, :] = h1
        u2 = _ln(h1) * (1.0 + mod_ref[4:5, :]) + mod_ref[3:4, :]
        _store_token_tiles(u2_ref, u2, r0)
        uh, ul = _split_bf16(u2)
        yield
        lg_ref[rs, :] = (_dot(uh, wrh_ref[...]) + _dot(uh, wrl_ref[...]) + _dot(ul, wrh_ref[...])
                         + rb_ref[...])

    _interleave(sub_tile(r0) for r0 in range(0, MERGE_TM, MERGE_SUB))

    for cp in zero_copies:
        cp.wait()


def _merge(outs, lses, retg, rest, x2, mod3, wa, wr, wo, ln1, wr_hi, wr_lo, rbias, seq, n_rows):
    t = x2.shape[0]
    tm = MERGE_TM
    per_seq = seq // tm
    zrows, rem = divmod(n_rows * ROW_TILE, (t // tm) * ZERO_SPLIT)
    assert rem == 0 and zrows % SUBLANES == 0
    row = lambda w: pl.BlockSpec((tm, w), lambda i: (i, 0))
    full = lambda a: pl.BlockSpec(a.shape, lambda i: (0,) * a.ndim)
    return pl.pallas_call(
        _merge_kernel,
        grid=(t // tm,),
        in_specs=[row(GROUP_WIDTH)] * 3 + [row(LANES)] * 3 + [
            row(RET_HEADS * RET_V),
            pl.BlockSpec((tm, D_MODEL), lambda i: (i, 6)),
            pl.BlockSpec((tm, D_MODEL), lambda i: (i, 7)),
            row(D_MODEL),
            pl.BlockSpec((None, 6, D_MODEL), lambda i: (i // per_seq, 0, 0)),
            full(wa), full(wr), full(wo), full(ln1), full(wr_hi), full(wr_lo), full(rbias)],
        out_specs=[row(D_MODEL), pl.BlockSpec((tm * ROW_TILE, LANES), lambda i: (i, 0)), row(LANES),
                   pl.BlockSpec(memory_space=pl.ANY)],
        out_shape=[jax.ShapeDtypeStruct((t, D_MODEL), F32),
                   jax.ShapeDtypeStruct((t * ROW_TILE, LANES), F32),
                   jax.ShapeDtypeStruct((t, LANES), F32),
                   jax.ShapeDtypeStruct((n_rows * ROW_TILE, LANES), F32)],
        scratch_shapes=[pltpu.VMEM((2, HEADS_PER_GROUP, tm, HEAD_DIM), F32), pltpu.VMEM((2, tm, LANES), F32),
                        pltpu.VMEM((zrows, LANES), F32), pltpu.SemaphoreType.DMA(())],
        compiler_params=_cparams(("arbitrary",)),
        name="merge",
    )(*outs, *lses, retg, rest, rest, x2, mod3, wa, wr, wo, ln1, wr_hi, wr_lo, rbias)


ROUTE_TM = 512
BIG = 1 << 20


def _route_kernel(lg_ref, cols_ref, ints_ref, cnt_ref, carry, tri):
    i = pl.program_id(0)
    tm = ROUTE_TM

    @pl.when(i == 0)
    def _():
        carry[...] = jnp.zeros_like(carry)
        r_i = lax.broadcasted_iota(I32, (tm, tm), 0)
        c_i = lax.broadcasted_iota(I32, (tm, tm), 1)
        tri[...] = jnp.where(r_i > c_i, 1.0, 0.0).astype(BF16)

    lg = lg_ref[...]
    lane = lax.broadcasted_iota(I32, (tm, LANES), 1)
    lane_f = lane.astype(F32)
    first = lambda mask: jnp.min(jnp.where(mask, lane_f, float(BIG)), axis=-1, keepdims=True).astype(I32)

    coarse = jnp.where(lane < N_GROUPS, lg, NEG)
    cmax = jnp.max(coarse, axis=-1, keepdims=True)
    gsel = first(coarse == cmax)
    p_group = 1.0 / jnp.sum(jnp.exp(coarse - cmax), axis=-1, keepdims=True)

    lo = N_GROUPS + EXPERTS_PER_GROUP * gsel
    fine = jnp.where((lane >= lo) & (lane < lo + EXPERTS_PER_GROUP), lg, NEG)
    v1 = jnp.max(fine, axis=-1, keepdims=True)
    i1 = first(fine == v1)
    fine2 = jnp.where(lane == i1, NEG, fine)
    v2 = jnp.max(fine2, axis=-1, keepdims=True)
    i2 = first(fine2 == v2)
    ex = jnp.exp(v2 - v1)
    den = 1.0 / (1.0 + ex)
    gate1 = p_group * den
    gate2 = p_group * (ex * den)
    e1 = i1 - N_GROUPS
    e2 = i2 - N_GROUPS

    oh1 = lane == e1
    oh2 = lane == e2
    cnt = jnp.where(oh1 | oh2, 1.0, 0.0)
    rank = _dot(tri[...], cnt.astype(BF16)) + carry[...]
    r1 = jnp.sum(jnp.where(oh1, rank, 0.0), axis=-1, keepdims=True)
    r2 = jnp.sum(jnp.where(oh2, rank, 0.0), axis=-1, keepdims=True)
    carry[...] = carry[...] + jnp.sum(cnt, axis=0, keepdims=True)
    cnt_ref[...] = jnp.broadcast_to(carry[...], cnt_ref.shape)

    cols_ref[...] = jnp.where(lane == 0, gate1, jnp.where(lane == 1, gate2, 0.0))
    packed = jnp.where(lane == 0, e1.astype(F32),
                       jnp.where(lane == 1, e2.astype(F32),
                                 jnp.where(lane == 2, r1, jnp.where(lane == 3, r2, 0.0))))
    ints_ref[...] = packed.T[0:SUBLANES, :].astype(I32)


def _route(logits):
    t = logits.shape[0]
    tm = ROUTE_TM
    return pl.pallas_call(
        _route_kernel,
        grid=(t // tm,),
        in_specs=[pl.BlockSpec((tm, LANES), lambda i: (i, 0))],
        out_specs=[pl.BlockSpec((tm, LANES), lambda i: (i, 0)),
                   pl.BlockSpec((SUBLANES, tm), lambda i: (0, i)),
                   pl.BlockSpec((SUBLANES, LANES), lambda i: (0, 0))],
        out_shape=[jax.ShapeDtypeStruct((t, LANES), F32),
                   jax.ShapeDtypeStruct((SUBLANES, t), I32),
                   jax.ShapeDtypeStruct((SUBLANES, LANES), F32)],
        scratch_shapes=[pltpu.VMEM((1, LANES), F32), pltpu.VMEM((tm, tm), BF16)],
        compiler_params=_cparams(("arbitrary",)),
        name="route",
    )(logits)


def _plan_kernel(ints_ref, cnt_ref, dest_ref, meta_ref, *, n_blocks_pad):
    sub = lax.broadcasted_iota(I32, (LANES, LANES), 0)
    lane = lax.broadcasted_iota(I32, (LANES, LANES), 1)
    cnt = cnt_ref[0:1, :]
    nblk_row = jnp.floor((cnt + (MOE_BLK - 1.0)) * (1.0 / MOE_BLK))
    nblk_mat = jnp.broadcast_to(nblk_row, (LANES, LANES))
    start_col = jnp.sum(jnp.where(lane < sub, nblk_mat, 0.0), axis=-1, keepdims=True)
    nblk_col = jnp.sum(jnp.where(lane == sub, nblk_mat, 0.0), axis=-1, keepdims=True)
    end_col = start_col + nblk_col

    ints = ints_ref[...]
    base = jnp.zeros(ints.shape, F32)
    for e in range(N_EXPERTS):
        base = jnp.where(ints == e, start_col[e:e + 1, :] * float(MOE_BLK), base)
    dest = base[0:2, :].astype(I32) + ints[2:4, :]
    dest_ref[...] = jnp.concatenate([dest, jnp.zeros((SUBLANES - 2, ints.shape[1]), I32)], axis=0)

    blk = lax.broadcasted_iota(I32, (LANES, n_blocks_pad), 1).astype(F32)
    e_sub = lax.broadcasted_iota(I32, (LANES, n_blocks_pad), 0)
    done = jnp.where((e_sub < N_EXPERTS) & (end_col <= blk), 1.0, 0.0)
    bexp = jnp.minimum(jnp.sum(done, axis=0, keepdims=True), N_EXPERTS - 1.0)
    used = jnp.sum(nblk_row, axis=-1, keepdims=True)
    row = lax.broadcasted_iota(I32, (SUBLANES, n_blocks_pad), 0)
    meta = jnp.where(row == 0, bexp, jnp.where(row == 1, used, 0.0))
    meta_ref[...] = meta.astype(I32)


def _plan(ints, counts, n_blocks_pad):
    t = ints.shape[1]
    return pl.pallas_call(
        functools.partial(_plan_kernel, n_blocks_pad=n_blocks_pad),
        out_shape=[jax.ShapeDtypeStruct((SUBLANES, t), I32), jax.ShapeDtypeStruct((SUBLANES, n_blocks_pad), I32)],
        compiler_params=pltpu.CompilerParams(vmem_limit_bytes=VMEM_LIMIT),
        name="plan",
    )(ints, counts)


DISPATCH_TM = 2048
DMA_UNROLL = 16


def _row_tile(ref, r):
    return ref.at[pl.ds(pl.multiple_of(r * ROW_TILE, ROW_TILE), ROW_TILE)]


def _dispatch_kernel(d0_ref, d1_ref, u2_ref, rows_in_ref, rows_ref, sem):
    del rows_in_ref

    def issue(t, carry):
        src = _row_tile(u2_ref, t)
        pltpu.make_async_copy(src, _row_tile(rows_ref, d0_ref[t]), sem).start(priority=0)
        pltpu.make_async_copy(src, _row_tile(rows_ref, d1_ref[t]), sem).start(priority=1)
        return carry

    lax.fori_loop(0, DISPATCH_TM, issue, 0, unroll=DMA_UNROLL)
    for _ in range(2):
        pltpu.make_async_copy(u2_ref, rows_ref.at[pl.ds(0, DISPATCH_TM * ROW_TILE)], sem).wait()


def _dispatch(dest0, dest1, u2, rows0):
    t = u2.shape[0] // ROW_TILE
    idx = pl.BlockSpec((DISPATCH_TM,), lambda i: (i,), memory_space=pltpu.SMEM)
    return pl.pallas_call(
        _dispatch_kernel,
        grid=(t // DISPATCH_TM,),
        in_specs=[idx, idx,
                  pl.BlockSpec((DISPATCH_TM * ROW_TILE, LANES), lambda i: (i, 0)),
                  pl.BlockSpec(memory_space=pl.ANY)],
        out_specs=pl.BlockSpec(memory_space=pl.ANY),
        out_shape=jax.ShapeDtypeStruct(rows0.shape, F32),
        scratch_shapes=[pltpu.SemaphoreType.DMA(())],
        input_output_aliases={3: 0},
        compiler_params=_cparams(("arbitrary",)),
        name="dispatch",
    )(dest0, dest1, u2, rows0)


def _expert_runs(bexp, used):
    n = bexp.shape[0]
    idx = jnp.arange(n, dtype=I32)
    first = (idx < used[0]) & ((idx == 0) | (bexp != jnp.roll(bexp, 1)))
    slot = (jnp.cumsum(first.astype(I32)) - 1) % 2
    first_at_or_after = lax.cummin(jnp.where(first, idx, n)[::-1])[::-1]
    first_after = jnp.concatenate([first_at_or_after[1:], jnp.full((1,), n, I32)])
    nxt = jnp.where(first_after < n, bexp[jnp.minimum(first_after, n - 1)], -1)
    return first.astype(I32), slot.astype(I32), nxt.astype(I32)


def _experts_kernel(bexp_ref, used_ref, first_ref, slot_ref, nxt_ref, x_ref, w1_hbm, w3_hbm, w2_hbm, y_ref,
                    wb1, wb3, wb2, w1s, w3s, w2s, sem):
    i = pl.program_id(0)
    active = i < used_ref[0]
    slot = slot_ref[i]

    def fetch(e, s):
        return [pltpu.make_async_copy(w.at[e], buf.at[s], sem.at[s])
                for w, buf in ((w1_hbm, wb1), (w3_hbm, wb3), (w2_hbm, wb2))]

    @pl.when(i == 0)
    def _():
        for cp in fetch(bexp_ref[0], 0):
            cp.start()

    @pl.when(first_ref[i] == 1)
    def _():
        @pl.when(nxt_ref[i] >= 0)
        def _():
            for cp in fetch(nxt_ref[i], 1 - slot):
                cp.start()

        for cp in fetch(bexp_ref[i], slot):
            cp.wait()
        w1s[...] = wb1[slot].astype(BF16)
        w3s[...] = wb3[slot].astype(BF16)
        w2s[...] = wb2[slot].astype(BF16)

    @pl.when(active)
    def _():
        def sub_block(r0):
            xb = _load_token_tiles(x_ref, MOE_SUB, r0).astype(BF16)
            yield
            a = _dot(xb, w1s[...])
            b = _dot(xb, w3s[...])
            yield
            hdn = (a * _sigmoid(a) * b).astype(BF16)
            yield
            _store_token_tiles(y_ref, _dot(hdn, w2s[...]), r0)

        _interleave(sub_block(r0) for r0 in range(0, MOE_BLK, MOE_SUB))


def _experts(bexp, used, rows, w1, w3, w2):
    n_blocks = rows.shape[0] // (MOE_BLK * ROW_TILE)
    first, slot, nxt = _expert_runs(bexp, used)
    any_space = pl.BlockSpec(memory_space=pl.ANY)
    row_blk = pl.BlockSpec((MOE_BLK * ROW_TILE, LANES), lambda i, be, nu, *_: (jnp.minimum(i, nu[0] - 1), 0))
    grid_spec = pltpu.PrefetchScalarGridSpec(
        num_scalar_prefetch=5,
        grid=(n_blocks,),
        in_specs=[row_blk, any_space, any_space, any_space],
        out_specs=row_blk,
        scratch_shapes=[pltpu.VMEM((2, D_MODEL, EXPERT_FF), F32), pltpu.VMEM((2, D_MODEL, EXPERT_FF), F32),
                        pltpu.VMEM((2, EXPERT_FF, D_MODEL), F32),
                        pltpu.VMEM((D_MODEL, EXPERT_FF), BF16), pltpu.VMEM((D_MODEL, EXPERT_FF), BF16),
                        pltpu.VMEM((EXPERT_FF, D_MODEL), BF16), pltpu.SemaphoreType.DMA((2,))],
    )
    return pl.pallas_call(
        _experts_kernel,
        grid_spec=grid_spec,
        out_shape=jax.ShapeDtypeStruct(rows.shape, F32),
        input_output_aliases={5: 0},
        compiler_params=_cparams(("arbitrary",)),
        name="experts",
    )(bexp, used, first, slot, nxt, rows, w1, w3, w2)


COMBINE_TM = 256


def _combine_kernel(d0_ref, d1_ref, n0_ref, n1_ref, y_ref, cols_ref, h1_ref, mod_ref, ln_ref, o_ref, ya, yb, sem):
    i = pl.program_id(0)
    slot = i % 2

    def gather(i0_ref, i1_ref, s):
        def issue(t, carry):
            pltpu.make_async_copy(_row_tile(y_ref, i0_ref[t]), _row_tile(ya.at[s], t), sem.at[s]).start(priority=0)
            pltpu.make_async_copy(_row_tile(y_ref, i1_ref[t]), _row_tile(yb.at[s], t), sem.at[s]).start(priority=1)
            return carry

        lax.fori_loop(0, COMBINE_TM, issue, 0, unroll=DMA_UNROLL)

    @pl.when(i == 0)
    def _():
        gather(d0_ref, d1_ref, 0)

    @pl.when(i + 1 < pl.num_programs(0))
    def _():
        gather(n0_ref, n1_ref, 1 - slot)

    for buf in (ya, yb):
        pltpu.make_async_copy(y_ref.at[pl.ds(0, COMBINE_TM * ROW_TILE)], buf.at[slot], sem.at[slot]).wait()

    cols = cols_ref[...]
    moe = (cols[:, 0:1] * _load_token_tiles(ya.at[slot], COMBINE_TM)
           + cols[:, 1:2] * _load_token_tiles(yb.at[slot], COMBINE_TM))
    pre = DEEPNORM_ALPHA * h1_ref[...] + mod_ref[5:6, :] * moe
    o_ref[...] = _ln(pre) * ln_ref[0:1, :] + ln_ref[1:2, :]


def _combine(dest0, dest1, y_rows, cols, h1, mod3, ln2, seq):
    t = h1.shape[0]
    tm = COMBINE_TM
    per_seq = seq // tm
    idx = pl.BlockSpec((tm,), lambda i: (i,), memory_space=pltpu.SMEM)
    idx_next = pl.BlockSpec((tm,), lambda i: (jnp.minimum(i + 1, t // tm - 1),), memory_space=pltpu.SMEM)
    return pl.pallas_call(
        _combine_kernel,
        grid=(t // tm,),
        in_specs=[idx, idx, idx_next, idx_next,
                  pl.BlockSpec(memory_space=pl.ANY),
                  pl.BlockSpec((tm, LANES), lambda i: (i, 0)),
                  pl.BlockSpec((tm, D_MODEL), lambda i: (i, 0)),
                  pl.BlockSpec((None, 6, D_MODEL), lambda i: (i // per_seq, 0, 0)),
                  pl.BlockSpec((2, D_MODEL), lambda i: (0, 0))],
        out_specs=pl.BlockSpec((tm, D_MODEL), lambda i: (i, 0)),
        out_shape=jax.ShapeDtypeStruct((t, D_MODEL), F32),
        scratch_shapes=[pltpu.VMEM((2, tm * ROW_TILE, LANES), F32), pltpu.VMEM((2, tm * ROW_TILE, LANES), F32),
                        pltpu.SemaphoreType.DMA((2,))],
        compiler_params=_cparams(("arbitrary",)),
        name="combine",
    )(dest0, dest1, dest0, dest1, y_rows, cols, h1, mod3, ln2)


def _layer(h, c8, w_ada, b_ada, w_in, w_attn_out, decay_f, decay_b, gn_gain, w_ret_out, w_out,
           ln1_gain, ln1_bias, w_coarse, b_coarse, w_fine, b_fine, w1, w3, w2, ln2_gain, ln2_bias):
    batch, seq, d = h.shape
    t = batch * seq
    x2 = h.reshape(t, d)

    mod = _ada(c8, w_ada, b_ada.reshape(1, -1))
    mod3 = mod[:batch].reshape(batch, 6, d)

    wb = w_in.astype(BF16)
    qkv, u1 = _inproj_attn(x2, mod3, wb, seq)
    rest = _inproj_rest(u1, wb)

    outs, lses = zip(*[_attention(qkv, g, batch, seq) for g in range(ATTN_GROUPS)])
    decays = jnp.concatenate([decay_f, decay_b]).astype(F32)
    retg = _retention(rest, decays, gn_gain.reshape(1, -1), batch, seq)

    w_route = jnp.concatenate([w_coarse, w_fine.transpose(1, 0, 2).reshape(d, N_EXPERTS)], axis=1)
    n_route = w_route.shape[1]
    w_route = jnp.pad(w_route, ((0, 0), (0, LANES - n_route)))
    wr_hi, wr_lo = _split_bf16(w_route)
    rbias = jnp.pad(jnp.concatenate([b_coarse, b_fine.reshape(-1)]), (0, LANES - n_route)).reshape(1, LANES)
    ln1 = jnp.stack([ln1_gain, ln1_bias])
    n_blocks = 2 * t // MOE_BLK + N_EXPERTS
    h1, u2, logits, rows0 = _merge(outs, lses, retg, rest, x2, mod3, w_attn_out.astype(BF16),
                                   w_ret_out.astype(BF16), w_out.astype(BF16), ln1, wr_hi, wr_lo, rbias, seq,
                                   n_blocks * MOE_BLK)

    cols, ints, counts = _route(logits)
    n_blocks_pad = -(-n_blocks // LANES) * LANES
    dest, meta = _plan(ints, counts, n_blocks_pad)
    dest0, dest1 = dest[0], dest[1]
    rows = _dispatch(dest0, dest1, u2, rows0)
    y_rows = _experts(meta[0, :n_blocks], meta[1, :1], rows, w1, w3, w2)
    out = _combine(dest0, dest1, y_rows, cols, h1, mod3, jnp.stack([ln2_gain, ln2_bias]), seq)
    return out.reshape(batch, seq, d)


def kernel(x, c, w_ada, b_ada, w_in, w_attn_out, ret_decay_fwd, ret_decay_bwd, ret_gn_gain, w_ret_out, w_out,
           ln1_gain, ln1_bias, w_coarse, b_coarse, w_fine, b_fine, w1, w3, w2, ln2_gain, ln2_bias):
    batch = x.shape[0]
    max_dil = max(dil for _, dil in ATTN_PATTERNS)
    seq_tile = max(max_dil * ATTN_KB, RET_STEP, REST_TM, DISPATCH_TM)
    assert batch <= SUBLANES and x.shape[1] % seq_tile == 0 and x.shape[2] == D_MODEL
    c8 = jnp.pad(c, ((0, SUBLANES - batch), (0, 0)))
    h = x
    for l in range(w_ada.shape[0]):
        h = _layer(h, c8, w_ada[l], b_ada[l], w_in[l], w_attn_out[l], ret_decay_fwd[l], ret_decay_bwd[l],
                   ret_gn_gain[l], w_ret_out[l], w_out[l], ln1_gain[l], ln1_bias[l], w_coarse[l], b_coarse[l],
                   w_fine[l], b_fine[l], w1[l], w3[l], w2[l], ln2_gain[l], ln2_bias[l])
    return h
```

```python
import functools

import jax
import jax.numpy as jnp
from jax import lax
from jax.experimental import pallas as pl
from jax.experimental.pallas import tpu as pltpu

F32 = jnp.float32
BF16 = jnp.bfloat16
I32 = jnp.int32

D_MODEL = 1024
ATTN_GROUPS = 3
HEADS_PER_GROUP = 4
HEAD_DIM = 128
ATTN_HEADS = ATTN_GROUPS * HEADS_PER_GROUP
GROUP_WIDTH = HEADS_PER_GROUP * HEAD_DIM
ATTN_PATTERNS = ((128, 1), (512, 4), (2048, 16))
ALIBI_MAX_EXP = 8.0
NEG = -1e30
RET_HEADS = 4
RET_QK = 256
RET_V = 512
N_GROUPS = 4
EXPERTS_PER_GROUP = 8
N_EXPERTS = N_GROUPS * EXPERTS_PER_GROUP
EXPERT_FF = 512
DEPTH = 1
DEEPNORM_ALPHA = (2.0 * DEPTH) ** 0.25
LN_EPS = 1e-5

LANES = 128
SUBLANES = 8
PERM_TILE = 512
ATTN_QB = 128
ATTN_HALF = 64
ATTN_KB = ATTN_QB + 2 * ATTN_HALF
RET_CHUNK = 256
RET_STEP = 2048
MOE_BLK = 512
MOE_SUB = 256
VMEM_LIMIT = 56 * 1024 * 1024


def _cparams(sem):
    return pltpu.CompilerParams(dimension_semantics=sem, vmem_limit_bytes=VMEM_LIMIT)


def _split_bf16(a):
    hi = a.astype(BF16)
    lo = (a - hi.astype(F32)).astype(BF16)
    return hi, lo


def _dot(a, b):
    return jnp.dot(a, b, preferred_element_type=F32)


def _dot3(a, b):
    ah, al = _split_bf16(a)
    bh, bl = _split_bf16(b)
    return _dot(ah, bh) + _dot(ah, bl) + _dot(al, bh)


def _ln(x):
    mu = jnp.mean(x, axis=-1, keepdims=True)
    xc = x - mu
    var = jnp.mean(xc * xc, axis=-1, keepdims=True)
    return xc * lax.rsqrt(var + LN_EPS)


def _sigmoid(x):
    return 1.0 / (1.0 + jnp.exp(-x))


ROW_TILE = D_MODEL // LANES


def _store_token_tiles(ref, val, row0=0):
    n = val.shape[0]
    for cc in range(ROW_TILE):
        ref[pl.ds(row0 * ROW_TILE + cc, n, stride=ROW_TILE), :] = val[:, cc * LANES:(cc + 1) * LANES]


def _load_token_tiles(ref, n, row0=0):
    return jnp.concatenate([ref[pl.ds(row0 * ROW_TILE + cc, n, stride=ROW_TILE), :] for cc in range(ROW_TILE)],
                           axis=1)


def _interleave(phased):
    pending = []
    for gen in phased:
        pending.append(gen)
        pending = [g for g in pending if next(g, StopIteration) is not StopIteration]
    while pending:
        pending = [g for g in pending if next(g, StopIteration) is not StopIteration]


def _ada_kernel(c_ref, w_ref, b_ref, o_ref):
    o_ref[...] = _dot3(c_ref[...], w_ref[...]) + b_ref[...]


def _ada(c8, w_ada, b_ada):
    n = w_ada.shape[1]
    return pl.pallas_call(
        _ada_kernel,
        grid=(n // D_MODEL,),
        in_specs=[pl.BlockSpec((SUBLANES, D_MODEL), lambda j: (0, 0)),
                  pl.BlockSpec((D_MODEL, D_MODEL), lambda j: (0, j)),
                  pl.BlockSpec((1, D_MODEL), lambda j: (0, j))],
        out_specs=pl.BlockSpec((SUBLANES, D_MODEL), lambda j: (0, j)),
        out_shape=jax.ShapeDtypeStruct((SUBLANES, n), F32),
        compiler_params=_cparams(("arbitrary",)),
        name="ada",
    )(c8, w_ada, b_ada)


INPROJ_TM = 1024
LN_ROWS = 128
LN_UNROLL = 4


def _modulated_ln_rows(x_ref, mod_ref, emit):
    gain = 1.0 + mod_ref[1:2, :]
    shift = mod_ref[0:1, :]

    def chunk(c, carry):
        r0 = pl.multiple_of(c * LN_ROWS, LN_ROWS)
        emit(r0, _ln(x_ref[pl.ds(r0, LN_ROWS), :]) * gain + shift)
        return carry

    lax.fori_loop(0, INPROJ_TM // LN_ROWS, chunk, 0, unroll=LN_UNROLL)


def _inproj_attn_kernel(x_ref, mod_ref, wq_ref, wk_ref, wv_ref, o_ref, un_ref, uf_scr, u_scr):
    j = pl.program_id(1)

    @pl.when(j == 0)
    def _():
        def emit(r0, u):
            for cc in range(D_MODEL // LANES):
                uf_scr[cc, pl.ds(r0, LN_ROWS), :] = u[:, cc * LANES:(cc + 1) * LANES]
            ub = u.astype(BF16)
            u_scr[0, pl.ds(r0, LN_ROWS), :] = ub
            un_ref[pl.ds(r0, LN_ROWS), :] = ub

        _modulated_ln_rows(x_ref, mod_ref, emit)
        for g in (1, 2):
            dil = ATTN_PATTERNS[g][1]
            n = PERM_TILE // dil
            for t0 in range(0, INPROJ_TM, PERM_TILE):
                for res in range(dil):
                    for cc in range(D_MODEL // LANES):
                        rows = uf_scr[cc, pl.ds(t0 + res, n, stride=dil), :]
                        u_scr[g, t0 + res * n:t0 + (res + 1) * n, cc * LANES:(cc + 1) * LANES] = rows.astype(BF16)

    u = u_scr[j]
    for s, w_ref in enumerate((wq_ref, wk_ref, wv_ref)):
        o_ref[:, s * GROUP_WIDTH:(s + 1) * GROUP_WIDTH] = _dot(u, w_ref[...]).astype(BF16)


ATTN_WIDTH = ATTN_HEADS * HEAD_DIM
W_RQ0 = 3 * ATTN_WIDTH
W_RK0 = W_RQ0 + RET_HEADS * RET_QK
W_RV0 = W_RK0 + RET_HEADS * RET_QK
W_TAIL0 = W_RV0 + RET_HEADS * RET_V


def _inproj_attn(x2, mod3, wb, seq):
    t = x2.shape[0]
    tn = 3 * GROUP_WIDTH
    per_seq = seq // INPROJ_TM
    groups_per_range = ATTN_WIDTH // GROUP_WIDTH
    wspec = lambda s: pl.BlockSpec((D_MODEL, GROUP_WIDTH), lambda i, j: (0, s * groups_per_range + j))
    return pl.pallas_call(
        _inproj_attn_kernel,
        grid=(t // INPROJ_TM, ATTN_GROUPS),
        in_specs=[pl.BlockSpec((INPROJ_TM, D_MODEL), lambda i, j: (i, 0)),
                  pl.BlockSpec((None, 6, D_MODEL), lambda i, j: (i // per_seq, 0, 0)),
                  wspec(0), wspec(1), wspec(2)],
        out_specs=[pl.BlockSpec((INPROJ_TM, tn), lambda i, j: (i, j)),
                   pl.BlockSpec((INPROJ_TM, D_MODEL), lambda i, j: (i, 0))],
        out_shape=[jax.ShapeDtypeStruct((t, ATTN_GROUPS * tn), BF16),
                   jax.ShapeDtypeStruct((t, D_MODEL), BF16)],
        scratch_shapes=[pltpu.VMEM((D_MODEL // LANES, INPROJ_TM, LANES), F32),
                        pltpu.VMEM((ATTN_GROUPS, INPROJ_TM, D_MODEL), BF16)],
        compiler_params=_cparams(("arbitrary", "arbitrary")),
        name="inproj_attn",
    )(x2, mod3, wb, wb, wb)


REST_TN = RET_HEAD_COLS = 2 * RET_QK + RET_V
REST_TAIL_TILES = (RET_HEADS * RET_V + 2 * D_MODEL) // REST_TN
REST_TM = 2048


def _inproj_rest_kernel(u_ref, wq_ref, wk_ref, wv_ref, wt0_ref, wt1_ref, o_ref):
    j = pl.program_id(0)

    def project(parts):
        u = u_ref[...]
        c0 = 0
        for w_ref in parts:
            wd = w_ref.shape[1]
            o_ref[:, c0:c0 + wd] = _dot(u, w_ref[...]).astype(BF16)
            c0 += wd

    @pl.when(j < RET_HEADS)
    def _():
        project((wq_ref, wk_ref, wv_ref))

    @pl.when(j >= RET_HEADS)
    def _():
        project((wt0_ref, wt1_ref))


def _inproj_rest(u, wb):
    t = u.shape[0]
    half = REST_TN // 2
    assert W_TAIL0 % half == 0 and W_RV0 % RET_V == 0
    head = lambda j: jnp.minimum(j, RET_HEADS - 1)
    tail = lambda j: jnp.maximum(j - RET_HEADS, 0)
    return pl.pallas_call(
        _inproj_rest_kernel,
        grid=(RET_HEADS + REST_TAIL_TILES, t // REST_TM),
        in_specs=[pl.BlockSpec((REST_TM, D_MODEL), lambda j, i: (i, 0)),
                  pl.BlockSpec((D_MODEL, RET_QK), lambda j, i: (0, W_RQ0 // RET_QK + head(j))),
                  pl.BlockSpec((D_MODEL, RET_QK), lambda j, i: (0, W_RK0 // RET_QK + head(j))),
                  pl.BlockSpec((D_MODEL, RET_V), lambda j, i: (0, W_RV0 // RET_V + head(j))),
                  pl.BlockSpec((D_MODEL, half), lambda j, i: (0, W_TAIL0 // half + 2 * tail(j))),
                  pl.BlockSpec((D_MODEL, half), lambda j, i: (0, W_TAIL0 // half + 2 * tail(j) + 1))],
        out_specs=pl.BlockSpec((REST_TM, REST_TN), lambda j, i: (i, j)),
        out_shape=jax.ShapeDtypeStruct((t, (RET_HEADS + REST_TAIL_TILES) * REST_TN), BF16),
        compiler_params=_cparams(("arbitrary", "arbitrary")),
        name="inproj_rest",
    )(u, wb, wb, wb, wb, wb)


ATTN_OFFSETS = ATTN_KB // ATTN_HALF - 1
ATTN_UNROLL = 16
ATTN_MIN_BLOCKS = 4


def _attn_kernel(q_ref, k_ref, v_ref, o_ref, lse_ref, bias_scr, *scratch, group, dil, nt, n, nres):
    n_sub = nt * n
    if nt == 1:
        seqs = [(q_ref.at[0, rr], k_ref.at[0, rr], v_ref.at[0, rr], o_ref.at[0, rr], lse_ref.at[0, rr])
                for rr in range(nres)]
    else:
        seqs = [tuple(s.at[rr] for s in scratch) for rr in range(nres)]
        for rr, (qs, ks, vs, _, _) in enumerate(seqs):
            for t in range(nt):
                qs[t * n:(t + 1) * n, :] = q_ref[t, rr]
                ks[t * n:(t + 1) * n, :] = k_ref[t, rr]
                vs[t * n:(t + 1) * n, :] = v_ref[t, rr]

    @pl.when((pl.program_id(0) == 0) & (pl.program_id(1) == 0))
    def _():
        base = (lax.broadcasted_iota(I32, (ATTN_QB, ATTN_KB), 1)
                - lax.broadcasted_iota(I32, (ATTN_QB, ATTN_KB), 0))
        for j in range(ATTN_OFFSETS):
            dist = jnp.abs(base - j * ATTN_HALF)
            for hh in range(HEADS_PER_GROUP):
                slope = dil * 2.0 ** (-ALIBI_MAX_EXP * (group * HEADS_PER_GROUP + hh + 1) / ATTN_HEADS)
                bias_scr[hh * ATTN_OFFSETS + j] = jnp.where(dist <= ATTN_HALF, -slope * dist.astype(F32), NEG)

    scale = HEAD_DIM ** -0.5
    lane = lax.broadcasted_iota(I32, (ATTN_QB, LANES), 1)

    def block(blk, carry):
        q0 = pl.multiple_of(blk * ATTN_QB, ATTN_QB)
        start = pl.multiple_of(jnp.clip(q0 - ATTN_HALF, 0, n_sub - ATTN_KB), ATTN_HALF)
        j = (q0 - start) // ATTN_HALF
        for qs, ks, vs, os_, ls in seqs:
            lse_tile = jnp.zeros((ATTN_QB, LANES), F32)
            for hh in range(HEADS_PER_GROUP):
                cs = slice(hh * HEAD_DIM, (hh + 1) * HEAD_DIM)
                qb = qs[pl.ds(q0, ATTN_QB), cs]
                kb = ks[pl.ds(start, ATTN_KB), cs]
                vb = vs[pl.ds(start, ATTN_KB), cs]
                s = lax.dot_general(qb, kb, (((1,), (1,)), ((), ())), preferred_element_type=F32) * scale
                s = s + bias_scr[hh * ATTN_OFFSETS + j]
                m = jnp.max(s, axis=-1, keepdims=True)
                p = jnp.exp(s - m)
                l = jnp.sum(p, axis=-1, keepdims=True)
                o = _dot(p.astype(BF16), vb) * (1.0 / l)
                os_[pl.ds(q0, ATTN_QB), cs] = o.astype(BF16)
                lse_tile = jnp.where(lane == hh, m + jnp.log(l), lse_tile)
            ls[pl.ds(q0, ATTN_QB), :] = lse_tile
        return carry

    n_blk = n_sub // ATTN_QB
    lax.fori_loop(0, n_blk, block, 0, unroll=min(max(ATTN_UNROLL // nres, 1), n_blk))

    if nt > 1:
        for rr, (_, _, _, os_, ls) in enumerate(seqs):
            for t in range(nt):
                o_ref[t, rr] = os_[t * n:(t + 1) * n, :]
                lse_ref[t, rr] = ls[t * n:(t + 1) * n, :]


def _attention(qkv, group, batch, seq):
    dil = ATTN_PATTERNS[group][1]
    if dil == 1:
        nt, n = 1, seq
    else:
        nt, n = seq // PERM_TILE, PERM_TILE // dil
    n_sub = nt * n
    t = batch * seq
    qkv5 = qkv.reshape(batch, nt, dil, n, qkv.shape[1])
    cb = group * 3
    nres = min(dil, max(1, ATTN_MIN_BLOCKS * ATTN_QB // n_sub))
    blk = (None, nt, nres, n, GROUP_WIDTH)
    scratch = [pltpu.VMEM((HEADS_PER_GROUP * ATTN_OFFSETS, ATTN_QB, ATTN_KB), F32)]
    if nt > 1:
        scratch += [pltpu.VMEM((nres, n_sub, GROUP_WIDTH), BF16)] * 4 + [pltpu.VMEM((nres, n_sub, LANES), F32)]
    out, lse = pl.pallas_call(
        functools.partial(_attn_kernel, group=group, dil=dil, nt=nt, n=n, nres=nres),
        grid=(batch, dil // nres),
        in_specs=[pl.BlockSpec(blk, lambda b, r: (b, 0, r, 0, cb)),
                  pl.BlockSpec(blk, lambda b, r: (b, 0, r, 0, cb + 1)),
                  pl.BlockSpec(blk, lambda b, r: (b, 0, r, 0, cb + 2))],
        out_specs=[pl.BlockSpec(blk, lambda b, r: (b, 0, r, 0, 0)),
                   pl.BlockSpec((None, nt, nres, n, LANES), lambda b, r: (b, 0, r, 0, 0))],
        out_shape=[jax.ShapeDtypeStruct((batch, nt, dil, n, GROUP_WIDTH), BF16),
                   jax.ShapeDtypeStruct((batch, nt, dil, n, LANES), F32)],
        scratch_shapes=scratch,
        compiler_params=_cparams(("arbitrary", "arbitrary")),
        name=f"attn_g{group}",
    )(qkv5, qkv5, qkv5)
    return out.reshape(t, GROUP_WIDTH), lse.reshape(t, LANES)


def _log_sigmoid(x):
    return jnp.minimum(x, 0.0) - jnp.log(1.0 + jnp.exp(-jnp.abs(x)))


def _retention_kernel(decay_ref, qkv_ref, g_ref, gain_ref, o_ref, state, ybwd, dmat, kdec, qdec, *, nc):
    c = RET_CHUNK
    h = pl.program_id(1)
    i = pl.program_id(2)
    kscale = RET_QK ** -0.5

    def set_decays(lg, forward):
        row = lax.broadcasted_iota(I32, (c, c), 0)
        col = lax.broadcasted_iota(I32, (c, c), 1)
        pos = lax.broadcasted_iota(I32, (c, LANES), 0).astype(F32)
        if forward:
            gap, key_pow, query_pow = row - col, (c - 1.0) - pos, pos + 1.0
            keep = gap >= 0
        else:
            gap, key_pow, query_pow = col - row, pos, c - pos
            keep = gap > 0
        dmat[...] = jnp.where(keep, jnp.exp(lg * jnp.maximum(gap, 0).astype(F32)) * kscale, 0.0)
        kdec[...] = jnp.exp(lg * key_pow) * kscale
        qdec[...] = jnp.exp(lg * query_pow)

    def chunk_step(lg, r0, finish):
        q = qkv_ref[r0:r0 + c, 0:RET_QK]
        k = qkv_ref[r0:r0 + c, RET_QK:2 * RET_QK]
        v = qkv_ref[r0:r0 + c, 2 * RET_QK:]
        inner = lax.dot_general(q, k, (((1,), (1,)), ((), ())), preferred_element_type=F32) * dmat[...]
        y = _dot(inner.astype(BF16), v)
        kd = (k.astype(F32) * jnp.concatenate([kdec[...]] * (RET_QK // LANES), axis=1)).astype(BF16)
        kv = lax.dot_general(kd, v, (((0,), (0,)), ((), ())), preferred_element_type=F32)
        yield
        qd = jnp.concatenate([qdec[...]] * (RET_V // LANES), axis=1)
        y = y + _dot(q, state[...].astype(BF16)) * qd
        state[...] = state[...] * jnp.exp(lg * float(c)) + kv
        yield
        finish(r0, y)

    @pl.when((i == 0) | (i == nc))
    def _():
        state[...] = jnp.zeros_like(state)

    @pl.when(i < nc)
    def _():
        lg = _log_sigmoid(jnp.zeros((1, 1), F32) + decay_ref[RET_HEADS + h])

        @pl.when(i == 0)
        def _():
            set_decays(lg, False)

        base = pl.multiple_of((nc - 1 - i) * RET_STEP, RET_STEP)

        def finish(r0, y):
            ybwd[pl.ds(base + r0, c), :] = y

        _interleave(chunk_step(lg, r0, finish) for r0 in reversed(range(0, RET_STEP, c)))

    @pl.when(i >= nc)
    def _():
        lg = _log_sigmoid(jnp.zeros((1, 1), F32) + decay_ref[h])

        @pl.when(i == nc)
        def _():
            set_decays(lg, True)

        base = pl.multiple_of((i - nc) * RET_STEP, RET_STEP)

        def finish(r0, y):
            y = y + ybwd[pl.ds(base + r0, c), :]
            g = g_ref[r0:r0 + c, :].astype(F32)
            o_ref[r0:r0 + c, :] = (g * _sigmoid(g) * (_ln(y) * gain_ref[...])).astype(BF16)

        _interleave(chunk_step(lg, r0, finish) for r0 in range(0, RET_STEP, c))


def _retention(rest, decays, gn_gain, batch, seq):
    c = RET_STEP
    nc = seq // c
    rest3 = rest.reshape(batch, seq, rest.shape[1])
    gate_blk = RET_HEADS * RET_HEAD_COLS // RET_V

    def chunk(i):
        return jnp.where(i < nc, nc - 1 - i, i - nc)

    grid_spec = pltpu.PrefetchScalarGridSpec(
        num_scalar_prefetch=1,
        grid=(batch, RET_HEADS, 2 * nc),
        in_specs=[pl.BlockSpec((None, c, RET_HEAD_COLS), lambda b, h, i, d: (b, chunk(i), h)),
                  pl.BlockSpec((None, c, RET_V), lambda b, h, i, d: (b, jnp.maximum(i - nc, 0), gate_blk + h)),
                  pl.BlockSpec((1, RET_V), lambda b, h, i, d: (0, h))],
        out_specs=pl.BlockSpec((None, c, RET_V), lambda b, h, i, d: (b, jnp.maximum(i - nc, 0), h)),
        scratch_shapes=[pltpu.VMEM((RET_QK, RET_V), F32), pltpu.VMEM((seq, RET_V), F32),
                        pltpu.VMEM((RET_CHUNK, RET_CHUNK), F32), pltpu.VMEM((RET_CHUNK, LANES), F32),
                        pltpu.VMEM((RET_CHUNK, LANES), F32)],
    )
    out = pl.pallas_call(
        functools.partial(_retention_kernel, nc=nc),
        grid_spec=grid_spec,
        out_shape=jax.ShapeDtypeStruct((batch, seq, RET_HEADS * RET_V), BF16),
        compiler_params=_cparams(("arbitrary", "arbitrary", "arbitrary")),
        name="retention",
    )(decays, rest3, rest3, gn_gain)
    return out.reshape(batch * seq, RET_HEADS * RET_V)


MERGE_TM = PERM_TILE
MERGE_SUB = 256
ZERO_SPLIT = 4


def _merge_kernel(o0_ref, o1_ref, o2_ref, l0_ref, l1_ref, l2_ref, retg_ref, ga_ref, gr_ref, x_ref, mod_ref,
                  wa_ref, wr_ref, wo_ref, ln_ref, wrh_ref, wrl_ref, rb_ref,
                  h1_ref, u2_ref, lg_ref, zrows_ref, on_scr, ln_scr, zbuf, zsem):
    i = pl.program_id(0)

    @pl.when(i == 0)
    def _():
        zbuf[...] = jnp.zeros_like(zbuf)

    zrows = zbuf.shape[0]
    zero_copies = [pltpu.make_async_copy(zbuf, zrows_ref.at[pl.ds((i * ZERO_SPLIT + k) * zrows, zrows)], zsem)
                   for k in range(ZERO_SPLIT)]
    for cp in zero_copies:
        cp.start()

    for g in (1, 2):
        dil = ATTN_PATTERNS[g][1]
        n = MERGE_TM // dil
        o_ref, l_ref = ((o1_ref, l1_ref), (o2_ref, l2_ref))[g - 1]
        for res in range(dil):
            rows = o_ref[res * n:(res + 1) * n, :].astype(F32)
            for hh in range(HEADS_PER_GROUP):
                on_scr[g - 1, hh, pl.ds(res, n, stride=dil), :] = rows[:, hh * HEAD_DIM:(hh + 1) * HEAD_DIM]
            ln_scr[g - 1, pl.ds(res, n, stride=dil), :] = l_ref[res * n:(res + 1) * n, :]

    def sub_tile(r0):
        rs = slice(r0, r0 + MERGE_SUB)
        l0, l1, l2 = l0_ref[rs, :], ln_scr[0, rs, :], ln_scr[1, rs, :]
        lm = jnp.maximum(jnp.maximum(l0, l1), l2)
        e0, e1, e2 = jnp.exp(l0 - lm), jnp.exp(l1 - lm), jnp.exp(l2 - lm)
        inv = 1.0 / (e0 + e1 + e2)
        parts = []
        for hh in range(HEADS_PER_GROUP):
            sl = slice(hh * HEAD_DIM, (hh + 1) * HEAD_DIM)
            acc = (e0[:, hh:hh + 1] * o0_ref[rs, sl].astype(F32)
                   + e1[:, hh:hh + 1] * on_scr[0, hh, rs, :]
                   + e2[:, hh:hh + 1] * on_scr[1, hh, rs, :])
            parts.append((acc * inv[:, hh:hh + 1]).astype(BF16))
        attn = jnp.concatenate(parts, axis=1)
        yield

        branch_a = _dot(attn, wa_ref[...])
        branch_r = _dot(retg_ref[rs, :], wr_ref[...])
        yield
        merged = (_sigmoid(ga_ref[rs, :].astype(F32)) * branch_a
                  + _sigmoid(gr_ref[rs, :].astype(F32)) * branch_r)
        yield
        y = _dot(merged.astype(BF16), wo_ref[...])
        yield

        h1 = _ln(DEEPNORM_ALPHA * x_ref[rs, :] + mod_ref[2:3, :] * y) * ln_ref[0:1, :] + ln_ref[1:2, :]
        h1_ref[rs, :] = h1
        u2 = _ln(h1) * (1.0 + mod_ref[4:5, :]) + mod_ref[3:4, :]
        _store_token_tiles(u2_ref, u2, r0)
        uh, ul = _split_bf16(u2)
        yield
        lg_ref[rs, :] = (_dot(uh, wrh_ref[...]) + _dot(uh, wrl_ref[...]) + _dot(ul, wrh_ref[...])
                         + rb_ref[...])

    _interleave(sub_tile(r0) for r0 in range(0, MERGE_TM, MERGE_SUB))

    for cp in zero_copies:
        cp.wait()


def _merge(outs, lses, retg, rest, x2, mod3, wa, wr, wo, ln1, wr_hi, wr_lo, rbias, seq, n_rows):
    t = x2.shape[0]
    tm = MERGE_TM
    per_seq = seq // tm
    zrows, rem = divmod(n_rows * ROW_TILE, (t // tm) * ZERO_SPLIT)
    assert rem == 0 and zrows % SUBLANES == 0
    row = lambda w: pl.BlockSpec((tm, w), lambda i: (i, 0))
    full = lambda a: pl.BlockSpec(a.shape, lambda i: (0,) * a.ndim)
    return pl.pallas_call(
        _merge_kernel,
        grid=(t // tm,),
        in_specs=[row(GROUP_WIDTH)] * 3 + [row(LANES)] * 3 + [
            row(RET_HEADS * RET_V),
            pl.BlockSpec((tm, D_MODEL), lambda i: (i, 6)),
            pl.BlockSpec((tm, D_MODEL), lambda i: (i, 7)),
            row(D_MODEL),
            pl.BlockSpec((None, 6, D_MODEL), lambda i: (i // per_seq, 0, 0)),
            full(wa), full(wr), full(wo), full(ln1), full(wr_hi), full(wr_lo), full(rbias)],
        out_specs=[row(D_MODEL), pl.BlockSpec((tm * ROW_TILE, LANES), lambda i: (i, 0)), row(LANES),
                   pl.BlockSpec(memory_space=pl.ANY)],
        out_shape=[jax.ShapeDtypeStruct((t, D_MODEL), F32),
                   jax.ShapeDtypeStruct((t * ROW_TILE, LANES), F32),
                   jax.ShapeDtypeStruct((t, LANES), F32),
                   jax.ShapeDtypeStruct((n_rows * ROW_TILE, LANES), F32)],
        scratch_shapes=[pltpu.VMEM((2, HEADS_PER_GROUP, tm, HEAD_DIM), F32), pltpu.VMEM((2, tm, LANES), F32),
                        pltpu.VMEM((zrows, LANES), F32), pltpu.SemaphoreType.DMA(())],
        compiler_params=_cparams(("arbitrary",)),
        name="merge",
    )(*outs, *lses, retg, rest, rest, x2, mod3, wa, wr, wo, ln1, wr_hi, wr_lo, rbias)


ROUTE_TM = 512
BIG = 1 << 20


def _route_kernel(lg_ref, cols_ref, ints_ref, cnt_ref, carry, tri):
    i = pl.program_id(0)
    tm = ROUTE_TM

    @pl.when(i == 0)
    def _():
        carry[...] = jnp.zeros_like(carry)
        r_i = lax.broadcasted_iota(I32, (tm, tm), 0)
        c_i = lax.broadcasted_iota(I32, (tm, tm), 1)
        tri[...] = jnp.where(r_i > c_i, 1.0, 0.0).astype(BF16)

    lg = lg_ref[...]
    lane = lax.broadcasted_iota(I32, (tm, LANES), 1)
    lane_f = lane.astype(F32)
    first = lambda mask: jnp.min(jnp.where(mask, lane_f, float(BIG)), axis=-1, keepdims=True).astype(I32)

    coarse = jnp.where(lane < N_GROUPS, lg, NEG)
    cmax = jnp.max(coarse, axis=-1, keepdims=True)
    gsel = first(coarse == cmax)
    p_group = 1.0 / jnp.sum(jnp.exp(coarse - cmax), axis=-1, keepdims=True)

    lo = N_GROUPS + EXPERTS_PER_GROUP * gsel
    fine = jnp.where((lane >= lo) & (lane < lo + EXPERTS_PER_GROUP), lg, NEG)
    v1 = jnp.max(fine, axis=-1, keepdims=True)
    i1 = first(fine == v1)
    fine2 = jnp.where(lane == i1, NEG, fine)
    v2 = jnp.max(fine2, axis=-1, keepdims=True)
    i2 = first(fine2 == v2)
    ex = jnp.exp(v2 - v1)
    den = 1.0 / (1.0 + ex)
    gate1 = p_group * den
    gate2 = p_group * (ex * den)
    e1 = i1 - N_GROUPS
    e2 = i2 - N_GROUPS

    oh1 = lane == e1
    oh2 = lane == e2
    cnt = jnp.where(oh1 | oh2, 1.0, 0.0)
    rank = _dot(tri[...], cnt.astype(BF16)) + carry[...]
    r1 = jnp.sum(jnp.where(oh1, rank, 0.0), axis=-1, keepdims=True)
    r2 = jnp.sum(jnp.where(oh2, rank, 0.0), axis=-1, keepdims=True)
    carry[...] = carry[...] + jnp.sum(cnt, axis=0, keepdims=True)
    cnt_ref[...] = jnp.broadcast_to(carry[...], cnt_ref.shape)

    cols_ref[...] = jnp.where(lane == 0, gate1, jnp.where(lane == 1, gate2, 0.0))
    packed = jnp.where(lane == 0, e1.astype(F32),
                       jnp.where(lane == 1, e2.astype(F32),
                                 jnp.where(lane == 2, r1, jnp.where(lane == 3, r2, 0.0))))
    ints_ref[...] = packed.T[0:SUBLANES, :].astype(I32)


def _route(logits):
    t = logits.shape[0]
    tm = ROUTE_TM
    return pl.pallas_call(
        _route_kernel,
        grid=(t // tm,),
        in_specs=[pl.BlockSpec((tm, LANES), lambda i: (i, 0))],
        out_specs=[pl.BlockSpec((tm, LANES), lambda i: (i, 0)),
                   pl.BlockSpec((SUBLANES, tm), lambda i: (0, i)),
                   pl.BlockSpec((SUBLANES, LANES), lambda i: (0, 0))],
        out_shape=[jax.ShapeDtypeStruct((t, LANES), F32),
                   jax.ShapeDtypeStruct((SUBLANES, t), I32),
                   jax.ShapeDtypeStruct((SUBLANES, LANES), F32)],
        scratch_shapes=[pltpu.VMEM((1, LANES), F32), pltpu.VMEM((tm, tm), BF16)],
        compiler_params=_cparams(("arbitrary",)),
        name="route",
    )(logits)


def _plan_kernel(ints_ref, cnt_ref, dest_ref, meta_ref, *, n_blocks_pad):
    sub = lax.broadcasted_iota(I32, (LANES, LANES), 0)
    lane = lax.broadcasted_iota(I32, (LANES, LANES), 1)
    cnt = cnt_ref[0:1, :]
    nblk_row = jnp.floor((cnt + (MOE_BLK - 1.0)) * (1.0 / MOE_BLK))
    nblk_mat = jnp.broadcast_to(nblk_row, (LANES, LANES))
    start_col = jnp.sum(jnp.where(lane < sub, nblk_mat, 0.0), axis=-1, keepdims=True)
    nblk_col = jnp.sum(jnp.where(lane == sub, nblk_mat, 0.0), axis=-1, keepdims=True)
    end_col = start_col + nblk_col

    ints = ints_ref[...]
    base = jnp.zeros(ints.shape, F32)
    for e in range(N_EXPERTS):
        base = jnp.where(ints == e, start_col[e:e + 1, :] * float(MOE_BLK), base)
    dest = base[0:2, :].astype(I32) + ints[2:4, :]
    dest_ref[...] = jnp.concatenate([dest, jnp.zeros((SUBLANES - 2, ints.shape[1]), I32)], axis=0)

    blk = lax.broadcasted_iota(I32, (LANES, n_blocks_pad), 1).astype(F32)
    e_sub = lax.broadcasted_iota(I32, (LANES, n_blocks_pad), 0)
    done = jnp.where((e_sub < N_EXPERTS) & (end_col <= blk), 1.0, 0.0)
    bexp = jnp.minimum(jnp.sum(done, axis=0, keepdims=True), N_EXPERTS - 1.0)
    used = jnp.sum(nblk_row, axis=-1, keepdims=True)
    row = lax.broadcasted_iota(I32, (SUBLANES, n_blocks_pad), 0)
    meta = jnp.where(row == 0, bexp, jnp.where(row == 1, used, 0.0))
    meta_ref[...] = meta.astype(I32)


def _plan(ints, counts, n_blocks_pad):
    t = ints.shape[1]
    return pl.pallas_call(
        functools.partial(_plan_kernel, n_blocks_pad=n_blocks_pad),
        out_shape=[jax.ShapeDtypeStruct((SUBLANES, t), I32), jax.ShapeDtypeStruct((SUBLANES, n_blocks_pad), I32)],
        compiler_params=pltpu.CompilerParams(vmem_limit_bytes=VMEM_LIMIT),
        name="plan",
    )(ints, counts)


DISPATCH_TM = 2048
DMA_UNROLL = 16


def _row_tile(ref, r):
    return ref.at[pl.ds(pl.multiple_of(r * ROW_TILE, ROW_TILE), ROW_TILE)]


def _dispatch_kernel(d0_ref, d1_ref, u2_ref, rows_in_ref, rows_ref, sem):
    del rows_in_ref

    def issue(t, carry):
        src = _row_tile(u2_ref, t)
        pltpu.make_async_copy(src, _row_tile(rows_ref, d0_ref[t]), sem).start(priority=0)
        pltpu.make_async_copy(src, _row_tile(rows_ref, d1_ref[t]), sem).start(priority=1)
        return carry

    lax.fori_loop(0, DISPATCH_TM, issue, 0, unroll=DMA_UNROLL)
    for _ in range(2):
        pltpu.make_async_copy(u2_ref, rows_ref.at[pl.ds(0, DISPATCH_TM * ROW_TILE)], sem).wait()


def _dispatch(dest0, dest1, u2, rows0):
    t = u2.shape[0] // ROW_TILE
    idx = pl.BlockSpec((DISPATCH_TM,), lambda i: (i,), memory_space=pltpu.SMEM)
    return pl.pallas_call(
        _dispatch_kernel,
        grid=(t // DISPATCH_TM,),
        in_specs=[idx, idx,
                  pl.BlockSpec((DISPATCH_TM * ROW_TILE, LANES), lambda i: (i, 0)),
                  pl.BlockSpec(memory_space=pl.ANY)],
        out_specs=pl.BlockSpec(memory_space=pl.ANY),
        out_shape=jax.ShapeDtypeStruct(rows0.shape, F32),
        scratch_shapes=[pltpu.SemaphoreType.DMA(())],
        input_output_aliases={3: 0},
        compiler_params=_cparams(("arbitrary",)),
        name="dispatch",
    )(dest0, dest1, u2, rows0)


def _expert_runs(bexp, used):
    n = bexp.shape[0]
    idx = jnp.arange(n, dtype=I32)
    first = (idx < used[0]) & ((idx == 0) | (bexp != jnp.roll(bexp, 1)))
    slot = (jnp.cumsum(first.astype(I32)) - 1) % 2
    first_at_or_after = lax.cummin(jnp.where(first, idx, n)[::-1])[::-1]
    first_after = jnp.concatenate([first_at_or_after[1:], jnp.full((1,), n, I32)])
    nxt = jnp.where(first_after < n, bexp[jnp.minimum(first_after, n - 1)], -1)
    return first.astype(I32), slot.astype(I32), nxt.astype(I32)


def _experts_kernel(bexp_ref, used_ref, first_ref, slot_ref, nxt_ref, x_ref, w1_hbm, w3_hbm, w2_hbm, y_ref,
                    wb1, wb3, wb2, w1s, w3s, w2s, sem):
    i = pl.program_id(0)
    active = i < used_ref[0]
    slot = slot_ref[i]

    def fetch(e, s):
        return [pltpu.make_async_copy(w.at[e], buf.at[s], sem.at[s])
                for w, buf in ((w1_hbm, wb1), (w3_hbm, wb3), (w2_hbm, wb2))]

    @pl.when(i == 0)
    def _():
        for cp in fetch(bexp_ref[0], 0):
            cp.start()

    @pl.when(first_ref[i] == 1)
    def _():
        @pl.when(nxt_ref[i] >= 0)
        def _():
            for cp in fetch(nxt_ref[i], 1 - slot):
                cp.start()

        for cp in fetch(bexp_ref[i], slot):
            cp.wait()
        w1s[...] = wb1[slot].astype(BF16)
        w3s[...] = wb3[slot].astype(BF16)
        w2s[...] = wb2[slot].astype(BF16)

    @pl.when(active)
    def _():
        def sub_block(r0):
            xb = _load_token_tiles(x_ref, MOE_SUB, r0).astype(BF16)
            yield
            a = _dot(xb, w1s[...])
            b = _dot(xb, w3s[...])
            yield
            hdn = (a * _sigmoid(a) * b).astype(BF16)
            yield
            _store_token_tiles(y_ref, _dot(hdn, w2s[...]), r0)

        _interleave(sub_block(r0) for r0 in range(0, MOE_BLK, MOE_SUB))


def _experts(bexp, used, rows, w1, w3, w2):
    n_blocks = rows.shape[0] // (MOE_BLK * ROW_TILE)
    first, slot, nxt = _expert_runs(bexp, used)
    any_space = pl.BlockSpec(memory_space=pl.ANY)
    row_blk = pl.BlockSpec((MOE_BLK * ROW_TILE, LANES), lambda i, be, nu, *_: (jnp.minimum(i, nu[0] - 1), 0))
    grid_spec = pltpu.PrefetchScalarGridSpec(
        num_scalar_prefetch=5,
        grid=(n_blocks,),
        in_specs=[row_blk, any_space, any_space, any_space],
        out_specs=row_blk,
        scratch_shapes=[pltpu.VMEM((2, D_MODEL, EXPERT_FF), F32), pltpu.VMEM((2, D_MODEL, EXPERT_FF), F32),
                        pltpu.VMEM((2, EXPERT_FF, D_MODEL), F32),
                        pltpu.VMEM((D_MODEL, EXPERT_FF), BF16), pltpu.VMEM((D_MODEL, EXPERT_FF), BF16),
                        pltpu.VMEM((EXPERT_FF, D_MODEL), BF16), pltpu.SemaphoreType.DMA((2,))],
    )
    return pl.pallas_call(
        _experts_kernel,
        grid_spec=grid_spec,
        out_shape=jax.ShapeDtypeStruct(rows.shape, F32),
        input_output_aliases={5: 0},
        compiler_params=_cparams(("arbitrary",)),
        name="experts",
    )(bexp, used, first, slot, nxt, rows, w1, w3, w2)


COMBINE_TM = 256


def _combine_kernel(d0_ref, d1_ref, n0_ref, n1_ref, y_ref, cols_ref, h1_ref, mod_ref, ln_ref, o_ref,
                    ya0, yb0, ya1, yb1, sem):
    i = pl.program_id(0)

    def copies(i0_ref, i1_ref, t, a, b, s):
        return (pltpu.make_async_copy(_row_tile(y_ref, i0_ref[t]), _row_tile(a, t), sem.at[s]),
                pltpu.make_async_copy(_row_tile(y_ref, i1_ref[t]), _row_tile(b, t), sem.at[s]))

    def wait_tile(a, b, s):
        for buf in (a, b):
            pltpu.make_async_copy(y_ref.at[pl.ds(0, COMBINE_TM * ROW_TILE)], buf, sem.at[s]).wait()

    @pl.when(i == 0)
    def _():
        def issue(t, carry):
            for prio, cp in enumerate(copies(d0_ref, d1_ref, t, ya0, yb0, 0)):
                cp.start(priority=prio)
            return carry

        lax.fori_loop(0, COMBINE_TM, issue, 0, unroll=DMA_UNROLL)

    def step(cur_a, cur_b, cur_s, nxt_a, nxt_b, nxt_s):
        wait_tile(cur_a, cur_b, cur_s)
        for t in range(COMBINE_TM):
            for prio, cp in enumerate(copies(n0_ref, n1_ref, t, nxt_a, nxt_b, nxt_s)):
                cp.start(priority=prio)
        cols = cols_ref[...]
        moe = (cols[:, 0:1] * _load_token_tiles(cur_a, COMBINE_TM)
               + cols[:, 1:2] * _load_token_tiles(cur_b, COMBINE_TM))
        pre = DEEPNORM_ALPHA * h1_ref[...] + mod_ref[5:6, :] * moe
        o_ref[...] = _ln(pre) * ln_ref[0:1, :] + ln_ref[1:2, :]

        @pl.when(i == pl.num_programs(0) - 1)
        def _():
            wait_tile(nxt_a, nxt_b, nxt_s)

    @pl.when(i % 2 == 0)
    def _():
        step(ya0, yb0, 0, ya1, yb1, 1)

    @pl.when(i % 2 == 1)
    def _():
        step(ya1, yb1, 1, ya0, yb0, 0)


def _combine(dest0, dest1, y_rows, cols, h1, mod3, ln2, seq):
    t = h1.shape[0]
    tm = COMBINE_TM
    per_seq = seq // tm
    idx = pl.BlockSpec((tm,), lambda i: (i,), memory_space=pltpu.SMEM)
    idx_next = pl.BlockSpec((tm,), lambda i: (jnp.minimum(i + 1, t // tm - 1),), memory_space=pltpu.SMEM)
    return pl.pallas_call(
        _combine_kernel,
        grid=(t // tm,),
        in_specs=[idx, idx, idx_next, idx_next,
                  pl.BlockSpec(memory_space=pl.ANY),
                  pl.BlockSpec((tm, LANES), lambda i: (i, 0)),
                  pl.BlockSpec((tm, D_MODEL), lambda i: (i, 0)),
                  pl.BlockSpec((None, 6, D_MODEL), lambda i: (i // per_seq, 0, 0)),
                  pl.BlockSpec((2, D_MODEL), lambda i: (0, 0))],
        out_specs=pl.BlockSpec((tm, D_MODEL), lambda i: (i, 0)),
        out_shape=jax.ShapeDtypeStruct((t, D_MODEL), F32),
        scratch_shapes=[pltpu.VMEM((tm * ROW_TILE, LANES), F32)] * 4 + [pltpu.SemaphoreType.DMA((2,))],
        compiler_params=_cparams(("arbitrary",)),
        name="combine",
    )(dest0, dest1, dest0, dest1, y_rows, cols, h1, mod3, ln2)


def _layer(h, c8, w_ada, b_ada, w_in, w_attn_out, decay_f, decay_b, gn_gain, w_ret_out, w_out,
           ln1_gain, ln1_bias, w_coarse, b_coarse, w_fine, b_fine, w1, w3, w2, ln2_gain, ln2_bias):
    batch, seq, d = h.shape
    t = batch * seq
    x2 = h.reshape(t, d)

    mod = _ada(c8, w_ada, b_ada.reshape(1, -1))
    mod3 = mod[:batch].reshape(batch, 6, d)

    wb = w_in.astype(BF16)
    qkv, u1 = _inproj_attn(x2, mod3, wb, seq)
    rest = _inproj_rest(u1, wb)

    outs, lses = zip(*[_attention(qkv, g, batch, seq) for g in range(ATTN_GROUPS)])
    decays = jnp.concatenate([decay_f, decay_b]).astype(F32)
    retg = _retention(rest, decays, gn_gain.reshape(1, -1), batch, seq)

    w_route = jnp.concatenate([w_coarse, w_fine.transpose(1, 0, 2).reshape(d, N_EXPERTS)], axis=1)
    n_route = w_route.shape[1]
    w_route = jnp.pad(w_route, ((0, 0), (0, LANES - n_route)))
    wr_hi, wr_lo = _split_bf16(w_route)
    rbias = jnp.pad(jnp.concatenate([b_coarse, b_fine.reshape(-1)]), (0, LANES - n_route)).reshape(1, LANES)
    ln1 = jnp.stack([ln1_gain, ln1_bias])
    n_blocks = 2 * t // MOE_BLK + N_EXPERTS
    h1, u2, logits, rows0 = _merge(outs, lses, retg, rest, x2, mod3, w_attn_out.astype(BF16),
                                   w_ret_out.astype(BF16), w_out.astype(BF16), ln1, wr_hi, wr_lo, rbias, seq,
                                   n_blocks * MOE_BLK)

    cols, ints, counts = _route(logits)
    n_blocks_pad = -(-n_blocks // LANES) * LANES
    dest, meta = _plan(ints, counts, n_blocks_pad)
    dest0, dest1 = dest[0], dest[1]
    rows = _dispatch(dest0, dest1, u2, rows0)
    y_rows = _experts(meta[0, :n_blocks], meta[1, :1], rows, w1, w3, w2)
    out = _combine(dest0, dest1, y_rows, cols, h1, mod3, jnp.stack([ln2_gain, ln2_bias]), seq)
    return out.reshape(batch, seq, d)


def kernel(x, c, w_ada, b_ada, w_in, w_attn_out, ret_decay_fwd, ret_decay_bwd, ret_gn_gain, w_ret_out, w_out,
           ln1_gain, ln1_bias, w_coarse, b_coarse, w_fine, b_fine, w1, w3, w2, ln2_gain, ln2_bias):
    batch = x.shape[0]
    max_dil = max(dil for _, dil in ATTN_PATTERNS)
    seq_tile = max(max_dil * ATTN_KB, RET_STEP, REST_TM, DISPATCH_TM)
    assert batch <= SUBLANES and x.shape[1] % seq_tile == 0 and x.shape[2] == D_MODEL
    c8 = jnp.pad(c, ((0, SUBLANES - batch), (0, 0)))
    h = x
    for l in range(w_ada.shape[0]):
        h = _layer(h, c8, w_ada[l], b_ada[l], w_in[l], w_attn_out[l], ret_decay_fwd[l], ret_decay_bwd[l],
                   ret_gn_gain[l], w_ret_out[l], w_out[l], ln1_gain[l], ln1_bias[l], w_coarse[l], b_coarse[l],
                   w_fine[l], b_fine[l], w1[l], w3[l], w2[l], ln2_gain[l], ln2_bias[l])
    return h
```

```python
import functools

import jax
import jax.numpy as jnp
from jax import lax
from jax.experimental import pallas as pl
from jax.experimental.pallas import tpu as pltpu

F32 = jnp.float32
BF16 = jnp.bfloat16
I32 = jnp.int32

D_MODEL = 1024
ATTN_GROUPS = 3
HEADS_PER_GROUP = 4
HEAD_DIM = 128
ATTN_HEADS = ATTN_GROUPS * HEADS_PER_GROUP
GROUP_WIDTH = HEADS_PER_GROUP * HEAD_DIM
ATTN_PATTERNS = ((128, 1), (512, 4), (2048, 16))
ALIBI_MAX_EXP = 8.0
NEG = -1e30
RET_HEADS = 4
RET_QK = 256
RET_V = 512
N_GROUPS = 4
EXPERTS_PER_GROUP = 8
N_EXPERTS = N_GROUPS * EXPERTS_PER_GROUP
EXPERT_FF = 512
DEPTH = 1
DEEPNORM_ALPHA = (2.0 * DEPTH) ** 0.25
LN_EPS = 1e-5

LANES = 128
SUBLANES = 8
PERM_TILE = 512
ATTN_QB = 128
ATTN_HALF = 64
ATTN_KB = ATTN_QB + 2 * ATTN_HALF
RET_CHUNK = 256
RET_STEP = 2048
MOE_BLK = 512
MOE_SUB = 256
VMEM_LIMIT = 56 * 1024 * 1024


def _cparams(sem):
    return pltpu.CompilerParams(dimension_semantics=sem, vmem_limit_bytes=VMEM_LIMIT)


def _split_bf16(a):
    hi = a.astype(BF16)
    lo = (a - hi.astype(F32)).astype(BF16)
    return hi, lo


def _dot(a, b):
    return jnp.dot(a, b, preferred_element_type=F32)


def _dot3(a, b):
    ah, al = _split_bf16(a)
    bh, bl = _split_bf16(b)
    return _dot(ah, bh) + _dot(ah, bl) + _dot(al, bh)


def _ln(x):
    mu = jnp.mean(x, axis=-1, keepdims=True)
    xc = x - mu
    var = jnp.mean(xc * xc, axis=-1, keepdims=True)
    return xc * lax.rsqrt(var + LN_EPS)


def _sigmoid(x):
    return 1.0 / (1.0 + jnp.exp(-x))


ROW_TILE = D_MODEL // LANES


def _store_token_tiles(ref, val, row0=0):
    n = val.shape[0]
    for cc in range(ROW_TILE):
        ref[pl.ds(row0 * ROW_TILE + cc, n, stride=ROW_TILE), :] = val[:, cc * LANES:(cc + 1) * LANES]


def _load_token_tiles(ref, n, row0=0):
    return jnp.concatenate([ref[pl.ds(row0 * ROW_TILE + cc, n, stride=ROW_TILE), :] for cc in range(ROW_TILE)],
                           axis=1)


def _interleave(phased):
    pending = []
    for gen in phased:
        pending.append(gen)
        pending = [g for g in pending if next(g, StopIteration) is not StopIteration]
    while pending:
        pending = [g for g in pending if next(g, StopIteration) is not StopIteration]


def _ada_kernel(c_ref, w_ref, b_ref, o_ref):
    o_ref[...] = _dot3(c_ref[...], w_ref[...]) + b_ref[...]


def _ada(c8, w_ada, b_ada):
    n = w_ada.shape[1]
    return pl.pallas_call(
        _ada_kernel,
        grid=(n // D_MODEL,),
        in_specs=[pl.BlockSpec((SUBLANES, D_MODEL), lambda j: (0, 0)),
                  pl.BlockSpec((D_MODEL, D_MODEL), lambda j: (0, j)),
                  pl.BlockSpec((1, D_MODEL), lambda j: (0, j))],
        out_specs=pl.BlockSpec((SUBLANES, D_MODEL), lambda j: (0, j)),
        out_shape=jax.ShapeDtypeStruct((SUBLANES, n), F32),
        compiler_params=_cparams(("arbitrary",)),
        name="ada",
    )(c8, w_ada, b_ada)


INPROJ_TM = 1024
LN_ROWS = 128
LN_UNROLL = 4


def _modulated_ln_rows(x_ref, mod_ref, emit):
    gain = 1.0 + mod_ref[1:2, :]
    shift = mod_ref[0:1, :]

    def chunk(c, carry):
        r0 = pl.multiple_of(c * LN_ROWS, LN_ROWS)
        emit(r0, _ln(x_ref[pl.ds(r0, LN_ROWS), :]) * gain + shift)
        return carry

    lax.fori_loop(0, INPROJ_TM // LN_ROWS, chunk, 0, unroll=LN_UNROLL)


def _inproj_attn_kernel(x_ref, mod_ref, wq_ref, wk_ref, wv_ref, o_ref, un_ref, uf_scr, u_scr):
    j = pl.program_id(1)

    @pl.when(j == 0)
    def _():
        def emit(r0, u):
            for cc in range(D_MODEL // LANES):
                uf_scr[cc, pl.ds(r0, LN_ROWS), :] = u[:, cc * LANES:(cc + 1) * LANES]
            ub = u.astype(BF16)
            u_scr[0, pl.ds(r0, LN_ROWS), :] = ub
            un_ref[pl.ds(r0, LN_ROWS), :] = ub

        _modulated_ln_rows(x_ref, mod_ref, emit)
        for g in (1, 2):
            dil = ATTN_PATTERNS[g][1]
            n = PERM_TILE // dil
            for t0 in range(0, INPROJ_TM, PERM_TILE):
                for res in range(dil):
                    for cc in range(D_MODEL // LANES):
                        rows = uf_scr[cc, pl.ds(t0 + res, n, stride=dil), :]
                        u_scr[g, t0 + res * n:t0 + (res + 1) * n, cc * LANES:(cc + 1) * LANES] = rows.astype(BF16)

    u = u_scr[j]
    for s, w_ref in enumerate((wq_ref, wk_ref, wv_ref)):
        o_ref[:, s * GROUP_WIDTH:(s + 1) * GROUP_WIDTH] = _dot(u, w_ref[...]).astype(BF16)


ATTN_WIDTH = ATTN_HEADS * HEAD_DIM
W_RQ0 = 3 * ATTN_WIDTH
W_RK0 = W_RQ0 + RET_HEADS * RET_QK
W_RV0 = W_RK0 + RET_HEADS * RET_QK
W_TAIL0 = W_RV0 + RET_HEADS * RET_V


def _inproj_attn(x2, mod3, wb, seq):
    t = x2.shape[0]
    tn = 3 * GROUP_WIDTH
    per_seq = seq // INPROJ_TM
    groups_per_range = ATTN_WIDTH // GROUP_WIDTH
    wspec = lambda s: pl.BlockSpec((D_MODEL, GROUP_WIDTH), lambda i, j: (0, s * groups_per_range + j))
    return pl.pallas_call(
        _inproj_attn_kernel,
        grid=(t // INPROJ_TM, ATTN_GROUPS),
        in_specs=[pl.BlockSpec((INPROJ_TM, D_MODEL), lambda i, j: (i, 0)),
                  pl.BlockSpec((None, 6, D_MODEL), lambda i, j: (i // per_seq, 0, 0)),
                  wspec(0), wspec(1), wspec(2)],
        out_specs=[pl.BlockSpec((INPROJ_TM, tn), lambda i, j: (i, j)),
                   pl.BlockSpec((INPROJ_TM, D_MODEL), lambda i, j: (i, 0))],
        out_shape=[jax.ShapeDtypeStruct((t, ATTN_GROUPS * tn), BF16),
                   jax.ShapeDtypeStruct((t, D_MODEL), BF16)],
        scratch_shapes=[pltpu.VMEM((D_MODEL // LANES, INPROJ_TM, LANES), F32),
                        pltpu.VMEM((ATTN_GROUPS, INPROJ_TM, D_MODEL), BF16)],
        compiler_params=_cparams(("arbitrary", "arbitrary")),
        name="inproj_attn",
    )(x2, mod3, wb, wb, wb)


REST_TN = RET_HEAD_COLS = 2 * RET_QK + RET_V
REST_TAIL_TILES = (RET_HEADS * RET_V + 2 * D_MODEL) // REST_TN
REST_TM = 2048


def _inproj_rest_kernel(u_ref, wq_ref, wk_ref, wv_ref, wt0_ref, wt1_ref, o_ref):
    j = pl.program_id(0)

    def project(parts):
        u = u_ref[...]
        c0 = 0
        for w_ref in parts:
            wd = w_ref.shape[1]
            o_ref[:, c0:c0 + wd] = _dot(u, w_ref[...]).astype(BF16)
            c0 += wd

    @pl.when(j < RET_HEADS)
    def _():
        project((wq_ref, wk_ref, wv_ref))

    @pl.when(j >= RET_HEADS)
    def _():
        project((wt0_ref, wt1_ref))


def _inproj_rest(u, wb):
    t = u.shape[0]
    half = REST_TN // 2
    assert W_TAIL0 % half == 0 and W_RV0 % RET_V == 0
    head = lambda j: jnp.minimum(j, RET_HEADS - 1)
    tail = lambda j: jnp.maximum(j - RET_HEADS, 0)
    return pl.pallas_call(
        _inproj_rest_kernel,
        grid=(RET_HEADS + REST_TAIL_TILES, t // REST_TM),
        in_specs=[pl.BlockSpec((REST_TM, D_MODEL), lambda j, i: (i, 0)),
                  pl.BlockSpec((D_MODEL, RET_QK), lambda j, i: (0, W_RQ0 // RET_QK + head(j))),
                  pl.BlockSpec((D_MODEL, RET_QK), lambda j, i: (0, W_RK0 // RET_QK + head(j))),
                  pl.BlockSpec((D_MODEL, RET_V), lambda j, i: (0, W_RV0 // RET_V + head(j))),
                  pl.BlockSpec((D_MODEL, half), lambda j, i: (0, W_TAIL0 // half + 2 * tail(j))),
                  pl.BlockSpec((D_MODEL, half), lambda j, i: (0, W_TAIL0 // half + 2 * tail(j) + 1))],
        out_specs=pl.BlockSpec((REST_TM, REST_TN), lambda j, i: (i, j)),
        out_shape=jax.ShapeDtypeStruct((t, (RET_HEADS + REST_TAIL_TILES) * REST_TN), BF16),
        compiler_params=_cparams(("arbitrary", "arbitrary")),
        name="inproj_rest",
    )(u, wb, wb, wb, wb, wb)


ATTN_OFFSETS = ATTN_KB // ATTN_HALF - 1
ATTN_UNROLL = 16
ATTN_MIN_BLOCKS = 4


def _attn_kernel(q_ref, k_ref, v_ref, o_ref, lse_ref, bias_scr, *scratch, group, dil, nt, n, nres):
    n_sub = nt * n
    if nt == 1:
        seqs = [(q_ref.at[0, rr], k_ref.at[0, rr], v_ref.at[0, rr], o_ref.at[0, rr], lse_ref.at[0, rr])
                for rr in range(nres)]
    else:
        seqs = [tuple(s.at[rr] for s in scratch) for rr in range(nres)]
        for rr, (qs, ks, vs, _, _) in enumerate(seqs):
            for t in range(nt):
                qs[t * n:(t + 1) * n, :] = q_ref[t, rr]
                ks[t * n:(t + 1) * n, :] = k_ref[t, rr]
                vs[t * n:(t + 1) * n, :] = v_ref[t, rr]

    @pl.when((pl.program_id(0) == 0) & (pl.program_id(1) == 0))
    def _():
        base = (lax.broadcasted_iota(I32, (ATTN_QB, ATTN_KB), 1)
                - lax.broadcasted_iota(I32, (ATTN_QB, ATTN_KB), 0))
        for j in range(ATTN_OFFSETS):
            dist = jnp.abs(base - j * ATTN_HALF)
            for hh in range(HEADS_PER_GROUP):
                slope = dil * 2.0 ** (-ALIBI_MAX_EXP * (group * HEADS_PER_GROUP + hh + 1) / ATTN_HEADS)
                bias_scr[hh * ATTN_OFFSETS + j] = jnp.where(dist <= ATTN_HALF, -slope * dist.astype(F32), NEG)

    scale = HEAD_DIM ** -0.5
    lane = lax.broadcasted_iota(I32, (ATTN_QB, LANES), 1)

    def block(blk, carry):
        q0 = pl.multiple_of(blk * ATTN_QB, ATTN_QB)
        start = pl.multiple_of(jnp.clip(q0 - ATTN_HALF, 0, n_sub - ATTN_KB), ATTN_HALF)
        j = (q0 - start) // ATTN_HALF
        for qs, ks, vs, os_, ls in seqs:
            lse_tile = jnp.zeros((ATTN_QB, LANES), F32)
            for hh in range(HEADS_PER_GROUP):
                cs = slice(hh * HEAD_DIM, (hh + 1) * HEAD_DIM)
                qb = qs[pl.ds(q0, ATTN_QB), cs]
                kb = ks[pl.ds(start, ATTN_KB), cs]
                vb = vs[pl.ds(start, ATTN_KB), cs]
                s = lax.dot_general(qb, kb, (((1,), (1,)), ((), ())), preferred_element_type=F32) * scale
                s = s + bias_scr[hh * ATTN_OFFSETS + j]
                m = jnp.max(s, axis=-1, keepdims=True)
                p = jnp.exp(s - m)
                l = jnp.sum(p, axis=-1, keepdims=True)
                o = _dot(p.astype(BF16), vb) * (1.0 / l)
                os_[pl.ds(q0, ATTN_QB), cs] = o.astype(BF16)
                lse_tile = jnp.where(lane == hh, m + jnp.log(l), lse_tile)
            ls[pl.ds(q0, ATTN_QB), :] = lse_tile
        return carry

    n_blk = n_sub // ATTN_QB
    lax.fori_loop(0, n_blk, block, 0, unroll=min(max(ATTN_UNROLL // nres, 1), n_blk))

    if nt > 1:
        for rr, (_, _, _, os_, ls) in enumerate(seqs):
            for t in range(nt):
                o_ref[t, rr] = os_[t * n:(t + 1) * n, :]
                lse_ref[t, rr] = ls[t * n:(t + 1) * n, :]


def _attention(qkv, group, batch, seq):
    dil = ATTN_PATTERNS[group][1]
    if dil == 1:
        nt, n = 1, seq
    else:
        nt, n = seq // PERM_TILE, PERM_TILE // dil
    n_sub = nt * n
    t = batch * seq
    qkv5 = qkv.reshape(batch, nt, dil, n, qkv.shape[1])
    cb = group * 3
    nres = min(dil, max(1, ATTN_MIN_BLOCKS * ATTN_QB // n_sub))
    blk = (None, nt, nres, n, GROUP_WIDTH)
    scratch = [pltpu.VMEM((HEADS_PER_GROUP * ATTN_OFFSETS, ATTN_QB, ATTN_KB), F32)]
    if nt > 1:
        scratch += [pltpu.VMEM((nres, n_sub, GROUP_WIDTH), BF16)] * 4 + [pltpu.VMEM((nres, n_sub, LANES), F32)]
    out, lse = pl.pallas_call(
        functools.partial(_attn_kernel, group=group, dil=dil, nt=nt, n=n, nres=nres),
        grid=(batch, dil // nres),
        in_specs=[pl.BlockSpec(blk, lambda b, r: (b, 0, r, 0, cb)),
                  pl.BlockSpec(blk, lambda b, r: (b, 0, r, 0, cb + 1)),
                  pl.BlockSpec(blk, lambda b, r: (b, 0, r, 0, cb + 2))],
        out_specs=[pl.BlockSpec(blk, lambda b, r: (b, 0, r, 0, 0)),
                   pl.BlockSpec((None, nt, nres, n, LANES), lambda b, r: (b, 0, r, 0, 0))],
        out_shape=[jax.ShapeDtypeStruct((batch, nt, dil, n, GROUP_WIDTH), BF16),
                   jax.ShapeDtypeStruct((batch, nt, dil, n, LANES), F32)],
        scratch_shapes=scratch,
        compiler_params=_cparams(("arbitrary", "arbitrary")),
        name=f"attn_g{group}",
    )(qkv5, qkv5, qkv5)
    return out.reshape(t, GROUP_WIDTH), lse.reshape(t, LANES)


def _log_sigmoid(x):
    return jnp.minimum(x, 0.0) - jnp.log(1.0 + jnp.exp(-jnp.abs(x)))


def _retention_kernel(decay_ref, qkv_ref, g_ref, gain_ref, o_ref, state, ybwd, dmat, kdec, qdec, *, nc):
    c = RET_CHUNK
    h = pl.program_id(1)
    i = pl.program_id(2)
    kscale = RET_QK ** -0.5

    def set_decays(lg, forward):
        row = lax.broadcasted_iota(I32, (c, c), 0)
        col = lax.broadcasted_iota(I32, (c, c), 1)
        pos = lax.broadcasted_iota(I32, (c, LANES), 0).astype(F32)
        if forward:
            gap, key_pow, query_pow = row - col, (c - 1.0) - pos, pos + 1.0
            keep = gap >= 0
        else:
            gap, key_pow, query_pow = col - row, pos, c - pos
            keep = gap > 0
        dmat[...] = jnp.where(keep, jnp.exp(lg * jnp.maximum(gap, 0).astype(F32)) * kscale, 0.0)
        kdec[...] = jnp.exp(lg * key_pow) * kscale
        qdec[...] = jnp.exp(lg * query_pow)

    def chunk_step(lg, r0, finish):
        q = qkv_ref[r0:r0 + c, 0:RET_QK]
        k = qkv_ref[r0:r0 + c, RET_QK:2 * RET_QK]
        v = qkv_ref[r0:r0 + c, 2 * RET_QK:]
        inner = lax.dot_general(q, k, (((1,), (1,)), ((), ())), preferred_element_type=F32) * dmat[...]
        y = _dot(inner.astype(BF16), v)
        kd = (k.astype(F32) * jnp.concatenate([kdec[...]] * (RET_QK // LANES), axis=1)).astype(BF16)
        kv = lax.dot_general(kd, v, (((0,), (0,)), ((), ())), preferred_element_type=F32)
        yield
        qd = jnp.concatenate([qdec[...]] * (RET_V // LANES), axis=1)
        y = y + _dot(q, state[...].astype(BF16)) * qd
        state[...] = state[...] * jnp.exp(lg * float(c)) + kv
        yield
        finish(r0, y)

    @pl.when((i == 0) | (i == nc))
    def _():
        state[...] = jnp.zeros_like(state)

    @pl.when(i < nc)
    def _():
        lg = _log_sigmoid(jnp.zeros((1, 1), F32) + decay_ref[RET_HEADS + h])

        @pl.when(i == 0)
        def _():
            set_decays(lg, False)

        base = pl.multiple_of((nc - 1 - i) * RET_STEP, RET_STEP)

        def finish(r0, y):
            ybwd[pl.ds(base + r0, c), :] = y

        _interleave(chunk_step(lg, r0, finish) for r0 in reversed(range(0, RET_STEP, c)))

    @pl.when(i >= nc)
    def _():
        lg = _log_sigmoid(jnp.zeros((1, 1), F32) + decay_ref[h])

        @pl.when(i == nc)
        def _():
            set_decays(lg, True)

        base = pl.multiple_of((i - nc) * RET_STEP, RET_STEP)

        def finish(r0, y):
            y = y + ybwd[pl.ds(base + r0, c), :]
            g = g_ref[r0:r0 + c, :].astype(F32)
            o_ref[r0:r0 + c, :] = (g * _sigmoid(g) * (_ln(y) * gain_ref[...])).astype(BF16)

        _interleave(chunk_step(lg, r0, finish) for r0 in range(0, RET_STEP, c))


def _retention(rest, decays, gn_gain, batch, seq):
    c = RET_STEP
    nc = seq // c
    rest3 = rest.reshape(batch, seq, rest.shape[1])
    gate_blk = RET_HEADS * RET_HEAD_COLS // RET_V

    def chunk(i):
        return jnp.where(i < nc, nc - 1 - i, i - nc)

    grid_spec = pltpu.PrefetchScalarGridSpec(
        num_scalar_prefetch=1,
        grid=(batch, RET_HEADS, 2 * nc),
        in_specs=[pl.BlockSpec((None, c, RET_HEAD_COLS), lambda b, h, i, d: (b, chunk(i), h)),
                  pl.BlockSpec((None, c, RET_V), lambda b, h, i, d: (b, jnp.maximum(i - nc, 0), gate_blk + h)),
                  pl.BlockSpec((1, RET_V), lambda b, h, i, d: (0, h))],
        out_specs=pl.BlockSpec((None, c, RET_V), lambda b, h, i, d: (b, jnp.maximum(i - nc, 0), h)),
        scratch_shapes=[pltpu.VMEM((RET_QK, RET_V), F32), pltpu.VMEM((seq, RET_V), F32),
                        pltpu.VMEM((RET_CHUNK, RET_CHUNK), F32), pltpu.VMEM((RET_CHUNK, LANES), F32),
                        pltpu.VMEM((RET_CHUNK, LANES), F32)],
    )
    out = pl.pallas_call(
        functools.partial(_retention_kernel, nc=nc),
        grid_spec=grid_spec,
        out_shape=jax.ShapeDtypeStruct((batch, seq, RET_HEADS * RET_V), BF16),
        compiler_params=_cparams(("arbitrary", "arbitrary", "arbitrary")),
        name="retention",
    )(decays, rest3, rest3, gn_gain)
    return out.reshape(batch * seq, RET_HEADS * RET_V)


MERGE_TM = PERM_TILE
MERGE_SUB = 256
ZERO_SPLIT = 4


def _merge_kernel(o0_ref, o1_ref, o2_ref, l0_ref, l1_ref, l2_ref, retg_ref, ga_ref, gr_ref, x_ref, mod_ref,
                  wa_ref, wr_ref, wo_ref, ln_ref, wrh_ref, wrl_ref, rb_ref,
                  h1_ref, u2_ref, lg_ref, zrows_ref, on_scr, ln_scr, zbuf, zsem):
    i = pl.program_id(0)

    @pl.when(i == 0)
    def _():
        zbuf[...] = jnp.zeros_like(zbuf)

    zrows = zbuf.shape[0]
    zero_copies = [pltpu.make_async_copy(zbuf, zrows_ref.at[pl.ds((i * ZERO_SPLIT + k) * zrows, zrows)], zsem)
                   for k in range(ZERO_SPLIT)]
    for cp in zero_copies:
        cp.start()

    for g in (1, 2):
        dil = ATTN_PATTERNS[g][1]
        n = MERGE_TM // dil
        o_ref, l_ref = ((o1_ref, l1_ref), (o2_ref, l2_ref))[g - 1]
        for res in range(dil):
            rows = o_ref[res * n:(res + 1) * n, :].astype(F32)
            for hh in range(HEADS_PER_GROUP):
                on_scr[g - 1, hh, pl.ds(res, n, stride=dil), :] = rows[:, hh * HEAD_DIM:(hh + 1) * HEAD_DIM]
            ln_scr[g - 1, pl.ds(res, n, stride=dil), :] = l_ref[res * n:(res + 1) * n, :]

    def sub_tile(r0):
        rs = slice(r0, r0 + MERGE_SUB)
        l0, l1, l2 = l0_ref[rs, :], ln_scr[0, rs, :], ln_scr[1, rs, :]
        lm = jnp.maximum(jnp.maximum(l0, l1), l2)
        e0, e1, e2 = jnp.exp(l0 - lm), jnp.exp(l1 - lm), jnp.exp(l2 - lm)
        inv = 1.0 / (e0 + e1 + e2)
        parts = []
        for hh in range(HEADS_PER_GROUP):
            sl = slice(hh * HEAD_DIM, (hh + 1) * HEAD_DIM)
            acc = (e0[:, hh:hh + 1] * o0_ref[rs, sl].astype(F32)
                   + e1[:, hh:hh + 1] * on_scr[0, hh, rs, :]
                   + e2[:, hh:hh + 1] * on_scr[1, hh, rs, :])
            parts.append((acc * inv[:, hh:hh + 1]).astype(BF16))
        attn = jnp.concatenate(parts, axis=1)
        yield

        branch_a = _dot(attn, wa_ref[...])
        branch_r = _dot(retg_ref[rs, :], wr_ref[...])
        yield
        merged = (_sigmoid(ga_ref[rs, :].astype(F32)) * branch_a
                  + _sigmoid(gr_ref[rs, :].astype(F32)) * branch_r)
        yield
        y = _dot(merged.astype(BF16), wo_ref[...])
        yield

        h1 = _ln(DEEPNORM_ALPHA * x_ref[rs, :] + mod_ref[2:3, :] * y) * ln_ref[0:1, :] + ln_ref[1:2, :]
        h1_ref[rs, :] = h1
        u2 = _ln(h1) * (1.0 + mod_ref[4:5, :]) + mod_ref[3:4, :]
        _store_token_tiles(u2_ref, u2, r0)
        uh, ul = _split_bf16(u2)
        yield
        lg_ref[rs, :] = (_dot(uh, wrh_ref[...]) + _dot(uh, wrl_ref[...]) + _dot(ul, wrh_ref[...])
                         + rb_ref[...])

    _interleave(sub_tile(r0) for r0 in range(0, MERGE_TM, MERGE_SUB))

    for cp in zero_copies:
        cp.wait()


def _merge(outs, lses, retg, rest, x2, mod3, wa, wr, wo, ln1, wr_hi, wr_lo, rbias, seq, n_rows):
    t = x2.shape[0]
    tm = MERGE_TM
    per_seq = seq // tm
    zrows, rem = divmod(n_rows * ROW_TILE, (t // tm) * ZERO_SPLIT)
    assert rem == 0 and zrows % SUBLANES == 0
    row = lambda w: pl.BlockSpec((tm, w), lambda i: (i, 0))
    full = lambda a: pl.BlockSpec(a.shape, lambda i: (0,) * a.ndim)
    return pl.pallas_call(
        _merge_kernel,
        grid=(t // tm,),
        in_specs=[row(GROUP_WIDTH)] * 3 + [row(LANES)] * 3 + [
            row(RET_HEADS * RET_V),
            pl.BlockSpec((tm, D_MODEL), lambda i: (i, 6)),
            pl.BlockSpec((tm, D_MODEL), lambda i: (i, 7)),
            row(D_MODEL),
            pl.BlockSpec((None, 6, D_MODEL), lambda i: (i // per_seq, 0, 0)),
            full(wa), full(wr), full(wo), full(ln1), full(wr_hi), full(wr_lo), full(rbias)],
        out_specs=[row(D_MODEL), pl.BlockSpec((tm * ROW_TILE, LANES), lambda i: (i, 0)), row(LANES),
                   pl.BlockSpec(memory_space=pl.ANY)],
        out_shape=[jax.ShapeDtypeStruct((t, D_MODEL), F32),
                   jax.ShapeDtypeStruct((t * ROW_TILE, LANES), F32),
                   jax.ShapeDtypeStruct((t, LANES), F32),
                   jax.ShapeDtypeStruct((n_rows * ROW_TILE, LANES), F32)],
        scratch_shapes=[pltpu.VMEM((2, HEADS_PER_GROUP, tm, HEAD_DIM), F32), pltpu.VMEM((2, tm, LANES), F32),
                        pltpu.VMEM((zrows, LANES), F32), pltpu.SemaphoreType.DMA(())],
        compiler_params=_cparams(("arbitrary",)),
        name="merge",
    )(*outs, *lses, retg, rest, rest, x2, mod3, wa, wr, wo, ln1, wr_hi, wr_lo, rbias)


ROUTE_TM = 512
BIG = 1 << 20


def _route_kernel(lg_ref, cols_ref, ints_ref, cnt_ref, carry, tri):
    i = pl.program_id(0)
    tm = ROUTE_TM

    @pl.when(i == 0)
    def _():
        carry[...] = jnp.zeros_like(carry)
        r_i = lax.broadcasted_iota(I32, (tm, tm), 0)
        c_i = lax.broadcasted_iota(I32, (tm, tm), 1)
        tri[...] = jnp.where(r_i > c_i, 1.0, 0.0).astype(BF16)

    lg = lg_ref[...]
    lane = lax.broadcasted_iota(I32, (tm, LANES), 1)
    lane_f = lane.astype(F32)
    first = lambda mask: jnp.min(jnp.where(mask, lane_f, float(BIG)), axis=-1, keepdims=True).astype(I32)

    coarse = jnp.where(lane < N_GROUPS, lg, NEG)
    cmax = jnp.max(coarse, axis=-1, keepdims=True)
    gsel = first(coarse == cmax)
    p_group = 1.0 / jnp.sum(jnp.exp(coarse - cmax), axis=-1, keepdims=True)

    lo = N_GROUPS + EXPERTS_PER_GROUP * gsel
    fine = jnp.where((lane >= lo) & (lane < lo + EXPERTS_PER_GROUP), lg, NEG)
    v1 = jnp.max(fine, axis=-1, keepdims=True)
    i1 = first(fine == v1)
    fine2 = jnp.where(lane == i1, NEG, fine)
    v2 = jnp.max(fine2, axis=-1, keepdims=True)
    i2 = first(fine2 == v2)
    ex = jnp.exp(v2 - v1)
    den = 1.0 / (1.0 + ex)
    gate1 = p_group * den
    gate2 = p_group * (ex * den)
    e1 = i1 - N_GROUPS
    e2 = i2 - N_GROUPS

    oh1 = lane == e1
    oh2 = lane == e2
    cnt = jnp.where(oh1 | oh2, 1.0, 0.0)
    rank = _dot(tri[...], cnt.astype(BF16)) + carry[...]
    r1 = jnp.sum(jnp.where(oh1, rank, 0.0), axis=-1, keepdims=True)
    r2 = jnp.sum(jnp.where(oh2, rank, 0.0), axis=-1, keepdims=True)
    carry[...] = carry[...] + jnp.sum(cnt, axis=0, keepdims=True)
    cnt_ref[...] = jnp.broadcast_to(carry[...], cnt_ref.shape)

    cols_ref[...] = jnp.where(lane == 0, gate1, jnp.where(lane == 1, gate2, 0.0))
    packed = jnp.where(lane == 0, e1.astype(F32),
                       jnp.where(lane == 1, e2.astype(F32),
                                 jnp.where(lane == 2, r1, jnp.where(lane == 3, r2, 0.0))))
    ints_ref[...] = packed.T[0:SUBLANES, :].astype(I32)


def _route(logits):
    t = logits.shape[0]
    tm = ROUTE_TM
    return pl.pallas_call(
        _route_kernel,
        grid=(t // tm,),
        in_specs=[pl.BlockSpec((tm, LANES), lambda i: (i, 0))],
        out_specs=[pl.BlockSpec((tm, LANES), lambda i: (i, 0)),
                   pl.BlockSpec((SUBLANES, tm), lambda i: (0, i)),
                   pl.BlockSpec((SUBLANES, LANES), lambda i: (0, 0))],
        out_shape=[jax.ShapeDtypeStruct((t, LANES), F32),
                   jax.ShapeDtypeStruct((SUBLANES, t), I32),
                   jax.ShapeDtypeStruct((SUBLANES, LANES), F32)],
        scratch_shapes=[pltpu.VMEM((1, LANES), F32), pltpu.VMEM((tm, tm), BF16)],
        compiler_params=_cparams(("arbitrary",)),
        name="route",
    )(logits)


def _plan_kernel(ints_ref, cnt_ref, dest_ref, meta_ref, *, n_blocks_pad):
    sub = lax.broadcasted_iota(I32, (LANES, LANES), 0)
    lane = lax.broadcasted_iota(I32, (LANES, LANES), 1)
    cnt = cnt_ref[0:1, :]
    nblk_row = jnp.floor((cnt + (MOE_BLK - 1.0)) * (1.0 / MOE_BLK))
    nblk_mat = jnp.broadcast_to(nblk_row, (LANES, LANES))
    start_col = jnp.sum(jnp.where(lane < sub, nblk_mat, 0.0), axis=-1, keepdims=True)
    nblk_col = jnp.sum(jnp.where(lane == sub, nblk_mat, 0.0), axis=-1, keepdims=True)
    end_col = start_col + nblk_col

    ints = ints_ref[...]
    base = jnp.zeros(ints.shape, F32)
    for e in range(N_EXPERTS):
        base = jnp.where(ints == e, start_col[e:e + 1, :] * float(MOE_BLK), base)
    dest = base[0:2, :].astype(I32) + ints[2:4, :]
    dest_ref[...] = jnp.concatenate([dest, jnp.zeros((SUBLANES - 2, ints.shape[1]), I32)], axis=0)

    blk = lax.broadcasted_iota(I32, (LANES, n_blocks_pad), 1).astype(F32)
    e_sub = lax.broadcasted_iota(I32, (LANES, n_blocks_pad), 0)
    done = jnp.where((e_sub < N_EXPERTS) & (end_col <= blk), 1.0, 0.0)
    bexp = jnp.minimum(jnp.sum(done, axis=0, keepdims=True), N_EXPERTS - 1.0)
    used = jnp.sum(nblk_row, axis=-1, keepdims=True)
    row = lax.broadcasted_iota(I32, (SUBLANES, n_blocks_pad), 0)
    meta = jnp.where(row == 0, bexp, jnp.where(row == 1, used, 0.0))
    meta_ref[...] = meta.astype(I32)


def _plan(ints, counts, n_blocks_pad):
    t = ints.shape[1]
    return pl.pallas_call(
        functools.partial(_plan_kernel, n_blocks_pad=n_blocks_pad),
        out_shape=[jax.ShapeDtypeStruct((SUBLANES, t), I32), jax.ShapeDtypeStruct((SUBLANES, n_blocks_pad), I32)],
        compiler_params=pltpu.CompilerParams(vmem_limit_bytes=VMEM_LIMIT),
        name="plan",
    )(ints, counts)


DISPATCH_TM = 2048
DMA_UNROLL = 16


def _row_tile(ref, r):
    return ref.at[pl.ds(pl.multiple_of(r * ROW_TILE, ROW_TILE), ROW_TILE)]


def _dispatch_kernel(d0_ref, d1_ref, u2_ref, rows_in_ref, rows_ref, sem):
    del rows_in_ref

    def issue(t, carry):
        src = _row_tile(u2_ref, t)
        pltpu.make_async_copy(src, _row_tile(rows_ref, d0_ref[t]), sem).start(priority=0)
        pltpu.make_async_copy(src, _row_tile(rows_ref, d1_ref[t]), sem).start(priority=1)
        return carry

    lax.fori_loop(0, DISPATCH_TM, issue, 0, unroll=DMA_UNROLL)
    for _ in range(2):
        pltpu.make_async_copy(u2_ref, rows_ref.at[pl.ds(0, DISPATCH_TM * ROW_TILE)], sem).wait()


def _dispatch(dest0, dest1, u2, rows0):
    t = u2.shape[0] // ROW_TILE
    idx = pl.BlockSpec((DISPATCH_TM,), lambda i: (i,), memory_space=pltpu.SMEM)
    return pl.pallas_call(
        _dispatch_kernel,
        grid=(t // DISPATCH_TM,),
        in_specs=[idx, idx,
                  pl.BlockSpec((DISPATCH_TM * ROW_TILE, LANES), lambda i: (i, 0)),
                  pl.BlockSpec(memory_space=pl.ANY)],
        out_specs=pl.BlockSpec(memory_space=pl.ANY),
        out_shape=jax.ShapeDtypeStruct(rows0.shape, F32),
        scratch_shapes=[pltpu.SemaphoreType.DMA(())],
        input_output_aliases={3: 0},
        compiler_params=_cparams(("arbitrary",)),
        name="dispatch",
    )(dest0, dest1, u2, rows0)


def _expert_runs(bexp, used):
    n = bexp.shape[0]
    idx = jnp.arange(n, dtype=I32)
    first = (idx < used[0]) & ((idx == 0) | (bexp != jnp.roll(bexp, 1)))
    slot = (jnp.cumsum(first.astype(I32)) - 1) % 2
    first_at_or_after = lax.cummin(jnp.where(first, idx, n)[::-1])[::-1]
    first_after = jnp.concatenate([first_at_or_after[1:], jnp.full((1,), n, I32)])
    nxt = jnp.where(first_after < n, bexp[jnp.minimum(first_after, n - 1)], -1)
    return first.astype(I32), slot.astype(I32), nxt.astype(I32)


def _experts_kernel(bexp_ref, used_ref, first_ref, slot_ref, nxt_ref, x_ref, w1_hbm, w3_hbm, w2_hbm, y_ref,
                    wb1, wb3, wb2, w1s, w3s, w2s, sem):
    i = pl.program_id(0)
    active = i < used_ref[0]
    slot = slot_ref[i]

    def fetch(e, s):
        return [pltpu.make_async_copy(w.at[e], buf.at[s], sem.at[s])
                for w, buf in ((w1_hbm, wb1), (w3_hbm, wb3), (w2_hbm, wb2))]

    @pl.when(i == 0)
    def _():
        for cp in fetch(bexp_ref[0], 0):
            cp.start()

    @pl.when(first_ref[i] == 1)
    def _():
        @pl.when(nxt_ref[i] >= 0)
        def _():
            for cp in fetch(nxt_ref[i], 1 - slot):
                cp.start()

        for cp in fetch(bexp_ref[i], slot):
            cp.wait()
        w1s[...] = wb1[slot].astype(BF16)
        w3s[...] = wb3[slot].astype(BF16)
        w2s[...] = wb2[slot].astype(BF16)

    @pl.when(active)
    def _():
        def sub_block(r0):
            xb = _load_token_tiles(x_ref, MOE_SUB, r0).astype(BF16)
            yield
            a = _dot(xb, w1s[...])
            b = _dot(xb, w3s[...])
            yield
            hdn = (a * _sigmoid(a) * b).astype(BF16)
            yield
            _store_token_tiles(y_ref, _dot(hdn, w2s[...]), r0)

        _interleave(sub_block(r0) for r0 in range(0, MOE_BLK, MOE_SUB))


def _experts(bexp, used, rows, w1, w3, w2):
    n_blocks = rows.shape[0] // (MOE_BLK * ROW_TILE)
    first, slot, nxt = _expert_runs(bexp, used)
    any_space = pl.BlockSpec(memory_space=pl.ANY)
    row_blk = pl.BlockSpec((MOE_BLK * ROW_TILE, LANES), lambda i, be, nu, *_: (jnp.minimum(i, nu[0] - 1), 0))
    grid_spec = pltpu.PrefetchScalarGridSpec(
        num_scalar_prefetch=5,
        grid=(n_blocks,),
        in_specs=[row_blk, any_space, any_space, any_space],
        out_specs=row_blk,
        scratch_shapes=[pltpu.VMEM((2, D_MODEL, EXPERT_FF), F32), pltpu.VMEM((2, D_MODEL, EXPERT_FF), F32),
                        pltpu.VMEM((2, EXPERT_FF, D_MODEL), F32),
                        pltpu.VMEM((D_MODEL, EXPERT_FF), BF16), pltpu.VMEM((D_MODEL, EXPERT_FF), BF16),
                        pltpu.VMEM((EXPERT_FF, D_MODEL), BF16), pltpu.SemaphoreType.DMA((2,))],
    )
    return pl.pallas_call(
        _experts_kernel,
        grid_spec=grid_spec,
        out_shape=jax.ShapeDtypeStruct(rows.shape, F32),
        input_output_aliases={5: 0},
        compiler_params=_cparams(("arbitrary",)),
        name="experts",
    )(bexp, used, first, slot, nxt, rows, w1, w3, w2)


COMBINE_TM = 256


COMBINE_AHEAD = 2


def _combine_kernel(d0_ref, d1_ref, p0_ref, p1_ref, n0_ref, n1_ref, y_ref, cols_ref, h1_ref, mod_ref, ln_ref,
                    o_ref, *scratch):
    n_pairs = COMBINE_AHEAD + 1
    pairs = [(scratch[2 * k], scratch[2 * k + 1]) for k in range(n_pairs)]
    sem = scratch[2 * n_pairs]
    i = pl.program_id(0)
    n_steps = pl.num_programs(0)

    def copies(i0_ref, i1_ref, t, k):
        a, b = pairs[k]
        return (pltpu.make_async_copy(_row_tile(y_ref, i0_ref[t]), _row_tile(a, t), sem.at[k]),
                pltpu.make_async_copy(_row_tile(y_ref, i1_ref[t]), _row_tile(b, t), sem.at[k]))

    def start_tile(i0_ref, i1_ref, t, k):
        for prio, cp in enumerate(copies(i0_ref, i1_ref, t, k)):
            cp.start(priority=prio)

    @pl.when(i == 0)
    def _():
        for k, (i0_ref, i1_ref) in enumerate(((d0_ref, d1_ref), (p0_ref, p1_ref))):
            def issue(t, carry):
                start_tile(i0_ref, i1_ref, t, k)
                return carry

            lax.fori_loop(0, COMBINE_TM, issue, 0, unroll=DMA_UNROLL)

    def step(k, prefetch):
        a, b = pairs[k]
        for buf in (a, b):
            pltpu.make_async_copy(y_ref.at[pl.ds(0, COMBINE_TM * ROW_TILE)], buf, sem.at[k]).wait()
        if prefetch:
            for t in range(COMBINE_TM):
                start_tile(n0_ref, n1_ref, t, (k + COMBINE_AHEAD) % n_pairs)
        cols = cols_ref[...]
        moe = cols[:, 0:1] * _load_token_tiles(a, COMBINE_TM) + cols[:, 1:2] * _load_token_tiles(b, COMBINE_TM)
        pre = DEEPNORM_ALPHA * h1_ref[...] + mod_ref[5:6, :] * moe
        o_ref[...] = _ln(pre) * ln_ref[0:1, :] + ln_ref[1:2, :]

    for k in range(n_pairs):
        @pl.when((i % n_pairs == k) & (i + COMBINE_AHEAD < n_steps))
        def _():
            step(k, True)

        @pl.when((i % n_pairs == k) & (i + COMBINE_AHEAD >= n_steps))
        def _():
            step(k, False)


def _combine(dest0, dest1, y_rows, cols, h1, mod3, ln2, seq):
    t = h1.shape[0]
    tm = COMBINE_TM
    per_seq = seq // tm
    assert COMBINE_AHEAD == 2 and t // tm >= COMBINE_AHEAD
    ahead = lambda k: pl.BlockSpec((tm,), lambda i: (jnp.minimum(i + k, t // tm - 1),), memory_space=pltpu.SMEM)
    return pl.pallas_call(
        _combine_kernel,
        grid=(t // tm,),
        in_specs=[ahead(0), ahead(0), ahead(1), ahead(1), ahead(COMBINE_AHEAD), ahead(COMBINE_AHEAD),
                  pl.BlockSpec(memory_space=pl.ANY),
                  pl.BlockSpec((tm, LANES), lambda i: (i, 0)),
                  pl.BlockSpec((tm, D_MODEL), lambda i: (i, 0)),
                  pl.BlockSpec((None, 6, D_MODEL), lambda i: (i // per_seq, 0, 0)),
                  pl.BlockSpec((2, D_MODEL), lambda i: (0, 0))],
        out_specs=pl.BlockSpec((tm, D_MODEL), lambda i: (i, 0)),
        out_shape=jax.ShapeDtypeStruct((t, D_MODEL), F32),
        scratch_shapes=([pltpu.VMEM((tm * ROW_TILE, LANES), F32)] * (2 * (COMBINE_AHEAD + 1))
                        + [pltpu.SemaphoreType.DMA((COMBINE_AHEAD + 1,))]),
        compiler_params=_cparams(("arbitrary",)),
        name="combine",
    )(dest0, dest1, dest0, dest1, dest0, dest1, y_rows, cols, h1, mod3, ln2)


def _layer(h, c8, w_ada, b_ada, w_in, w_attn_out, decay_f, decay_b, gn_gain, w_ret_out, w_out,
           ln1_gain, ln1_bias, w_coarse, b_coarse, w_fine, b_fine, w1, w3, w2, ln2_gain, ln2_bias):
    batch, seq, d = h.shape
    t = batch * seq
    x2 = h.reshape(t, d)

    mod = _ada(c8, w_ada, b_ada.reshape(1, -1))
    mod3 = mod[:batch].reshape(batch, 6, d)

    wb = w_in.astype(BF16)
    qkv, u1 = _inproj_attn(x2, mod3, wb, seq)
    rest = _inproj_rest(u1, wb)

    outs, lses = zip(*[_attention(qkv, g, batch, seq) for g in range(ATTN_GROUPS)])
    decays = jnp.concatenate([decay_f, decay_b]).astype(F32)
    retg = _retention(rest, decays, gn_gain.reshape(1, -1), batch, seq)

    w_route = jnp.concatenate([w_coarse, w_fine.transpose(1, 0, 2).reshape(d, N_EXPERTS)], axis=1)
    n_route = w_route.shape[1]
    w_route = jnp.pad(w_route, ((0, 0), (0, LANES - n_route)))
    wr_hi, wr_lo = _split_bf16(w_route)
    rbias = jnp.pad(jnp.concatenate([b_coarse, b_fine.reshape(-1)]), (0, LANES - n_route)).reshape(1, LANES)
    ln1 = jnp.stack([ln1_gain, ln1_bias])
    n_blocks = 2 * t // MOE_BLK + N_EXPERTS
    h1, u2, logits, rows0 = _merge(outs, lses, retg, rest, x2, mod3, w_attn_out.astype(BF16),
                                   w_ret_out.astype(BF16), w_out.astype(BF16), ln1, wr_hi, wr_lo, rbias, seq,
                                   n_blocks * MOE_BLK)

    cols, ints, counts = _route(logits)
    n_blocks_pad = -(-n_blocks // LANES) * LANES
    dest, meta = _plan(ints, counts, n_blocks_pad)
    dest0, dest1 = dest[0], dest[1]
    rows = _dispatch(dest0, dest1, u2, rows0)
    y_rows = _experts(meta[0, :n_blocks], meta[1, :1], rows, w1, w3, w2)
    out = _combine(dest0, dest1, y_rows, cols, h1, mod3, jnp.stack([ln2_gain, ln2_bias]), seq)
    return out.reshape(batch, seq, d)


def kernel(x, c, w_ada, b_ada, w_in, w_attn_out, ret_decay_fwd, ret_decay_bwd, ret_gn_gain, w_ret_out, w_out,
           ln1_gain, ln1_bias, w_coarse, b_coarse, w_fine, b_fine, w1, w3, w2, ln2_gain, ln2_bias):
    batch = x.shape[0]
    max_dil = max(dil for _, dil in ATTN_PATTERNS)
    seq_tile = max(max_dil * ATTN_KB, RET_STEP, REST_TM, DISPATCH_TM)
    assert batch <= SUBLANES and x.shape[1] % seq_tile == 0 and x.shape[2] == D_MODEL
    c8 = jnp.pad(c, ((0, SUBLANES - batch), (0, 0)))
    h = x
    for l in range(w_ada.shape[0]):
        h = _layer(h, c8, w_ada[l], b_ada[l], w_in[l], w_attn_out[l], ret_decay_fwd[l], ret_decay_bwd[l],
                   ret_gn_gain[l], w_ret_out[l], w_out[l], ln1_gain[l], ln1_bias[l], w_coarse[l], b_coarse[l],
                   w_fine[l], b_fine[l], w1[l], w3[l], w2[l], ln2_gain[l], ln2_bias[l])
    return h
```

```python
import functools

import jax
import jax.numpy as jnp
from jax import lax
from jax.experimental import pallas as pl
from jax.experimental.pallas import tpu as pltpu

F32 = jnp.float32
BF16 = jnp.bfloat16
I32 = jnp.int32

D_MODEL = 1024
ATTN_GROUPS = 3
HEADS_PER_GROUP = 4
HEAD_DIM = 128
ATTN_HEADS = ATTN_GROUPS * HEADS_PER_GROUP
GROUP_WIDTH = HEADS_PER_GROUP * HEAD_DIM
ATTN_PATTERNS = ((128, 1), (512, 4), (2048, 16))
ALIBI_MAX_EXP = 8.0
NEG = -1e30
RET_HEADS = 4
RET_QK = 256
RET_V = 512
N_GROUPS = 4
EXPERTS_PER_GROUP = 8
N_EXPERTS = N_GROUPS * EXPERTS_PER_GROUP
EXPERT_FF = 512
DEPTH = 1
DEEPNORM_ALPHA = (2.0 * DEPTH) ** 0.25
LN_EPS = 1e-5

LANES = 128
SUBLANES = 8
PERM_TILE = 512
ATTN_QB = 128
ATTN_HALF = 64
ATTN_KB = ATTN_QB + 2 * ATTN_HALF
RET_CHUNK = 256
RET_STEP = 2048
MOE_BLK = 512
MOE_SUB = 256
VMEM_LIMIT = 56 * 1024 * 1024


def _cparams(sem):
    return pltpu.CompilerParams(dimension_semantics=sem, vmem_limit_bytes=VMEM_LIMIT)


def _split_bf16(a):
    hi = a.astype(BF16)
    lo = (a - hi.astype(F32)).astype(BF16)
    return hi, lo


def _dot(a, b):
    return jnp.dot(a, b, preferred_element_type=F32)


def _dot3(a, b):
    ah, al = _split_bf16(a)
    bh, bl = _split_bf16(b)
    return _dot(ah, bh) + _dot(ah, bl) + _dot(al, bh)


def _ln(x):
    mu = jnp.mean(x, axis=-1, keepdims=True)
    xc = x - mu
    var = jnp.mean(xc * xc, axis=-1, keepdims=True)
    return xc * lax.rsqrt(var + LN_EPS)


def _sigmoid(x):
    return 1.0 / (1.0 + jnp.exp(-x))


ROW_TILE = D_MODEL // LANES


def _store_token_tiles(ref, val, row0=0):
    n = val.shape[0]
    for cc in range(ROW_TILE):
        ref[pl.ds(row0 * ROW_TILE + cc, n, stride=ROW_TILE), :] = val[:, cc * LANES:(cc + 1) * LANES]


def _load_token_tiles(ref, n, row0=0):
    return jnp.concatenate([ref[pl.ds(row0 * ROW_TILE + cc, n, stride=ROW_TILE), :] for cc in range(ROW_TILE)],
                           axis=1)


def _interleave(phased):
    pending = []
    for gen in phased:
        pending.append(gen)
        pending = [g for g in pending if next(g, StopIteration) is not StopIteration]
    while pending:
        pending = [g for g in pending if next(g, StopIteration) is not StopIteration]


def _ada_kernel(c_ref, w_ref, b_ref, o_ref):
    o_ref[...] = _dot3(c_ref[...], w_ref[...]) + b_ref[...]


def _ada(c8, w_ada, b_ada):
    n = w_ada.shape[1]
    return pl.pallas_call(
        _ada_kernel,
        grid=(n // D_MODEL,),
        in_specs=[pl.BlockSpec((SUBLANES, D_MODEL), lambda j: (0, 0)),
                  pl.BlockSpec((D_MODEL, D_MODEL), lambda j: (0, j)),
                  pl.BlockSpec((1, D_MODEL), lambda j: (0, j))],
        out_specs=pl.BlockSpec((SUBLANES, D_MODEL), lambda j: (0, j)),
        out_shape=jax.ShapeDtypeStruct((SUBLANES, n), F32),
        compiler_params=_cparams(("arbitrary",)),
        name="ada",
    )(c8, w_ada, b_ada)


INPROJ_TM = 1024
LN_ROWS = 128
LN_UNROLL = 4


def _modulated_ln_rows(x_ref, mod_ref, emit):
    gain = 1.0 + mod_ref[1:2, :]
    shift = mod_ref[0:1, :]

    def chunk(c, carry):
        r0 = pl.multiple_of(c * LN_ROWS, LN_ROWS)
        emit(r0, _ln(x_ref[pl.ds(r0, LN_ROWS), :]) * gain + shift)
        return carry

    lax.fori_loop(0, INPROJ_TM // LN_ROWS, chunk, 0, unroll=LN_UNROLL)


def _inproj_attn_kernel(x_ref, mod_ref, wq_ref, wk_ref, wv_ref, o_ref, un_ref, uf_scr, u_scr):
    j = pl.program_id(1)

    @pl.when(j == 0)
    def _():
        def emit(r0, u):
            for cc in range(D_MODEL // LANES):
                uf_scr[cc, pl.ds(r0, LN_ROWS), :] = u[:, cc * LANES:(cc + 1) * LANES]
            ub = u.astype(BF16)
            u_scr[0, pl.ds(r0, LN_ROWS), :] = ub
            un_ref[pl.ds(r0, LN_ROWS), :] = ub

        _modulated_ln_rows(x_ref, mod_ref, emit)
        for g in (1, 2):
            dil = ATTN_PATTERNS[g][1]
            n = PERM_TILE // dil
            for t0 in range(0, INPROJ_TM, PERM_TILE):
                for res in range(dil):
                    for cc in range(D_MODEL // LANES):
                        rows = uf_scr[cc, pl.ds(t0 + res, n, stride=dil), :]
                        u_scr[g, t0 + res * n:t0 + (res + 1) * n, cc * LANES:(cc + 1) * LANES] = rows.astype(BF16)

    u = u_scr[j]
    for s, w_ref in enumerate((wq_ref, wk_ref, wv_ref)):
        o_ref[:, s * GROUP_WIDTH:(s + 1) * GROUP_WIDTH] = _dot(u, w_ref[...]).astype(BF16)


ATTN_WIDTH = ATTN_HEADS * HEAD_DIM
W_RQ0 = 3 * ATTN_WIDTH
W_RK0 = W_RQ0 + RET_HEADS * RET_QK
W_RV0 = W_RK0 + RET_HEADS * RET_QK
W_TAIL0 = W_RV0 + RET_HEADS * RET_V


def _inproj_attn(x2, mod3, wb, seq):
    t = x2.shape[0]
    tn = 3 * GROUP_WIDTH
    per_seq = seq // INPROJ_TM
    groups_per_range = ATTN_WIDTH // GROUP_WIDTH
    wspec = lambda s: pl.BlockSpec((D_MODEL, GROUP_WIDTH), lambda i, j: (0, s * groups_per_range + j))
    return pl.pallas_call(
        _inproj_attn_kernel,
        grid=(t // INPROJ_TM, ATTN_GROUPS),
        in_specs=[pl.BlockSpec((INPROJ_TM, D_MODEL), lambda i, j: (i, 0)),
                  pl.BlockSpec((None, 6, D_MODEL), lambda i, j: (i // per_seq, 0, 0)),
                  wspec(0), wspec(1), wspec(2)],
        out_specs=[pl.BlockSpec((INPROJ_TM, tn), lambda i, j: (i, j)),
                   pl.BlockSpec((INPROJ_TM, D_MODEL), lambda i, j: (i, 0))],
        out_shape=[jax.ShapeDtypeStruct((t, ATTN_GROUPS * tn), BF16),
                   jax.ShapeDtypeStruct((t, D_MODEL), BF16)],
        scratch_shapes=[pltpu.VMEM((D_MODEL // LANES, INPROJ_TM, LANES), F32),
                        pltpu.VMEM((ATTN_GROUPS, INPROJ_TM, D_MODEL), BF16)],
        compiler_params=_cparams(("arbitrary", "arbitrary")),
        name="inproj_attn",
    )(x2, mod3, wb, wb, wb)


REST_TN = RET_HEAD_COLS = 2 * RET_QK + RET_V
REST_TAIL_TILES = (RET_HEADS * RET_V + 2 * D_MODEL) // REST_TN
REST_TM = 2048


def _inproj_rest_kernel(u_ref, wq_ref, wk_ref, wv_ref, wt0_ref, wt1_ref, o_ref):
    j = pl.program_id(0)

    def project(parts):
        u = u_ref[...]
        c0 = 0
        for w_ref in parts:
            wd = w_ref.shape[1]
            o_ref[:, c0:c0 + wd] = _dot(u, w_ref[...]).astype(BF16)
            c0 += wd

    @pl.when(j < RET_HEADS)
    def _():
        project((wq_ref, wk_ref, wv_ref))

    @pl.when(j >= RET_HEADS)
    def _():
        project((wt0_ref, wt1_ref))


def _inproj_rest(u, wb):
    t = u.shape[0]
    half = REST_TN // 2
    assert W_TAIL0 % half == 0 and W_RV0 % RET_V == 0
    head = lambda j: jnp.minimum(j, RET_HEADS - 1)
    tail = lambda j: jnp.maximum(j - RET_HEADS, 0)
    return pl.pallas_call(
        _inproj_rest_kernel,
        grid=(RET_HEADS + REST_TAIL_TILES, t // REST_TM),
        in_specs=[pl.BlockSpec((REST_TM, D_MODEL), lambda j, i: (i, 0)),
                  pl.BlockSpec((D_MODEL, RET_QK), lambda j, i: (0, W_RQ0 // RET_QK + head(j))),
                  pl.BlockSpec((D_MODEL, RET_QK), lambda j, i: (0, W_RK0 // RET_QK + head(j))),
                  pl.BlockSpec((D_MODEL, RET_V), lambda j, i: (0, W_RV0 // RET_V + head(j))),
                  pl.BlockSpec((D_MODEL, half), lambda j, i: (0, W_TAIL0 // half + 2 * tail(j))),
                  pl.BlockSpec((D_MODEL, half), lambda j, i: (0, W_TAIL0 // half + 2 * tail(j) + 1))],
        out_specs=pl.BlockSpec((REST_TM, REST_TN), lambda j, i: (i, j)),
        out_shape=jax.ShapeDtypeStruct((t, (RET_HEADS + REST_TAIL_TILES) * REST_TN), BF16),
        compiler_params=_cparams(("arbitrary", "arbitrary")),
        name="inproj_rest",
    )(u, wb, wb, wb, wb, wb)


ATTN_OFFSETS = ATTN_KB // ATTN_HALF - 1
ATTN_UNROLL = 16
ATTN_MIN_BLOCKS = 4


def _attn_kernel(q_ref, k_ref, v_ref, o_ref, lse_ref, bias_scr, *scratch, group, dil, nt, n, nres):
    n_sub = nt * n
    if nt == 1:
        seqs = [(q_ref.at[0, rr], k_ref.at[0, rr], v_ref.at[0, rr], o_ref.at[0, rr], lse_ref.at[0, rr])
                for rr in range(nres)]
    else:
        seqs = [tuple(s.at[rr] for s in scratch) for rr in range(nres)]
        for rr, (qs, ks, vs, _, _) in enumerate(seqs):
            for t in range(nt):
                qs[t * n:(t + 1) * n, :] = q_ref[t, rr]
                ks[t * n:(t + 1) * n, :] = k_ref[t, rr]
                vs[t * n:(t + 1) * n, :] = v_ref[t, rr]

    @pl.when((pl.program_id(0) == 0) & (pl.program_id(1) == 0))
    def _():
        base = (lax.broadcasted_iota(I32, (ATTN_QB, ATTN_KB), 1)
                - lax.broadcasted_iota(I32, (ATTN_QB, ATTN_KB), 0))
        for j in range(ATTN_OFFSETS):
            dist = jnp.abs(base - j * ATTN_HALF)
            for hh in range(HEADS_PER_GROUP):
                slope = dil * 2.0 ** (-ALIBI_MAX_EXP * (group * HEADS_PER_GROUP + hh + 1) / ATTN_HEADS)
                bias_scr[hh * ATTN_OFFSETS + j] = jnp.where(dist <= ATTN_HALF, -slope * dist.astype(F32), NEG)

    scale = HEAD_DIM ** -0.5
    lane = lax.broadcasted_iota(I32, (ATTN_QB, LANES), 1)

    def block(blk, carry):
        q0 = pl.multiple_of(blk * ATTN_QB, ATTN_QB)
        start = pl.multiple_of(jnp.clip(q0 - ATTN_HALF, 0, n_sub - ATTN_KB), ATTN_HALF)
        j = (q0 - start) // ATTN_HALF
        for qs, ks, vs, os_, ls in seqs:
            lse_tile = jnp.zeros((ATTN_QB, LANES), F32)
            for hh in range(HEADS_PER_GROUP):
                cs = slice(hh * HEAD_DIM, (hh + 1) * HEAD_DIM)
                qb = qs[pl.ds(q0, ATTN_QB), cs]
                kb = ks[pl.ds(start, ATTN_KB), cs]
                vb = vs[pl.ds(start, ATTN_KB), cs]
                s = lax.dot_general(qb, kb, (((1,), (1,)), ((), ())), preferred_element_type=F32) * scale
                s = s + bias_scr[hh * ATTN_OFFSETS + j]
                m = jnp.max(s, axis=-1, keepdims=True)
                p = jnp.exp(s - m)
                l = jnp.sum(p, axis=-1, keepdims=True)
                o = _dot(p.astype(BF16), vb) * (1.0 / l)
                os_[pl.ds(q0, ATTN_QB), cs] = o.astype(BF16)
                lse_tile = jnp.where(lane == hh, m + jnp.log(l), lse_tile)
            ls[pl.ds(q0, ATTN_QB), :] = lse_tile
        return carry

    n_blk = n_sub // ATTN_QB
    lax.fori_loop(0, n_blk, block, 0, unroll=min(max(ATTN_UNROLL // nres, 1), n_blk))

    if nt > 1:
        for rr, (_, _, _, os_, ls) in enumerate(seqs):
            for t in range(nt):
                o_ref[t, rr] = os_[t * n:(t + 1) * n, :]
                lse_ref[t, rr] = ls[t * n:(t + 1) * n, :]


def _attention(qkv, group, batch, seq):
    dil = ATTN_PATTERNS[group][1]
    if dil == 1:
        nt, n = 1, seq
    else:
        nt, n = seq // PERM_TILE, PERM_TILE // dil
    n_sub = nt * n
    t = batch * seq
    qkv5 = qkv.reshape(batch, nt, dil, n, qkv.shape[1])
    cb = group * 3
    nres = min(dil, max(1, ATTN_MIN_BLOCKS * ATTN_QB // n_sub))
    blk = (None, nt, nres, n, GROUP_WIDTH)
    scratch = [pltpu.VMEM((HEADS_PER_GROUP * ATTN_OFFSETS, ATTN_QB, ATTN_KB), F32)]
    if nt > 1:
        scratch += [pltpu.VMEM((nres, n_sub, GROUP_WIDTH), BF16)] * 4 + [pltpu.VMEM((nres, n_sub, LANES), F32)]
    out, lse = pl.pallas_call(
        functools.partial(_attn_kernel, group=group, dil=dil, nt=nt, n=n, nres=nres),
        grid=(batch, dil // nres),
        in_specs=[pl.BlockSpec(blk, lambda b, r: (b, 0, r, 0, cb)),
                  pl.BlockSpec(blk, lambda b, r: (b, 0, r, 0, cb + 1)),
                  pl.BlockSpec(blk, lambda b, r: (b, 0, r, 0, cb + 2))],
        out_specs=[pl.BlockSpec(blk, lambda b, r: (b, 0, r, 0, 0)),
                   pl.BlockSpec((None, nt, nres, n, LANES), lambda b, r: (b, 0, r, 0, 0))],
        out_shape=[jax.ShapeDtypeStruct((batch, nt, dil, n, GROUP_WIDTH), BF16),
                   jax.ShapeDtypeStruct((batch, nt, dil, n, LANES), F32)],
        scratch_shapes=scratch,
        compiler_params=_cparams(("arbitrary", "arbitrary")),
        name=f"attn_g{group}",
    )(qkv5, qkv5, qkv5)
    return out.reshape(t, GROUP_WIDTH), lse.reshape(t, LANES)


def _log_sigmoid(x):
    return jnp.minimum(x, 0.0) - jnp.log(1.0 + jnp.exp(-jnp.abs(x)))


def _retention_kernel(decay_ref, qkv_ref, g_ref, gain_ref, o_ref, state, ybwd, dmat, kdec, qdec, *, nc):
    c = RET_CHUNK
    h = pl.program_id(1)
    i = pl.program_id(2)
    kscale = RET_QK ** -0.5

    def set_decays(lg, forward):
        row = lax.broadcasted_iota(I32, (c, c), 0)
        col = lax.broadcasted_iota(I32, (c, c), 1)
        pos = lax.broadcasted_iota(I32, (c, LANES), 0).astype(F32)
        if forward:
            gap, key_pow, query_pow = row - col, (c - 1.0) - pos, pos + 1.0
            keep = gap >= 0
        else:
            gap, key_pow, query_pow = col - row, pos, c - pos
            keep = gap > 0
        dmat[...] = jnp.where(keep, jnp.exp(lg * jnp.maximum(gap, 0).astype(F32)) * kscale, 0.0)
        kdec[...] = jnp.exp(lg * key_pow) * kscale
        qdec[...] = jnp.exp(lg * query_pow)

    def chunk_step(lg, r0, finish):
        q = qkv_ref[r0:r0 + c, 0:RET_QK]
        k = qkv_ref[r0:r0 + c, RET_QK:2 * RET_QK]
        v = qkv_ref[r0:r0 + c, 2 * RET_QK:]
        inner = lax.dot_general(q, k, (((1,), (1,)), ((), ())), preferred_element_type=F32) * dmat[...]
        y = _dot(inner.astype(BF16), v)
        kd = (k.astype(F32) * jnp.concatenate([kdec[...]] * (RET_QK // LANES), axis=1)).astype(BF16)
        kv = lax.dot_general(kd, v, (((0,), (0,)), ((), ())), preferred_element_type=F32)
        yield
        qd = jnp.concatenate([qdec[...]] * (RET_V // LANES), axis=1)
        y = y + _dot(q, state[...].astype(BF16)) * qd
        state[...] = state[...] * jnp.exp(lg * float(c)) + kv
        yield
        finish(r0, y)

    @pl.when((i == 0) | (i == nc))
    def _():
        state[...] = jnp.zeros_like(state)

    @pl.when(i < nc)
    def _():
        lg = _log_sigmoid(jnp.zeros((1, 1), F32) + decay_ref[RET_HEADS + h])

        @pl.when(i == 0)
        def _():
            set_decays(lg, False)

        base = pl.multiple_of((nc - 1 - i) * RET_STEP, RET_STEP)

        def finish(r0, y):
            ybwd[pl.ds(base + r0, c), :] = y

        _interleave(chunk_step(lg, r0, finish) for r0 in reversed(range(0, RET_STEP, c)))

    @pl.when(i >= nc)
    def _():
        lg = _log_sigmoid(jnp.zeros((1, 1), F32) + decay_ref[h])

        @pl.when(i == nc)
        def _():
            set_decays(lg, True)

        base = pl.multiple_of((i - nc) * RET_STEP, RET_STEP)

        def finish(r0, y):
            y = y + ybwd[pl.ds(base + r0, c), :]
            g = g_ref[r0:r0 + c, :].astype(F32)
            o_ref[r0:r0 + c, :] = (g * _sigmoid(g) * (_ln(y) * gain_ref[...])).astype(BF16)

        _interleave(chunk_step(lg, r0, finish) for r0 in range(0, RET_STEP, c))


def _retention(rest, decays, gn_gain, batch, seq):
    c = RET_STEP
    nc = seq // c
    rest3 = rest.reshape(batch, seq, rest.shape[1])
    gate_blk = RET_HEADS * RET_HEAD_COLS // RET_V

    def chunk(i):
        return jnp.where(i < nc, nc - 1 - i, i - nc)

    grid_spec = pltpu.PrefetchScalarGridSpec(
        num_scalar_prefetch=1,
        grid=(batch, RET_HEADS, 2 * nc),
        in_specs=[pl.BlockSpec((None, c, RET_HEAD_COLS), lambda b, h, i, d: (b, chunk(i), h)),
                  pl.BlockSpec((None, c, RET_V), lambda b, h, i, d: (b, jnp.maximum(i - nc, 0), gate_blk + h)),
                  pl.BlockSpec((1, RET_V), lambda b, h, i, d: (0, h))],
        out_specs=pl.BlockSpec((None, c, RET_V), lambda b, h, i, d: (b, jnp.maximum(i - nc, 0), h)),
        scratch_shapes=[pltpu.VMEM((RET_QK, RET_V), F32), pltpu.VMEM((seq, RET_V), F32),
                        pltpu.VMEM((RET_CHUNK, RET_CHUNK), F32), pltpu.VMEM((RET_CHUNK, LANES), F32),
                        pltpu.VMEM((RET_CHUNK, LANES), F32)],
    )
    out = pl.pallas_call(
        functools.partial(_retention_kernel, nc=nc),
        grid_spec=grid_spec,
        out_shape=jax.ShapeDtypeStruct((batch, seq, RET_HEADS * RET_V), BF16),
        compiler_params=_cparams(("arbitrary", "arbitrary", "arbitrary")),
        name="retention",
    )(decays, rest3, rest3, gn_gain)
    return out.reshape(batch * seq, RET_HEADS * RET_V)


MERGE_TM = PERM_TILE
MERGE_SUB = 256
ZERO_SPLIT = 4


def _merge_kernel(o0_ref, o1_ref, o2_ref, l0_ref, l1_ref, l2_ref, retg_ref, ga_ref, gr_ref, x_ref, mod_ref,
                  wa_ref, wr_ref, wo_ref, ln_ref, wrh_ref, wrl_ref, rb_ref,
                  h1_ref, u2_ref, lg_ref, zrows_ref, on_scr, ln_scr, zbuf, zsem):
    i = pl.program_id(0)

    @pl.when(i == 0)
    def _():
        zbuf[...] = jnp.zeros_like(zbuf)

    zrows = zbuf.shape[0]
    zero_copies = [pltpu.make_async_copy(zbuf, zrows_ref.at[pl.ds((i * ZERO_SPLIT + k) * zrows, zrows)], zsem)
                   for k in range(ZERO_SPLIT)]
    for cp in zero_copies:
        cp.start()

    for g in (1, 2):
        dil = ATTN_PATTERNS[g][1]
        n = MERGE_TM // dil
        o_ref, l_ref = ((o1_ref, l1_ref), (o2_ref, l2_ref))[g - 1]
        for res in range(dil):
            rows = o_ref[res * n:(res + 1) * n, :].astype(F32)
            for hh in range(HEADS_PER_GROUP):
                on_scr[g - 1, hh, pl.ds(res, n, stride=dil), :] = rows[:, hh * HEAD_DIM:(hh + 1) * HEAD_DIM]
            ln_scr[g - 1, pl.ds(res, n, stride=dil), :] = l_ref[res * n:(res + 1) * n, :]

    def sub_tile(r0):
        rs = slice(r0, r0 + MERGE_SUB)
        l0, l1, l2 = l0_ref[rs, :], ln_scr[0, rs, :], ln_scr[1, rs, :]
        lm = jnp.maximum(jnp.maximum(l0, l1), l2)
        e0, e1, e2 = jnp.exp(l0 - lm), jnp.exp(l1 - lm), jnp.exp(l2 - lm)
        inv = 1.0 / (e0 + e1 + e2)
        parts = []
        for hh in range(HEADS_PER_GROUP):
            sl = slice(hh * HEAD_DIM, (hh + 1) * HEAD_DIM)
            acc = (e0[:, hh:hh + 1] * o0_ref[rs, sl].astype(F32)
                   + e1[:, hh:hh + 1] * on_scr[0, hh, rs, :]
                   + e2[:, hh:hh + 1] * on_scr[1, hh, rs, :])
            parts.append((acc * inv[:, hh:hh + 1]).astype(BF16))
        attn = jnp.concatenate(parts, axis=1)
        yield

        branch_a = _dot(attn, wa_ref[...])
        branch_r = _dot(retg_ref[rs, :], wr_ref[...])
        yield
        merged = (_sigmoid(ga_ref[rs, :].astype(F32)) * branch_a
                  + _sigmoid(gr_ref[rs, :].astype(F32)) * branch_r)
        yield
        y = _dot(merged.astype(BF16), wo_ref[...])
        yield

        h1 = _ln(DEEPNORM_ALPHA * x_ref[rs, :] + mod_ref[2:3, :] * y) * ln_ref[0:1, :] + ln_ref[1:2, :]
        h1_ref[rs, :] = h1
        u2 = _ln(h1) * (1.0 + mod_ref[4:5, :]) + mod_ref[3:4, :]
        _store_token_tiles(u2_ref, u2, r0)
        uh, ul = _split_bf16(u2)
        yield
        lg_ref[rs, :] = (_dot(uh, wrh_ref[...]) + _dot(uh, wrl_ref[...]) + _dot(ul, wrh_ref[...])
                         + rb_ref[...])

    _interleave(sub_tile(r0) for r0 in range(0, MERGE_TM, MERGE_SUB))

    for cp in zero_copies:
        cp.wait()


def _merge(outs, lses, retg, rest, x2, mod3, wa, wr, wo, ln1, wr_hi, wr_lo, rbias, seq, n_rows):
    t = x2.shape[0]
    tm = MERGE_TM
    per_seq = seq // tm
    zrows, rem = divmod(n_rows * ROW_TILE, (t // tm) * ZERO_SPLIT)
    assert rem == 0 and zrows % SUBLANES == 0
    row = lambda w: pl.BlockSpec((tm, w), lambda i: (i, 0))
    full = lambda a: pl.BlockSpec(a.shape, lambda i: (0,) * a.ndim)
    return pl.pallas_call(
        _merge_kernel,
        grid=(t // tm,),
        in_specs=[row(GROUP_WIDTH)] * 3 + [row(LANES)] * 3 + [
            row(RET_HEADS * RET_V),
            pl.BlockSpec((tm, D_MODEL), lambda i: (i, 6)),
            pl.BlockSpec((tm, D_MODEL), lambda i: (i, 7)),
            row(D_MODEL),
            pl.BlockSpec((None, 6, D_MODEL), lambda i: (i // per_seq, 0, 0)),
            full(wa), full(wr), full(wo), full(ln1), full(wr_hi), full(wr_lo), full(rbias)],
        out_specs=[row(D_MODEL), pl.BlockSpec((tm * ROW_TILE, LANES), lambda i: (i, 0)), row(LANES),
                   pl.BlockSpec(memory_space=pl.ANY)],
        out_shape=[jax.ShapeDtypeStruct((t, D_MODEL), F32),
                   jax.ShapeDtypeStruct((t * ROW_TILE, LANES), F32),
                   jax.ShapeDtypeStruct((t, LANES), F32),
                   jax.ShapeDtypeStruct((n_rows * ROW_TILE, LANES), F32)],
        scratch_shapes=[pltpu.VMEM((2, HEADS_PER_GROUP, tm, HEAD_DIM), F32), pltpu.VMEM((2, tm, LANES), F32),
                        pltpu.VMEM((zrows, LANES), F32), pltpu.SemaphoreType.DMA(())],
        compiler_params=_cparams(("arbitrary",)),
        name="merge",
    )(*outs, *lses, retg, rest, rest, x2, mod3, wa, wr, wo, ln1, wr_hi, wr_lo, rbias)


ROUTE_TM = 512
BIG = 1 << 20


def _route_kernel(lg_ref, cols_ref, ints_ref, cnt_ref, carry, tri):
    i = pl.program_id(0)
    tm = ROUTE_TM

    @pl.when(i == 0)
    def _():
        carry[...] = jnp.zeros_like(carry)
        r_i = lax.broadcasted_iota(I32, (tm, tm), 0)
        c_i = lax.broadcasted_iota(I32, (tm, tm), 1)
        tri[...] = jnp.where(r_i > c_i, 1.0, 0.0).astype(BF16)

    lg = lg_ref[...]
    lane = lax.broadcasted_iota(I32, (tm, LANES), 1)
    lane_f = lane.astype(F32)
    first = lambda mask: jnp.min(jnp.where(mask, lane_f, float(BIG)), axis=-1, keepdims=True).astype(I32)

    coarse = jnp.where(lane < N_GROUPS, lg, NEG)
    cmax = jnp.max(coarse, axis=-1, keepdims=True)
    gsel = first(coarse == cmax)
    p_group = 1.0 / jnp.sum(jnp.exp(coarse - cmax), axis=-1, keepdims=True)

    lo = N_GROUPS + EXPERTS_PER_GROUP * gsel
    fine = jnp.where((lane >= lo) & (lane < lo + EXPERTS_PER_GROUP), lg, NEG)
    v1 = jnp.max(fine, axis=-1, keepdims=True)
    i1 = first(fine == v1)
    fine2 = jnp.where(lane == i1, NEG, fine)
    v2 = jnp.max(fine2, axis=-1, keepdims=True)
    i2 = first(fine2 == v2)
    ex = jnp.exp(v2 - v1)
    den = 1.0 / (1.0 + ex)
    gate1 = p_group * den
    gate2 = p_group * (ex * den)
    e1 = i1 - N_GROUPS
    e2 = i2 - N_GROUPS

    oh1 = lane == e1
    oh2 = lane == e2
    cnt = jnp.where(oh1 | oh2, 1.0, 0.0)
    rank = _dot(tri[...], cnt.astype(BF16)) + carry[...]
    r1 = jnp.sum(jnp.where(oh1, rank, 0.0), axis=-1, keepdims=True)
    r2 = jnp.sum(jnp.where(oh2, rank, 0.0), axis=-1, keepdims=True)
    carry[...] = carry[...] + jnp.sum(cnt, axis=0, keepdims=True)
    cnt_ref[...] = jnp.broadcast_to(carry[...], cnt_ref.shape)

    cols_ref[...] = jnp.where(lane == 0, gate1, jnp.where(lane == 1, gate2, 0.0))
    packed = jnp.where(lane == 0, e1.astype(F32),
                       jnp.where(lane == 1, e2.astype(F32),
                                 jnp.where(lane == 2, r1, jnp.where(lane == 3, r2, 0.0))))
    ints_ref[...] = packed.T[0:SUBLANES, :].astype(I32)


def _route(logits):
    t = logits.shape[0]
    tm = ROUTE_TM
    return pl.pallas_call(
        _route_kernel,
        grid=(t // tm,),
        in_specs=[pl.BlockSpec((tm, LANES), lambda i: (i, 0))],
        out_specs=[pl.BlockSpec((tm, LANES), lambda i: (i, 0)),
                   pl.BlockSpec((SUBLANES, tm), lambda i: (0, i)),
                   pl.BlockSpec((SUBLANES, LANES), lambda i: (0, 0))],
        out_shape=[jax.ShapeDtypeStruct((t, LANES), F32),
                   jax.ShapeDtypeStruct((SUBLANES, t), I32),
                   jax.ShapeDtypeStruct((SUBLANES, LANES), F32)],
        scratch_shapes=[pltpu.VMEM((1, LANES), F32), pltpu.VMEM((tm, tm), BF16)],
        compiler_params=_cparams(("arbitrary",)),
        name="route",
    )(logits)


def _plan_kernel(ints_ref, cnt_ref, dest_ref, meta_ref, *, n_blocks_pad):
    sub = lax.broadcasted_iota(I32, (LANES, LANES), 0)
    lane = lax.broadcasted_iota(I32, (LANES, LANES), 1)
    cnt = cnt_ref[0:1, :]
    nblk_row = jnp.floor((cnt + (MOE_BLK - 1.0)) * (1.0 / MOE_BLK))
    nblk_mat = jnp.broadcast_to(nblk_row, (LANES, LANES))
    start_col = jnp.sum(jnp.where(lane < sub, nblk_mat, 0.0), axis=-1, keepdims=True)
    nblk_col = jnp.sum(jnp.where(lane == sub, nblk_mat, 0.0), axis=-1, keepdims=True)
    end_col = start_col + nblk_col

    ints = ints_ref[...]
    base = jnp.zeros(ints.shape, F32)
    for e in range(N_EXPERTS):
        base = jnp.where(ints == e, start_col[e:e + 1, :] * float(MOE_BLK), base)
    dest = base[0:2, :].astype(I32) + ints[2:4, :]
    dest_ref[...] = jnp.concatenate([dest, jnp.zeros((SUBLANES - 2, ints.shape[1]), I32)], axis=0)

    blk = lax.broadcasted_iota(I32, (LANES, n_blocks_pad), 1).astype(F32)
    e_sub = lax.broadcasted_iota(I32, (LANES, n_blocks_pad), 0)
    done = jnp.where((e_sub < N_EXPERTS) & (end_col <= blk), 1.0, 0.0)
    bexp = jnp.minimum(jnp.sum(done, axis=0, keepdims=True), N_EXPERTS - 1.0)
    used = jnp.sum(nblk_row, axis=-1, keepdims=True)
    row = lax.broadcasted_iota(I32, (SUBLANES, n_blocks_pad), 0)
    meta = jnp.where(row == 0, bexp, jnp.where(row == 1, used, 0.0))
    meta_ref[...] = meta.astype(I32)


def _plan(ints, counts, n_blocks_pad):
    t = ints.shape[1]
    return pl.pallas_call(
        functools.partial(_plan_kernel, n_blocks_pad=n_blocks_pad),
        out_shape=[jax.ShapeDtypeStruct((SUBLANES, t), I32), jax.ShapeDtypeStruct((SUBLANES, n_blocks_pad), I32)],
        compiler_params=pltpu.CompilerParams(vmem_limit_bytes=VMEM_LIMIT),
        name="plan",
    )(ints, counts)


DISPATCH_TM = 2048
DMA_UNROLL = 16


def _row_tile(ref, r):
    return ref.at[pl.ds(pl.multiple_of(r * ROW_TILE, ROW_TILE), ROW_TILE)]


def _dispatch_kernel(d0_ref, d1_ref, u2_ref, rows_in_ref, rows_ref, sem):
    del rows_in_ref

    def issue(t, carry):
        src = _row_tile(u2_ref, t)
        pltpu.make_async_copy(src, _row_tile(rows_ref, d0_ref[t]), sem).start(priority=0)
        pltpu.make_async_copy(src, _row_tile(rows_ref, d1_ref[t]), sem).start(priority=1)
        return carry

    lax.fori_loop(0, DISPATCH_TM, issue, 0, unroll=DMA_UNROLL)
    for _ in range(2):
        pltpu.make_async_copy(u2_ref, rows_ref.at[pl.ds(0, DISPATCH_TM * ROW_TILE)], sem).wait()


def _dispatch(dest0, dest1, u2, rows0):
    t = u2.shape[0] // ROW_TILE
    idx = pl.BlockSpec((DISPATCH_TM,), lambda i: (i,), memory_space=pltpu.SMEM)
    return pl.pallas_call(
        _dispatch_kernel,
        grid=(t // DISPATCH_TM,),
        in_specs=[idx, idx,
                  pl.BlockSpec((DISPATCH_TM * ROW_TILE, LANES), lambda i: (i, 0)),
                  pl.BlockSpec(memory_space=pl.ANY)],
        out_specs=pl.BlockSpec(memory_space=pl.ANY),
        out_shape=jax.ShapeDtypeStruct(rows0.shape, F32),
        scratch_shapes=[pltpu.SemaphoreType.DMA(())],
        input_output_aliases={3: 0},
        compiler_params=_cparams(("arbitrary",)),
        name="dispatch",
    )(dest0, dest1, u2, rows0)


def _expert_runs(bexp, used):
    n = bexp.shape[0]
    idx = jnp.arange(n, dtype=I32)
    first = (idx < used[0]) & ((idx == 0) | (bexp != jnp.roll(bexp, 1)))
    slot = (jnp.cumsum(first.astype(I32)) - 1) % 2
    first_at_or_after = lax.cummin(jnp.where(first, idx, n)[::-1])[::-1]
    first_after = jnp.concatenate([first_at_or_after[1:], jnp.full((1,), n, I32)])
    nxt = jnp.where(first_after < n, bexp[jnp.minimum(first_after, n - 1)], -1)
    return first.astype(I32), slot.astype(I32), nxt.astype(I32)


def _experts_kernel(bexp_ref, used_ref, first_ref, slot_ref, nxt_ref, x_ref, w1_hbm, w3_hbm, w2_hbm, y_ref,
                    wb1, wb3, wb2, w1s, w3s, w2s, sem):
    i = pl.program_id(0)
    active = i < used_ref[0]
    slot = slot_ref[i]

    def fetch(e, s):
        return [pltpu.make_async_copy(w.at[e], buf.at[s], sem.at[s])
                for w, buf in ((w1_hbm, wb1), (w3_hbm, wb3), (w2_hbm, wb2))]

    @pl.when(i == 0)
    def _():
        for cp in fetch(bexp_ref[0], 0):
            cp.start()

    @pl.when(first_ref[i] == 1)
    def _():
        @pl.when(nxt_ref[i] >= 0)
        def _():
            for cp in fetch(nxt_ref[i], 1 - slot):
                cp.start()

        for cp in fetch(bexp_ref[i], slot):
            cp.wait()
        w1s[...] = wb1[slot].astype(BF16)
        w3s[...] = wb3[slot].astype(BF16)
        w2s[...] = wb2[slot].astype(BF16)

    @pl.when(active)
    def _():
        def sub_block(r0):
            xb = _load_token_tiles(x_ref, MOE_SUB, r0).astype(BF16)
            yield
            a = _dot(xb, w1s[...])
            b = _dot(xb, w3s[...])
            yield
            hdn = (a * _sigmoid(a) * b).astype(BF16)
            yield
            _store_token_tiles(y_ref, _dot(hdn, w2s[...]), r0)

        _interleave(sub_block(r0) for r0 in range(0, MOE_BLK, MOE_SUB))


def _experts(bexp, used, rows, w1, w3, w2):
    n_blocks = rows.shape[0] // (MOE_BLK * ROW_TILE)
    first, slot, nxt = _expert_runs(bexp, used)
    any_space = pl.BlockSpec(memory_space=pl.ANY)
    row_blk = pl.BlockSpec((MOE_BLK * ROW_TILE, LANES), lambda i, be, nu, *_: (jnp.minimum(i, nu[0] - 1), 0))
    grid_spec = pltpu.PrefetchScalarGridSpec(
        num_scalar_prefetch=5,
        grid=(n_blocks,),
        in_specs=[row_blk, any_space, any_space, any_space],
        out_specs=row_blk,
        scratch_shapes=[pltpu.VMEM((2, D_MODEL, EXPERT_FF), F32), pltpu.VMEM((2, D_MODEL, EXPERT_FF), F32),
                        pltpu.VMEM((2, EXPERT_FF, D_MODEL), F32),
                        pltpu.VMEM((D_MODEL, EXPERT_FF), BF16), pltpu.VMEM((D_MODEL, EXPERT_FF), BF16),
                        pltpu.VMEM((EXPERT_FF, D_MODEL), BF16), pltpu.SemaphoreType.DMA((2,))],
    )
    return pl.pallas_call(
        _experts_kernel,
        grid_spec=grid_spec,
        out_shape=jax.ShapeDtypeStruct(rows.shape, F32),
        input_output_aliases={5: 0},
        compiler_params=_cparams(("arbitrary",)),
        name="experts",
    )(bexp, used, first, slot, nxt, rows, w1, w3, w2)


COMBINE_TM = 256


COMBINE_AHEAD = 3


def _combine_kernel(*refs):
    n_pairs = COMBINE_AHEAD + 1
    idx = [(refs[2 * k], refs[2 * k + 1]) for k in range(n_pairs)]
    y_ref, cols_ref, h1_ref, mod_ref, ln_ref, o_ref = refs[2 * n_pairs:2 * n_pairs + 6]
    scratch = refs[2 * n_pairs + 6:]
    pairs = [(scratch[2 * k], scratch[2 * k + 1]) for k in range(n_pairs)]
    sem = scratch[2 * n_pairs]
    n0_ref, n1_ref = idx[COMBINE_AHEAD]
    i = pl.program_id(0)
    n_steps = pl.num_programs(0)

    def copies(i0_ref, i1_ref, t, k):
        a, b = pairs[k]
        return (pltpu.make_async_copy(_row_tile(y_ref, i0_ref[t]), _row_tile(a, t), sem.at[k]),
                pltpu.make_async_copy(_row_tile(y_ref, i1_ref[t]), _row_tile(b, t), sem.at[k]))

    def start_tile(i0_ref, i1_ref, t, k):
        for prio, cp in enumerate(copies(i0_ref, i1_ref, t, k)):
            cp.start(priority=prio)

    @pl.when(i == 0)
    def _():
        for k, (i0_ref, i1_ref) in enumerate(idx[:COMBINE_AHEAD]):
            def issue(t, carry):
                start_tile(i0_ref, i1_ref, t, k)
                return carry

            lax.fori_loop(0, COMBINE_TM, issue, 0, unroll=DMA_UNROLL)

    def step(k, prefetch):
        a, b = pairs[k]
        for buf in (a, b):
            pltpu.make_async_copy(y_ref.at[pl.ds(0, COMBINE_TM * ROW_TILE)], buf, sem.at[k]).wait()
        if prefetch:
            for t in range(COMBINE_TM):
                start_tile(n0_ref, n1_ref, t, (k + COMBINE_AHEAD) % n_pairs)
        cols = cols_ref[...]
        moe = cols[:, 0:1] * _load_token_tiles(a, COMBINE_TM) + cols[:, 1:2] * _load_token_tiles(b, COMBINE_TM)
        pre = DEEPNORM_ALPHA * h1_ref[...] + mod_ref[5:6, :] * moe
        o_ref[...] = _ln(pre) * ln_ref[0:1, :] + ln_ref[1:2, :]

    for k in range(n_pairs):
        @pl.when((i % n_pairs == k) & (i + COMBINE_AHEAD < n_steps))
        def _():
            step(k, True)

        @pl.when((i % n_pairs == k) & (i + COMBINE_AHEAD >= n_steps))
        def _():
            step(k, False)


def _combine(dest0, dest1, y_rows, cols, h1, mod3, ln2, seq):
    t = h1.shape[0]
    tm = COMBINE_TM
    per_seq = seq // tm
    assert t // tm >= COMBINE_AHEAD
    ahead = lambda k: pl.BlockSpec((tm,), lambda i: (jnp.minimum(i + k, t // tm - 1),), memory_space=pltpu.SMEM)
    return pl.pallas_call(
        _combine_kernel,
        grid=(t // tm,),
        in_specs=[ahead(k) for k in range(COMBINE_AHEAD + 1) for _ in range(2)] + [
                  pl.BlockSpec(memory_space=pl.ANY),
                  pl.BlockSpec((tm, LANES), lambda i: (i, 0)),
                  pl.BlockSpec((tm, D_MODEL), lambda i: (i, 0)),
                  pl.BlockSpec((None, 6, D_MODEL), lambda i: (i // per_seq, 0, 0)),
                  pl.BlockSpec((2, D_MODEL), lambda i: (0, 0))],
        out_specs=pl.BlockSpec((tm, D_MODEL), lambda i: (i, 0)),
        out_shape=jax.ShapeDtypeStruct((t, D_MODEL), F32),
        scratch_shapes=([pltpu.VMEM((tm * ROW_TILE, LANES), F32)] * (2 * (COMBINE_AHEAD + 1))
                        + [pltpu.SemaphoreType.DMA((COMBINE_AHEAD + 1,))]),
        compiler_params=_cparams(("arbitrary",)),
        name="combine",
    )(*([dest0, dest1] * (COMBINE_AHEAD + 1)), y_rows, cols, h1, mod3, ln2)


def _layer(h, c8, w_ada, b_ada, w_in, w_attn_out, decay_f, decay_b, gn_gain, w_ret_out, w_out,
           ln1_gain, ln1_bias, w_coarse, b_coarse, w_fine, b_fine, w1, w3, w2, ln2_gain, ln2_bias):
    batch, seq, d = h.shape
    t = batch * seq
    x2 = h.reshape(t, d)

    mod = _ada(c8, w_ada, b_ada.reshape(1, -1))
    mod3 = mod[:batch].reshape(batch, 6, d)

    wb = w_in.astype(BF16)
    qkv, u1 = _inproj_attn(x2, mod3, wb, seq)
    rest = _inproj_rest(u1, wb)

    outs, lses = zip(*[_attention(qkv, g, batch, seq) for g in range(ATTN_GROUPS)])
    decays = jnp.concatenate([decay_f, decay_b]).astype(F32)
    retg = _retention(rest, decays, gn_gain.reshape(1, -1), batch, seq)

    w_route = jnp.concatenate([w_coarse, w_fine.transpose(1, 0, 2).reshape(d, N_EXPERTS)], axis=1)
    n_route = w_route.shape[1]
    w_route = jnp.pad(w_route, ((0, 0), (0, LANES - n_route)))
    wr_hi, wr_lo = _split_bf16(w_route)
    rbias = jnp.pad(jnp.concatenate([b_coarse, b_fine.reshape(-1)]), (0, LANES - n_route)).reshape(1, LANES)
    ln1 = jnp.stack([ln1_gain, ln1_bias])
    n_blocks = 2 * t // MOE_BLK + N_EXPERTS
    h1, u2, logits, rows0 = _merge(outs, lses, retg, rest, x2, mod3, w_attn_out.astype(BF16),
                                   w_ret_out.astype(BF16), w_out.astype(BF16), ln1, wr_hi, wr_lo, rbias, seq,
                                   n_blocks * MOE_BLK)

    cols, ints, counts = _route(logits)
    n_blocks_pad = -(-n_blocks // LANES) * LANES
    dest, meta = _plan(ints, counts, n_blocks_pad)
    dest0, dest1 = dest[0], dest[1]
    rows = _dispatch(dest0, dest1, u2, rows0)
    y_rows = _experts(meta[0, :n_blocks], meta[1, :1], rows, w1, w3, w2)
    out = _combine(dest0, dest1, y_rows, cols, h1, mod3, jnp.stack([ln2_gain, ln2_bias]), seq)
    return out.reshape(batch, seq, d)


def kernel(x, c, w_ada, b_ada, w_in, w_attn_out, ret_decay_fwd, ret_decay_bwd, ret_gn_gain, w_ret_out, w_out,
           ln1_gain, ln1_bias, w_coarse, b_coarse, w_fine, b_fine, w1, w3, w2, ln2_gain, ln2_bias):
    batch = x.shape[0]
    max_dil = max(dil for _, dil in ATTN_PATTERNS)
    seq_tile = max(max_dil * ATTN_KB, RET_STEP, REST_TM, DISPATCH_TM)
    assert batch <= SUBLANES and x.shape[1] % seq_tile == 0 and x.shape[2] == D_MODEL
    c8 = jnp.pad(c, ((0, SUBLANES - batch), (0, 0)))
    h = x
    for l in range(w_ada.shape[0]):
        h = _layer(h, c8, w_ada[l], b_ada[l], w_in[l], w_attn_out[l], ret_decay_fwd[l], ret_decay_bwd[l],
                   ret_gn_gain[l], w_ret_out[l], w_out[l], ln1_gain[l], ln1_bias[l], w_coarse[l], b_coarse[l],
                   w_fine[l], b_fine[l], w1[l], w3[l], w2[l], ln2_gain[l], ln2_bias[l])
    return h
```

```python
import functools

import jax
import jax.numpy as jnp
from jax import lax
from jax.experimental import pallas as pl
from jax.experimental.pallas import tpu as pltpu

F32 = jnp.float32
BF16 = jnp.bfloat16
I32 = jnp.int32

D_MODEL = 1024
ATTN_GROUPS = 3
HEADS_PER_GROUP = 4
HEAD_DIM = 128
ATTN_HEADS = ATTN_GROUPS * HEADS_PER_GROUP
GROUP_WIDTH = HEADS_PER_GROUP * HEAD_DIM
ATTN_PATTERNS = ((128, 1), (512, 4), (2048, 16))
ALIBI_MAX_EXP = 8.0
NEG = -1e30
RET_HEADS = 4
RET_QK = 256
RET_V = 512
N_GROUPS = 4
EXPERTS_PER_GROUP = 8
N_EXPERTS = N_GROUPS * EXPERTS_PER_GROUP
EXPERT_FF = 512
DEPTH = 1
DEEPNORM_ALPHA = (2.0 * DEPTH) ** 0.25
LN_EPS = 1e-5

LANES = 128
SUBLANES = 8
PERM_TILE = 512
ATTN_QB = 128
ATTN_HALF = 64
ATTN_KB = ATTN_QB + 2 * ATTN_HALF
RET_CHUNK = 256
RET_STEP = 2048
MOE_BLK = 512
MOE_SUB = 256
VMEM_LIMIT = 56 * 1024 * 1024


def _cparams(sem):
    return pltpu.CompilerParams(dimension_semantics=sem, vmem_limit_bytes=VMEM_LIMIT)


def _split_bf16(a):
    hi = a.astype(BF16)
    lo = (a - hi.astype(F32)).astype(BF16)
    return hi, lo


def _dot(a, b):
    return jnp.dot(a, b, preferred_element_type=F32)


def _dot3(a, b):
    ah, al = _split_bf16(a)
    bh, bl = _split_bf16(b)
    return _dot(ah, bh) + _dot(ah, bl) + _dot(al, bh)


def _ln(x):
    mu = jnp.mean(x, axis=-1, keepdims=True)
    xc = x - mu
    var = jnp.mean(xc * xc, axis=-1, keepdims=True)
    return xc * lax.rsqrt(var + LN_EPS)


def _sigmoid(x):
    return 1.0 / (1.0 + jnp.exp(-x))


ROW_TILE = D_MODEL // LANES


def _store_token_tiles(ref, val, row0=0):
    n = val.shape[0]
    for cc in range(ROW_TILE):
        ref[pl.ds(row0 * ROW_TILE + cc, n, stride=ROW_TILE), :] = val[:, cc * LANES:(cc + 1) * LANES]


def _load_token_tiles(ref, n, row0=0):
    return jnp.concatenate([ref[pl.ds(row0 * ROW_TILE + cc, n, stride=ROW_TILE), :] for cc in range(ROW_TILE)],
                           axis=1)


def _interleave(phased):
    pending = []
    for gen in phased:
        pending.append(gen)
        pending = [g for g in pending if next(g, StopIteration) is not StopIteration]
    while pending:
        pending = [g for g in pending if next(g, StopIteration) is not StopIteration]


def _ada_kernel(c_ref, w_ref, b_ref, o_ref):
    o_ref[...] = _dot3(c_ref[...], w_ref[...]) + b_ref[...]


def _ada(c8, w_ada, b_ada):
    n = w_ada.shape[1]
    return pl.pallas_call(
        _ada_kernel,
        grid=(n // D_MODEL,),
        in_specs=[pl.BlockSpec((SUBLANES, D_MODEL), lambda j: (0, 0)),
                  pl.BlockSpec((D_MODEL, D_MODEL), lambda j: (0, j)),
                  pl.BlockSpec((1, D_MODEL), lambda j: (0, j))],
        out_specs=pl.BlockSpec((SUBLANES, D_MODEL), lambda j: (0, j)),
        out_shape=jax.ShapeDtypeStruct((SUBLANES, n), F32),
        compiler_params=_cparams(("arbitrary",)),
        name="ada",
    )(c8, w_ada, b_ada)


INPROJ_TM = 1024
LN_ROWS = 128
LN_UNROLL = 4


def _modulated_ln_rows(x_ref, mod_ref, emit):
    gain = 1.0 + mod_ref[1:2, :]
    shift = mod_ref[0:1, :]

    def chunk(c, carry):
        r0 = pl.multiple_of(c * LN_ROWS, LN_ROWS)
        emit(r0, _ln(x_ref[pl.ds(r0, LN_ROWS), :]) * gain + shift)
        return carry

    lax.fori_loop(0, INPROJ_TM // LN_ROWS, chunk, 0, unroll=LN_UNROLL)


def _inproj_attn_kernel(x_ref, mod_ref, wq_ref, wk_ref, wv_ref, o_ref, un_ref, uf_scr, u_scr):
    j = pl.program_id(1)

    @pl.when(j == 0)
    def _():
        def emit(r0, u):
            for cc in range(D_MODEL // LANES):
                uf_scr[cc, pl.ds(r0, LN_ROWS), :] = u[:, cc * LANES:(cc + 1) * LANES]
            ub = u.astype(BF16)
            u_scr[0, pl.ds(r0, LN_ROWS), :] = ub
            un_ref[pl.ds(r0, LN_ROWS), :] = ub

        _modulated_ln_rows(x_ref, mod_ref, emit)
        for g in (1, 2):
            dil = ATTN_PATTERNS[g][1]
            n = PERM_TILE // dil
            for t0 in range(0, INPROJ_TM, PERM_TILE):
                for res in range(dil):
                    for cc in range(D_MODEL // LANES):
                        rows = uf_scr[cc, pl.ds(t0 + res, n, stride=dil), :]
                        u_scr[g, t0 + res * n:t0 + (res + 1) * n, cc * LANES:(cc + 1) * LANES] = rows.astype(BF16)

    u = u_scr[j]
    for s, w_ref in enumerate((wq_ref, wk_ref, wv_ref)):
        o_ref[:, s * GROUP_WIDTH:(s + 1) * GROUP_WIDTH] = _dot(u, w_ref[...]).astype(BF16)


ATTN_WIDTH = ATTN_HEADS * HEAD_DIM
W_RQ0 = 3 * ATTN_WIDTH
W_RK0 = W_RQ0 + RET_HEADS * RET_QK
W_RV0 = W_RK0 + RET_HEADS * RET_QK
W_TAIL0 = W_RV0 + RET_HEADS * RET_V


def _inproj_attn(x2, mod3, wb, seq):
    t = x2.shape[0]
    tn = 3 * GROUP_WIDTH
    per_seq = seq // INPROJ_TM
    groups_per_range = ATTN_WIDTH // GROUP_WIDTH
    wspec = lambda s: pl.BlockSpec((D_MODEL, GROUP_WIDTH), lambda i, j: (0, s * groups_per_range + j))
    return pl.pallas_call(
        _inproj_attn_kernel,
        grid=(t // INPROJ_TM, ATTN_GROUPS),
        in_specs=[pl.BlockSpec((INPROJ_TM, D_MODEL), lambda i, j: (i, 0)),
                  pl.BlockSpec((None, 6, D_MODEL), lambda i, j: (i // per_seq, 0, 0)),
                  wspec(0), wspec(1), wspec(2)],
        out_specs=[pl.BlockSpec((INPROJ_TM, tn), lambda i, j: (i, j)),
                   pl.BlockSpec((INPROJ_TM, D_MODEL), lambda i, j: (i, 0))],
        out_shape=[jax.ShapeDtypeStruct((t, ATTN_GROUPS * tn), BF16),
                   jax.ShapeDtypeStruct((t, D_MODEL), BF16)],
        scratch_shapes=[pltpu.VMEM((D_MODEL // LANES, INPROJ_TM, LANES), F32),
                        pltpu.VMEM((ATTN_GROUPS, INPROJ_TM, D_MODEL), BF16)],
        compiler_params=_cparams(("arbitrary", "arbitrary")),
        name="inproj_attn",
    )(x2, mod3, wb, wb, wb)


REST_TN = RET_HEAD_COLS = 2 * RET_QK + RET_V
REST_TAIL_TILES = (RET_HEADS * RET_V + 2 * D_MODEL) // REST_TN
REST_TM = 2048


def _inproj_rest_kernel(u_ref, wq_ref, wk_ref, wv_ref, wt0_ref, wt1_ref, o_ref):
    j = pl.program_id(0)

    def project(parts):
        u = u_ref[...]
        c0 = 0
        for w_ref in parts:
            wd = w_ref.shape[1]
            o_ref[:, c0:c0 + wd] = _dot(u, w_ref[...]).astype(BF16)
            c0 += wd

    @pl.when(j < RET_HEADS)
    def _():
        project((wq_ref, wk_ref, wv_ref))

    @pl.when(j >= RET_HEADS)
    def _():
        project((wt0_ref, wt1_ref))


def _inproj_rest(u, wb):
    t = u.shape[0]
    half = REST_TN // 2
    assert W_TAIL0 % half == 0 and W_RV0 % RET_V == 0
    head = lambda j: jnp.minimum(j, RET_HEADS - 1)
    tail = lambda j: jnp.maximum(j - RET_HEADS, 0)
    return pl.pallas_call(
        _inproj_rest_kernel,
        grid=(RET_HEADS + REST_TAIL_TILES, t // REST_TM),
        in_specs=[pl.BlockSpec((REST_TM, D_MODEL), lambda j, i: (i, 0)),
                  pl.BlockSpec((D_MODEL, RET_QK), lambda j, i: (0, W_RQ0 // RET_QK + head(j))),
                  pl.BlockSpec((D_MODEL, RET_QK), lambda j, i: (0, W_RK0 // RET_QK + head(j))),
                  pl.BlockSpec((D_MODEL, RET_V), lambda j, i: (0, W_RV0 // RET_V + head(j))),
                  pl.BlockSpec((D_MODEL, half), lambda j, i: (0, W_TAIL0 // half + 2 * tail(j))),
                  pl.BlockSpec((D_MODEL, half), lambda j, i: (0, W_TAIL0 // half + 2 * tail(j) + 1))],
        out_specs=pl.BlockSpec((REST_TM, REST_TN), lambda j, i: (i, j)),
        out_shape=jax.ShapeDtypeStruct((t, (RET_HEADS + REST_TAIL_TILES) * REST_TN), BF16),
        compiler_params=_cparams(("arbitrary", "arbitrary")),
        name="inproj_rest",
    )(u, wb, wb, wb, wb, wb)


ATTN_OFFSETS = ATTN_KB // ATTN_HALF - 1
ATTN_UNROLL = 16
ATTN_MIN_BLOCKS = 16


def _attn_kernel(q_ref, k_ref, v_ref, o_ref, lse_ref, bias_scr, *scratch, group, dil, nt, n, nres):
    n_sub = nt * n
    if nt == 1:
        seqs = [(q_ref.at[0, rr], k_ref.at[0, rr], v_ref.at[0, rr], o_ref.at[0, rr], lse_ref.at[0, rr])
                for rr in range(nres)]
    else:
        seqs = [tuple(s.at[rr] for s in scratch) for rr in range(nres)]
        for rr, (qs, ks, vs, _, _) in enumerate(seqs):
            for t in range(nt):
                qs[t * n:(t + 1) * n, :] = q_ref[t, rr]
                ks[t * n:(t + 1) * n, :] = k_ref[t, rr]
                vs[t * n:(t + 1) * n, :] = v_ref[t, rr]

    @pl.when((pl.program_id(0) == 0) & (pl.program_id(1) == 0))
    def _():
        base = (lax.broadcasted_iota(I32, (ATTN_QB, ATTN_KB), 1)
                - lax.broadcasted_iota(I32, (ATTN_QB, ATTN_KB), 0))
        for j in range(ATTN_OFFSETS):
            dist = jnp.abs(base - j * ATTN_HALF)
            for hh in range(HEADS_PER_GROUP):
                slope = dil * 2.0 ** (-ALIBI_MAX_EXP * (group * HEADS_PER_GROUP + hh + 1) / ATTN_HEADS)
                bias_scr[hh * ATTN_OFFSETS + j] = jnp.where(dist <= ATTN_HALF, -slope * dist.astype(F32), NEG)

    scale = HEAD_DIM ** -0.5
    lane = lax.broadcasted_iota(I32, (ATTN_QB, LANES), 1)

    def block(blk, carry):
        q0 = pl.multiple_of(blk * ATTN_QB, ATTN_QB)
        start = pl.multiple_of(jnp.clip(q0 - ATTN_HALF, 0, n_sub - ATTN_KB), ATTN_HALF)
        j = (q0 - start) // ATTN_HALF
        for qs, ks, vs, os_, ls in seqs:
            lse_tile = jnp.zeros((ATTN_QB, LANES), F32)
            for hh in range(HEADS_PER_GROUP):
                cs = slice(hh * HEAD_DIM, (hh + 1) * HEAD_DIM)
                qb = qs[pl.ds(q0, ATTN_QB), cs]
                kb = ks[pl.ds(start, ATTN_KB), cs]
                vb = vs[pl.ds(start, ATTN_KB), cs]
                s = lax.dot_general(qb, kb, (((1,), (1,)), ((), ())), preferred_element_type=F32) * scale
                s = s + bias_scr[hh * ATTN_OFFSETS + j]
                m = jnp.max(s, axis=-1, keepdims=True)
                p = jnp.exp(s - m)
                l = jnp.sum(p, axis=-1, keepdims=True)
                o = _dot(p.astype(BF16), vb) * (1.0 / l)
                os_[pl.ds(q0, ATTN_QB), cs] = o.astype(BF16)
                lse_tile = jnp.where(lane == hh, m + jnp.log(l), lse_tile)
            ls[pl.ds(q0, ATTN_QB), :] = lse_tile
        return carry

    n_blk = n_sub // ATTN_QB
    lax.fori_loop(0, n_blk, block, 0, unroll=min(max(ATTN_UNROLL // nres, 1), n_blk))

    if nt > 1:
        for rr, (_, _, _, os_, ls) in enumerate(seqs):
            for t in range(nt):
                o_ref[t, rr] = os_[t * n:(t + 1) * n, :]
                lse_ref[t, rr] = ls[t * n:(t + 1) * n, :]


def _attention(qkv, group, batch, seq):
    dil = ATTN_PATTERNS[group][1]
    if dil == 1:
        nt, n = 1, seq
    else:
        nt, n = seq // PERM_TILE, PERM_TILE // dil
    n_sub = nt * n
    t = batch * seq
    qkv5 = qkv.reshape(batch, nt, dil, n, qkv.shape[1])
    cb = group * 3
    nres = min(dil, max(1, ATTN_MIN_BLOCKS * ATTN_QB // n_sub))
    blk = (None, nt, nres, n, GROUP_WIDTH)
    scratch = [pltpu.VMEM((HEADS_PER_GROUP * ATTN_OFFSETS, ATTN_QB, ATTN_KB), F32)]
    if nt > 1:
        scratch += [pltpu.VMEM((nres, n_sub, GROUP_WIDTH), BF16)] * 4 + [pltpu.VMEM((nres, n_sub, LANES), F32)]
    out, lse = pl.pallas_call(
        functools.partial(_attn_kernel, group=group, dil=dil, nt=nt, n=n, nres=nres),
        grid=(batch, dil // nres),
        in_specs=[pl.BlockSpec(blk, lambda b, r: (b, 0, r, 0, cb)),
                  pl.BlockSpec(blk, lambda b, r: (b, 0, r, 0, cb + 1)),
                  pl.BlockSpec(blk, lambda b, r: (b, 0, r, 0, cb + 2))],
        out_specs=[pl.BlockSpec(blk, lambda b, r: (b, 0, r, 0, 0)),
                   pl.BlockSpec((None, nt, nres, n, LANES), lambda b, r: (b, 0, r, 0, 0))],
        out_shape=[jax.ShapeDtypeStruct((batch, nt, dil, n, GROUP_WIDTH), BF16),
                   jax.ShapeDtypeStruct((batch, nt, dil, n, LANES), F32)],
        scratch_shapes=scratch,
        compiler_params=_cparams(("arbitrary", "arbitrary")),
        name=f"attn_g{group}",
    )(qkv5, qkv5, qkv5)
    return out.reshape(t, GROUP_WIDTH), lse.reshape(t, LANES)


def _log_sigmoid(x):
    return jnp.minimum(x, 0.0) - jnp.log(1.0 + jnp.exp(-jnp.abs(x)))


def _retention_kernel(decay_ref, qkv_ref, g_ref, gain_ref, o_ref, state, ybwd, dmat, kdec, qdec, *, nc):
    c = RET_CHUNK
    h = pl.program_id(1)
    i = pl.program_id(2)
    kscale = RET_QK ** -0.5

    def set_decays(lg, forward):
        row = lax.broadcasted_iota(I32, (c, c), 0)
        col = lax.broadcasted_iota(I32, (c, c), 1)
        pos = lax.broadcasted_iota(I32, (c, LANES), 0).astype(F32)
        if forward:
            gap, key_pow, query_pow = row - col, (c - 1.0) - pos, pos + 1.0
            keep = gap >= 0
        else:
            gap, key_pow, query_pow = col - row, pos, c - pos
            keep = gap > 0
        dmat[...] = jnp.where(keep, jnp.exp(lg * jnp.maximum(gap, 0).astype(F32)) * kscale, 0.0)
        kdec[...] = jnp.exp(lg * key_pow) * kscale
        qdec[...] = jnp.exp(lg * query_pow)

    def chunk_step(lg, r0, finish):
        q = qkv_ref[r0:r0 + c, 0:RET_QK]
        k = qkv_ref[r0:r0 + c, RET_QK:2 * RET_QK]
        v = qkv_ref[r0:r0 + c, 2 * RET_QK:]
        inner = lax.dot_general(q, k, (((1,), (1,)), ((), ())), preferred_element_type=F32) * dmat[...]
        y = _dot(inner.astype(BF16), v)
        kd = (k.astype(F32) * jnp.concatenate([kdec[...]] * (RET_QK // LANES), axis=1)).astype(BF16)
        kv = lax.dot_general(kd, v, (((0,), (0,)), ((), ())), preferred_element_type=F32)
        yield
        qd = jnp.concatenate([qdec[...]] * (RET_V // LANES), axis=1)
        y = y + _dot(q, state[...].astype(BF16)) * qd
        state[...] = state[...] * jnp.exp(lg * float(c)) + kv
        yield
        finish(r0, y)

    @pl.when((i == 0) | (i == nc))
    def _():
        state[...] = jnp.zeros_like(state)

    @pl.when(i < nc)
    def _():
        lg = _log_sigmoid(jnp.zeros((1, 1), F32) + decay_ref[RET_HEADS + h])

        @pl.when(i == 0)
        def _():
            set_decays(lg, False)

        base = pl.multiple_of((nc - 1 - i) * RET_STEP, RET_STEP)

        def finish(r0, y):
            ybwd[pl.ds(base + r0, c), :] = y

        _interleave(chunk_step(lg, r0, finish) for r0 in reversed(range(0, RET_STEP, c)))

    @pl.when(i >= nc)
    def _():
        lg = _log_sigmoid(jnp.zeros((1, 1), F32) + decay_ref[h])

        @pl.when(i == nc)
        def _():
            set_decays(lg, True)

        base = pl.multiple_of((i - nc) * RET_STEP, RET_STEP)

        def finish(r0, y):
            y = y + ybwd[pl.ds(base + r0, c), :]
            g = g_ref[r0:r0 + c, :].astype(F32)
            o_ref[r0:r0 + c, :] = (g * _sigmoid(g) * (_ln(y) * gain_ref[...])).astype(BF16)

        _interleave(chunk_step(lg, r0, finish) for r0 in range(0, RET_STEP, c))


def _retention(rest, decays, gn_gain, batch, seq):
    c = RET_STEP
    nc = seq // c
    rest3 = rest.reshape(batch, seq, rest.shape[1])
    gate_blk = RET_HEADS * RET_HEAD_COLS // RET_V

    def chunk(i):
        return jnp.where(i < nc, nc - 1 - i, i - nc)

    grid_spec = pltpu.PrefetchScalarGridSpec(
        num_scalar_prefetch=1,
        grid=(batch, RET_HEADS, 2 * nc),
        in_specs=[pl.BlockSpec((None, c, RET_HEAD_COLS), lambda b, h, i, d: (b, chunk(i), h)),
                  pl.BlockSpec((None, c, RET_V), lambda b, h, i, d: (b, jnp.maximum(i - nc, 0), gate_blk + h)),
                  pl.BlockSpec((1, RET_V), lambda b, h, i, d: (0, h))],
        out_specs=pl.BlockSpec((None, c, RET_V), lambda b, h, i, d: (b, jnp.maximum(i - nc, 0), h)),
        scratch_shapes=[pltpu.VMEM((RET_QK, RET_V), F32), pltpu.VMEM((seq, RET_V), F32),
                        pltpu.VMEM((RET_CHUNK, RET_CHUNK), F32), pltpu.VMEM((RET_CHUNK, LANES), F32),
                        pltpu.VMEM((RET_CHUNK, LANES), F32)],
    )
    out = pl.pallas_call(
        functools.partial(_retention_kernel, nc=nc),
        grid_spec=grid_spec,
        out_shape=jax.ShapeDtypeStruct((batch, seq, RET_HEADS * RET_V), BF16),
        compiler_params=_cparams(("arbitrary", "arbitrary", "arbitrary")),
        name="retention",
    )(decays, rest3, rest3, gn_gain)
    return out.reshape(batch * seq, RET_HEADS * RET_V)


MERGE_TM = PERM_TILE
MERGE_SUB = 256
ZERO_SPLIT = 4


def _merge_kernel(o0_ref, o1_ref, o2_ref, l0_ref, l1_ref, l2_ref, retg_ref, ga_ref, gr_ref, x_ref, mod_ref,
                  wa_ref, wr_ref, wo_ref, ln_ref, wrh_ref, wrl_ref, rb_ref,
                  h1_ref, u2_ref, lg_ref, zrows_ref, on_scr, ln_scr, zbuf, zsem):
    i = pl.program_id(0)

    @pl.when(i == 0)
    def _():
        zbuf[...] = jnp.zeros_like(zbuf)

    zrows = zbuf.shape[0]
    zero_copies = [pltpu.make_async_copy(zbuf, zrows_ref.at[pl.ds((i * ZERO_SPLIT + k) * zrows, zrows)], zsem)
                   for k in range(ZERO_SPLIT)]
    for cp in zero_copies:
        cp.start()

    for g in (1, 2):
        dil = ATTN_PATTERNS[g][1]
        n = MERGE_TM // dil
        o_ref, l_ref = ((o1_ref, l1_ref), (o2_ref, l2_ref))[g - 1]
        for res in range(dil):
            rows = o_ref[res * n:(res + 1) * n, :].astype(F32)
            for hh in range(HEADS_PER_GROUP):
                on_scr[g - 1, hh, pl.ds(res, n, stride=dil), :] = rows[:, hh * HEAD_DIM:(hh + 1) * HEAD_DIM]
            ln_scr[g - 1, pl.ds(res, n, stride=dil), :] = l_ref[res * n:(res + 1) * n, :]

    def sub_tile(r0):
        rs = slice(r0, r0 + MERGE_SUB)
        l0, l1, l2 = l0_ref[rs, :], ln_scr[0, rs, :], ln_scr[1, rs, :]
        lm = jnp.maximum(jnp.maximum(l0, l1), l2)
        e0, e1, e2 = jnp.exp(l0 - lm), jnp.exp(l1 - lm), jnp.exp(l2 - lm)
        inv = 1.0 / (e0 + e1 + e2)
        parts = []
        for hh in range(HEADS_PER_GROUP):
            sl = slice(hh * HEAD_DIM, (hh + 1) * HEAD_DIM)
            acc = (e0[:, hh:hh + 1] * o0_ref[rs, sl].astype(F32)
                   + e1[:, hh:hh + 1] * on_scr[0, hh, rs, :]
                   + e2[:, hh:hh + 1] * on_scr[1, hh, rs, :])
            parts.append((acc * inv[:, hh:hh + 1]).astype(BF16))
        attn = jnp.concatenate(parts, axis=1)
        yield

        branch_a = _dot(attn, wa_ref[...])
        branch_r = _dot(retg_ref[rs, :], wr_ref[...])
        yield
        merged = (_sigmoid(ga_ref[rs, :].astype(F32)) * branch_a
                  + _sigmoid(gr_ref[rs, :].astype(F32)) * branch_r)
        yield
        y = _dot(merged.astype(BF16), wo_ref[...])
        yield

        h1 = _ln(DEEPNORM_ALPHA * x_ref[rs, :] + mod_ref[2:3, :] * y) * ln_ref[0:1, :] + ln_ref[1:2, :]
        h1_ref[rs, :] = h1
        u2 = _ln(h1) * (1.0 + mod_ref[4:5, :]) + mod_ref[3:4, :]
        _store_token_tiles(u2_ref, u2, r0)
        uh, ul = _split_bf16(u2)
        yield
        lg_ref[rs, :] = (_dot(uh, wrh_ref[...]) + _dot(uh, wrl_ref[...]) + _dot(ul, wrh_ref[...])
                         + rb_ref[...])

    _interleave(sub_tile(r0) for r0 in range(0, MERGE_TM, MERGE_SUB))

    for cp in zero_copies:
        cp.wait()


def _merge(outs, lses, retg, rest, x2, mod3, wa, wr, wo, ln1, wr_hi, wr_lo, rbias, seq, n_rows):
    t = x2.shape[0]
    tm = MERGE_TM
    per_seq = seq // tm
    zrows, rem = divmod(n_rows * ROW_TILE, (t // tm) * ZERO_SPLIT)
    assert rem == 0 and zrows % SUBLANES == 0
    row = lambda w: pl.BlockSpec((tm, w), lambda i: (i, 0))
    full = lambda a: pl.BlockSpec(a.shape, lambda i: (0,) * a.ndim)
    return pl.pallas_call(
        _merge_kernel,
        grid=(t // tm,),
        in_specs=[row(GROUP_WIDTH)] * 3 + [row(LANES)] * 3 + [
            row(RET_HEADS * RET_V),
            pl.BlockSpec((tm, D_MODEL), lambda i: (i, 6)),
            pl.BlockSpec((tm, D_MODEL), lambda i: (i, 7)),
            row(D_MODEL),
            pl.BlockSpec((None, 6, D_MODEL), lambda i: (i // per_seq, 0, 0)),
            full(wa), full(wr), full(wo), full(ln1), full(wr_hi), full(wr_lo), full(rbias)],
        out_specs=[row(D_MODEL), pl.BlockSpec((tm * ROW_TILE, LANES), lambda i: (i, 0)), row(LANES),
                   pl.BlockSpec(memory_space=pl.ANY)],
        out_shape=[jax.ShapeDtypeStruct((t, D_MODEL), F32),
                   jax.ShapeDtypeStruct((t * ROW_TILE, LANES), F32),
                   jax.ShapeDtypeStruct((t, LANES), F32),
                   jax.ShapeDtypeStruct((n_rows * ROW_TILE, LANES), F32)],
        scratch_shapes=[pltpu.VMEM((2, HEADS_PER_GROUP, tm, HEAD_DIM), F32), pltpu.VMEM((2, tm, LANES), F32),
                        pltpu.VMEM((zrows, LANES), F32), pltpu.SemaphoreType.DMA(())],
        compiler_params=_cparams(("arbitrary",)),
        name="merge",
    )(*outs, *lses, retg, rest, rest, x2, mod3, wa, wr, wo, ln1, wr_hi, wr_lo, rbias)


ROUTE_TM = 512
BIG = 1 << 20


def _route_kernel(lg_ref, cols_ref, ints_ref, cnt_ref, carry, tri):
    i = pl.program_id(0)
    tm = ROUTE_TM

    @pl.when(i == 0)
    def _():
        carry[...] = jnp.zeros_like(carry)
        r_i = lax.broadcasted_iota(I32, (tm, tm), 0)
        c_i = lax.broadcasted_iota(I32, (tm, tm), 1)
        tri[...] = jnp.where(r_i > c_i, 1.0, 0.0).astype(BF16)

    lg = lg_ref[...]
    lane = lax.broadcasted_iota(I32, (tm, LANES), 1)
    lane_f = lane.astype(F32)
    first = lambda mask: jnp.min(jnp.where(mask, lane_f, float(BIG)), axis=-1, keepdims=True).astype(I32)

    coarse = jnp.where(lane < N_GROUPS, lg, NEG)
    cmax = jnp.max(coarse, axis=-1, keepdims=True)
    gsel = first(coarse == cmax)
    p_group = 1.0 / jnp.sum(jnp.exp(coarse - cmax), axis=-1, keepdims=True)

    lo = N_GROUPS + EXPERTS_PER_GROUP * gsel
    fine = jnp.where((lane >= lo) & (lane < lo + EXPERTS_PER_GROUP), lg, NEG)
    v1 = jnp.max(fine, axis=-1, keepdims=True)
    i1 = first(fine == v1)
    fine2 = jnp.where(lane == i1, NEG, fine)
    v2 = jnp.max(fine2, axis=-1, keepdims=True)
    i2 = first(fine2 == v2)
    ex = jnp.exp(v2 - v1)
    den = 1.0 / (1.0 + ex)
    gate1 = p_group * den
    gate2 = p_group * (ex * den)
    e1 = i1 - N_GROUPS
    e2 = i2 - N_GROUPS

    oh1 = lane == e1
    oh2 = lane == e2
    cnt = jnp.where(oh1 | oh2, 1.0, 0.0)
    rank = _dot(tri[...], cnt.astype(BF16)) + carry[...]
    r1 = jnp.sum(jnp.where(oh1, rank, 0.0), axis=-1, keepdims=True)
    r2 = jnp.sum(jnp.where(oh2, rank, 0.0), axis=-1, keepdims=True)
    carry[...] = carry[...] + jnp.sum(cnt, axis=0, keepdims=True)
    cnt_ref[...] = jnp.broadcast_to(carry[...], cnt_ref.shape)

    cols_ref[...] = jnp.where(lane == 0, gate1, jnp.where(lane == 1, gate2, 0.0))
    packed = jnp.where(lane == 0, e1.astype(F32),
                       jnp.where(lane == 1, e2.astype(F32),
                                 jnp.where(lane == 2, r1, jnp.where(lane == 3, r2, 0.0))))
    ints_ref[...] = packed.T[0:SUBLANES, :].astype(I32)


def _route(logits):
    t = logits.shape[0]
    tm = ROUTE_TM
    return pl.pallas_call(
        _route_kernel,
        grid=(t // tm,),
        in_specs=[pl.BlockSpec((tm, LANES), lambda i: (i, 0))],
        out_specs=[pl.BlockSpec((tm, LANES), lambda i: (i, 0)),
                   pl.BlockSpec((SUBLANES, tm), lambda i: (0, i)),
                   pl.BlockSpec((SUBLANES, LANES), lambda i: (0, 0))],
        out_shape=[jax.ShapeDtypeStruct((t, LANES), F32),
                   jax.ShapeDtypeStruct((SUBLANES, t), I32),
                   jax.ShapeDtypeStruct((SUBLANES, LANES), F32)],
        scratch_shapes=[pltpu.VMEM((1, LANES), F32), pltpu.VMEM((tm, tm), BF16)],
        compiler_params=_cparams(("arbitrary",)),
        name="route",
    )(logits)


def _plan_kernel(ints_ref, cnt_ref, dest_ref, meta_ref, *, n_blocks_pad):
    sub = lax.broadcasted_iota(I32, (LANES, LANES), 0)
    lane = lax.broadcasted_iota(I32, (LANES, LANES), 1)
    cnt = cnt_ref[0:1, :]
    nblk_row = jnp.floor((cnt + (MOE_BLK - 1.0)) * (1.0 / MOE_BLK))
    nblk_mat = jnp.broadcast_to(nblk_row, (LANES, LANES))
    start_col = jnp.sum(jnp.where(lane < sub, nblk_mat, 0.0), axis=-1, keepdims=True)
    nblk_col = jnp.sum(jnp.where(lane == sub, nblk_mat, 0.0), axis=-1, keepdims=True)
    end_col = start_col + nblk_col

    ints = ints_ref[...]
    base = jnp.zeros(ints.shape, F32)
    for e in range(N_EXPERTS):
        base = jnp.where(ints == e, start_col[e:e + 1, :] * float(MOE_BLK), base)
    dest = base[0:2, :].astype(I32) + ints[2:4, :]
    dest_ref[...] = jnp.concatenate([dest, jnp.zeros((SUBLANES - 2, ints.shape[1]), I32)], axis=0)

    blk = lax.broadcasted_iota(I32, (LANES, n_blocks_pad), 1).astype(F32)
    e_sub = lax.broadcasted_iota(I32, (LANES, n_blocks_pad), 0)
    done = jnp.where((e_sub < N_EXPERTS) & (end_col <= blk), 1.0, 0.0)
    bexp = jnp.minimum(jnp.sum(done, axis=0, keepdims=True), N_EXPERTS - 1.0)
    used = jnp.sum(nblk_row, axis=-1, keepdims=True)
    row = lax.broadcasted_iota(I32, (SUBLANES, n_blocks_pad), 0)
    meta = jnp.where(row == 0, bexp, jnp.where(row == 1, used, 0.0))
    meta_ref[...] = meta.astype(I32)


def _plan(ints, counts, n_blocks_pad):
    t = ints.shape[1]
    return pl.pallas_call(
        functools.partial(_plan_kernel, n_blocks_pad=n_blocks_pad),
        out_shape=[jax.ShapeDtypeStruct((SUBLANES, t), I32), jax.ShapeDtypeStruct((SUBLANES, n_blocks_pad), I32)],
        compiler_params=pltpu.CompilerParams(vmem_limit_bytes=VMEM_LIMIT),
        name="plan",
    )(ints, counts)


DISPATCH_TM = 2048
DMA_UNROLL = 16


def _row_tile(ref, r):
    return ref.at[pl.ds(pl.multiple_of(r * ROW_TILE, ROW_TILE), ROW_TILE)]


def _dispatch_kernel(d0_ref, d1_ref, u2_ref, rows_in_ref, rows_ref, sem):
    del rows_in_ref

    def issue(t, carry):
        src = _row_tile(u2_ref, t)
        pltpu.make_async_copy(src, _row_tile(rows_ref, d0_ref[t]), sem).start(priority=0)
        pltpu.make_async_copy(src, _row_tile(rows_ref, d1_ref[t]), sem).start(priority=1)
        return carry

    lax.fori_loop(0, DISPATCH_TM, issue, 0, unroll=DMA_UNROLL)
    for _ in range(2):
        pltpu.make_async_copy(u2_ref, rows_ref.at[pl.ds(0, DISPATCH_TM * ROW_TILE)], sem).wait()


def _dispatch(dest0, dest1, u2, rows0):
    t = u2.shape[0] // ROW_TILE
    idx = pl.BlockSpec((DISPATCH_TM,), lambda i: (i,), memory_space=pltpu.SMEM)
    return pl.pallas_call(
        _dispatch_kernel,
        grid=(t // DISPATCH_TM,),
        in_specs=[idx, idx,
                  pl.BlockSpec((DISPATCH_TM * ROW_TILE, LANES), lambda i: (i, 0)),
                  pl.BlockSpec(memory_space=pl.ANY)],
        out_specs=pl.BlockSpec(memory_space=pl.ANY),
        out_shape=jax.ShapeDtypeStruct(rows0.shape, F32),
        scratch_shapes=[pltpu.SemaphoreType.DMA(())],
        input_output_aliases={3: 0},
        compiler_params=_cparams(("arbitrary",)),
        name="dispatch",
    )(dest0, dest1, u2, rows0)


def _expert_runs(bexp, used):
    n = bexp.shape[0]
    idx = jnp.arange(n, dtype=I32)
    first = (idx < used[0]) & ((idx == 0) | (bexp != jnp.roll(bexp, 1)))
    slot = (jnp.cumsum(first.astype(I32)) - 1) % 2
    first_at_or_after = lax.cummin(jnp.where(first, idx, n)[::-1])[::-1]
    first_after = jnp.concatenate([first_at_or_after[1:], jnp.full((1,), n, I32)])
    nxt = jnp.where(first_after < n, bexp[jnp.minimum(first_after, n - 1)], -1)
    return first.astype(I32), slot.astype(I32), nxt.astype(I32)


def _experts_kernel(bexp_ref, used_ref, first_ref, slot_ref, nxt_ref, x_ref, w1_hbm, w3_hbm, w2_hbm, y_ref,
                    wb1, wb3, wb2, w1s, w3s, w2s, sem):
    i = pl.program_id(0)
    active = i < used_ref[0]
    slot = slot_ref[i]

    def fetch(e, s):
        return [pltpu.make_async_copy(w.at[e], buf.at[s], sem.at[s])
                for w, buf in ((w1_hbm, wb1), (w3_hbm, wb3), (w2_hbm, wb2))]

    @pl.when(i == 0)
    def _():
        for cp in fetch(bexp_ref[0], 0):
            cp.start()

    @pl.when(first_ref[i] == 1)
    def _():
        @pl.when(nxt_ref[i] >= 0)
        def _():
            for cp in fetch(nxt_ref[i], 1 - slot):
                cp.start()

        for cp in fetch(bexp_ref[i], slot):
            cp.wait()
        w1s[...] = wb1[slot].astype(BF16)
        w3s[...] = wb3[slot].astype(BF16)
        w2s[...] = wb2[slot].astype(BF16)

    @pl.when(active)
    def _():
        def sub_block(r0):
            xb = _load_token_tiles(x_ref, MOE_SUB, r0).astype(BF16)
            yield
            a = _dot(xb, w1s[...])
            b = _dot(xb, w3s[...])
            yield
            hdn = (a * _sigmoid(a) * b).astype(BF16)
            yield
            _store_token_tiles(y_ref, _dot(hdn, w2s[...]), r0)

        _interleave(sub_block(r0) for r0 in range(0, MOE_BLK, MOE_SUB))


def _experts(bexp, used, rows, w1, w3, w2):
    n_blocks = rows.shape[0] // (MOE_BLK * ROW_TILE)
    first, slot, nxt = _expert_runs(bexp, used)
    any_space = pl.BlockSpec(memory_space=pl.ANY)
    row_blk = pl.BlockSpec((MOE_BLK * ROW_TILE, LANES), lambda i, be, nu, *_: (jnp.minimum(i, nu[0] - 1), 0))
    grid_spec = pltpu.PrefetchScalarGridSpec(
        num_scalar_prefetch=5,
        grid=(n_blocks,),
        in_specs=[row_blk, any_space, any_space, any_space],
        out_specs=row_blk,
        scratch_shapes=[pltpu.VMEM((2, D_MODEL, EXPERT_FF), F32), pltpu.VMEM((2, D_MODEL, EXPERT_FF), F32),
                        pltpu.VMEM((2, EXPERT_FF, D_MODEL), F32),
                        pltpu.VMEM((D_MODEL, EXPERT_FF), BF16), pltpu.VMEM((D_MODEL, EXPERT_FF), BF16),
                        pltpu.VMEM((EXPERT_FF, D_MODEL), BF16), pltpu.SemaphoreType.DMA((2,))],
    )
    return pl.pallas_call(
        _experts_kernel,
        grid_spec=grid_spec,
        out_shape=jax.ShapeDtypeStruct(rows.shape, F32),
        input_output_aliases={5: 0},
        compiler_params=_cparams(("arbitrary",)),
        name="experts",
    )(bexp, used, first, slot, nxt, rows, w1, w3, w2)


COMBINE_TM = 256


COMBINE_AHEAD = 2


def _combine_kernel(d0_ref, d1_ref, p0_ref, p1_ref, n0_ref, n1_ref, y_ref, cols_ref, h1_ref, mod_ref, ln_ref,
                    o_ref, *scratch):
    n_pairs = COMBINE_AHEAD + 1
    pairs = [(scratch[2 * k], scratch[2 * k + 1]) for k in range(n_pairs)]
    sem = scratch[2 * n_pairs]
    i = pl.program_id(0)
    n_steps = pl.num_programs(0)

    def copies(i0_ref, i1_ref, t, k):
        a, b = pairs[k]
        return (pltpu.make_async_copy(_row_tile(y_ref, i0_ref[t]), _row_tile(a, t), sem.at[k]),
                pltpu.make_async_copy(_row_tile(y_ref, i1_ref[t]), _row_tile(b, t), sem.at[k]))

    def start_tile(i0_ref, i1_ref, t, k):
        for prio, cp in enumerate(copies(i0_ref, i1_ref, t, k)):
            cp.start(priority=prio)

    @pl.when(i == 0)
    def _():
        for k, (i0_ref, i1_ref) in enumerate(((d0_ref, d1_ref), (p0_ref, p1_ref))):
            def issue(t, carry):
                start_tile(i0_ref, i1_ref, t, k)
                return carry

            lax.fori_loop(0, COMBINE_TM, issue, 0, unroll=DMA_UNROLL)

    def step(k, prefetch):
        a, b = pairs[k]
        for buf in (a, b):
            pltpu.make_async_copy(y_ref.at[pl.ds(0, COMBINE_TM * ROW_TILE)], buf, sem.at[k]).wait()
        if prefetch:
            for t in range(COMBINE_TM):
                start_tile(n0_ref, n1_ref, t, (k + COMBINE_AHEAD) % n_pairs)
        cols = cols_ref[...]
        moe = cols[:, 0:1] * _load_token_tiles(a, COMBINE_TM) + cols[:, 1:2] * _load_token_tiles(b, COMBINE_TM)
        pre = DEEPNORM_ALPHA * h1_ref[...] + mod_ref[5:6, :] * moe
        o_ref[...] = _ln(pre) * ln_ref[0:1, :] + ln_ref[1:2, :]

    for k in range(n_pairs):
        @pl.when((i % n_pairs == k) & (i + COMBINE_AHEAD < n_steps))
        def _():
            step(k, True)

        @pl.when((i % n_pairs == k) & (i + COMBINE_AHEAD >= n_steps))
        def _():
            step(k, False)


def _combine(dest0, dest1, y_rows, cols, h1, mod3, ln2, seq):
    t = h1.shape[0]
    tm = COMBINE_TM
    per_seq = seq // tm
    assert COMBINE_AHEAD == 2 and t // tm >= COMBINE_AHEAD
    ahead = lambda k: pl.BlockSpec((tm,), lambda i: (jnp.minimum(i + k, t // tm - 1),), memory_space=pltpu.SMEM)
    return pl.pallas_call(
        _combine_kernel,
        grid=(t // tm,),
        in_specs=[ahead(0), ahead(0), ahead(1), ahead(1), ahead(COMBINE_AHEAD), ahead(COMBINE_AHEAD),
                  pl.BlockSpec(memory_space=pl.ANY),
                  pl.BlockSpec((tm, LANES), lambda i: (i, 0)),
                  pl.BlockSpec((tm, D_MODEL), lambda i: (i, 0)),
                  pl.BlockSpec((None, 6, D_MODEL), lambda i: (i // per_seq, 0, 0)),
                  pl.BlockSpec((2, D_MODEL), lambda i: (0, 0))],
        out_specs=pl.BlockSpec((tm, D_MODEL), lambda i: (i, 0)),
        out_shape=jax.ShapeDtypeStruct((t, D_MODEL), F32),
        scratch_shapes=([pltpu.VMEM((tm * ROW_TILE, LANES), F32)] * (2 * (COMBINE_AHEAD + 1))
                        + [pltpu.SemaphoreType.DMA((COMBINE_AHEAD + 1,))]),
        compiler_params=_cparams(("arbitrary",)),
        name="combine",
    )(dest0, dest1, dest0, dest1, dest0, dest1, y_rows, cols, h1, mod3, ln2)


def _layer(h, c8, w_ada, b_ada, w_in, w_attn_out, decay_f, decay_b, gn_gain, w_ret_out, w_out,
           ln1_gain, ln1_bias, w_coarse, b_coarse, w_fine, b_fine, w1, w3, w2, ln2_gain, ln2_bias):
    batch, seq, d = h.shape
    t = batch * seq
    x2 = h.reshape(t, d)

    mod = _ada(c8, w_ada, b_ada.reshape(1, -1))
    mod3 = mod[:batch].reshape(batch, 6, d)

    wb = w_in.astype(BF16)
    qkv, u1 = _inproj_attn(x2, mod3, wb, seq)
    rest = _inproj_rest(u1, wb)

    outs, lses = zip(*[_attention(qkv, g, batch, seq) for g in range(ATTN_GROUPS)])
    decays = jnp.concatenate([decay_f, decay_b]).astype(F32)
    retg = _retention(rest, decays, gn_gain.reshape(1, -1), batch, seq)

    w_route = jnp.concatenate([w_coarse, w_fine.transpose(1, 0, 2).reshape(d, N_EXPERTS)], axis=1)
    n_route = w_route.shape[1]
    w_route = jnp.pad(w_route, ((0, 0), (0, LANES - n_route)))
    wr_hi, wr_lo = _split_bf16(w_route)
    rbias = jnp.pad(jnp.concatenate([b_coarse, b_fine.reshape(-1)]), (0, LANES - n_route)).reshape(1, LANES)
    ln1 = jnp.stack([ln1_gain, ln1_bias])
    n_blocks = 2 * t // MOE_BLK + N_EXPERTS
    h1, u2, logits, rows0 = _merge(outs, lses, retg, rest, x2, mod3, w_attn_out.astype(BF16),
                                   w_ret_out.astype(BF16), w_out.astype(BF16), ln1, wr_hi, wr_lo, rbias, seq,
                                   n_blocks * MOE_BLK)

    cols, ints, counts = _route(logits)
    n_blocks_pad = -(-n_blocks // LANES) * LANES
    dest, meta = _plan(ints, counts, n_blocks_pad)
    dest0, dest1 = dest[0], dest[1]
    rows = _dispatch(dest0, dest1, u2, rows0)
    y_rows = _experts(meta[0, :n_blocks], meta[1, :1], rows, w1, w3, w2)
    out = _combine(dest0, dest1, y_rows, cols, h1, mod3, jnp.stack([ln2_gain, ln2_bias]), seq)
    return out.reshape(batch, seq, d)


def kernel(x, c, w_ada, b_ada, w_in, w_attn_out, ret_decay_fwd, ret_decay_bwd, ret_gn_gain, w_ret_out, w_out,
           ln1_gain, ln1_bias, w_coarse, b_coarse, w_fine, b_fine, w1, w3, w2, ln2_gain, ln2_bias):
    batch = x.shape[0]
    max_dil = max(dil for _, dil in ATTN_PATTERNS)
    seq_tile = max(max_dil * ATTN_KB, RET_STEP, REST_TM, DISPATCH_TM)
    assert batch <= SUBLANES and x.shape[1] % seq_tile == 0 and x.shape[2] == D_MODEL
    c8 = jnp.pad(c, ((0, SUBLANES - batch), (0, 0)))
    h = x
    for l in range(w_ada.shape[0]):
        h = _layer(h, c8, w_ada[l], b_ada[l], w_in[l], w_attn_out[l], ret_decay_fwd[l], ret_decay_bwd[l],
                   ret_gn_gain[l], w_ret_out[l], w_out[l], ln1_gain[l], ln1_bias[l], w_coarse[l], b_coarse[l],
                   w_fine[l], b_fine[l], w1[l], w3[l], w2[l], ln2_gain[l], ln2_bias[l])
    return h
```
